```python
import math
import jax
import jax.numpy as jnp
from jax import lax
import numpy as np

D_MODEL = 2048
BATCH = 2
SEQ = 8192
DEPTH = 1
DEC_BATCH = 16
DEC_SEQ = 32
PAST_LEN = 4096

CHUNK = 64
D_GDN = D_MODEL // 2
GDN_HEADS = 8
GDN_DK = D_GDN // GDN_HEADS
GDN_DV = D_GDN // GDN_HEADS
CONV_W = 4
D_QKV = 3 * D_GDN
D_SSM = D_MODEL - D_GDN
SSM_CG = 16
SSM_GROUPS = D_SSM // SSM_CG
SSM_P = 64
D_IN = D_QKV + D_GDN + 2 * GDN_HEADS + D_SSM
N_EXPERTS = 32
TOP_K = 4
D_FF = D_MODEL
SWIGLU_ALPHA = 1.702
SWIGLU_LIMIT = 7.0
MOE_BLOCK = 128
LN_EPS = 1e-5
RMS_EPS = 1e-6

kernel_name = 'hybrid_gdn_s5_moe_stream_step'


def _layernorm(x, g, b):
    xf = x.astype(jnp.float32)
    mu = jnp.mean(xf, -1, keepdims=True)
    var = jnp.mean(jnp.square(xf - mu), -1, keepdims=True)
    return ((xf - mu) * lax.rsqrt(var + LN_EPS) * g.astype(jnp.float32) + b.astype(jnp.float32)).astype(x.dtype)


def _l2norm(x):
    return x * lax.rsqrt(jnp.sum(x * x, -1, keepdims=True) + 1e-6)


def _causal_conv(x, hist, w):
    t = x.shape[1]
    xp = jnp.concatenate([hist.astype(x.dtype), x], axis=1)
    y = xp[:, 0:t] * w[0]
    for j in range(1, CONV_W):
        y = y + xp[:, j:j + t] * w[j]
    return jax.nn.silu(y), xp[:, t:]


def _gdn_chunks(q, k, v, g, beta, s0):
    L = q.shape[3]
    pos = jnp.arange(L)
    incl = pos[:, None] >= pos[None, :]
    strict = pos[:, None] > pos[None, :]
    gc = jnp.cumsum(g, axis=-1)
    decay = jnp.exp(jnp.where(incl, gc[..., :, None] - gc[..., None, :], -jnp.inf))
    kb = k * beta[..., None]
    tri = jnp.eye(L, dtype=jnp.float32) + jnp.where(strict, jnp.einsum('bnhid,bnhjd->bnhij', kb, k) * decay, 0.0)
    rhs = jnp.concatenate([v * beta[..., None], kb * jnp.exp(gc)[..., None]], axis=-1)
    sol = lax.linalg.triangular_solve(tri, rhs, left_side=True, lower=True, unit_diagonal=True)
    u, w = sol[..., :GDN_DV], sol[..., GDN_DV:]
    attn = jnp.einsum('bnhid,bnhjd->bnhij', q, k) * decay
    q_dec = q * jnp.exp(gc)[..., None]
    k_dec = k * jnp.exp(gc[..., -1:] - gc)[..., None]
    g_tot = jnp.exp(gc[..., -1])

    def step(s, xs):
        u_c, w_c, a_c, qd_c, kd_c, gt_c = xs
        v_new = u_c - jnp.einsum('bhld,bhde->bhle', w_c, s)
        o = jnp.einsum('bhld,bhde->bhle', qd_c, s) + jnp.einsum('bhij,bhje->bhie', a_c, v_new)
        s = s * gt_c[..., None, None] + jnp.einsum('bhld,bhle->bhde', kd_c, v_new)
        return s, o

    xs = (jnp.moveaxis(u, 1, 0), jnp.moveaxis(w, 1, 0), jnp.moveaxis(attn, 1, 0),
          jnp.moveaxis(q_dec, 1, 0), jnp.moveaxis(k_dec, 1, 0), jnp.moveaxis(g_tot, 1, 0))
    s_last, o = lax.scan(step, s0, xs)
    return jnp.moveaxis(o, 0, 1), s_last


def _s5_combine(e1, e2):
    a1, b1 = e1
    a2, b2 = e2
    return a1 * a2, a2 * b1 + b2


def _s5_blocks(u, h0, lam_bar, b_bar, c_c):
    def step(h, u_c):
        bu = jnp.einsum('gpc,blgc->blgp', b_bar, u_c.astype(jnp.complex64))
        bu = bu.at[:, 0].add(lam_bar * h)
        a = jnp.broadcast_to(lam_bar, bu.shape)
        _, hs = lax.associative_scan(_s5_combine, (a, bu), axis=1)
        y = jnp.real(jnp.einsum('gcp,blgp->blgc', c_c, hs))
        return hs[:, -1], y

    h_last, y = lax.scan(step, h0, jnp.moveaxis(u, 1, 0))
    return jnp.moveaxis(y, 0, 1), h_last


def _token_mixer(x, conv_hist, s_gdn, h_ssm, chunk_len, p):
    bsz, t, _ = x.shape
    n = t // chunk_len
    f32 = jnp.float32
    proj = x @ p['w_in']
    off = D_QKV + D_GDN
    qkv = proj[..., :D_QKV]
    z = proj[..., D_QKV:off]
    a = proj[..., off:off + GDN_HEADS]
    b = proj[..., off + GDN_HEADS:off + 2 * GDN_HEADS]
    u = proj[..., off + 2 * GDN_HEADS:]
    qkv, conv_new = _causal_conv(qkv, conv_hist, p['w_conv'])
    qkv = jnp.transpose(qkv.astype(f32).reshape(bsz, n, chunk_len, 3, GDN_HEADS, GDN_DK), (3, 0, 1, 4, 2, 5))
    q = _l2norm(qkv[0]) * (GDN_DK ** -0.5)
    k = _l2norm(qkv[1])
    v = qkv[2]
    g = -jnp.exp(p['a_log'].astype(f32)) * jax.nn.softplus(a.astype(f32) + p['dt_bias'].astype(f32))
    g = jnp.transpose(g.reshape(bsz, n, chunk_len, GDN_HEADS), (0, 1, 3, 2))
    beta = jnp.transpose(jax.nn.sigmoid(b.astype(f32)).reshape(bsz, n, chunk_len, GDN_HEADS), (0, 1, 3, 2))
    o, s_new = _gdn_chunks(q, k, v, g, beta, s_gdn.astype(f32))
    o = jnp.transpose(o, (0, 1, 3, 2, 4)).reshape(bsz, t, GDN_HEADS, GDN_DV)
    o = (o * lax.rsqrt(jnp.mean(o * o, -1, keepdims=True) + RMS_EPS) * p['w_onorm'].astype(f32)
         * jax.nn.silu(z.astype(f32).reshape(bsz, t, GDN_HEADS, GDN_DV)))
    o_gdn = o.reshape(bsz, t, D_GDN).astype(x.dtype)
    dt = jnp.exp(p['log_dt'].astype(f32))
    lam = lax.complex(jnp.minimum(p['lam_re'].astype(f32), -1e-4), p['lam_im'].astype(f32))
    lam_bar = jnp.exp(lam * dt[:, None])
    b_bar = ((lam_bar - 1.0) / lam)[..., None] * lax.complex(p['b_re'].astype(f32), p['b_im'].astype(f32))
    c_c = lax.complex(p['c_re'].astype(f32), p['c_im'].astype(f32))
    uf = u.astype(f32).reshape(bsz, n, chunk_len, SSM_GROUPS, SSM_CG)
    y, h_new = _s5_blocks(uf, h_ssm, lam_bar, b_bar, c_c)
    y = y.reshape(bsz, t, SSM_GROUPS, SSM_CG) + p['d_skip'].astype(f32) * uf.reshape(bsz, t, SSM_GROUPS, SSM_CG)
    y = jax.nn.gelu(y.reshape(bsz, t, D_SSM)).astype(x.dtype)
    o_ssm = y * jax.nn.sigmoid(y @ p['w_glu'] + p['b_glu'])
    mix = jnp.concatenate([o_gdn, o_ssm], axis=-1) @ p['w_out']
    return mix, conv_new, s_new, h_new


def _moe(x, p):
    m, d = x.shape
    logits = (x @ p['w_router'] + p['b_router']).astype(jnp.float32)
    top_val, top_idx = lax.top_k(logits, TOP_K)
    gates = jax.nn.softmax(top_val, axis=-1)
    n_assign = m * TOP_K
    e_flat = top_idx.reshape(n_assign).astype(jnp.int32)
    tok_flat = jnp.repeat(jnp.arange(m, dtype=jnp.int32), TOP_K)
    g_flat = gates.reshape(n_assign)
    order = jnp.argsort(e_flat)
    e_sorted = e_flat[order]
    tok_sorted = tok_flat[order]
    g_sorted = g_flat[order]
    counts = jnp.zeros((N_EXPERTS,), jnp.int32).at[e_flat].add(1)
    starts = jnp.cumsum(counts) - counts
    padded = (counts + MOE_BLOCK - 1) // MOE_BLOCK * MOE_BLOCK
    pends = jnp.cumsum(padded)
    pstarts = pends - padded
    dest = pstarts[e_sorted] + jnp.arange(n_assign, dtype=jnp.int32) - starts[e_sorted]
    n_blocks = -(-n_assign // MOE_BLOCK) + N_EXPERTS
    cap = n_blocks * MOE_BLOCK
    row_tok = jnp.full((cap,), m, jnp.int32).at[dest].set(tok_sorted)
    row_gate = jnp.zeros((cap,), jnp.float32).at[dest].set(g_sorted)
    block_start = jnp.arange(n_blocks, dtype=jnp.int32) * MOE_BLOCK
    block_exp = jnp.minimum(jnp.searchsorted(pends, block_start, side='right'), N_EXPERTS - 1)
    x_pad = jnp.concatenate([x, jnp.zeros((1, d), x.dtype)], axis=0)

    def run_block(args):
        e, toks, gw = args
        xb = x_pad[toks]
        hg = jnp.minimum(xb @ p['w_gate'][e] + p['b_gate'][e], SWIGLU_LIMIT)
        hu = jnp.clip(xb @ p['w_up'][e] + p['b_up'][e], -SWIGLU_LIMIT, SWIGLU_LIMIT)
        h = (hu + 1.0) * (hg * jax.nn.sigmoid(SWIGLU_ALPHA * hg))
        out = h @ p['w_down'][e] + p['b_down'][e]
        return out.astype(jnp.float32) * gw[:, None]

    outs = lax.map(run_block, (block_exp, row_tok.reshape(n_blocks, MOE_BLOCK), row_gate.reshape(n_blocks, MOE_BLOCK)))
    y = jnp.zeros((m + 1, d), jnp.float32).at[row_tok].add(outs.reshape(cap, d))
    return y[:m].astype(x.dtype)


def _layer(x, conv_hist, s_gdn, h_ssm, chunk_len, alpha, p):
    mix, conv_new, s_new, h_new = _token_mixer(x, conv_hist, s_gdn, h_ssm, chunk_len, p)
    x = _layernorm(alpha * x + mix, p['ln1_g'], p['ln1_b'])
    bsz, t, d = x.shape
    ff = _moe(x.reshape(bsz * t, d), p).reshape(bsz, t, d)
    x = _layernorm(alpha * x + ff, p['ln2_g'], p['ln2_b'])
    return x, conv_new, s_new, h_new


def setup_inputs(seed: int = 0) -> dict:
    key = jax.random.key(seed)
    ks = iter(jax.random.split(key, 48))
    f32 = jnp.float32
    beta_dn = (8.0 * DEPTH) ** -0.25

    def nrm(shape, scale):
        return jax.random.normal(next(ks), shape, f32) * scale

    def unif(shape, lo, hi):
        return jax.random.uniform(next(ks), shape, f32, lo, hi)

    dt0 = jnp.exp(unif((DEPTH, GDN_HEADS), math.log(1e-3), math.log(1e-1)))
    return {
        'x_prompt': nrm((BATCH, SEQ, D_MODEL), 1.0),
        'x_sample': nrm((DEC_BATCH, DEC_SEQ, D_MODEL), 1.0),
        'state_conv': nrm((DEPTH, DEC_BATCH, CONV_W - 1, D_QKV), 1.0),
        'state_gdn': nrm((DEPTH, DEC_BATCH, GDN_HEADS, GDN_DK, GDN_DV), 0.1),
        'state_ssm_re': nrm((DEPTH, DEC_BATCH, SSM_GROUPS, SSM_P), 0.1),
        'state_ssm_im': nrm((DEPTH, DEC_BATCH, SSM_GROUPS, SSM_P), 0.1),
        'w_in': nrm((DEPTH, D_MODEL, D_IN), D_MODEL ** -0.5),
        'w_conv': nrm((DEPTH, CONV_W, D_QKV), CONV_W ** -0.5),
        'a_log': jnp.log(unif((DEPTH, GDN_HEADS), 1.0, 16.0)),
        'dt_bias': dt0 + jnp.log(-jnp.expm1(-dt0)),
        'w_onorm': 1.0 + nrm((DEPTH, GDN_DV), 0.02),
        'lam_re': -0.5 + nrm((DEPTH, SSM_GROUPS, SSM_P), 0.01),
        'lam_im': math.pi * jnp.arange(SSM_P, dtype=f32) + nrm((DEPTH, SSM_GROUPS, SSM_P), 0.01),
        'log_dt': unif((DEPTH, SSM_GROUPS), math.log(1e-3), math.log(1e-1)),
        'b_re': nrm((DEPTH, SSM_GROUPS, SSM_P, SSM_CG), (2 * SSM_CG) ** -0.5),
        'b_im': nrm((DEPTH, SSM_GROUPS, SSM_P, SSM_CG), (2 * SSM_CG) ** -0.5),
        'c_re': nrm((DEPTH, SSM_GROUPS, SSM_CG, SSM_P), SSM_P ** -0.5),
        'c_im': nrm((DEPTH, SSM_GROUPS, SSM_CG, SSM_P), SSM_P ** -0.5),
        'd_skip': nrm((DEPTH, SSM_GROUPS, SSM_CG), 1.0),
        'w_glu': nrm((DEPTH, D_SSM, D_SSM), D_SSM ** -0.5),
        'b_glu': nrm((DEPTH, D_SSM), 0.01),
        'w_out': nrm((DEPTH, D_MODEL, D_MODEL), D_MODEL ** -0.5 * beta_dn),
        'ln1_g': 1.0 + nrm((DEPTH, D_MODEL), 0.02),
        'ln1_b': nrm((DEPTH, D_MODEL), 0.01),
        'w_router': nrm((DEPTH, D_MODEL, N_EXPERTS), D_MODEL ** -0.5),
        'b_router': nrm((DEPTH, N_EXPERTS), 0.01),
        'w_gate': nrm((DEPTH, N_EXPERTS, D_MODEL, D_FF), D_MODEL ** -0.5),
        'b_gate': nrm((DEPTH, N_EXPERTS, D_FF), 0.01),
        'w_up': nrm((DEPTH, N_EXPERTS, D_MODEL, D_FF), D_MODEL ** -0.5),
        'b_up': nrm((DEPTH, N_EXPERTS, D_FF), 0.01),
        'w_down': nrm((DEPTH, N_EXPERTS, D_FF, D_MODEL), D_FF ** -0.5 * beta_dn),
        'b_down': nrm((DEPTH, N_EXPERTS, D_MODEL), 0.01),
        'ln2_g': 1.0 + nrm((DEPTH, D_MODEL), 0.02),
        'ln2_b': nrm((DEPTH, D_MODEL), 0.01),
    }


def reference(x_prompt, x_sample, state_conv, state_gdn, state_ssm_re, state_ssm_im,
              w_in, w_conv, a_log, dt_bias, w_onorm, lam_re, lam_im, log_dt, b_re, b_im, c_re, c_im,
              d_skip, w_glu, b_glu, w_out, ln1_g, ln1_b, w_router, b_router, w_gate, b_gate,
              w_up, b_up, w_down, b_down, ln2_g, ln2_b):
    alpha = (2.0 * DEPTH) ** 0.25
    yp = x_prompt
    ys = x_sample
    bp = x_prompt.shape[0]
    conv_p, gdn_p, re_p, im_p = [], [], [], []
    conv_s, gdn_s, re_s, im_s = [], [], [], []
    for l in range(DEPTH):
        p = {'w_in': w_in[l], 'w_conv': w_conv[l], 'a_log': a_log[l], 'dt_bias': dt_bias[l],
             'w_onorm': w_onorm[l], 'lam_re': lam_re[l], 'lam_im': lam_im[l], 'log_dt': log_dt[l],
             'b_re': b_re[l], 'b_im': b_im[l], 'c_re': c_re[l], 'c_im': c_im[l], 'd_skip': d_skip[l],
             'w_glu': w_glu[l], 'b_glu': b_glu[l], 'w_out': w_out[l], 'ln1_g': ln1_g[l], 'ln1_b': ln1_b[l],
             'w_router': w_router[l], 'b_router': b_router[l], 'w_gate': w_gate[l], 'b_gate': b_gate[l],
             'w_up': w_up[l], 'b_up': b_up[l], 'w_down': w_down[l], 'b_down': b_down[l],
             'ln2_g': ln2_g[l], 'ln2_b': ln2_b[l]}
        yp, cp, sp, hp = _layer(
            yp, jnp.zeros((bp, CONV_W - 1, D_QKV), yp.dtype),
            jnp.zeros((bp, GDN_HEADS, GDN_DK, GDN_DV), jnp.float32),
            jnp.zeros((bp, SSM_GROUPS, SSM_P), jnp.complex64), CHUNK, alpha, p)
        h0 = lax.complex(state_ssm_re[l].astype(jnp.float32), state_ssm_im[l].astype(jnp.float32))
        ys, cs, ss, hs = _layer(ys, state_conv[l], state_gdn[l], h0, ys.shape[1], alpha, p)
        conv_p.append(cp); gdn_p.append(sp); re_p.append(jnp.real(hp)); im_p.append(jnp.imag(hp))
        conv_s.append(cs); gdn_s.append(ss); re_s.append(jnp.real(hs)); im_s.append(jnp.imag(hs))
    return (yp, ys, jnp.stack(conv_p), jnp.stack(gdn_p), jnp.stack(re_p), jnp.stack(im_p),
            jnp.stack(conv_s), jnp.stack(gdn_s), jnp.stack(re_s), jnp.stack(im_s))
```

```python
import functools
import math

import jax
import jax.numpy as jnp
from jax import lax
from jax.experimental import pallas as pl
from jax.experimental.pallas import tpu as pltpu

F32 = jnp.float32
BF16 = jnp.bfloat16

GDN_HEADS = 8
HEAD_DIM = 128
CONV_W = 4
SSM_CG = 16
SSM_P = 64
N_EXPERTS = 32
TOP_K = 4
SWIGLU_ALPHA = 1.702
SWIGLU_LIMIT = 7.0
LN_EPS = 1e-5
RMS_EPS = 1e-6
L2_EPS = 1e-6

LANES = 128
SUBLANES = 8
S5_SUB = 8
S5_TILE = 2048
GDN_TILE = 256
GROUPS_PER_BLOCK = LANES // SSM_CG
VMEM_LIMIT = 56 * 1024 * 1024

NT_DIMS = (((1,), (1,)), ((), ()))
TN_DIMS = (((0,), (0,)), ((), ()))


def _dot(a, b, dims=(((1,), (0,)), ((), ()))):
    return lax.dot_general(a, b, dims, preferred_element_type=F32)


def _split(a):
    hi = a.astype(BF16)
    lo = (a - hi.astype(F32)).astype(BF16)
    return hi, lo


def _dot_x3(a, b, dims=(((1,), (0,)), ((), ()))):
    ah, al = _split(a)
    bh, bl = _split(b)
    return _dot(ah, bh, dims) + (_dot(ah, bl, dims) + _dot(al, bh, dims))


def _dot_exact_lhs(a, b_bf, dims=(((1,), (0,)), ((), ()))):
    a0 = a.astype(BF16)
    r1 = a - a0.astype(F32)
    a1 = r1.astype(BF16)
    a2 = (r1 - a1.astype(F32)).astype(BF16)
    return _dot(a0, b_bf, dims) + (_dot(a1, b_bf, dims) + _dot(a2, b_bf, dims))


def _sigmoid(x):
    return 1.0 / (1.0 + jnp.exp(-x))


def _cparams(sem):
    return pltpu.CompilerParams(dimension_semantics=sem, vmem_limit_bytes=VMEM_LIMIT)


def _proj_body(x_ref, w_ref, o_ref, xb_ref):
    @pl.when(pl.program_id(1) == 0)
    def _():
        xb_ref[...] = x_ref[...].astype(BF16)

    o_ref[...] = _dot(xb_ref[...], w_ref[...])


def _in_proj(x2, w_bf, tm, tn):
    m, k = x2.shape
    n = w_bf.shape[1]
    return pl.pallas_call(
        _proj_body,
        grid=(m // tm, n // tn),
        in_specs=[pl.BlockSpec((tm, k), lambda i, j: (i, 0)),
                  pl.BlockSpec((k, tn), lambda i, j: (0, j))],
        out_specs=pl.BlockSpec((tm, tn), lambda i, j: (i, j)),
        out_shape=jax.ShapeDtypeStruct((m, n), F32),
        scratch_shapes=[pltpu.VMEM((tm, k), BF16)],
        compiler_params=_cparams(("parallel", "arbitrary")),
        name="in_proj",
    )(x2, w_bf)


def _gdn_prep_body(qkv_ref, ab_ref, hist_ref, wc_ref, alog_ref, dtb_ref,
                   q_ref, k_ref, v_ref, cols_ref, gct_ref, xbuf_ref, *, tt, chunk):
    d_gdn = GDN_HEADS * HEAD_DIM
    halo = SUBLANES

    @pl.when(pl.program_id(1) == 0)
    def _():
        xbuf_ref[0:halo, :] = jnp.zeros((halo, 3 * d_gdn), F32)
        xbuf_ref[halo - (CONV_W - 1):halo, :] = hist_ref[0]

    xbuf_ref[halo:halo + tt, :] = qkv_ref[0]

    for part, out_ref in enumerate((q_ref, k_ref, v_ref)):
        c0 = part * d_gdn
        y = None
        for j in range(CONV_W):
            r0 = halo - (CONV_W - 1) + j
            term = xbuf_ref[r0:r0 + tt, c0:c0 + d_gdn] * wc_ref[j:j + 1, c0:c0 + d_gdn]
            y = term if y is None else y + term
        s = y * _sigmoid(y)
        if part == 2:
            out_ref[0] = s
        else:
            for h in range(GDN_HEADS):
                sh = s[:, h * HEAD_DIM:(h + 1) * HEAD_DIM]
                nrm = sh * lax.rsqrt(jnp.sum(sh * sh, axis=-1, keepdims=True) + L2_EPS)
                if part == 0:
                    nrm = nrm * (HEAD_DIM ** -0.5)
                out_ref[0, :, h * HEAD_DIM:(h + 1) * HEAD_DIM] = nrm

    xbuf_ref[0:halo, :] = xbuf_ref[tt:tt + halo, :]

    ab = ab_ref[0]
    lane = lax.broadcasted_iota(jnp.int32, (tt, LANES), 1)
    is_a = lane < GDN_HEADS
    z = ab + dtb_ref[...]
    softplus = jnp.maximum(z, 0.0) + jnp.log1p(jnp.exp(-jnp.abs(z)))
    g = jnp.where(is_a, -jnp.exp(alog_ref[...]) * softplus, 0.0)
    beta = _sigmoid(ab)

    shift = int(math.log2(chunk))
    r = lax.broadcasted_iota(jnp.int32, (tt, tt), 0)
    c = lax.broadcasted_iota(jnp.int32, (tt, tt), 1)
    same = (r >> shift) == (c >> shift)
    m_incl = jnp.where(same & (r >= c), 1.0, 0.0).astype(BF16)
    m_all = jnp.where(same, 1.0, 0.0).astype(BF16)
    gc = _dot_exact_lhs_rhs(m_incl, g)
    glast = _dot_exact_lhs_rhs(m_all, g)
    eg = jnp.exp(gc)
    egl = jnp.exp(glast - gc)
    egt = jnp.exp(glast)
    zero = jnp.zeros_like(gc)
    cols = (jnp.where(is_a, gc, zero)
            + jnp.where((lane >= 8) & (lane < 16), beta, zero)
            + pltpu.roll(jnp.where(is_a, eg, zero), 16, 1)
            + pltpu.roll(jnp.where(is_a, egl, zero), 24, 1)
            + pltpu.roll(jnp.where(is_a, egt, zero), 32, 1))
    cols_ref[0] = cols

    er = lax.broadcasted_iota(jnp.int32, (SUBLANES, LANES), 0)
    ec = lax.broadcasted_iota(jnp.int32, (SUBLANES, LANES), 1)
    sel = jnp.where(er == ec, 1.0, 0.0).astype(BF16)
    gct_ref[0] = _dot_exact_rhs(sel, jnp.where(is_a, gc, zero), NT_DIMS)


def _three_pieces(a):
    a0 = a.astype(BF16)
    r1 = a - a0.astype(F32)
    a1 = r1.astype(BF16)
    a2 = (r1 - a1.astype(F32)).astype(BF16)
    return a0, a1, a2


def _dot_exact_lhs_rhs(mask_bf, a):
    a0, a1, a2 = _three_pieces(a)
    return _dot(mask_bf, a0) + (_dot(mask_bf, a1) + _dot(mask_bf, a2))


def _dot_exact_rhs(mask_bf, a, dims):
    a0, a1, a2 = _three_pieces(a)
    return _dot(mask_bf, a0, dims) + (_dot(mask_bf, a1, dims) + _dot(mask_bf, a2, dims))


def _gdn_prep(proj3, hist, w_conv, alog_p, dtb_p, chunk, ab_block):
    b, t, _ = proj3.shape
    d_gdn = GDN_HEADS * HEAD_DIM
    tt = min(t, GDN_TILE)
    body = functools.partial(_gdn_prep_body, tt=tt, chunk=chunk)
    big = jax.ShapeDtypeStruct((b, t, d_gdn), F32)
    return pl.pallas_call(
        body,
        grid=(b, t // tt),
        in_specs=[pl.BlockSpec((1, tt, 3 * d_gdn), lambda i, j: (i, j, 0)),
                  pl.BlockSpec((1, tt, LANES), lambda i, j: (i, j, ab_block)),
                  pl.BlockSpec((1, CONV_W - 1, 3 * d_gdn), lambda i, j: (i, 0, 0)),
                  pl.BlockSpec((CONV_W, 3 * d_gdn), lambda i, j: (0, 0)),
                  pl.BlockSpec((1, LANES), lambda i, j: (0, 0)),
                  pl.BlockSpec((1, LANES), lambda i, j: (0, 0))],
        out_specs=[pl.BlockSpec((1, tt, d_gdn), lambda i, j: (i, j, 0)),
                   pl.BlockSpec((1, tt, d_gdn), lambda i, j: (i, j, 0)),
                   pl.BlockSpec((1, tt, d_gdn), lambda i, j: (i, j, 0)),
                   pl.BlockSpec((1, tt, LANES), lambda i, j: (i, j, 0)),
                   pl.BlockSpec((1, SUBLANES, tt), lambda i, j: (i, 0, j))],
        out_shape=[big, big, big,
                   jax.ShapeDtypeStruct((b, t, LANES), F32),
                   jax.ShapeDtypeStruct((b, SUBLANES, t), F32)],
        scratch_shapes=[pltpu.VMEM((tt + SUBLANES, 3 * d_gdn), F32)],
        compiler_params=_cparams(("parallel", "arbitrary")),
        name="gdn_prep",
    )(proj3, proj3, hist, w_conv, alog_p, dtb_p)


def _unit_lower_inverse(a, r, c, chunk):
    base = 16
    eye = jnp.where(r == c, 1.0, 0.0)

    def blk(bs):
        s = int(math.log2(bs))
        return (r >> s) == (c >> s)

    d1 = jnp.where(blk(base), a, 0.0)
    d2 = _dot_x3(d1, d1)
    d4 = _dot_x3(d2, d2)
    d8 = _dot_x3(d4, d4)
    t = eye - d1
    t = t + _dot_x3(t, d2)
    t = t + _dot_x3(t, d4)
    t = t + _dot_x3(t, d8)
    bs = base
    while bs < chunk:
        off = jnp.where(blk(2 * bs) & jnp.logical_not(blk(bs)), a, 0.0)
        t = t - _dot_x3(t, _dot_x3(off, t))
        bs *= 2
    return t


def _gdn_body(q_ref, k_ref, v_ref, cols_ref, gct_ref, z_ref, wn_ref, s0_ref,
              o_ref, s_ref, *, tb, chunk):
    @pl.when(pl.program_id(1) == 0)
    def _():
        s_ref[...] = s0_ref[...]

    shift = int(math.log2(chunk))
    r = lax.broadcasted_iota(jnp.int32, (tb, tb), 0)
    c = lax.broadcasted_iota(jnp.int32, (tb, tb), 1)
    same = (r >> shift) == (c >> shift)
    incl = same & (r >= c)
    strict = same & (r > c)
    n_chunks = tb // chunk

    for h in range(GDN_HEADS):
        hs = slice(h * HEAD_DIM, (h + 1) * HEAD_DIM)
        qh = q_ref[0, :, hs]
        kh = k_ref[0, :, hs]
        vh = v_ref[0, :, hs]
        gcol = cols_ref[0, :, h:h + 1]
        beta = cols_ref[0, :, 8 + h:9 + h]
        eg = cols_ref[0, :, 16 + h:17 + h]
        egl = cols_ref[0, :, 24 + h:25 + h]
        grow = gct_ref[0, h:h + 1, :]
        decay = jnp.exp(jnp.where(incl, gcol - grow, -jnp.inf))
        kb = kh * beta
        a = jnp.where(strict, _dot_x3(kb, kh, NT_DIMS) * decay, 0.0)
        tinv = _unit_lower_inverse(a, r, c, chunk)
        rhs = jnp.concatenate([vh * beta, kb * eg], axis=1)
        sol = _dot_x3(tinv, rhs)
        u = sol[:, :HEAD_DIM]
        w = sol[:, HEAD_DIM:]
        attn = (_dot(qh.astype(BF16), kh.astype(BF16), NT_DIMS) * decay).astype(BF16)
        q_dec = (qh * eg).astype(BF16)
        k_dec = (kh * egl).astype(BF16)
        w_bf = w.astype(BF16)

        s = s_ref[0, h]
        v_new_parts = []
        o_state_parts = []
        for ci in range(n_chunks):
            rs = slice(ci * chunk, (ci + 1) * chunk)
            s_bf = s.astype(BF16)
            v_new = u[rs] - _dot(w_bf[rs], s_bf)
            o_state_parts.append(_dot(q_dec[rs], s_bf))
            g_tot = cols_ref[0, ci * chunk:ci * chunk + 1, 32 + h:33 + h]
            s = s * g_tot + _dot(k_dec[rs], v_new.astype(BF16), TN_DIMS)
            v_new_parts.append(v_new)
        s_ref[0, h] = s
        v_new_all = v_new_parts[0] if n_chunks == 1 else jnp.concatenate(v_new_parts, axis=0)
        o_state = o_state_parts[0] if n_chunks == 1 else jnp.concatenate(o_state_parts, axis=0)
        o = o_state + _dot(attn, v_new_all.astype(BF16))

        zh = z_ref[0, :, hs]
        o = (o * lax.rsqrt(jnp.mean(o * o, axis=-1, keepdims=True) + RMS_EPS) * wn_ref[...]
             * (zh * _sigmoid(zh)))
        o_ref[0, :, hs] = o.astype(BF16)


def _gdn(q, k, v, cols, gct, proj3, wn, s0, chunk, z_block):
    b, t, d_gdn = q.shape
    tb = min(t, GDN_TILE)
    body = functools.partial(_gdn_body, tb=tb, chunk=chunk)
    tile = lambda i, j: (i, j, 0)
    return pl.pallas_call(
        body,
        grid=(b, t // tb),
        in_specs=[pl.BlockSpec((1, tb, d_gdn), tile),
                  pl.BlockSpec((1, tb, d_gdn), tile),
                  pl.BlockSpec((1, tb, d_gdn), tile),
                  pl.BlockSpec((1, tb, LANES), tile),
                  pl.BlockSpec((1, SUBLANES, tb), lambda i, j: (i, 0, j)),
                  pl.BlockSpec((1, tb, d_gdn), lambda i, j: (i, j, z_block)),
                  pl.BlockSpec((1, HEAD_DIM), lambda i, j: (0, 0)),
                  pl.BlockSpec((1, GDN_HEADS, HEAD_DIM, HEAD_DIM), lambda i, j: (i, 0, 0, 0))],
        out_specs=[pl.BlockSpec((1, tb, d_gdn), tile),
                   pl.BlockSpec((1, GDN_HEADS, HEAD_DIM, HEAD_DIM), lambda i, j: (i, 0, 0, 0))],
        out_shape=[jax.ShapeDtypeStruct((b, t, d_gdn), BF16),
                   jax.ShapeDtypeStruct((b, GDN_HEADS, HEAD_DIM, HEAD_DIM), F32)],
        compiler_params=_cparams(("parallel", "arbitrary")),
        name="gdn_delta",
    )(q, k, v, cols, gct, proj3, wn, s0)


def _gelu_tanh(x):
    return 0.5 * x * (1.0 + jnp.tanh(math.sqrt(2.0 / math.pi) * (x + 0.044715 * (x * x * x))))


def _s5_body(u_ref, km_ref, wm_ref, vm_ref, lre_ref, lim_ref, dsk_ref, h0re_ref, h0im_ref,
             y_ref, hre_ref, him_ref, xbuf_ref, hbuf_ref, *, n_sub):
    half = GROUPS_PER_BLOCK * SSM_P

    @pl.when(pl.program_id(2) == 0)
    def _():
        hre_ref[...] = h0re_ref[...]
        him_ref[...] = h0im_ref[...]

    u_f = [u_ref[0, pl.ds(j, n_sub, stride=S5_SUB), :] for j in range(S5_SUB)]
    u_b = [x.astype(BF16) for x in u_f]

    x = _dot(u_b[0], wm_ref[0, 0])
    for j in range(1, S5_SUB):
        x = x + _dot(u_b[j], wm_ref[j, 0])
    xbuf_ref[...] = x

    lre = lre_ref[0]
    lim = lim_ref[0]

    def step(n, carry):
        hre, him = carry
        hbuf_ref[pl.ds(n, 1), 0:half] = hre
        hbuf_ref[pl.ds(n, 1), half:2 * half] = him
        xr = xbuf_ref[pl.ds(n, 1), 0:half]
        xi = xbuf_ref[pl.ds(n, 1), half:2 * half]
        return (lre * hre - lim * him + xr, lre * him + lim * hre + xi)

    hre, him = lax.fori_loop(0, n_sub, step, (hre_ref[0, 0], him_ref[0, 0]))
    hre_ref[0, 0] = hre
    him_ref[0, 0] = him

    h_b = hbuf_ref[...].astype(BF16)
    dsk = dsk_ref[0]
    for l in range(S5_SUB):
        y = _dot(h_b, vm_ref[l, 0])
        for d in range(l + 1):
            y = y + _dot(u_b[l - d], km_ref[d, 0])
        y = y + dsk * u_f[l]
        y_ref[0, pl.ds(l, n_sub, stride=S5_SUB), :] = _gelu_tanh(y)


def _s5(proj3, mats, h0re, h0im, u_block0):
    km, wm, vm, lre, lim, dsk = mats
    b, t, _ = proj3.shape
    n_gb = km.shape[1]
    tt = min(t, S5_TILE)
    n_sub = tt // S5_SUB
    half = GROUPS_PER_BLOCK * SSM_P
    body = functools.partial(_s5_body, n_sub=n_sub)
    state_spec = pl.BlockSpec((1, 1, 1, half), lambda g, i, j: (i, g, 0, 0))
    par_spec = pl.BlockSpec((1, 1, half), lambda g, i, j: (g, 0, 0))
    return pl.pallas_call(
        body,
        grid=(n_gb, b, t // tt),
        in_specs=[pl.BlockSpec((1, tt, LANES), lambda g, i, j: (i, j, u_block0 + g)),
                  pl.BlockSpec((S5_SUB, 1, LANES, LANES), lambda g, i, j: (0, g, 0, 0)),
                  pl.BlockSpec((S5_SUB, 1, LANES, 2 * half), lambda g, i, j: (0, g, 0, 0)),
                  pl.BlockSpec((S5_SUB, 1, 2 * half, LANES), lambda g, i, j: (0, g, 0, 0)),
                  par_spec, par_spec,
                  pl.BlockSpec((1, 1, LANES), lambda g, i, j: (g, 0, 0)),
                  state_spec, state_spec],
        out_specs=[pl.BlockSpec((1, tt, LANES), lambda g, i, j: (i, j, g)),
                   state_spec, state_spec],
        out_shape=[jax.ShapeDtypeStruct((b, t, n_gb * LANES), F32),
                   jax.ShapeDtypeStruct((b, n_gb, 1, half), F32),
                   jax.ShapeDtypeStruct((b, n_gb, 1, half), F32)],
        scratch_shapes=[pltpu.VMEM((n_sub, 2 * half), F32),
                        pltpu.VMEM((n_sub, 2 * half), F32)],
        compiler_params=_cparams(("parallel", "parallel", "arbitrary")),
        name="s5_scan",
    )(proj3, km, wm, vm, lre, lim, dsk, h0re, h0im)


def _s5_matrices(lam_re, lam_im, log_dt, b_re, b_im, c_re, c_im, d_skip):
    g, p = lam_re.shape
    n_gb = g // GROUPS_PER_BLOCK
    gpb = GROUPS_PER_BLOCK
    dt = jnp.exp(log_dt.astype(F32))
    lam = lax.complex(jnp.minimum(lam_re.astype(F32), -1e-4), lam_im.astype(F32))
    lam_bar = jnp.exp(lam * dt[:, None])
    b_bar = ((lam_bar - 1.0) / lam)[..., None] * lax.complex(b_re.astype(F32), b_im.astype(F32))
    c_c = lax.complex(c_re.astype(F32), c_im.astype(F32))
    pows = [jnp.ones_like(lam_bar)]
    for _ in range(S5_SUB):
        pows.append(pows[-1] * lam_bar)
    pw = jnp.stack(pows)
    eye = jnp.eye(gpb, dtype=F32)

    kd = jnp.real(jnp.einsum('gop,dgp,gpi->dgio', c_c, pw[:S5_SUB], b_bar))
    km = jnp.einsum('dbgio,gh->dbgiho', kd.reshape(S5_SUB, n_gb, gpb, SSM_CG, SSM_CG), eye)
    km = km.reshape(S5_SUB, n_gb, LANES, LANES)

    wj = pw[:S5_SUB][::-1][:, :, :, None] * b_bar[None]
    wj = jnp.transpose(wj, (0, 1, 3, 2)).reshape(S5_SUB, n_gb, gpb, SSM_CG, p)
    wre = jnp.einsum('dbgcp,gh->dbgchp', jnp.real(wj), eye).reshape(S5_SUB, n_gb, LANES, gpb * p)
    wim = jnp.einsum('dbgcp,gh->dbgchp', jnp.imag(wj), eye).reshape(S5_SUB, n_gb, LANES, gpb * p)
    wm = jnp.concatenate([wre, wim], axis=-1)

    cl = c_c[None] * pw[1:S5_SUB + 1][:, :, None, :]
    cl = jnp.transpose(cl, (0, 1, 3, 2)).reshape(S5_SUB, n_gb, gpb, p, SSM_CG)
    vre = jnp.einsum('dbgpc,gh->dbgphc', jnp.real(cl), eye).reshape(S5_SUB, n_gb, gpb * p, LANES)
    vim = jnp.einsum('dbgpc,gh->dbgphc', -jnp.imag(cl), eye).reshape(S5_SUB, n_gb, gpb * p, LANES)
    vm = jnp.concatenate([vre, vim], axis=-2)

    lam_s = pw[S5_SUB].reshape(n_gb, 1, gpb * p)
    dsk = d_skip.astype(F32).reshape(n_gb, 1, LANES)
    return (km.astype(BF16), wm.astype(BF16), vm.astype(BF16),
            jnp.real(lam_s), jnp.imag(lam_s), dsk)


def _glu_body(y_ref, w_ref, b_ref, o_ref):
    y = y_ref[...]
    gate = _dot(y.astype(BF16), w_ref[...]) + b_ref[...]
    o_ref[...] = (y * _sigmoid(gate)).astype(BF16)


def _glu(y2, w_bf, b_row, tm):
    m, d = y2.shape
    return pl.pallas_call(
        _glu_body,
        grid=(m // tm,),
        in_specs=[pl.BlockSpec((tm, d), lambda i: (i, 0)),
                  pl.BlockSpec((d, d), lambda i: (0, 0)),
                  pl.BlockSpec((1, d), lambda i: (0, 0))],
        out_specs=pl.BlockSpec((tm, d), lambda i: (i, 0)),
        out_shape=jax.ShapeDtypeStruct((m, d), BF16),
        compiler_params=_cparams(("parallel",)),
        name="s5_glu",
    )(y2, w_bf, b_row)


def _layernorm(v, g, b):
    mu = jnp.mean(v, axis=-1, keepdims=True)
    var = jnp.mean(jnp.square(v - mu), axis=-1, keepdims=True)
    return (v - mu) * lax.rsqrt(var + LN_EPS) * g + b


def _mix_body(og_ref, os_ref, wa_ref, wb_ref, x_ref, g_ref, b_ref, wr_ref, br_ref,
              h_ref, hb_ref, route_ref, *, alpha):
    mix = _dot(og_ref[...], wa_ref[...]) + _dot(os_ref[...], wb_ref[...])
    h = _layernorm(alpha * x_ref[...] + mix, g_ref[...], b_ref[...])
    h_ref[...] = h
    hb_ref[...] = h.astype(BF16)

    logits = _dot_x3(h, wr_ref[...]) + br_ref[...]
    tm = logits.shape[0]
    lane = lax.broadcasted_iota(jnp.int32, (tm, LANES), 1)
    work = jnp.where(lane < N_EXPERTS, logits, -jnp.inf)
    vals, idxs = [], []
    for _ in range(TOP_K):
        mx = jnp.max(work, axis=-1, keepdims=True)
        ix = jnp.min(jnp.where(work == mx, lane, LANES), axis=-1, keepdims=True)
        vals.append(mx)
        idxs.append(ix)
        work = jnp.where(lane == ix, -jnp.inf, work)
    exps = [jnp.exp(v - vals[0]) for v in vals]
    denom = exps[0]
    for e in exps[1:]:
        denom = denom + e
    route = jnp.zeros((tm, LANES), F32)
    for k in range(TOP_K):
        route = jnp.where(lane == k, idxs[k].astype(F32), route)
        route = jnp.where(lane == TOP_K + k, exps[k] / denom, route)
    route_ref[...] = route


def _mix_ln_route(og, osm, wa, wb, x2, g_row, b_row, wr, br, alpha, tm):
    m, d = x2.shape
    dh = og.shape[1]
    body = functools.partial(_mix_body, alpha=alpha)
    row = lambda i: (i, 0)
    fix = lambda i: (0, 0)
    return pl.pallas_call(
        body,
        grid=(m // tm,),
        in_specs=[pl.BlockSpec((tm, dh), row), pl.BlockSpec((tm, dh), row),
                  pl.BlockSpec((dh, d), fix), pl.BlockSpec((dh, d), fix),
                  pl.BlockSpec((tm, d), row),
                  pl.BlockSpec((1, d), fix), pl.BlockSpec((1, d), fix),
                  pl.BlockSpec((d, LANES), fix), pl.BlockSpec((1, LANES), fix)],
        out_specs=[pl.BlockSpec((tm, d), row), pl.BlockSpec((tm, d), row),
                   pl.BlockSpec((tm, LANES), row)],
        out_shape=[jax.ShapeDtypeStruct((m, d), F32),
                   jax.ShapeDtypeStruct((m, d), BF16),
                   jax.ShapeDtypeStruct((m, LANES), F32)],
        compiler_params=_cparams(("parallel",)),
        name="mix_ln_route",
    )(og, osm, wa, wb, x2, g_row, b_row, wr, br)


def _moe_body(te_ref, nu_ref, x_ref, gate_ref, wg_ref, bg_ref, wu_ref, bu_ref, wd_ref, bd_ref,
              o_ref, acc_ref, *, n_f):
    i = pl.program_id(0)
    f = pl.program_id(1)
    used = i < nu_ref[0]

    @pl.when(used)
    def _():
        xb = x_ref[...]
        hg = jnp.minimum(_dot(xb, wg_ref[0].astype(BF16)) + bg_ref[0], SWIGLU_LIMIT)
        hu = jnp.clip(_dot(xb, wu_ref[0].astype(BF16)) + bu_ref[0], -SWIGLU_LIMIT, SWIGLU_LIMIT)
        hh = (hu + 1.0) * (hg * _sigmoid(SWIGLU_ALPHA * hg))
        part = _dot(hh.astype(BF16), wd_ref[0].astype(BF16))

        @pl.when(f == 0)
        def _():
            acc_ref[...] = part

        @pl.when(f != 0)
        def _():
            acc_ref[...] += part

        @pl.when(f == n_f - 1)
        def _():
            o_ref[...] = (acc_ref[...] + bd_ref[0]) * gate_ref[...]

    @pl.when(jnp.logical_not(used) & (f == n_f - 1))
    def _():
        o_ref[...] = jnp.zeros_like(o_ref)


def _moe_ffn(tile_expert, n_used, x_sorted, gate_col, w_gate, b_gate, w_up, b_up, w_down, b_down, tm, tf):
    cap, d = x_sorted.shape
    n_e, _, d_ff = w_gate.shape
    n_tiles = cap // tm
    n_f = d_ff // tf
    body = functools.partial(_moe_body, n_f=n_f)

    def fcol(i, f, nu):
        return jnp.where(i < nu[0], f, n_f - 1)

    grid_spec = pltpu.PrefetchScalarGridSpec(
        num_scalar_prefetch=2,
        grid=(n_tiles, n_f),
        in_specs=[pl.BlockSpec((tm, d), lambda i, f, te, nu: (i, 0)),
                  pl.BlockSpec((tm, 1), lambda i, f, te, nu: (i, 0)),
                  pl.BlockSpec((1, d, tf), lambda i, f, te, nu: (te[i], 0, fcol(i, f, nu))),
                  pl.BlockSpec((1, 1, tf), lambda i, f, te, nu: (te[i], 0, fcol(i, f, nu))),
                  pl.BlockSpec((1, d, tf), lambda i, f, te, nu: (te[i], 0, fcol(i, f, nu))),
                  pl.BlockSpec((1, 1, tf), lambda i, f, te, nu: (te[i], 0, fcol(i, f, nu))),
                  pl.BlockSpec((1, tf, d), lambda i, f, te, nu: (te[i], fcol(i, f, nu), 0)),
                  pl.BlockSpec((1, 1, d), lambda i, f, te, nu: (te[i], 0, 0))],
        out_specs=pl.BlockSpec((tm, d), lambda i, f, te, nu: (i, 0)),
        scratch_shapes=[pltpu.VMEM((tm, d), F32)],
    )
    return pl.pallas_call(
        body,
        grid_spec=grid_spec,
        out_shape=jax.ShapeDtypeStruct((cap, d), F32),
        compiler_params=_cparams(("arbitrary", "arbitrary")),
        name="moe_ffn",
    )(tile_expert, n_used, x_sorted, gate_col, w_gate, b_gate.reshape(n_e, 1, d_ff),
      w_up, b_up.reshape(n_e, 1, d_ff), w_down, b_down.reshape(n_e, 1, d))


def _final_body(h_ref, ff_ref, g_ref, b_ref, o_ref, *, alpha):
    o_ref[...] = _layernorm(alpha * h_ref[...] + ff_ref[...], g_ref[...], b_ref[...])


def _final_ln(h, ff, g_row, b_row, alpha, tm):
    m, d = h.shape
    body = functools.partial(_final_body, alpha=alpha)
    row = lambda i: (i, 0)
    fix = lambda i: (0, 0)
    return pl.pallas_call(
        body,
        grid=(m // tm,),
        in_specs=[pl.BlockSpec((tm, d), row), pl.BlockSpec((tm, d), row),
                  pl.BlockSpec((1, d), fix), pl.BlockSpec((1, d), fix)],
        out_specs=pl.BlockSpec((tm, d), row),
        out_shape=jax.ShapeDtypeStruct((m, d), F32),
        compiler_params=_cparams(("parallel",)),
        name="final_ln",
    )(h, ff, g_row, b_row)


def _row_tile(m, pref):
    t = min(m, pref)
    while m % t:
        t //= 2
    return t


def _route_tables(route, m, tm):
    n_assign = m * TOP_K
    e_flat = route[:, :TOP_K].astype(jnp.int32).reshape(n_assign)
    g_flat = route[:, TOP_K:2 * TOP_K].reshape(n_assign)
    tok_flat = jnp.repeat(jnp.arange(m, dtype=jnp.int32), TOP_K)
    order = jnp.argsort(e_flat)
    e_sorted = e_flat[order]
    counts = jnp.zeros((N_EXPERTS,), jnp.int32).at[e_flat].add(1)
    starts = jnp.cumsum(counts) - counts
    padded = (counts + tm - 1) // tm * tm
    pends = jnp.cumsum(padded)
    pstarts = pends - padded
    dest = pstarts[e_sorted] + jnp.arange(n_assign, dtype=jnp.int32) - starts[e_sorted]
    n_tiles = -(-n_assign // tm) + N_EXPERTS
    cap = n_tiles * tm
    row_tok = jnp.full((cap,), m, jnp.int32).at[dest].set(tok_flat[order])
    row_gate = jnp.zeros((cap,), F32).at[dest].set(g_flat[order])
    tile_start = jnp.arange(n_tiles, dtype=jnp.int32) * tm
    tile_expert = jnp.minimum(jnp.searchsorted(pends, tile_start, side='right'), N_EXPERTS - 1).astype(jnp.int32)
    n_used = (pends[-1] // tm).astype(jnp.int32).reshape(1)
    last_used = jnp.maximum(n_used[0] - 1, 0)
    tile_expert = jnp.where(tile_start // tm < n_used[0], tile_expert, tile_expert[last_used])
    return row_tok, row_gate, tile_expert, n_used


def _layer(x, conv_hist, s_gdn, h_re, h_im, chunk, alpha, p):
    b, t, d = x.shape
    m = b * t
    d_gdn = GDN_HEADS * HEAD_DIM
    x2 = x.reshape(m, d)

    proj = _in_proj(x2, p['w_in'], _row_tile(m, 1024), p['proj_tn'])
    proj3 = proj.reshape(b, t, proj.shape[1])
    conv_new = proj3[:, t - (CONV_W - 1):, :3 * d_gdn]

    q, k, v, cols, gct = _gdn_prep(proj3, conv_hist, p['w_conv'], p['alog_row'], p['dtb_row'],
                                   chunk, p['ab_block'])
    o_gdn, s_new = _gdn(q, k, v, cols, gct, proj3, p['wn_row'], s_gdn, chunk, p['z_block'])

    n_gb = p['s5_mats'][0].shape[1]
    half = GROUPS_PER_BLOCK * SSM_P
    yg, hre_new, him_new = _s5(proj3, p['s5_mats'], h_re.reshape(b, n_gb, 1, half),
                               h_im.reshape(b, n_gb, 1, half), p['u_block0'])
    o_ssm = _glu(yg.reshape(m, yg.shape[2]), p['w_glu'], p['b_glu_row'], _row_tile(m, 512))

    h, hb, route = _mix_ln_route(o_gdn.reshape(m, d_gdn), o_ssm, p['w_out_a'], p['w_out_b'], x2,
                                 p['ln1_g'], p['ln1_b'], p['w_router'], p['b_router'], alpha, _row_tile(m, 256))

    tm = p['moe_tm'] if m * TOP_K >= 8 * p['moe_tm'] else 128
    row_tok, row_gate, tile_expert, n_used = _route_tables(route, m, tm)
    hb_pad = jnp.concatenate([hb, jnp.zeros((1, d), BF16)], axis=0)
    x_sorted = hb_pad[row_tok]
    outs = _moe_ffn(tile_expert, n_used, x_sorted, row_gate[:, None], p['w_gate'], p['b_gate'],
                    p['w_up'], p['b_up'], p['w_down'], p['b_down'], tm, p['moe_tf'])
    ff = jnp.zeros((m + 1, d), F32).at[row_tok].add(outs)[:m]

    y = _final_ln(h, ff, p['ln2_g'], p['ln2_b'], alpha, _row_tile(m, 512))
    g_all = h_re.shape[1]
    return (y.reshape(b, t, d), conv_new, s_new,
            hre_new.reshape(b, g_all, SSM_P), him_new.reshape(b, g_all, SSM_P))


def _pad_lanes(v, fill=0.0):
    return jnp.pad(v.astype(F32), (0, LANES - v.shape[0]), constant_values=fill).reshape(1, LANES)


def _layer_params(l, w_in, w_conv, a_log, dt_bias, w_onorm, lam_re, lam_im, log_dt, b_re, b_im, c_re, c_im,
                  d_skip, w_glu, b_glu, w_out, ln1_g, ln1_b, w_router, b_router, w_gate, b_gate,
                  w_up, b_up, w_down, b_down, ln2_g, ln2_b):
    d_model = w_in.shape[1]
    d_gdn = GDN_HEADS * HEAD_DIM
    d_qkvz = 4 * d_gdn
    d_ssm = d_model - d_gdn
    wi = w_in[l]
    proj_tn = 896
    n_cols = d_qkvz + d_ssm + LANES
    n_pad = -(-n_cols // proj_tn) * proj_tn
    w_in_r = jnp.concatenate([wi[:, :d_qkvz], wi[:, d_qkvz + 2 * GDN_HEADS:],
                              wi[:, d_qkvz:d_qkvz + 2 * GDN_HEADS],
                              jnp.zeros((d_model, n_pad - d_qkvz - d_ssm - 2 * GDN_HEADS), wi.dtype)], axis=1)
    wo = w_out[l].astype(BF16)
    wr = jnp.pad(w_router[l].astype(F32), ((0, 0), (0, LANES - N_EXPERTS)))
    return {
        'w_in': w_in_r.astype(BF16), 'proj_tn': proj_tn,
        'z_block': 3, 'u_block0': (d_qkvz) // LANES, 'ab_block': (d_qkvz + d_ssm) // LANES,
        'w_conv': w_conv[l].astype(F32),
        'alog_row': _pad_lanes(a_log[l]), 'dtb_row': _pad_lanes(dt_bias[l]),
        'wn_row': w_onorm[l].astype(F32).reshape(1, HEAD_DIM),
        's5_mats': _s5_matrices(lam_re[l], lam_im[l], log_dt[l], b_re[l], b_im[l], c_re[l], c_im[l], d_skip[l]),
        'w_glu': w_glu[l].astype(BF16), 'b_glu_row': b_glu[l].astype(F32).reshape(1, d_ssm),
        'w_out_a': wo[:d_gdn], 'w_out_b': wo[d_gdn:],
        'ln1_g': ln1_g[l].astype(F32).reshape(1, d_model), 'ln1_b': ln1_b[l].astype(F32).reshape(1, d_model),
        'w_router': wr, 'b_router': _pad_lanes(b_router[l]),
        'w_gate': w_gate[l], 'b_gate': b_gate[l], 'w_up': w_up[l], 'b_up': b_up[l],
        'w_down': w_down[l], 'b_down': b_down[l],
        'ln2_g': ln2_g[l].astype(F32).reshape(1, d_model), 'ln2_b': ln2_b[l].astype(F32).reshape(1, d_model),
        'moe_tm': 768, 'moe_tf': 256,
    }


def kernel(x_prompt, x_sample, state_conv, state_gdn, state_ssm_re, state_ssm_im, w_in, w_conv, a_log, dt_bias, w_onorm, lam_re, lam_im, log_dt, b_re, b_im, c_re, c_im, d_skip, w_glu, b_glu, w_out, ln1_g, ln1_b, w_router, b_router, w_gate, b_gate, w_up, b_up, w_down, b_down, ln2_g, ln2_b):
    depth = w_in.shape[0]
    alpha = (2.0 * depth) ** 0.25
    bp, seq, _ = x_prompt.shape
    chunk_p = 64
    d_qkv = state_conv.shape[-1]
    n_groups, n_p = state_ssm_re.shape[-2:]
    yp, ys = x_prompt, x_sample
    outs_p = [[], [], [], []]
    outs_s = [[], [], [], []]
    for l in range(depth):
        p = _layer_params(l, w_in, w_conv, a_log, dt_bias, w_onorm, lam_re, lam_im, log_dt, b_re, b_im,
                          c_re, c_im, d_skip, w_glu, b_glu, w_out, ln1_g, ln1_b, w_router, b_router,
                          w_gate, b_gate, w_up, b_up, w_down, b_down, ln2_g, ln2_b)
        yp, cp, sp, rp, ip = _layer(
            yp, jnp.zeros((bp, CONV_W - 1, d_qkv), F32),
            jnp.zeros((bp, GDN_HEADS, HEAD_DIM, HEAD_DIM), F32),
            jnp.zeros((bp, n_groups, n_p), F32), jnp.zeros((bp, n_groups, n_p), F32),
            chunk_p, alpha, p)
        ys, cs, ss, rs, is_ = _layer(
            ys, state_conv[l].astype(F32), state_gdn[l].astype(F32),
            state_ssm_re[l].astype(F32), state_ssm_im[l].astype(F32),
            ys.shape[1], alpha, p)
        for acc, val in zip(outs_p, (cp, sp, rp, ip)):
            acc.append(val)
        for acc, val in zip(outs_s, (cs, ss, rs, is_)):
            acc.append(val)
    return (yp, ys, *[jnp.stack(a) for a in outs_p], *[jnp.stack(a) for a in outs_s])
```

```python
import functools
import math

import jax
import jax.numpy as jnp
from jax import lax
from jax.experimental import pallas as pl
from jax.experimental.pallas import tpu as pltpu

F32 = jnp.float32
BF16 = jnp.bfloat16

GDN_HEADS = 8
HEAD_DIM = 128
CONV_W = 4
SSM_CG = 16
SSM_P = 64
N_EXPERTS = 32
TOP_K = 4
SWIGLU_ALPHA = 1.702
SWIGLU_LIMIT = 7.0
LN_EPS = 1e-5
RMS_EPS = 1e-6
L2_EPS = 1e-6

LANES = 128
SUBLANES = 8
S5_SUB = 8
S5_TILE = 2048
GDN_TILE = 128
GROUPS_PER_BLOCK = LANES // SSM_CG
VMEM_LIMIT = 56 * 1024 * 1024

NT_DIMS = (((1,), (1,)), ((), ()))
TN_DIMS = (((0,), (0,)), ((), ()))


def _dot(a, b, dims=(((1,), (0,)), ((), ()))):
    return lax.dot_general(a, b, dims, preferred_element_type=F32)


def _split(a):
    hi = a.astype(BF16)
    lo = (a - hi.astype(F32)).astype(BF16)
    return hi, lo


def _dot_x3(a, b, dims=(((1,), (0,)), ((), ()))):
    ah, al = _split(a)
    bh, bl = _split(b)
    return _dot(ah, bh, dims) + (_dot(ah, bl, dims) + _dot(al, bh, dims))


def _dot_bf(a, b, dims=(((1,), (0,)), ((), ()))):
    return _dot(a.astype(BF16), b.astype(BF16), dims)


_gdn_mm = _dot_bf


def _sigmoid(x):
    return 1.0 / (1.0 + jnp.exp(-x))


def _cparams(sem):
    return pltpu.CompilerParams(dimension_semantics=sem, vmem_limit_bytes=VMEM_LIMIT)


def _proj_body(x_ref, w_ref, o_ref, xb_ref):
    @pl.when(pl.program_id(1) == 0)
    def _():
        xb_ref[...] = x_ref[...].astype(BF16)

    o_ref[...] = _dot(xb_ref[...], w_ref[...])


def _in_proj(x2, w_bf, tm, tn):
    m, k = x2.shape
    n = w_bf.shape[1]
    return pl.pallas_call(
        _proj_body,
        grid=(m // tm, n // tn),
        in_specs=[pl.BlockSpec((tm, k), lambda i, j: (i, 0)),
                  pl.BlockSpec((k, tn), lambda i, j: (0, j))],
        out_specs=pl.BlockSpec((tm, tn), lambda i, j: (i, j)),
        out_shape=jax.ShapeDtypeStruct((m, n), F32),
        scratch_shapes=[pltpu.VMEM((tm, k), BF16)],
        compiler_params=_cparams(("parallel", "arbitrary")),
        name="in_proj",
    )(x2, w_bf)


def _gdn_prep_body(qkv_ref, ab_ref, hist_ref, wc_ref, alog_ref, dtb_ref,
                   q_ref, k_ref, v_ref, cols_ref, gct_ref, xbuf_ref, *, tt, chunk):
    d_gdn = GDN_HEADS * HEAD_DIM
    halo = SUBLANES

    @pl.when(pl.program_id(1) == 0)
    def _():
        xbuf_ref[0:halo, :] = jnp.zeros((halo, 3 * d_gdn), F32)
        xbuf_ref[halo - (CONV_W - 1):halo, :] = hist_ref[0]

    xbuf_ref[halo:halo + tt, :] = qkv_ref[0]

    for part, out_ref in enumerate((q_ref, k_ref, v_ref)):
        c0 = part * d_gdn
        y = None
        for j in range(CONV_W):
            r0 = halo - (CONV_W - 1) + j
            term = xbuf_ref[r0:r0 + tt, c0:c0 + d_gdn] * wc_ref[j:j + 1, c0:c0 + d_gdn]
            y = term if y is None else y + term
        s = y * _sigmoid(y)
        if part == 2:
            out_ref[0] = s
        else:
            for h in range(GDN_HEADS):
                sh = s[:, h * HEAD_DIM:(h + 1) * HEAD_DIM]
                nrm = sh * lax.rsqrt(jnp.sum(sh * sh, axis=-1, keepdims=True) + L2_EPS)
                if part == 0:
                    nrm = nrm * (HEAD_DIM ** -0.5)
                out_ref[0, :, h * HEAD_DIM:(h + 1) * HEAD_DIM] = nrm

    xbuf_ref[0:halo, :] = xbuf_ref[tt:tt + halo, :]

    ab = ab_ref[0]
    lane = lax.broadcasted_iota(jnp.int32, (tt, LANES), 1)
    is_a = lane < GDN_HEADS
    z = ab + dtb_ref[...]
    softplus = jnp.maximum(z, 0.0) + jnp.log1p(jnp.exp(-jnp.abs(z)))
    g = jnp.where(is_a, -jnp.exp(alog_ref[...]) * softplus, 0.0)
    beta = _sigmoid(ab)

    shift = int(math.log2(chunk))
    r = lax.broadcasted_iota(jnp.int32, (tt, tt), 0)
    c = lax.broadcasted_iota(jnp.int32, (tt, tt), 1)
    same = (r >> shift) == (c >> shift)
    m_incl = jnp.where(same & (r >= c), 1.0, 0.0).astype(BF16)
    m_all = jnp.where(same, 1.0, 0.0).astype(BF16)
    gc = _dot_exact_lhs_rhs(m_incl, g)
    glast = _dot_exact_lhs_rhs(m_all, g)
    eg = jnp.exp(gc)
    egl = jnp.exp(glast - gc)
    egt = jnp.exp(glast)
    zero = jnp.zeros_like(gc)
    cols = (jnp.where(is_a, gc, zero)
            + jnp.where((lane >= 8) & (lane < 16), beta, zero)
            + pltpu.roll(jnp.where(is_a, eg, zero), 16, 1)
            + pltpu.roll(jnp.where(is_a, egl, zero), 24, 1)
            + pltpu.roll(jnp.where(is_a, egt, zero), 32, 1))
    cols_ref[0] = cols

    er = lax.broadcasted_iota(jnp.int32, (SUBLANES, LANES), 0)
    ec = lax.broadcasted_iota(jnp.int32, (SUBLANES, LANES), 1)
    sel = jnp.where(er == ec, 1.0, 0.0).astype(BF16)
    gct_ref[0] = _dot_exact_rhs(sel, jnp.where(is_a, gc, zero), NT_DIMS)


def _three_pieces(a):
    a0 = a.astype(BF16)
    r1 = a - a0.astype(F32)
    a1 = r1.astype(BF16)
    a2 = (r1 - a1.astype(F32)).astype(BF16)
    return a0, a1, a2


def _dot_exact_lhs_rhs(mask_bf, a):
    a0, a1, a2 = _three_pieces(a)
    return _dot(mask_bf, a0) + (_dot(mask_bf, a1) + _dot(mask_bf, a2))


def _dot_exact_rhs(mask_bf, a, dims):
    a0, a1, a2 = _three_pieces(a)
    return _dot(mask_bf, a0, dims) + (_dot(mask_bf, a1, dims) + _dot(mask_bf, a2, dims))


def _gdn_prep(proj3, hist, w_conv, alog_p, dtb_p, chunk, ab_block):
    b, t, _ = proj3.shape
    d_gdn = GDN_HEADS * HEAD_DIM
    tt = min(t, GDN_TILE)
    body = functools.partial(_gdn_prep_body, tt=tt, chunk=chunk)
    big = jax.ShapeDtypeStruct((b, t, d_gdn), F32)
    return pl.pallas_call(
        body,
        grid=(b, t // tt),
        in_specs=[pl.BlockSpec((1, tt, 3 * d_gdn), lambda i, j: (i, j, 0)),
                  pl.BlockSpec((1, tt, LANES), lambda i, j: (i, j, ab_block)),
                  pl.BlockSpec((1, CONV_W - 1, 3 * d_gdn), lambda i, j: (i, 0, 0)),
                  pl.BlockSpec((CONV_W, 3 * d_gdn), lambda i, j: (0, 0)),
                  pl.BlockSpec((1, LANES), lambda i, j: (0, 0)),
                  pl.BlockSpec((1, LANES), lambda i, j: (0, 0))],
        out_specs=[pl.BlockSpec((1, tt, d_gdn), lambda i, j: (i, j, 0)),
                   pl.BlockSpec((1, tt, d_gdn), lambda i, j: (i, j, 0)),
                   pl.BlockSpec((1, tt, d_gdn), lambda i, j: (i, j, 0)),
                   pl.BlockSpec((1, tt, LANES), lambda i, j: (i, j, 0)),
                   pl.BlockSpec((1, SUBLANES, tt), lambda i, j: (i, 0, j))],
        out_shape=[big, big, big,
                   jax.ShapeDtypeStruct((b, t, LANES), F32),
                   jax.ShapeDtypeStruct((b, SUBLANES, t), F32)],
        scratch_shapes=[pltpu.VMEM((tt + SUBLANES, 3 * d_gdn), F32)],
        compiler_params=_cparams(("parallel", "arbitrary")),
        name="gdn_prep",
    )(proj3, proj3, hist, w_conv, alog_p, dtb_p)


def _unit_lower_inverses(mats, r, c, chunk):
    base = 16
    eye = jnp.where(r == c, 1.0, 0.0)

    def blk(bs):
        s = int(math.log2(bs))
        return (r >> s) == (c >> s)

    d1 = [jnp.where(blk(base), a, 0.0) for a in mats]
    d2 = [_gdn_mm(x, x) for x in d1]
    d4 = [_gdn_mm(x, x) for x in d2]
    d8 = [_gdn_mm(x, x) for x in d4]
    t = [eye - x for x in d1]
    t = [x + _gdn_mm(x, y) for x, y in zip(t, d2)]
    t = [x + _gdn_mm(x, y) for x, y in zip(t, d4)]
    t = [x + _gdn_mm(x, y) for x, y in zip(t, d8)]
    bs = base
    while bs < chunk:
        off_mask = blk(2 * bs) & jnp.logical_not(blk(bs))
        inner = [_gdn_mm(jnp.where(off_mask, a, 0.0), x) for a, x in zip(mats, t)]
        t = [x - _gdn_mm(x, y) for x, y in zip(t, inner)]
        bs *= 2
    return t


def _gdn_body(q_ref, k_ref, v_ref, cols_ref, gct_ref, z_ref, wn_ref, s0_ref,
              o_ref, s_ref, *, tb, chunk):
    @pl.when(pl.program_id(1) == 0)
    def _():
        s_ref[...] = s0_ref[...]

    shift = int(math.log2(chunk))
    r = lax.broadcasted_iota(jnp.int32, (tb, tb), 0)
    c = lax.broadcasted_iota(jnp.int32, (tb, tb), 1)
    same = (r >> shift) == (c >> shift)
    incl = same & (r >= c)
    strict = same & (r > c)
    n_chunks = tb // chunk

    heads = range(GDN_HEADS)
    hsl = [slice(h * HEAD_DIM, (h + 1) * HEAD_DIM) for h in heads]
    q = [q_ref[0, :, s] for s in hsl]
    k = [k_ref[0, :, s] for s in hsl]
    beta = [cols_ref[0, :, 8 + h:9 + h] for h in heads]
    eg = [cols_ref[0, :, 16 + h:17 + h] for h in heads]
    egl = [cols_ref[0, :, 24 + h:25 + h] for h in heads]
    decay = [jnp.exp(jnp.where(incl, cols_ref[0, :, h:h + 1] - gct_ref[0, h:h + 1, :], -jnp.inf)) for h in heads]
    kb = [k[h] * beta[h] for h in heads]
    k_bf = [x.astype(BF16) for x in k]
    a = [jnp.where(strict, _dot(kb[h].astype(BF16), k_bf[h], NT_DIMS) * decay[h], 0.0) for h in heads]
    tinv = _unit_lower_inverses(a, r, c, chunk)
    sol = [_gdn_mm(tinv[h], jnp.concatenate([v_ref[0, :, hsl[h]] * beta[h], kb[h] * eg[h]], axis=1)) for h in heads]
    u = [x[:, :HEAD_DIM] for x in sol]
    w_bf = [x[:, HEAD_DIM:].astype(BF16) for x in sol]
    attn = [(_dot(q[h].astype(BF16), k_bf[h], NT_DIMS) * decay[h]).astype(BF16) for h in heads]
    q_dec = [(q[h] * eg[h]).astype(BF16) for h in heads]
    k_dec = [(k[h] * egl[h]).astype(BF16) for h in heads]

    s = [s_ref[0, h] for h in heads]
    v_new = [[] for _ in heads]
    o_state = [[] for _ in heads]
    for ci in range(n_chunks):
        rs = slice(ci * chunk, (ci + 1) * chunk)
        s_bf = [x.astype(BF16) for x in s]
        vn = [u[h][rs] - _dot(w_bf[h][rs], s_bf[h]) for h in heads]
        for h in heads:
            o_state[h].append(_dot(q_dec[h][rs], s_bf[h]))
            v_new[h].append(vn[h])
        s = [s[h] * cols_ref[0, ci * chunk:ci * chunk + 1, 32 + h:33 + h]
             + _dot(k_dec[h][rs], vn[h].astype(BF16), TN_DIMS) for h in heads]
    for h in heads:
        s_ref[0, h] = s[h]

    def cat(parts):
        return parts[0] if len(parts) == 1 else jnp.concatenate(parts, axis=0)

    for h in heads:
        o = cat(o_state[h]) + _dot(attn[h], cat(v_new[h]).astype(BF16))
        zh = z_ref[0, :, hsl[h]]
        o = (o * lax.rsqrt(jnp.mean(o * o, axis=-1, keepdims=True) + RMS_EPS) * wn_ref[...]
             * (zh * _sigmoid(zh)))
        o_ref[0, :, hsl[h]] = o.astype(BF16)


def _gdn(q, k, v, cols, gct, proj3, wn, s0, chunk, z_block):
    b, t, d_gdn = q.shape
    tb = min(t, GDN_TILE)
    body = functools.partial(_gdn_body, tb=tb, chunk=chunk)
    tile = lambda i, j: (i, j, 0)
    return pl.pallas_call(
        body,
        grid=(b, t // tb),
        in_specs=[pl.BlockSpec((1, tb, d_gdn), tile),
                  pl.BlockSpec((1, tb, d_gdn), tile),
                  pl.BlockSpec((1, tb, d_gdn), tile),
                  pl.BlockSpec((1, tb, LANES), tile),
                  pl.BlockSpec((1, SUBLANES, tb), lambda i, j: (i, 0, j)),
                  pl.BlockSpec((1, tb, d_gdn), lambda i, j: (i, j, z_block)),
                  pl.BlockSpec((1, HEAD_DIM), lambda i, j: (0, 0)),
                  pl.BlockSpec((1, GDN_HEADS, HEAD_DIM, HEAD_DIM), lambda i, j: (i, 0, 0, 0))],
        out_specs=[pl.BlockSpec((1, tb, d_gdn), tile),
                   pl.BlockSpec((1, GDN_HEADS, HEAD_DIM, HEAD_DIM), lambda i, j: (i, 0, 0, 0))],
        out_shape=[jax.ShapeDtypeStruct((b, t, d_gdn), BF16),
                   jax.ShapeDtypeStruct((b, GDN_HEADS, HEAD_DIM, HEAD_DIM), F32)],
        compiler_params=_cparams(("parallel", "arbitrary")),
        name="gdn_delta",
    )(q, k, v, cols, gct, proj3, wn, s0)


def _gelu_tanh(x):
    return 0.5 * x * (1.0 + jnp.tanh(math.sqrt(2.0 / math.pi) * (x + 0.044715 * (x * x * x))))


def _s5_body(u_ref, km_ref, wm_ref, vm_ref, lre_ref, lim_ref, dsk_ref, h0re_ref, h0im_ref,
             y_ref, hre_ref, him_ref, xbuf_ref, hbuf_ref, *, n_sub):
    half = GROUPS_PER_BLOCK * SSM_P

    @pl.when(pl.program_id(2) == 0)
    def _():
        hre_ref[...] = h0re_ref[...]
        him_ref[...] = h0im_ref[...]

    u_f = [u_ref[0, pl.ds(j, n_sub, stride=S5_SUB), :] for j in range(S5_SUB)]
    u_b = [x.astype(BF16) for x in u_f]

    x = _dot(u_b[0], wm_ref[0, 0])
    for j in range(1, S5_SUB):
        x = x + _dot(u_b[j], wm_ref[j, 0])
    xbuf_ref[...] = x

    lre = lre_ref[0]
    lim = lim_ref[0]

    def step(n, carry):
        hre, him = carry
        hbuf_ref[pl.ds(n, 1), 0:half] = hre
        hbuf_ref[pl.ds(n, 1), half:2 * half] = him
        xr = xbuf_ref[pl.ds(n, 1), 0:half]
        xi = xbuf_ref[pl.ds(n, 1), half:2 * half]
        return (lre * hre - lim * him + xr, lre * him + lim * hre + xi)

    hre, him = lax.fori_loop(0, n_sub, step, (hre_ref[0, 0], him_ref[0, 0]))
    hre_ref[0, 0] = hre
    him_ref[0, 0] = him

    h_b = hbuf_ref[...].astype(BF16)
    dsk = dsk_ref[0]
    for l in range(S5_SUB):
        y = _dot(h_b, vm_ref[l, 0])
        for d in range(l + 1):
            y = y + _dot(u_b[l - d], km_ref[d, 0])
        y = y + dsk * u_f[l]
        y_ref[0, pl.ds(l, n_sub, stride=S5_SUB), :] = _gelu_tanh(y)


def _s5(proj3, mats, h0re, h0im, u_block0):
    km, wm, vm, lre, lim, dsk = mats
    b, t, _ = proj3.shape
    n_gb = km.shape[1]
    tt = min(t, S5_TILE)
    n_sub = tt // S5_SUB
    half = GROUPS_PER_BLOCK * SSM_P
    body = functools.partial(_s5_body, n_sub=n_sub)
    state_spec = pl.BlockSpec((1, 1, 1, half), lambda g, i, j: (i, g, 0, 0))
    par_spec = pl.BlockSpec((1, 1, half), lambda g, i, j: (g, 0, 0))
    return pl.pallas_call(
        body,
        grid=(n_gb, b, t // tt),
        in_specs=[pl.BlockSpec((1, tt, LANES), lambda g, i, j: (i, j, u_block0 + g)),
                  pl.BlockSpec((S5_SUB, 1, LANES, LANES), lambda g, i, j: (0, g, 0, 0)),
                  pl.BlockSpec((S5_SUB, 1, LANES, 2 * half), lambda g, i, j: (0, g, 0, 0)),
                  pl.BlockSpec((S5_SUB, 1, 2 * half, LANES), lambda g, i, j: (0, g, 0, 0)),
                  par_spec, par_spec,
                  pl.BlockSpec((1, 1, LANES), lambda g, i, j: (g, 0, 0)),
                  state_spec, state_spec],
        out_specs=[pl.BlockSpec((1, tt, LANES), lambda g, i, j: (i, j, g)),
                   state_spec, state_spec],
        out_shape=[jax.ShapeDtypeStruct((b, t, n_gb * LANES), F32),
                   jax.ShapeDtypeStruct((b, n_gb, 1, half), F32),
                   jax.ShapeDtypeStruct((b, n_gb, 1, half), F32)],
        scratch_shapes=[pltpu.VMEM((n_sub, 2 * half), F32),
                        pltpu.VMEM((n_sub, 2 * half), F32)],
        compiler_params=_cparams(("parallel", "parallel", "arbitrary")),
        name="s5_scan",
    )(proj3, km, wm, vm, lre, lim, dsk, h0re, h0im)


def _s5_matrices(lam_re, lam_im, log_dt, b_re, b_im, c_re, c_im, d_skip):
    g, p = lam_re.shape
    n_gb = g // GROUPS_PER_BLOCK
    gpb = GROUPS_PER_BLOCK
    dt = jnp.exp(log_dt.astype(F32))
    lam = lax.complex(jnp.minimum(lam_re.astype(F32), -1e-4), lam_im.astype(F32))
    lam_bar = jnp.exp(lam * dt[:, None])
    b_bar = ((lam_bar - 1.0) / lam)[..., None] * lax.complex(b_re.astype(F32), b_im.astype(F32))
    c_c = lax.complex(c_re.astype(F32), c_im.astype(F32))
    pows = [jnp.ones_like(lam_bar)]
    for _ in range(S5_SUB):
        pows.append(pows[-1] * lam_bar)
    pw = jnp.stack(pows)
    eye = jnp.eye(gpb, dtype=F32)

    kd = jnp.real(jnp.einsum('gop,dgp,gpi->dgio', c_c, pw[:S5_SUB], b_bar))
    km = jnp.einsum('dbgio,gh->dbgiho', kd.reshape(S5_SUB, n_gb, gpb, SSM_CG, SSM_CG), eye)
    km = km.reshape(S5_SUB, n_gb, LANES, LANES)

    wj = pw[:S5_SUB][::-1][:, :, :, None] * b_bar[None]
    wj = jnp.transpose(wj, (0, 1, 3, 2)).reshape(S5_SUB, n_gb, gpb, SSM_CG, p)
    wre = jnp.einsum('dbgcp,gh->dbgchp', jnp.real(wj), eye).reshape(S5_SUB, n_gb, LANES, gpb * p)
    wim = jnp.einsum('dbgcp,gh->dbgchp', jnp.imag(wj), eye).reshape(S5_SUB, n_gb, LANES, gpb * p)
    wm = jnp.concatenate([wre, wim], axis=-1)

    cl = c_c[None] * pw[1:S5_SUB + 1][:, :, None, :]
    cl = jnp.transpose(cl, (0, 1, 3, 2)).reshape(S5_SUB, n_gb, gpb, p, SSM_CG)
    vre = jnp.einsum('dbgpc,gh->dbgphc', jnp.real(cl), eye).reshape(S5_SUB, n_gb, gpb * p, LANES)
    vim = jnp.einsum('dbgpc,gh->dbgphc', -jnp.imag(cl), eye).reshape(S5_SUB, n_gb, gpb * p, LANES)
    vm = jnp.concatenate([vre, vim], axis=-2)

    lam_s = pw[S5_SUB].reshape(n_gb, 1, gpb * p)
    dsk = d_skip.astype(F32).reshape(n_gb, 1, LANES)
    return (km.astype(BF16), wm.astype(BF16), vm.astype(BF16),
            jnp.real(lam_s), jnp.imag(lam_s), dsk)


def _glu_body(y_ref, w_ref, b_ref, o_ref):
    y = y_ref[...]
    gate = _dot(y.astype(BF16), w_ref[...]) + b_ref[...]
    o_ref[...] = (y * _sigmoid(gate)).astype(BF16)


def _glu(y2, w_bf, b_row, tm):
    m, d = y2.shape
    return pl.pallas_call(
        _glu_body,
        grid=(m // tm,),
        in_specs=[pl.BlockSpec((tm, d), lambda i: (i, 0)),
                  pl.BlockSpec((d, d), lambda i: (0, 0)),
                  pl.BlockSpec((1, d), lambda i: (0, 0))],
        out_specs=pl.BlockSpec((tm, d), lambda i: (i, 0)),
        out_shape=jax.ShapeDtypeStruct((m, d), BF16),
        compiler_params=_cparams(("parallel",)),
        name="s5_glu",
    )(y2, w_bf, b_row)


def _layernorm(v, g, b):
    mu = jnp.mean(v, axis=-1, keepdims=True)
    var = jnp.mean(jnp.square(v - mu), axis=-1, keepdims=True)
    return (v - mu) * lax.rsqrt(var + LN_EPS) * g + b


def _mix_body(og_ref, os_ref, wa_ref, wb_ref, x_ref, g_ref, b_ref, wr_ref, br_ref,
              h_ref, hb_ref, route_ref, *, alpha):
    mix = _dot(og_ref[...], wa_ref[...]) + _dot(os_ref[...], wb_ref[...])
    h = _layernorm(alpha * x_ref[...] + mix, g_ref[...], b_ref[...])
    h_ref[...] = h
    hb_ref[...] = h.astype(BF16)

    logits = _dot_x3(h, wr_ref[...]) + br_ref[...]
    tm = logits.shape[0]
    lane = lax.broadcasted_iota(jnp.int32, (tm, LANES), 1)
    work = jnp.where(lane < N_EXPERTS, logits, -jnp.inf)
    vals, idxs = [], []
    for _ in range(TOP_K):
        mx = jnp.max(work, axis=-1, keepdims=True)
        ix = jnp.min(jnp.where(work == mx, lane, LANES), axis=-1, keepdims=True)
        vals.append(mx)
        idxs.append(ix)
        work = jnp.where(lane == ix, -jnp.inf, work)
    exps = [jnp.exp(v - vals[0]) for v in vals]
    denom = exps[0]
    for e in exps[1:]:
        denom = denom + e
    route = jnp.zeros((tm, LANES), F32)
    for k in range(TOP_K):
        route = jnp.where(lane == k, idxs[k].astype(F32), route)
        route = jnp.where(lane == TOP_K + k, exps[k] / denom, route)
    route_ref[...] = route


def _mix_ln_route(og, osm, wa, wb, x2, g_row, b_row, wr, br, alpha, tm):
    m, d = x2.shape
    dh = og.shape[1]
    body = functools.partial(_mix_body, alpha=alpha)
    row = lambda i: (i, 0)
    fix = lambda i: (0, 0)
    return pl.pallas_call(
        body,
        grid=(m // tm,),
        in_specs=[pl.BlockSpec((tm, dh), row), pl.BlockSpec((tm, dh), row),
                  pl.BlockSpec((dh, d), fix), pl.BlockSpec((dh, d), fix),
                  pl.BlockSpec((tm, d), row),
                  pl.BlockSpec((1, d), fix), pl.BlockSpec((1, d), fix),
                  pl.BlockSpec((d, LANES), fix), pl.BlockSpec((1, LANES), fix)],
        out_specs=[pl.BlockSpec((tm, d), row), pl.BlockSpec((tm, d), row),
                   pl.BlockSpec((tm, LANES), row)],
        out_shape=[jax.ShapeDtypeStruct((m, d), F32),
                   jax.ShapeDtypeStruct((m, d), BF16),
                   jax.ShapeDtypeStruct((m, LANES), F32)],
        compiler_params=_cparams(("parallel",)),
        name="mix_ln_route",
    )(og, osm, wa, wb, x2, g_row, b_row, wr, br)


def _moe_body(te_ref, nu_ref, x_ref, gate_ref, wg_ref, bg_ref, wu_ref, bu_ref, wd_ref, bd_ref,
              o_ref, acc_ref, *, n_f):
    i = pl.program_id(0)
    f = pl.program_id(1)
    used = i < nu_ref[0]

    @pl.when(used)
    def _():
        xb = x_ref[...]
        hg = jnp.minimum(_dot(xb, wg_ref[0].astype(BF16)) + bg_ref[0], SWIGLU_LIMIT)
        hu = jnp.clip(_dot(xb, wu_ref[0].astype(BF16)) + bu_ref[0], -SWIGLU_LIMIT, SWIGLU_LIMIT)
        hh = (hu + 1.0) * (hg * _sigmoid(SWIGLU_ALPHA * hg))
        part = _dot(hh.astype(BF16), wd_ref[0].astype(BF16))

        @pl.when(f == 0)
        def _():
            acc_ref[...] = part

        @pl.when(f != 0)
        def _():
            acc_ref[...] += part

        @pl.when(f == n_f - 1)
        def _():
            o_ref[...] = (acc_ref[...] + bd_ref[0]) * gate_ref[...]

    @pl.when(jnp.logical_not(used) & (f == n_f - 1))
    def _():
        o_ref[...] = jnp.zeros_like(o_ref)


def _moe_ffn(tile_expert, n_used, x_sorted, gate_col, w_gate, b_gate, w_up, b_up, w_down, b_down, tm, tf):
    cap, d = x_sorted.shape
    n_e, _, d_ff = w_gate.shape
    n_tiles = cap // tm
    n_f = d_ff // tf
    body = functools.partial(_moe_body, n_f=n_f)

    def fcol(i, f, nu):
        return jnp.where(i < nu[0], f, n_f - 1)

    grid_spec = pltpu.PrefetchScalarGridSpec(
        num_scalar_prefetch=2,
        grid=(n_tiles, n_f),
        in_specs=[pl.BlockSpec((tm, d), lambda i, f, te, nu: (i, 0)),
                  pl.BlockSpec((tm, 1), lambda i, f, te, nu: (i, 0)),
                  pl.BlockSpec((1, d, tf), lambda i, f, te, nu: (te[i], 0, fcol(i, f, nu))),
                  pl.BlockSpec((1, 1, tf), lambda i, f, te, nu: (te[i], 0, fcol(i, f, nu))),
                  pl.BlockSpec((1, d, tf), lambda i, f, te, nu: (te[i], 0, fcol(i, f, nu))),
                  pl.BlockSpec((1, 1, tf), lambda i, f, te, nu: (te[i], 0, fcol(i, f, nu))),
                  pl.BlockSpec((1, tf, d), lambda i, f, te, nu: (te[i], fcol(i, f, nu), 0)),
                  pl.BlockSpec((1, 1, d), lambda i, f, te, nu: (te[i], 0, 0))],
        out_specs=pl.BlockSpec((tm, d), lambda i, f, te, nu: (i, 0)),
        scratch_shapes=[pltpu.VMEM((tm, d), F32)],
    )
    return pl.pallas_call(
        body,
        grid_spec=grid_spec,
        out_shape=jax.ShapeDtypeStruct((cap, d), F32),
        compiler_params=_cparams(("arbitrary", "arbitrary")),
        name="moe_ffn",
    )(tile_expert, n_used, x_sorted, gate_col, w_gate, b_gate.reshape(n_e, 1, d_ff),
      w_up, b_up.reshape(n_e, 1, d_ff), w_down, b_down.reshape(n_e, 1, d))


def _final_body(h_ref, ff_ref, g_ref, b_ref, o_ref, *, alpha):
    o_ref[...] = _layernorm(alpha * h_ref[...] + ff_ref[...], g_ref[...], b_ref[...])


def _final_ln(h, ff, g_row, b_row, alpha, tm):
    m, d = h.shape
    body = functools.partial(_final_body, alpha=alpha)
    row = lambda i: (i, 0)
    fix = lambda i: (0, 0)
    return pl.pallas_call(
        body,
        grid=(m // tm,),
        in_specs=[pl.BlockSpec((tm, d), row), pl.BlockSpec((tm, d), row),
                  pl.BlockSpec((1, d), fix), pl.BlockSpec((1, d), fix)],
        out_specs=pl.BlockSpec((tm, d), row),
        out_shape=jax.ShapeDtypeStruct((m, d), F32),
        compiler_params=_cparams(("parallel",)),
        name="final_ln",
    )(h, ff, g_row, b_row)


def _row_tile(m, pref):
    t = min(m, pref)
    while m % t:
        t //= 2
    return t


def _route_tables(route, m, tm):
    n_assign = m * TOP_K
    e_flat = route[:, :TOP_K].astype(jnp.int32).reshape(n_assign)
    g_flat = route[:, TOP_K:2 * TOP_K].reshape(n_assign)
    tok_flat = jnp.repeat(jnp.arange(m, dtype=jnp.int32), TOP_K)
    order = jnp.argsort(e_flat).astype(jnp.int32)
    inv_order = jnp.argsort(order).astype(jnp.int32)
    e_sorted = e_flat[order]
    experts = jnp.arange(N_EXPERTS, dtype=jnp.int32)
    starts = jnp.searchsorted(e_sorted, experts, side='left').astype(jnp.int32)
    counts = jnp.searchsorted(e_sorted, experts, side='right').astype(jnp.int32) - starts
    padded = (counts + tm - 1) // tm * tm
    pends = jnp.cumsum(padded)
    pstarts = pends - padded
    pos = (pstarts[e_flat] + inv_order - starts[e_flat]).reshape(m, TOP_K)
    n_tiles = -(-n_assign // tm) + N_EXPERTS
    cap = n_tiles * tm
    rows = jnp.arange(cap, dtype=jnp.int32)
    e_row = jnp.minimum(jnp.searchsorted(pends, rows, side='right'), N_EXPERTS - 1).astype(jnp.int32)
    off = rows - pstarts[e_row]
    valid = (off < counts[e_row]) & (rows < pends[-1])
    src = order[jnp.clip(starts[e_row] + off, 0, n_assign - 1)]
    row_tok = jnp.where(valid, tok_flat[src], 0)
    row_gate = jnp.where(valid, g_flat[src], 0.0)
    tile_start = jnp.arange(n_tiles, dtype=jnp.int32) * tm
    tile_expert = e_row[tile_start]
    n_used = (pends[-1] // tm).astype(jnp.int32).reshape(1)
    last_used = jnp.maximum(n_used[0] - 1, 0)
    tile_expert = jnp.where(tile_start // tm < n_used[0], tile_expert, tile_expert[last_used])
    return row_tok, row_gate, pos, tile_expert, n_used


def _layer(x, conv_hist, s_gdn, h_re, h_im, chunk, alpha, p):
    b, t, d = x.shape
    m = b * t
    d_gdn = GDN_HEADS * HEAD_DIM
    x2 = x.reshape(m, d)

    proj = _in_proj(x2, p['w_in'], _row_tile(m, 1024), p['proj_tn'])
    proj3 = proj.reshape(b, t, proj.shape[1])
    conv_new = proj3[:, t - (CONV_W - 1):, :3 * d_gdn]

    q, k, v, cols, gct = _gdn_prep(proj3, conv_hist, p['w_conv'], p['alog_row'], p['dtb_row'],
                                   chunk, p['ab_block'])
    o_gdn, s_new = _gdn(q, k, v, cols, gct, proj3, p['wn_row'], s_gdn, chunk, p['z_block'])

    n_gb = p['s5_mats'][0].shape[1]
    half = GROUPS_PER_BLOCK * SSM_P
    yg, hre_new, him_new = _s5(proj3, p['s5_mats'], h_re.reshape(b, n_gb, 1, half),
                               h_im.reshape(b, n_gb, 1, half), p['u_block0'])
    o_ssm = _glu(yg.reshape(m, yg.shape[2]), p['w_glu'], p['b_glu_row'], _row_tile(m, 512))

    h, hb, route = _mix_ln_route(o_gdn.reshape(m, d_gdn), o_ssm, p['w_out_a'], p['w_out_b'], x2,
                                 p['ln1_g'], p['ln1_b'], p['w_router'], p['b_router'], alpha, _row_tile(m, 256))

    big = m * TOP_K >= 8 * p['moe_tm']
    tm = p['moe_tm'] if big else 128
    tf = p['moe_tf'] if big else 2 * p['moe_tf']
    row_tok, row_gate, pos, tile_expert, n_used = _route_tables(route, m, tm)
    x_sorted = hb[row_tok]
    outs = _moe_ffn(tile_expert, n_used, x_sorted, row_gate[:, None], p['w_gate'], p['b_gate'],
                    p['w_up'], p['b_up'], p['w_down'], p['b_down'], tm, tf)
    ff = outs[pos[:, 0]]
    for kk in range(1, TOP_K):
        ff = ff + outs[pos[:, kk]]

    y = _final_ln(h, ff, p['ln2_g'], p['ln2_b'], alpha, _row_tile(m, 512))
    g_all = h_re.shape[1]
    return (y.reshape(b, t, d), conv_new, s_new,
            hre_new.reshape(b, g_all, SSM_P), him_new.reshape(b, g_all, SSM_P))


def _pad_lanes(v, fill=0.0):
    return jnp.pad(v.astype(F32), (0, LANES - v.shape[0]), constant_values=fill).reshape(1, LANES)


def _layer_params(l, w_in, w_conv, a_log, dt_bias, w_onorm, lam_re, lam_im, log_dt, b_re, b_im, c_re, c_im,
                  d_skip, w_glu, b_glu, w_out, ln1_g, ln1_b, w_router, b_router, w_gate, b_gate,
                  w_up, b_up, w_down, b_down, ln2_g, ln2_b):
    d_model = w_in.shape[1]
    d_gdn = GDN_HEADS * HEAD_DIM
    d_qkvz = 4 * d_gdn
    d_ssm = d_model - d_gdn
    wi = w_in[l]
    proj_tn = 896
    n_cols = d_qkvz + d_ssm + LANES
    n_pad = -(-n_cols // proj_tn) * proj_tn
    w_in_r = jnp.concatenate([wi[:, :d_qkvz], wi[:, d_qkvz + 2 * GDN_HEADS:],
                              wi[:, d_qkvz:d_qkvz + 2 * GDN_HEADS],
                              jnp.zeros((d_model, n_pad - d_qkvz - d_ssm - 2 * GDN_HEADS), wi.dtype)], axis=1)
    wo = w_out[l].astype(BF16)
    wr = jnp.pad(w_router[l].astype(F32), ((0, 0), (0, LANES - N_EXPERTS)))
    return {
        'w_in': w_in_r.astype(BF16), 'proj_tn': proj_tn,
        'z_block': 3, 'u_block0': (d_qkvz) // LANES, 'ab_block': (d_qkvz + d_ssm) // LANES,
        'w_conv': w_conv[l].astype(F32),
        'alog_row': _pad_lanes(a_log[l]), 'dtb_row': _pad_lanes(dt_bias[l]),
        'wn_row': w_onorm[l].astype(F32).reshape(1, HEAD_DIM),
        's5_mats': _s5_matrices(lam_re[l], lam_im[l], log_dt[l], b_re[l], b_im[l], c_re[l], c_im[l], d_skip[l]),
        'w_glu': w_glu[l].astype(BF16), 'b_glu_row': b_glu[l].astype(F32).reshape(1, d_ssm),
        'w_out_a': wo[:d_gdn], 'w_out_b': wo[d_gdn:],
        'ln1_g': ln1_g[l].astype(F32).reshape(1, d_model), 'ln1_b': ln1_b[l].astype(F32).reshape(1, d_model),
        'w_router': wr, 'b_router': _pad_lanes(b_router[l]),
        'w_gate': w_gate[l], 'b_gate': b_gate[l], 'w_up': w_up[l], 'b_up': b_up[l],
        'w_down': w_down[l], 'b_down': b_down[l],
        'ln2_g': ln2_g[l].astype(F32).reshape(1, d_model), 'ln2_b': ln2_b[l].astype(F32).reshape(1, d_model),
        'moe_tm': 768, 'moe_tf': 256,
    }


def kernel(x_prompt, x_sample, state_conv, state_gdn, state_ssm_re, state_ssm_im, w_in, w_conv, a_log, dt_bias, w_onorm, lam_re, lam_im, log_dt, b_re, b_im, c_re, c_im, d_skip, w_glu, b_glu, w_out, ln1_g, ln1_b, w_router, b_router, w_gate, b_gate, w_up, b_up, w_down, b_down, ln2_g, ln2_b):
    depth = w_in.shape[0]
    alpha = (2.0 * depth) ** 0.25
    bp, seq, _ = x_prompt.shape
    chunk_p = 64
    d_qkv = state_conv.shape[-1]
    n_groups, n_p = state_ssm_re.shape[-2:]
    yp, ys = x_prompt, x_sample
    outs_p = [[], [], [], []]
    outs_s = [[], [], [], []]
    for l in range(depth):
        p = _layer_params(l, w_in, w_conv, a_log, dt_bias, w_onorm, lam_re, lam_im, log_dt, b_re, b_im,
                          c_re, c_im, d_skip, w_glu, b_glu, w_out, ln1_g, ln1_b, w_router, b_router,
                          w_gate, b_gate, w_up, b_up, w_down, b_down, ln2_g, ln2_b)
        yp, cp, sp, rp, ip = _layer(
            yp, jnp.zeros((bp, CONV_W - 1, d_qkv), F32),
            jnp.zeros((bp, GDN_HEADS, HEAD_DIM, HEAD_DIM), F32),
            jnp.zeros((bp, n_groups, n_p), F32), jnp.zeros((bp, n_groups, n_p), F32),
            chunk_p, alpha, p)
        ys, cs, ss, rs, is_ = _layer(
            ys, state_conv[l].astype(F32), state_gdn[l].astype(F32),
            state_ssm_re[l].astype(F32), state_ssm_im[l].astype(F32),
            ys.shape[1], alpha, p)
        for acc, val in zip(outs_p, (cp, sp, rp, ip)):
            acc.append(val)
        for acc, val in zip(outs_s, (cs, ss, rs, is_)):
            acc.append(val)
    return (yp, ys, *[jnp.stack(a) for a in outs_p], *[jnp.stack(a) for a in outs_s])
```

```python
import functools
import math

import jax
import jax.numpy as jnp
from jax import lax
from jax.experimental import pallas as pl
from jax.experimental.pallas import tpu as pltpu

F32 = jnp.float32
BF16 = jnp.bfloat16

GDN_HEADS = 8
HEAD_DIM = 128
CONV_W = 4
SSM_CG = 16
SSM_P = 64
N_EXPERTS = 32
TOP_K = 4
SWIGLU_ALPHA = 1.702
SWIGLU_LIMIT = 7.0
LN_EPS = 1e-5
RMS_EPS = 1e-6
L2_EPS = 1e-6

LANES = 128
SUBLANES = 8
S5_SUB = 8
S5_TILE = 2048
GDN_TILE = 128
GROUPS_PER_BLOCK = LANES // SSM_CG
VMEM_LIMIT = 56 * 1024 * 1024

NT_DIMS = (((1,), (1,)), ((), ()))
TN_DIMS = (((0,), (0,)), ((), ()))


def _dot(a, b, dims=(((1,), (0,)), ((), ()))):
    return lax.dot_general(a, b, dims, preferred_element_type=F32)


def _split(a):
    hi = a.astype(BF16)
    lo = (a - hi.astype(F32)).astype(BF16)
    return hi, lo


def _dot_x3(a, b, dims=(((1,), (0,)), ((), ()))):
    ah, al = _split(a)
    bh, bl = _split(b)
    return _dot(ah, bh, dims) + (_dot(ah, bl, dims) + _dot(al, bh, dims))


def _dot_bf(a, b, dims=(((1,), (0,)), ((), ()))):
    return _dot(a.astype(BF16), b.astype(BF16), dims)


_gdn_mm = _dot_bf


def _sigmoid(x):
    return 1.0 / (1.0 + jnp.exp(-x))


def _cparams(sem):
    return pltpu.CompilerParams(dimension_semantics=sem, vmem_limit_bytes=VMEM_LIMIT)


def _proj_body(x_ref, w_ref, o_ref, xb_ref):
    @pl.when(pl.program_id(1) == 0)
    def _():
        xb_ref[...] = x_ref[...].astype(BF16)

    o_ref[...] = _dot(xb_ref[...], w_ref[...])


def _in_proj(x2, w_bf, tm, tn):
    m, k = x2.shape
    n = w_bf.shape[1]
    return pl.pallas_call(
        _proj_body,
        grid=(m // tm, n // tn),
        in_specs=[pl.BlockSpec((tm, k), lambda i, j: (i, 0)),
                  pl.BlockSpec((k, tn), lambda i, j: (0, j))],
        out_specs=pl.BlockSpec((tm, tn), lambda i, j: (i, j)),
        out_shape=jax.ShapeDtypeStruct((m, n), F32),
        scratch_shapes=[pltpu.VMEM((tm, k), BF16)],
        compiler_params=_cparams(("parallel", "arbitrary")),
        name="in_proj",
    )(x2, w_bf)


def _gdn_prep_body(qkv_ref, ab_ref, hist_ref, wc_ref, alog_ref, dtb_ref,
                   q_ref, k_ref, v_ref, cols_ref, gct_ref, xbuf_ref, *, tt, chunk):
    d_gdn = GDN_HEADS * HEAD_DIM
    halo = SUBLANES

    @pl.when(pl.program_id(1) == 0)
    def _():
        xbuf_ref[0:halo, :] = jnp.zeros((halo, 3 * d_gdn), F32)
        xbuf_ref[halo - (CONV_W - 1):halo, :] = hist_ref[0]

    xbuf_ref[halo:halo + tt, :] = qkv_ref[0]

    for part, out_ref in enumerate((q_ref, k_ref, v_ref)):
        c0 = part * d_gdn
        y = None
        for j in range(CONV_W):
            r0 = halo - (CONV_W - 1) + j
            term = xbuf_ref[r0:r0 + tt, c0:c0 + d_gdn] * wc_ref[j:j + 1, c0:c0 + d_gdn]
            y = term if y is None else y + term
        s = y * _sigmoid(y)
        if part == 2:
            out_ref[0] = s
        else:
            for h in range(GDN_HEADS):
                sh = s[:, h * HEAD_DIM:(h + 1) * HEAD_DIM]
                nrm = sh * lax.rsqrt(jnp.sum(sh * sh, axis=-1, keepdims=True) + L2_EPS)
                if part == 0:
                    nrm = nrm * (HEAD_DIM ** -0.5)
                out_ref[0, :, h * HEAD_DIM:(h + 1) * HEAD_DIM] = nrm

    xbuf_ref[0:halo, :] = xbuf_ref[tt:tt + halo, :]

    ab = ab_ref[0]
    lane = lax.broadcasted_iota(jnp.int32, (tt, LANES), 1)
    is_a = lane < GDN_HEADS
    z = ab + dtb_ref[...]
    softplus = jnp.maximum(z, 0.0) + jnp.log1p(jnp.exp(-jnp.abs(z)))
    g = jnp.where(is_a, -jnp.exp(alog_ref[...]) * softplus, 0.0)
    beta = _sigmoid(ab)

    shift = int(math.log2(chunk))
    r = lax.broadcasted_iota(jnp.int32, (tt, tt), 0)
    c = lax.broadcasted_iota(jnp.int32, (tt, tt), 1)
    same = (r >> shift) == (c >> shift)
    m_incl = jnp.where(same & (r >= c), 1.0, 0.0).astype(BF16)
    m_all = jnp.where(same, 1.0, 0.0).astype(BF16)
    gc = _dot_exact_lhs_rhs(m_incl, g)
    glast = _dot_exact_lhs_rhs(m_all, g)
    eg = jnp.exp(gc)
    egl = jnp.exp(glast - gc)
    egt = jnp.exp(glast)
    zero = jnp.zeros_like(gc)
    cols = (jnp.where(is_a, gc, zero)
            + jnp.where((lane >= 8) & (lane < 16), beta, zero)
            + pltpu.roll(jnp.where(is_a, eg, zero), 16, 1)
            + pltpu.roll(jnp.where(is_a, egl, zero), 24, 1)
            + pltpu.roll(jnp.where(is_a, egt, zero), 32, 1))
    cols_ref[0] = cols

    er = lax.broadcasted_iota(jnp.int32, (SUBLANES, LANES), 0)
    ec = lax.broadcasted_iota(jnp.int32, (SUBLANES, LANES), 1)
    sel = jnp.where(er == ec, 1.0, 0.0).astype(BF16)
    gct_ref[0] = _dot_exact_rhs(sel, jnp.where(is_a, gc, zero), NT_DIMS)


def _three_pieces(a):
    a0 = a.astype(BF16)
    r1 = a - a0.astype(F32)
    a1 = r1.astype(BF16)
    a2 = (r1 - a1.astype(F32)).astype(BF16)
    return a0, a1, a2


def _dot_exact_lhs_rhs(mask_bf, a):
    a0, a1, a2 = _three_pieces(a)
    return _dot(mask_bf, a0) + (_dot(mask_bf, a1) + _dot(mask_bf, a2))


def _dot_exact_rhs(mask_bf, a, dims):
    a0, a1, a2 = _three_pieces(a)
    return _dot(mask_bf, a0, dims) + (_dot(mask_bf, a1, dims) + _dot(mask_bf, a2, dims))


def _gdn_prep(proj3, hist, w_conv, alog_p, dtb_p, chunk, ab_block):
    b, t, _ = proj3.shape
    d_gdn = GDN_HEADS * HEAD_DIM
    tt = min(t, GDN_TILE)
    body = functools.partial(_gdn_prep_body, tt=tt, chunk=chunk)
    big = jax.ShapeDtypeStruct((b, t, d_gdn), F32)
    return pl.pallas_call(
        body,
        grid=(b, t // tt),
        in_specs=[pl.BlockSpec((1, tt, 3 * d_gdn), lambda i, j: (i, j, 0)),
                  pl.BlockSpec((1, tt, LANES), lambda i, j: (i, j, ab_block)),
                  pl.BlockSpec((1, CONV_W - 1, 3 * d_gdn), lambda i, j: (i, 0, 0)),
                  pl.BlockSpec((CONV_W, 3 * d_gdn), lambda i, j: (0, 0)),
                  pl.BlockSpec((1, LANES), lambda i, j: (0, 0)),
                  pl.BlockSpec((1, LANES), lambda i, j: (0, 0))],
        out_specs=[pl.BlockSpec((1, tt, d_gdn), lambda i, j: (i, j, 0)),
                   pl.BlockSpec((1, tt, d_gdn), lambda i, j: (i, j, 0)),
                   pl.BlockSpec((1, tt, d_gdn), lambda i, j: (i, j, 0)),
                   pl.BlockSpec((1, tt, LANES), lambda i, j: (i, j, 0)),
                   pl.BlockSpec((1, SUBLANES, tt), lambda i, j: (i, 0, j))],
        out_shape=[big, big, big,
                   jax.ShapeDtypeStruct((b, t, LANES), F32),
                   jax.ShapeDtypeStruct((b, SUBLANES, t), F32)],
        scratch_shapes=[pltpu.VMEM((tt + SUBLANES, 3 * d_gdn), F32)],
        compiler_params=_cparams(("parallel", "arbitrary")),
        name="gdn_prep",
    )(proj3, proj3, hist, w_conv, alog_p, dtb_p)


def _unit_lower_inverses(mats, r, c, chunk):
    base = 16
    eye = jnp.where(r == c, 1.0, 0.0)

    def blk(bs):
        s = int(math.log2(bs))
        return (r >> s) == (c >> s)

    d1 = [jnp.where(blk(base), a, 0.0) for a in mats]
    d2 = [_gdn_mm(x, x) for x in d1]
    d4 = [_gdn_mm(x, x) for x in d2]
    d8 = [_gdn_mm(x, x) for x in d4]
    t = [eye - x for x in d1]
    t = [x + _gdn_mm(x, y) for x, y in zip(t, d2)]
    t = [x + _gdn_mm(x, y) for x, y in zip(t, d4)]
    t = [x + _gdn_mm(x, y) for x, y in zip(t, d8)]
    bs = base
    while bs < chunk:
        off_mask = blk(2 * bs) & jnp.logical_not(blk(bs))
        inner = [_gdn_mm(jnp.where(off_mask, a, 0.0), x) for a, x in zip(mats, t)]
        t = [x - _gdn_mm(x, y) for x, y in zip(t, inner)]
        bs *= 2
    return t


def _gdn_body(q_ref, k_ref, v_ref, cols_ref, gct_ref, z_ref, wn_ref, s0_ref,
              o_ref, s_ref, *, tb, chunk):
    @pl.when(pl.program_id(1) == 0)
    def _():
        s_ref[...] = s0_ref[...]

    shift = int(math.log2(chunk))
    r = lax.broadcasted_iota(jnp.int32, (tb, tb), 0)
    c = lax.broadcasted_iota(jnp.int32, (tb, tb), 1)
    same = (r >> shift) == (c >> shift)
    incl = same & (r >= c)
    strict = same & (r > c)
    n_chunks = tb // chunk

    heads = range(GDN_HEADS)
    hsl = [slice(h * HEAD_DIM, (h + 1) * HEAD_DIM) for h in heads]
    q = [q_ref[0, :, s] for s in hsl]
    k = [k_ref[0, :, s] for s in hsl]
    beta = [cols_ref[0, :, 8 + h:9 + h] for h in heads]
    eg = [cols_ref[0, :, 16 + h:17 + h] for h in heads]
    egl = [cols_ref[0, :, 24 + h:25 + h] for h in heads]
    decay = [jnp.exp(jnp.where(incl, cols_ref[0, :, h:h + 1] - gct_ref[0, h:h + 1, :], -jnp.inf)) for h in heads]
    kb = [k[h] * beta[h] for h in heads]
    k_bf = [x.astype(BF16) for x in k]
    a = [jnp.where(strict, _dot(kb[h].astype(BF16), k_bf[h], NT_DIMS) * decay[h], 0.0) for h in heads]
    tinv = _unit_lower_inverses(a, r, c, chunk)
    sol = [_gdn_mm(tinv[h], jnp.concatenate([v_ref[0, :, hsl[h]] * beta[h], kb[h] * eg[h]], axis=1)) for h in heads]
    u = [x[:, :HEAD_DIM] for x in sol]
    w_bf = [x[:, HEAD_DIM:].astype(BF16) for x in sol]
    attn = [(_dot(q[h].astype(BF16), k_bf[h], NT_DIMS) * decay[h]).astype(BF16) for h in heads]
    q_dec = [(q[h] * eg[h]).astype(BF16) for h in heads]
    k_dec = [(k[h] * egl[h]).astype(BF16) for h in heads]

    s = [s_ref[0, h] for h in heads]
    v_new = [[] for _ in heads]
    o_state = [[] for _ in heads]
    for ci in range(n_chunks):
        rs = slice(ci * chunk, (ci + 1) * chunk)
        s_bf = [x.astype(BF16) for x in s]
        vn = [u[h][rs] - _dot(w_bf[h][rs], s_bf[h]) for h in heads]
        for h in heads:
            o_state[h].append(_dot(q_dec[h][rs], s_bf[h]))
            v_new[h].append(vn[h])
        s = [s[h] * cols_ref[0, ci * chunk:ci * chunk + 1, 32 + h:33 + h]
             + _dot(k_dec[h][rs], vn[h].astype(BF16), TN_DIMS) for h in heads]
    for h in heads:
        s_ref[0, h] = s[h]

    def cat(parts):
        return parts[0] if len(parts) == 1 else jnp.concatenate(parts, axis=0)

    for h in heads:
        o = cat(o_state[h]) + _dot(attn[h], cat(v_new[h]).astype(BF16))
        zh = z_ref[0, :, hsl[h]]
        o = (o * lax.rsqrt(jnp.mean(o * o, axis=-1, keepdims=True) + RMS_EPS) * wn_ref[...]
             * (zh * _sigmoid(zh)))
        o_ref[0, :, hsl[h]] = o.astype(BF16)


def _gdn(q, k, v, cols, gct, proj3, wn, s0, chunk, z_block):
    b, t, d_gdn = q.shape
    tb = min(t, GDN_TILE)
    body = functools.partial(_gdn_body, tb=tb, chunk=chunk)
    tile = lambda i, j: (i, j, 0)
    return pl.pallas_call(
        body,
        grid=(b, t // tb),
        in_specs=[pl.BlockSpec((1, tb, d_gdn), tile),
                  pl.BlockSpec((1, tb, d_gdn), tile),
                  pl.BlockSpec((1, tb, d_gdn), tile),
                  pl.BlockSpec((1, tb, LANES), tile),
                  pl.BlockSpec((1, SUBLANES, tb), lambda i, j: (i, 0, j)),
                  pl.BlockSpec((1, tb, d_gdn), lambda i, j: (i, j, z_block)),
                  pl.BlockSpec((1, HEAD_DIM), lambda i, j: (0, 0)),
                  pl.BlockSpec((1, GDN_HEADS, HEAD_DIM, HEAD_DIM), lambda i, j: (i, 0, 0, 0))],
        out_specs=[pl.BlockSpec((1, tb, d_gdn), tile),
                   pl.BlockSpec((1, GDN_HEADS, HEAD_DIM, HEAD_DIM), lambda i, j: (i, 0, 0, 0))],
        out_shape=[jax.ShapeDtypeStruct((b, t, d_gdn), BF16),
                   jax.ShapeDtypeStruct((b, GDN_HEADS, HEAD_DIM, HEAD_DIM), F32)],
        compiler_params=_cparams(("parallel", "arbitrary")),
        name="gdn_delta",
    )(q, k, v, cols, gct, proj3, wn, s0)


def _gelu_tanh(x):
    return 0.5 * x * (1.0 + jnp.tanh(math.sqrt(2.0 / math.pi) * (x + 0.044715 * (x * x * x))))


def _s5_body(u_ref, km_ref, wm_ref, vm_ref, lre_ref, lim_ref, dsk_ref, h0re_ref, h0im_ref,
             y_ref, hre_ref, him_ref, xbuf_ref, hbuf_ref, *, n_sub):
    half = GROUPS_PER_BLOCK * SSM_P

    @pl.when(pl.program_id(2) == 0)
    def _():
        hre_ref[...] = h0re_ref[...]
        him_ref[...] = h0im_ref[...]

    u_f = [u_ref[0, pl.ds(j, n_sub, stride=S5_SUB), :] for j in range(S5_SUB)]
    u_b = [x.astype(BF16) for x in u_f]

    x = _dot(u_b[0], wm_ref[0, 0])
    for j in range(1, S5_SUB):
        x = x + _dot(u_b[j], wm_ref[j, 0])
    xbuf_ref[...] = x

    lre = lre_ref[0]
    lim = lim_ref[0]

    def step(n, carry):
        hre, him = carry
        hbuf_ref[pl.ds(n, 1), 0:half] = hre
        hbuf_ref[pl.ds(n, 1), half:2 * half] = him
        xr = xbuf_ref[pl.ds(n, 1), 0:half]
        xi = xbuf_ref[pl.ds(n, 1), half:2 * half]
        return (lre * hre - lim * him + xr, lre * him + lim * hre + xi)

    hre, him = lax.fori_loop(0, n_sub, step, (hre_ref[0, 0], him_ref[0, 0]))
    hre_ref[0, 0] = hre
    him_ref[0, 0] = him

    h_b = hbuf_ref[...].astype(BF16)
    dsk = dsk_ref[0]
    for l in range(S5_SUB):
        y = _dot(h_b, vm_ref[l, 0])
        for d in range(l + 1):
            y = y + _dot(u_b[l - d], km_ref[d, 0])
        y = y + dsk * u_f[l]
        y_ref[0, pl.ds(l, n_sub, stride=S5_SUB), :] = _gelu_tanh(y)


def _s5(proj3, mats, h0re, h0im, u_block0):
    km, wm, vm, lre, lim, dsk = mats
    b, t, _ = proj3.shape
    n_gb = km.shape[1]
    tt = min(t, S5_TILE)
    n_sub = tt // S5_SUB
    half = GROUPS_PER_BLOCK * SSM_P
    body = functools.partial(_s5_body, n_sub=n_sub)
    state_spec = pl.BlockSpec((1, 1, 1, half), lambda g, i, j: (i, g, 0, 0))
    par_spec = pl.BlockSpec((1, 1, half), lambda g, i, j: (g, 0, 0))
    return pl.pallas_call(
        body,
        grid=(n_gb, b, t // tt),
        in_specs=[pl.BlockSpec((1, tt, LANES), lambda g, i, j: (i, j, u_block0 + g)),
                  pl.BlockSpec((S5_SUB, 1, LANES, LANES), lambda g, i, j: (0, g, 0, 0)),
                  pl.BlockSpec((S5_SUB, 1, LANES, 2 * half), lambda g, i, j: (0, g, 0, 0)),
                  pl.BlockSpec((S5_SUB, 1, 2 * half, LANES), lambda g, i, j: (0, g, 0, 0)),
                  par_spec, par_spec,
                  pl.BlockSpec((1, 1, LANES), lambda g, i, j: (g, 0, 0)),
                  state_spec, state_spec],
        out_specs=[pl.BlockSpec((1, tt, LANES), lambda g, i, j: (i, j, g)),
                   state_spec, state_spec],
        out_shape=[jax.ShapeDtypeStruct((b, t, n_gb * LANES), F32),
                   jax.ShapeDtypeStruct((b, n_gb, 1, half), F32),
                   jax.ShapeDtypeStruct((b, n_gb, 1, half), F32)],
        scratch_shapes=[pltpu.VMEM((n_sub, 2 * half), F32),
                        pltpu.VMEM((n_sub, 2 * half), F32)],
        compiler_params=_cparams(("parallel", "parallel", "arbitrary")),
        name="s5_scan",
    )(proj3, km, wm, vm, lre, lim, dsk, h0re, h0im)


def _s5_matrices(lam_re, lam_im, log_dt, b_re, b_im, c_re, c_im, d_skip):
    g, p = lam_re.shape
    n_gb = g // GROUPS_PER_BLOCK
    gpb = GROUPS_PER_BLOCK
    dt = jnp.exp(log_dt.astype(F32))
    lam = lax.complex(jnp.minimum(lam_re.astype(F32), -1e-4), lam_im.astype(F32))
    lam_bar = jnp.exp(lam * dt[:, None])
    b_bar = ((lam_bar - 1.0) / lam)[..., None] * lax.complex(b_re.astype(F32), b_im.astype(F32))
    c_c = lax.complex(c_re.astype(F32), c_im.astype(F32))
    pows = [jnp.ones_like(lam_bar)]
    for _ in range(S5_SUB):
        pows.append(pows[-1] * lam_bar)
    pw = jnp.stack(pows)
    eye = jnp.eye(gpb, dtype=F32)

    kd = jnp.real(jnp.einsum('gop,dgp,gpi->dgio', c_c, pw[:S5_SUB], b_bar))
    km = jnp.einsum('dbgio,gh->dbgiho', kd.reshape(S5_SUB, n_gb, gpb, SSM_CG, SSM_CG), eye)
    km = km.reshape(S5_SUB, n_gb, LANES, LANES)

    wj = pw[:S5_SUB][::-1][:, :, :, None] * b_bar[None]
    wj = jnp.transpose(wj, (0, 1, 3, 2)).reshape(S5_SUB, n_gb, gpb, SSM_CG, p)
    wre = jnp.einsum('dbgcp,gh->dbgchp', jnp.real(wj), eye).reshape(S5_SUB, n_gb, LANES, gpb * p)
    wim = jnp.einsum('dbgcp,gh->dbgchp', jnp.imag(wj), eye).reshape(S5_SUB, n_gb, LANES, gpb * p)
    wm = jnp.concatenate([wre, wim], axis=-1)

    cl = c_c[None] * pw[1:S5_SUB + 1][:, :, None, :]
    cl = jnp.transpose(cl, (0, 1, 3, 2)).reshape(S5_SUB, n_gb, gpb, p, SSM_CG)
    vre = jnp.einsum('dbgpc,gh->dbgphc', jnp.real(cl), eye).reshape(S5_SUB, n_gb, gpb * p, LANES)
    vim = jnp.einsum('dbgpc,gh->dbgphc', -jnp.imag(cl), eye).reshape(S5_SUB, n_gb, gpb * p, LANES)
    vm = jnp.concatenate([vre, vim], axis=-2)

    lam_s = pw[S5_SUB].reshape(n_gb, 1, gpb * p)
    dsk = d_skip.astype(F32).reshape(n_gb, 1, LANES)
    return (km.astype(BF16), wm.astype(BF16), vm.astype(BF16),
            jnp.real(lam_s), jnp.imag(lam_s), dsk)


def _glu_body(y_ref, w_ref, b_ref, o_ref):
    y = y_ref[...]
    gate = _dot(y.astype(BF16), w_ref[...]) + b_ref[...]
    o_ref[...] = (y * _sigmoid(gate)).astype(BF16)


def _glu(y2, w_bf, b_row, tm):
    m, d = y2.shape
    return pl.pallas_call(
        _glu_body,
        grid=(m // tm,),
        in_specs=[pl.BlockSpec((tm, d), lambda i: (i, 0)),
                  pl.BlockSpec((d, d), lambda i: (0, 0)),
                  pl.BlockSpec((1, d), lambda i: (0, 0))],
        out_specs=pl.BlockSpec((tm, d), lambda i: (i, 0)),
        out_shape=jax.ShapeDtypeStruct((m, d), BF16),
        compiler_params=_cparams(("parallel",)),
        name="s5_glu",
    )(y2, w_bf, b_row)


def _layernorm(v, g, b):
    mu = jnp.mean(v, axis=-1, keepdims=True)
    var = jnp.mean(jnp.square(v - mu), axis=-1, keepdims=True)
    return (v - mu) * lax.rsqrt(var + LN_EPS) * g + b


def _mix_body(og_ref, os_ref, wa_ref, wb_ref, x_ref, g_ref, b_ref, wr_ref, br_ref,
              h_ref, hb_ref, route_ref, cnt_ref, *, alpha):
    @pl.when(pl.program_id(0) == 0)
    def _():
        cnt_ref[...] = jnp.zeros_like(cnt_ref)

    mix = _dot(og_ref[...], wa_ref[...]) + _dot(os_ref[...], wb_ref[...])
    h = _layernorm(alpha * x_ref[...] + mix, g_ref[...], b_ref[...])
    h_ref[...] = h
    hb_ref[...] = h.astype(BF16)

    logits = _dot_x3(h, wr_ref[...]) + br_ref[...]
    tm = logits.shape[0]
    lane = lax.broadcasted_iota(jnp.int32, (tm, LANES), 1)
    work = jnp.where(lane < N_EXPERTS, logits, -jnp.inf)
    vals, idxs = [], []
    for _ in range(TOP_K):
        mx = jnp.max(work, axis=-1, keepdims=True)
        ix = jnp.min(jnp.where(work == mx, lane, LANES), axis=-1, keepdims=True)
        vals.append(mx)
        idxs.append(ix)
        work = jnp.where(lane == ix, -jnp.inf, work)
    exps = [jnp.exp(v - vals[0]) for v in vals]
    denom = exps[0]
    for e in exps[1:]:
        denom = denom + e
    chosen = jnp.zeros((tm, LANES), F32)
    for k in range(TOP_K):
        chosen = jnp.where(lane == idxs[k], 1.0, chosen)
    rr = lax.broadcasted_iota(jnp.int32, (tm, tm), 0)
    cc = lax.broadcasted_iota(jnp.int32, (tm, tm), 1)
    lower = jnp.where(rr > cc, 1.0, 0.0).astype(BF16)
    before = cnt_ref[...] + _dot(lower, chosen.astype(BF16))
    cnt_ref[...] = cnt_ref[...] + jnp.sum(chosen, axis=0, keepdims=True)

    route = jnp.zeros((tm, LANES), F32)
    for k in range(TOP_K):
        rank = jnp.sum(jnp.where(lane == idxs[k], before, 0.0), axis=-1, keepdims=True)
        route = jnp.where(lane == k, idxs[k].astype(F32), route)
        route = jnp.where(lane == TOP_K + k, exps[k] / denom, route)
        route = jnp.where(lane == 2 * TOP_K + k, rank, route)
    route_ref[...] = route


def _mix_ln_route(og, osm, wa, wb, x2, g_row, b_row, wr, br, alpha, tm):
    m, d = x2.shape
    dh = og.shape[1]
    body = functools.partial(_mix_body, alpha=alpha)
    row = lambda i: (i, 0)
    fix = lambda i: (0, 0)
    return pl.pallas_call(
        body,
        grid=(m // tm,),
        in_specs=[pl.BlockSpec((tm, dh), row), pl.BlockSpec((tm, dh), row),
                  pl.BlockSpec((dh, d), fix), pl.BlockSpec((dh, d), fix),
                  pl.BlockSpec((tm, d), row),
                  pl.BlockSpec((1, d), fix), pl.BlockSpec((1, d), fix),
                  pl.BlockSpec((d, LANES), fix), pl.BlockSpec((1, LANES), fix)],
        out_specs=[pl.BlockSpec((tm, d), row), pl.BlockSpec((tm, d), row),
                   pl.BlockSpec((tm, LANES), row), pl.BlockSpec((1, LANES), fix)],
        out_shape=[jax.ShapeDtypeStruct((m, d), F32),
                   jax.ShapeDtypeStruct((m, d), BF16),
                   jax.ShapeDtypeStruct((m, LANES), F32),
                   jax.ShapeDtypeStruct((1, LANES), F32)],
        compiler_params=_cparams(("arbitrary",)),
        name="mix_ln_route",
    )(og, osm, wa, wb, x2, g_row, b_row, wr, br)


def _moe_body(te_ref, nu_ref, x_ref, wg_ref, bg_ref, wu_ref, bu_ref, wd_ref, bd_ref,
              o_ref, acc_ref, *, n_f):
    i = pl.program_id(0)
    f = pl.program_id(1)
    used = i < nu_ref[0]

    @pl.when(used)
    def _():
        xb = x_ref[...]
        hg = jnp.minimum(_dot(xb, wg_ref[0].astype(BF16)) + bg_ref[0], SWIGLU_LIMIT)
        hu = jnp.clip(_dot(xb, wu_ref[0].astype(BF16)) + bu_ref[0], -SWIGLU_LIMIT, SWIGLU_LIMIT)
        hh = (hu + 1.0) * (hg * _sigmoid(SWIGLU_ALPHA * hg))
        part = _dot(hh.astype(BF16), wd_ref[0].astype(BF16))

        @pl.when(f == 0)
        def _():
            acc_ref[...] = part

        @pl.when(f != 0)
        def _():
            acc_ref[...] += part

        @pl.when(f == n_f - 1)
        def _():
            o_ref[...] = (acc_ref[...] + bd_ref[0]).astype(BF16)

    @pl.when(jnp.logical_not(used) & (f == n_f - 1))
    def _():
        o_ref[...] = jnp.zeros_like(o_ref)


def _moe_ffn(tile_expert, n_used, x_sorted, w_gate, b_gate, w_up, b_up, w_down, b_down, tm, tf):
    cap, d = x_sorted.shape
    n_e, _, d_ff = w_gate.shape
    n_tiles = cap // tm
    n_f = d_ff // tf
    body = functools.partial(_moe_body, n_f=n_f)

    def fcol(i, f, nu):
        return jnp.where(i < nu[0], f, n_f - 1)

    grid_spec = pltpu.PrefetchScalarGridSpec(
        num_scalar_prefetch=2,
        grid=(n_tiles, n_f),
        in_specs=[pl.BlockSpec((tm, d), lambda i, f, te, nu: (i, 0)),
                  pl.BlockSpec((1, d, tf), lambda i, f, te, nu: (te[i], 0, fcol(i, f, nu))),
                  pl.BlockSpec((1, 1, tf), lambda i, f, te, nu: (te[i], 0, fcol(i, f, nu))),
                  pl.BlockSpec((1, d, tf), lambda i, f, te, nu: (te[i], 0, fcol(i, f, nu))),
                  pl.BlockSpec((1, 1, tf), lambda i, f, te, nu: (te[i], 0, fcol(i, f, nu))),
                  pl.BlockSpec((1, tf, d), lambda i, f, te, nu: (te[i], fcol(i, f, nu), 0)),
                  pl.BlockSpec((1, 1, d), lambda i, f, te, nu: (te[i], 0, 0))],
        out_specs=pl.BlockSpec((tm, d), lambda i, f, te, nu: (i, 0)),
        scratch_shapes=[pltpu.VMEM((tm, d), F32)],
    )
    return pl.pallas_call(
        body,
        grid_spec=grid_spec,
        out_shape=jax.ShapeDtypeStruct((cap, d), BF16),
        compiler_params=_cparams(("arbitrary", "arbitrary")),
        name="moe_ffn",
    )(tile_expert, n_used, x_sorted, w_gate, b_gate.reshape(n_e, 1, d_ff),
      w_up, b_up.reshape(n_e, 1, d_ff), w_down, b_down.reshape(n_e, 1, d))


def _final_body(h_ref, ex_ref, route_ref, g_ref, b_ref, o_ref, *, alpha):
    d = h_ref.shape[1]
    ff = None
    for k in range(TOP_K):
        term = ex_ref[:, k * d:(k + 1) * d].astype(F32) * route_ref[:, TOP_K + k:TOP_K + k + 1]
        ff = term if ff is None else ff + term
    o_ref[...] = _layernorm(alpha * h_ref[...] + ff, g_ref[...], b_ref[...])


def _final_ln(h, ex, route, g_row, b_row, alpha, tm):
    m, d = h.shape
    body = functools.partial(_final_body, alpha=alpha)
    row = lambda i: (i, 0)
    fix = lambda i: (0, 0)
    return pl.pallas_call(
        body,
        grid=(m // tm,),
        in_specs=[pl.BlockSpec((tm, d), row), pl.BlockSpec((tm, TOP_K * d), row),
                  pl.BlockSpec((tm, LANES), row),
                  pl.BlockSpec((1, d), fix), pl.BlockSpec((1, d), fix)],
        out_specs=pl.BlockSpec((tm, d), row),
        out_shape=jax.ShapeDtypeStruct((m, d), F32),
        compiler_params=_cparams(("parallel",)),
        name="final_ln",
    )(h, ex, route, g_row, b_row)


def _row_tile(m, pref):
    t = min(m, pref)
    while m % t:
        t //= 2
    return t


def _route_tables(route, counts_row, m, tm):
    n_assign = m * TOP_K
    e_tok = route[:, :TOP_K].astype(jnp.int32)
    rank = route[:, 2 * TOP_K:3 * TOP_K].astype(jnp.int32)
    counts = counts_row[0, :N_EXPERTS].astype(jnp.int32)
    padded = (counts + tm - 1) // tm * tm
    pends = jnp.cumsum(padded)
    pstarts = pends - padded
    pos = pstarts[e_tok] + rank
    n_tiles = -(-n_assign // tm) + N_EXPERTS
    cap = n_tiles * tm
    tok = jnp.broadcast_to(jnp.arange(m, dtype=jnp.int32)[:, None], (m, TOP_K))
    row_tok = jnp.zeros((cap,), jnp.int32).at[pos.reshape(n_assign)].set(
        tok.reshape(n_assign), unique_indices=True)
    tile_start = jnp.arange(n_tiles, dtype=jnp.int32) * tm
    tile_expert = jnp.minimum(jnp.searchsorted(pends, tile_start, side='right'), N_EXPERTS - 1).astype(jnp.int32)
    n_used = (pends[-1] // tm).astype(jnp.int32).reshape(1)
    last_used = jnp.maximum(n_used[0] - 1, 0)
    tile_expert = jnp.where(tile_start // tm < n_used[0], tile_expert, tile_expert[last_used])
    return row_tok, pos, tile_expert, n_used


def _layer(x, conv_hist, s_gdn, h_re, h_im, chunk, alpha, p):
    b, t, d = x.shape
    m = b * t
    d_gdn = GDN_HEADS * HEAD_DIM
    x2 = x.reshape(m, d)

    proj = _in_proj(x2, p['w_in'], _row_tile(m, 1024), p['proj_tn'])
    proj3 = proj.reshape(b, t, proj.shape[1])
    conv_new = proj3[:, t - (CONV_W - 1):, :3 * d_gdn]

    q, k, v, cols, gct = _gdn_prep(proj3, conv_hist, p['w_conv'], p['alog_row'], p['dtb_row'],
                                   chunk, p['ab_block'])
    o_gdn, s_new = _gdn(q, k, v, cols, gct, proj3, p['wn_row'], s_gdn, chunk, p['z_block'])

    n_gb = p['s5_mats'][0].shape[1]
    half = GROUPS_PER_BLOCK * SSM_P
    yg, hre_new, him_new = _s5(proj3, p['s5_mats'], h_re.reshape(b, n_gb, 1, half),
                               h_im.reshape(b, n_gb, 1, half), p['u_block0'])
    o_ssm = _glu(yg.reshape(m, yg.shape[2]), p['w_glu'], p['b_glu_row'], _row_tile(m, 512))

    h, hb, route, counts_row = _mix_ln_route(o_gdn.reshape(m, d_gdn), o_ssm, p['w_out_a'], p['w_out_b'], x2,
                                             p['ln1_g'], p['ln1_b'], p['w_router'], p['b_router'], alpha,
                                             _row_tile(m, 256))

    big = m * TOP_K >= 8 * p['moe_tm']
    tm = p['moe_tm'] if big else 128
    row_tok, pos, tile_expert, n_used = _route_tables(route, counts_row, m, tm)
    x_sorted = hb[row_tok]
    outs = _moe_ffn(tile_expert, n_used, x_sorted, p['w_gate'], p['b_gate'],
                    p['w_up'], p['b_up'], p['w_down'], p['b_down'], tm, p['moe_tf'])
    ex = outs[pos.reshape(m * TOP_K)].reshape(m, TOP_K * d)

    y = _final_ln(h, ex, route, p['ln2_g'], p['ln2_b'], alpha, _row_tile(m, 256))
    g_all = h_re.shape[1]
    return (y.reshape(b, t, d), conv_new, s_new,
            hre_new.reshape(b, g_all, SSM_P), him_new.reshape(b, g_all, SSM_P))


def _pad_lanes(v, fill=0.0):
    return jnp.pad(v.astype(F32), (0, LANES - v.shape[0]), constant_values=fill).reshape(1, LANES)


def _layer_params(l, w_in, w_conv, a_log, dt_bias, w_onorm, lam_re, lam_im, log_dt, b_re, b_im, c_re, c_im,
                  d_skip, w_glu, b_glu, w_out, ln1_g, ln1_b, w_router, b_router, w_gate, b_gate,
                  w_up, b_up, w_down, b_down, ln2_g, ln2_b):
    d_model = w_in.shape[1]
    d_gdn = GDN_HEADS * HEAD_DIM
    d_qkvz = 4 * d_gdn
    d_ssm = d_model - d_gdn
    wi = w_in[l]
    proj_tn = 896
    n_cols = d_qkvz + d_ssm + LANES
    n_pad = -(-n_cols // proj_tn) * proj_tn
    w_in_r = jnp.concatenate([wi[:, :d_qkvz], wi[:, d_qkvz + 2 * GDN_HEADS:],
                              wi[:, d_qkvz:d_qkvz + 2 * GDN_HEADS],
                              jnp.zeros((d_model, n_pad - d_qkvz - d_ssm - 2 * GDN_HEADS), wi.dtype)], axis=1)
    wo = w_out[l].astype(BF16)
    wr = jnp.pad(w_router[l].astype(F32), ((0, 0), (0, LANES - N_EXPERTS)))
    return {
        'w_in': w_in_r.astype(BF16), 'proj_tn': proj_tn,
        'z_block': 3, 'u_block0': (d_qkvz) // LANES, 'ab_block': (d_qkvz + d_ssm) // LANES,
        'w_conv': w_conv[l].astype(F32),
        'alog_row': _pad_lanes(a_log[l]), 'dtb_row': _pad_lanes(dt_bias[l]),
        'wn_row': w_onorm[l].astype(F32).reshape(1, HEAD_DIM),
        's5_mats': _s5_matrices(lam_re[l], lam_im[l], log_dt[l], b_re[l], b_im[l], c_re[l], c_im[l], d_skip[l]),
        'w_glu': w_glu[l].astype(BF16), 'b_glu_row': b_glu[l].astype(F32).reshape(1, d_ssm),
        'w_out_a': wo[:d_gdn], 'w_out_b': wo[d_gdn:],
        'ln1_g': ln1_g[l].astype(F32).reshape(1, d_model), 'ln1_b': ln1_b[l].astype(F32).reshape(1, d_model),
        'w_router': wr, 'b_router': _pad_lanes(b_router[l]),
        'w_gate': w_gate[l], 'b_gate': b_gate[l], 'w_up': w_up[l], 'b_up': b_up[l],
        'w_down': w_down[l], 'b_down': b_down[l],
        'ln2_g': ln2_g[l].astype(F32).reshape(1, d_model), 'ln2_b': ln2_b[l].astype(F32).reshape(1, d_model),
        'moe_tm': 512, 'moe_tf': 512,
    }


def kernel(x_prompt, x_sample, state_conv, state_gdn, state_ssm_re, state_ssm_im, w_in, w_conv, a_log, dt_bias, w_onorm, lam_re, lam_im, log_dt, b_re, b_im, c_re, c_im, d_skip, w_glu, b_glu, w_out, ln1_g, ln1_b, w_router, b_router, w_gate, b_gate, w_up, b_up, w_down, b_down, ln2_g, ln2_b):
    depth = w_in.shape[0]
    alpha = (2.0 * depth) ** 0.25
    bp, seq, _ = x_prompt.shape
    chunk_p = 64
    d_qkv = state_conv.shape[-1]
    n_groups, n_p = state_ssm_re.shape[-2:]
    yp, ys = x_prompt, x_sample
    outs_p = [[], [], [], []]
    outs_s = [[], [], [], []]
    for l in range(depth):
        p = _layer_params(l, w_in, w_conv, a_log, dt_bias, w_onorm, lam_re, lam_im, log_dt, b_re, b_im,
                          c_re, c_im, d_skip, w_glu, b_glu, w_out, ln1_g, ln1_b, w_router, b_router,
                          w_gate, b_gate, w_up, b_up, w_down, b_down, ln2_g, ln2_b)
        yp, cp, sp, rp, ip = _layer(
            yp, jnp.zeros((bp, CONV_W - 1, d_qkv), F32),
            jnp.zeros((bp, GDN_HEADS, HEAD_DIM, HEAD_DIM), F32),
            jnp.zeros((bp, n_groups, n_p), F32), jnp.zeros((bp, n_groups, n_p), F32),
            chunk_p, alpha, p)
        ys, cs, ss, rs, is_ = _layer(
            ys, state_conv[l].astype(F32), state_gdn[l].astype(F32),
            state_ssm_re[l].astype(F32), state_ssm_im[l].astype(F32),
            ys.shape[1], alpha, p)
        for acc, val in zip(outs_p, (cp, sp, rp, ip)):
            acc.append(val)
        for acc, val in zip(outs_s, (cs, ss, rs, is_)):
            acc.append(val)
    return (yp, ys, *[jnp.stack(a) for a in outs_p], *[jnp.stack(a) for a in outs_s])
```

```python
import functools
import math

import jax
import jax.numpy as jnp
from jax import lax
from jax.experimental import pallas as pl
from jax.experimental.pallas import tpu as pltpu

F32 = jnp.float32
BF16 = jnp.bfloat16

GDN_HEADS = 8
HEAD_DIM = 128
CONV_W = 4
SSM_CG = 16
SSM_P = 64
N_EXPERTS = 32
TOP_K = 4
SWIGLU_ALPHA = 1.702
SWIGLU_LIMIT = 7.0
LN_EPS = 1e-5
RMS_EPS = 1e-6
L2_EPS = 1e-6

LANES = 128
SUBLANES = 8
S5_SUB = 8
S5_TILE = 2048
GDN_TILE = 128
GROUPS_PER_BLOCK = LANES // SSM_CG
VMEM_LIMIT = 56 * 1024 * 1024

NT_DIMS = (((1,), (1,)), ((), ()))
TN_DIMS = (((0,), (0,)), ((), ()))


def _dot(a, b, dims=(((1,), (0,)), ((), ()))):
    return lax.dot_general(a, b, dims, preferred_element_type=F32)


def _split(a):
    hi = a.astype(BF16)
    lo = (a - hi.astype(F32)).astype(BF16)
    return hi, lo


def _dot_x3(a, b, dims=(((1,), (0,)), ((), ()))):
    ah, al = _split(a)
    bh, bl = _split(b)
    return _dot(ah, bh, dims) + (_dot(ah, bl, dims) + _dot(al, bh, dims))


def _dot_bf(a, b, dims=(((1,), (0,)), ((), ()))):
    return _dot(a.astype(BF16), b.astype(BF16), dims)


_gdn_mm = _dot_bf


def _sigmoid(x):
    return 1.0 / (1.0 + jnp.exp(-x))


def _cparams(sem):
    return pltpu.CompilerParams(dimension_semantics=sem, vmem_limit_bytes=VMEM_LIMIT)


def _proj_body(x_ref, w_ref, o_ref, xb_ref):
    @pl.when(pl.program_id(1) == 0)
    def _():
        xb_ref[...] = x_ref[...].astype(BF16)

    o_ref[...] = _dot(xb_ref[...], w_ref[...])


def _in_proj(x2, w_bf, tm, tn):
    m, k = x2.shape
    n = w_bf.shape[1]
    return pl.pallas_call(
        _proj_body,
        grid=(m // tm, n // tn),
        in_specs=[pl.BlockSpec((tm, k), lambda i, j: (i, 0)),
                  pl.BlockSpec((k, tn), lambda i, j: (0, j))],
        out_specs=pl.BlockSpec((tm, tn), lambda i, j: (i, j)),
        out_shape=jax.ShapeDtypeStruct((m, n), F32),
        scratch_shapes=[pltpu.VMEM((tm, k), BF16)],
        compiler_params=_cparams(("parallel", "arbitrary")),
        name="in_proj",
    )(x2, w_bf)


def _gdn_prep_body(qkv_ref, ab_ref, hist_ref, wc_ref, alog_ref, dtb_ref,
                   q_ref, k_ref, v_ref, cols_ref, gct_ref, xbuf_ref, *, tt, chunk):
    d_gdn = GDN_HEADS * HEAD_DIM
    halo = SUBLANES

    @pl.when(pl.program_id(1) == 0)
    def _():
        xbuf_ref[0:halo, :] = jnp.zeros((halo, 3 * d_gdn), F32)
        xbuf_ref[halo - (CONV_W - 1):halo, :] = hist_ref[0]

    xbuf_ref[halo:halo + tt, :] = qkv_ref[0]

    for part, out_ref in enumerate((q_ref, k_ref, v_ref)):
        c0 = part * d_gdn
        y = None
        for j in range(CONV_W):
            r0 = halo - (CONV_W - 1) + j
            term = xbuf_ref[r0:r0 + tt, c0:c0 + d_gdn] * wc_ref[j:j + 1, c0:c0 + d_gdn]
            y = term if y is None else y + term
        s = y * _sigmoid(y)
        if part == 2:
            out_ref[0] = s
        else:
            for h in range(GDN_HEADS):
                sh = s[:, h * HEAD_DIM:(h + 1) * HEAD_DIM]
                nrm = sh * lax.rsqrt(jnp.sum(sh * sh, axis=-1, keepdims=True) + L2_EPS)
                if part == 0:
                    nrm = nrm * (HEAD_DIM ** -0.5)
                out_ref[0, :, h * HEAD_DIM:(h + 1) * HEAD_DIM] = nrm

    xbuf_ref[0:halo, :] = xbuf_ref[tt:tt + halo, :]

    ab = ab_ref[0]
    lane = lax.broadcasted_iota(jnp.int32, (tt, LANES), 1)
    is_a = lane < GDN_HEADS
    z = ab + dtb_ref[...]
    softplus = jnp.maximum(z, 0.0) + jnp.log1p(jnp.exp(-jnp.abs(z)))
    g = jnp.where(is_a, -jnp.exp(alog_ref[...]) * softplus, 0.0)
    beta = _sigmoid(ab)

    shift = int(math.log2(chunk))
    r = lax.broadcasted_iota(jnp.int32, (tt, tt), 0)
    c = lax.broadcasted_iota(jnp.int32, (tt, tt), 1)
    same = (r >> shift) == (c >> shift)
    m_incl = jnp.where(same & (r >= c), 1.0, 0.0).astype(BF16)
    m_all = jnp.where(same, 1.0, 0.0).astype(BF16)
    gc = _dot_exact_lhs_rhs(m_incl, g)
    glast = _dot_exact_lhs_rhs(m_all, g)
    eg = jnp.exp(gc)
    egl = jnp.exp(glast - gc)
    egt = jnp.exp(glast)
    zero = jnp.zeros_like(gc)
    cols = (jnp.where(is_a, gc, zero)
            + jnp.where((lane >= 8) & (lane < 16), beta, zero)
            + pltpu.roll(jnp.where(is_a, eg, zero), 16, 1)
            + pltpu.roll(jnp.where(is_a, egl, zero), 24, 1)
            + pltpu.roll(jnp.where(is_a, egt, zero), 32, 1))
    cols_ref[0] = cols

    er = lax.broadcasted_iota(jnp.int32, (SUBLANES, LANES), 0)
    ec = lax.broadcasted_iota(jnp.int32, (SUBLANES, LANES), 1)
    sel = jnp.where(er == ec, 1.0, 0.0).astype(BF16)
    gct_ref[0] = _dot_exact_rhs(sel, jnp.where(is_a, gc, zero), NT_DIMS)


def _three_pieces(a):
    a0 = a.astype(BF16)
    r1 = a - a0.astype(F32)
    a1 = r1.astype(BF16)
    a2 = (r1 - a1.astype(F32)).astype(BF16)
    return a0, a1, a2


def _dot_exact_lhs_rhs(mask_bf, a):
    a0, a1, a2 = _three_pieces(a)
    return _dot(mask_bf, a0) + (_dot(mask_bf, a1) + _dot(mask_bf, a2))


def _dot_exact_rhs(mask_bf, a, dims):
    a0, a1, a2 = _three_pieces(a)
    return _dot(mask_bf, a0, dims) + (_dot(mask_bf, a1, dims) + _dot(mask_bf, a2, dims))


def _gdn_prep(proj3, hist, w_conv, alog_p, dtb_p, chunk, ab_block):
    b, t, _ = proj3.shape
    d_gdn = GDN_HEADS * HEAD_DIM
    tt = min(t, GDN_TILE)
    body = functools.partial(_gdn_prep_body, tt=tt, chunk=chunk)
    big = jax.ShapeDtypeStruct((b, t, d_gdn), F32)
    return pl.pallas_call(
        body,
        grid=(b, t // tt),
        in_specs=[pl.BlockSpec((1, tt, 3 * d_gdn), lambda i, j: (i, j, 0)),
                  pl.BlockSpec((1, tt, LANES), lambda i, j: (i, j, ab_block)),
                  pl.BlockSpec((1, CONV_W - 1, 3 * d_gdn), lambda i, j: (i, 0, 0)),
                  pl.BlockSpec((CONV_W, 3 * d_gdn), lambda i, j: (0, 0)),
                  pl.BlockSpec((1, LANES), lambda i, j: (0, 0)),
                  pl.BlockSpec((1, LANES), lambda i, j: (0, 0))],
        out_specs=[pl.BlockSpec((1, tt, d_gdn), lambda i, j: (i, j, 0)),
                   pl.BlockSpec((1, tt, d_gdn), lambda i, j: (i, j, 0)),
                   pl.BlockSpec((1, tt, d_gdn), lambda i, j: (i, j, 0)),
                   pl.BlockSpec((1, tt, LANES), lambda i, j: (i, j, 0)),
                   pl.BlockSpec((1, SUBLANES, tt), lambda i, j: (i, 0, j))],
        out_shape=[big, big, big,
                   jax.ShapeDtypeStruct((b, t, LANES), F32),
                   jax.ShapeDtypeStruct((b, SUBLANES, t), F32)],
        scratch_shapes=[pltpu.VMEM((tt + SUBLANES, 3 * d_gdn), F32)],
        compiler_params=_cparams(("parallel", "arbitrary")),
        name="gdn_prep",
    )(proj3, proj3, hist, w_conv, alog_p, dtb_p)


def _unit_lower_inverses(mats, r, c, chunk):
    base = 16
    eye = jnp.where(r == c, 1.0, 0.0)

    def blk(bs):
        s = int(math.log2(bs))
        return (r >> s) == (c >> s)

    d1 = [jnp.where(blk(base), a, 0.0) for a in mats]
    d2 = [_gdn_mm(x, x) for x in d1]
    d4 = [_gdn_mm(x, x) for x in d2]
    d8 = [_gdn_mm(x, x) for x in d4]
    t = [eye - x for x in d1]
    t = [x + _gdn_mm(x, y) for x, y in zip(t, d2)]
    t = [x + _gdn_mm(x, y) for x, y in zip(t, d4)]
    t = [x + _gdn_mm(x, y) for x, y in zip(t, d8)]
    bs = base
    while bs < chunk:
        off_mask = blk(2 * bs) & jnp.logical_not(blk(bs))
        inner = [_gdn_mm(jnp.where(off_mask, a, 0.0), x) for a, x in zip(mats, t)]
        t = [x - _gdn_mm(x, y) for x, y in zip(t, inner)]
        bs *= 2
    return t


def _gdn_body(q_ref, k_ref, v_ref, cols_ref, gct_ref, z_ref, wn_ref, s0_ref,
              o_ref, s_ref, *, tb, chunk):
    @pl.when(pl.program_id(1) == 0)
    def _():
        s_ref[...] = s0_ref[...]

    shift = int(math.log2(chunk))
    r = lax.broadcasted_iota(jnp.int32, (tb, tb), 0)
    c = lax.broadcasted_iota(jnp.int32, (tb, tb), 1)
    same = (r >> shift) == (c >> shift)
    incl = same & (r >= c)
    strict = same & (r > c)
    n_chunks = tb // chunk

    heads = range(GDN_HEADS)
    hsl = [slice(h * HEAD_DIM, (h + 1) * HEAD_DIM) for h in heads]
    q = [q_ref[0, :, s] for s in hsl]
    k = [k_ref[0, :, s] for s in hsl]
    beta = [cols_ref[0, :, 8 + h:9 + h] for h in heads]
    eg = [cols_ref[0, :, 16 + h:17 + h] for h in heads]
    egl = [cols_ref[0, :, 24 + h:25 + h] for h in heads]
    decay = [jnp.exp(jnp.where(incl, cols_ref[0, :, h:h + 1] - gct_ref[0, h:h + 1, :], -jnp.inf)) for h in heads]
    kb = [k[h] * beta[h] for h in heads]
    k_bf = [x.astype(BF16) for x in k]
    a = [jnp.where(strict, _dot(kb[h].astype(BF16), k_bf[h], NT_DIMS) * decay[h], 0.0) for h in heads]
    tinv = _unit_lower_inverses(a, r, c, chunk)
    sol = [_gdn_mm(tinv[h], jnp.concatenate([v_ref[0, :, hsl[h]] * beta[h], kb[h] * eg[h]], axis=1)) for h in heads]
    u = [x[:, :HEAD_DIM] for x in sol]
    w_bf = [x[:, HEAD_DIM:].astype(BF16) for x in sol]
    attn = [(_dot(q[h].astype(BF16), k_bf[h], NT_DIMS) * decay[h]).astype(BF16) for h in heads]
    q_dec = [(q[h] * eg[h]).astype(BF16) for h in heads]
    k_dec = [(k[h] * egl[h]).astype(BF16) for h in heads]

    s = [s_ref[0, h] for h in heads]
    v_new = [[] for _ in heads]
    o_state = [[] for _ in heads]
    for ci in range(n_chunks):
        rs = slice(ci * chunk, (ci + 1) * chunk)
        s_bf = [x.astype(BF16) for x in s]
        vn = [u[h][rs] - _dot(w_bf[h][rs], s_bf[h]) for h in heads]
        for h in heads:
            o_state[h].append(_dot(q_dec[h][rs], s_bf[h]))
            v_new[h].append(vn[h])
        s = [s[h] * cols_ref[0, ci * chunk:ci * chunk + 1, 32 + h:33 + h]
             + _dot(k_dec[h][rs], vn[h].astype(BF16), TN_DIMS) for h in heads]
    for h in heads:
        s_ref[0, h] = s[h]

    def cat(parts):
        return parts[0] if len(parts) == 1 else jnp.concatenate(parts, axis=0)

    for h in heads:
        o = cat(o_state[h]) + _dot(attn[h], cat(v_new[h]).astype(BF16))
        zh = z_ref[0, :, hsl[h]]
        o = (o * lax.rsqrt(jnp.mean(o * o, axis=-1, keepdims=True) + RMS_EPS) * wn_ref[...]
             * (zh * _sigmoid(zh)))
        o_ref[0, :, hsl[h]] = o.astype(BF16)


def _gdn(q, k, v, cols, gct, proj3, wn, s0, chunk, z_block):
    b, t, d_gdn = q.shape
    tb = min(t, GDN_TILE)
    body = functools.partial(_gdn_body, tb=tb, chunk=chunk)
    tile = lambda i, j: (i, j, 0)
    return pl.pallas_call(
        body,
        grid=(b, t // tb),
        in_specs=[pl.BlockSpec((1, tb, d_gdn), tile),
                  pl.BlockSpec((1, tb, d_gdn), tile),
                  pl.BlockSpec((1, tb, d_gdn), tile),
                  pl.BlockSpec((1, tb, LANES), tile),
                  pl.BlockSpec((1, SUBLANES, tb), lambda i, j: (i, 0, j)),
                  pl.BlockSpec((1, tb, d_gdn), lambda i, j: (i, j, z_block)),
                  pl.BlockSpec((1, HEAD_DIM), lambda i, j: (0, 0)),
                  pl.BlockSpec((1, GDN_HEADS, HEAD_DIM, HEAD_DIM), lambda i, j: (i, 0, 0, 0))],
        out_specs=[pl.BlockSpec((1, tb, d_gdn), tile),
                   pl.BlockSpec((1, GDN_HEADS, HEAD_DIM, HEAD_DIM), lambda i, j: (i, 0, 0, 0))],
        out_shape=[jax.ShapeDtypeStruct((b, t, d_gdn), BF16),
                   jax.ShapeDtypeStruct((b, GDN_HEADS, HEAD_DIM, HEAD_DIM), F32)],
        compiler_params=_cparams(("parallel", "arbitrary")),
        name="gdn_delta",
    )(q, k, v, cols, gct, proj3, wn, s0)


def _gelu_tanh(x):
    return 0.5 * x * (1.0 + jnp.tanh(math.sqrt(2.0 / math.pi) * (x + 0.044715 * (x * x * x))))


def _s5_body(u_ref, km_ref, wm_ref, vm_ref, lre_ref, lim_ref, dsk_ref, h0re_ref, h0im_ref,
             y_ref, hre_ref, him_ref, xbuf_ref, hbuf_ref, *, n_sub):
    half = GROUPS_PER_BLOCK * SSM_P

    @pl.when(pl.program_id(2) == 0)
    def _():
        hre_ref[...] = h0re_ref[...]
        him_ref[...] = h0im_ref[...]

    u_f = [u_ref[0, pl.ds(j, n_sub, stride=S5_SUB), :] for j in range(S5_SUB)]
    u_b = [x.astype(BF16) for x in u_f]

    x = _dot(u_b[0], wm_ref[0, 0])
    for j in range(1, S5_SUB):
        x = x + _dot(u_b[j], wm_ref[j, 0])
    xbuf_ref[...] = x

    lre = lre_ref[0]
    lim = lim_ref[0]

    def step(n, carry):
        hre, him = carry
        hbuf_ref[pl.ds(n, 1), 0:half] = hre
        hbuf_ref[pl.ds(n, 1), half:2 * half] = him
        xr = xbuf_ref[pl.ds(n, 1), 0:half]
        xi = xbuf_ref[pl.ds(n, 1), half:2 * half]
        return (lre * hre - lim * him + xr, lre * him + lim * hre + xi)

    hre, him = lax.fori_loop(0, n_sub, step, (hre_ref[0, 0], him_ref[0, 0]))
    hre_ref[0, 0] = hre
    him_ref[0, 0] = him

    h_b = hbuf_ref[...].astype(BF16)
    dsk = dsk_ref[0]
    for l in range(S5_SUB):
        y = _dot(h_b, vm_ref[l, 0])
        for d in range(l + 1):
            y = y + _dot(u_b[l - d], km_ref[d, 0])
        y = y + dsk * u_f[l]
        y_ref[0, pl.ds(l, n_sub, stride=S5_SUB), :] = _gelu_tanh(y)


def _s5(proj3, mats, h0re, h0im, u_block0):
    km, wm, vm, lre, lim, dsk = mats
    b, t, _ = proj3.shape
    n_gb = km.shape[1]
    tt = min(t, S5_TILE)
    n_sub = tt // S5_SUB
    half = GROUPS_PER_BLOCK * SSM_P
    body = functools.partial(_s5_body, n_sub=n_sub)
    state_spec = pl.BlockSpec((1, 1, 1, half), lambda g, i, j: (i, g, 0, 0))
    par_spec = pl.BlockSpec((1, 1, half), lambda g, i, j: (g, 0, 0))
    return pl.pallas_call(
        body,
        grid=(n_gb, b, t // tt),
        in_specs=[pl.BlockSpec((1, tt, LANES), lambda g, i, j: (i, j, u_block0 + g)),
                  pl.BlockSpec((S5_SUB, 1, LANES, LANES), lambda g, i, j: (0, g, 0, 0)),
                  pl.BlockSpec((S5_SUB, 1, LANES, 2 * half), lambda g, i, j: (0, g, 0, 0)),
                  pl.BlockSpec((S5_SUB, 1, 2 * half, LANES), lambda g, i, j: (0, g, 0, 0)),
                  par_spec, par_spec,
                  pl.BlockSpec((1, 1, LANES), lambda g, i, j: (g, 0, 0)),
                  state_spec, state_spec],
        out_specs=[pl.BlockSpec((1, tt, LANES), lambda g, i, j: (i, j, g)),
                   state_spec, state_spec],
        out_shape=[jax.ShapeDtypeStruct((b, t, n_gb * LANES), F32),
                   jax.ShapeDtypeStruct((b, n_gb, 1, half), F32),
                   jax.ShapeDtypeStruct((b, n_gb, 1, half), F32)],
        scratch_shapes=[pltpu.VMEM((n_sub, 2 * half), F32),
                        pltpu.VMEM((n_sub, 2 * half), F32)],
        compiler_params=_cparams(("parallel", "parallel", "arbitrary")),
        name="s5_scan",
    )(proj3, km, wm, vm, lre, lim, dsk, h0re, h0im)


def _s5_matrices(lam_re, lam_im, log_dt, b_re, b_im, c_re, c_im, d_skip):
    g, p = lam_re.shape
    n_gb = g // GROUPS_PER_BLOCK
    gpb = GROUPS_PER_BLOCK
    dt = jnp.exp(log_dt.astype(F32))
    lam = lax.complex(jnp.minimum(lam_re.astype(F32), -1e-4), lam_im.astype(F32))
    lam_bar = jnp.exp(lam * dt[:, None])
    b_bar = ((lam_bar - 1.0) / lam)[..., None] * lax.complex(b_re.astype(F32), b_im.astype(F32))
    c_c = lax.complex(c_re.astype(F32), c_im.astype(F32))
    pows = [jnp.ones_like(lam_bar)]
    for _ in range(S5_SUB):
        pows.append(pows[-1] * lam_bar)
    pw = jnp.stack(pows)
    eye = jnp.eye(gpb, dtype=F32)

    kd = jnp.real(jnp.einsum('gop,dgp,gpi->dgio', c_c, pw[:S5_SUB], b_bar))
    km = jnp.einsum('dbgio,gh->dbgiho', kd.reshape(S5_SUB, n_gb, gpb, SSM_CG, SSM_CG), eye)
    km = km.reshape(S5_SUB, n_gb, LANES, LANES)

    wj = pw[:S5_SUB][::-1][:, :, :, None] * b_bar[None]
    wj = jnp.transpose(wj, (0, 1, 3, 2)).reshape(S5_SUB, n_gb, gpb, SSM_CG, p)
    wre = jnp.einsum('dbgcp,gh->dbgchp', jnp.real(wj), eye).reshape(S5_SUB, n_gb, LANES, gpb * p)
    wim = jnp.einsum('dbgcp,gh->dbgchp', jnp.imag(wj), eye).reshape(S5_SUB, n_gb, LANES, gpb * p)
    wm = jnp.concatenate([wre, wim], axis=-1)

    cl = c_c[None] * pw[1:S5_SUB + 1][:, :, None, :]
    cl = jnp.transpose(cl, (0, 1, 3, 2)).reshape(S5_SUB, n_gb, gpb, p, SSM_CG)
    vre = jnp.einsum('dbgpc,gh->dbgphc', jnp.real(cl), eye).reshape(S5_SUB, n_gb, gpb * p, LANES)
    vim = jnp.einsum('dbgpc,gh->dbgphc', -jnp.imag(cl), eye).reshape(S5_SUB, n_gb, gpb * p, LANES)
    vm = jnp.concatenate([vre, vim], axis=-2)

    lam_s = pw[S5_SUB].reshape(n_gb, 1, gpb * p)
    dsk = d_skip.astype(F32).reshape(n_gb, 1, LANES)
    return (km.astype(BF16), wm.astype(BF16), vm.astype(BF16),
            jnp.real(lam_s), jnp.imag(lam_s), dsk)


def _glu_body(y_ref, w_ref, b_ref, o_ref):
    y = y_ref[...]
    gate = _dot(y.astype(BF16), w_ref[...]) + b_ref[...]
    o_ref[...] = (y * _sigmoid(gate)).astype(BF16)


def _glu(y2, w_bf, b_row, tm):
    m, d = y2.shape
    return pl.pallas_call(
        _glu_body,
        grid=(m // tm,),
        in_specs=[pl.BlockSpec((tm, d), lambda i: (i, 0)),
                  pl.BlockSpec((d, d), lambda i: (0, 0)),
                  pl.BlockSpec((1, d), lambda i: (0, 0))],
        out_specs=pl.BlockSpec((tm, d), lambda i: (i, 0)),
        out_shape=jax.ShapeDtypeStruct((m, d), BF16),
        compiler_params=_cparams(("parallel",)),
        name="s5_glu",
    )(y2, w_bf, b_row)


def _layernorm(v, g, b):
    mu = jnp.mean(v, axis=-1, keepdims=True)
    var = jnp.mean(jnp.square(v - mu), axis=-1, keepdims=True)
    return (v - mu) * lax.rsqrt(var + LN_EPS) * g + b


def _mix_body(og_ref, os_ref, wa_ref, wb_ref, x_ref, g_ref, b_ref, wr_ref, br_ref,
              h_ref, hb_ref, route_ref, cnt_ref, *, alpha):
    @pl.when(pl.program_id(0) == 0)
    def _():
        cnt_ref[...] = jnp.zeros_like(cnt_ref)

    mix = _dot(og_ref[...], wa_ref[...]) + _dot(os_ref[...], wb_ref[...])
    h = _layernorm(alpha * x_ref[...] + mix, g_ref[...], b_ref[...])
    h_ref[...] = h
    hb_ref[...] = h.astype(BF16)

    logits = _dot_x3(h, wr_ref[...]) + br_ref[...]
    tm = logits.shape[0]
    lane = lax.broadcasted_iota(jnp.int32, (tm, LANES), 1)
    work = jnp.where(lane < N_EXPERTS, logits, -jnp.inf)
    vals, idxs = [], []
    for _ in range(TOP_K):
        mx = jnp.max(work, axis=-1, keepdims=True)
        ix = jnp.min(jnp.where(work == mx, lane, LANES), axis=-1, keepdims=True)
        vals.append(mx)
        idxs.append(ix)
        work = jnp.where(lane == ix, -jnp.inf, work)
    exps = [jnp.exp(v - vals[0]) for v in vals]
    denom = exps[0]
    for e in exps[1:]:
        denom = denom + e
    chosen = jnp.zeros((tm, LANES), F32)
    for k in range(TOP_K):
        chosen = jnp.where(lane == idxs[k], 1.0, chosen)
    rr = lax.broadcasted_iota(jnp.int32, (tm, tm), 0)
    cc = lax.broadcasted_iota(jnp.int32, (tm, tm), 1)
    lower = jnp.where(rr > cc, 1.0, 0.0).astype(BF16)
    before = cnt_ref[...] + _dot(lower, chosen.astype(BF16))
    cnt_ref[...] = cnt_ref[...] + jnp.sum(chosen, axis=0, keepdims=True)

    route = jnp.zeros((tm, LANES), F32)
    for k in range(TOP_K):
        rank = jnp.sum(jnp.where(lane == idxs[k], before, 0.0), axis=-1, keepdims=True)
        route = jnp.where(lane == k, idxs[k].astype(F32), route)
        route = jnp.where(lane == TOP_K + k, exps[k] / denom, route)
        route = jnp.where(lane == 2 * TOP_K + k, rank, route)
    route_ref[...] = route


def _mix_ln_route(og, osm, wa, wb, x2, g_row, b_row, wr, br, alpha, tm):
    m, d = x2.shape
    dh = og.shape[1]
    body = functools.partial(_mix_body, alpha=alpha)
    row = lambda i: (i, 0)
    fix = lambda i: (0, 0)
    return pl.pallas_call(
        body,
        grid=(m // tm,),
        in_specs=[pl.BlockSpec((tm, dh), row), pl.BlockSpec((tm, dh), row),
                  pl.BlockSpec((dh, d), fix), pl.BlockSpec((dh, d), fix),
                  pl.BlockSpec((tm, d), row),
                  pl.BlockSpec((1, d), fix), pl.BlockSpec((1, d), fix),
                  pl.BlockSpec((d, LANES), fix), pl.BlockSpec((1, LANES), fix)],
        out_specs=[pl.BlockSpec((tm, d), row), pl.BlockSpec((tm, d), row),
                   pl.BlockSpec((tm, LANES), row), pl.BlockSpec((1, LANES), fix)],
        out_shape=[jax.ShapeDtypeStruct((m, d), F32),
                   jax.ShapeDtypeStruct((m, d), BF16),
                   jax.ShapeDtypeStruct((m, LANES), F32),
                   jax.ShapeDtypeStruct((1, LANES), F32)],
        compiler_params=_cparams(("arbitrary",)),
        name="mix_ln_route",
    )(og, osm, wa, wb, x2, g_row, b_row, wr, br)


def _moe_body(te_ref, nu_ref, x_ref, wg_ref, bg_ref, wu_ref, bu_ref, wd_ref, bd_ref, *rest, n_f):
    o_ref, acc_ref = rest[-2], rest[-1]
    i = pl.program_id(0)
    f = pl.program_id(1)
    used = i < nu_ref[0]

    @pl.when(used)
    def _():
        xb = x_ref[...]
        hg = jnp.minimum(_dot(xb, wg_ref[0].astype(BF16)) + bg_ref[0], SWIGLU_LIMIT)
        hu = jnp.clip(_dot(xb, wu_ref[0].astype(BF16)) + bu_ref[0], -SWIGLU_LIMIT, SWIGLU_LIMIT)
        hh = (hu + 1.0) * (hg * _sigmoid(SWIGLU_ALPHA * hg))
        part = _dot(hh.astype(BF16), wd_ref[0].astype(BF16))

        @pl.when(f == 0)
        def _():
            acc_ref[...] = part

        @pl.when(f != 0)
        def _():
            acc_ref[...] += part

        @pl.when(f == n_f - 1)
        def _():
            o_ref[...] = (acc_ref[...] + bd_ref[0]).astype(BF16)

    @pl.when(jnp.logical_not(used) & (f == n_f - 1))
    def _():
        o_ref[...] = jnp.zeros_like(o_ref)


def _moe_ffn(tile_expert, n_used, x_rows, w_gate, b_gate, w_up, b_up, w_down, b_down, tm, tf,
             tile_offset, total_tiles, prev):
    rows, d = x_rows.shape
    n_e, _, d_ff = w_gate.shape
    n_tiles = rows // tm
    n_f = d_ff // tf
    body = functools.partial(_moe_body, n_f=n_f)

    def fcol(i, f, nu):
        return jnp.where(i < nu[0], f, n_f - 1)

    in_specs = [pl.BlockSpec((tm, d), lambda i, f, te, nu: (i, 0)),
                pl.BlockSpec((1, d, tf), lambda i, f, te, nu: (te[i], 0, fcol(i, f, nu))),
                pl.BlockSpec((1, 1, tf), lambda i, f, te, nu: (te[i], 0, fcol(i, f, nu))),
                pl.BlockSpec((1, d, tf), lambda i, f, te, nu: (te[i], 0, fcol(i, f, nu))),
                pl.BlockSpec((1, 1, tf), lambda i, f, te, nu: (te[i], 0, fcol(i, f, nu))),
                pl.BlockSpec((1, tf, d), lambda i, f, te, nu: (te[i], fcol(i, f, nu), 0)),
                pl.BlockSpec((1, 1, d), lambda i, f, te, nu: (te[i], 0, 0))]
    args = [tile_expert, n_used, x_rows, w_gate, b_gate.reshape(n_e, 1, d_ff),
            w_up, b_up.reshape(n_e, 1, d_ff), w_down, b_down.reshape(n_e, 1, d)]
    aliases = {}
    if prev is not None:
        in_specs.append(pl.BlockSpec(memory_space=pl.ANY))
        args.append(prev)
        aliases = {len(args) - 1: 0}
    grid_spec = pltpu.PrefetchScalarGridSpec(
        num_scalar_prefetch=2,
        grid=(n_tiles, n_f),
        in_specs=in_specs,
        out_specs=pl.BlockSpec((tm, d), lambda i, f, te, nu: (i + tile_offset, 0)),
        scratch_shapes=[pltpu.VMEM((tm, d), F32)],
    )
    return pl.pallas_call(
        body,
        grid_spec=grid_spec,
        out_shape=jax.ShapeDtypeStruct((total_tiles * tm, d), BF16),
        input_output_aliases=aliases,
        compiler_params=_cparams(("arbitrary", "arbitrary")),
        name="moe_ffn",
    )(*args)


def _final_body(h_ref, ex_ref, route_ref, g_ref, b_ref, o_ref, *, alpha):
    ff = None
    for k in range(TOP_K):
        term = ex_ref[k].astype(F32) * route_ref[:, TOP_K + k:TOP_K + k + 1]
        ff = term if ff is None else ff + term
    o_ref[...] = _layernorm(alpha * h_ref[...] + ff, g_ref[...], b_ref[...])


def _final_ln(h, ex, route, g_row, b_row, alpha, tm):
    m, d = h.shape
    body = functools.partial(_final_body, alpha=alpha)
    row = lambda i: (i, 0)
    fix = lambda i: (0, 0)
    return pl.pallas_call(
        body,
        grid=(m // tm,),
        in_specs=[pl.BlockSpec((tm, d), row), pl.BlockSpec((TOP_K, tm, d), lambda i: (0, i, 0)),
                  pl.BlockSpec((tm, LANES), row),
                  pl.BlockSpec((1, d), fix), pl.BlockSpec((1, d), fix)],
        out_specs=pl.BlockSpec((tm, d), row),
        out_shape=jax.ShapeDtypeStruct((m, d), F32),
        compiler_params=_cparams(("parallel",)),
        name="final_ln",
    )(h, ex, route, g_row, b_row)


def _row_tile(m, pref):
    t = min(m, pref)
    while m % t:
        t //= 2
    return t


def _route_tables(route, counts_row, m, tm):
    n_assign = m * TOP_K
    e_tok = route[:, :TOP_K].astype(jnp.int32)
    rank = route[:, 2 * TOP_K:3 * TOP_K].astype(jnp.int32)
    counts = counts_row[0, :N_EXPERTS].astype(jnp.int32)
    padded = (counts + tm - 1) // tm * tm
    pends = jnp.cumsum(padded)
    pstarts = pends - padded
    pos = pstarts[e_tok] + rank
    n_tiles = -(-n_assign // tm) + N_EXPERTS
    cap = n_tiles * tm
    tok = jnp.broadcast_to(jnp.arange(m, dtype=jnp.int32)[:, None], (m, TOP_K))
    row_tok = jnp.zeros((cap,), jnp.int32).at[pos.reshape(n_assign)].set(
        tok.reshape(n_assign), unique_indices=True)
    tile_start = jnp.arange(n_tiles, dtype=jnp.int32) * tm
    tile_expert = jnp.minimum(jnp.searchsorted(pends, tile_start, side='right'), N_EXPERTS - 1).astype(jnp.int32)
    n_used = (pends[-1] // tm).astype(jnp.int32).reshape(1)
    last_used = jnp.maximum(n_used[0] - 1, 0)
    tile_expert = jnp.where(tile_start // tm < n_used[0], tile_expert, tile_expert[last_used])
    return row_tok, pos, tile_expert, n_used


def _layer(x, conv_hist, s_gdn, h_re, h_im, chunk, alpha, p):
    b, t, d = x.shape
    m = b * t
    d_gdn = GDN_HEADS * HEAD_DIM
    x2 = x.reshape(m, d)

    proj = _in_proj(x2, p['w_in'], _row_tile(m, 1024), p['proj_tn'])
    proj3 = proj.reshape(b, t, proj.shape[1])
    conv_new = proj3[:, t - (CONV_W - 1):, :3 * d_gdn]

    q, k, v, cols, gct = _gdn_prep(proj3, conv_hist, p['w_conv'], p['alog_row'], p['dtb_row'],
                                   chunk, p['ab_block'])
    o_gdn, s_new = _gdn(q, k, v, cols, gct, proj3, p['wn_row'], s_gdn, chunk, p['z_block'])

    n_gb = p['s5_mats'][0].shape[1]
    half = GROUPS_PER_BLOCK * SSM_P
    yg, hre_new, him_new = _s5(proj3, p['s5_mats'], h_re.reshape(b, n_gb, 1, half),
                               h_im.reshape(b, n_gb, 1, half), p['u_block0'])
    o_ssm = _glu(yg.reshape(m, yg.shape[2]), p['w_glu'], p['b_glu_row'], _row_tile(m, 512))

    h, hb, route, counts_row = _mix_ln_route(o_gdn.reshape(m, d_gdn), o_ssm, p['w_out_a'], p['w_out_b'], x2,
                                             p['ln1_g'], p['ln1_b'], p['w_router'], p['b_router'], alpha,
                                             _row_tile(m, 256))

    big = m * TOP_K >= 8 * p['moe_tm']
    tm = p['moe_tm'] if big else 128
    row_tok, pos, tile_expert, n_used = _route_tables(route, counts_row, m, tm)
    n_tiles = tile_expert.shape[0]
    n_chunks = 4 if (big and n_tiles % 4 == 0) else 1
    ct = n_tiles // n_chunks
    outs = None
    for c in range(n_chunks):
        x_rows = hb[row_tok[c * ct * tm:(c + 1) * ct * tm]]
        nu_c = jnp.clip(n_used - c * ct, 0, ct)
        outs = _moe_ffn(tile_expert[c * ct:(c + 1) * ct], nu_c, x_rows, p['w_gate'], p['b_gate'],
                        p['w_up'], p['b_up'], p['w_down'], p['b_down'], tm, p['moe_tf'],
                        c * ct, n_tiles, outs)
    ex = outs[pos.T.reshape(TOP_K * m)].reshape(TOP_K, m, d)

    y = _final_ln(h, ex, route, p['ln2_g'], p['ln2_b'], alpha, _row_tile(m, 256))
    g_all = h_re.shape[1]
    return (y.reshape(b, t, d), conv_new, s_new,
            hre_new.reshape(b, g_all, SSM_P), him_new.reshape(b, g_all, SSM_P))


def _pad_lanes(v, fill=0.0):
    return jnp.pad(v.astype(F32), (0, LANES - v.shape[0]), constant_values=fill).reshape(1, LANES)


def _layer_params(l, w_in, w_conv, a_log, dt_bias, w_onorm, lam_re, lam_im, log_dt, b_re, b_im, c_re, c_im,
                  d_skip, w_glu, b_glu, w_out, ln1_g, ln1_b, w_router, b_router, w_gate, b_gate,
                  w_up, b_up, w_down, b_down, ln2_g, ln2_b):
    d_model = w_in.shape[1]
    d_gdn = GDN_HEADS * HEAD_DIM
    d_qkvz = 4 * d_gdn
    d_ssm = d_model - d_gdn
    wi = w_in[l]
    proj_tn = 896
    n_cols = d_qkvz + d_ssm + LANES
    n_pad = -(-n_cols // proj_tn) * proj_tn
    w_in_r = jnp.concatenate([wi[:, :d_qkvz], wi[:, d_qkvz + 2 * GDN_HEADS:],
                              wi[:, d_qkvz:d_qkvz + 2 * GDN_HEADS],
                              jnp.zeros((d_model, n_pad - d_qkvz - d_ssm - 2 * GDN_HEADS), wi.dtype)], axis=1)
    wo = w_out[l].astype(BF16)
    wr = jnp.pad(w_router[l].astype(F32), ((0, 0), (0, LANES - N_EXPERTS)))
    return {
        'w_in': w_in_r.astype(BF16), 'proj_tn': proj_tn,
        'z_block': 3, 'u_block0': (d_qkvz) // LANES, 'ab_block': (d_qkvz + d_ssm) // LANES,
        'w_conv': w_conv[l].astype(F32),
        'alog_row': _pad_lanes(a_log[l]), 'dtb_row': _pad_lanes(dt_bias[l]),
        'wn_row': w_onorm[l].astype(F32).reshape(1, HEAD_DIM),
        's5_mats': _s5_matrices(lam_re[l], lam_im[l], log_dt[l], b_re[l], b_im[l], c_re[l], c_im[l], d_skip[l]),
        'w_glu': w_glu[l].astype(BF16), 'b_glu_row': b_glu[l].astype(F32).reshape(1, d_ssm),
        'w_out_a': wo[:d_gdn], 'w_out_b': wo[d_gdn:],
        'ln1_g': ln1_g[l].astype(F32).reshape(1, d_model), 'ln1_b': ln1_b[l].astype(F32).reshape(1, d_model),
        'w_router': wr, 'b_router': _pad_lanes(b_router[l]),
        'w_gate': w_gate[l], 'b_gate': b_gate[l], 'w_up': w_up[l], 'b_up': b_up[l],
        'w_down': w_down[l], 'b_down': b_down[l],
        'ln2_g': ln2_g[l].astype(F32).reshape(1, d_model), 'ln2_b': ln2_b[l].astype(F32).reshape(1, d_model),
        'moe_tm': 512, 'moe_tf': 512,
    }


def kernel(x_prompt, x_sample, state_conv, state_gdn, state_ssm_re, state_ssm_im, w_in, w_conv, a_log, dt_bias, w_onorm, lam_re, lam_im, log_dt, b_re, b_im, c_re, c_im, d_skip, w_glu, b_glu, w_out, ln1_g, ln1_b, w_router, b_router, w_gate, b_gate, w_up, b_up, w_down, b_down, ln2_g, ln2_b):
    depth = w_in.shape[0]
    alpha = (2.0 * depth) ** 0.25
    bp, seq, _ = x_prompt.shape
    chunk_p = 64
    d_qkv = state_conv.shape[-1]
    n_groups, n_p = state_ssm_re.shape[-2:]
    yp, ys = x_prompt, x_sample
    outs_p = [[], [], [], []]
    outs_s = [[], [], [], []]
    for l in range(depth):
        p = _layer_params(l, w_in, w_conv, a_log, dt_bias, w_onorm, lam_re, lam_im, log_dt, b_re, b_im,
                          c_re, c_im, d_skip, w_glu, b_glu, w_out, ln1_g, ln1_b, w_router, b_router,
                          w_gate, b_gate, w_up, b_up, w_down, b_down, ln2_g, ln2_b)
        yp, cp, sp, rp, ip = _layer(
            yp, jnp.zeros((bp, CONV_W - 1, d_qkv), F32),
            jnp.zeros((bp, GDN_HEADS, HEAD_DIM, HEAD_DIM), F32),
            jnp.zeros((bp, n_groups, n_p), F32), jnp.zeros((bp, n_groups, n_p), F32),
            chunk_p, alpha, p)
        ys, cs, ss, rs, is_ = _layer(
            ys, state_conv[l].astype(F32), state_gdn[l].astype(F32),
            state_ssm_re[l].astype(F32), state_ssm_im[l].astype(F32),
            ys.shape[1], alpha, p)
        for acc, val in zip(outs_p, (cp, sp, rp, ip)):
            acc.append(val)
        for acc, val in zip(outs_s, (cs, ss, rs, is_)):
            acc.append(val)
    return (yp, ys, *[jnp.stack(a) for a in outs_p], *[jnp.stack(a) for a in outs_s])
```

```python
import functools
import math

import jax
import jax.numpy as jnp
from jax import lax
from jax.experimental import pallas as pl
from jax.experimental.pallas import tpu as pltpu

F32 = jnp.float32
BF16 = jnp.bfloat16

GDN_HEADS = 8
HEAD_DIM = 128
CONV_W = 4
SSM_CG = 16
SSM_P = 64
N_EXPERTS = 32
TOP_K = 4
SWIGLU_ALPHA = 1.702
SWIGLU_LIMIT = 7.0
LN_EPS = 1e-5
RMS_EPS = 1e-6
L2_EPS = 1e-6

LANES = 128
SUBLANES = 8
S5_SUB = 8
S5_TILE = 2048
GDN_TILE = 128
GROUPS_PER_BLOCK = LANES // SSM_CG
VMEM_LIMIT = 56 * 1024 * 1024

NT_DIMS = (((1,), (1,)), ((), ()))
TN_DIMS = (((0,), (0,)), ((), ()))


def _dot(a, b, dims=(((1,), (0,)), ((), ()))):
    return lax.dot_general(a, b, dims, preferred_element_type=F32)


def _split(a):
    hi = a.astype(BF16)
    lo = (a - hi.astype(F32)).astype(BF16)
    return hi, lo


def _dot_x3(a, b, dims=(((1,), (0,)), ((), ()))):
    ah, al = _split(a)
    bh, bl = _split(b)
    return _dot(ah, bh, dims) + (_dot(ah, bl, dims) + _dot(al, bh, dims))


def _dot_bf(a, b, dims=(((1,), (0,)), ((), ()))):
    return _dot(a.astype(BF16), b.astype(BF16), dims)


_gdn_mm = _dot_bf


def _sigmoid(x):
    return 1.0 / (1.0 + jnp.exp(-x))


ROW_SUB = SUBLANES


def _pack_rows(x, ref, tm):
    half = x.shape[1] // 2
    for s in range(ROW_SUB):
        lo = x[:, s * LANES:(s + 1) * LANES].astype(BF16).astype(F32)
        hi = x[:, half + s * LANES:half + (s + 1) * LANES].astype(BF16).astype(F32)
        word = (lax.bitcast_convert_type(lo, jnp.uint32) >> 16) | lax.bitcast_convert_type(hi, jnp.uint32)
        ref[pl.ds(s, tm, stride=ROW_SUB), :] = word


def _unpack_rows(ref, base, tm):
    lo, hi = [], []
    for s in range(ROW_SUB):
        word = ref[pl.ds(base + s, tm, stride=ROW_SUB), :]
        lo.append(lax.bitcast_convert_type(word << 16, F32))
        hi.append(lax.bitcast_convert_type(word & jnp.uint32(0xFFFF0000), F32))
    return lo, hi


def _cparams(sem):
    return pltpu.CompilerParams(dimension_semantics=sem, vmem_limit_bytes=VMEM_LIMIT)


def _proj_body(x_ref, w_ref, o_ref, xb_ref):
    @pl.when(pl.program_id(1) == 0)
    def _():
        xb_ref[...] = x_ref[...].astype(BF16)

    o_ref[...] = _dot(xb_ref[...], w_ref[...])


def _in_proj(x2, w_bf, tm, tn):
    m, k = x2.shape
    n = w_bf.shape[1]
    return pl.pallas_call(
        _proj_body,
        grid=(m // tm, n // tn),
        in_specs=[pl.BlockSpec((tm, k), lambda i, j: (i, 0)),
                  pl.BlockSpec((k, tn), lambda i, j: (0, j))],
        out_specs=pl.BlockSpec((tm, tn), lambda i, j: (i, j)),
        out_shape=jax.ShapeDtypeStruct((m, n), F32),
        scratch_shapes=[pltpu.VMEM((tm, k), BF16)],
        compiler_params=_cparams(("parallel", "arbitrary")),
        name="in_proj",
    )(x2, w_bf)


def _gdn_prep_body(qkv_ref, ab_ref, hist_ref, wc_ref, alog_ref, dtb_ref,
                   q_ref, k_ref, v_ref, cols_ref, gct_ref, xbuf_ref, *, tt, chunk):
    d_gdn = GDN_HEADS * HEAD_DIM
    halo = SUBLANES

    @pl.when(pl.program_id(1) == 0)
    def _():
        xbuf_ref[0:halo, :] = jnp.zeros((halo, 3 * d_gdn), F32)
        xbuf_ref[halo - (CONV_W - 1):halo, :] = hist_ref[0]

    xbuf_ref[halo:halo + tt, :] = qkv_ref[0]

    for part, out_ref in enumerate((q_ref, k_ref, v_ref)):
        c0 = part * d_gdn
        y = None
        for j in range(CONV_W):
            r0 = halo - (CONV_W - 1) + j
            term = xbuf_ref[r0:r0 + tt, c0:c0 + d_gdn] * wc_ref[j:j + 1, c0:c0 + d_gdn]
            y = term if y is None else y + term
        s = y * _sigmoid(y)
        if part == 2:
            out_ref[0] = s
        else:
            for h in range(GDN_HEADS):
                sh = s[:, h * HEAD_DIM:(h + 1) * HEAD_DIM]
                nrm = sh * lax.rsqrt(jnp.sum(sh * sh, axis=-1, keepdims=True) + L2_EPS)
                if part == 0:
                    nrm = nrm * (HEAD_DIM ** -0.5)
                out_ref[0, :, h * HEAD_DIM:(h + 1) * HEAD_DIM] = nrm

    xbuf_ref[0:halo, :] = xbuf_ref[tt:tt + halo, :]

    ab = ab_ref[0]
    lane = lax.broadcasted_iota(jnp.int32, (tt, LANES), 1)
    is_a = lane < GDN_HEADS
    z = ab + dtb_ref[...]
    softplus = jnp.maximum(z, 0.0) + jnp.log1p(jnp.exp(-jnp.abs(z)))
    g = jnp.where(is_a, -jnp.exp(alog_ref[...]) * softplus, 0.0)
    beta = _sigmoid(ab)

    shift = int(math.log2(chunk))
    r = lax.broadcasted_iota(jnp.int32, (tt, tt), 0)
    c = lax.broadcasted_iota(jnp.int32, (tt, tt), 1)
    same = (r >> shift) == (c >> shift)
    m_incl = jnp.where(same & (r >= c), 1.0, 0.0).astype(BF16)
    m_all = jnp.where(same, 1.0, 0.0).astype(BF16)
    gc = _dot_exact_lhs_rhs(m_incl, g)
    glast = _dot_exact_lhs_rhs(m_all, g)
    eg = jnp.exp(gc)
    egl = jnp.exp(glast - gc)
    egt = jnp.exp(glast)
    zero = jnp.zeros_like(gc)
    cols = (jnp.where(is_a, gc, zero)
            + jnp.where((lane >= 8) & (lane < 16), beta, zero)
            + pltpu.roll(jnp.where(is_a, eg, zero), 16, 1)
            + pltpu.roll(jnp.where(is_a, egl, zero), 24, 1)
            + pltpu.roll(jnp.where(is_a, egt, zero), 32, 1))
    cols_ref[0] = cols

    er = lax.broadcasted_iota(jnp.int32, (SUBLANES, LANES), 0)
    ec = lax.broadcasted_iota(jnp.int32, (SUBLANES, LANES), 1)
    sel = jnp.where(er == ec, 1.0, 0.0).astype(BF16)
    gct_ref[0] = _dot_exact_rhs(sel, jnp.where(is_a, gc, zero), NT_DIMS)


def _three_pieces(a):
    a0 = a.astype(BF16)
    r1 = a - a0.astype(F32)
    a1 = r1.astype(BF16)
    a2 = (r1 - a1.astype(F32)).astype(BF16)
    return a0, a1, a2


def _dot_exact_lhs_rhs(mask_bf, a):
    a0, a1, a2 = _three_pieces(a)
    return _dot(mask_bf, a0) + (_dot(mask_bf, a1) + _dot(mask_bf, a2))


def _dot_exact_rhs(mask_bf, a, dims):
    a0, a1, a2 = _three_pieces(a)
    return _dot(mask_bf, a0, dims) + (_dot(mask_bf, a1, dims) + _dot(mask_bf, a2, dims))


def _gdn_prep(proj3, hist, w_conv, alog_p, dtb_p, chunk, ab_block):
    b, t, _ = proj3.shape
    d_gdn = GDN_HEADS * HEAD_DIM
    tt = min(t, GDN_TILE)
    body = functools.partial(_gdn_prep_body, tt=tt, chunk=chunk)
    big = jax.ShapeDtypeStruct((b, t, d_gdn), F32)
    return pl.pallas_call(
        body,
        grid=(b, t // tt),
        in_specs=[pl.BlockSpec((1, tt, 3 * d_gdn), lambda i, j: (i, j, 0)),
                  pl.BlockSpec((1, tt, LANES), lambda i, j: (i, j, ab_block)),
                  pl.BlockSpec((1, CONV_W - 1, 3 * d_gdn), lambda i, j: (i, 0, 0)),
                  pl.BlockSpec((CONV_W, 3 * d_gdn), lambda i, j: (0, 0)),
                  pl.BlockSpec((1, LANES), lambda i, j: (0, 0)),
                  pl.BlockSpec((1, LANES), lambda i, j: (0, 0))],
        out_specs=[pl.BlockSpec((1, tt, d_gdn), lambda i, j: (i, j, 0)),
                   pl.BlockSpec((1, tt, d_gdn), lambda i, j: (i, j, 0)),
                   pl.BlockSpec((1, tt, d_gdn), lambda i, j: (i, j, 0)),
                   pl.BlockSpec((1, tt, LANES), lambda i, j: (i, j, 0)),
                   pl.BlockSpec((1, SUBLANES, tt), lambda i, j: (i, 0, j))],
        out_shape=[big, big, big,
                   jax.ShapeDtypeStruct((b, t, LANES), F32),
                   jax.ShapeDtypeStruct((b, SUBLANES, t), F32)],
        scratch_shapes=[pltpu.VMEM((tt + SUBLANES, 3 * d_gdn), F32)],
        compiler_params=_cparams(("parallel", "arbitrary")),
        name="gdn_prep",
    )(proj3, proj3, hist, w_conv, alog_p, dtb_p)


def _unit_lower_inverses(mats, r, c, chunk):
    base = 16
    eye = jnp.where(r == c, 1.0, 0.0)

    def blk(bs):
        s = int(math.log2(bs))
        return (r >> s) == (c >> s)

    d1 = [jnp.where(blk(base), a, 0.0) for a in mats]
    d2 = [_gdn_mm(x, x) for x in d1]
    d4 = [_gdn_mm(x, x) for x in d2]
    d8 = [_gdn_mm(x, x) for x in d4]
    t = [eye - x for x in d1]
    t = [x + _gdn_mm(x, y) for x, y in zip(t, d2)]
    t = [x + _gdn_mm(x, y) for x, y in zip(t, d4)]
    t = [x + _gdn_mm(x, y) for x, y in zip(t, d8)]
    bs = base
    while bs < chunk:
        off_mask = blk(2 * bs) & jnp.logical_not(blk(bs))
        inner = [_gdn_mm(jnp.where(off_mask, a, 0.0), x) for a, x in zip(mats, t)]
        t = [x - _gdn_mm(x, y) for x, y in zip(t, inner)]
        bs *= 2
    return t


def _gdn_body(q_ref, k_ref, v_ref, cols_ref, gct_ref, z_ref, wn_ref, s0_ref,
              o_ref, s_ref, *, tb, chunk):
    @pl.when(pl.program_id(1) == 0)
    def _():
        s_ref[...] = s0_ref[...]

    shift = int(math.log2(chunk))
    r = lax.broadcasted_iota(jnp.int32, (tb, tb), 0)
    c = lax.broadcasted_iota(jnp.int32, (tb, tb), 1)
    same = (r >> shift) == (c >> shift)
    incl = same & (r >= c)
    strict = same & (r > c)
    n_chunks = tb // chunk

    heads = range(GDN_HEADS)
    hsl = [slice(h * HEAD_DIM, (h + 1) * HEAD_DIM) for h in heads]
    q = [q_ref[0, :, s] for s in hsl]
    k = [k_ref[0, :, s] for s in hsl]
    beta = [cols_ref[0, :, 8 + h:9 + h] for h in heads]
    eg = [cols_ref[0, :, 16 + h:17 + h] for h in heads]
    egl = [cols_ref[0, :, 24 + h:25 + h] for h in heads]
    decay = [jnp.exp(jnp.where(incl, cols_ref[0, :, h:h + 1] - gct_ref[0, h:h + 1, :], -jnp.inf)) for h in heads]
    kb = [k[h] * beta[h] for h in heads]
    k_bf = [x.astype(BF16) for x in k]
    a = [jnp.where(strict, _dot(kb[h].astype(BF16), k_bf[h], NT_DIMS) * decay[h], 0.0) for h in heads]
    tinv = _unit_lower_inverses(a, r, c, chunk)
    sol = [_gdn_mm(tinv[h], jnp.concatenate([v_ref[0, :, hsl[h]] * beta[h], kb[h] * eg[h]], axis=1)) for h in heads]
    u = [x[:, :HEAD_DIM] for x in sol]
    w_bf = [x[:, HEAD_DIM:].astype(BF16) for x in sol]
    attn = [(_dot(q[h].astype(BF16), k_bf[h], NT_DIMS) * decay[h]).astype(BF16) for h in heads]
    q_dec = [(q[h] * eg[h]).astype(BF16) for h in heads]
    k_dec = [(k[h] * egl[h]).astype(BF16) for h in heads]

    s = [s_ref[0, h] for h in heads]
    v_new = [[] for _ in heads]
    o_state = [[] for _ in heads]
    for ci in range(n_chunks):
        rs = slice(ci * chunk, (ci + 1) * chunk)
        s_bf = [x.astype(BF16) for x in s]
        vn = [u[h][rs] - _dot(w_bf[h][rs], s_bf[h]) for h in heads]
        for h in heads:
            o_state[h].append(_dot(q_dec[h][rs], s_bf[h]))
            v_new[h].append(vn[h])
        s = [s[h] * cols_ref[0, ci * chunk:ci * chunk + 1, 32 + h:33 + h]
             + _dot(k_dec[h][rs], vn[h].astype(BF16), TN_DIMS) for h in heads]
    for h in heads:
        s_ref[0, h] = s[h]

    def cat(parts):
        return parts[0] if len(parts) == 1 else jnp.concatenate(parts, axis=0)

    for h in heads:
        o = cat(o_state[h]) + _dot(attn[h], cat(v_new[h]).astype(BF16))
        zh = z_ref[0, :, hsl[h]]
        o = (o * lax.rsqrt(jnp.mean(o * o, axis=-1, keepdims=True) + RMS_EPS) * wn_ref[...]
             * (zh * _sigmoid(zh)))
        o_ref[0, :, hsl[h]] = o.astype(BF16)


def _gdn(q, k, v, cols, gct, proj3, wn, s0, chunk, z_block):
    b, t, d_gdn = q.shape
    tb = min(t, GDN_TILE)
    body = functools.partial(_gdn_body, tb=tb, chunk=chunk)
    tile = lambda i, j: (i, j, 0)
    return pl.pallas_call(
        body,
        grid=(b, t // tb),
        in_specs=[pl.BlockSpec((1, tb, d_gdn), tile),
                  pl.BlockSpec((1, tb, d_gdn), tile),
                  pl.BlockSpec((1, tb, d_gdn), tile),
                  pl.BlockSpec((1, tb, LANES), tile),
                  pl.BlockSpec((1, SUBLANES, tb), lambda i, j: (i, 0, j)),
                  pl.BlockSpec((1, tb, d_gdn), lambda i, j: (i, j, z_block)),
                  pl.BlockSpec((1, HEAD_DIM), lambda i, j: (0, 0)),
                  pl.BlockSpec((1, GDN_HEADS, HEAD_DIM, HEAD_DIM), lambda i, j: (i, 0, 0, 0))],
        out_specs=[pl.BlockSpec((1, tb, d_gdn), tile),
                   pl.BlockSpec((1, GDN_HEADS, HEAD_DIM, HEAD_DIM), lambda i, j: (i, 0, 0, 0))],
        out_shape=[jax.ShapeDtypeStruct((b, t, d_gdn), BF16),
                   jax.ShapeDtypeStruct((b, GDN_HEADS, HEAD_DIM, HEAD_DIM), F32)],
        compiler_params=_cparams(("parallel", "arbitrary")),
        name="gdn_delta",
    )(q, k, v, cols, gct, proj3, wn, s0)


def _gelu_tanh(x):
    return 0.5 * x * (1.0 + jnp.tanh(math.sqrt(2.0 / math.pi) * (x + 0.044715 * (x * x * x))))


def _s5_body(u_ref, km_ref, wm_ref, vm_ref, lre_ref, lim_ref, dsk_ref, h0re_ref, h0im_ref,
             y_ref, hre_ref, him_ref, xbuf_ref, hbuf_ref, *, n_sub):
    half = GROUPS_PER_BLOCK * SSM_P

    @pl.when(pl.program_id(2) == 0)
    def _():
        hre_ref[...] = h0re_ref[...]
        him_ref[...] = h0im_ref[...]

    u_f = [u_ref[0, pl.ds(j, n_sub, stride=S5_SUB), :] for j in range(S5_SUB)]
    u_b = [x.astype(BF16) for x in u_f]

    x = _dot(u_b[0], wm_ref[0, 0])
    for j in range(1, S5_SUB):
        x = x + _dot(u_b[j], wm_ref[j, 0])
    xbuf_ref[...] = x

    lre = lre_ref[0]
    lim = lim_ref[0]

    def step(n, carry):
        hre, him = carry
        hbuf_ref[pl.ds(n, 1), 0:half] = hre
        hbuf_ref[pl.ds(n, 1), half:2 * half] = him
        xr = xbuf_ref[pl.ds(n, 1), 0:half]
        xi = xbuf_ref[pl.ds(n, 1), half:2 * half]
        return (lre * hre - lim * him + xr, lre * him + lim * hre + xi)

    hre, him = lax.fori_loop(0, n_sub, step, (hre_ref[0, 0], him_ref[0, 0]))
    hre_ref[0, 0] = hre
    him_ref[0, 0] = him

    h_b = hbuf_ref[...].astype(BF16)
    dsk = dsk_ref[0]
    for l in range(S5_SUB):
        y = _dot(h_b, vm_ref[l, 0])
        for d in range(l + 1):
            y = y + _dot(u_b[l - d], km_ref[d, 0])
        y = y + dsk * u_f[l]
        y_ref[0, pl.ds(l, n_sub, stride=S5_SUB), :] = _gelu_tanh(y)


def _s5(proj3, mats, h0re, h0im, u_block0):
    km, wm, vm, lre, lim, dsk = mats
    b, t, _ = proj3.shape
    n_gb = km.shape[1]
    tt = min(t, S5_TILE)
    n_sub = tt // S5_SUB
    half = GROUPS_PER_BLOCK * SSM_P
    body = functools.partial(_s5_body, n_sub=n_sub)
    state_spec = pl.BlockSpec((1, 1, 1, half), lambda g, i, j: (i, g, 0, 0))
    par_spec = pl.BlockSpec((1, 1, half), lambda g, i, j: (g, 0, 0))
    return pl.pallas_call(
        body,
        grid=(n_gb, b, t // tt),
        in_specs=[pl.BlockSpec((1, tt, LANES), lambda g, i, j: (i, j, u_block0 + g)),
                  pl.BlockSpec((S5_SUB, 1, LANES, LANES), lambda g, i, j: (0, g, 0, 0)),
                  pl.BlockSpec((S5_SUB, 1, LANES, 2 * half), lambda g, i, j: (0, g, 0, 0)),
                  pl.BlockSpec((S5_SUB, 1, 2 * half, LANES), lambda g, i, j: (0, g, 0, 0)),
                  par_spec, par_spec,
                  pl.BlockSpec((1, 1, LANES), lambda g, i, j: (g, 0, 0)),
                  state_spec, state_spec],
        out_specs=[pl.BlockSpec((1, tt, LANES), lambda g, i, j: (i, j, g)),
                   state_spec, state_spec],
        out_shape=[jax.ShapeDtypeStruct((b, t, n_gb * LANES), F32),
                   jax.ShapeDtypeStruct((b, n_gb, 1, half), F32),
                   jax.ShapeDtypeStruct((b, n_gb, 1, half), F32)],
        scratch_shapes=[pltpu.VMEM((n_sub, 2 * half), F32),
                        pltpu.VMEM((n_sub, 2 * half), F32)],
        compiler_params=_cparams(("parallel", "parallel", "arbitrary")),
        name="s5_scan",
    )(proj3, km, wm, vm, lre, lim, dsk, h0re, h0im)


def _s5_matrices(lam_re, lam_im, log_dt, b_re, b_im, c_re, c_im, d_skip):
    g, p = lam_re.shape
    n_gb = g // GROUPS_PER_BLOCK
    gpb = GROUPS_PER_BLOCK
    dt = jnp.exp(log_dt.astype(F32))
    lam = lax.complex(jnp.minimum(lam_re.astype(F32), -1e-4), lam_im.astype(F32))
    lam_bar = jnp.exp(lam * dt[:, None])
    b_bar = ((lam_bar - 1.0) / lam)[..., None] * lax.complex(b_re.astype(F32), b_im.astype(F32))
    c_c = lax.complex(c_re.astype(F32), c_im.astype(F32))
    pows = [jnp.ones_like(lam_bar)]
    for _ in range(S5_SUB):
        pows.append(pows[-1] * lam_bar)
    pw = jnp.stack(pows)
    eye = jnp.eye(gpb, dtype=F32)

    kd = jnp.real(jnp.einsum('gop,dgp,gpi->dgio', c_c, pw[:S5_SUB], b_bar))
    km = jnp.einsum('dbgio,gh->dbgiho', kd.reshape(S5_SUB, n_gb, gpb, SSM_CG, SSM_CG), eye)
    km = km.reshape(S5_SUB, n_gb, LANES, LANES)

    wj = pw[:S5_SUB][::-1][:, :, :, None] * b_bar[None]
    wj = jnp.transpose(wj, (0, 1, 3, 2)).reshape(S5_SUB, n_gb, gpb, SSM_CG, p)
    wre = jnp.einsum('dbgcp,gh->dbgchp', jnp.real(wj), eye).reshape(S5_SUB, n_gb, LANES, gpb * p)
    wim = jnp.einsum('dbgcp,gh->dbgchp', jnp.imag(wj), eye).reshape(S5_SUB, n_gb, LANES, gpb * p)
    wm = jnp.concatenate([wre, wim], axis=-1)

    cl = c_c[None] * pw[1:S5_SUB + 1][:, :, None, :]
    cl = jnp.transpose(cl, (0, 1, 3, 2)).reshape(S5_SUB, n_gb, gpb, p, SSM_CG)
    vre = jnp.einsum('dbgpc,gh->dbgphc', jnp.real(cl), eye).reshape(S5_SUB, n_gb, gpb * p, LANES)
    vim = jnp.einsum('dbgpc,gh->dbgphc', -jnp.imag(cl), eye).reshape(S5_SUB, n_gb, gpb * p, LANES)
    vm = jnp.concatenate([vre, vim], axis=-2)

    lam_s = pw[S5_SUB].reshape(n_gb, 1, gpb * p)
    dsk = d_skip.astype(F32).reshape(n_gb, 1, LANES)
    return (km.astype(BF16), wm.astype(BF16), vm.astype(BF16),
            jnp.real(lam_s), jnp.imag(lam_s), dsk)


def _glu_body(y_ref, w_ref, b_ref, o_ref):
    y = y_ref[...]
    gate = _dot(y.astype(BF16), w_ref[...]) + b_ref[...]
    o_ref[...] = (y * _sigmoid(gate)).astype(BF16)


def _glu(y2, w_bf, b_row, tm):
    m, d = y2.shape
    return pl.pallas_call(
        _glu_body,
        grid=(m // tm,),
        in_specs=[pl.BlockSpec((tm, d), lambda i: (i, 0)),
                  pl.BlockSpec((d, d), lambda i: (0, 0)),
                  pl.BlockSpec((1, d), lambda i: (0, 0))],
        out_specs=pl.BlockSpec((tm, d), lambda i: (i, 0)),
        out_shape=jax.ShapeDtypeStruct((m, d), BF16),
        compiler_params=_cparams(("parallel",)),
        name="s5_glu",
    )(y2, w_bf, b_row)


def _layernorm(v, g, b):
    mu = jnp.mean(v, axis=-1, keepdims=True)
    var = jnp.mean(jnp.square(v - mu), axis=-1, keepdims=True)
    return (v - mu) * lax.rsqrt(var + LN_EPS) * g + b


def _mix_body(og_ref, os_ref, wa_ref, wb_ref, x_ref, g_ref, b_ref, wr_ref, br_ref,
              h_ref, hb_ref, route_ref, cnt_ref, *, alpha):
    @pl.when(pl.program_id(0) == 0)
    def _():
        cnt_ref[...] = jnp.zeros_like(cnt_ref)

    mix = _dot(og_ref[...], wa_ref[...]) + _dot(os_ref[...], wb_ref[...])
    h = _layernorm(alpha * x_ref[...] + mix, g_ref[...], b_ref[...])
    h_ref[...] = h
    tm = h.shape[0]
    _pack_rows(h, hb_ref, tm)

    logits = _dot_x3(h, wr_ref[...]) + br_ref[...]
    lane = lax.broadcasted_iota(jnp.int32, (tm, LANES), 1)
    work = jnp.where(lane < N_EXPERTS, logits, -jnp.inf)
    vals, idxs = [], []
    for _ in range(TOP_K):
        mx = jnp.max(work, axis=-1, keepdims=True)
        ix = jnp.min(jnp.where(work == mx, lane, LANES), axis=-1, keepdims=True)
        vals.append(mx)
        idxs.append(ix)
        work = jnp.where(lane == ix, -jnp.inf, work)
    exps = [jnp.exp(v - vals[0]) for v in vals]
    denom = exps[0]
    for e in exps[1:]:
        denom = denom + e
    chosen = jnp.zeros((tm, LANES), F32)
    for k in range(TOP_K):
        chosen = jnp.where(lane == idxs[k], 1.0, chosen)
    rr = lax.broadcasted_iota(jnp.int32, (tm, tm), 0)
    cc = lax.broadcasted_iota(jnp.int32, (tm, tm), 1)
    lower = jnp.where(rr > cc, 1.0, 0.0).astype(BF16)
    before = cnt_ref[...] + _dot(lower, chosen.astype(BF16))
    cnt_ref[...] = cnt_ref[...] + jnp.sum(chosen, axis=0, keepdims=True)

    route = jnp.zeros((tm, LANES), F32)
    for k in range(TOP_K):
        rank = jnp.sum(jnp.where(lane == idxs[k], before, 0.0), axis=-1, keepdims=True)
        route = jnp.where(lane == k, idxs[k].astype(F32), route)
        route = jnp.where(lane == TOP_K + k, exps[k] / denom, route)
        route = jnp.where(lane == 2 * TOP_K + k, rank, route)
    route_ref[...] = route


def _mix_ln_route(og, osm, wa, wb, x2, g_row, b_row, wr, br, alpha, tm):
    m, d = x2.shape
    dh = og.shape[1]
    body = functools.partial(_mix_body, alpha=alpha)
    row = lambda i: (i, 0)
    fix = lambda i: (0, 0)
    return pl.pallas_call(
        body,
        grid=(m // tm,),
        in_specs=[pl.BlockSpec((tm, dh), row), pl.BlockSpec((tm, dh), row),
                  pl.BlockSpec((dh, d), fix), pl.BlockSpec((dh, d), fix),
                  pl.BlockSpec((tm, d), row),
                  pl.BlockSpec((1, d), fix), pl.BlockSpec((1, d), fix),
                  pl.BlockSpec((d, LANES), fix), pl.BlockSpec((1, LANES), fix)],
        out_specs=[pl.BlockSpec((tm, d), row), pl.BlockSpec((tm * ROW_SUB, LANES), row),
                   pl.BlockSpec((tm, LANES), row), pl.BlockSpec((1, LANES), fix)],
        out_shape=[jax.ShapeDtypeStruct((m, d), F32),
                   jax.ShapeDtypeStruct((m * ROW_SUB, LANES), jnp.uint32),
                   jax.ShapeDtypeStruct((m, LANES), F32),
                   jax.ShapeDtypeStruct((1, LANES), F32)],
        compiler_params=_cparams(("arbitrary",)),
        name="mix_ln_route",
    )(og, osm, wa, wb, x2, g_row, b_row, wr, br)


def _dispatch_body(pos_ref, src_ref, init_ref, dst_ref, sem, *, tt):
    del init_ref
    i = pl.program_id(0)

    def issue(t, carry):
        srow = pl.multiple_of((i * tt + t) * ROW_SUB, ROW_SUB)
        for k in range(TOP_K):
            drow = pl.multiple_of(pos_ref[t * TOP_K + k] * ROW_SUB, ROW_SUB)
            pltpu.make_async_copy(src_ref.at[pl.ds(srow, ROW_SUB)], dst_ref.at[pl.ds(drow, ROW_SUB)], sem).start()
        return carry

    lax.fori_loop(0, tt, issue, 0)
    n = tt * TOP_K * ROW_SUB
    pltpu.make_async_copy(src_ref.at[pl.ds(0, n)], dst_ref.at[pl.ds(0, n)], sem).wait()


def _dispatch_rows(pos_flat, hbp, cap, tt):
    m = hbp.shape[0] // ROW_SUB
    body = functools.partial(_dispatch_body, tt=tt)
    init = jnp.zeros((cap * ROW_SUB, LANES), jnp.uint32)
    return pl.pallas_call(
        body,
        grid=(m // tt,),
        in_specs=[pl.BlockSpec((tt * TOP_K,), lambda i: (i,), memory_space=pltpu.SMEM),
                  pl.BlockSpec(memory_space=pl.ANY), pl.BlockSpec(memory_space=pl.ANY)],
        out_specs=pl.BlockSpec(memory_space=pl.ANY),
        out_shape=jax.ShapeDtypeStruct((cap * ROW_SUB, LANES), jnp.uint32),
        scratch_shapes=[pltpu.SemaphoreType.DMA(())],
        input_output_aliases={2: 0},
        compiler_params=_cparams(("arbitrary",)),
        name="moe_dispatch",
    )(pos_flat, hbp, init)


def _moe_body(te_ref, nu_ref, x_ref, wg_ref, bg_ref, wu_ref, bu_ref, wd_ref, bd_ref,
              o_ref, xb_ref, acc_ref, *, n_f, tm):
    i = pl.program_id(0)
    f = pl.program_id(1)
    used = i < nu_ref[0]
    half = xb_ref.shape[1] // 2

    @pl.when(used & (f == 0))
    def _():
        lo, hi = _unpack_rows(x_ref, 0, tm)
        for s in range(ROW_SUB):
            xb_ref[:, s * LANES:(s + 1) * LANES] = lo[s].astype(BF16)
            xb_ref[:, half + s * LANES:half + (s + 1) * LANES] = hi[s].astype(BF16)

    @pl.when(used)
    def _():
        xb = xb_ref[...]
        hg = jnp.minimum(_dot(xb, wg_ref[0].astype(BF16)) + bg_ref[0], SWIGLU_LIMIT)
        hu = jnp.clip(_dot(xb, wu_ref[0].astype(BF16)) + bu_ref[0], -SWIGLU_LIMIT, SWIGLU_LIMIT)
        hh = (hu + 1.0) * (hg * _sigmoid(SWIGLU_ALPHA * hg))
        part = _dot(hh.astype(BF16), wd_ref[0].astype(BF16))

        @pl.when(f == 0)
        def _():
            acc_ref[...] = part

        @pl.when(f != 0)
        def _():
            acc_ref[...] += part

        @pl.when(f == n_f - 1)
        def _():
            _pack_rows(acc_ref[...] + bd_ref[0], o_ref, tm)

    @pl.when(jnp.logical_not(used) & (f == n_f - 1))
    def _():
        o_ref[...] = jnp.zeros_like(o_ref)


def _moe_ffn(tile_expert, n_used, x_rows, w_gate, b_gate, w_up, b_up, w_down, b_down, tm, tf):
    n_e, d, d_ff = w_gate.shape
    n_tiles = x_rows.shape[0] // (tm * ROW_SUB)
    n_f = d_ff // tf
    body = functools.partial(_moe_body, n_f=n_f, tm=tm)

    def fcol(i, f, nu):
        return jnp.where(i < nu[0], f, n_f - 1)

    grid_spec = pltpu.PrefetchScalarGridSpec(
        num_scalar_prefetch=2,
        grid=(n_tiles, n_f),
        in_specs=[pl.BlockSpec((tm * ROW_SUB, LANES), lambda i, f, te, nu: (i, 0)),
                  pl.BlockSpec((1, d, tf), lambda i, f, te, nu: (te[i], 0, fcol(i, f, nu))),
                  pl.BlockSpec((1, 1, tf), lambda i, f, te, nu: (te[i], 0, fcol(i, f, nu))),
                  pl.BlockSpec((1, d, tf), lambda i, f, te, nu: (te[i], 0, fcol(i, f, nu))),
                  pl.BlockSpec((1, 1, tf), lambda i, f, te, nu: (te[i], 0, fcol(i, f, nu))),
                  pl.BlockSpec((1, tf, d), lambda i, f, te, nu: (te[i], fcol(i, f, nu), 0)),
                  pl.BlockSpec((1, 1, d), lambda i, f, te, nu: (te[i], 0, 0))],
        out_specs=pl.BlockSpec((tm * ROW_SUB, LANES), lambda i, f, te, nu: (i, 0)),
        scratch_shapes=[pltpu.VMEM((tm, d), BF16), pltpu.VMEM((tm, d), F32)],
    )
    return pl.pallas_call(
        body,
        grid_spec=grid_spec,
        out_shape=jax.ShapeDtypeStruct(x_rows.shape, jnp.uint32),
        compiler_params=_cparams(("arbitrary", "arbitrary")),
        name="moe_ffn",
    )(tile_expert, n_used, x_rows, w_gate, b_gate.reshape(n_e, 1, d_ff),
      w_up, b_up.reshape(n_e, 1, d_ff), w_down, b_down.reshape(n_e, 1, d))


def _final_body(pos_cur_ref, pos_nxt_ref, h_ref, route_ref, g_ref, b_ref, src_ref,
                o_ref, buf_ref, sem, *, alpha, tt, n_steps):
    i = pl.program_id(0)
    slot_rows = tt * TOP_K * ROW_SUB

    def issue(pref, slot):
        def one(t, carry):
            for k in range(TOP_K):
                srow = pl.multiple_of(pref[t * TOP_K + k] * ROW_SUB, ROW_SUB)
                drow = pl.multiple_of(slot * slot_rows + (k * tt + t) * ROW_SUB, ROW_SUB)
                pltpu.make_async_copy(src_ref.at[pl.ds(srow, ROW_SUB)], buf_ref.at[pl.ds(drow, ROW_SUB)],
                                      sem.at[slot]).start()
            return carry
        lax.fori_loop(0, tt, one, 0)

    @pl.when(i == 0)
    def _():
        issue(pos_cur_ref, 0)

    @pl.when(i + 1 < n_steps)
    def _():
        issue(pos_nxt_ref, (i + 1) % 2)

    slot = i % 2
    base = pl.multiple_of(slot * slot_rows, ROW_SUB)
    pltpu.make_async_copy(src_ref.at[pl.ds(0, slot_rows)], buf_ref.at[pl.ds(base, slot_rows)], sem.at[slot]).wait()

    ff_lo = [None] * ROW_SUB
    ff_hi = [None] * ROW_SUB
    for k in range(TOP_K):
        gate = route_ref[:, TOP_K + k:TOP_K + k + 1]
        lo, hi = _unpack_rows(buf_ref, base + k * tt * ROW_SUB, tt)
        for s in range(ROW_SUB):
            ff_lo[s] = lo[s] * gate if k == 0 else ff_lo[s] + lo[s] * gate
            ff_hi[s] = hi[s] * gate if k == 0 else ff_hi[s] + hi[s] * gate
    ff = jnp.concatenate(ff_lo + ff_hi, axis=1)
    o_ref[...] = _layernorm(alpha * h_ref[...] + ff, g_ref[...], b_ref[...])


def _combine_ln(pos_flat, h, route, g_row, b_row, outs, alpha, tt):
    m, d = h.shape
    n_steps = m // tt
    body = functools.partial(_final_body, alpha=alpha, tt=tt, n_steps=n_steps)
    row = lambda i: (i, 0)
    fix = lambda i: (0, 0)
    return pl.pallas_call(
        body,
        grid=(n_steps,),
        in_specs=[pl.BlockSpec((tt * TOP_K,), lambda i: (i,), memory_space=pltpu.SMEM),
                  pl.BlockSpec((tt * TOP_K,), lambda i: (jnp.minimum(i + 1, n_steps - 1),),
                               memory_space=pltpu.SMEM),
                  pl.BlockSpec((tt, d), row), pl.BlockSpec((tt, LANES), row),
                  pl.BlockSpec((1, d), fix), pl.BlockSpec((1, d), fix),
                  pl.BlockSpec(memory_space=pl.ANY)],
        out_specs=pl.BlockSpec((tt, d), row),
        out_shape=jax.ShapeDtypeStruct((m, d), F32),
        scratch_shapes=[pltpu.VMEM((2 * tt * TOP_K * ROW_SUB, LANES), jnp.uint32),
                        pltpu.SemaphoreType.DMA((2,))],
        compiler_params=_cparams(("arbitrary",)),
        name="combine_ln",
    )(pos_flat, pos_flat, h, route, g_row, b_row, outs)


def _row_tile(m, pref):
    t = min(m, pref)
    while m % t:
        t //= 2
    return t


def _route_tables(route, counts_row, m, tm):
    n_assign = m * TOP_K
    e_tok = route[:, :TOP_K].astype(jnp.int32)
    rank = route[:, 2 * TOP_K:3 * TOP_K].astype(jnp.int32)
    counts = counts_row[0, :N_EXPERTS].astype(jnp.int32)
    padded = (counts + tm - 1) // tm * tm
    pends = jnp.cumsum(padded)
    pstarts = pends - padded
    pos = pstarts[e_tok] + rank
    n_tiles = -(-n_assign // tm) + N_EXPERTS
    tile_start = jnp.arange(n_tiles, dtype=jnp.int32) * tm
    tile_expert = jnp.minimum(jnp.searchsorted(pends, tile_start, side='right'), N_EXPERTS - 1).astype(jnp.int32)
    n_used = (pends[-1] // tm).astype(jnp.int32).reshape(1)
    last_used = jnp.maximum(n_used[0] - 1, 0)
    tile_expert = jnp.where(tile_start // tm < n_used[0], tile_expert, tile_expert[last_used])
    return pos.reshape(n_assign), tile_expert, n_used


def _layer(x, conv_hist, s_gdn, h_re, h_im, chunk, alpha, p):
    b, t, d = x.shape
    m = b * t
    d_gdn = GDN_HEADS * HEAD_DIM
    x2 = x.reshape(m, d)

    proj = _in_proj(x2, p['w_in'], _row_tile(m, 1024), p['proj_tn'])
    proj3 = proj.reshape(b, t, proj.shape[1])
    conv_new = proj3[:, t - (CONV_W - 1):, :3 * d_gdn]

    q, k, v, cols, gct = _gdn_prep(proj3, conv_hist, p['w_conv'], p['alog_row'], p['dtb_row'],
                                   chunk, p['ab_block'])
    o_gdn, s_new = _gdn(q, k, v, cols, gct, proj3, p['wn_row'], s_gdn, chunk, p['z_block'])

    n_gb = p['s5_mats'][0].shape[1]
    half = GROUPS_PER_BLOCK * SSM_P
    yg, hre_new, him_new = _s5(proj3, p['s5_mats'], h_re.reshape(b, n_gb, 1, half),
                               h_im.reshape(b, n_gb, 1, half), p['u_block0'])
    o_ssm = _glu(yg.reshape(m, yg.shape[2]), p['w_glu'], p['b_glu_row'], _row_tile(m, 512))

    h, hb, route, counts_row = _mix_ln_route(o_gdn.reshape(m, d_gdn), o_ssm, p['w_out_a'], p['w_out_b'], x2,
                                             p['ln1_g'], p['ln1_b'], p['w_router'], p['b_router'], alpha,
                                             _row_tile(m, 256))

    big = m * TOP_K >= 8 * p['moe_tm']
    tm = p['moe_tm'] if big else 128
    pos_flat, tile_expert, n_used = _route_tables(route, counts_row, m, tm)
    cap = tile_expert.shape[0] * tm
    x_rows = _dispatch_rows(pos_flat, hb, cap, _row_tile(m, 256))
    outs = _moe_ffn(tile_expert, n_used, x_rows, p['w_gate'], p['b_gate'],
                    p['w_up'], p['b_up'], p['w_down'], p['b_down'], tm, p['moe_tf'])
    y = _combine_ln(pos_flat, h, route, p['ln2_g'], p['ln2_b'], outs, alpha, _row_tile(m, 128))
    g_all = h_re.shape[1]
    return (y.reshape(b, t, d), conv_new, s_new,
            hre_new.reshape(b, g_all, SSM_P), him_new.reshape(b, g_all, SSM_P))


def _pad_lanes(v, fill=0.0):
    return jnp.pad(v.astype(F32), (0, LANES - v.shape[0]), constant_values=fill).reshape(1, LANES)


def _layer_params(l, w_in, w_conv, a_log, dt_bias, w_onorm, lam_re, lam_im, log_dt, b_re, b_im, c_re, c_im,
                  d_skip, w_glu, b_glu, w_out, ln1_g, ln1_b, w_router, b_router, w_gate, b_gate,
                  w_up, b_up, w_down, b_down, ln2_g, ln2_b):
    d_model = w_in.shape[1]
    d_gdn = GDN_HEADS * HEAD_DIM
    d_qkvz = 4 * d_gdn
    d_ssm = d_model - d_gdn
    wi = w_in[l]
    proj_tn = 896
    n_cols = d_qkvz + d_ssm + LANES
    n_pad = -(-n_cols // proj_tn) * proj_tn
    w_in_r = jnp.concatenate([wi[:, :d_qkvz], wi[:, d_qkvz + 2 * GDN_HEADS:],
                              wi[:, d_qkvz:d_qkvz + 2 * GDN_HEADS],
                              jnp.zeros((d_model, n_pad - d_qkvz - d_ssm - 2 * GDN_HEADS), wi.dtype)], axis=1)
    wo = w_out[l].astype(BF16)
    wr = jnp.pad(w_router[l].astype(F32), ((0, 0), (0, LANES - N_EXPERTS)))
    return {
        'w_in': w_in_r.astype(BF16), 'proj_tn': proj_tn,
        'z_block': 3, 'u_block0': (d_qkvz) // LANES, 'ab_block': (d_qkvz + d_ssm) // LANES,
        'w_conv': w_conv[l].astype(F32),
        'alog_row': _pad_lanes(a_log[l]), 'dtb_row': _pad_lanes(dt_bias[l]),
        'wn_row': w_onorm[l].astype(F32).reshape(1, HEAD_DIM),
        's5_mats': _s5_matrices(lam_re[l], lam_im[l], log_dt[l], b_re[l], b_im[l], c_re[l], c_im[l], d_skip[l]),
        'w_glu': w_glu[l].astype(BF16), 'b_glu_row': b_glu[l].astype(F32).reshape(1, d_ssm),
        'w_out_a': wo[:d_gdn], 'w_out_b': wo[d_gdn:],
        'ln1_g': ln1_g[l].astype(F32).reshape(1, d_model), 'ln1_b': ln1_b[l].astype(F32).reshape(1, d_model),
        'w_router': wr, 'b_router': _pad_lanes(b_router[l]),
        'w_gate': w_gate[l], 'b_gate': b_gate[l], 'w_up': w_up[l], 'b_up': b_up[l],
        'w_down': w_down[l], 'b_down': b_down[l],
        'ln2_g': ln2_g[l].astype(F32).reshape(1, d_model), 'ln2_b': ln2_b[l].astype(F32).reshape(1, d_model),
        'moe_tm': 512, 'moe_tf': 512,
    }


def kernel(x_prompt, x_sample, state_conv, state_gdn, state_ssm_re, state_ssm_im, w_in, w_conv, a_log, dt_bias, w_onorm, lam_re, lam_im, log_dt, b_re, b_im, c_re, c_im, d_skip, w_glu, b_glu, w_out, ln1_g, ln1_b, w_router, b_router, w_gate, b_gate, w_up, b_up, w_down, b_down, ln2_g, ln2_b):
    depth = w_in.shape[0]
    alpha = (2.0 * depth) ** 0.25
    bp, seq, _ = x_prompt.shape
    chunk_p = 64
    d_qkv = state_conv.shape[-1]
    n_groups, n_p = state_ssm_re.shape[-2:]
    yp, ys = x_prompt, x_sample
    outs_p = [[], [], [], []]
    outs_s = [[], [], [], []]
    for l in range(depth):
        p = _layer_params(l, w_in, w_conv, a_log, dt_bias, w_onorm, lam_re, lam_im, log_dt, b_re, b_im,
                          c_re, c_im, d_skip, w_glu, b_glu, w_out, ln1_g, ln1_b, w_router, b_router,
                          w_gate, b_gate, w_up, b_up, w_down, b_down, ln2_g, ln2_b)
        yp, cp, sp, rp, ip = _layer(
            yp, jnp.zeros((bp, CONV_W - 1, d_qkv), F32),
            jnp.zeros((bp, GDN_HEADS, HEAD_DIM, HEAD_DIM), F32),
            jnp.zeros((bp, n_groups, n_p), F32), jnp.zeros((bp, n_groups, n_p), F32),
            chunk_p, alpha, p)
        ys, cs, ss, rs, is_ = _layer(
            ys, state_conv[l].astype(F32), state_gdn[l].astype(F32),
            state_ssm_re[l].astype(F32), state_ssm_im[l].astype(F32),
            ys.shape[1], alpha, p)
        for acc, val in zip(outs_p, (cp, sp, rp, ip)):
            acc.append(val)
        for acc, val in zip(outs_s, (cs, ss, rs, is_)):
            acc.append(val)
    return (yp, ys, *[jnp.stack(a) for a in outs_p], *[jnp.stack(a) for a in outs_s])
```

```python
import functools
import math

import jax
import jax.numpy as jnp
from jax import lax
from jax.experimental import pallas as pl
from jax.experimental.pallas import tpu as pltpu

F32 = jnp.float32
BF16 = jnp.bfloat16

GDN_HEADS = 8
HEAD_DIM = 128
CONV_W = 4
SSM_CG = 16
SSM_P = 64
N_EXPERTS = 32
TOP_K = 4
SWIGLU_ALPHA = 1.702
SWIGLU_LIMIT = 7.0
LN_EPS = 1e-5
RMS_EPS = 1e-6
L2_EPS = 1e-6

LANES = 128
SUBLANES = 8
S5_SUB = 8
S5_TILE = 2048
GDN_TILE = 128
GROUPS_PER_BLOCK = LANES // SSM_CG
VMEM_LIMIT = 56 * 1024 * 1024

NT_DIMS = (((1,), (1,)), ((), ()))
TN_DIMS = (((0,), (0,)), ((), ()))


def _dot(a, b, dims=(((1,), (0,)), ((), ()))):
    return lax.dot_general(a, b, dims, preferred_element_type=F32)


def _split(a):
    hi = a.astype(BF16)
    lo = (a - hi.astype(F32)).astype(BF16)
    return hi, lo


def _dot_x3(a, b, dims=(((1,), (0,)), ((), ()))):
    ah, al = _split(a)
    bh, bl = _split(b)
    return _dot(ah, bh, dims) + (_dot(ah, bl, dims) + _dot(al, bh, dims))


def _dot_bf(a, b, dims=(((1,), (0,)), ((), ()))):
    return _dot(a.astype(BF16), b.astype(BF16), dims)


_gdn_mm = _dot_bf


def _sigmoid(x):
    return 1.0 / (1.0 + jnp.exp(-x))


ROW_SUB = SUBLANES


def _pack_rows(x, ref, tm):
    half = x.shape[1] // 2
    for s in range(ROW_SUB):
        lo = x[:, s * LANES:(s + 1) * LANES].astype(BF16).astype(F32)
        hi = x[:, half + s * LANES:half + (s + 1) * LANES].astype(BF16).astype(F32)
        word = (lax.bitcast_convert_type(lo, jnp.uint32) >> 16) | lax.bitcast_convert_type(hi, jnp.uint32)
        ref[pl.ds(s, tm, stride=ROW_SUB), :] = word


def _unpack_rows(ref, base, tm):
    lo, hi = [], []
    for s in range(ROW_SUB):
        word = ref[pl.ds(base + s, tm, stride=ROW_SUB), :]
        lo.append(lax.bitcast_convert_type(word << 16, F32))
        hi.append(lax.bitcast_convert_type(word & jnp.uint32(0xFFFF0000), F32))
    return lo, hi


def _cparams(sem):
    return pltpu.CompilerParams(dimension_semantics=sem, vmem_limit_bytes=VMEM_LIMIT)


def _proj_body(x_ref, w_ref, o_ref, xb_ref):
    @pl.when(pl.program_id(1) == 0)
    def _():
        xb_ref[...] = x_ref[...].astype(BF16)

    o_ref[...] = _dot(xb_ref[...], w_ref[...])


def _in_proj(x2, w_bf, tm, tn):
    m, k = x2.shape
    n = w_bf.shape[1]
    return pl.pallas_call(
        _proj_body,
        grid=(m // tm, n // tn),
        in_specs=[pl.BlockSpec((tm, k), lambda i, j: (i, 0)),
                  pl.BlockSpec((k, tn), lambda i, j: (0, j))],
        out_specs=pl.BlockSpec((tm, tn), lambda i, j: (i, j)),
        out_shape=jax.ShapeDtypeStruct((m, n), F32),
        scratch_shapes=[pltpu.VMEM((tm, k), BF16)],
        compiler_params=_cparams(("parallel", "arbitrary")),
        name="in_proj",
    )(x2, w_bf)


def _gdn_prep_body(qkv_ref, ab_ref, hist_ref, wc_ref, alog_ref, dtb_ref,
                   q_ref, k_ref, v_ref, cols_ref, gct_ref, xbuf_ref, *, tt, chunk):
    d_gdn = GDN_HEADS * HEAD_DIM
    halo = SUBLANES

    @pl.when(pl.program_id(1) == 0)
    def _():
        xbuf_ref[0:halo, :] = jnp.zeros((halo, 3 * d_gdn), F32)
        xbuf_ref[halo - (CONV_W - 1):halo, :] = hist_ref[0]

    xbuf_ref[halo:halo + tt, :] = qkv_ref[0]

    for part, out_ref in enumerate((q_ref, k_ref, v_ref)):
        c0 = part * d_gdn
        y = None
        for j in range(CONV_W):
            r0 = halo - (CONV_W - 1) + j
            term = xbuf_ref[r0:r0 + tt, c0:c0 + d_gdn] * wc_ref[j:j + 1, c0:c0 + d_gdn]
            y = term if y is None else y + term
        s = y * _sigmoid(y)
        if part == 2:
            out_ref[0] = s
        else:
            for h in range(GDN_HEADS):
                sh = s[:, h * HEAD_DIM:(h + 1) * HEAD_DIM]
                nrm = sh * lax.rsqrt(jnp.sum(sh * sh, axis=-1, keepdims=True) + L2_EPS)
                if part == 0:
                    nrm = nrm * (HEAD_DIM ** -0.5)
                out_ref[0, :, h * HEAD_DIM:(h + 1) * HEAD_DIM] = nrm

    xbuf_ref[0:halo, :] = xbuf_ref[tt:tt + halo, :]

    ab = ab_ref[0]
    lane = lax.broadcasted_iota(jnp.int32, (tt, LANES), 1)
    is_a = lane < GDN_HEADS
    z = ab + dtb_ref[...]
    softplus = jnp.maximum(z, 0.0) + jnp.log1p(jnp.exp(-jnp.abs(z)))
    g = jnp.where(is_a, -jnp.exp(alog_ref[...]) * softplus, 0.0)
    beta = _sigmoid(ab)

    shift = int(math.log2(chunk))
    r = lax.broadcasted_iota(jnp.int32, (tt, tt), 0)
    c = lax.broadcasted_iota(jnp.int32, (tt, tt), 1)
    same = (r >> shift) == (c >> shift)
    m_incl = jnp.where(same & (r >= c), 1.0, 0.0).astype(BF16)
    m_all = jnp.where(same, 1.0, 0.0).astype(BF16)
    gc = _dot_exact_lhs_rhs(m_incl, g)
    glast = _dot_exact_lhs_rhs(m_all, g)
    eg = jnp.exp(gc)
    egl = jnp.exp(glast - gc)
    egt = jnp.exp(glast)
    zero = jnp.zeros_like(gc)
    cols = (jnp.where(is_a, gc, zero)
            + jnp.where((lane >= 8) & (lane < 16), beta, zero)
            + pltpu.roll(jnp.where(is_a, eg, zero), 16, 1)
            + pltpu.roll(jnp.where(is_a, egl, zero), 24, 1)
            + pltpu.roll(jnp.where(is_a, egt, zero), 32, 1))
    cols_ref[0] = cols

    er = lax.broadcasted_iota(jnp.int32, (SUBLANES, LANES), 0)
    ec = lax.broadcasted_iota(jnp.int32, (SUBLANES, LANES), 1)
    sel = jnp.where(er == ec, 1.0, 0.0).astype(BF16)
    gct_ref[0] = _dot_exact_rhs(sel, jnp.where(is_a, gc, zero), NT_DIMS)


def _three_pieces(a):
    a0 = a.astype(BF16)
    r1 = a - a0.astype(F32)
    a1 = r1.astype(BF16)
    a2 = (r1 - a1.astype(F32)).astype(BF16)
    return a0, a1, a2


def _dot_exact_lhs_rhs(mask_bf, a):
    a0, a1, a2 = _three_pieces(a)
    return _dot(mask_bf, a0) + (_dot(mask_bf, a1) + _dot(mask_bf, a2))


def _dot_exact_rhs(mask_bf, a, dims):
    a0, a1, a2 = _three_pieces(a)
    return _dot(mask_bf, a0, dims) + (_dot(mask_bf, a1, dims) + _dot(mask_bf, a2, dims))


def _gdn_prep(proj3, hist, w_conv, alog_p, dtb_p, chunk, ab_block):
    b, t, _ = proj3.shape
    d_gdn = GDN_HEADS * HEAD_DIM
    tt = min(t, GDN_TILE)
    body = functools.partial(_gdn_prep_body, tt=tt, chunk=chunk)
    big = jax.ShapeDtypeStruct((b, t, d_gdn), F32)
    return pl.pallas_call(
        body,
        grid=(b, t // tt),
        in_specs=[pl.BlockSpec((1, tt, 3 * d_gdn), lambda i, j: (i, j, 0)),
                  pl.BlockSpec((1, tt, LANES), lambda i, j: (i, j, ab_block)),
                  pl.BlockSpec((1, CONV_W - 1, 3 * d_gdn), lambda i, j: (i, 0, 0)),
                  pl.BlockSpec((CONV_W, 3 * d_gdn), lambda i, j: (0, 0)),
                  pl.BlockSpec((1, LANES), lambda i, j: (0, 0)),
                  pl.BlockSpec((1, LANES), lambda i, j: (0, 0))],
        out_specs=[pl.BlockSpec((1, tt, d_gdn), lambda i, j: (i, j, 0)),
                   pl.BlockSpec((1, tt, d_gdn), lambda i, j: (i, j, 0)),
                   pl.BlockSpec((1, tt, d_gdn), lambda i, j: (i, j, 0)),
                   pl.BlockSpec((1, tt, LANES), lambda i, j: (i, j, 0)),
                   pl.BlockSpec((1, SUBLANES, tt), lambda i, j: (i, 0, j))],
        out_shape=[big, big, big,
                   jax.ShapeDtypeStruct((b, t, LANES), F32),
                   jax.ShapeDtypeStruct((b, SUBLANES, t), F32)],
        scratch_shapes=[pltpu.VMEM((tt + SUBLANES, 3 * d_gdn), F32)],
        compiler_params=_cparams(("parallel", "arbitrary")),
        name="gdn_prep",
    )(proj3, proj3, hist, w_conv, alog_p, dtb_p)


def _unit_lower_inverses(mats, r, c, chunk):
    base = 16
    eye = jnp.where(r == c, 1.0, 0.0)

    def blk(bs):
        s = int(math.log2(bs))
        return (r >> s) == (c >> s)

    d1 = [jnp.where(blk(base), a, 0.0) for a in mats]
    d2 = [_gdn_mm(x, x) for x in d1]
    d4 = [_gdn_mm(x, x) for x in d2]
    d8 = [_gdn_mm(x, x) for x in d4]
    t = [eye - x for x in d1]
    t = [x + _gdn_mm(x, y) for x, y in zip(t, d2)]
    t = [x + _gdn_mm(x, y) for x, y in zip(t, d4)]
    t = [x + _gdn_mm(x, y) for x, y in zip(t, d8)]
    bs = base
    while bs < chunk:
        off_mask = blk(2 * bs) & jnp.logical_not(blk(bs))
        inner = [_gdn_mm(jnp.where(off_mask, a, 0.0), x) for a, x in zip(mats, t)]
        t = [x - _gdn_mm(x, y) for x, y in zip(t, inner)]
        bs *= 2
    return t


def _gdn_body(q_ref, k_ref, v_ref, cols_ref, gct_ref, z_ref, wn_ref, s0_ref,
              o_ref, s_ref, *, tb, chunk):
    @pl.when(pl.program_id(1) == 0)
    def _():
        s_ref[...] = s0_ref[...]

    shift = int(math.log2(chunk))
    r = lax.broadcasted_iota(jnp.int32, (tb, tb), 0)
    c = lax.broadcasted_iota(jnp.int32, (tb, tb), 1)
    same = (r >> shift) == (c >> shift)
    incl = same & (r >= c)
    strict = same & (r > c)
    n_chunks = tb // chunk

    heads = range(GDN_HEADS)
    hsl = [slice(h * HEAD_DIM, (h + 1) * HEAD_DIM) for h in heads]
    q = [q_ref[0, :, s] for s in hsl]
    k = [k_ref[0, :, s] for s in hsl]
    beta = [cols_ref[0, :, 8 + h:9 + h] for h in heads]
    eg = [cols_ref[0, :, 16 + h:17 + h] for h in heads]
    egl = [cols_ref[0, :, 24 + h:25 + h] for h in heads]
    decay = [jnp.exp(jnp.where(incl, cols_ref[0, :, h:h + 1] - gct_ref[0, h:h + 1, :], -jnp.inf)) for h in heads]
    kb = [k[h] * beta[h] for h in heads]
    k_bf = [x.astype(BF16) for x in k]
    a = [jnp.where(strict, _dot(kb[h].astype(BF16), k_bf[h], NT_DIMS) * decay[h], 0.0) for h in heads]
    tinv = _unit_lower_inverses(a, r, c, chunk)
    sol = [_gdn_mm(tinv[h], jnp.concatenate([v_ref[0, :, hsl[h]] * beta[h], kb[h] * eg[h]], axis=1)) for h in heads]
    u = [x[:, :HEAD_DIM] for x in sol]
    w_bf = [x[:, HEAD_DIM:].astype(BF16) for x in sol]
    attn = [(_dot(q[h].astype(BF16), k_bf[h], NT_DIMS) * decay[h]).astype(BF16) for h in heads]
    q_dec = [(q[h] * eg[h]).astype(BF16) for h in heads]
    k_dec = [(k[h] * egl[h]).astype(BF16) for h in heads]

    s = [s_ref[0, h] for h in heads]
    v_new = [[] for _ in heads]
    o_state = [[] for _ in heads]
    for ci in range(n_chunks):
        rs = slice(ci * chunk, (ci + 1) * chunk)
        s_bf = [x.astype(BF16) for x in s]
        vn = [u[h][rs] - _dot(w_bf[h][rs], s_bf[h]) for h in heads]
        for h in heads:
            o_state[h].append(_dot(q_dec[h][rs], s_bf[h]))
            v_new[h].append(vn[h])
        s = [s[h] * cols_ref[0, ci * chunk:ci * chunk + 1, 32 + h:33 + h]
             + _dot(k_dec[h][rs], vn[h].astype(BF16), TN_DIMS) for h in heads]
    for h in heads:
        s_ref[0, h] = s[h]

    def cat(parts):
        return parts[0] if len(parts) == 1 else jnp.concatenate(parts, axis=0)

    for h in heads:
        o = cat(o_state[h]) + _dot(attn[h], cat(v_new[h]).astype(BF16))
        zh = z_ref[0, :, hsl[h]]
        o = (o * lax.rsqrt(jnp.mean(o * o, axis=-1, keepdims=True) + RMS_EPS) * wn_ref[...]
             * (zh * _sigmoid(zh)))
        o_ref[0, :, hsl[h]] = o.astype(BF16)


def _gdn(q, k, v, cols, gct, proj3, wn, s0, chunk, z_block):
    b, t, d_gdn = q.shape
    tb = min(t, GDN_TILE)
    body = functools.partial(_gdn_body, tb=tb, chunk=chunk)
    tile = lambda i, j: (i, j, 0)
    return pl.pallas_call(
        body,
        grid=(b, t // tb),
        in_specs=[pl.BlockSpec((1, tb, d_gdn), tile),
                  pl.BlockSpec((1, tb, d_gdn), tile),
                  pl.BlockSpec((1, tb, d_gdn), tile),
                  pl.BlockSpec((1, tb, LANES), tile),
                  pl.BlockSpec((1, SUBLANES, tb), lambda i, j: (i, 0, j)),
                  pl.BlockSpec((1, tb, d_gdn), lambda i, j: (i, j, z_block)),
                  pl.BlockSpec((1, HEAD_DIM), lambda i, j: (0, 0)),
                  pl.BlockSpec((1, GDN_HEADS, HEAD_DIM, HEAD_DIM), lambda i, j: (i, 0, 0, 0))],
        out_specs=[pl.BlockSpec((1, tb, d_gdn), tile),
                   pl.BlockSpec((1, GDN_HEADS, HEAD_DIM, HEAD_DIM), lambda i, j: (i, 0, 0, 0))],
        out_shape=[jax.ShapeDtypeStruct((b, t, d_gdn), BF16),
                   jax.ShapeDtypeStruct((b, GDN_HEADS, HEAD_DIM, HEAD_DIM), F32)],
        compiler_params=_cparams(("parallel", "arbitrary")),
        name="gdn_delta",
    )(q, k, v, cols, gct, proj3, wn, s0)


def _gelu_tanh(x):
    return 0.5 * x * (1.0 + jnp.tanh(math.sqrt(2.0 / math.pi) * (x + 0.044715 * (x * x * x))))


def _s5_body(u_ref, km_ref, wm_ref, vm_ref, lre_ref, lim_ref, dsk_ref, h0re_ref, h0im_ref,
             y_ref, hre_ref, him_ref, xbuf_ref, hbuf_ref, *, n_sub):
    half = GROUPS_PER_BLOCK * SSM_P

    @pl.when(pl.program_id(2) == 0)
    def _():
        hre_ref[...] = h0re_ref[...]
        him_ref[...] = h0im_ref[...]

    u_f = [u_ref[0, pl.ds(j, n_sub, stride=S5_SUB), :] for j in range(S5_SUB)]
    u_b = [x.astype(BF16) for x in u_f]

    x = _dot(u_b[0], wm_ref[0, 0])
    for j in range(1, S5_SUB):
        x = x + _dot(u_b[j], wm_ref[j, 0])
    xbuf_ref[...] = x

    lre = lre_ref[0]
    lim = lim_ref[0]

    def step(n, carry):
        hre, him = carry
        hbuf_ref[pl.ds(n, 1), 0:half] = hre
        hbuf_ref[pl.ds(n, 1), half:2 * half] = him
        xr = xbuf_ref[pl.ds(n, 1), 0:half]
        xi = xbuf_ref[pl.ds(n, 1), half:2 * half]
        return (lre * hre - lim * him + xr, lre * him + lim * hre + xi)

    hre, him = lax.fori_loop(0, n_sub, step, (hre_ref[0, 0], him_ref[0, 0]))
    hre_ref[0, 0] = hre
    him_ref[0, 0] = him

    h_b = hbuf_ref[...].astype(BF16)
    dsk = dsk_ref[0]
    for l in range(S5_SUB):
        y = _dot(h_b, vm_ref[l, 0])
        for d in range(l + 1):
            y = y + _dot(u_b[l - d], km_ref[d, 0])
        y = y + dsk * u_f[l]
        y_ref[0, pl.ds(l, n_sub, stride=S5_SUB), :] = _gelu_tanh(y)


def _s5(proj3, mats, h0re, h0im, u_block0):
    km, wm, vm, lre, lim, dsk = mats
    b, t, _ = proj3.shape
    n_gb = km.shape[1]
    tt = min(t, S5_TILE)
    n_sub = tt // S5_SUB
    half = GROUPS_PER_BLOCK * SSM_P
    body = functools.partial(_s5_body, n_sub=n_sub)
    state_spec = pl.BlockSpec((1, 1, 1, half), lambda g, i, j: (i, g, 0, 0))
    par_spec = pl.BlockSpec((1, 1, half), lambda g, i, j: (g, 0, 0))
    return pl.pallas_call(
        body,
        grid=(n_gb, b, t // tt),
        in_specs=[pl.BlockSpec((1, tt, LANES), lambda g, i, j: (i, j, u_block0 + g)),
                  pl.BlockSpec((S5_SUB, 1, LANES, LANES), lambda g, i, j: (0, g, 0, 0)),
                  pl.BlockSpec((S5_SUB, 1, LANES, 2 * half), lambda g, i, j: (0, g, 0, 0)),
                  pl.BlockSpec((S5_SUB, 1, 2 * half, LANES), lambda g, i, j: (0, g, 0, 0)),
                  par_spec, par_spec,
                  pl.BlockSpec((1, 1, LANES), lambda g, i, j: (g, 0, 0)),
                  state_spec, state_spec],
        out_specs=[pl.BlockSpec((1, tt, LANES), lambda g, i, j: (i, j, g)),
                   state_spec, state_spec],
        out_shape=[jax.ShapeDtypeStruct((b, t, n_gb * LANES), F32),
                   jax.ShapeDtypeStruct((b, n_gb, 1, half), F32),
                   jax.ShapeDtypeStruct((b, n_gb, 1, half), F32)],
        scratch_shapes=[pltpu.VMEM((n_sub, 2 * half), F32),
                        pltpu.VMEM((n_sub, 2 * half), F32)],
        compiler_params=_cparams(("parallel", "parallel", "arbitrary")),
        name="s5_scan",
    )(proj3, km, wm, vm, lre, lim, dsk, h0re, h0im)


def _s5_matrices(lam_re, lam_im, log_dt, b_re, b_im, c_re, c_im, d_skip):
    g, p = lam_re.shape
    n_gb = g // GROUPS_PER_BLOCK
    gpb = GROUPS_PER_BLOCK
    dt = jnp.exp(log_dt.astype(F32))
    lam = lax.complex(jnp.minimum(lam_re.astype(F32), -1e-4), lam_im.astype(F32))
    lam_bar = jnp.exp(lam * dt[:, None])
    b_bar = ((lam_bar - 1.0) / lam)[..., None] * lax.complex(b_re.astype(F32), b_im.astype(F32))
    c_c = lax.complex(c_re.astype(F32), c_im.astype(F32))
    pows = [jnp.ones_like(lam_bar)]
    for _ in range(S5_SUB):
        pows.append(pows[-1] * lam_bar)
    pw = jnp.stack(pows)
    eye = jnp.eye(gpb, dtype=F32)

    kd = jnp.real(jnp.einsum('gop,dgp,gpi->dgio', c_c, pw[:S5_SUB], b_bar))
    km = jnp.einsum('dbgio,gh->dbgiho', kd.reshape(S5_SUB, n_gb, gpb, SSM_CG, SSM_CG), eye)
    km = km.reshape(S5_SUB, n_gb, LANES, LANES)

    wj = pw[:S5_SUB][::-1][:, :, :, None] * b_bar[None]
    wj = jnp.transpose(wj, (0, 1, 3, 2)).reshape(S5_SUB, n_gb, gpb, SSM_CG, p)
    wre = jnp.einsum('dbgcp,gh->dbgchp', jnp.real(wj), eye).reshape(S5_SUB, n_gb, LANES, gpb * p)
    wim = jnp.einsum('dbgcp,gh->dbgchp', jnp.imag(wj), eye).reshape(S5_SUB, n_gb, LANES, gpb * p)
    wm = jnp.concatenate([wre, wim], axis=-1)

    cl = c_c[None] * pw[1:S5_SUB + 1][:, :, None, :]
    cl = jnp.transpose(cl, (0, 1, 3, 2)).reshape(S5_SUB, n_gb, gpb, p, SSM_CG)
    vre = jnp.einsum('dbgpc,gh->dbgphc', jnp.real(cl), eye).reshape(S5_SUB, n_gb, gpb * p, LANES)
    vim = jnp.einsum('dbgpc,gh->dbgphc', -jnp.imag(cl), eye).reshape(S5_SUB, n_gb, gpb * p, LANES)
    vm = jnp.concatenate([vre, vim], axis=-2)

    lam_s = pw[S5_SUB].reshape(n_gb, 1, gpb * p)
    dsk = d_skip.astype(F32).reshape(n_gb, 1, LANES)
    return (km.astype(BF16), wm.astype(BF16), vm.astype(BF16),
            jnp.real(lam_s), jnp.imag(lam_s), dsk)


def _glu_body(y_ref, w_ref, b_ref, o_ref):
    y = y_ref[...]
    gate = _dot(y.astype(BF16), w_ref[...]) + b_ref[...]
    o_ref[...] = (y * _sigmoid(gate)).astype(BF16)


def _glu(y2, w_bf, b_row, tm):
    m, d = y2.shape
    return pl.pallas_call(
        _glu_body,
        grid=(m // tm,),
        in_specs=[pl.BlockSpec((tm, d), lambda i: (i, 0)),
                  pl.BlockSpec((d, d), lambda i: (0, 0)),
                  pl.BlockSpec((1, d), lambda i: (0, 0))],
        out_specs=pl.BlockSpec((tm, d), lambda i: (i, 0)),
        out_shape=jax.ShapeDtypeStruct((m, d), BF16),
        compiler_params=_cparams(("parallel",)),
        name="s5_glu",
    )(y2, w_bf, b_row)


def _layernorm(v, g, b):
    mu = jnp.mean(v, axis=-1, keepdims=True)
    var = jnp.mean(jnp.square(v - mu), axis=-1, keepdims=True)
    return (v - mu) * lax.rsqrt(var + LN_EPS) * g + b


def _mix_body(og_ref, os_ref, wa_ref, wb_ref, x_ref, g_ref, b_ref, wr_ref, br_ref,
              h_ref, hb_ref, route_ref, cnt_ref, *, alpha):
    @pl.when(pl.program_id(0) == 0)
    def _():
        cnt_ref[...] = jnp.zeros_like(cnt_ref)

    mix = _dot(og_ref[...], wa_ref[...]) + _dot(os_ref[...], wb_ref[...])
    h = _layernorm(alpha * x_ref[...] + mix, g_ref[...], b_ref[...])
    h_ref[...] = h
    tm = h.shape[0]
    _pack_rows(h, hb_ref, tm)

    logits = _dot_x3(h, wr_ref[...]) + br_ref[...]
    lane = lax.broadcasted_iota(jnp.int32, (tm, LANES), 1)
    work = jnp.where(lane < N_EXPERTS, logits, -jnp.inf)
    vals, idxs = [], []
    for _ in range(TOP_K):
        mx = jnp.max(work, axis=-1, keepdims=True)
        ix = jnp.min(jnp.where(work == mx, lane, LANES), axis=-1, keepdims=True)
        vals.append(mx)
        idxs.append(ix)
        work = jnp.where(lane == ix, -jnp.inf, work)
    exps = [jnp.exp(v - vals[0]) for v in vals]
    denom = exps[0]
    for e in exps[1:]:
        denom = denom + e
    chosen = jnp.zeros((tm, LANES), F32)
    for k in range(TOP_K):
        chosen = jnp.where(lane == idxs[k], 1.0, chosen)
    rr = lax.broadcasted_iota(jnp.int32, (tm, tm), 0)
    cc = lax.broadcasted_iota(jnp.int32, (tm, tm), 1)
    lower = jnp.where(rr > cc, 1.0, 0.0).astype(BF16)
    before = cnt_ref[...] + _dot(lower, chosen.astype(BF16))
    cnt_ref[...] = cnt_ref[...] + jnp.sum(chosen, axis=0, keepdims=True)

    route = jnp.zeros((tm, LANES), F32)
    for k in range(TOP_K):
        rank = jnp.sum(jnp.where(lane == idxs[k], before, 0.0), axis=-1, keepdims=True)
        route = jnp.where(lane == k, idxs[k].astype(F32), route)
        route = jnp.where(lane == TOP_K + k, exps[k] / denom, route)
        route = jnp.where(lane == 2 * TOP_K + k, rank, route)
    route_ref[...] = route


def _mix_ln_route(og, osm, wa, wb, x2, g_row, b_row, wr, br, alpha, tm):
    m, d = x2.shape
    dh = og.shape[1]
    body = functools.partial(_mix_body, alpha=alpha)
    row = lambda i: (i, 0)
    fix = lambda i: (0, 0)
    return pl.pallas_call(
        body,
        grid=(m // tm,),
        in_specs=[pl.BlockSpec((tm, dh), row), pl.BlockSpec((tm, dh), row),
                  pl.BlockSpec((dh, d), fix), pl.BlockSpec((dh, d), fix),
                  pl.BlockSpec((tm, d), row),
                  pl.BlockSpec((1, d), fix), pl.BlockSpec((1, d), fix),
                  pl.BlockSpec((d, LANES), fix), pl.BlockSpec((1, LANES), fix)],
        out_specs=[pl.BlockSpec((tm, d), row), pl.BlockSpec((tm * ROW_SUB, LANES), row),
                   pl.BlockSpec((tm, LANES), row), pl.BlockSpec((1, LANES), fix)],
        out_shape=[jax.ShapeDtypeStruct((m, d), F32),
                   jax.ShapeDtypeStruct((m * ROW_SUB, LANES), jnp.uint32),
                   jax.ShapeDtypeStruct((m, LANES), F32),
                   jax.ShapeDtypeStruct((1, LANES), F32)],
        compiler_params=_cparams(("arbitrary",)),
        name="mix_ln_route",
    )(og, osm, wa, wb, x2, g_row, b_row, wr, br)


def _dispatch_body(pos_ref, src_ref, init_ref, dst_ref, sem, *, tt):
    del init_ref

    def issue(t, carry):
        srow = pl.multiple_of(t * ROW_SUB, ROW_SUB)
        for k in range(TOP_K):
            drow = pl.multiple_of(pos_ref[t * TOP_K + k] * ROW_SUB, ROW_SUB)
            pltpu.make_async_copy(src_ref.at[pl.ds(srow, ROW_SUB)], dst_ref.at[pl.ds(drow, ROW_SUB)], sem).start()
        return carry

    lax.fori_loop(0, tt, issue, 0)
    n = tt * ROW_SUB
    for _ in range(TOP_K):
        pltpu.make_async_copy(src_ref.at[pl.ds(0, n)], dst_ref.at[pl.ds(0, n)], sem).wait()


def _dispatch_rows(pos_flat, hbp, cap, tt):
    m = hbp.shape[0] // ROW_SUB
    body = functools.partial(_dispatch_body, tt=tt)
    init = jnp.zeros((cap * ROW_SUB, LANES), jnp.uint32)
    return pl.pallas_call(
        body,
        grid=(m // tt,),
        in_specs=[pl.BlockSpec((tt * TOP_K,), lambda i: (i,), memory_space=pltpu.SMEM),
                  pl.BlockSpec((tt * ROW_SUB, LANES), lambda i: (i, 0)), pl.BlockSpec(memory_space=pl.ANY)],
        out_specs=pl.BlockSpec(memory_space=pl.ANY),
        out_shape=jax.ShapeDtypeStruct((cap * ROW_SUB, LANES), jnp.uint32),
        scratch_shapes=[pltpu.SemaphoreType.DMA(())],
        input_output_aliases={2: 0},
        compiler_params=_cparams(("arbitrary",)),
        name="moe_dispatch",
    )(pos_flat, hbp, init)


def _moe_body(te_ref, nu_ref, x_ref, wg_ref, bg_ref, wu_ref, bu_ref, wd_ref, bd_ref,
              o_ref, xb_ref, acc_ref, *, n_f, tm):
    i = pl.program_id(0)
    f = pl.program_id(1)
    used = i < nu_ref[0]
    half = xb_ref.shape[1] // 2

    @pl.when(used & (f == 0))
    def _():
        lo, hi = _unpack_rows(x_ref, 0, tm)
        for s in range(ROW_SUB):
            xb_ref[:, s * LANES:(s + 1) * LANES] = lo[s].astype(BF16)
            xb_ref[:, half + s * LANES:half + (s + 1) * LANES] = hi[s].astype(BF16)

    @pl.when(used)
    def _():
        xb = xb_ref[...]
        hg = jnp.minimum(_dot(xb, wg_ref[0].astype(BF16)) + bg_ref[0], SWIGLU_LIMIT)
        hu = jnp.clip(_dot(xb, wu_ref[0].astype(BF16)) + bu_ref[0], -SWIGLU_LIMIT, SWIGLU_LIMIT)
        hh = (hu + 1.0) * (hg * _sigmoid(SWIGLU_ALPHA * hg))
        part = _dot(hh.astype(BF16), wd_ref[0].astype(BF16))

        @pl.when(f == 0)
        def _():
            acc_ref[...] = part

        @pl.when(f != 0)
        def _():
            acc_ref[...] += part

        @pl.when(f == n_f - 1)
        def _():
            _pack_rows(acc_ref[...] + bd_ref[0], o_ref, tm)

    @pl.when(jnp.logical_not(used) & (f == n_f - 1))
    def _():
        o_ref[...] = jnp.zeros_like(o_ref)


def _moe_ffn(tile_expert, n_used, x_rows, w_gate, b_gate, w_up, b_up, w_down, b_down, tm, tf):
    n_e, d, d_ff = w_gate.shape
    n_tiles = x_rows.shape[0] // (tm * ROW_SUB)
    n_f = d_ff // tf
    body = functools.partial(_moe_body, n_f=n_f, tm=tm)

    def fcol(i, f, nu):
        return jnp.where(i < nu[0], f, n_f - 1)

    grid_spec = pltpu.PrefetchScalarGridSpec(
        num_scalar_prefetch=2,
        grid=(n_tiles, n_f),
        in_specs=[pl.BlockSpec((tm * ROW_SUB, LANES), lambda i, f, te, nu: (i, 0)),
                  pl.BlockSpec((1, d, tf), lambda i, f, te, nu: (te[i], 0, fcol(i, f, nu))),
                  pl.BlockSpec((1, 1, tf), lambda i, f, te, nu: (te[i], 0, fcol(i, f, nu))),
                  pl.BlockSpec((1, d, tf), lambda i, f, te, nu: (te[i], 0, fcol(i, f, nu))),
                  pl.BlockSpec((1, 1, tf), lambda i, f, te, nu: (te[i], 0, fcol(i, f, nu))),
                  pl.BlockSpec((1, tf, d), lambda i, f, te, nu: (te[i], fcol(i, f, nu), 0)),
                  pl.BlockSpec((1, 1, d), lambda i, f, te, nu: (te[i], 0, 0))],
        out_specs=pl.BlockSpec((tm * ROW_SUB, LANES), lambda i, f, te, nu: (i, 0)),
        scratch_shapes=[pltpu.VMEM((tm, d), BF16), pltpu.VMEM((tm, d), F32)],
    )
    return pl.pallas_call(
        body,
        grid_spec=grid_spec,
        out_shape=jax.ShapeDtypeStruct(x_rows.shape, jnp.uint32),
        compiler_params=_cparams(("arbitrary", "arbitrary")),
        name="moe_ffn",
    )(tile_expert, n_used, x_rows, w_gate, b_gate.reshape(n_e, 1, d_ff),
      w_up, b_up.reshape(n_e, 1, d_ff), w_down, b_down.reshape(n_e, 1, d))


def _final_body(pos_cur_ref, pos_nxt_ref, h_ref, route_ref, g_ref, b_ref, src_ref,
                o_ref, buf_ref, sem, *, alpha, tt, n_steps):
    i = pl.program_id(0)
    slot_rows = tt * TOP_K * ROW_SUB

    def issue(pref, slot):
        def one(t, carry):
            for k in range(TOP_K):
                srow = pl.multiple_of(pref[t * TOP_K + k] * ROW_SUB, ROW_SUB)
                drow = pl.multiple_of(slot * slot_rows + (k * tt + t) * ROW_SUB, ROW_SUB)
                pltpu.make_async_copy(src_ref.at[pl.ds(srow, ROW_SUB)], buf_ref.at[pl.ds(drow, ROW_SUB)],
                                      sem.at[slot]).start()
            return carry
        lax.fori_loop(0, tt, one, 0)

    @pl.when(i == 0)
    def _():
        issue(pos_cur_ref, 0)

    @pl.when(i + 1 < n_steps)
    def _():
        issue(pos_nxt_ref, (i + 1) % 2)

    slot = i % 2
    base = pl.multiple_of(slot * slot_rows, ROW_SUB)
    pltpu.make_async_copy(src_ref.at[pl.ds(0, slot_rows)], buf_ref.at[pl.ds(base, slot_rows)], sem.at[slot]).wait()

    ff_lo = [None] * ROW_SUB
    ff_hi = [None] * ROW_SUB
    for k in range(TOP_K):
        gate = route_ref[:, TOP_K + k:TOP_K + k + 1]
        lo, hi = _unpack_rows(buf_ref, base + k * tt * ROW_SUB, tt)
        for s in range(ROW_SUB):
            ff_lo[s] = lo[s] * gate if k == 0 else ff_lo[s] + lo[s] * gate
            ff_hi[s] = hi[s] * gate if k == 0 else ff_hi[s] + hi[s] * gate
    ff = jnp.concatenate(ff_lo + ff_hi, axis=1)
    o_ref[...] = _layernorm(alpha * h_ref[...] + ff, g_ref[...], b_ref[...])


def _combine_ln(pos_flat, h, route, g_row, b_row, outs, alpha, tt):
    m, d = h.shape
    n_steps = m // tt
    body = functools.partial(_final_body, alpha=alpha, tt=tt, n_steps=n_steps)
    row = lambda i: (i, 0)
    fix = lambda i: (0, 0)
    return pl.pallas_call(
        body,
        grid=(n_steps,),
        in_specs=[pl.BlockSpec((tt * TOP_K,), lambda i: (i,), memory_space=pltpu.SMEM),
                  pl.BlockSpec((tt * TOP_K,), lambda i: (jnp.minimum(i + 1, n_steps - 1),),
                               memory_space=pltpu.SMEM),
                  pl.BlockSpec((tt, d), row), pl.BlockSpec((tt, LANES), row),
                  pl.BlockSpec((1, d), fix), pl.BlockSpec((1, d), fix),
                  pl.BlockSpec(memory_space=pl.ANY)],
        out_specs=pl.BlockSpec((tt, d), row),
        out_shape=jax.ShapeDtypeStruct((m, d), F32),
        scratch_shapes=[pltpu.VMEM((2 * tt * TOP_K * ROW_SUB, LANES), jnp.uint32),
                        pltpu.SemaphoreType.DMA((2,))],
        compiler_params=_cparams(("arbitrary",)),
        name="combine_ln",
    )(pos_flat, pos_flat, h, route, g_row, b_row, outs)


def _row_tile(m, pref):
    t = min(m, pref)
    while m % t:
        t //= 2
    return t


def _route_tables(route, counts_row, m, tm):
    n_assign = m * TOP_K
    e_tok = route[:, :TOP_K].astype(jnp.int32)
    rank = route[:, 2 * TOP_K:3 * TOP_K].astype(jnp.int32)
    counts = counts_row[0, :N_EXPERTS].astype(jnp.int32)
    padded = (counts + tm - 1) // tm * tm
    pends = jnp.cumsum(padded)
    pstarts = pends - padded
    pos = pstarts[e_tok] + rank
    n_tiles = -(-n_assign // tm) + N_EXPERTS
    tile_start = jnp.arange(n_tiles, dtype=jnp.int32) * tm
    tile_expert = jnp.minimum(jnp.sum((pends[None, :] <= tile_start[:, None]).astype(jnp.int32), axis=1),
                              N_EXPERTS - 1)
    n_used = (pends[-1] // tm).astype(jnp.int32).reshape(1)
    last_used = jnp.maximum(n_used[0] - 1, 0)
    tile_expert = jnp.where(tile_start // tm < n_used[0], tile_expert, tile_expert[last_used])
    return pos.reshape(n_assign), tile_expert, n_used


def _layer(x, conv_hist, s_gdn, h_re, h_im, chunk, alpha, p):
    b, t, d = x.shape
    m = b * t
    d_gdn = GDN_HEADS * HEAD_DIM
    x2 = x.reshape(m, d)

    proj = _in_proj(x2, p['w_in'], _row_tile(m, 1024), p['proj_tn'])
    proj3 = proj.reshape(b, t, proj.shape[1])
    conv_new = proj3[:, t - (CONV_W - 1):, :3 * d_gdn]

    q, k, v, cols, gct = _gdn_prep(proj3, conv_hist, p['w_conv'], p['alog_row'], p['dtb_row'],
                                   chunk, p['ab_block'])
    o_gdn, s_new = _gdn(q, k, v, cols, gct, proj3, p['wn_row'], s_gdn, chunk, p['z_block'])

    n_gb = p['s5_mats'][0].shape[1]
    half = GROUPS_PER_BLOCK * SSM_P
    yg, hre_new, him_new = _s5(proj3, p['s5_mats'], h_re.reshape(b, n_gb, 1, half),
                               h_im.reshape(b, n_gb, 1, half), p['u_block0'])
    o_ssm = _glu(yg.reshape(m, yg.shape[2]), p['w_glu'], p['b_glu_row'], _row_tile(m, 512))

    h, hb, route, counts_row = _mix_ln_route(o_gdn.reshape(m, d_gdn), o_ssm, p['w_out_a'], p['w_out_b'], x2,
                                             p['ln1_g'], p['ln1_b'], p['w_router'], p['b_router'], alpha,
                                             _row_tile(m, 256))

    big = m * TOP_K >= 8 * p['moe_tm']
    tm = p['moe_tm'] if big else 128
    pos_flat, tile_expert, n_used = _route_tables(route, counts_row, m, tm)
    cap = tile_expert.shape[0] * tm
    x_rows = _dispatch_rows(pos_flat, hb, cap, _row_tile(m, 256))
    outs = _moe_ffn(tile_expert, n_used, x_rows, p['w_gate'], p['b_gate'],
                    p['w_up'], p['b_up'], p['w_down'], p['b_down'], tm, p['moe_tf'])
    y = _combine_ln(pos_flat, h, route, p['ln2_g'], p['ln2_b'], outs, alpha, _row_tile(m, 128))
    g_all = h_re.shape[1]
    return (y.reshape(b, t, d), conv_new, s_new,
            hre_new.reshape(b, g_all, SSM_P), him_new.reshape(b, g_all, SSM_P))


def _pad_lanes(v, fill=0.0):
    return jnp.pad(v.astype(F32), (0, LANES - v.shape[0]), constant_values=fill).reshape(1, LANES)


def _layer_params(l, w_in, w_conv, a_log, dt_bias, w_onorm, lam_re, lam_im, log_dt, b_re, b_im, c_re, c_im,
                  d_skip, w_glu, b_glu, w_out, ln1_g, ln1_b, w_router, b_router, w_gate, b_gate,
                  w_up, b_up, w_down, b_down, ln2_g, ln2_b):
    d_model = w_in.shape[1]
    d_gdn = GDN_HEADS * HEAD_DIM
    d_qkvz = 4 * d_gdn
    d_ssm = d_model - d_gdn
    wi = w_in[l]
    proj_tn = 896
    n_cols = d_qkvz + d_ssm + LANES
    n_pad = -(-n_cols // proj_tn) * proj_tn
    w_in_r = jnp.concatenate([wi[:, :d_qkvz], wi[:, d_qkvz + 2 * GDN_HEADS:],
                              wi[:, d_qkvz:d_qkvz + 2 * GDN_HEADS],
                              jnp.zeros((d_model, n_pad - d_qkvz - d_ssm - 2 * GDN_HEADS), wi.dtype)], axis=1)
    wo = w_out[l].astype(BF16)
    wr = jnp.pad(w_router[l].astype(F32), ((0, 0), (0, LANES - N_EXPERTS)))
    return {
        'w_in': w_in_r.astype(BF16), 'proj_tn': proj_tn,
        'z_block': 3, 'u_block0': (d_qkvz) // LANES, 'ab_block': (d_qkvz + d_ssm) // LANES,
        'w_conv': w_conv[l].astype(F32),
        'alog_row': _pad_lanes(a_log[l]), 'dtb_row': _pad_lanes(dt_bias[l]),
        'wn_row': w_onorm[l].astype(F32).reshape(1, HEAD_DIM),
        's5_mats': _s5_matrices(lam_re[l], lam_im[l], log_dt[l], b_re[l], b_im[l], c_re[l], c_im[l], d_skip[l]),
        'w_glu': w_glu[l].astype(BF16), 'b_glu_row': b_glu[l].astype(F32).reshape(1, d_ssm),
        'w_out_a': wo[:d_gdn], 'w_out_b': wo[d_gdn:],
        'ln1_g': ln1_g[l].astype(F32).reshape(1, d_model), 'ln1_b': ln1_b[l].astype(F32).reshape(1, d_model),
        'w_router': wr, 'b_router': _pad_lanes(b_router[l]),
        'w_gate': w_gate[l], 'b_gate': b_gate[l], 'w_up': w_up[l], 'b_up': b_up[l],
        'w_down': w_down[l], 'b_down': b_down[l],
        'ln2_g': ln2_g[l].astype(F32).reshape(1, d_model), 'ln2_b': ln2_b[l].astype(F32).reshape(1, d_model),
        'moe_tm': 512, 'moe_tf': 512,
    }


def kernel(x_prompt, x_sample, state_conv, state_gdn, state_ssm_re, state_ssm_im, w_in, w_conv, a_log, dt_bias, w_onorm, lam_re, lam_im, log_dt, b_re, b_im, c_re, c_im, d_skip, w_glu, b_glu, w_out, ln1_g, ln1_b, w_router, b_router, w_gate, b_gate, w_up, b_up, w_down, b_down, ln2_g, ln2_b):
    depth = w_in.shape[0]
    alpha = (2.0 * depth) ** 0.25
    bp, seq, _ = x_prompt.shape
    chunk_p = 64
    d_qkv = state_conv.shape[-1]
    n_groups, n_p = state_ssm_re.shape[-2:]
    yp, ys = x_prompt, x_sample
    outs_p = [[], [], [], []]
    outs_s = [[], [], [], []]
    for l in range(depth):
        p = _layer_params(l, w_in, w_conv, a_log, dt_bias, w_onorm, lam_re, lam_im, log_dt, b_re, b_im,
                          c_re, c_im, d_skip, w_glu, b_glu, w_out, ln1_g, ln1_b, w_router, b_router,
                          w_gate, b_gate, w_up, b_up, w_down, b_down, ln2_g, ln2_b)
        yp, cp, sp, rp, ip = _layer(
            yp, jnp.zeros((bp, CONV_W - 1, d_qkv), F32),
            jnp.zeros((bp, GDN_HEADS, HEAD_DIM, HEAD_DIM), F32),
            jnp.zeros((bp, n_groups, n_p), F32), jnp.zeros((bp, n_groups, n_p), F32),
            chunk_p, alpha, p)
        ys, cs, ss, rs, is_ = _layer(
            ys, state_conv[l].astype(F32), state_gdn[l].astype(F32),
            state_ssm_re[l].astype(F32), state_ssm_im[l].astype(F32),
            ys.shape[1], alpha, p)
        for acc, val in zip(outs_p, (cp, sp, rp, ip)):
            acc.append(val)
        for acc, val in zip(outs_s, (cs, ss, rs, is_)):
            acc.append(val)
    return (yp, ys, *[jnp.stack(a) for a in outs_p], *[jnp.stack(a) for a in outs_s])
```

```python
import functools
import math

import jax
import jax.numpy as jnp
from jax import lax
from jax.experimental import pallas as pl
from jax.experimental.pallas import tpu as pltpu

F32 = jnp.float32
BF16 = jnp.bfloat16

GDN_HEADS = 8
HEAD_DIM = 128
CONV_W = 4
SSM_CG = 16
SSM_P = 64
N_EXPERTS = 32
TOP_K = 4
SWIGLU_ALPHA = 1.702
SWIGLU_LIMIT = 7.0
LN_EPS = 1e-5
RMS_EPS = 1e-6
L2_EPS = 1e-6

LANES = 128
SUBLANES = 8
S5_SUB = 8
S5_TILE = 2048
GDN_TILE = 128
GROUPS_PER_BLOCK = LANES // SSM_CG
VMEM_LIMIT = 56 * 1024 * 1024

NT_DIMS = (((1,), (1,)), ((), ()))
TN_DIMS = (((0,), (0,)), ((), ()))


def _dot(a, b, dims=(((1,), (0,)), ((), ()))):
    return lax.dot_general(a, b, dims, preferred_element_type=F32)


def _split(a):
    hi = a.astype(BF16)
    lo = (a - hi.astype(F32)).astype(BF16)
    return hi, lo


def _dot_x3(a, b, dims=(((1,), (0,)), ((), ()))):
    ah, al = _split(a)
    bh, bl = _split(b)
    return _dot(ah, bh, dims) + (_dot(ah, bl, dims) + _dot(al, bh, dims))


def _dot_bf(a, b, dims=(((1,), (0,)), ((), ()))):
    return _dot(a.astype(BF16), b.astype(BF16), dims)


_gdn_mm = _dot_bf


def _sigmoid(x):
    return 1.0 / (1.0 + jnp.exp(-x))


ROW_SUB = SUBLANES


def _pack_rows(x, ref, tm):
    half = x.shape[1] // 2
    for s in range(ROW_SUB):
        lo = x[:, s * LANES:(s + 1) * LANES].astype(BF16).astype(F32)
        hi = x[:, half + s * LANES:half + (s + 1) * LANES].astype(BF16).astype(F32)
        word = (lax.bitcast_convert_type(lo, jnp.uint32) >> 16) | lax.bitcast_convert_type(hi, jnp.uint32)
        ref[pl.ds(s, tm, stride=ROW_SUB), :] = word


def _unpack_rows(ref, base, tm):
    lo, hi = [], []
    for s in range(ROW_SUB):
        word = ref[pl.ds(base + s, tm, stride=ROW_SUB), :]
        lo.append(lax.bitcast_convert_type(word << 16, F32))
        hi.append(lax.bitcast_convert_type(word & jnp.uint32(0xFFFF0000), F32))
    return lo, hi


def _cparams(sem):
    return pltpu.CompilerParams(dimension_semantics=sem, vmem_limit_bytes=VMEM_LIMIT)


def _proj_body(x_ref, w_ref, o_ref, xb_ref):
    @pl.when(pl.program_id(1) == 0)
    def _():
        xb_ref[...] = x_ref[...].astype(BF16)

    o_ref[...] = _dot(xb_ref[...], w_ref[...])


def _in_proj(x2, w_bf, tm, tn):
    m, k = x2.shape
    n = w_bf.shape[1]
    return pl.pallas_call(
        _proj_body,
        grid=(m // tm, n // tn),
        in_specs=[pl.BlockSpec((tm, k), lambda i, j: (i, 0)),
                  pl.BlockSpec((k, tn), lambda i, j: (0, j))],
        out_specs=pl.BlockSpec((tm, tn), lambda i, j: (i, j)),
        out_shape=jax.ShapeDtypeStruct((m, n), F32),
        scratch_shapes=[pltpu.VMEM((tm, k), BF16)],
        compiler_params=_cparams(("parallel", "arbitrary")),
        name="in_proj",
    )(x2, w_bf)


def _gdn_prep_body(qkv_ref, ab_ref, hist_ref, wc_ref, alog_ref, dtb_ref,
                   q_ref, k_ref, v_ref, cols_ref, gct_ref, xbuf_ref, *, tt, chunk):
    d_gdn = GDN_HEADS * HEAD_DIM
    halo = SUBLANES

    @pl.when(pl.program_id(1) == 0)
    def _():
        xbuf_ref[0:halo, :] = jnp.zeros((halo, 3 * d_gdn), F32)
        xbuf_ref[halo - (CONV_W - 1):halo, :] = hist_ref[0]

    xbuf_ref[halo:halo + tt, :] = qkv_ref[0]

    for part, out_ref in enumerate((q_ref, k_ref, v_ref)):
        c0 = part * d_gdn
        y = None
        for j in range(CONV_W):
            r0 = halo - (CONV_W - 1) + j
            term = xbuf_ref[r0:r0 + tt, c0:c0 + d_gdn] * wc_ref[j:j + 1, c0:c0 + d_gdn]
            y = term if y is None else y + term
        s = y * _sigmoid(y)
        if part == 2:
            out_ref[0] = s
        else:
            for h in range(GDN_HEADS):
                sh = s[:, h * HEAD_DIM:(h + 1) * HEAD_DIM]
                nrm = sh * lax.rsqrt(jnp.sum(sh * sh, axis=-1, keepdims=True) + L2_EPS)
                if part == 0:
                    nrm = nrm * (HEAD_DIM ** -0.5)
                out_ref[0, :, h * HEAD_DIM:(h + 1) * HEAD_DIM] = nrm

    xbuf_ref[0:halo, :] = xbuf_ref[tt:tt + halo, :]

    ab = ab_ref[0]
    lane = lax.broadcasted_iota(jnp.int32, (tt, LANES), 1)
    is_a = lane < GDN_HEADS
    z = ab + dtb_ref[...]
    softplus = jnp.maximum(z, 0.0) + jnp.log1p(jnp.exp(-jnp.abs(z)))
    g = jnp.where(is_a, -jnp.exp(alog_ref[...]) * softplus, 0.0)
    beta = _sigmoid(ab)

    shift = int(math.log2(chunk))
    r = lax.broadcasted_iota(jnp.int32, (tt, tt), 0)
    c = lax.broadcasted_iota(jnp.int32, (tt, tt), 1)
    same = (r >> shift) == (c >> shift)
    m_incl = jnp.where(same & (r >= c), 1.0, 0.0).astype(BF16)
    m_all = jnp.where(same, 1.0, 0.0).astype(BF16)
    gc = _dot_exact_lhs_rhs(m_incl, g)
    glast = _dot_exact_lhs_rhs(m_all, g)
    eg = jnp.exp(gc)
    egl = jnp.exp(glast - gc)
    egt = jnp.exp(glast)
    zero = jnp.zeros_like(gc)
    cols = (jnp.where(is_a, gc, zero)
            + jnp.where((lane >= 8) & (lane < 16), beta, zero)
            + pltpu.roll(jnp.where(is_a, eg, zero), 16, 1)
            + pltpu.roll(jnp.where(is_a, egl, zero), 24, 1)
            + pltpu.roll(jnp.where(is_a, egt, zero), 32, 1))
    cols_ref[0] = cols

    er = lax.broadcasted_iota(jnp.int32, (SUBLANES, LANES), 0)
    ec = lax.broadcasted_iota(jnp.int32, (SUBLANES, LANES), 1)
    sel = jnp.where(er == ec, 1.0, 0.0).astype(BF16)
    gct_ref[0] = _dot_exact_rhs(sel, jnp.where(is_a, gc, zero), NT_DIMS)


def _three_pieces(a):
    a0 = a.astype(BF16)
    r1 = a - a0.astype(F32)
    a1 = r1.astype(BF16)
    a2 = (r1 - a1.astype(F32)).astype(BF16)
    return a0, a1, a2


def _dot_exact_lhs_rhs(mask_bf, a):
    a0, a1, a2 = _three_pieces(a)
    return _dot(mask_bf, a0) + (_dot(mask_bf, a1) + _dot(mask_bf, a2))


def _dot_exact_rhs(mask_bf, a, dims):
    a0, a1, a2 = _three_pieces(a)
    return _dot(mask_bf, a0, dims) + (_dot(mask_bf, a1, dims) + _dot(mask_bf, a2, dims))


def _gdn_prep(proj3, hist, w_conv, alog_p, dtb_p, chunk, ab_block):
    b, t, _ = proj3.shape
    d_gdn = GDN_HEADS * HEAD_DIM
    tt = min(t, GDN_TILE)
    body = functools.partial(_gdn_prep_body, tt=tt, chunk=chunk)
    big = jax.ShapeDtypeStruct((b, t, d_gdn), F32)
    return pl.pallas_call(
        body,
        grid=(b, t // tt),
        in_specs=[pl.BlockSpec((1, tt, 3 * d_gdn), lambda i, j: (i, j, 0)),
                  pl.BlockSpec((1, tt, LANES), lambda i, j: (i, j, ab_block)),
                  pl.BlockSpec((1, CONV_W - 1, 3 * d_gdn), lambda i, j: (i, 0, 0)),
                  pl.BlockSpec((CONV_W, 3 * d_gdn), lambda i, j: (0, 0)),
                  pl.BlockSpec((1, LANES), lambda i, j: (0, 0)),
                  pl.BlockSpec((1, LANES), lambda i, j: (0, 0))],
        out_specs=[pl.BlockSpec((1, tt, d_gdn), lambda i, j: (i, j, 0)),
                   pl.BlockSpec((1, tt, d_gdn), lambda i, j: (i, j, 0)),
                   pl.BlockSpec((1, tt, d_gdn), lambda i, j: (i, j, 0)),
                   pl.BlockSpec((1, tt, LANES), lambda i, j: (i, j, 0)),
                   pl.BlockSpec((1, SUBLANES, tt), lambda i, j: (i, 0, j))],
        out_shape=[big, big, big,
                   jax.ShapeDtypeStruct((b, t, LANES), F32),
                   jax.ShapeDtypeStruct((b, SUBLANES, t), F32)],
        scratch_shapes=[pltpu.VMEM((tt + SUBLANES, 3 * d_gdn), F32)],
        compiler_params=_cparams(("parallel", "arbitrary")),
        name="gdn_prep",
    )(proj3, proj3, hist, w_conv, alog_p, dtb_p)


def _unit_lower_inverses(mats, r, c, chunk):
    base = 16
    eye = jnp.where(r == c, 1.0, 0.0)

    def blk(bs):
        s = int(math.log2(bs))
        return (r >> s) == (c >> s)

    d1 = [jnp.where(blk(base), a, 0.0) for a in mats]
    d2 = [_gdn_mm(x, x) for x in d1]
    d4 = [_gdn_mm(x, x) for x in d2]
    d8 = [_gdn_mm(x, x) for x in d4]
    t = [eye - x for x in d1]
    t = [x + _gdn_mm(x, y) for x, y in zip(t, d2)]
    t = [x + _gdn_mm(x, y) for x, y in zip(t, d4)]
    t = [x + _gdn_mm(x, y) for x, y in zip(t, d8)]
    bs = base
    while bs < chunk:
        off_mask = blk(2 * bs) & jnp.logical_not(blk(bs))
        inner = [_gdn_mm(jnp.where(off_mask, a, 0.0), x) for a, x in zip(mats, t)]
        t = [x - _gdn_mm(x, y) for x, y in zip(t, inner)]
        bs *= 2
    return t


def _gdn_body(q_ref, k_ref, v_ref, cols_ref, gct_ref, z_ref, wn_ref, s0_ref,
              o_ref, s_ref, *, tb, chunk):
    @pl.when(pl.program_id(1) == 0)
    def _():
        s_ref[...] = s0_ref[...]

    shift = int(math.log2(chunk))
    r = lax.broadcasted_iota(jnp.int32, (tb, tb), 0)
    c = lax.broadcasted_iota(jnp.int32, (tb, tb), 1)
    same = (r >> shift) == (c >> shift)
    incl = same & (r >= c)
    strict = same & (r > c)
    n_chunks = tb // chunk

    heads = range(GDN_HEADS)
    hsl = [slice(h * HEAD_DIM, (h + 1) * HEAD_DIM) for h in heads]
    q = [q_ref[0, :, s] for s in hsl]
    k = [k_ref[0, :, s] for s in hsl]
    beta = [cols_ref[0, :, 8 + h:9 + h] for h in heads]
    eg = [cols_ref[0, :, 16 + h:17 + h] for h in heads]
    egl = [cols_ref[0, :, 24 + h:25 + h] for h in heads]
    decay = [jnp.exp(jnp.where(incl, cols_ref[0, :, h:h + 1] - gct_ref[0, h:h + 1, :], -jnp.inf)) for h in heads]
    kb = [k[h] * beta[h] for h in heads]
    k_bf = [x.astype(BF16) for x in k]
    a = [jnp.where(strict, _dot(kb[h].astype(BF16), k_bf[h], NT_DIMS) * decay[h], 0.0) for h in heads]
    tinv = _unit_lower_inverses(a, r, c, chunk)
    sol = [_gdn_mm(tinv[h], jnp.concatenate([v_ref[0, :, hsl[h]] * beta[h], kb[h] * eg[h]], axis=1)) for h in heads]
    u = [x[:, :HEAD_DIM] for x in sol]
    w_bf = [x[:, HEAD_DIM:].astype(BF16) for x in sol]
    attn = [(_dot(q[h].astype(BF16), k_bf[h], NT_DIMS) * decay[h]).astype(BF16) for h in heads]
    q_dec = [(q[h] * eg[h]).astype(BF16) for h in heads]
    k_dec = [(k[h] * egl[h]).astype(BF16) for h in heads]

    s = [s_ref[0, h] for h in heads]
    v_new = [[] for _ in heads]
    o_state = [[] for _ in heads]
    for ci in range(n_chunks):
        rs = slice(ci * chunk, (ci + 1) * chunk)
        s_bf = [x.astype(BF16) for x in s]
        vn = [u[h][rs] - _dot(w_bf[h][rs], s_bf[h]) for h in heads]
        for h in heads:
            o_state[h].append(_dot(q_dec[h][rs], s_bf[h]))
            v_new[h].append(vn[h])
        s = [s[h] * cols_ref[0, ci * chunk:ci * chunk + 1, 32 + h:33 + h]
             + _dot(k_dec[h][rs], vn[h].astype(BF16), TN_DIMS) for h in heads]
    for h in heads:
        s_ref[0, h] = s[h]

    def cat(parts):
        return parts[0] if len(parts) == 1 else jnp.concatenate(parts, axis=0)

    for h in heads:
        o = cat(o_state[h]) + _dot(attn[h], cat(v_new[h]).astype(BF16))
        zh = z_ref[0, :, hsl[h]]
        o = (o * lax.rsqrt(jnp.mean(o * o, axis=-1, keepdims=True) + RMS_EPS) * wn_ref[...]
             * (zh * _sigmoid(zh)))
        o_ref[0, :, hsl[h]] = o.astype(BF16)


def _gdn(q, k, v, cols, gct, proj3, wn, s0, chunk, z_block):
    b, t, d_gdn = q.shape
    tb = min(t, GDN_TILE)
    body = functools.partial(_gdn_body, tb=tb, chunk=chunk)
    tile = lambda i, j: (i, j, 0)
    return pl.pallas_call(
        body,
        grid=(b, t // tb),
        in_specs=[pl.BlockSpec((1, tb, d_gdn), tile),
                  pl.BlockSpec((1, tb, d_gdn), tile),
                  pl.BlockSpec((1, tb, d_gdn), tile),
                  pl.BlockSpec((1, tb, LANES), tile),
                  pl.BlockSpec((1, SUBLANES, tb), lambda i, j: (i, 0, j)),
                  pl.BlockSpec((1, tb, d_gdn), lambda i, j: (i, j, z_block)),
                  pl.BlockSpec((1, HEAD_DIM), lambda i, j: (0, 0)),
                  pl.BlockSpec((1, GDN_HEADS, HEAD_DIM, HEAD_DIM), lambda i, j: (i, 0, 0, 0))],
        out_specs=[pl.BlockSpec((1, tb, d_gdn), tile),
                   pl.BlockSpec((1, GDN_HEADS, HEAD_DIM, HEAD_DIM), lambda i, j: (i, 0, 0, 0))],
        out_shape=[jax.ShapeDtypeStruct((b, t, d_gdn), BF16),
                   jax.ShapeDtypeStruct((b, GDN_HEADS, HEAD_DIM, HEAD_DIM), F32)],
        compiler_params=_cparams(("parallel", "arbitrary")),
        name="gdn_delta",
    )(q, k, v, cols, gct, proj3, wn, s0)


def _gelu_tanh(x):
    return 0.5 * x * (1.0 + jnp.tanh(math.sqrt(2.0 / math.pi) * (x + 0.044715 * (x * x * x))))


def _s5_body(u_ref, km_ref, wm_ref, vm_ref, lre_ref, lim_ref, dsk_ref, h0re_ref, h0im_ref,
             y_ref, hre_ref, him_ref, xbuf_ref, hbuf_ref, *, n_sub):
    half = GROUPS_PER_BLOCK * SSM_P

    @pl.when(pl.program_id(2) == 0)
    def _():
        hre_ref[...] = h0re_ref[...]
        him_ref[...] = h0im_ref[...]

    u_f = [u_ref[0, pl.ds(j, n_sub, stride=S5_SUB), :] for j in range(S5_SUB)]
    u_b = [x.astype(BF16) for x in u_f]

    x = _dot(u_b[0], wm_ref[0, 0])
    for j in range(1, S5_SUB):
        x = x + _dot(u_b[j], wm_ref[j, 0])
    xbuf_ref[...] = x

    lre = lre_ref[0]
    lim = lim_ref[0]

    def step(n, carry):
        hre, him = carry
        hbuf_ref[pl.ds(n, 1), 0:half] = hre
        hbuf_ref[pl.ds(n, 1), half:2 * half] = him
        xr = xbuf_ref[pl.ds(n, 1), 0:half]
        xi = xbuf_ref[pl.ds(n, 1), half:2 * half]
        return (lre * hre - lim * him + xr, lre * him + lim * hre + xi)

    hre, him = lax.fori_loop(0, n_sub, step, (hre_ref[0, 0], him_ref[0, 0]))
    hre_ref[0, 0] = hre
    him_ref[0, 0] = him

    h_b = hbuf_ref[...].astype(BF16)
    dsk = dsk_ref[0]
    for l in range(S5_SUB):
        y = _dot(h_b, vm_ref[l, 0])
        for d in range(l + 1):
            y = y + _dot(u_b[l - d], km_ref[d, 0])
        y = y + dsk * u_f[l]
        y_ref[0, pl.ds(l, n_sub, stride=S5_SUB), :] = _gelu_tanh(y)


def _s5(proj3, mats, h0re, h0im, u_block0):
    km, wm, vm, lre, lim, dsk = mats
    b, t, _ = proj3.shape
    n_gb = km.shape[1]
    tt = min(t, S5_TILE)
    n_sub = tt // S5_SUB
    half = GROUPS_PER_BLOCK * SSM_P
    body = functools.partial(_s5_body, n_sub=n_sub)
    state_spec = pl.BlockSpec((1, 1, 1, half), lambda g, i, j: (i, g, 0, 0))
    par_spec = pl.BlockSpec((1, 1, half), lambda g, i, j: (g, 0, 0))
    return pl.pallas_call(
        body,
        grid=(n_gb, b, t // tt),
        in_specs=[pl.BlockSpec((1, tt, LANES), lambda g, i, j: (i, j, u_block0 + g)),
                  pl.BlockSpec((S5_SUB, 1, LANES, LANES), lambda g, i, j: (0, g, 0, 0)),
                  pl.BlockSpec((S5_SUB, 1, LANES, 2 * half), lambda g, i, j: (0, g, 0, 0)),
                  pl.BlockSpec((S5_SUB, 1, 2 * half, LANES), lambda g, i, j: (0, g, 0, 0)),
                  par_spec, par_spec,
                  pl.BlockSpec((1, 1, LANES), lambda g, i, j: (g, 0, 0)),
                  state_spec, state_spec],
        out_specs=[pl.BlockSpec((1, tt, LANES), lambda g, i, j: (i, j, g)),
                   state_spec, state_spec],
        out_shape=[jax.ShapeDtypeStruct((b, t, n_gb * LANES), F32),
                   jax.ShapeDtypeStruct((b, n_gb, 1, half), F32),
                   jax.ShapeDtypeStruct((b, n_gb, 1, half), F32)],
        scratch_shapes=[pltpu.VMEM((n_sub, 2 * half), F32),
                        pltpu.VMEM((n_sub, 2 * half), F32)],
        compiler_params=_cparams(("parallel", "parallel", "arbitrary")),
        name="s5_scan",
    )(proj3, km, wm, vm, lre, lim, dsk, h0re, h0im)


def _s5_matrices(lam_re, lam_im, log_dt, b_re, b_im, c_re, c_im, d_skip):
    g, p = lam_re.shape
    n_gb = g // GROUPS_PER_BLOCK
    gpb = GROUPS_PER_BLOCK
    dt = jnp.exp(log_dt.astype(F32))
    lam = lax.complex(jnp.minimum(lam_re.astype(F32), -1e-4), lam_im.astype(F32))
    lam_bar = jnp.exp(lam * dt[:, None])
    b_bar = ((lam_bar - 1.0) / lam)[..., None] * lax.complex(b_re.astype(F32), b_im.astype(F32))
    c_c = lax.complex(c_re.astype(F32), c_im.astype(F32))
    pows = [jnp.ones_like(lam_bar)]
    for _ in range(S5_SUB):
        pows.append(pows[-1] * lam_bar)
    pw = jnp.stack(pows)
    eye = jnp.eye(gpb, dtype=F32)

    kd = jnp.real(jnp.einsum('gop,dgp,gpi->dgio', c_c, pw[:S5_SUB], b_bar))
    km = jnp.einsum('dbgio,gh->dbgiho', kd.reshape(S5_SUB, n_gb, gpb, SSM_CG, SSM_CG), eye)
    km = km.reshape(S5_SUB, n_gb, LANES, LANES)

    wj = pw[:S5_SUB][::-1][:, :, :, None] * b_bar[None]
    wj = jnp.transpose(wj, (0, 1, 3, 2)).reshape(S5_SUB, n_gb, gpb, SSM_CG, p)
    wre = jnp.einsum('dbgcp,gh->dbgchp', jnp.real(wj), eye).reshape(S5_SUB, n_gb, LANES, gpb * p)
    wim = jnp.einsum('dbgcp,gh->dbgchp', jnp.imag(wj), eye).reshape(S5_SUB, n_gb, LANES, gpb * p)
    wm = jnp.concatenate([wre, wim], axis=-1)

    cl = c_c[None] * pw[1:S5_SUB + 1][:, :, None, :]
    cl = jnp.transpose(cl, (0, 1, 3, 2)).reshape(S5_SUB, n_gb, gpb, p, SSM_CG)
    vre = jnp.einsum('dbgpc,gh->dbgphc', jnp.real(cl), eye).reshape(S5_SUB, n_gb, gpb * p, LANES)
    vim = jnp.einsum('dbgpc,gh->dbgphc', -jnp.imag(cl), eye).reshape(S5_SUB, n_gb, gpb * p, LANES)
    vm = jnp.concatenate([vre, vim], axis=-2)

    lam_s = pw[S5_SUB].reshape(n_gb, 1, gpb * p)
    dsk = d_skip.astype(F32).reshape(n_gb, 1, LANES)
    return (km.astype(BF16), wm.astype(BF16), vm.astype(BF16),
            jnp.real(lam_s), jnp.imag(lam_s), dsk)


def _glu_body(y_ref, w_ref, b_ref, o_ref):
    y = y_ref[...]
    gate = _dot(y.astype(BF16), w_ref[...]) + b_ref[...]
    o_ref[...] = (y * _sigmoid(gate)).astype(BF16)


def _glu(y2, w_bf, b_row, tm):
    m, d = y2.shape
    return pl.pallas_call(
        _glu_body,
        grid=(m // tm,),
        in_specs=[pl.BlockSpec((tm, d), lambda i: (i, 0)),
                  pl.BlockSpec((d, d), lambda i: (0, 0)),
                  pl.BlockSpec((1, d), lambda i: (0, 0))],
        out_specs=pl.BlockSpec((tm, d), lambda i: (i, 0)),
        out_shape=jax.ShapeDtypeStruct((m, d), BF16),
        compiler_params=_cparams(("parallel",)),
        name="s5_glu",
    )(y2, w_bf, b_row)


def _layernorm(v, g, b):
    mu = jnp.mean(v, axis=-1, keepdims=True)
    var = jnp.mean(jnp.square(v - mu), axis=-1, keepdims=True)
    return (v - mu) * lax.rsqrt(var + LN_EPS) * g + b


def _mix_body(og_ref, os_ref, wa_ref, wb_ref, x_ref, g_ref, b_ref, wr_ref, br_ref,
              h_ref, hb_ref, route_ref, cnt_ref, *, alpha):
    @pl.when(pl.program_id(0) == 0)
    def _():
        cnt_ref[...] = jnp.zeros_like(cnt_ref)

    mix = _dot(og_ref[...], wa_ref[...]) + _dot(os_ref[...], wb_ref[...])
    h = _layernorm(alpha * x_ref[...] + mix, g_ref[...], b_ref[...])
    h_ref[...] = h
    tm = h.shape[0]
    _pack_rows(h, hb_ref, tm)

    logits = _dot_x3(h, wr_ref[...]) + br_ref[...]
    lane = lax.broadcasted_iota(jnp.int32, (tm, LANES), 1)
    work = jnp.where(lane < N_EXPERTS, logits, -jnp.inf)
    vals, idxs = [], []
    for _ in range(TOP_K):
        mx = jnp.max(work, axis=-1, keepdims=True)
        ix = jnp.min(jnp.where(work == mx, lane, LANES), axis=-1, keepdims=True)
        vals.append(mx)
        idxs.append(ix)
        work = jnp.where(lane == ix, -jnp.inf, work)
    exps = [jnp.exp(v - vals[0]) for v in vals]
    denom = exps[0]
    for e in exps[1:]:
        denom = denom + e
    chosen = jnp.zeros((tm, LANES), F32)
    for k in range(TOP_K):
        chosen = jnp.where(lane == idxs[k], 1.0, chosen)
    rr = lax.broadcasted_iota(jnp.int32, (tm, tm), 0)
    cc = lax.broadcasted_iota(jnp.int32, (tm, tm), 1)
    lower = jnp.where(rr > cc, 1.0, 0.0).astype(BF16)
    before = cnt_ref[...] + _dot(lower, chosen.astype(BF16))
    cnt_ref[...] = cnt_ref[...] + jnp.sum(chosen, axis=0, keepdims=True)

    route = jnp.zeros((tm, LANES), F32)
    for k in range(TOP_K):
        rank = jnp.sum(jnp.where(lane == idxs[k], before, 0.0), axis=-1, keepdims=True)
        route = jnp.where(lane == k, idxs[k].astype(F32), route)
        route = jnp.where(lane == TOP_K + k, exps[k] / denom, route)
        route = jnp.where(lane == 2 * TOP_K + k, rank, route)
    route_ref[...] = route


def _mix_ln_route(og, osm, wa, wb, x2, g_row, b_row, wr, br, alpha, tm):
    m, d = x2.shape
    dh = og.shape[1]
    body = functools.partial(_mix_body, alpha=alpha)
    row = lambda i: (i, 0)
    fix = lambda i: (0, 0)
    return pl.pallas_call(
        body,
        grid=(m // tm,),
        in_specs=[pl.BlockSpec((tm, dh), row), pl.BlockSpec((tm, dh), row),
                  pl.BlockSpec((dh, d), fix), pl.BlockSpec((dh, d), fix),
                  pl.BlockSpec((tm, d), row),
                  pl.BlockSpec((1, d), fix), pl.BlockSpec((1, d), fix),
                  pl.BlockSpec((d, LANES), fix), pl.BlockSpec((1, LANES), fix)],
        out_specs=[pl.BlockSpec((tm, d), row), pl.BlockSpec((tm * ROW_SUB, LANES), row),
                   pl.BlockSpec((tm, LANES), row), pl.BlockSpec((1, LANES), fix)],
        out_shape=[jax.ShapeDtypeStruct((m, d), F32),
                   jax.ShapeDtypeStruct((m * ROW_SUB, LANES), jnp.uint32),
                   jax.ShapeDtypeStruct((m, LANES), F32),
                   jax.ShapeDtypeStruct((1, LANES), F32)],
        compiler_params=_cparams(("arbitrary",)),
        name="mix_ln_route",
    )(og, osm, wa, wb, x2, g_row, b_row, wr, br)


def _dispatch_body(pos_ref, src_ref, init_ref, dst_ref, sem, *, tt):
    del init_ref

    def issue(t, carry):
        srow = pl.multiple_of(t * ROW_SUB, ROW_SUB)
        for k in range(TOP_K):
            drow = pl.multiple_of(pos_ref[t * TOP_K + k] * ROW_SUB, ROW_SUB)
            pltpu.make_async_copy(src_ref.at[pl.ds(srow, ROW_SUB)], dst_ref.at[pl.ds(drow, ROW_SUB)], sem).start()
        return carry

    lax.fori_loop(0, tt, issue, 0)
    n = tt * ROW_SUB
    for _ in range(TOP_K):
        pltpu.make_async_copy(src_ref.at[pl.ds(0, n)], dst_ref.at[pl.ds(0, n)], sem).wait()


def _dispatch_rows(pos_flat, hbp, cap, tt):
    m = hbp.shape[0] // ROW_SUB
    body = functools.partial(_dispatch_body, tt=tt)
    init = jnp.zeros((cap * ROW_SUB, LANES), jnp.uint32)
    return pl.pallas_call(
        body,
        grid=(m // tt,),
        in_specs=[pl.BlockSpec((tt * TOP_K,), lambda i: (i,), memory_space=pltpu.SMEM),
                  pl.BlockSpec((tt * ROW_SUB, LANES), lambda i: (i, 0)), pl.BlockSpec(memory_space=pl.ANY)],
        out_specs=pl.BlockSpec(memory_space=pl.ANY),
        out_shape=jax.ShapeDtypeStruct((cap * ROW_SUB, LANES), jnp.uint32),
        scratch_shapes=[pltpu.SemaphoreType.DMA(())],
        input_output_aliases={2: 0},
        compiler_params=_cparams(("arbitrary",)),
        name="moe_dispatch",
    )(pos_flat, hbp, init)


def _moe_body(te_ref, nu_ref, x_ref, wg_ref, bg_ref, wu_ref, bu_ref, wd_ref, bd_ref,
              o_ref, xb_ref, hid_ref, ob_ref, *, n_f, tm, tf):
    i = pl.program_id(0)
    s = pl.program_id(1)
    used = i < nu_ref[0]
    half = xb_ref.shape[1] // 2

    @pl.when(used & (s == 0))
    def _():
        lo, hi = _unpack_rows(x_ref, 0, tm)
        for r in range(ROW_SUB):
            xb_ref[:, r * LANES:(r + 1) * LANES] = lo[r].astype(BF16)
            xb_ref[:, half + r * LANES:half + (r + 1) * LANES] = hi[r].astype(BF16)

    @pl.when(used & (s < n_f))
    def _():
        xb = xb_ref[...]
        hg = jnp.minimum(_dot(xb, wg_ref[0].astype(BF16)) + bg_ref[0], SWIGLU_LIMIT)
        hu = jnp.clip(_dot(xb, wu_ref[0].astype(BF16)) + bu_ref[0], -SWIGLU_LIMIT, SWIGLU_LIMIT)
        hid_ref[s] = ((hu + 1.0) * (hg * _sigmoid(SWIGLU_ALPHA * hg))).astype(BF16)

    @pl.when(used & (s >= n_f))
    def _():
        wd = wd_ref[0].astype(BF16)
        val = bd_ref[0] + _dot(hid_ref[0], wd[0:tf, :])
        for f in range(1, n_f):
            val = val + _dot(hid_ref[f], wd[f * tf:(f + 1) * tf, :])
        ob_ref[s - n_f] = val.astype(BF16)

    @pl.when(used & (s == 2 * n_f - 1))
    def _():
        _pack_rows(jnp.concatenate([ob_ref[j] for j in range(n_f)], axis=1), o_ref, tm)

    @pl.when(jnp.logical_not(used) & (s == 2 * n_f - 1))
    def _():
        o_ref[...] = jnp.zeros_like(o_ref)


def _moe_ffn(tile_expert, n_used, x_rows, w_gate, b_gate, w_up, b_up, w_down, b_down, tm, tf):
    n_e, d, d_ff = w_gate.shape
    assert d == d_ff, "the two-phase schedule tiles d_ff and d_model with the same width"
    n_tiles = x_rows.shape[0] // (tm * ROW_SUB)
    n_f = d_ff // tf
    body = functools.partial(_moe_body, n_f=n_f, tm=tm, tf=tf)

    def fcol(i, s, nu):
        return jnp.where(i < nu[0], jnp.minimum(s, n_f - 1), n_f - 1)

    def ncol(i, s, nu):
        return jnp.where(i < nu[0], jnp.maximum(s - n_f, 0), n_f - 1)

    grid_spec = pltpu.PrefetchScalarGridSpec(
        num_scalar_prefetch=2,
        grid=(n_tiles, 2 * n_f),
        in_specs=[pl.BlockSpec((tm * ROW_SUB, LANES), lambda i, s, te, nu: (i, 0)),
                  pl.BlockSpec((1, d, tf), lambda i, s, te, nu: (te[i], 0, fcol(i, s, nu))),
                  pl.BlockSpec((1, 1, tf), lambda i, s, te, nu: (te[i], 0, fcol(i, s, nu))),
                  pl.BlockSpec((1, d, tf), lambda i, s, te, nu: (te[i], 0, fcol(i, s, nu))),
                  pl.BlockSpec((1, 1, tf), lambda i, s, te, nu: (te[i], 0, fcol(i, s, nu))),
                  pl.BlockSpec((1, d_ff, tf), lambda i, s, te, nu: (te[i], 0, ncol(i, s, nu))),
                  pl.BlockSpec((1, 1, tf), lambda i, s, te, nu: (te[i], 0, ncol(i, s, nu)))],
        out_specs=pl.BlockSpec((tm * ROW_SUB, LANES), lambda i, s, te, nu: (i, 0)),
        scratch_shapes=[pltpu.VMEM((tm, d), BF16), pltpu.VMEM((n_f, tm, tf), BF16),
                        pltpu.VMEM((n_f, tm, tf), BF16)],
    )
    return pl.pallas_call(
        body,
        grid_spec=grid_spec,
        out_shape=jax.ShapeDtypeStruct(x_rows.shape, jnp.uint32),
        compiler_params=_cparams(("arbitrary", "arbitrary")),
        name="moe_ffn",
    )(tile_expert, n_used, x_rows, w_gate, b_gate.reshape(n_e, 1, d_ff),
      w_up, b_up.reshape(n_e, 1, d_ff), w_down, b_down.reshape(n_e, 1, d))


def _final_body(pos_cur_ref, pos_nxt_ref, h_ref, route_ref, g_ref, b_ref, src_ref,
                o_ref, buf_ref, sem, *, alpha, tt, n_steps):
    i = pl.program_id(0)
    slot_rows = tt * TOP_K * ROW_SUB

    def issue(pref, slot):
        def one(t, carry):
            for k in range(TOP_K):
                srow = pl.multiple_of(pref[t * TOP_K + k] * ROW_SUB, ROW_SUB)
                drow = pl.multiple_of(slot * slot_rows + (k * tt + t) * ROW_SUB, ROW_SUB)
                pltpu.make_async_copy(src_ref.at[pl.ds(srow, ROW_SUB)], buf_ref.at[pl.ds(drow, ROW_SUB)],
                                      sem.at[slot]).start()
            return carry
        lax.fori_loop(0, tt, one, 0)

    @pl.when(i == 0)
    def _():
        issue(pos_cur_ref, 0)

    @pl.when(i + 1 < n_steps)
    def _():
        issue(pos_nxt_ref, (i + 1) % 2)

    slot = i % 2
    base = pl.multiple_of(slot * slot_rows, ROW_SUB)
    pltpu.make_async_copy(src_ref.at[pl.ds(0, slot_rows)], buf_ref.at[pl.ds(base, slot_rows)], sem.at[slot]).wait()

    ff_lo = [None] * ROW_SUB
    ff_hi = [None] * ROW_SUB
    for k in range(TOP_K):
        gate = route_ref[:, TOP_K + k:TOP_K + k + 1]
        lo, hi = _unpack_rows(buf_ref, base + k * tt * ROW_SUB, tt)
        for s in range(ROW_SUB):
            ff_lo[s] = lo[s] * gate if k == 0 else ff_lo[s] + lo[s] * gate
            ff_hi[s] = hi[s] * gate if k == 0 else ff_hi[s] + hi[s] * gate
    ff = jnp.concatenate(ff_lo + ff_hi, axis=1)
    o_ref[...] = _layernorm(alpha * h_ref[...] + ff, g_ref[...], b_ref[...])


def _combine_ln(pos_flat, h, route, g_row, b_row, outs, alpha, tt):
    m, d = h.shape
    n_steps = m // tt
    body = functools.partial(_final_body, alpha=alpha, tt=tt, n_steps=n_steps)
    row = lambda i: (i, 0)
    fix = lambda i: (0, 0)
    return pl.pallas_call(
        body,
        grid=(n_steps,),
        in_specs=[pl.BlockSpec((tt * TOP_K,), lambda i: (i,), memory_space=pltpu.SMEM),
                  pl.BlockSpec((tt * TOP_K,), lambda i: (jnp.minimum(i + 1, n_steps - 1),),
                               memory_space=pltpu.SMEM),
                  pl.BlockSpec((tt, d), row), pl.BlockSpec((tt, LANES), row),
                  pl.BlockSpec((1, d), fix), pl.BlockSpec((1, d), fix),
                  pl.BlockSpec(memory_space=pl.ANY)],
        out_specs=pl.BlockSpec((tt, d), row),
        out_shape=jax.ShapeDtypeStruct((m, d), F32),
        scratch_shapes=[pltpu.VMEM((2 * tt * TOP_K * ROW_SUB, LANES), jnp.uint32),
                        pltpu.SemaphoreType.DMA((2,))],
        compiler_params=_cparams(("arbitrary",)),
        name="combine_ln",
    )(pos_flat, pos_flat, h, route, g_row, b_row, outs)


def _row_tile(m, pref):
    t = min(m, pref)
    while m % t:
        t //= 2
    return t


def _route_tables(route, counts_row, m, tm):
    n_assign = m * TOP_K
    e_tok = route[:, :TOP_K].astype(jnp.int32)
    rank = route[:, 2 * TOP_K:3 * TOP_K].astype(jnp.int32)
    counts = counts_row[0, :N_EXPERTS].astype(jnp.int32)
    padded = (counts + tm - 1) // tm * tm
    pends = jnp.cumsum(padded)
    pstarts = pends - padded
    pos = pstarts[e_tok] + rank
    n_tiles = -(-n_assign // tm) + N_EXPERTS
    tile_start = jnp.arange(n_tiles, dtype=jnp.int32) * tm
    tile_expert = jnp.minimum(jnp.sum((pends[None, :] <= tile_start[:, None]).astype(jnp.int32), axis=1),
                              N_EXPERTS - 1)
    n_used = (pends[-1] // tm).astype(jnp.int32).reshape(1)
    last_used = jnp.maximum(n_used[0] - 1, 0)
    tile_expert = jnp.where(tile_start // tm < n_used[0], tile_expert, tile_expert[last_used])
    return pos.reshape(n_assign), tile_expert, n_used


def _layer(x, conv_hist, s_gdn, h_re, h_im, chunk, alpha, p):
    b, t, d = x.shape
    m = b * t
    d_gdn = GDN_HEADS * HEAD_DIM
    x2 = x.reshape(m, d)

    proj = _in_proj(x2, p['w_in'], _row_tile(m, 1024), p['proj_tn'])
    proj3 = proj.reshape(b, t, proj.shape[1])
    conv_new = proj3[:, t - (CONV_W - 1):, :3 * d_gdn]

    q, k, v, cols, gct = _gdn_prep(proj3, conv_hist, p['w_conv'], p['alog_row'], p['dtb_row'],
                                   chunk, p['ab_block'])
    o_gdn, s_new = _gdn(q, k, v, cols, gct, proj3, p['wn_row'], s_gdn, chunk, p['z_block'])

    n_gb = p['s5_mats'][0].shape[1]
    half = GROUPS_PER_BLOCK * SSM_P
    yg, hre_new, him_new = _s5(proj3, p['s5_mats'], h_re.reshape(b, n_gb, 1, half),
                               h_im.reshape(b, n_gb, 1, half), p['u_block0'])
    o_ssm = _glu(yg.reshape(m, yg.shape[2]), p['w_glu'], p['b_glu_row'], _row_tile(m, 512))

    h, hb, route, counts_row = _mix_ln_route(o_gdn.reshape(m, d_gdn), o_ssm, p['w_out_a'], p['w_out_b'], x2,
                                             p['ln1_g'], p['ln1_b'], p['w_router'], p['b_router'], alpha,
                                             _row_tile(m, 256))

    big = m * TOP_K >= 8 * p['moe_tm']
    tm = p['moe_tm'] if big else 128
    pos_flat, tile_expert, n_used = _route_tables(route, counts_row, m, tm)
    cap = tile_expert.shape[0] * tm
    x_rows = _dispatch_rows(pos_flat, hb, cap, _row_tile(m, 256))
    outs = _moe_ffn(tile_expert, n_used, x_rows, p['w_gate'], p['b_gate'],
                    p['w_up'], p['b_up'], p['w_down'], p['b_down'], tm, p['moe_tf'])
    y = _combine_ln(pos_flat, h, route, p['ln2_g'], p['ln2_b'], outs, alpha, _row_tile(m, 128))
    g_all = h_re.shape[1]
    return (y.reshape(b, t, d), conv_new, s_new,
            hre_new.reshape(b, g_all, SSM_P), him_new.reshape(b, g_all, SSM_P))


def _pad_lanes(v, fill=0.0):
    return jnp.pad(v.astype(F32), (0, LANES - v.shape[0]), constant_values=fill).reshape(1, LANES)


def _layer_params(l, w_in, w_conv, a_log, dt_bias, w_onorm, lam_re, lam_im, log_dt, b_re, b_im, c_re, c_im,
                  d_skip, w_glu, b_glu, w_out, ln1_g, ln1_b, w_router, b_router, w_gate, b_gate,
                  w_up, b_up, w_down, b_down, ln2_g, ln2_b):
    d_model = w_in.shape[1]
    d_gdn = GDN_HEADS * HEAD_DIM
    d_qkvz = 4 * d_gdn
    d_ssm = d_model - d_gdn
    wi = w_in[l]
    proj_tn = 896
    n_cols = d_qkvz + d_ssm + LANES
    n_pad = -(-n_cols // proj_tn) * proj_tn
    w_in_r = jnp.concatenate([wi[:, :d_qkvz], wi[:, d_qkvz + 2 * GDN_HEADS:],
                              wi[:, d_qkvz:d_qkvz + 2 * GDN_HEADS],
                              jnp.zeros((d_model, n_pad - d_qkvz - d_ssm - 2 * GDN_HEADS), wi.dtype)], axis=1)
    wo = w_out[l].astype(BF16)
    wr = jnp.pad(w_router[l].astype(F32), ((0, 0), (0, LANES - N_EXPERTS)))
    return {
        'w_in': w_in_r.astype(BF16), 'proj_tn': proj_tn,
        'z_block': 3, 'u_block0': (d_qkvz) // LANES, 'ab_block': (d_qkvz + d_ssm) // LANES,
        'w_conv': w_conv[l].astype(F32),
        'alog_row': _pad_lanes(a_log[l]), 'dtb_row': _pad_lanes(dt_bias[l]),
        'wn_row': w_onorm[l].astype(F32).reshape(1, HEAD_DIM),
        's5_mats': _s5_matrices(lam_re[l], lam_im[l], log_dt[l], b_re[l], b_im[l], c_re[l], c_im[l], d_skip[l]),
        'w_glu': w_glu[l].astype(BF16), 'b_glu_row': b_glu[l].astype(F32).reshape(1, d_ssm),
        'w_out_a': wo[:d_gdn], 'w_out_b': wo[d_gdn:],
        'ln1_g': ln1_g[l].astype(F32).reshape(1, d_model), 'ln1_b': ln1_b[l].astype(F32).reshape(1, d_model),
        'w_router': wr, 'b_router': _pad_lanes(b_router[l]),
        'w_gate': w_gate[l], 'b_gate': b_gate[l], 'w_up': w_up[l], 'b_up': b_up[l],
        'w_down': w_down[l], 'b_down': b_down[l],
        'ln2_g': ln2_g[l].astype(F32).reshape(1, d_model), 'ln2_b': ln2_b[l].astype(F32).reshape(1, d_model),
        'moe_tm': 768, 'moe_tf': 512,
    }


def kernel(x_prompt, x_sample, state_conv, state_gdn, state_ssm_re, state_ssm_im, w_in, w_conv, a_log, dt_bias, w_onorm, lam_re, lam_im, log_dt, b_re, b_im, c_re, c_im, d_skip, w_glu, b_glu, w_out, ln1_g, ln1_b, w_router, b_router, w_gate, b_gate, w_up, b_up, w_down, b_down, ln2_g, ln2_b):
    depth = w_in.shape[0]
    alpha = (2.0 * depth) ** 0.25
    bp, seq, _ = x_prompt.shape
    chunk_p = 64
    d_qkv = state_conv.shape[-1]
    n_groups, n_p = state_ssm_re.shape[-2:]
    yp, ys = x_prompt, x_sample
    outs_p = [[], [], [], []]
    outs_s = [[], [], [], []]
    for l in range(depth):
        p = _layer_params(l, w_in, w_conv, a_log, dt_bias, w_onorm, lam_re, lam_im, log_dt, b_re, b_im,
                          c_re, c_im, d_skip, w_glu, b_glu, w_out, ln1_g, ln1_b, w_router, b_router,
                          w_gate, b_gate, w_up, b_up, w_down, b_down, ln2_g, ln2_b)
        yp, cp, sp, rp, ip = _layer(
            yp, jnp.zeros((bp, CONV_W - 1, d_qkv), F32),
            jnp.zeros((bp, GDN_HEADS, HEAD_DIM, HEAD_DIM), F32),
            jnp.zeros((bp, n_groups, n_p), F32), jnp.zeros((bp, n_groups, n_p), F32),
            chunk_p, alpha, p)
        ys, cs, ss, rs, is_ = _layer(
            ys, state_conv[l].astype(F32), state_gdn[l].astype(F32),
            state_ssm_re[l].astype(F32), state_ssm_im[l].astype(F32),
            ys.shape[1], alpha, p)
        for acc, val in zip(outs_p, (cp, sp, rp, ip)):
            acc.append(val)
        for acc, val in zip(outs_s, (cs, ss, rs, is_)):
            acc.append(val)
    return (yp, ys, *[jnp.stack(a) for a in outs_p], *[jnp.stack(a) for a in outs_s])
```

```python
import functools
import math

import jax
import jax.numpy as jnp
from jax import lax
from jax.experimental import pallas as pl
from jax.experimental.pallas import tpu as pltpu

F32 = jnp.float32
BF16 = jnp.bfloat16

GDN_HEADS = 8
HEAD_DIM = 128
CONV_W = 4
SSM_CG = 16
SSM_P = 64
N_EXPERTS = 32
TOP_K = 4
SWIGLU_ALPHA = 1.702
SWIGLU_LIMIT = 7.0
LN_EPS = 1e-5
RMS_EPS = 1e-6
L2_EPS = 1e-6

LANES = 128
SUBLANES = 8
S5_SUB = 8
S5_TILE = 2048
GDN_TILE = 128
GROUPS_PER_BLOCK = LANES // SSM_CG
VMEM_LIMIT = 56 * 1024 * 1024

NT_DIMS = (((1,), (1,)), ((), ()))
TN_DIMS = (((0,), (0,)), ((), ()))


def _dot(a, b, dims=(((1,), (0,)), ((), ()))):
    return lax.dot_general(a, b, dims, preferred_element_type=F32)


def _split(a):
    hi = a.astype(BF16)
    lo = (a - hi.astype(F32)).astype(BF16)
    return hi, lo


def _dot_x3(a, b, dims=(((1,), (0,)), ((), ()))):
    ah, al = _split(a)
    bh, bl = _split(b)
    return _dot(ah, bh, dims) + (_dot(ah, bl, dims) + _dot(al, bh, dims))


def _dot_bf(a, b, dims=(((1,), (0,)), ((), ()))):
    return _dot(a.astype(BF16), b.astype(BF16), dims)


_gdn_mm = _dot_bf


def _sigmoid(x):
    return 1.0 / (1.0 + jnp.exp(-x))


ROW_SUB = SUBLANES


def _pack_rows(x, ref, tm):
    half = x.shape[1] // 2
    for s in range(ROW_SUB):
        lo = x[:, s * LANES:(s + 1) * LANES].astype(BF16).astype(F32)
        hi = x[:, half + s * LANES:half + (s + 1) * LANES].astype(BF16).astype(F32)
        word = (lax.bitcast_convert_type(lo, jnp.uint32) >> 16) | lax.bitcast_convert_type(hi, jnp.uint32)
        ref[pl.ds(s, tm, stride=ROW_SUB), :] = word


def _unpack_rows(ref, base, tm):
    lo, hi = [], []
    for s in range(ROW_SUB):
        word = ref[pl.ds(base + s, tm, stride=ROW_SUB), :]
        lo.append(lax.bitcast_convert_type(word << 16, F32))
        hi.append(lax.bitcast_convert_type(word & jnp.uint32(0xFFFF0000), F32))
    return lo, hi


def _cparams(sem):
    return pltpu.CompilerParams(dimension_semantics=sem, vmem_limit_bytes=VMEM_LIMIT)


def _proj_body(x_ref, w_ref, o_ref, xb_ref):
    @pl.when(pl.program_id(1) == 0)
    def _():
        xb_ref[...] = x_ref[...].astype(BF16)

    o_ref[...] = _dot(xb_ref[...], w_ref[...])


def _in_proj(x2, w_bf, tm, tn):
    m, k = x2.shape
    n = w_bf.shape[1]
    return pl.pallas_call(
        _proj_body,
        grid=(m // tm, n // tn),
        in_specs=[pl.BlockSpec((tm, k), lambda i, j: (i, 0)),
                  pl.BlockSpec((k, tn), lambda i, j: (0, j))],
        out_specs=pl.BlockSpec((tm, tn), lambda i, j: (i, j)),
        out_shape=jax.ShapeDtypeStruct((m, n), F32),
        scratch_shapes=[pltpu.VMEM((tm, k), BF16)],
        compiler_params=_cparams(("parallel", "arbitrary")),
        name="in_proj",
    )(x2, w_bf)


def _gdn_prep_body(qkv_ref, ab_ref, hist_ref, wc_ref, alog_ref, dtb_ref,
                   q_ref, k_ref, v_ref, cols_ref, gct_ref, xbuf_ref, *, tt, chunk):
    d_gdn = GDN_HEADS * HEAD_DIM
    halo = SUBLANES

    @pl.when(pl.program_id(1) == 0)
    def _():
        xbuf_ref[0:halo, :] = jnp.zeros((halo, 3 * d_gdn), F32)
        xbuf_ref[halo - (CONV_W - 1):halo, :] = hist_ref[0]

    xbuf_ref[halo:halo + tt, :] = qkv_ref[0]

    for part, out_ref in enumerate((q_ref, k_ref, v_ref)):
        c0 = part * d_gdn
        y = None
        for j in range(CONV_W):
            r0 = halo - (CONV_W - 1) + j
            term = xbuf_ref[r0:r0 + tt, c0:c0 + d_gdn] * wc_ref[j:j + 1, c0:c0 + d_gdn]
            y = term if y is None else y + term
        s = y * _sigmoid(y)
        if part == 2:
            out_ref[0] = s
        else:
            for h in range(GDN_HEADS):
                sh = s[:, h * HEAD_DIM:(h + 1) * HEAD_DIM]
                nrm = sh * lax.rsqrt(jnp.sum(sh * sh, axis=-1, keepdims=True) + L2_EPS)
                if part == 0:
                    nrm = nrm * (HEAD_DIM ** -0.5)
                out_ref[0, :, h * HEAD_DIM:(h + 1) * HEAD_DIM] = nrm

    xbuf_ref[0:halo, :] = xbuf_ref[tt:tt + halo, :]

    ab = ab_ref[0]
    lane = lax.broadcasted_iota(jnp.int32, (tt, LANES), 1)
    is_a = lane < GDN_HEADS
    z = ab + dtb_ref[...]
    softplus = jnp.maximum(z, 0.0) + jnp.log1p(jnp.exp(-jnp.abs(z)))
    g = jnp.where(is_a, -jnp.exp(alog_ref[...]) * softplus, 0.0)
    beta = _sigmoid(ab)

    shift = int(math.log2(chunk))
    r = lax.broadcasted_iota(jnp.int32, (tt, tt), 0)
    c = lax.broadcasted_iota(jnp.int32, (tt, tt), 1)
    same = (r >> shift) == (c >> shift)
    m_incl = jnp.where(same & (r >= c), 1.0, 0.0).astype(BF16)
    m_all = jnp.where(same, 1.0, 0.0).astype(BF16)
    gc = _dot_exact_lhs_rhs(m_incl, g)
    glast = _dot_exact_lhs_rhs(m_all, g)
    eg = jnp.exp(gc)
    egl = jnp.exp(glast - gc)
    egt = jnp.exp(glast)
    zero = jnp.zeros_like(gc)
    cols = (jnp.where(is_a, gc, zero)
            + jnp.where((lane >= 8) & (lane < 16), beta, zero)
            + pltpu.roll(jnp.where(is_a, eg, zero), 16, 1)
            + pltpu.roll(jnp.where(is_a, egl, zero), 24, 1)
            + pltpu.roll(jnp.where(is_a, egt, zero), 32, 1))
    cols_ref[0] = cols

    er = lax.broadcasted_iota(jnp.int32, (SUBLANES, LANES), 0)
    ec = lax.broadcasted_iota(jnp.int32, (SUBLANES, LANES), 1)
    sel = jnp.where(er == ec, 1.0, 0.0).astype(BF16)
    gct_ref[0] = _dot_exact_rhs(sel, jnp.where(is_a, gc, zero), NT_DIMS)


def _three_pieces(a):
    a0 = a.astype(BF16)
    r1 = a - a0.astype(F32)
    a1 = r1.astype(BF16)
    a2 = (r1 - a1.astype(F32)).astype(BF16)
    return a0, a1, a2


def _dot_exact_lhs_rhs(mask_bf, a):
    a0, a1, a2 = _three_pieces(a)
    return _dot(mask_bf, a0) + (_dot(mask_bf, a1) + _dot(mask_bf, a2))


def _dot_exact_rhs(mask_bf, a, dims):
    a0, a1, a2 = _three_pieces(a)
    return _dot(mask_bf, a0, dims) + (_dot(mask_bf, a1, dims) + _dot(mask_bf, a2, dims))


def _gdn_prep(proj3, hist, w_conv, alog_p, dtb_p, chunk, ab_block):
    b, t, _ = proj3.shape
    d_gdn = GDN_HEADS * HEAD_DIM
    tt = min(t, GDN_TILE)
    body = functools.partial(_gdn_prep_body, tt=tt, chunk=chunk)
    big = jax.ShapeDtypeStruct((b, t, d_gdn), F32)
    return pl.pallas_call(
        body,
        grid=(b, t // tt),
        in_specs=[pl.BlockSpec((1, tt, 3 * d_gdn), lambda i, j: (i, j, 0)),
                  pl.BlockSpec((1, tt, LANES), lambda i, j: (i, j, ab_block)),
                  pl.BlockSpec((1, CONV_W - 1, 3 * d_gdn), lambda i, j: (i, 0, 0)),
                  pl.BlockSpec((CONV_W, 3 * d_gdn), lambda i, j: (0, 0)),
                  pl.BlockSpec((1, LANES), lambda i, j: (0, 0)),
                  pl.BlockSpec((1, LANES), lambda i, j: (0, 0))],
        out_specs=[pl.BlockSpec((1, tt, d_gdn), lambda i, j: (i, j, 0)),
                   pl.BlockSpec((1, tt, d_gdn), lambda i, j: (i, j, 0)),
                   pl.BlockSpec((1, tt, d_gdn), lambda i, j: (i, j, 0)),
                   pl.BlockSpec((1, tt, LANES), lambda i, j: (i, j, 0)),
                   pl.BlockSpec((1, SUBLANES, tt), lambda i, j: (i, 0, j))],
        out_shape=[big, big, big,
                   jax.ShapeDtypeStruct((b, t, LANES), F32),
                   jax.ShapeDtypeStruct((b, SUBLANES, t), F32)],
        scratch_shapes=[pltpu.VMEM((tt + SUBLANES, 3 * d_gdn), F32)],
        compiler_params=_cparams(("parallel", "arbitrary")),
        name="gdn_prep",
    )(proj3, proj3, hist, w_conv, alog_p, dtb_p)


def _unit_lower_inverses(mats, r, c, chunk):
    base = 16
    eye = jnp.where(r == c, 1.0, 0.0)

    def blk(bs):
        s = int(math.log2(bs))
        return (r >> s) == (c >> s)

    d1 = [jnp.where(blk(base), a, 0.0) for a in mats]
    d2 = [_gdn_mm(x, x) for x in d1]
    d4 = [_gdn_mm(x, x) for x in d2]
    d8 = [_gdn_mm(x, x) for x in d4]
    t = [eye - x for x in d1]
    t = [x + _gdn_mm(x, y) for x, y in zip(t, d2)]
    t = [x + _gdn_mm(x, y) for x, y in zip(t, d4)]
    t = [x + _gdn_mm(x, y) for x, y in zip(t, d8)]
    bs = base
    while bs < chunk:
        off_mask = blk(2 * bs) & jnp.logical_not(blk(bs))
        inner = [_gdn_mm(jnp.where(off_mask, a, 0.0), x) for a, x in zip(mats, t)]
        t = [x - _gdn_mm(x, y) for x, y in zip(t, inner)]
        bs *= 2
    return t


def _gdn_body(q_ref, k_ref, v_ref, cols_ref, gct_ref, z_ref, wn_ref, s0_ref,
              o_ref, s_ref, *, tb, chunk):
    @pl.when(pl.program_id(1) == 0)
    def _():
        s_ref[...] = s0_ref[...]

    shift = int(math.log2(chunk))
    r = lax.broadcasted_iota(jnp.int32, (tb, tb), 0)
    c = lax.broadcasted_iota(jnp.int32, (tb, tb), 1)
    same = (r >> shift) == (c >> shift)
    incl = same & (r >= c)
    strict = same & (r > c)
    n_chunks = tb // chunk

    heads = range(GDN_HEADS)
    hsl = [slice(h * HEAD_DIM, (h + 1) * HEAD_DIM) for h in heads]
    q = [q_ref[0, :, s] for s in hsl]
    k = [k_ref[0, :, s] for s in hsl]
    beta = [cols_ref[0, :, 8 + h:9 + h] for h in heads]
    eg = [cols_ref[0, :, 16 + h:17 + h] for h in heads]
    egl = [cols_ref[0, :, 24 + h:25 + h] for h in heads]
    decay = [jnp.exp(jnp.where(incl, cols_ref[0, :, h:h + 1] - gct_ref[0, h:h + 1, :], -jnp.inf)) for h in heads]
    kb = [k[h] * beta[h] for h in heads]
    k_bf = [x.astype(BF16) for x in k]
    a = [jnp.where(strict, _dot(kb[h].astype(BF16), k_bf[h], NT_DIMS) * decay[h], 0.0) for h in heads]
    tinv = _unit_lower_inverses(a, r, c, chunk)
    sol = [_gdn_mm(tinv[h], jnp.concatenate([v_ref[0, :, hsl[h]] * beta[h], kb[h] * eg[h]], axis=1)) for h in heads]
    u = [x[:, :HEAD_DIM] for x in sol]
    w_bf = [x[:, HEAD_DIM:].astype(BF16) for x in sol]
    attn = [(_dot(q[h].astype(BF16), k_bf[h], NT_DIMS) * decay[h]).astype(BF16) for h in heads]
    q_dec = [(q[h] * eg[h]).astype(BF16) for h in heads]
    k_dec = [(k[h] * egl[h]).astype(BF16) for h in heads]

    s = [s_ref[0, h] for h in heads]
    v_new = [[] for _ in heads]
    o_state = [[] for _ in heads]
    for ci in range(n_chunks):
        rs = slice(ci * chunk, (ci + 1) * chunk)
        s_bf = [x.astype(BF16) for x in s]
        vn = [u[h][rs] - _dot(w_bf[h][rs], s_bf[h]) for h in heads]
        for h in heads:
            o_state[h].append(_dot(q_dec[h][rs], s_bf[h]))
            v_new[h].append(vn[h])
        s = [s[h] * cols_ref[0, ci * chunk:ci * chunk + 1, 32 + h:33 + h]
             + _dot(k_dec[h][rs], vn[h].astype(BF16), TN_DIMS) for h in heads]
    for h in heads:
        s_ref[0, h] = s[h]

    def cat(parts):
        return parts[0] if len(parts) == 1 else jnp.concatenate(parts, axis=0)

    for h in heads:
        o = cat(o_state[h]) + _dot(attn[h], cat(v_new[h]).astype(BF16))
        zh = z_ref[0, :, hsl[h]]
        o = (o * lax.rsqrt(jnp.mean(o * o, axis=-1, keepdims=True) + RMS_EPS) * wn_ref[...]
             * (zh * _sigmoid(zh)))
        o_ref[0, :, hsl[h]] = o.astype(BF16)


def _gdn(q, k, v, cols, gct, proj3, wn, s0, chunk, z_block):
    b, t, d_gdn = q.shape
    tb = min(t, GDN_TILE)
    body = functools.partial(_gdn_body, tb=tb, chunk=chunk)
    tile = lambda i, j: (i, j, 0)
    return pl.pallas_call(
        body,
        grid=(b, t // tb),
        in_specs=[pl.BlockSpec((1, tb, d_gdn), tile),
                  pl.BlockSpec((1, tb, d_gdn), tile),
                  pl.BlockSpec((1, tb, d_gdn), tile),
                  pl.BlockSpec((1, tb, LANES), tile),
                  pl.BlockSpec((1, SUBLANES, tb), lambda i, j: (i, 0, j)),
                  pl.BlockSpec((1, tb, d_gdn), lambda i, j: (i, j, z_block)),
                  pl.BlockSpec((1, HEAD_DIM), lambda i, j: (0, 0)),
                  pl.BlockSpec((1, GDN_HEADS, HEAD_DIM, HEAD_DIM), lambda i, j: (i, 0, 0, 0))],
        out_specs=[pl.BlockSpec((1, tb, d_gdn), tile),
                   pl.BlockSpec((1, GDN_HEADS, HEAD_DIM, HEAD_DIM), lambda i, j: (i, 0, 0, 0))],
        out_shape=[jax.ShapeDtypeStruct((b, t, d_gdn), BF16),
                   jax.ShapeDtypeStruct((b, GDN_HEADS, HEAD_DIM, HEAD_DIM), F32)],
        compiler_params=_cparams(("parallel", "arbitrary")),
        name="gdn_delta",
    )(q, k, v, cols, gct, proj3, wn, s0)


def _gelu_tanh(x):
    return 0.5 * x * (1.0 + jnp.tanh(math.sqrt(2.0 / math.pi) * (x + 0.044715 * (x * x * x))))


def _s5_body(u_ref, km_ref, wm_ref, vm_ref, lre_ref, lim_ref, dsk_ref, h0re_ref, h0im_ref,
             y_ref, hre_ref, him_ref, xbuf_ref, hbuf_ref, *, n_sub):
    half = GROUPS_PER_BLOCK * SSM_P

    @pl.when(pl.program_id(2) == 0)
    def _():
        hre_ref[...] = h0re_ref[...]
        him_ref[...] = h0im_ref[...]

    u_f = [u_ref[0, pl.ds(j, n_sub, stride=S5_SUB), :] for j in range(S5_SUB)]
    u_b = [x.astype(BF16) for x in u_f]

    x = _dot(u_b[0], wm_ref[0, 0])
    for j in range(1, S5_SUB):
        x = x + _dot(u_b[j], wm_ref[j, 0])
    xbuf_ref[...] = x

    lre = lre_ref[0]
    lim = lim_ref[0]

    def step(n, carry):
        hre, him = carry
        hbuf_ref[pl.ds(n, 1), 0:half] = hre
        hbuf_ref[pl.ds(n, 1), half:2 * half] = him
        xr = xbuf_ref[pl.ds(n, 1), 0:half]
        xi = xbuf_ref[pl.ds(n, 1), half:2 * half]
        return (lre * hre - lim * him + xr, lre * him + lim * hre + xi)

    hre, him = lax.fori_loop(0, n_sub, step, (hre_ref[0, 0], him_ref[0, 0]))
    hre_ref[0, 0] = hre
    him_ref[0, 0] = him

    h_b = hbuf_ref[...].astype(BF16)
    dsk = dsk_ref[0]
    for l in range(S5_SUB):
        y = _dot(h_b, vm_ref[l, 0])
        for d in range(l + 1):
            y = y + _dot(u_b[l - d], km_ref[d, 0])
        y = y + dsk * u_f[l]
        y_ref[0, pl.ds(l, n_sub, stride=S5_SUB), :] = _gelu_tanh(y)


def _s5(proj3, mats, h0re, h0im, u_block0):
    km, wm, vm, lre, lim, dsk = mats
    b, t, _ = proj3.shape
    n_gb = km.shape[1]
    tt = min(t, S5_TILE)
    n_sub = tt // S5_SUB
    half = GROUPS_PER_BLOCK * SSM_P
    body = functools.partial(_s5_body, n_sub=n_sub)
    state_spec = pl.BlockSpec((1, 1, 1, half), lambda g, i, j: (i, g, 0, 0))
    par_spec = pl.BlockSpec((1, 1, half), lambda g, i, j: (g, 0, 0))
    return pl.pallas_call(
        body,
        grid=(n_gb, b, t // tt),
        in_specs=[pl.BlockSpec((1, tt, LANES), lambda g, i, j: (i, j, u_block0 + g)),
                  pl.BlockSpec((S5_SUB, 1, LANES, LANES), lambda g, i, j: (0, g, 0, 0)),
                  pl.BlockSpec((S5_SUB, 1, LANES, 2 * half), lambda g, i, j: (0, g, 0, 0)),
                  pl.BlockSpec((S5_SUB, 1, 2 * half, LANES), lambda g, i, j: (0, g, 0, 0)),
                  par_spec, par_spec,
                  pl.BlockSpec((1, 1, LANES), lambda g, i, j: (g, 0, 0)),
                  state_spec, state_spec],
        out_specs=[pl.BlockSpec((1, tt, LANES), lambda g, i, j: (i, j, g)),
                   state_spec, state_spec],
        out_shape=[jax.ShapeDtypeStruct((b, t, n_gb * LANES), F32),
                   jax.ShapeDtypeStruct((b, n_gb, 1, half), F32),
                   jax.ShapeDtypeStruct((b, n_gb, 1, half), F32)],
        scratch_shapes=[pltpu.VMEM((n_sub, 2 * half), F32),
                        pltpu.VMEM((n_sub, 2 * half), F32)],
        compiler_params=_cparams(("parallel", "parallel", "arbitrary")),
        name="s5_scan",
    )(proj3, km, wm, vm, lre, lim, dsk, h0re, h0im)


def _s5_matrices(lam_re, lam_im, log_dt, b_re, b_im, c_re, c_im, d_skip):
    g, p = lam_re.shape
    n_gb = g // GROUPS_PER_BLOCK
    gpb = GROUPS_PER_BLOCK
    dt = jnp.exp(log_dt.astype(F32))
    lam = lax.complex(jnp.minimum(lam_re.astype(F32), -1e-4), lam_im.astype(F32))
    lam_bar = jnp.exp(lam * dt[:, None])
    b_bar = ((lam_bar - 1.0) / lam)[..., None] * lax.complex(b_re.astype(F32), b_im.astype(F32))
    c_c = lax.complex(c_re.astype(F32), c_im.astype(F32))
    pows = [jnp.ones_like(lam_bar)]
    for _ in range(S5_SUB):
        pows.append(pows[-1] * lam_bar)
    pw = jnp.stack(pows)
    eye = jnp.eye(gpb, dtype=F32)

    kd = jnp.real(jnp.einsum('gop,dgp,gpi->dgio', c_c, pw[:S5_SUB], b_bar))
    km = jnp.einsum('dbgio,gh->dbgiho', kd.reshape(S5_SUB, n_gb, gpb, SSM_CG, SSM_CG), eye)
    km = km.reshape(S5_SUB, n_gb, LANES, LANES)

    wj = pw[:S5_SUB][::-1][:, :, :, None] * b_bar[None]
    wj = jnp.transpose(wj, (0, 1, 3, 2)).reshape(S5_SUB, n_gb, gpb, SSM_CG, p)
    wre = jnp.einsum('dbgcp,gh->dbgchp', jnp.real(wj), eye).reshape(S5_SUB, n_gb, LANES, gpb * p)
    wim = jnp.einsum('dbgcp,gh->dbgchp', jnp.imag(wj), eye).reshape(S5_SUB, n_gb, LANES, gpb * p)
    wm = jnp.concatenate([wre, wim], axis=-1)

    cl = c_c[None] * pw[1:S5_SUB + 1][:, :, None, :]
    cl = jnp.transpose(cl, (0, 1, 3, 2)).reshape(S5_SUB, n_gb, gpb, p, SSM_CG)
    vre = jnp.einsum('dbgpc,gh->dbgphc', jnp.real(cl), eye).reshape(S5_SUB, n_gb, gpb * p, LANES)
    vim = jnp.einsum('dbgpc,gh->dbgphc', -jnp.imag(cl), eye).reshape(S5_SUB, n_gb, gpb * p, LANES)
    vm = jnp.concatenate([vre, vim], axis=-2)

    lam_s = pw[S5_SUB].reshape(n_gb, 1, gpb * p)
    dsk = d_skip.astype(F32).reshape(n_gb, 1, LANES)
    return (km.astype(BF16), wm.astype(BF16), vm.astype(BF16),
            jnp.real(lam_s), jnp.imag(lam_s), dsk)


def _glu_body(y_ref, w_ref, b_ref, o_ref):
    y = y_ref[...]
    gate = _dot(y.astype(BF16), w_ref[...]) + b_ref[...]
    o_ref[...] = (y * _sigmoid(gate)).astype(BF16)


def _glu(y2, w_bf, b_row, tm):
    m, d = y2.shape
    return pl.pallas_call(
        _glu_body,
        grid=(m // tm,),
        in_specs=[pl.BlockSpec((tm, d), lambda i: (i, 0)),
                  pl.BlockSpec((d, d), lambda i: (0, 0)),
                  pl.BlockSpec((1, d), lambda i: (0, 0))],
        out_specs=pl.BlockSpec((tm, d), lambda i: (i, 0)),
        out_shape=jax.ShapeDtypeStruct((m, d), BF16),
        compiler_params=_cparams(("parallel",)),
        name="s5_glu",
    )(y2, w_bf, b_row)


def _layernorm(v, g, b):
    mu = jnp.mean(v, axis=-1, keepdims=True)
    var = jnp.mean(jnp.square(v - mu), axis=-1, keepdims=True)
    return (v - mu) * lax.rsqrt(var + LN_EPS) * g + b


def _mix_body(og_ref, os_ref, wa_ref, wb_ref, x_ref, g_ref, b_ref, wr_ref, br_ref,
              h_ref, hb_ref, route_ref, cnt_ref, *, alpha):
    @pl.when(pl.program_id(0) == 0)
    def _():
        cnt_ref[...] = jnp.zeros_like(cnt_ref)

    mix = _dot(og_ref[...], wa_ref[...]) + _dot(os_ref[...], wb_ref[...])
    h = _layernorm(alpha * x_ref[...] + mix, g_ref[...], b_ref[...])
    h_ref[...] = h
    tm = h.shape[0]
    _pack_rows(h, hb_ref, tm)

    logits = _dot_x3(h, wr_ref[...]) + br_ref[...]
    lane = lax.broadcasted_iota(jnp.int32, (tm, LANES), 1)
    work = jnp.where(lane < N_EXPERTS, logits, -jnp.inf)
    vals, idxs = [], []
    for _ in range(TOP_K):
        mx = jnp.max(work, axis=-1, keepdims=True)
        ix = jnp.min(jnp.where(work == mx, lane, LANES), axis=-1, keepdims=True)
        vals.append(mx)
        idxs.append(ix)
        work = jnp.where(lane == ix, -jnp.inf, work)
    exps = [jnp.exp(v - vals[0]) for v in vals]
    denom = exps[0]
    for e in exps[1:]:
        denom = denom + e
    chosen = jnp.zeros((tm, LANES), F32)
    for k in range(TOP_K):
        chosen = jnp.where(lane == idxs[k], 1.0, chosen)
    rr = lax.broadcasted_iota(jnp.int32, (tm, tm), 0)
    cc = lax.broadcasted_iota(jnp.int32, (tm, tm), 1)
    lower = jnp.where(rr > cc, 1.0, 0.0).astype(BF16)
    before = cnt_ref[...] + _dot(lower, chosen.astype(BF16))
    cnt_ref[...] = cnt_ref[...] + jnp.sum(chosen, axis=0, keepdims=True)

    route = jnp.zeros((tm, LANES), F32)
    for k in range(TOP_K):
        rank = jnp.sum(jnp.where(lane == idxs[k], before, 0.0), axis=-1, keepdims=True)
        route = jnp.where(lane == k, idxs[k].astype(F32), route)
        route = jnp.where(lane == TOP_K + k, exps[k] / denom, route)
        route = jnp.where(lane == 2 * TOP_K + k, rank, route)
    route_ref[...] = route


def _mix_ln_route(og, osm, wa, wb, x2, g_row, b_row, wr, br, alpha, tm):
    m, d = x2.shape
    dh = og.shape[1]
    body = functools.partial(_mix_body, alpha=alpha)
    row = lambda i: (i, 0)
    fix = lambda i: (0, 0)
    return pl.pallas_call(
        body,
        grid=(m // tm,),
        in_specs=[pl.BlockSpec((tm, dh), row), pl.BlockSpec((tm, dh), row),
                  pl.BlockSpec((dh, d), fix), pl.BlockSpec((dh, d), fix),
                  pl.BlockSpec((tm, d), row),
                  pl.BlockSpec((1, d), fix), pl.BlockSpec((1, d), fix),
                  pl.BlockSpec((d, LANES), fix), pl.BlockSpec((1, LANES), fix)],
        out_specs=[pl.BlockSpec((tm, d), row), pl.BlockSpec((tm * ROW_SUB, LANES), row),
                   pl.BlockSpec((tm, LANES), row), pl.BlockSpec((1, LANES), fix)],
        out_shape=[jax.ShapeDtypeStruct((m, d), F32),
                   jax.ShapeDtypeStruct((m * ROW_SUB, LANES), jnp.uint32),
                   jax.ShapeDtypeStruct((m, LANES), F32),
                   jax.ShapeDtypeStruct((1, LANES), F32)],
        compiler_params=_cparams(("arbitrary",)),
        name="mix_ln_route",
    )(og, osm, wa, wb, x2, g_row, b_row, wr, br)


def _dispatch_body(pos_ref, src_ref, init_ref, dst_ref, sem, *, tt):
    del init_ref

    def issue(t, carry):
        srow = pl.multiple_of(t * ROW_SUB, ROW_SUB)
        for k in range(TOP_K):
            drow = pl.multiple_of(pos_ref[t * TOP_K + k] * ROW_SUB, ROW_SUB)
            pltpu.make_async_copy(src_ref.at[pl.ds(srow, ROW_SUB)], dst_ref.at[pl.ds(drow, ROW_SUB)], sem).start()
        return carry

    lax.fori_loop(0, tt, issue, 0)
    n = tt * ROW_SUB
    for _ in range(TOP_K):
        pltpu.make_async_copy(src_ref.at[pl.ds(0, n)], dst_ref.at[pl.ds(0, n)], sem).wait()


def _dispatch_rows(pos_flat, hbp, cap, tt):
    m = hbp.shape[0] // ROW_SUB
    body = functools.partial(_dispatch_body, tt=tt)
    init = jnp.zeros((cap * ROW_SUB, LANES), jnp.uint32)
    return pl.pallas_call(
        body,
        grid=(m // tt,),
        in_specs=[pl.BlockSpec((tt * TOP_K,), lambda i: (i,), memory_space=pltpu.SMEM),
                  pl.BlockSpec((tt * ROW_SUB, LANES), lambda i: (i, 0)), pl.BlockSpec(memory_space=pl.ANY)],
        out_specs=pl.BlockSpec(memory_space=pl.ANY),
        out_shape=jax.ShapeDtypeStruct((cap * ROW_SUB, LANES), jnp.uint32),
        scratch_shapes=[pltpu.SemaphoreType.DMA(())],
        input_output_aliases={2: 0},
        compiler_params=_cparams(("arbitrary",)),
        name="moe_dispatch",
    )(pos_flat, hbp, init)


MOE_COL = 256


def _moe_body(te_ref, nu_ref, x_ref, wg_ref, bg_ref, wu_ref, bu_ref, wd_ref, bd_ref,
              o_ref, xb_ref, acc_ref, *, n_f, tm, tf):
    i = pl.program_id(0)
    f = pl.program_id(1)
    used = i < nu_ref[0]
    half = xb_ref.shape[1] // 2

    @pl.when(used & (f == 0))
    def _():
        lo, hi = _unpack_rows(x_ref, 0, tm)
        for r in range(ROW_SUB):
            xb_ref[:, r * LANES:(r + 1) * LANES] = lo[r].astype(BF16)
            xb_ref[:, half + r * LANES:half + (r + 1) * LANES] = hi[r].astype(BF16)

    @pl.when(used)
    def _():
        xb = xb_ref[...]
        part = None
        for c in range(tf // MOE_COL):
            cs = slice(c * MOE_COL, (c + 1) * MOE_COL)
            hg = jnp.minimum(_dot(xb, wg_ref[0, :, cs].astype(BF16)) + bg_ref[0, :, cs], SWIGLU_LIMIT)
            hu = jnp.clip(_dot(xb, wu_ref[0, :, cs].astype(BF16)) + bu_ref[0, :, cs], -SWIGLU_LIMIT, SWIGLU_LIMIT)
            hh = ((hu + 1.0) * (hg * _sigmoid(SWIGLU_ALPHA * hg))).astype(BF16)
            p = _dot(hh, wd_ref[0, cs, :].astype(BF16))
            part = p if part is None else part + p

        @pl.when(f == 0)
        def _():
            acc_ref[...] = part

        @pl.when(f != 0)
        def _():
            acc_ref[...] += part

        @pl.when(f == n_f - 1)
        def _():
            _pack_rows(acc_ref[...] + bd_ref[0], o_ref, tm)

    @pl.when(jnp.logical_not(used) & (f == n_f - 1))
    def _():
        o_ref[...] = jnp.zeros_like(o_ref)


def _moe_ffn(tile_expert, n_used, x_rows, w_gate, b_gate, w_up, b_up, w_down, b_down, tm, tf):
    n_e, d, d_ff = w_gate.shape
    n_tiles = x_rows.shape[0] // (tm * ROW_SUB)
    n_f = d_ff // tf
    body = functools.partial(_moe_body, n_f=n_f, tm=tm, tf=tf)

    def fcol(i, f, nu):
        return jnp.where(i < nu[0], f, n_f - 1)

    grid_spec = pltpu.PrefetchScalarGridSpec(
        num_scalar_prefetch=2,
        grid=(n_tiles, n_f),
        in_specs=[pl.BlockSpec((tm * ROW_SUB, LANES), lambda i, f, te, nu: (i, 0)),
                  pl.BlockSpec((1, d, tf), lambda i, f, te, nu: (te[i], 0, fcol(i, f, nu))),
                  pl.BlockSpec((1, 1, tf), lambda i, f, te, nu: (te[i], 0, fcol(i, f, nu))),
                  pl.BlockSpec((1, d, tf), lambda i, f, te, nu: (te[i], 0, fcol(i, f, nu))),
                  pl.BlockSpec((1, 1, tf), lambda i, f, te, nu: (te[i], 0, fcol(i, f, nu))),
                  pl.BlockSpec((1, tf, d), lambda i, f, te, nu: (te[i], fcol(i, f, nu), 0)),
                  pl.BlockSpec((1, 1, d), lambda i, f, te, nu: (te[i], 0, 0))],
        out_specs=pl.BlockSpec((tm * ROW_SUB, LANES), lambda i, f, te, nu: (i, 0)),
        scratch_shapes=[pltpu.VMEM((tm, d), BF16), pltpu.VMEM((tm, d), F32)],
    )
    return pl.pallas_call(
        body,
        grid_spec=grid_spec,
        out_shape=jax.ShapeDtypeStruct(x_rows.shape, jnp.uint32),
        compiler_params=_cparams(("arbitrary", "arbitrary")),
        name="moe_ffn",
    )(tile_expert, n_used, x_rows, w_gate, b_gate.reshape(n_e, 1, d_ff),
      w_up, b_up.reshape(n_e, 1, d_ff), w_down, b_down.reshape(n_e, 1, d))


def _final_body(pos_cur_ref, pos_nxt_ref, h_ref, route_ref, g_ref, b_ref, src_ref,
                o_ref, buf_ref, sem, *, alpha, tt, n_steps):
    i = pl.program_id(0)
    slot_rows = tt * TOP_K * ROW_SUB

    def issue(pref, slot):
        def one(t, carry):
            for k in range(TOP_K):
                srow = pl.multiple_of(pref[t * TOP_K + k] * ROW_SUB, ROW_SUB)
                drow = pl.multiple_of(slot * slot_rows + (k * tt + t) * ROW_SUB, ROW_SUB)
                pltpu.make_async_copy(src_ref.at[pl.ds(srow, ROW_SUB)], buf_ref.at[pl.ds(drow, ROW_SUB)],
                                      sem.at[slot]).start()
            return carry
        lax.fori_loop(0, tt, one, 0)

    @pl.when(i == 0)
    def _():
        issue(pos_cur_ref, 0)

    @pl.when(i + 1 < n_steps)
    def _():
        issue(pos_nxt_ref, (i + 1) % 2)

    slot = i % 2
    base = pl.multiple_of(slot * slot_rows, ROW_SUB)
    pltpu.make_async_copy(src_ref.at[pl.ds(0, slot_rows)], buf_ref.at[pl.ds(base, slot_rows)], sem.at[slot]).wait()

    ff_lo = [None] * ROW_SUB
    ff_hi = [None] * ROW_SUB
    for k in range(TOP_K):
        gate = route_ref[:, TOP_K + k:TOP_K + k + 1]
        lo, hi = _unpack_rows(buf_ref, base + k * tt * ROW_SUB, tt)
        for s in range(ROW_SUB):
            ff_lo[s] = lo[s] * gate if k == 0 else ff_lo[s] + lo[s] * gate
            ff_hi[s] = hi[s] * gate if k == 0 else ff_hi[s] + hi[s] * gate
    ff = jnp.concatenate(ff_lo + ff_hi, axis=1)
    o_ref[...] = _layernorm(alpha * h_ref[...] + ff, g_ref[...], b_ref[...])


def _combine_ln(pos_flat, h, route, g_row, b_row, outs, alpha, tt):
    m, d = h.shape
    n_steps = m // tt
    body = functools.partial(_final_body, alpha=alpha, tt=tt, n_steps=n_steps)
    row = lambda i: (i, 0)
    fix = lambda i: (0, 0)
    return pl.pallas_call(
        body,
        grid=(n_steps,),
        in_specs=[pl.BlockSpec((tt * TOP_K,), lambda i: (i,), memory_space=pltpu.SMEM),
                  pl.BlockSpec((tt * TOP_K,), lambda i: (jnp.minimum(i + 1, n_steps - 1),),
                               memory_space=pltpu.SMEM),
                  pl.BlockSpec((tt, d), row), pl.BlockSpec((tt, LANES), row),
                  pl.BlockSpec((1, d), fix), pl.BlockSpec((1, d), fix),
                  pl.BlockSpec(memory_space=pl.ANY)],
        out_specs=pl.BlockSpec((tt, d), row),
        out_shape=jax.ShapeDtypeStruct((m, d), F32),
        scratch_shapes=[pltpu.VMEM((2 * tt * TOP_K * ROW_SUB, LANES), jnp.uint32),
                        pltpu.SemaphoreType.DMA((2,))],
        compiler_params=_cparams(("arbitrary",)),
        name="combine_ln",
    )(pos_flat, pos_flat, h, route, g_row, b_row, outs)


def _row_tile(m, pref):
    t = min(m, pref)
    while m % t:
        t //= 2
    return t


def _route_tables(route, counts_row, m, tm):
    n_assign = m * TOP_K
    e_tok = route[:, :TOP_K].astype(jnp.int32)
    rank = route[:, 2 * TOP_K:3 * TOP_K].astype(jnp.int32)
    counts = counts_row[0, :N_EXPERTS].astype(jnp.int32)
    padded = (counts + tm - 1) // tm * tm
    pends = jnp.cumsum(padded)
    pstarts = pends - padded
    pos = pstarts[e_tok] + rank
    n_tiles = -(-n_assign // tm) + N_EXPERTS
    tile_start = jnp.arange(n_tiles, dtype=jnp.int32) * tm
    tile_expert = jnp.minimum(jnp.sum((pends[None, :] <= tile_start[:, None]).astype(jnp.int32), axis=1),
                              N_EXPERTS - 1)
    n_used = (pends[-1] // tm).astype(jnp.int32).reshape(1)
    last_used = jnp.maximum(n_used[0] - 1, 0)
    tile_expert = jnp.where(tile_start // tm < n_used[0], tile_expert, tile_expert[last_used])
    return pos.reshape(n_assign), tile_expert, n_used


def _layer(x, conv_hist, s_gdn, h_re, h_im, chunk, alpha, p):
    b, t, d = x.shape
    m = b * t
    d_gdn = GDN_HEADS * HEAD_DIM
    x2 = x.reshape(m, d)

    proj = _in_proj(x2, p['w_in'], _row_tile(m, 1024), p['proj_tn'])
    proj3 = proj.reshape(b, t, proj.shape[1])
    conv_new = proj3[:, t - (CONV_W - 1):, :3 * d_gdn]

    q, k, v, cols, gct = _gdn_prep(proj3, conv_hist, p['w_conv'], p['alog_row'], p['dtb_row'],
                                   chunk, p['ab_block'])
    o_gdn, s_new = _gdn(q, k, v, cols, gct, proj3, p['wn_row'], s_gdn, chunk, p['z_block'])

    n_gb = p['s5_mats'][0].shape[1]
    half = GROUPS_PER_BLOCK * SSM_P
    yg, hre_new, him_new = _s5(proj3, p['s5_mats'], h_re.reshape(b, n_gb, 1, half),
                               h_im.reshape(b, n_gb, 1, half), p['u_block0'])
    o_ssm = _glu(yg.reshape(m, yg.shape[2]), p['w_glu'], p['b_glu_row'], _row_tile(m, 512))

    h, hb, route, counts_row = _mix_ln_route(o_gdn.reshape(m, d_gdn), o_ssm, p['w_out_a'], p['w_out_b'], x2,
                                             p['ln1_g'], p['ln1_b'], p['w_router'], p['b_router'], alpha,
                                             _row_tile(m, 256))

    big = m * TOP_K >= 8 * p['moe_tm']
    tm = p['moe_tm'] if big else 128
    pos_flat, tile_expert, n_used = _route_tables(route, counts_row, m, tm)
    cap = tile_expert.shape[0] * tm
    x_rows = _dispatch_rows(pos_flat, hb, cap, _row_tile(m, 256))
    outs = _moe_ffn(tile_expert, n_used, x_rows, p['w_gate'], p['b_gate'],
                    p['w_up'], p['b_up'], p['w_down'], p['b_down'], tm, p['moe_tf'])
    y = _combine_ln(pos_flat, h, route, p['ln2_g'], p['ln2_b'], outs, alpha, _row_tile(m, 128))
    g_all = h_re.shape[1]
    return (y.reshape(b, t, d), conv_new, s_new,
            hre_new.reshape(b, g_all, SSM_P), him_new.reshape(b, g_all, SSM_P))


def _pad_lanes(v, fill=0.0):
    return jnp.pad(v.astype(F32), (0, LANES - v.shape[0]), constant_values=fill).reshape(1, LANES)


def _layer_params(l, w_in, w_conv, a_log, dt_bias, w_onorm, lam_re, lam_im, log_dt, b_re, b_im, c_re, c_im,
                  d_skip, w_glu, b_glu, w_out, ln1_g, ln1_b, w_router, b_router, w_gate, b_gate,
                  w_up, b_up, w_down, b_down, ln2_g, ln2_b):
    d_model = w_in.shape[1]
    d_gdn = GDN_HEADS * HEAD_DIM
    d_qkvz = 4 * d_gdn
    d_ssm = d_model - d_gdn
    wi = w_in[l]
    proj_tn = 896
    n_cols = d_qkvz + d_ssm + LANES
    n_pad = -(-n_cols // proj_tn) * proj_tn
    w_in_r = jnp.concatenate([wi[:, :d_qkvz], wi[:, d_qkvz + 2 * GDN_HEADS:],
                              wi[:, d_qkvz:d_qkvz + 2 * GDN_HEADS],
                              jnp.zeros((d_model, n_pad - d_qkvz - d_ssm - 2 * GDN_HEADS), wi.dtype)], axis=1)
    wo = w_out[l].astype(BF16)
    wr = jnp.pad(w_router[l].astype(F32), ((0, 0), (0, LANES - N_EXPERTS)))
    return {
        'w_in': w_in_r.astype(BF16), 'proj_tn': proj_tn,
        'z_block': 3, 'u_block0': (d_qkvz) // LANES, 'ab_block': (d_qkvz + d_ssm) // LANES,
        'w_conv': w_conv[l].astype(F32),
        'alog_row': _pad_lanes(a_log[l]), 'dtb_row': _pad_lanes(dt_bias[l]),
        'wn_row': w_onorm[l].astype(F32).reshape(1, HEAD_DIM),
        's5_mats': _s5_matrices(lam_re[l], lam_im[l], log_dt[l], b_re[l], b_im[l], c_re[l], c_im[l], d_skip[l]),
        'w_glu': w_glu[l].astype(BF16), 'b_glu_row': b_glu[l].astype(F32).reshape(1, d_ssm),
        'w_out_a': wo[:d_gdn], 'w_out_b': wo[d_gdn:],
        'ln1_g': ln1_g[l].astype(F32).reshape(1, d_model), 'ln1_b': ln1_b[l].astype(F32).reshape(1, d_model),
        'w_router': wr, 'b_router': _pad_lanes(b_router[l]),
        'w_gate': w_gate[l], 'b_gate': b_gate[l], 'w_up': w_up[l], 'b_up': b_up[l],
        'w_down': w_down[l], 'b_down': b_down[l],
        'ln2_g': ln2_g[l].astype(F32).reshape(1, d_model), 'ln2_b': ln2_b[l].astype(F32).reshape(1, d_model),
        'moe_tm': 512, 'moe_tf': 512,
    }


def kernel(x_prompt, x_sample, state_conv, state_gdn, state_ssm_re, state_ssm_im, w_in, w_conv, a_log, dt_bias, w_onorm, lam_re, lam_im, log_dt, b_re, b_im, c_re, c_im, d_skip, w_glu, b_glu, w_out, ln1_g, ln1_b, w_router, b_router, w_gate, b_gate, w_up, b_up, w_down, b_down, ln2_g, ln2_b):
    depth = w_in.shape[0]
    alpha = (2.0 * depth) ** 0.25
    bp, seq, _ = x_prompt.shape
    chunk_p = 64
    d_qkv = state_conv.shape[-1]
    n_groups, n_p = state_ssm_re.shape[-2:]
    yp, ys = x_prompt, x_sample
    outs_p = [[], [], [], []]
    outs_s = [[], [], [], []]
    for l in range(depth):
        p = _layer_params(l, w_in, w_conv, a_log, dt_bias, w_onorm, lam_re, lam_im, log_dt, b_re, b_im,
                          c_re, c_im, d_skip, w_glu, b_glu, w_out, ln1_g, ln1_b, w_router, b_router,
                          w_gate, b_gate, w_up, b_up, w_down, b_down, ln2_g, ln2_b)
        yp, cp, sp, rp, ip = _layer(
            yp, jnp.zeros((bp, CONV_W - 1, d_qkv), F32),
            jnp.zeros((bp, GDN_HEADS, HEAD_DIM, HEAD_DIM), F32),
            jnp.zeros((bp, n_groups, n_p), F32), jnp.zeros((bp, n_groups, n_p), F32),
            chunk_p, alpha, p)
        ys, cs, ss, rs, is_ = _layer(
            ys, state_conv[l].astype(F32), state_gdn[l].astype(F32),
            state_ssm_re[l].astype(F32), state_ssm_im[l].astype(F32),
            ys.shape[1], alpha, p)
        for acc, val in zip(outs_p, (cp, sp, rp, ip)):
            acc.append(val)
        for acc, val in zip(outs_s, (cs, ss, rs, is_)):
            acc.append(val)
    return (yp, ys, *[jnp.stack(a) for a in outs_p], *[jnp.stack(a) for a in outs_s])
```

```python
import functools
import math

import jax
import jax.numpy as jnp
from jax import lax
from jax.experimental import pallas as pl
from jax.experimental.pallas import tpu as pltpu

F32 = jnp.float32
BF16 = jnp.bfloat16

GDN_HEADS = 8
HEAD_DIM = 128
CONV_W = 4
SSM_CG = 16
SSM_P = 64
N_EXPERTS = 32
TOP_K = 4
SWIGLU_ALPHA = 1.702
SWIGLU_LIMIT = 7.0
LN_EPS = 1e-5
RMS_EPS = 1e-6
L2_EPS = 1e-6

LANES = 128
SUBLANES = 8
S5_SUB = 8
S5_TILE = 2048
GDN_TILE = 128
GROUPS_PER_BLOCK = LANES // SSM_CG
VMEM_LIMIT = 56 * 1024 * 1024

NT_DIMS = (((1,), (1,)), ((), ()))
TN_DIMS = (((0,), (0,)), ((), ()))


def _dot(a, b, dims=(((1,), (0,)), ((), ()))):
    return lax.dot_general(a, b, dims, preferred_element_type=F32)


def _split(a):
    hi = a.astype(BF16)
    lo = (a - hi.astype(F32)).astype(BF16)
    return hi, lo


def _dot_x3(a, b, dims=(((1,), (0,)), ((), ()))):
    ah, al = _split(a)
    bh, bl = _split(b)
    return _dot(ah, bh, dims) + (_dot(ah, bl, dims) + _dot(al, bh, dims))


def _dot_bf(a, b, dims=(((1,), (0,)), ((), ()))):
    return _dot(a.astype(BF16), b.astype(BF16), dims)


_gdn_mm = _dot_bf


def _sigmoid(x):
    return 1.0 / (1.0 + jnp.exp(-x))


ROW_SUB = SUBLANES


def _pack_rows(x, ref, tm):
    half = x.shape[1] // 2
    for s in range(ROW_SUB):
        lo = x[:, s * LANES:(s + 1) * LANES].astype(BF16).astype(F32)
        hi = x[:, half + s * LANES:half + (s + 1) * LANES].astype(BF16).astype(F32)
        word = (lax.bitcast_convert_type(lo, jnp.uint32) >> 16) | lax.bitcast_convert_type(hi, jnp.uint32)
        ref[pl.ds(s, tm, stride=ROW_SUB), :] = word


def _unpack_rows(ref, base, tm):
    lo, hi = [], []
    for s in range(ROW_SUB):
        word = ref[pl.ds(base + s, tm, stride=ROW_SUB), :]
        lo.append(lax.bitcast_convert_type(word << 16, F32))
        hi.append(lax.bitcast_convert_type(word & jnp.uint32(0xFFFF0000), F32))
    return lo, hi


def _cparams(sem):
    return pltpu.CompilerParams(dimension_semantics=sem, vmem_limit_bytes=VMEM_LIMIT)


def _proj_body(x_ref, w_ref, o_ref, xb_ref):
    @pl.when(pl.program_id(1) == 0)
    def _():
        xb_ref[...] = x_ref[...].astype(BF16)

    o_ref[...] = _dot(xb_ref[...], w_ref[...])


def _in_proj(x2, w_bf, tm, tn):
    m, k = x2.shape
    n = w_bf.shape[1]
    return pl.pallas_call(
        _proj_body,
        grid=(m // tm, n // tn),
        in_specs=[pl.BlockSpec((tm, k), lambda i, j: (i, 0)),
                  pl.BlockSpec((k, tn), lambda i, j: (0, j))],
        out_specs=pl.BlockSpec((tm, tn), lambda i, j: (i, j)),
        out_shape=jax.ShapeDtypeStruct((m, n), F32),
        scratch_shapes=[pltpu.VMEM((tm, k), BF16)],
        compiler_params=_cparams(("parallel", "arbitrary")),
        name="in_proj",
    )(x2, w_bf)


def _gdn_prep_body(qkv_ref, ab_ref, hist_ref, wc_ref, alog_ref, dtb_ref,
                   q_ref, k_ref, v_ref, cols_ref, gct_ref, xbuf_ref, *, tt, chunk):
    d_gdn = GDN_HEADS * HEAD_DIM
    halo = SUBLANES

    @pl.when(pl.program_id(1) == 0)
    def _():
        xbuf_ref[0:halo, :] = jnp.zeros((halo, 3 * d_gdn), F32)
        xbuf_ref[halo - (CONV_W - 1):halo, :] = hist_ref[0]

    xbuf_ref[halo:halo + tt, :] = qkv_ref[0]

    for part, out_ref in enumerate((q_ref, k_ref, v_ref)):
        c0 = part * d_gdn
        y = None
        for j in range(CONV_W):
            r0 = halo - (CONV_W - 1) + j
            term = xbuf_ref[r0:r0 + tt, c0:c0 + d_gdn] * wc_ref[j:j + 1, c0:c0 + d_gdn]
            y = term if y is None else y + term
        s = y * _sigmoid(y)
        if part == 2:
            out_ref[0] = s
        else:
            for h in range(GDN_HEADS):
                sh = s[:, h * HEAD_DIM:(h + 1) * HEAD_DIM]
                nrm = sh * lax.rsqrt(jnp.sum(sh * sh, axis=-1, keepdims=True) + L2_EPS)
                if part == 0:
                    nrm = nrm * (HEAD_DIM ** -0.5)
                out_ref[0, :, h * HEAD_DIM:(h + 1) * HEAD_DIM] = nrm

    xbuf_ref[0:halo, :] = xbuf_ref[tt:tt + halo, :]

    ab = ab_ref[0]
    lane = lax.broadcasted_iota(jnp.int32, (tt, LANES), 1)
    is_a = lane < GDN_HEADS
    z = ab + dtb_ref[...]
    softplus = jnp.maximum(z, 0.0) + jnp.log1p(jnp.exp(-jnp.abs(z)))
    g = jnp.where(is_a, -jnp.exp(alog_ref[...]) * softplus, 0.0)
    beta = _sigmoid(ab)

    shift = int(math.log2(chunk))
    r = lax.broadcasted_iota(jnp.int32, (tt, tt), 0)
    c = lax.broadcasted_iota(jnp.int32, (tt, tt), 1)
    same = (r >> shift) == (c >> shift)
    m_incl = jnp.where(same & (r >= c), 1.0, 0.0).astype(BF16)
    m_all = jnp.where(same, 1.0, 0.0).astype(BF16)
    gc = _dot_exact_lhs_rhs(m_incl, g)
    glast = _dot_exact_lhs_rhs(m_all, g)
    eg = jnp.exp(gc)
    egl = jnp.exp(glast - gc)
    egt = jnp.exp(glast)
    zero = jnp.zeros_like(gc)
    cols = (jnp.where(is_a, gc, zero)
            + jnp.where((lane >= 8) & (lane < 16), beta, zero)
            + pltpu.roll(jnp.where(is_a, eg, zero), 16, 1)
            + pltpu.roll(jnp.where(is_a, egl, zero), 24, 1)
            + pltpu.roll(jnp.where(is_a, egt, zero), 32, 1))
    cols_ref[0] = cols

    er = lax.broadcasted_iota(jnp.int32, (SUBLANES, LANES), 0)
    ec = lax.broadcasted_iota(jnp.int32, (SUBLANES, LANES), 1)
    sel = jnp.where(er == ec, 1.0, 0.0).astype(BF16)
    gct_ref[0] = _dot_exact_rhs(sel, jnp.where(is_a, gc, zero), NT_DIMS)


def _three_pieces(a):
    a0 = a.astype(BF16)
    r1 = a - a0.astype(F32)
    a1 = r1.astype(BF16)
    a2 = (r1 - a1.astype(F32)).astype(BF16)
    return a0, a1, a2


def _dot_exact_lhs_rhs(mask_bf, a):
    a0, a1, a2 = _three_pieces(a)
    return _dot(mask_bf, a0) + (_dot(mask_bf, a1) + _dot(mask_bf, a2))


def _dot_exact_rhs(mask_bf, a, dims):
    a0, a1, a2 = _three_pieces(a)
    return _dot(mask_bf, a0, dims) + (_dot(mask_bf, a1, dims) + _dot(mask_bf, a2, dims))


def _gdn_prep(proj3, hist, w_conv, alog_p, dtb_p, chunk, ab_block):
    b, t, _ = proj3.shape
    d_gdn = GDN_HEADS * HEAD_DIM
    tt = min(t, GDN_TILE)
    body = functools.partial(_gdn_prep_body, tt=tt, chunk=chunk)
    big = jax.ShapeDtypeStruct((b, t, d_gdn), F32)
    return pl.pallas_call(
        body,
        grid=(b, t // tt),
        in_specs=[pl.BlockSpec((1, tt, 3 * d_gdn), lambda i, j: (i, j, 0)),
                  pl.BlockSpec((1, tt, LANES), lambda i, j: (i, j, ab_block)),
                  pl.BlockSpec((1, CONV_W - 1, 3 * d_gdn), lambda i, j: (i, 0, 0)),
                  pl.BlockSpec((CONV_W, 3 * d_gdn), lambda i, j: (0, 0)),
                  pl.BlockSpec((1, LANES), lambda i, j: (0, 0)),
                  pl.BlockSpec((1, LANES), lambda i, j: (0, 0))],
        out_specs=[pl.BlockSpec((1, tt, d_gdn), lambda i, j: (i, j, 0)),
                   pl.BlockSpec((1, tt, d_gdn), lambda i, j: (i, j, 0)),
                   pl.BlockSpec((1, tt, d_gdn), lambda i, j: (i, j, 0)),
                   pl.BlockSpec((1, tt, LANES), lambda i, j: (i, j, 0)),
                   pl.BlockSpec((1, SUBLANES, tt), lambda i, j: (i, 0, j))],
        out_shape=[big, big, big,
                   jax.ShapeDtypeStruct((b, t, LANES), F32),
                   jax.ShapeDtypeStruct((b, SUBLANES, t), F32)],
        scratch_shapes=[pltpu.VMEM((tt + SUBLANES, 3 * d_gdn), F32)],
        compiler_params=_cparams(("parallel", "arbitrary")),
        name="gdn_prep",
    )(proj3, proj3, hist, w_conv, alog_p, dtb_p)


def _unit_lower_inverses(mats, r, c, chunk):
    base = 16
    eye = jnp.where(r == c, 1.0, 0.0)

    def blk(bs):
        s = int(math.log2(bs))
        return (r >> s) == (c >> s)

    d1 = [jnp.where(blk(base), a, 0.0) for a in mats]
    d2 = [_gdn_mm(x, x) for x in d1]
    d4 = [_gdn_mm(x, x) for x in d2]
    d8 = [_gdn_mm(x, x) for x in d4]
    t = [eye - x for x in d1]
    t = [x + _gdn_mm(x, y) for x, y in zip(t, d2)]
    t = [x + _gdn_mm(x, y) for x, y in zip(t, d4)]
    t = [x + _gdn_mm(x, y) for x, y in zip(t, d8)]
    bs = base
    while bs < chunk:
        off_mask = blk(2 * bs) & jnp.logical_not(blk(bs))
        inner = [_gdn_mm(jnp.where(off_mask, a, 0.0), x) for a, x in zip(mats, t)]
        t = [x - _gdn_mm(x, y) for x, y in zip(t, inner)]
        bs *= 2
    return t


def _gdn_body(q_ref, k_ref, v_ref, cols_ref, gct_ref, z_ref, wn_ref, s0_ref,
              o_ref, s_ref, *, tb, chunk):
    @pl.when(pl.program_id(1) == 0)
    def _():
        s_ref[...] = s0_ref[...]

    shift = int(math.log2(chunk))
    r = lax.broadcasted_iota(jnp.int32, (tb, tb), 0)
    c = lax.broadcasted_iota(jnp.int32, (tb, tb), 1)
    same = (r >> shift) == (c >> shift)
    incl = same & (r >= c)
    strict = same & (r > c)
    n_chunks = tb // chunk

    heads = range(GDN_HEADS)
    hsl = [slice(h * HEAD_DIM, (h + 1) * HEAD_DIM) for h in heads]
    q = [q_ref[0, :, s] for s in hsl]
    k = [k_ref[0, :, s] for s in hsl]
    beta = [cols_ref[0, :, 8 + h:9 + h] for h in heads]
    eg = [cols_ref[0, :, 16 + h:17 + h] for h in heads]
    egl = [cols_ref[0, :, 24 + h:25 + h] for h in heads]
    decay = [jnp.exp(jnp.where(incl, cols_ref[0, :, h:h + 1] - gct_ref[0, h:h + 1, :], -jnp.inf)) for h in heads]
    kb = [k[h] * beta[h] for h in heads]
    k_bf = [x.astype(BF16) for x in k]
    a = [jnp.where(strict, _dot(kb[h].astype(BF16), k_bf[h], NT_DIMS) * decay[h], 0.0) for h in heads]
    tinv = _unit_lower_inverses(a, r, c, chunk)
    sol = [_gdn_mm(tinv[h], jnp.concatenate([v_ref[0, :, hsl[h]] * beta[h], kb[h] * eg[h]], axis=1)) for h in heads]
    u = [x[:, :HEAD_DIM] for x in sol]
    w_bf = [x[:, HEAD_DIM:].astype(BF16) for x in sol]
    attn = [(_dot(q[h].astype(BF16), k_bf[h], NT_DIMS) * decay[h]).astype(BF16) for h in heads]
    q_dec = [(q[h] * eg[h]).astype(BF16) for h in heads]
    k_dec = [(k[h] * egl[h]).astype(BF16) for h in heads]

    s = [s_ref[0, h] for h in heads]
    v_new = [[] for _ in heads]
    o_state = [[] for _ in heads]
    for ci in range(n_chunks):
        rs = slice(ci * chunk, (ci + 1) * chunk)
        s_bf = [x.astype(BF16) for x in s]
        vn = [u[h][rs] - _dot(w_bf[h][rs], s_bf[h]) for h in heads]
        for h in heads:
            o_state[h].append(_dot(q_dec[h][rs], s_bf[h]))
            v_new[h].append(vn[h])
        s = [s[h] * cols_ref[0, ci * chunk:ci * chunk + 1, 32 + h:33 + h]
             + _dot(k_dec[h][rs], vn[h].astype(BF16), TN_DIMS) for h in heads]
    for h in heads:
        s_ref[0, h] = s[h]

    def cat(parts):
        return parts[0] if len(parts) == 1 else jnp.concatenate(parts, axis=0)

    for h in heads:
        o = cat(o_state[h]) + _dot(attn[h], cat(v_new[h]).astype(BF16))
        zh = z_ref[0, :, hsl[h]]
        o = (o * lax.rsqrt(jnp.mean(o * o, axis=-1, keepdims=True) + RMS_EPS) * wn_ref[...]
             * (zh * _sigmoid(zh)))
        o_ref[0, :, hsl[h]] = o.astype(BF16)


def _gdn(q, k, v, cols, gct, proj3, wn, s0, chunk, z_block):
    b, t, d_gdn = q.shape
    tb = min(t, GDN_TILE)
    body = functools.partial(_gdn_body, tb=tb, chunk=chunk)
    tile = lambda i, j: (i, j, 0)
    return pl.pallas_call(
        body,
        grid=(b, t // tb),
        in_specs=[pl.BlockSpec((1, tb, d_gdn), tile),
                  pl.BlockSpec((1, tb, d_gdn), tile),
                  pl.BlockSpec((1, tb, d_gdn), tile),
                  pl.BlockSpec((1, tb, LANES), tile),
                  pl.BlockSpec((1, SUBLANES, tb), lambda i, j: (i, 0, j)),
                  pl.BlockSpec((1, tb, d_gdn), lambda i, j: (i, j, z_block)),
                  pl.BlockSpec((1, HEAD_DIM), lambda i, j: (0, 0)),
                  pl.BlockSpec((1, GDN_HEADS, HEAD_DIM, HEAD_DIM), lambda i, j: (i, 0, 0, 0))],
        out_specs=[pl.BlockSpec((1, tb, d_gdn), tile),
                   pl.BlockSpec((1, GDN_HEADS, HEAD_DIM, HEAD_DIM), lambda i, j: (i, 0, 0, 0))],
        out_shape=[jax.ShapeDtypeStruct((b, t, d_gdn), BF16),
                   jax.ShapeDtypeStruct((b, GDN_HEADS, HEAD_DIM, HEAD_DIM), F32)],
        compiler_params=_cparams(("parallel", "arbitrary")),
        name="gdn_delta",
    )(q, k, v, cols, gct, proj3, wn, s0)


def _gelu_tanh(x):
    return 0.5 * x * (1.0 + jnp.tanh(math.sqrt(2.0 / math.pi) * (x + 0.044715 * (x * x * x))))


def _s5_body(u_ref, km_ref, wm_ref, vm_ref, lre_ref, lim_ref, dsk_ref, h0re_ref, h0im_ref,
             y_ref, hre_ref, him_ref, xbuf_ref, hbuf_ref, *, n_sub):
    half = GROUPS_PER_BLOCK * SSM_P

    @pl.when(pl.program_id(2) == 0)
    def _():
        hre_ref[...] = h0re_ref[...]
        him_ref[...] = h0im_ref[...]

    u_f = [u_ref[0, pl.ds(j, n_sub, stride=S5_SUB), :] for j in range(S5_SUB)]
    u_b = [x.astype(BF16) for x in u_f]

    x = _dot(u_b[0], wm_ref[0, 0])
    for j in range(1, S5_SUB):
        x = x + _dot(u_b[j], wm_ref[j, 0])
    xbuf_ref[...] = x

    lre = lre_ref[0]
    lim = lim_ref[0]

    def step(n, carry):
        hre, him = carry
        hbuf_ref[pl.ds(n, 1), 0:half] = hre
        hbuf_ref[pl.ds(n, 1), half:2 * half] = him
        xr = xbuf_ref[pl.ds(n, 1), 0:half]
        xi = xbuf_ref[pl.ds(n, 1), half:2 * half]
        return (lre * hre - lim * him + xr, lre * him + lim * hre + xi)

    hre, him = lax.fori_loop(0, n_sub, step, (hre_ref[0, 0], him_ref[0, 0]))
    hre_ref[0, 0] = hre
    him_ref[0, 0] = him

    h_b = hbuf_ref[...].astype(BF16)
    dsk = dsk_ref[0]
    for l in range(S5_SUB):
        y = _dot(h_b, vm_ref[l, 0])
        for d in range(l + 1):
            y = y + _dot(u_b[l - d], km_ref[d, 0])
        y = y + dsk * u_f[l]
        y_ref[0, pl.ds(l, n_sub, stride=S5_SUB), :] = _gelu_tanh(y)


def _s5(proj3, mats, h0re, h0im, u_block0):
    km, wm, vm, lre, lim, dsk = mats
    b, t, _ = proj3.shape
    n_gb = km.shape[1]
    tt = min(t, S5_TILE)
    n_sub = tt // S5_SUB
    half = GROUPS_PER_BLOCK * SSM_P
    body = functools.partial(_s5_body, n_sub=n_sub)
    state_spec = pl.BlockSpec((1, 1, 1, half), lambda g, i, j: (i, g, 0, 0))
    par_spec = pl.BlockSpec((1, 1, half), lambda g, i, j: (g, 0, 0))
    return pl.pallas_call(
        body,
        grid=(n_gb, b, t // tt),
        in_specs=[pl.BlockSpec((1, tt, LANES), lambda g, i, j: (i, j, u_block0 + g)),
                  pl.BlockSpec((S5_SUB, 1, LANES, LANES), lambda g, i, j: (0, g, 0, 0)),
                  pl.BlockSpec((S5_SUB, 1, LANES, 2 * half), lambda g, i, j: (0, g, 0, 0)),
                  pl.BlockSpec((S5_SUB, 1, 2 * half, LANES), lambda g, i, j: (0, g, 0, 0)),
                  par_spec, par_spec,
                  pl.BlockSpec((1, 1, LANES), lambda g, i, j: (g, 0, 0)),
                  state_spec, state_spec],
        out_specs=[pl.BlockSpec((1, tt, LANES), lambda g, i, j: (i, j, g)),
                   state_spec, state_spec],
        out_shape=[jax.ShapeDtypeStruct((b, t, n_gb * LANES), F32),
                   jax.ShapeDtypeStruct((b, n_gb, 1, half), F32),
                   jax.ShapeDtypeStruct((b, n_gb, 1, half), F32)],
        scratch_shapes=[pltpu.VMEM((n_sub, 2 * half), F32),
                        pltpu.VMEM((n_sub, 2 * half), F32)],
        compiler_params=_cparams(("parallel", "parallel", "arbitrary")),
        name="s5_scan",
    )(proj3, km, wm, vm, lre, lim, dsk, h0re, h0im)


def _s5_matrices(lam_re, lam_im, log_dt, b_re, b_im, c_re, c_im, d_skip):
    g, p = lam_re.shape
    n_gb = g // GROUPS_PER_BLOCK
    gpb = GROUPS_PER_BLOCK
    dt = jnp.exp(log_dt.astype(F32))
    lam = lax.complex(jnp.minimum(lam_re.astype(F32), -1e-4), lam_im.astype(F32))
    lam_bar = jnp.exp(lam * dt[:, None])
    b_bar = ((lam_bar - 1.0) / lam)[..., None] * lax.complex(b_re.astype(F32), b_im.astype(F32))
    c_c = lax.complex(c_re.astype(F32), c_im.astype(F32))
    pows = [jnp.ones_like(lam_bar)]
    for _ in range(S5_SUB):
        pows.append(pows[-1] * lam_bar)
    pw = jnp.stack(pows)
    eye = jnp.eye(gpb, dtype=F32)

    kd = jnp.real(jnp.einsum('gop,dgp,gpi->dgio', c_c, pw[:S5_SUB], b_bar))
    km = jnp.einsum('dbgio,gh->dbgiho', kd.reshape(S5_SUB, n_gb, gpb, SSM_CG, SSM_CG), eye)
    km = km.reshape(S5_SUB, n_gb, LANES, LANES)

    wj = pw[:S5_SUB][::-1][:, :, :, None] * b_bar[None]
    wj = jnp.transpose(wj, (0, 1, 3, 2)).reshape(S5_SUB, n_gb, gpb, SSM_CG, p)
    wre = jnp.einsum('dbgcp,gh->dbgchp', jnp.real(wj), eye).reshape(S5_SUB, n_gb, LANES, gpb * p)
    wim = jnp.einsum('dbgcp,gh->dbgchp', jnp.imag(wj), eye).reshape(S5_SUB, n_gb, LANES, gpb * p)
    wm = jnp.concatenate([wre, wim], axis=-1)

    cl = c_c[None] * pw[1:S5_SUB + 1][:, :, None, :]
    cl = jnp.transpose(cl, (0, 1, 3, 2)).reshape(S5_SUB, n_gb, gpb, p, SSM_CG)
    vre = jnp.einsum('dbgpc,gh->dbgphc', jnp.real(cl), eye).reshape(S5_SUB, n_gb, gpb * p, LANES)
    vim = jnp.einsum('dbgpc,gh->dbgphc', -jnp.imag(cl), eye).reshape(S5_SUB, n_gb, gpb * p, LANES)
    vm = jnp.concatenate([vre, vim], axis=-2)

    lam_s = pw[S5_SUB].reshape(n_gb, 1, gpb * p)
    dsk = d_skip.astype(F32).reshape(n_gb, 1, LANES)
    return (km.astype(BF16), wm.astype(BF16), vm.astype(BF16),
            jnp.real(lam_s), jnp.imag(lam_s), dsk)


def _glu_body(y_ref, w_ref, b_ref, o_ref):
    y = y_ref[...]
    gate = _dot(y.astype(BF16), w_ref[...]) + b_ref[...]
    o_ref[...] = (y * _sigmoid(gate)).astype(BF16)


def _glu(y2, w_bf, b_row, tm):
    m, d = y2.shape
    return pl.pallas_call(
        _glu_body,
        grid=(m // tm,),
        in_specs=[pl.BlockSpec((tm, d), lambda i: (i, 0)),
                  pl.BlockSpec((d, d), lambda i: (0, 0)),
                  pl.BlockSpec((1, d), lambda i: (0, 0))],
        out_specs=pl.BlockSpec((tm, d), lambda i: (i, 0)),
        out_shape=jax.ShapeDtypeStruct((m, d), BF16),
        compiler_params=_cparams(("parallel",)),
        name="s5_glu",
    )(y2, w_bf, b_row)


def _layernorm(v, g, b):
    mu = jnp.mean(v, axis=-1, keepdims=True)
    var = jnp.mean(jnp.square(v - mu), axis=-1, keepdims=True)
    return (v - mu) * lax.rsqrt(var + LN_EPS) * g + b


def _mix_body(og_ref, os_ref, wa_ref, wb_ref, x_ref, g_ref, b_ref, wr_ref, br_ref,
              h_ref, hb_ref, route_ref, cnt_ref, *, alpha):
    @pl.when(pl.program_id(0) == 0)
    def _():
        cnt_ref[...] = jnp.zeros_like(cnt_ref)

    mix = _dot(og_ref[...], wa_ref[...]) + _dot(os_ref[...], wb_ref[...])
    h = _layernorm(alpha * x_ref[...] + mix, g_ref[...], b_ref[...])
    h_ref[...] = h
    tm = h.shape[0]
    _pack_rows(h, hb_ref, tm)

    logits = _dot_x3(h, wr_ref[...]) + br_ref[...]
    lane = lax.broadcasted_iota(jnp.int32, (tm, LANES), 1)
    work = jnp.where(lane < N_EXPERTS, logits, -jnp.inf)
    vals, idxs = [], []
    for _ in range(TOP_K):
        mx = jnp.max(work, axis=-1, keepdims=True)
        ix = jnp.min(jnp.where(work == mx, lane, LANES), axis=-1, keepdims=True)
        vals.append(mx)
        idxs.append(ix)
        work = jnp.where(lane == ix, -jnp.inf, work)
    exps = [jnp.exp(v - vals[0]) for v in vals]
    denom = exps[0]
    for e in exps[1:]:
        denom = denom + e
    chosen = jnp.zeros((tm, LANES), F32)
    for k in range(TOP_K):
        chosen = jnp.where(lane == idxs[k], 1.0, chosen)
    rr = lax.broadcasted_iota(jnp.int32, (tm, tm), 0)
    cc = lax.broadcasted_iota(jnp.int32, (tm, tm), 1)
    lower = jnp.where(rr > cc, 1.0, 0.0).astype(BF16)
    before = cnt_ref[...] + _dot(lower, chosen.astype(BF16))
    cnt_ref[...] = cnt_ref[...] + jnp.sum(chosen, axis=0, keepdims=True)

    route = jnp.zeros((tm, LANES), F32)
    for k in range(TOP_K):
        rank = jnp.sum(jnp.where(lane == idxs[k], before, 0.0), axis=-1, keepdims=True)
        route = jnp.where(lane == k, idxs[k].astype(F32), route)
        route = jnp.where(lane == TOP_K + k, exps[k] / denom, route)
        route = jnp.where(lane == 2 * TOP_K + k, rank, route)
    route_ref[...] = route


def _mix_ln_route(og, osm, wa, wb, x2, g_row, b_row, wr, br, alpha, tm):
    m, d = x2.shape
    dh = og.shape[1]
    body = functools.partial(_mix_body, alpha=alpha)
    row = lambda i: (i, 0)
    fix = lambda i: (0, 0)
    return pl.pallas_call(
        body,
        grid=(m // tm,),
        in_specs=[pl.BlockSpec((tm, dh), row), pl.BlockSpec((tm, dh), row),
                  pl.BlockSpec((dh, d), fix), pl.BlockSpec((dh, d), fix),
                  pl.BlockSpec((tm, d), row),
                  pl.BlockSpec((1, d), fix), pl.BlockSpec((1, d), fix),
                  pl.BlockSpec((d, LANES), fix), pl.BlockSpec((1, LANES), fix)],
        out_specs=[pl.BlockSpec((tm, d), row), pl.BlockSpec((tm * ROW_SUB, LANES), row),
                   pl.BlockSpec((tm, LANES), row), pl.BlockSpec((1, LANES), fix)],
        out_shape=[jax.ShapeDtypeStruct((m, d), F32),
                   jax.ShapeDtypeStruct((m * ROW_SUB, LANES), jnp.uint32),
                   jax.ShapeDtypeStruct((m, LANES), F32),
                   jax.ShapeDtypeStruct((1, LANES), F32)],
        compiler_params=_cparams(("arbitrary",)),
        name="mix_ln_route",
    )(og, osm, wa, wb, x2, g_row, b_row, wr, br)


def _dispatch_body(pos_ref, src_ref, init_ref, dst_ref, sem, *, tt):
    del init_ref

    def issue(t, carry):
        srow = pl.multiple_of(t * ROW_SUB, ROW_SUB)
        for k in range(TOP_K):
            drow = pl.multiple_of(pos_ref[t * TOP_K + k] * ROW_SUB, ROW_SUB)
            pltpu.make_async_copy(src_ref.at[pl.ds(srow, ROW_SUB)], dst_ref.at[pl.ds(drow, ROW_SUB)], sem).start()
        return carry

    lax.fori_loop(0, tt, issue, 0)
    n = tt * ROW_SUB
    for _ in range(TOP_K):
        pltpu.make_async_copy(src_ref.at[pl.ds(0, n)], dst_ref.at[pl.ds(0, n)], sem).wait()


def _dispatch_rows(pos_flat, hbp, cap, tt):
    m = hbp.shape[0] // ROW_SUB
    body = functools.partial(_dispatch_body, tt=tt)
    init = jnp.zeros((cap * ROW_SUB, LANES), jnp.uint32)
    return pl.pallas_call(
        body,
        grid=(m // tt,),
        in_specs=[pl.BlockSpec((tt * TOP_K,), lambda i: (i,), memory_space=pltpu.SMEM),
                  pl.BlockSpec((tt * ROW_SUB, LANES), lambda i: (i, 0)), pl.BlockSpec(memory_space=pl.ANY)],
        out_specs=pl.BlockSpec(memory_space=pl.ANY),
        out_shape=jax.ShapeDtypeStruct((cap * ROW_SUB, LANES), jnp.uint32),
        scratch_shapes=[pltpu.SemaphoreType.DMA(())],
        input_output_aliases={2: 0},
        compiler_params=_cparams(("arbitrary",)),
        name="moe_dispatch",
    )(pos_flat, hbp, init)


MOE_COL = 256
MOE_GROUP = 2


def _moe_body(te_ref, nu_ref, x_ref, wg_ref, bg_ref, wu_ref, bu_ref, wd_ref, bd_ref,
              o_ref, xb_ref, acc_ref, *, n_f, tm, tf):
    g = pl.program_id(2)
    f = pl.program_id(1)
    used = pl.program_id(0) * MOE_GROUP + g < nu_ref[0]
    half = xb_ref.shape[2] // 2
    xb_ref = xb_ref.at[g]
    acc_ref = acc_ref.at[g]

    @pl.when(used & (f == 0))
    def _():
        lo, hi = _unpack_rows(x_ref, 0, tm)
        for r in range(ROW_SUB):
            xb_ref[:, r * LANES:(r + 1) * LANES] = lo[r].astype(BF16)
            xb_ref[:, half + r * LANES:half + (r + 1) * LANES] = hi[r].astype(BF16)

    @pl.when(used)
    def _():
        xb = xb_ref[...]
        part = None
        for c in range(tf // MOE_COL):
            cs = slice(c * MOE_COL, (c + 1) * MOE_COL)
            hg = jnp.minimum(_dot(xb, wg_ref[0, :, cs].astype(BF16)) + bg_ref[0, :, cs], SWIGLU_LIMIT)
            hu = jnp.clip(_dot(xb, wu_ref[0, :, cs].astype(BF16)) + bu_ref[0, :, cs], -SWIGLU_LIMIT, SWIGLU_LIMIT)
            hh = ((hu + 1.0) * (hg * _sigmoid(SWIGLU_ALPHA * hg))).astype(BF16)
            p = _dot(hh, wd_ref[0, cs, :].astype(BF16))
            part = p if part is None else part + p

        @pl.when(f == 0)
        def _():
            acc_ref[...] = part

        @pl.when(f != 0)
        def _():
            acc_ref[...] += part

        @pl.when(f == n_f - 1)
        def _():
            _pack_rows(acc_ref[...] + bd_ref[0], o_ref, tm)

    @pl.when(jnp.logical_not(used) & (f == n_f - 1))
    def _():
        o_ref[...] = jnp.zeros_like(o_ref)


def _moe_ffn(tile_expert, n_used, x_rows, w_gate, b_gate, w_up, b_up, w_down, b_down, tm, tf):
    n_e, d, d_ff = w_gate.shape
    n_tiles = x_rows.shape[0] // (tm * ROW_SUB)
    n_f = d_ff // tf
    grp = MOE_GROUP
    assert n_tiles % grp == 0
    body = functools.partial(_moe_body, n_f=n_f, tm=tm, tf=tf)

    def tile(p, g):
        return p * grp + g

    def fcol(p, f, nu):
        return jnp.where(p * grp < nu[0], f, n_f - 1)

    def xrow(p, f, g):
        return jnp.where(f == 0, tile(p, g), tile(p, grp - 1))

    def orow(p, f, g):
        return jnp.where(f == n_f - 1, tile(p, g), tile(p, 0))

    grid_spec = pltpu.PrefetchScalarGridSpec(
        num_scalar_prefetch=2,
        grid=(n_tiles // grp, n_f, grp),
        in_specs=[pl.BlockSpec((tm * ROW_SUB, LANES), lambda p, f, g, te, nu: (xrow(p, f, g), 0)),
                  pl.BlockSpec((1, d, tf), lambda p, f, g, te, nu: (te[tile(p, g)], 0, fcol(p, f, nu))),
                  pl.BlockSpec((1, 1, tf), lambda p, f, g, te, nu: (te[tile(p, g)], 0, fcol(p, f, nu))),
                  pl.BlockSpec((1, d, tf), lambda p, f, g, te, nu: (te[tile(p, g)], 0, fcol(p, f, nu))),
                  pl.BlockSpec((1, 1, tf), lambda p, f, g, te, nu: (te[tile(p, g)], 0, fcol(p, f, nu))),
                  pl.BlockSpec((1, tf, d), lambda p, f, g, te, nu: (te[tile(p, g)], fcol(p, f, nu), 0)),
                  pl.BlockSpec((1, 1, d), lambda p, f, g, te, nu: (te[tile(p, g)], 0, 0))],
        out_specs=pl.BlockSpec((tm * ROW_SUB, LANES), lambda p, f, g, te, nu: (orow(p, f, g), 0)),
        scratch_shapes=[pltpu.VMEM((grp, tm, d), BF16), pltpu.VMEM((grp, tm, d), F32)],
    )
    return pl.pallas_call(
        body,
        grid_spec=grid_spec,
        out_shape=jax.ShapeDtypeStruct(x_rows.shape, jnp.uint32),
        compiler_params=_cparams(("arbitrary", "arbitrary", "arbitrary")),
        name="moe_ffn",
    )(tile_expert, n_used, x_rows, w_gate, b_gate.reshape(n_e, 1, d_ff),
      w_up, b_up.reshape(n_e, 1, d_ff), w_down, b_down.reshape(n_e, 1, d))


def _final_body(pos_cur_ref, pos_nxt_ref, h_ref, route_ref, g_ref, b_ref, src_ref,
                o_ref, buf_ref, sem, *, alpha, tt, n_steps):
    i = pl.program_id(0)
    slot_rows = tt * TOP_K * ROW_SUB

    def issue(pref, slot):
        def one(t, carry):
            for k in range(TOP_K):
                srow = pl.multiple_of(pref[t * TOP_K + k] * ROW_SUB, ROW_SUB)
                drow = pl.multiple_of(slot * slot_rows + (k * tt + t) * ROW_SUB, ROW_SUB)
                pltpu.make_async_copy(src_ref.at[pl.ds(srow, ROW_SUB)], buf_ref.at[pl.ds(drow, ROW_SUB)],
                                      sem.at[slot]).start()
            return carry
        lax.fori_loop(0, tt, one, 0)

    @pl.when(i == 0)
    def _():
        issue(pos_cur_ref, 0)

    @pl.when(i + 1 < n_steps)
    def _():
        issue(pos_nxt_ref, (i + 1) % 2)

    slot = i % 2
    base = pl.multiple_of(slot * slot_rows, ROW_SUB)
    pltpu.make_async_copy(src_ref.at[pl.ds(0, slot_rows)], buf_ref.at[pl.ds(base, slot_rows)], sem.at[slot]).wait()

    ff_lo = [None] * ROW_SUB
    ff_hi = [None] * ROW_SUB
    for k in range(TOP_K):
        gate = route_ref[:, TOP_K + k:TOP_K + k + 1]
        lo, hi = _unpack_rows(buf_ref, base + k * tt * ROW_SUB, tt)
        for s in range(ROW_SUB):
            ff_lo[s] = lo[s] * gate if k == 0 else ff_lo[s] + lo[s] * gate
            ff_hi[s] = hi[s] * gate if k == 0 else ff_hi[s] + hi[s] * gate
    ff = jnp.concatenate(ff_lo + ff_hi, axis=1)
    o_ref[...] = _layernorm(alpha * h_ref[...] + ff, g_ref[...], b_ref[...])


def _combine_ln(pos_flat, h, route, g_row, b_row, outs, alpha, tt):
    m, d = h.shape
    n_steps = m // tt
    body = functools.partial(_final_body, alpha=alpha, tt=tt, n_steps=n_steps)
    row = lambda i: (i, 0)
    fix = lambda i: (0, 0)
    return pl.pallas_call(
        body,
        grid=(n_steps,),
        in_specs=[pl.BlockSpec((tt * TOP_K,), lambda i: (i,), memory_space=pltpu.SMEM),
                  pl.BlockSpec((tt * TOP_K,), lambda i: (jnp.minimum(i + 1, n_steps - 1),),
                               memory_space=pltpu.SMEM),
                  pl.BlockSpec((tt, d), row), pl.BlockSpec((tt, LANES), row),
                  pl.BlockSpec((1, d), fix), pl.BlockSpec((1, d), fix),
                  pl.BlockSpec(memory_space=pl.ANY)],
        out_specs=pl.BlockSpec((tt, d), row),
        out_shape=jax.ShapeDtypeStruct((m, d), F32),
        scratch_shapes=[pltpu.VMEM((2 * tt * TOP_K * ROW_SUB, LANES), jnp.uint32),
                        pltpu.SemaphoreType.DMA((2,))],
        compiler_params=_cparams(("arbitrary",)),
        name="combine_ln",
    )(pos_flat, pos_flat, h, route, g_row, b_row, outs)


def _row_tile(m, pref):
    t = min(m, pref)
    while m % t:
        t //= 2
    return t


def _route_tables(route, counts_row, m, tm):
    n_assign = m * TOP_K
    e_tok = route[:, :TOP_K].astype(jnp.int32)
    rank = route[:, 2 * TOP_K:3 * TOP_K].astype(jnp.int32)
    counts = counts_row[0, :N_EXPERTS].astype(jnp.int32)
    padded = (counts + tm - 1) // tm * tm
    pends = jnp.cumsum(padded)
    pstarts = pends - padded
    pos = pstarts[e_tok] + rank
    n_tiles = -(-n_assign // tm) + N_EXPERTS
    tile_start = jnp.arange(n_tiles, dtype=jnp.int32) * tm
    tile_expert = jnp.minimum(jnp.sum((pends[None, :] <= tile_start[:, None]).astype(jnp.int32), axis=1),
                              N_EXPERTS - 1)
    n_used = (pends[-1] // tm).astype(jnp.int32).reshape(1)
    last_used = jnp.maximum(n_used[0] - 1, 0)
    tile_expert = jnp.where(tile_start // tm < n_used[0], tile_expert, tile_expert[last_used])
    return pos.reshape(n_assign), tile_expert, n_used


def _layer(x, conv_hist, s_gdn, h_re, h_im, chunk, alpha, p):
    b, t, d = x.shape
    m = b * t
    d_gdn = GDN_HEADS * HEAD_DIM
    x2 = x.reshape(m, d)

    proj = _in_proj(x2, p['w_in'], _row_tile(m, 1024), p['proj_tn'])
    proj3 = proj.reshape(b, t, proj.shape[1])
    conv_new = proj3[:, t - (CONV_W - 1):, :3 * d_gdn]

    q, k, v, cols, gct = _gdn_prep(proj3, conv_hist, p['w_conv'], p['alog_row'], p['dtb_row'],
                                   chunk, p['ab_block'])
    o_gdn, s_new = _gdn(q, k, v, cols, gct, proj3, p['wn_row'], s_gdn, chunk, p['z_block'])

    n_gb = p['s5_mats'][0].shape[1]
    half = GROUPS_PER_BLOCK * SSM_P
    yg, hre_new, him_new = _s5(proj3, p['s5_mats'], h_re.reshape(b, n_gb, 1, half),
                               h_im.reshape(b, n_gb, 1, half), p['u_block0'])
    o_ssm = _glu(yg.reshape(m, yg.shape[2]), p['w_glu'], p['b_glu_row'], _row_tile(m, 512))

    h, hb, route, counts_row = _mix_ln_route(o_gdn.reshape(m, d_gdn), o_ssm, p['w_out_a'], p['w_out_b'], x2,
                                             p['ln1_g'], p['ln1_b'], p['w_router'], p['b_router'], alpha,
                                             _row_tile(m, 256))

    big = m * TOP_K >= 8 * p['moe_tm']
    tm = p['moe_tm'] if big else 128
    pos_flat, tile_expert, n_used = _route_tables(route, counts_row, m, tm)
    cap = tile_expert.shape[0] * tm
    x_rows = _dispatch_rows(pos_flat, hb, cap, _row_tile(m, 256))
    outs = _moe_ffn(tile_expert, n_used, x_rows, p['w_gate'], p['b_gate'],
                    p['w_up'], p['b_up'], p['w_down'], p['b_down'], tm, p['moe_tf'])
    y = _combine_ln(pos_flat, h, route, p['ln2_g'], p['ln2_b'], outs, alpha, _row_tile(m, 128))
    g_all = h_re.shape[1]
    return (y.reshape(b, t, d), conv_new, s_new,
            hre_new.reshape(b, g_all, SSM_P), him_new.reshape(b, g_all, SSM_P))


def _pad_lanes(v, fill=0.0):
    return jnp.pad(v.astype(F32), (0, LANES - v.shape[0]), constant_values=fill).reshape(1, LANES)


def _layer_params(l, w_in, w_conv, a_log, dt_bias, w_onorm, lam_re, lam_im, log_dt, b_re, b_im, c_re, c_im,
                  d_skip, w_glu, b_glu, w_out, ln1_g, ln1_b, w_router, b_router, w_gate, b_gate,
                  w_up, b_up, w_down, b_down, ln2_g, ln2_b):
    d_model = w_in.shape[1]
    d_gdn = GDN_HEADS * HEAD_DIM
    d_qkvz = 4 * d_gdn
    d_ssm = d_model - d_gdn
    wi = w_in[l]
    proj_tn = 896
    n_cols = d_qkvz + d_ssm + LANES
    n_pad = -(-n_cols // proj_tn) * proj_tn
    w_in_r = jnp.concatenate([wi[:, :d_qkvz], wi[:, d_qkvz + 2 * GDN_HEADS:],
                              wi[:, d_qkvz:d_qkvz + 2 * GDN_HEADS],
                              jnp.zeros((d_model, n_pad - d_qkvz - d_ssm - 2 * GDN_HEADS), wi.dtype)], axis=1)
    wo = w_out[l].astype(BF16)
    wr = jnp.pad(w_router[l].astype(F32), ((0, 0), (0, LANES - N_EXPERTS)))
    return {
        'w_in': w_in_r.astype(BF16), 'proj_tn': proj_tn,
        'z_block': 3, 'u_block0': (d_qkvz) // LANES, 'ab_block': (d_qkvz + d_ssm) // LANES,
        'w_conv': w_conv[l].astype(F32),
        'alog_row': _pad_lanes(a_log[l]), 'dtb_row': _pad_lanes(dt_bias[l]),
        'wn_row': w_onorm[l].astype(F32).reshape(1, HEAD_DIM),
        's5_mats': _s5_matrices(lam_re[l], lam_im[l], log_dt[l], b_re[l], b_im[l], c_re[l], c_im[l], d_skip[l]),
        'w_glu': w_glu[l].astype(BF16), 'b_glu_row': b_glu[l].astype(F32).reshape(1, d_ssm),
        'w_out_a': wo[:d_gdn], 'w_out_b': wo[d_gdn:],
        'ln1_g': ln1_g[l].astype(F32).reshape(1, d_model), 'ln1_b': ln1_b[l].astype(F32).reshape(1, d_model),
        'w_router': wr, 'b_router': _pad_lanes(b_router[l]),
        'w_gate': w_gate[l], 'b_gate': b_gate[l], 'w_up': w_up[l], 'b_up': b_up[l],
        'w_down': w_down[l], 'b_down': b_down[l],
        'ln2_g': ln2_g[l].astype(F32).reshape(1, d_model), 'ln2_b': ln2_b[l].astype(F32).reshape(1, d_model),
        'moe_tm': 512, 'moe_tf': 512,
    }


def kernel(x_prompt, x_sample, state_conv, state_gdn, state_ssm_re, state_ssm_im, w_in, w_conv, a_log, dt_bias, w_onorm, lam_re, lam_im, log_dt, b_re, b_im, c_re, c_im, d_skip, w_glu, b_glu, w_out, ln1_g, ln1_b, w_router, b_router, w_gate, b_gate, w_up, b_up, w_down, b_down, ln2_g, ln2_b):
    depth = w_in.shape[0]
    alpha = (2.0 * depth) ** 0.25
    bp, seq, _ = x_prompt.shape
    chunk_p = 64
    d_qkv = state_conv.shape[-1]
    n_groups, n_p = state_ssm_re.shape[-2:]
    yp, ys = x_prompt, x_sample
    outs_p = [[], [], [], []]
    outs_s = [[], [], [], []]
    for l in range(depth):
        p = _layer_params(l, w_in, w_conv, a_log, dt_bias, w_onorm, lam_re, lam_im, log_dt, b_re, b_im,
                          c_re, c_im, d_skip, w_glu, b_glu, w_out, ln1_g, ln1_b, w_router, b_router,
                          w_gate, b_gate, w_up, b_up, w_down, b_down, ln2_g, ln2_b)
        yp, cp, sp, rp, ip = _layer(
            yp, jnp.zeros((bp, CONV_W - 1, d_qkv), F32),
            jnp.zeros((bp, GDN_HEADS, HEAD_DIM, HEAD_DIM), F32),
            jnp.zeros((bp, n_groups, n_p), F32), jnp.zeros((bp, n_groups, n_p), F32),
            chunk_p, alpha, p)
        ys, cs, ss, rs, is_ = _layer(
            ys, state_conv[l].astype(F32), state_gdn[l].astype(F32),
            state_ssm_re[l].astype(F32), state_ssm_im[l].astype(F32),
            ys.shape[1], alpha, p)
        for acc, val in zip(outs_p, (cp, sp, rp, ip)):
            acc.append(val)
        for acc, val in zip(outs_s, (cs, ss, rs, is_)):
            acc.append(val)
    return (yp, ys, *[jnp.stack(a) for a in outs_p], *[jnp.stack(a) for a in outs_s])
```

```python
import functools
import math

import jax
import jax.numpy as jnp
from jax import lax
from jax.experimental import pallas as pl
from jax.experimental.pallas import tpu as pltpu

F32 = jnp.float32
BF16 = jnp.bfloat16

GDN_HEADS = 8
HEAD_DIM = 128
CONV_W = 4
SSM_CG = 16
SSM_P = 64
N_EXPERTS = 32
TOP_K = 4
SWIGLU_ALPHA = 1.702
SWIGLU_LIMIT = 7.0
LN_EPS = 1e-5
RMS_EPS = 1e-6
L2_EPS = 1e-6

LANES = 128
SUBLANES = 8
S5_SUB = 8
S5_TILE = 2048
GDN_TILE = 128
GROUPS_PER_BLOCK = LANES // SSM_CG
VMEM_LIMIT = 56 * 1024 * 1024

NT_DIMS = (((1,), (1,)), ((), ()))
TN_DIMS = (((0,), (0,)), ((), ()))


def _dot(a, b, dims=(((1,), (0,)), ((), ()))):
    return lax.dot_general(a, b, dims, preferred_element_type=F32)


def _split(a):
    hi = a.astype(BF16)
    lo = (a - hi.astype(F32)).astype(BF16)
    return hi, lo


def _dot_x3(a, b, dims=(((1,), (0,)), ((), ()))):
    ah, al = _split(a)
    bh, bl = _split(b)
    return _dot(ah, bh, dims) + (_dot(ah, bl, dims) + _dot(al, bh, dims))


def _dot_bf(a, b, dims=(((1,), (0,)), ((), ()))):
    return _dot(a.astype(BF16), b.astype(BF16), dims)


_gdn_mm = _dot_bf


def _sigmoid(x):
    return 1.0 / (1.0 + jnp.exp(-x))


ROW_SUB = SUBLANES


def _pack_rows(x, ref, tm):
    half = x.shape[1] // 2
    for s in range(ROW_SUB):
        lo = x[:, s * LANES:(s + 1) * LANES].astype(BF16).astype(F32)
        hi = x[:, half + s * LANES:half + (s + 1) * LANES].astype(BF16).astype(F32)
        word = (lax.bitcast_convert_type(lo, jnp.uint32) >> 16) | lax.bitcast_convert_type(hi, jnp.uint32)
        ref[pl.ds(s, tm, stride=ROW_SUB), :] = word


def _unpack_rows(ref, base, tm):
    lo, hi = [], []
    for s in range(ROW_SUB):
        word = ref[pl.ds(base + s, tm, stride=ROW_SUB), :]
        lo.append(lax.bitcast_convert_type(word << 16, F32))
        hi.append(lax.bitcast_convert_type(word & jnp.uint32(0xFFFF0000), F32))
    return lo, hi


def _cparams(sem):
    return pltpu.CompilerParams(dimension_semantics=sem, vmem_limit_bytes=VMEM_LIMIT)


def _proj_body(x_ref, w_ref, o_ref, xb_ref):
    @pl.when(pl.program_id(1) == 0)
    def _():
        xb_ref[...] = x_ref[...].astype(BF16)

    o_ref[...] = _dot(xb_ref[...], w_ref[...])


def _in_proj(x2, w_bf, tm, tn):
    m, k = x2.shape
    n = w_bf.shape[1]
    return pl.pallas_call(
        _proj_body,
        grid=(m // tm, n // tn),
        in_specs=[pl.BlockSpec((tm, k), lambda i, j: (i, 0)),
                  pl.BlockSpec((k, tn), lambda i, j: (0, j))],
        out_specs=pl.BlockSpec((tm, tn), lambda i, j: (i, j)),
        out_shape=jax.ShapeDtypeStruct((m, n), F32),
        scratch_shapes=[pltpu.VMEM((tm, k), BF16)],
        compiler_params=_cparams(("parallel", "arbitrary")),
        name="in_proj",
    )(x2, w_bf)


def _gdn_prep_body(qkv_ref, ab_ref, hist_ref, wc_ref, alog_ref, dtb_ref,
                   q_ref, k_ref, v_ref, cols_ref, gct_ref, xbuf_ref, *, tt, chunk):
    d_gdn = GDN_HEADS * HEAD_DIM
    halo = SUBLANES

    @pl.when(pl.program_id(1) == 0)
    def _():
        xbuf_ref[0:halo, :] = jnp.zeros((halo, 3 * d_gdn), F32)
        xbuf_ref[halo - (CONV_W - 1):halo, :] = hist_ref[0]

    xbuf_ref[halo:halo + tt, :] = qkv_ref[0]

    for part, out_ref in enumerate((q_ref, k_ref, v_ref)):
        c0 = part * d_gdn
        y = None
        for j in range(CONV_W):
            r0 = halo - (CONV_W - 1) + j
            term = xbuf_ref[r0:r0 + tt, c0:c0 + d_gdn] * wc_ref[j:j + 1, c0:c0 + d_gdn]
            y = term if y is None else y + term
        s = y * _sigmoid(y)
        if part == 2:
            out_ref[0] = s
        else:
            for h in range(GDN_HEADS):
                sh = s[:, h * HEAD_DIM:(h + 1) * HEAD_DIM]
                nrm = sh * lax.rsqrt(jnp.sum(sh * sh, axis=-1, keepdims=True) + L2_EPS)
                if part == 0:
                    nrm = nrm * (HEAD_DIM ** -0.5)
                out_ref[0, :, h * HEAD_DIM:(h + 1) * HEAD_DIM] = nrm

    xbuf_ref[0:halo, :] = xbuf_ref[tt:tt + halo, :]

    ab = ab_ref[0]
    lane = lax.broadcasted_iota(jnp.int32, (tt, LANES), 1)
    is_a = lane < GDN_HEADS
    z = ab + dtb_ref[...]
    softplus = jnp.maximum(z, 0.0) + jnp.log1p(jnp.exp(-jnp.abs(z)))
    g = jnp.where(is_a, -jnp.exp(alog_ref[...]) * softplus, 0.0)
    beta = _sigmoid(ab)

    shift = int(math.log2(chunk))
    r = lax.broadcasted_iota(jnp.int32, (tt, tt), 0)
    c = lax.broadcasted_iota(jnp.int32, (tt, tt), 1)
    same = (r >> shift) == (c >> shift)
    m_incl = jnp.where(same & (r >= c), 1.0, 0.0).astype(BF16)
    m_all = jnp.where(same, 1.0, 0.0).astype(BF16)
    gc = _dot_exact_lhs_rhs(m_incl, g)
    glast = _dot_exact_lhs_rhs(m_all, g)
    eg = jnp.exp(gc)
    egl = jnp.exp(glast - gc)
    egt = jnp.exp(glast)
    zero = jnp.zeros_like(gc)
    cols = (jnp.where(is_a, gc, zero)
            + jnp.where((lane >= 8) & (lane < 16), beta, zero)
            + pltpu.roll(jnp.where(is_a, eg, zero), 16, 1)
            + pltpu.roll(jnp.where(is_a, egl, zero), 24, 1)
            + pltpu.roll(jnp.where(is_a, egt, zero), 32, 1))
    cols_ref[0] = cols

    er = lax.broadcasted_iota(jnp.int32, (SUBLANES, LANES), 0)
    ec = lax.broadcasted_iota(jnp.int32, (SUBLANES, LANES), 1)
    sel = jnp.where(er == ec, 1.0, 0.0).astype(BF16)
    gct_ref[0] = _dot_exact_rhs(sel, jnp.where(is_a, gc, zero), NT_DIMS)


def _three_pieces(a):
    a0 = a.astype(BF16)
    r1 = a - a0.astype(F32)
    a1 = r1.astype(BF16)
    a2 = (r1 - a1.astype(F32)).astype(BF16)
    return a0, a1, a2


def _dot_exact_lhs_rhs(mask_bf, a):
    a0, a1, a2 = _three_pieces(a)
    return _dot(mask_bf, a0) + (_dot(mask_bf, a1) + _dot(mask_bf, a2))


def _dot_exact_rhs(mask_bf, a, dims):
    a0, a1, a2 = _three_pieces(a)
    return _dot(mask_bf, a0, dims) + (_dot(mask_bf, a1, dims) + _dot(mask_bf, a2, dims))


def _gdn_prep(proj3, hist, w_conv, alog_p, dtb_p, chunk, ab_block):
    b, t, _ = proj3.shape
    d_gdn = GDN_HEADS * HEAD_DIM
    tt = min(t, GDN_TILE)
    body = functools.partial(_gdn_prep_body, tt=tt, chunk=chunk)
    big = jax.ShapeDtypeStruct((b, t, d_gdn), F32)
    return pl.pallas_call(
        body,
        grid=(b, t // tt),
        in_specs=[pl.BlockSpec((1, tt, 3 * d_gdn), lambda i, j: (i, j, 0)),
                  pl.BlockSpec((1, tt, LANES), lambda i, j: (i, j, ab_block)),
                  pl.BlockSpec((1, CONV_W - 1, 3 * d_gdn), lambda i, j: (i, 0, 0)),
                  pl.BlockSpec((CONV_W, 3 * d_gdn), lambda i, j: (0, 0)),
                  pl.BlockSpec((1, LANES), lambda i, j: (0, 0)),
                  pl.BlockSpec((1, LANES), lambda i, j: (0, 0))],
        out_specs=[pl.BlockSpec((1, tt, d_gdn), lambda i, j: (i, j, 0)),
                   pl.BlockSpec((1, tt, d_gdn), lambda i, j: (i, j, 0)),
                   pl.BlockSpec((1, tt, d_gdn), lambda i, j: (i, j, 0)),
                   pl.BlockSpec((1, tt, LANES), lambda i, j: (i, j, 0)),
                   pl.BlockSpec((1, SUBLANES, tt), lambda i, j: (i, 0, j))],
        out_shape=[big, big, big,
                   jax.ShapeDtypeStruct((b, t, LANES), F32),
                   jax.ShapeDtypeStruct((b, SUBLANES, t), F32)],
        scratch_shapes=[pltpu.VMEM((tt + SUBLANES, 3 * d_gdn), F32)],
        compiler_params=_cparams(("parallel", "arbitrary")),
        name="gdn_prep",
    )(proj3, proj3, hist, w_conv, alog_p, dtb_p)


def _unit_lower_inverses(mats, r, c, chunk):
    base = 16
    eye = jnp.where(r == c, 1.0, 0.0)

    def blk(bs):
        s = int(math.log2(bs))
        return (r >> s) == (c >> s)

    d1 = [jnp.where(blk(base), a, 0.0) for a in mats]
    d2 = [_gdn_mm(x, x) for x in d1]
    d4 = [_gdn_mm(x, x) for x in d2]
    d8 = [_gdn_mm(x, x) for x in d4]
    t = [eye - x for x in d1]
    t = [x + _gdn_mm(x, y) for x, y in zip(t, d2)]
    t = [x + _gdn_mm(x, y) for x, y in zip(t, d4)]
    t = [x + _gdn_mm(x, y) for x, y in zip(t, d8)]
    bs = base
    while bs < chunk:
        off_mask = blk(2 * bs) & jnp.logical_not(blk(bs))
        inner = [_gdn_mm(jnp.where(off_mask, a, 0.0), x) for a, x in zip(mats, t)]
        t = [x - _gdn_mm(x, y) for x, y in zip(t, inner)]
        bs *= 2
    return t


def _gdn_body(q_ref, k_ref, v_ref, cols_ref, gct_ref, z_ref, wn_ref, s0_ref,
              o_ref, s_ref, *, tb, chunk):
    @pl.when(pl.program_id(1) == 0)
    def _():
        s_ref[...] = s0_ref[...]

    shift = int(math.log2(chunk))
    r = lax.broadcasted_iota(jnp.int32, (tb, tb), 0)
    c = lax.broadcasted_iota(jnp.int32, (tb, tb), 1)
    same = (r >> shift) == (c >> shift)
    incl = same & (r >= c)
    strict = same & (r > c)
    n_chunks = tb // chunk

    heads = range(GDN_HEADS)
    hsl = [slice(h * HEAD_DIM, (h + 1) * HEAD_DIM) for h in heads]
    q = [q_ref[0, :, s] for s in hsl]
    k = [k_ref[0, :, s] for s in hsl]
    beta = [cols_ref[0, :, 8 + h:9 + h] for h in heads]
    eg = [cols_ref[0, :, 16 + h:17 + h] for h in heads]
    egl = [cols_ref[0, :, 24 + h:25 + h] for h in heads]
    decay = [jnp.exp(jnp.where(incl, cols_ref[0, :, h:h + 1] - gct_ref[0, h:h + 1, :], -jnp.inf)) for h in heads]
    kb = [k[h] * beta[h] for h in heads]
    k_bf = [x.astype(BF16) for x in k]
    a = [jnp.where(strict, _dot(kb[h].astype(BF16), k_bf[h], NT_DIMS) * decay[h], 0.0) for h in heads]
    tinv = _unit_lower_inverses(a, r, c, chunk)
    sol = [_gdn_mm(tinv[h], jnp.concatenate([v_ref[0, :, hsl[h]] * beta[h], kb[h] * eg[h]], axis=1)) for h in heads]
    u = [x[:, :HEAD_DIM] for x in sol]
    w_bf = [x[:, HEAD_DIM:].astype(BF16) for x in sol]
    attn = [(_dot(q[h].astype(BF16), k_bf[h], NT_DIMS) * decay[h]).astype(BF16) for h in heads]
    q_dec = [(q[h] * eg[h]).astype(BF16) for h in heads]
    k_dec = [(k[h] * egl[h]).astype(BF16) for h in heads]

    s = [s_ref[0, h] for h in heads]
    v_new = [[] for _ in heads]
    o_state = [[] for _ in heads]
    for ci in range(n_chunks):
        rs = slice(ci * chunk, (ci + 1) * chunk)
        s_bf = [x.astype(BF16) for x in s]
        vn = [u[h][rs] - _dot(w_bf[h][rs], s_bf[h]) for h in heads]
        for h in heads:
            o_state[h].append(_dot(q_dec[h][rs], s_bf[h]))
            v_new[h].append(vn[h])
        s = [s[h] * cols_ref[0, ci * chunk:ci * chunk + 1, 32 + h:33 + h]
             + _dot(k_dec[h][rs], vn[h].astype(BF16), TN_DIMS) for h in heads]
    for h in heads:
        s_ref[0, h] = s[h]

    def cat(parts):
        return parts[0] if len(parts) == 1 else jnp.concatenate(parts, axis=0)

    for h in heads:
        o = cat(o_state[h]) + _dot(attn[h], cat(v_new[h]).astype(BF16))
        zh = z_ref[0, :, hsl[h]]
        o = (o * lax.rsqrt(jnp.mean(o * o, axis=-1, keepdims=True) + RMS_EPS) * wn_ref[...]
             * (zh * _sigmoid(zh)))
        o_ref[0, :, hsl[h]] = o.astype(BF16)


def _gdn(q, k, v, cols, gct, proj3, wn, s0, chunk, z_block):
    b, t, d_gdn = q.shape
    tb = min(t, GDN_TILE)
    body = functools.partial(_gdn_body, tb=tb, chunk=chunk)
    tile = lambda i, j: (i, j, 0)
    return pl.pallas_call(
        body,
        grid=(b, t // tb),
        in_specs=[pl.BlockSpec((1, tb, d_gdn), tile),
                  pl.BlockSpec((1, tb, d_gdn), tile),
                  pl.BlockSpec((1, tb, d_gdn), tile),
                  pl.BlockSpec((1, tb, LANES), tile),
                  pl.BlockSpec((1, SUBLANES, tb), lambda i, j: (i, 0, j)),
                  pl.BlockSpec((1, tb, d_gdn), lambda i, j: (i, j, z_block)),
                  pl.BlockSpec((1, HEAD_DIM), lambda i, j: (0, 0)),
                  pl.BlockSpec((1, GDN_HEADS, HEAD_DIM, HEAD_DIM), lambda i, j: (i, 0, 0, 0))],
        out_specs=[pl.BlockSpec((1, tb, d_gdn), tile),
                   pl.BlockSpec((1, GDN_HEADS, HEAD_DIM, HEAD_DIM), lambda i, j: (i, 0, 0, 0))],
        out_shape=[jax.ShapeDtypeStruct((b, t, d_gdn), BF16),
                   jax.ShapeDtypeStruct((b, GDN_HEADS, HEAD_DIM, HEAD_DIM), F32)],
        compiler_params=_cparams(("parallel", "arbitrary")),
        name="gdn_delta",
    )(q, k, v, cols, gct, proj3, wn, s0)


def _gelu_tanh(x):
    return 0.5 * x * (1.0 + jnp.tanh(math.sqrt(2.0 / math.pi) * (x + 0.044715 * (x * x * x))))


def _s5_body(u_ref, km_ref, wm_ref, vm_ref, lre_ref, lim_ref, dsk_ref, h0re_ref, h0im_ref,
             y_ref, hre_ref, him_ref, xbuf_ref, hbuf_ref, *, n_sub, seg):
    half = GROUPS_PER_BLOCK * SSM_P

    if seg is None:
        @pl.when(pl.program_id(2) == 0)
        def _():
            hre_ref[...] = h0re_ref[...]
            him_ref[...] = h0im_ref[...]

    u_f = [u_ref[0, pl.ds(j, n_sub, stride=S5_SUB), :] for j in range(S5_SUB)]
    u_b = [x.astype(BF16) for x in u_f]

    x = _dot(u_b[0], wm_ref[0, 0])
    for j in range(1, S5_SUB):
        x = x + _dot(u_b[j], wm_ref[j, 0])
    xbuf_ref[...] = x

    lre = lre_ref[0]
    lim = lim_ref[0]

    def advance(n, hre, him):
        hbuf_ref[pl.ds(n, 1), 0:half] = hre
        hbuf_ref[pl.ds(n, 1), half:2 * half] = him
        xr = xbuf_ref[pl.ds(n, 1), 0:half]
        xi = xbuf_ref[pl.ds(n, 1), half:2 * half]
        return (lre * hre - lim * him + xr, lre * him + lim * hre + xi)

    if seg is None:
        hre, him = lax.fori_loop(0, n_sub, lambda n, c: advance(n, *c), (hre_ref[0, 0], him_ref[0, 0]))
        hre_ref[0, 0] = hre
        him_ref[0, 0] = him
    else:
        def step(n, carry):
            q = n // seg
            first = n - q * seg == 0
            hre = jnp.where(first, h0re_ref[q, 0], carry[0])
            him = jnp.where(first, h0im_ref[q, 0], carry[1])
            hre, him = advance(n, hre, him)
            hre_ref[q, 0] = hre
            him_ref[q, 0] = him
            return hre, him

        lax.fori_loop(0, n_sub, step, (h0re_ref[0, 0], h0im_ref[0, 0]))

    h_b = hbuf_ref[...].astype(BF16)
    dsk = dsk_ref[0]
    for l in range(S5_SUB):
        y = _dot(h_b, vm_ref[l, 0])
        for d in range(l + 1):
            y = y + _dot(u_b[l - d], km_ref[d, 0])
        y = y + dsk * u_f[l]
        y_ref[0, pl.ds(l, n_sub, stride=S5_SUB), :] = _gelu_tanh(y)


def _s5(proj3, mats, h0re, h0im, u_block0):
    km, wm, vm, lre, lim, dsk = mats
    b, t, width = proj3.shape
    n_gb = km.shape[1]
    half = GROUPS_PER_BLOCK * SSM_P
    if b > 1 and b * t <= S5_TILE:
        y, hre, him = _s5_call(proj3.reshape(1, b * t, width), mats, h0re, h0im, u_block0,
                               tt=b * t, seg=t // S5_SUB, state_rows=b)
        return y.reshape(b, t, n_gb * LANES), hre, him
    return _s5_call(proj3, mats, h0re, h0im, u_block0, tt=min(t, S5_TILE), seg=None, state_rows=1)


def _s5_call(proj3, mats, h0re, h0im, u_block0, tt, seg, state_rows):
    km, wm, vm, lre, lim, dsk = mats
    b, t, _ = proj3.shape
    n_gb = km.shape[1]
    n_sub = tt // S5_SUB
    half = GROUPS_PER_BLOCK * SSM_P
    body = functools.partial(_s5_body, n_sub=n_sub, seg=seg)
    state_spec = pl.BlockSpec((state_rows, 1, 1, half), lambda g, i, j: (i, g, 0, 0))
    par_spec = pl.BlockSpec((1, 1, half), lambda g, i, j: (g, 0, 0))
    return pl.pallas_call(
        body,
        grid=(n_gb, b, t // tt),
        in_specs=[pl.BlockSpec((1, tt, LANES), lambda g, i, j: (i, j, u_block0 + g)),
                  pl.BlockSpec((S5_SUB, 1, LANES, LANES), lambda g, i, j: (0, g, 0, 0)),
                  pl.BlockSpec((S5_SUB, 1, LANES, 2 * half), lambda g, i, j: (0, g, 0, 0)),
                  pl.BlockSpec((S5_SUB, 1, 2 * half, LANES), lambda g, i, j: (0, g, 0, 0)),
                  par_spec, par_spec,
                  pl.BlockSpec((1, 1, LANES), lambda g, i, j: (g, 0, 0)),
                  state_spec, state_spec],
        out_specs=[pl.BlockSpec((1, tt, LANES), lambda g, i, j: (i, j, g)),
                   state_spec, state_spec],
        out_shape=[jax.ShapeDtypeStruct((b, t, n_gb * LANES), F32),
                   jax.ShapeDtypeStruct(h0re.shape, F32),
                   jax.ShapeDtypeStruct(h0im.shape, F32)],
        scratch_shapes=[pltpu.VMEM((n_sub, 2 * half), F32),
                        pltpu.VMEM((n_sub, 2 * half), F32)],
        compiler_params=_cparams(("parallel", "parallel", "arbitrary")),
        name="s5_scan",
    )(proj3, km, wm, vm, lre, lim, dsk, h0re, h0im)


def _s5_matrices(lam_re, lam_im, log_dt, b_re, b_im, c_re, c_im, d_skip):
    g, p = lam_re.shape
    n_gb = g // GROUPS_PER_BLOCK
    gpb = GROUPS_PER_BLOCK
    dt = jnp.exp(log_dt.astype(F32))
    lam = lax.complex(jnp.minimum(lam_re.astype(F32), -1e-4), lam_im.astype(F32))
    lam_bar = jnp.exp(lam * dt[:, None])
    b_bar = ((lam_bar - 1.0) / lam)[..., None] * lax.complex(b_re.astype(F32), b_im.astype(F32))
    c_c = lax.complex(c_re.astype(F32), c_im.astype(F32))
    pows = [jnp.ones_like(lam_bar)]
    for _ in range(S5_SUB):
        pows.append(pows[-1] * lam_bar)
    pw = jnp.stack(pows)
    eye = jnp.eye(gpb, dtype=F32)

    kd = jnp.real(jnp.einsum('gop,dgp,gpi->dgio', c_c, pw[:S5_SUB], b_bar))
    km = jnp.einsum('dbgio,gh->dbgiho', kd.reshape(S5_SUB, n_gb, gpb, SSM_CG, SSM_CG), eye)
    km = km.reshape(S5_SUB, n_gb, LANES, LANES)

    wj = pw[:S5_SUB][::-1][:, :, :, None] * b_bar[None]
    wj = jnp.transpose(wj, (0, 1, 3, 2)).reshape(S5_SUB, n_gb, gpb, SSM_CG, p)
    wre = jnp.einsum('dbgcp,gh->dbgchp', jnp.real(wj), eye).reshape(S5_SUB, n_gb, LANES, gpb * p)
    wim = jnp.einsum('dbgcp,gh->dbgchp', jnp.imag(wj), eye).reshape(S5_SUB, n_gb, LANES, gpb * p)
    wm = jnp.concatenate([wre, wim], axis=-1)

    cl = c_c[None] * pw[1:S5_SUB + 1][:, :, None, :]
    cl = jnp.transpose(cl, (0, 1, 3, 2)).reshape(S5_SUB, n_gb, gpb, p, SSM_CG)
    vre = jnp.einsum('dbgpc,gh->dbgphc', jnp.real(cl), eye).reshape(S5_SUB, n_gb, gpb * p, LANES)
    vim = jnp.einsum('dbgpc,gh->dbgphc', -jnp.imag(cl), eye).reshape(S5_SUB, n_gb, gpb * p, LANES)
    vm = jnp.concatenate([vre, vim], axis=-2)

    lam_s = pw[S5_SUB].reshape(n_gb, 1, gpb * p)
    dsk = d_skip.astype(F32).reshape(n_gb, 1, LANES)
    return (km.astype(BF16), wm.astype(BF16), vm.astype(BF16),
            jnp.real(lam_s), jnp.imag(lam_s), dsk)


def _glu_body(y_ref, w_ref, b_ref, o_ref):
    y = y_ref[...]
    gate = _dot(y.astype(BF16), w_ref[...]) + b_ref[...]
    o_ref[...] = (y * _sigmoid(gate)).astype(BF16)


def _glu(y2, w_bf, b_row, tm):
    m, d = y2.shape
    return pl.pallas_call(
        _glu_body,
        grid=(m // tm,),
        in_specs=[pl.BlockSpec((tm, d), lambda i: (i, 0)),
                  pl.BlockSpec((d, d), lambda i: (0, 0)),
                  pl.BlockSpec((1, d), lambda i: (0, 0))],
        out_specs=pl.BlockSpec((tm, d), lambda i: (i, 0)),
        out_shape=jax.ShapeDtypeStruct((m, d), BF16),
        compiler_params=_cparams(("parallel",)),
        name="s5_glu",
    )(y2, w_bf, b_row)


def _layernorm(v, g, b):
    mu = jnp.mean(v, axis=-1, keepdims=True)
    var = jnp.mean(jnp.square(v - mu), axis=-1, keepdims=True)
    return (v - mu) * lax.rsqrt(var + LN_EPS) * g + b


def _mix_body(og_ref, os_ref, wa_ref, wb_ref, x_ref, g_ref, b_ref, wr_ref, br_ref,
              h_ref, hb_ref, route_ref, cnt_ref, *, alpha):
    @pl.when(pl.program_id(0) == 0)
    def _():
        cnt_ref[...] = jnp.zeros_like(cnt_ref)

    mix = _dot(og_ref[...], wa_ref[...]) + _dot(os_ref[...], wb_ref[...])
    h = _layernorm(alpha * x_ref[...] + mix, g_ref[...], b_ref[...])
    h_ref[...] = h
    tm = h.shape[0]
    _pack_rows(h, hb_ref, tm)

    logits = _dot_x3(h, wr_ref[...]) + br_ref[...]
    lane = lax.broadcasted_iota(jnp.int32, (tm, LANES), 1)
    work = jnp.where(lane < N_EXPERTS, logits, -jnp.inf)
    vals, idxs = [], []
    for _ in range(TOP_K):
        mx = jnp.max(work, axis=-1, keepdims=True)
        ix = jnp.min(jnp.where(work == mx, lane, LANES), axis=-1, keepdims=True)
        vals.append(mx)
        idxs.append(ix)
        work = jnp.where(lane == ix, -jnp.inf, work)
    exps = [jnp.exp(v - vals[0]) for v in vals]
    denom = exps[0]
    for e in exps[1:]:
        denom = denom + e
    chosen = jnp.zeros((tm, LANES), F32)
    for k in range(TOP_K):
        chosen = jnp.where(lane == idxs[k], 1.0, chosen)
    rr = lax.broadcasted_iota(jnp.int32, (tm, tm), 0)
    cc = lax.broadcasted_iota(jnp.int32, (tm, tm), 1)
    lower = jnp.where(rr > cc, 1.0, 0.0).astype(BF16)
    before = cnt_ref[...] + _dot(lower, chosen.astype(BF16))
    cnt_ref[...] = cnt_ref[...] + jnp.sum(chosen, axis=0, keepdims=True)

    route = jnp.zeros((tm, LANES), F32)
    for k in range(TOP_K):
        rank = jnp.sum(jnp.where(lane == idxs[k], before, 0.0), axis=-1, keepdims=True)
        route = jnp.where(lane == k, idxs[k].astype(F32), route)
        route = jnp.where(lane == TOP_K + k, exps[k] / denom, route)
        route = jnp.where(lane == 2 * TOP_K + k, rank, route)
    route_ref[...] = route


def _mix_ln_route(og, osm, wa, wb, x2, g_row, b_row, wr, br, alpha, tm):
    m, d = x2.shape
    dh = og.shape[1]
    body = functools.partial(_mix_body, alpha=alpha)
    row = lambda i: (i, 0)
    fix = lambda i: (0, 0)
    return pl.pallas_call(
        body,
        grid=(m // tm,),
        in_specs=[pl.BlockSpec((tm, dh), row), pl.BlockSpec((tm, dh), row),
                  pl.BlockSpec((dh, d), fix), pl.BlockSpec((dh, d), fix),
                  pl.BlockSpec((tm, d), row),
                  pl.BlockSpec((1, d), fix), pl.BlockSpec((1, d), fix),
                  pl.BlockSpec((d, LANES), fix), pl.BlockSpec((1, LANES), fix)],
        out_specs=[pl.BlockSpec((tm, d), row), pl.BlockSpec((tm * ROW_SUB, LANES), row),
                   pl.BlockSpec((tm, LANES), row), pl.BlockSpec((1, LANES), fix)],
        out_shape=[jax.ShapeDtypeStruct((m, d), F32),
                   jax.ShapeDtypeStruct((m * ROW_SUB, LANES), jnp.uint32),
                   jax.ShapeDtypeStruct((m, LANES), F32),
                   jax.ShapeDtypeStruct((1, LANES), F32)],
        compiler_params=_cparams(("arbitrary",)),
        name="mix_ln_route",
    )(og, osm, wa, wb, x2, g_row, b_row, wr, br)


def _dispatch_body(pos_ref, src_ref, init_ref, dst_ref, sem, *, tt):
    del init_ref

    def issue(t, carry):
        srow = pl.multiple_of(t * ROW_SUB, ROW_SUB)
        for k in range(TOP_K):
            drow = pl.multiple_of(pos_ref[t * TOP_K + k] * ROW_SUB, ROW_SUB)
            pltpu.make_async_copy(src_ref.at[pl.ds(srow, ROW_SUB)], dst_ref.at[pl.ds(drow, ROW_SUB)],
                                  sem).start(priority=k % 2)
        return carry

    lax.fori_loop(0, tt, issue, 0)
    n = tt * ROW_SUB
    for _ in range(TOP_K):
        pltpu.make_async_copy(src_ref.at[pl.ds(0, n)], dst_ref.at[pl.ds(0, n)], sem).wait()


def _dispatch_rows(pos_flat, hbp, cap, tt):
    m = hbp.shape[0] // ROW_SUB
    body = functools.partial(_dispatch_body, tt=tt)
    init = jnp.zeros((cap * ROW_SUB, LANES), jnp.uint32)
    return pl.pallas_call(
        body,
        grid=(m // tt,),
        in_specs=[pl.BlockSpec((tt * TOP_K,), lambda i: (i,), memory_space=pltpu.SMEM),
                  pl.BlockSpec((tt * ROW_SUB, LANES), lambda i: (i, 0)), pl.BlockSpec(memory_space=pl.ANY)],
        out_specs=pl.BlockSpec(memory_space=pl.ANY),
        out_shape=jax.ShapeDtypeStruct((cap * ROW_SUB, LANES), jnp.uint32),
        scratch_shapes=[pltpu.SemaphoreType.DMA(())],
        input_output_aliases={2: 0},
        compiler_params=_cparams(("arbitrary",)),
        name="moe_dispatch",
    )(pos_flat, hbp, init)


MOE_COL = 256
MOE_GROUP = 2


def _moe_body(te_ref, nu_ref, x_ref, wg_ref, bg_ref, wu_ref, bu_ref, wd_ref, bd_ref,
              o_ref, xb_ref, acc_ref, *, n_f, tm, tf):
    g = pl.program_id(2)
    f = pl.program_id(1)
    used = pl.program_id(0) * MOE_GROUP + g < nu_ref[0]
    half = xb_ref.shape[2] // 2
    xb_ref = xb_ref.at[g]
    acc_ref = acc_ref.at[g]

    @pl.when(used & (f == 0))
    def _():
        lo, hi = _unpack_rows(x_ref, 0, tm)
        for r in range(ROW_SUB):
            xb_ref[:, r * LANES:(r + 1) * LANES] = lo[r].astype(BF16)
            xb_ref[:, half + r * LANES:half + (r + 1) * LANES] = hi[r].astype(BF16)

    @pl.when(used)
    def _():
        xb = xb_ref[...]
        part = None
        for c in range(tf // MOE_COL):
            cs = slice(c * MOE_COL, (c + 1) * MOE_COL)
            hg = jnp.minimum(_dot(xb, wg_ref[0, :, cs].astype(BF16)) + bg_ref[0, :, cs], SWIGLU_LIMIT)
            hu = jnp.clip(_dot(xb, wu_ref[0, :, cs].astype(BF16)) + bu_ref[0, :, cs], -SWIGLU_LIMIT, SWIGLU_LIMIT)
            hh = ((hu + 1.0) * (hg * _sigmoid(SWIGLU_ALPHA * hg))).astype(BF16)
            p = _dot(hh, wd_ref[0, cs, :].astype(BF16))
            part = p if part is None else part + p

        @pl.when(f == 0)
        def _():
            acc_ref[...] = part

        @pl.when(f != 0)
        def _():
            acc_ref[...] += part

        @pl.when(f == n_f - 1)
        def _():
            _pack_rows(acc_ref[...] + bd_ref[0], o_ref, tm)

    @pl.when(jnp.logical_not(used) & (f == n_f - 1))
    def _():
        o_ref[...] = jnp.zeros_like(o_ref)


def _moe_ffn(tile_expert, n_used, x_rows, w_gate, b_gate, w_up, b_up, w_down, b_down, tm, tf):
    n_e, d, d_ff = w_gate.shape
    n_tiles = x_rows.shape[0] // (tm * ROW_SUB)
    n_f = d_ff // tf
    grp = MOE_GROUP
    assert n_tiles % grp == 0
    body = functools.partial(_moe_body, n_f=n_f, tm=tm, tf=tf)

    def tile(p, g):
        return p * grp + g

    def fcol(p, f, nu):
        return jnp.where(p * grp < nu[0], f, n_f - 1)

    def xrow(p, f, g):
        return jnp.where(f == 0, tile(p, g), tile(p, grp - 1))

    def orow(p, f, g):
        return jnp.where(f == n_f - 1, tile(p, g), tile(p, 0))

    grid_spec = pltpu.PrefetchScalarGridSpec(
        num_scalar_prefetch=2,
        grid=(n_tiles // grp, n_f, grp),
        in_specs=[pl.BlockSpec((tm * ROW_SUB, LANES), lambda p, f, g, te, nu: (xrow(p, f, g), 0)),
                  pl.BlockSpec((1, d, tf), lambda p, f, g, te, nu: (te[tile(p, g)], 0, fcol(p, f, nu))),
                  pl.BlockSpec((1, 1, tf), lambda p, f, g, te, nu: (te[tile(p, g)], 0, fcol(p, f, nu))),
                  pl.BlockSpec((1, d, tf), lambda p, f, g, te, nu: (te[tile(p, g)], 0, fcol(p, f, nu))),
                  pl.BlockSpec((1, 1, tf), lambda p, f, g, te, nu: (te[tile(p, g)], 0, fcol(p, f, nu))),
                  pl.BlockSpec((1, tf, d), lambda p, f, g, te, nu: (te[tile(p, g)], fcol(p, f, nu), 0)),
                  pl.BlockSpec((1, 1, d), lambda p, f, g, te, nu: (te[tile(p, g)], 0, 0))],
        out_specs=pl.BlockSpec((tm * ROW_SUB, LANES), lambda p, f, g, te, nu: (orow(p, f, g), 0)),
        scratch_shapes=[pltpu.VMEM((grp, tm, d), BF16), pltpu.VMEM((grp, tm, d), F32)],
    )
    return pl.pallas_call(
        body,
        grid_spec=grid_spec,
        out_shape=jax.ShapeDtypeStruct(x_rows.shape, jnp.uint32),
        compiler_params=_cparams(("arbitrary", "arbitrary", "arbitrary")),
        name="moe_ffn",
    )(tile_expert, n_used, x_rows, w_gate, b_gate.reshape(n_e, 1, d_ff),
      w_up, b_up.reshape(n_e, 1, d_ff), w_down, b_down.reshape(n_e, 1, d))


def _final_body(pos_cur_ref, pos_nxt_ref, h_ref, route_ref, g_ref, b_ref, src_ref,
                o_ref, buf_ref, sem, *, alpha, tt, n_steps):
    i = pl.program_id(0)
    slot_rows = tt * TOP_K * ROW_SUB

    def issue(pref, slot):
        def one(t, carry):
            for k in range(TOP_K):
                srow = pl.multiple_of(pref[t * TOP_K + k] * ROW_SUB, ROW_SUB)
                drow = pl.multiple_of(slot * slot_rows + (k * tt + t) * ROW_SUB, ROW_SUB)
                pltpu.make_async_copy(src_ref.at[pl.ds(srow, ROW_SUB)], buf_ref.at[pl.ds(drow, ROW_SUB)],
                                      sem.at[slot]).start(priority=k % 2)
            return carry
        lax.fori_loop(0, tt, one, 0)

    @pl.when(i == 0)
    def _():
        issue(pos_cur_ref, 0)

    @pl.when(i + 1 < n_steps)
    def _():
        issue(pos_nxt_ref, (i + 1) % 2)

    slot = i % 2
    base = pl.multiple_of(slot * slot_rows, ROW_SUB)
    pltpu.make_async_copy(src_ref.at[pl.ds(0, slot_rows)], buf_ref.at[pl.ds(base, slot_rows)], sem.at[slot]).wait()

    ff_lo = [None] * ROW_SUB
    ff_hi = [None] * ROW_SUB
    for k in range(TOP_K):
        gate = route_ref[:, TOP_K + k:TOP_K + k + 1]
        lo, hi = _unpack_rows(buf_ref, base + k * tt * ROW_SUB, tt)
        for s in range(ROW_SUB):
            ff_lo[s] = lo[s] * gate if k == 0 else ff_lo[s] + lo[s] * gate
            ff_hi[s] = hi[s] * gate if k == 0 else ff_hi[s] + hi[s] * gate
    ff = jnp.concatenate(ff_lo + ff_hi, axis=1)
    o_ref[...] = _layernorm(alpha * h_ref[...] + ff, g_ref[...], b_ref[...])


def _combine_ln(pos_flat, h, route, g_row, b_row, outs, alpha, tt):
    m, d = h.shape
    n_steps = m // tt
    body = functools.partial(_final_body, alpha=alpha, tt=tt, n_steps=n_steps)
    row = lambda i: (i, 0)
    fix = lambda i: (0, 0)
    return pl.pallas_call(
        body,
        grid=(n_steps,),
        in_specs=[pl.BlockSpec((tt * TOP_K,), lambda i: (i,), memory_space=pltpu.SMEM),
                  pl.BlockSpec((tt * TOP_K,), lambda i: (jnp.minimum(i + 1, n_steps - 1),),
                               memory_space=pltpu.SMEM),
                  pl.BlockSpec((tt, d), row), pl.BlockSpec((tt, LANES), row),
                  pl.BlockSpec((1, d), fix), pl.BlockSpec((1, d), fix),
                  pl.BlockSpec(memory_space=pl.ANY)],
        out_specs=pl.BlockSpec((tt, d), row),
        out_shape=jax.ShapeDtypeStruct((m, d), F32),
        scratch_shapes=[pltpu.VMEM((2 * tt * TOP_K * ROW_SUB, LANES), jnp.uint32),
                        pltpu.SemaphoreType.DMA((2,))],
        compiler_params=_cparams(("arbitrary",)),
        name="combine_ln",
    )(pos_flat, pos_flat, h, route, g_row, b_row, outs)


def _row_tile(m, pref):
    t = min(m, pref)
    while m % t:
        t //= 2
    return t


def _route_tables(route, counts_row, m, tm):
    n_assign = m * TOP_K
    e_tok = route[:, :TOP_K].astype(jnp.int32)
    rank = route[:, 2 * TOP_K:3 * TOP_K].astype(jnp.int32)
    counts = counts_row[0, :N_EXPERTS].astype(jnp.int32)
    padded = (counts + tm - 1) // tm * tm
    pends = jnp.cumsum(padded)
    pstarts = pends - padded
    pos = pstarts[e_tok] + rank
    n_tiles = -(-n_assign // tm) + N_EXPERTS
    tile_start = jnp.arange(n_tiles, dtype=jnp.int32) * tm
    tile_expert = jnp.minimum(jnp.sum((pends[None, :] <= tile_start[:, None]).astype(jnp.int32), axis=1),
                              N_EXPERTS - 1)
    n_used = (pends[-1] // tm).astype(jnp.int32).reshape(1)
    last_used = jnp.maximum(n_used[0] - 1, 0)
    tile_expert = jnp.where(tile_start // tm < n_used[0], tile_expert, tile_expert[last_used])
    return pos.reshape(n_assign), tile_expert, n_used


def _layer(x, conv_hist, s_gdn, h_re, h_im, chunk, alpha, p):
    b, t, d = x.shape
    m = b * t
    d_gdn = GDN_HEADS * HEAD_DIM
    x2 = x.reshape(m, d)

    proj = _in_proj(x2, p['w_in'], _row_tile(m, 1024), p['proj_tn'])
    proj3 = proj.reshape(b, t, proj.shape[1])
    conv_new = proj3[:, t - (CONV_W - 1):, :3 * d_gdn]

    q, k, v, cols, gct = _gdn_prep(proj3, conv_hist, p['w_conv'], p['alog_row'], p['dtb_row'],
                                   chunk, p['ab_block'])
    o_gdn, s_new = _gdn(q, k, v, cols, gct, proj3, p['wn_row'], s_gdn, chunk, p['z_block'])

    n_gb = p['s5_mats'][0].shape[1]
    half = GROUPS_PER_BLOCK * SSM_P
    yg, hre_new, him_new = _s5(proj3, p['s5_mats'], h_re.reshape(b, n_gb, 1, half),
                               h_im.reshape(b, n_gb, 1, half), p['u_block0'])
    o_ssm = _glu(yg.reshape(m, yg.shape[2]), p['w_glu'], p['b_glu_row'], _row_tile(m, 512))

    h, hb, route, counts_row = _mix_ln_route(o_gdn.reshape(m, d_gdn), o_ssm, p['w_out_a'], p['w_out_b'], x2,
                                             p['ln1_g'], p['ln1_b'], p['w_router'], p['b_router'], alpha,
                                             _row_tile(m, 256))

    big = m * TOP_K >= 8 * p['moe_tm']
    tm = p['moe_tm'] if big else 128
    pos_flat, tile_expert, n_used = _route_tables(route, counts_row, m, tm)
    cap = tile_expert.shape[0] * tm
    x_rows = _dispatch_rows(pos_flat, hb, cap, _row_tile(m, 256))
    outs = _moe_ffn(tile_expert, n_used, x_rows, p['w_gate'], p['b_gate'],
                    p['w_up'], p['b_up'], p['w_down'], p['b_down'], tm, p['moe_tf'])
    y = _combine_ln(pos_flat, h, route, p['ln2_g'], p['ln2_b'], outs, alpha, _row_tile(m, 128))
    g_all = h_re.shape[1]
    return (y.reshape(b, t, d), conv_new, s_new,
            hre_new.reshape(b, g_all, SSM_P), him_new.reshape(b, g_all, SSM_P))


def _pad_lanes(v, fill=0.0):
    return jnp.pad(v.astype(F32), (0, LANES - v.shape[0]), constant_values=fill).reshape(1, LANES)


def _layer_params(l, w_in, w_conv, a_log, dt_bias, w_onorm, lam_re, lam_im, log_dt, b_re, b_im, c_re, c_im,
                  d_skip, w_glu, b_glu, w_out, ln1_g, ln1_b, w_router, b_router, w_gate, b_gate,
                  w_up, b_up, w_down, b_down, ln2_g, ln2_b):
    d_model = w_in.shape[1]
    d_gdn = GDN_HEADS * HEAD_DIM
    d_qkvz = 4 * d_gdn
    d_ssm = d_model - d_gdn
    wi = w_in[l]
    proj_tn = 896
    n_cols = d_qkvz + d_ssm + LANES
    n_pad = -(-n_cols // proj_tn) * proj_tn
    w_in_r = jnp.concatenate([wi[:, :d_qkvz], wi[:, d_qkvz + 2 * GDN_HEADS:],
                              wi[:, d_qkvz:d_qkvz + 2 * GDN_HEADS],
                              jnp.zeros((d_model, n_pad - d_qkvz - d_ssm - 2 * GDN_HEADS), wi.dtype)], axis=1)
    wo = w_out[l].astype(BF16)
    wr = jnp.pad(w_router[l].astype(F32), ((0, 0), (0, LANES - N_EXPERTS)))
    return {
        'w_in': w_in_r.astype(BF16), 'proj_tn': proj_tn,
        'z_block': 3, 'u_block0': (d_qkvz) // LANES, 'ab_block': (d_qkvz + d_ssm) // LANES,
        'w_conv': w_conv[l].astype(F32),
        'alog_row': _pad_lanes(a_log[l]), 'dtb_row': _pad_lanes(dt_bias[l]),
        'wn_row': w_onorm[l].astype(F32).reshape(1, HEAD_DIM),
        's5_mats': _s5_matrices(lam_re[l], lam_im[l], log_dt[l], b_re[l], b_im[l], c_re[l], c_im[l], d_skip[l]),
        'w_glu': w_glu[l].astype(BF16), 'b_glu_row': b_glu[l].astype(F32).reshape(1, d_ssm),
        'w_out_a': wo[:d_gdn], 'w_out_b': wo[d_gdn:],
        'ln1_g': ln1_g[l].astype(F32).reshape(1, d_model), 'ln1_b': ln1_b[l].astype(F32).reshape(1, d_model),
        'w_router': wr, 'b_router': _pad_lanes(b_router[l]),
        'w_gate': w_gate[l], 'b_gate': b_gate[l], 'w_up': w_up[l], 'b_up': b_up[l],
        'w_down': w_down[l], 'b_down': b_down[l],
        'ln2_g': ln2_g[l].astype(F32).reshape(1, d_model), 'ln2_b': ln2_b[l].astype(F32).reshape(1, d_model),
        'moe_tm': 512, 'moe_tf': 512,
    }


def kernel(x_prompt, x_sample, state_conv, state_gdn, state_ssm_re, state_ssm_im, w_in, w_conv, a_log, dt_bias, w_onorm, lam_re, lam_im, log_dt, b_re, b_im, c_re, c_im, d_skip, w_glu, b_glu, w_out, ln1_g, ln1_b, w_router, b_router, w_gate, b_gate, w_up, b_up, w_down, b_down, ln2_g, ln2_b):
    depth = w_in.shape[0]
    alpha = (2.0 * depth) ** 0.25
    bp, seq, _ = x_prompt.shape
    chunk_p = 64
    d_qkv = state_conv.shape[-1]
    n_groups, n_p = state_ssm_re.shape[-2:]
    yp, ys = x_prompt, x_sample
    outs_p = [[], [], [], []]
    outs_s = [[], [], [], []]
    for l in range(depth):
        p = _layer_params(l, w_in, w_conv, a_log, dt_bias, w_onorm, lam_re, lam_im, log_dt, b_re, b_im,
                          c_re, c_im, d_skip, w_glu, b_glu, w_out, ln1_g, ln1_b, w_router, b_router,
                          w_gate, b_gate, w_up, b_up, w_down, b_down, ln2_g, ln2_b)
        yp, cp, sp, rp, ip = _layer(
            yp, jnp.zeros((bp, CONV_W - 1, d_qkv), F32),
            jnp.zeros((bp, GDN_HEADS, HEAD_DIM, HEAD_DIM), F32),
            jnp.zeros((bp, n_groups, n_p), F32), jnp.zeros((bp, n_groups, n_p), F32),
            chunk_p, alpha, p)
        ys, cs, ss, rs, is_ = _layer(
            ys, state_conv[l].astype(F32), state_gdn[l].astype(F32),
            state_ssm_re[l].astype(F32), state_ssm_im[l].astype(F32),
            ys.shape[1], alpha, p)
        for acc, val in zip(outs_p, (cp, sp, rp, ip)):
            acc.append(val)
        for acc, val in zip(outs_s, (cs, ss, rs, is_)):
            acc.append(val)
    return (yp, ys, *[jnp.stack(a) for a in outs_p], *[jnp.stack(a) for a in outs_s])
```

```python
import functools
import math

import jax
import jax.numpy as jnp
from jax import lax
from jax.experimental import pallas as pl
from jax.experimental.pallas import tpu as pltpu

F32 = jnp.float32
BF16 = jnp.bfloat16

GDN_HEADS = 8
HEAD_DIM = 128
CONV_W = 4
SSM_CG = 16
SSM_P = 64
N_EXPERTS = 32
TOP_K = 4
SWIGLU_ALPHA = 1.702
SWIGLU_LIMIT = 7.0
LN_EPS = 1e-5
RMS_EPS = 1e-6
L2_EPS = 1e-6

LANES = 128
SUBLANES = 8
S5_SUB = 8
S5_TILE = 2048
GDN_TILE = 128
GROUPS_PER_BLOCK = LANES // SSM_CG
VMEM_LIMIT = 56 * 1024 * 1024

NT_DIMS = (((1,), (1,)), ((), ()))
TN_DIMS = (((0,), (0,)), ((), ()))


def _dot(a, b, dims=(((1,), (0,)), ((), ()))):
    return lax.dot_general(a, b, dims, preferred_element_type=F32)


def _split(a):
    hi = a.astype(BF16)
    lo = (a - hi.astype(F32)).astype(BF16)
    return hi, lo


def _dot_x3(a, b, dims=(((1,), (0,)), ((), ()))):
    ah, al = _split(a)
    bh, bl = _split(b)
    return _dot(ah, bh, dims) + (_dot(ah, bl, dims) + _dot(al, bh, dims))


def _dot_bf(a, b, dims=(((1,), (0,)), ((), ()))):
    return _dot(a.astype(BF16), b.astype(BF16), dims)


_gdn_mm = _dot_bf


def _sigmoid(x):
    return 1.0 / (1.0 + jnp.exp(-x))


ROW_SUB = SUBLANES


def _pack_rows(x, ref, tm):
    half = x.shape[1] // 2
    for s in range(ROW_SUB):
        lo = x[:, s * LANES:(s + 1) * LANES].astype(BF16).astype(F32)
        hi = x[:, half + s * LANES:half + (s + 1) * LANES].astype(BF16).astype(F32)
        word = (lax.bitcast_convert_type(lo, jnp.uint32) >> 16) | lax.bitcast_convert_type(hi, jnp.uint32)
        ref[pl.ds(s, tm, stride=ROW_SUB), :] = word


def _unpack_rows(ref, base, tm):
    lo, hi = [], []
    for s in range(ROW_SUB):
        word = ref[pl.ds(base + s, tm, stride=ROW_SUB), :]
        lo.append(lax.bitcast_convert_type(word << 16, F32))
        hi.append(lax.bitcast_convert_type(word & jnp.uint32(0xFFFF0000), F32))
    return lo, hi


def _cparams(sem):
    return pltpu.CompilerParams(dimension_semantics=sem, vmem_limit_bytes=VMEM_LIMIT)


def _proj_body(x_ref, w_ref, o_ref, xb_ref):
    @pl.when(pl.program_id(1) == 0)
    def _():
        xb_ref[...] = x_ref[...].astype(BF16)

    o_ref[...] = _dot(xb_ref[...], w_ref[...])


def _in_proj(x2, w_bf, tm, tn):
    m, k = x2.shape
    n = w_bf.shape[1]
    return pl.pallas_call(
        _proj_body,
        grid=(m // tm, n // tn),
        in_specs=[pl.BlockSpec((tm, k), lambda i, j: (i, 0)),
                  pl.BlockSpec((k, tn), lambda i, j: (0, j))],
        out_specs=pl.BlockSpec((tm, tn), lambda i, j: (i, j)),
        out_shape=jax.ShapeDtypeStruct((m, n), F32),
        scratch_shapes=[pltpu.VMEM((tm, k), BF16)],
        compiler_params=_cparams(("parallel", "arbitrary")),
        name="in_proj",
    )(x2, w_bf)


def _gdn_prep_body(qkv_ref, ab_ref, hist_ref, wc_ref, alog_ref, dtb_ref,
                   q_ref, k_ref, v_ref, cols_ref, gct_ref, xbuf_ref, *, tt, chunk):
    d_gdn = GDN_HEADS * HEAD_DIM
    halo = SUBLANES

    @pl.when(pl.program_id(1) == 0)
    def _():
        xbuf_ref[0:halo, :] = jnp.zeros((halo, 3 * d_gdn), F32)
        xbuf_ref[halo - (CONV_W - 1):halo, :] = hist_ref[0]

    xbuf_ref[halo:halo + tt, :] = qkv_ref[0]

    for part, out_ref in enumerate((q_ref, k_ref, v_ref)):
        c0 = part * d_gdn
        y = None
        for j in range(CONV_W):
            r0 = halo - (CONV_W - 1) + j
            term = xbuf_ref[r0:r0 + tt, c0:c0 + d_gdn] * wc_ref[j:j + 1, c0:c0 + d_gdn]
            y = term if y is None else y + term
        s = y * _sigmoid(y)
        if part == 2:
            out_ref[0] = s
        else:
            for h in range(GDN_HEADS):
                sh = s[:, h * HEAD_DIM:(h + 1) * HEAD_DIM]
                nrm = sh * lax.rsqrt(jnp.sum(sh * sh, axis=-1, keepdims=True) + L2_EPS)
                if part == 0:
                    nrm = nrm * (HEAD_DIM ** -0.5)
                out_ref[0, :, h * HEAD_DIM:(h + 1) * HEAD_DIM] = nrm

    xbuf_ref[0:halo, :] = xbuf_ref[tt:tt + halo, :]

    ab = ab_ref[0]
    lane = lax.broadcasted_iota(jnp.int32, (tt, LANES), 1)
    is_a = lane < GDN_HEADS
    z = ab + dtb_ref[...]
    softplus = jnp.maximum(z, 0.0) + jnp.log1p(jnp.exp(-jnp.abs(z)))
    g = jnp.where(is_a, -jnp.exp(alog_ref[...]) * softplus, 0.0)
    beta = _sigmoid(ab)

    shift = int(math.log2(chunk))
    r = lax.broadcasted_iota(jnp.int32, (tt, tt), 0)
    c = lax.broadcasted_iota(jnp.int32, (tt, tt), 1)
    same = (r >> shift) == (c >> shift)
    m_incl = jnp.where(same & (r >= c), 1.0, 0.0).astype(BF16)
    m_all = jnp.where(same, 1.0, 0.0).astype(BF16)
    gc = _dot_exact_lhs_rhs(m_incl, g)
    glast = _dot_exact_lhs_rhs(m_all, g)
    eg = jnp.exp(gc)
    egl = jnp.exp(glast - gc)
    egt = jnp.exp(glast)
    zero = jnp.zeros_like(gc)
    cols = (jnp.where(is_a, gc, zero)
            + jnp.where((lane >= 8) & (lane < 16), beta, zero)
            + pltpu.roll(jnp.where(is_a, eg, zero), 16, 1)
            + pltpu.roll(jnp.where(is_a, egl, zero), 24, 1)
            + pltpu.roll(jnp.where(is_a, egt, zero), 32, 1))
    cols_ref[0] = cols

    er = lax.broadcasted_iota(jnp.int32, (SUBLANES, LANES), 0)
    ec = lax.broadcasted_iota(jnp.int32, (SUBLANES, LANES), 1)
    sel = jnp.where(er == ec, 1.0, 0.0).astype(BF16)
    gct_ref[0] = _dot_exact_rhs(sel, jnp.where(is_a, gc, zero), NT_DIMS)


def _three_pieces(a):
    a0 = a.astype(BF16)
    r1 = a - a0.astype(F32)
    a1 = r1.astype(BF16)
    a2 = (r1 - a1.astype(F32)).astype(BF16)
    return a0, a1, a2


def _dot_exact_lhs_rhs(mask_bf, a):
    a0, a1, a2 = _three_pieces(a)
    return _dot(mask_bf, a0) + (_dot(mask_bf, a1) + _dot(mask_bf, a2))


def _dot_exact_rhs(mask_bf, a, dims):
    a0, a1, a2 = _three_pieces(a)
    return _dot(mask_bf, a0, dims) + (_dot(mask_bf, a1, dims) + _dot(mask_bf, a2, dims))


def _gdn_prep(proj3, hist, w_conv, alog_p, dtb_p, chunk, ab_block):
    b, t, _ = proj3.shape
    d_gdn = GDN_HEADS * HEAD_DIM
    tt = min(t, GDN_TILE)
    body = functools.partial(_gdn_prep_body, tt=tt, chunk=chunk)
    big = jax.ShapeDtypeStruct((b, t, d_gdn), F32)
    return pl.pallas_call(
        body,
        grid=(b, t // tt),
        in_specs=[pl.BlockSpec((1, tt, 3 * d_gdn), lambda i, j: (i, j, 0)),
                  pl.BlockSpec((1, tt, LANES), lambda i, j: (i, j, ab_block)),
                  pl.BlockSpec((1, CONV_W - 1, 3 * d_gdn), lambda i, j: (i, 0, 0)),
                  pl.BlockSpec((CONV_W, 3 * d_gdn), lambda i, j: (0, 0)),
                  pl.BlockSpec((1, LANES), lambda i, j: (0, 0)),
                  pl.BlockSpec((1, LANES), lambda i, j: (0, 0))],
        out_specs=[pl.BlockSpec((1, tt, d_gdn), lambda i, j: (i, j, 0)),
                   pl.BlockSpec((1, tt, d_gdn), lambda i, j: (i, j, 0)),
                   pl.BlockSpec((1, tt, d_gdn), lambda i, j: (i, j, 0)),
                   pl.BlockSpec((1, tt, LANES), lambda i, j: (i, j, 0)),
                   pl.BlockSpec((1, SUBLANES, tt), lambda i, j: (i, 0, j))],
        out_shape=[big, big, big,
                   jax.ShapeDtypeStruct((b, t, LANES), F32),
                   jax.ShapeDtypeStruct((b, SUBLANES, t), F32)],
        scratch_shapes=[pltpu.VMEM((tt + SUBLANES, 3 * d_gdn), F32)],
        compiler_params=_cparams(("parallel", "arbitrary")),
        name="gdn_prep",
    )(proj3, proj3, hist, w_conv, alog_p, dtb_p)


def _unit_lower_inverses(mats, r, c, chunk):
    base = 16
    eye = jnp.where(r == c, 1.0, 0.0)

    def blk(bs):
        s = int(math.log2(bs))
        return (r >> s) == (c >> s)

    d1 = [jnp.where(blk(base), a, 0.0) for a in mats]
    d2 = [_gdn_mm(x, x) for x in d1]
    d4 = [_gdn_mm(x, x) for x in d2]
    d8 = [_gdn_mm(x, x) for x in d4]
    t = [eye - x for x in d1]
    t = [x + _gdn_mm(x, y) for x, y in zip(t, d2)]
    t = [x + _gdn_mm(x, y) for x, y in zip(t, d4)]
    t = [x + _gdn_mm(x, y) for x, y in zip(t, d8)]
    bs = base
    while bs < chunk:
        off_mask = blk(2 * bs) & jnp.logical_not(blk(bs))
        inner = [_gdn_mm(jnp.where(off_mask, a, 0.0), x) for a, x in zip(mats, t)]
        t = [x - _gdn_mm(x, y) for x, y in zip(t, inner)]
        bs *= 2
    return t


def _gdn_body(q_ref, k_ref, v_ref, cols_ref, gct_ref, z_ref, wn_ref, s0_ref,
              o_ref, s_ref, *, tb, chunk):
    @pl.when(pl.program_id(1) == 0)
    def _():
        s_ref[...] = s0_ref[...]

    shift = int(math.log2(chunk))
    r = lax.broadcasted_iota(jnp.int32, (tb, tb), 0)
    c = lax.broadcasted_iota(jnp.int32, (tb, tb), 1)
    same = (r >> shift) == (c >> shift)
    incl = same & (r >= c)
    strict = same & (r > c)
    n_chunks = tb // chunk

    heads = range(GDN_HEADS)
    hsl = [slice(h * HEAD_DIM, (h + 1) * HEAD_DIM) for h in heads]
    q = [q_ref[0, :, s] for s in hsl]
    k = [k_ref[0, :, s] for s in hsl]
    beta = [cols_ref[0, :, 8 + h:9 + h] for h in heads]
    eg = [cols_ref[0, :, 16 + h:17 + h] for h in heads]
    egl = [cols_ref[0, :, 24 + h:25 + h] for h in heads]
    decay = [jnp.exp(jnp.where(incl, cols_ref[0, :, h:h + 1] - gct_ref[0, h:h + 1, :], -jnp.inf)) for h in heads]
    kb = [k[h] * beta[h] for h in heads]
    k_bf = [x.astype(BF16) for x in k]
    a = [jnp.where(strict, _dot(kb[h].astype(BF16), k_bf[h], NT_DIMS) * decay[h], 0.0) for h in heads]
    tinv = _unit_lower_inverses(a, r, c, chunk)
    sol = [_gdn_mm(tinv[h], jnp.concatenate([v_ref[0, :, hsl[h]] * beta[h], kb[h] * eg[h]], axis=1)) for h in heads]
    u = [x[:, :HEAD_DIM] for x in sol]
    w_bf = [x[:, HEAD_DIM:].astype(BF16) for x in sol]
    attn = [(_dot(q[h].astype(BF16), k_bf[h], NT_DIMS) * decay[h]).astype(BF16) for h in heads]
    q_dec = [(q[h] * eg[h]).astype(BF16) for h in heads]
    k_dec = [(k[h] * egl[h]).astype(BF16) for h in heads]

    s = [s_ref[0, h] for h in heads]
    v_new = [[] for _ in heads]
    o_state = [[] for _ in heads]
    for ci in range(n_chunks):
        rs = slice(ci * chunk, (ci + 1) * chunk)
        s_bf = [x.astype(BF16) for x in s]
        vn = [u[h][rs] - _dot(w_bf[h][rs], s_bf[h]) for h in heads]
        for h in heads:
            o_state[h].append(_dot(q_dec[h][rs], s_bf[h]))
            v_new[h].append(vn[h])
        s = [s[h] * cols_ref[0, ci * chunk:ci * chunk + 1, 32 + h:33 + h]
             + _dot(k_dec[h][rs], vn[h].astype(BF16), TN_DIMS) for h in heads]
    for h in heads:
        s_ref[0, h] = s[h]

    def cat(parts):
        return parts[0] if len(parts) == 1 else jnp.concatenate(parts, axis=0)

    for h in heads:
        o = cat(o_state[h]) + _dot(attn[h], cat(v_new[h]).astype(BF16))
        zh = z_ref[0, :, hsl[h]]
        o = (o * lax.rsqrt(jnp.mean(o * o, axis=-1, keepdims=True) + RMS_EPS) * wn_ref[...]
             * (zh * _sigmoid(zh)))
        o_ref[0, :, hsl[h]] = o.astype(BF16)


def _gdn(q, k, v, cols, gct, proj3, wn, s0, chunk, z_block):
    b, t, d_gdn = q.shape
    tb = min(t, GDN_TILE)
    body = functools.partial(_gdn_body, tb=tb, chunk=chunk)
    tile = lambda i, j: (i, j, 0)
    return pl.pallas_call(
        body,
        grid=(b, t // tb),
        in_specs=[pl.BlockSpec((1, tb, d_gdn), tile),
                  pl.BlockSpec((1, tb, d_gdn), tile),
                  pl.BlockSpec((1, tb, d_gdn), tile),
                  pl.BlockSpec((1, tb, LANES), tile),
                  pl.BlockSpec((1, SUBLANES, tb), lambda i, j: (i, 0, j)),
                  pl.BlockSpec((1, tb, d_gdn), lambda i, j: (i, j, z_block)),
                  pl.BlockSpec((1, HEAD_DIM), lambda i, j: (0, 0)),
                  pl.BlockSpec((1, GDN_HEADS, HEAD_DIM, HEAD_DIM), lambda i, j: (i, 0, 0, 0))],
        out_specs=[pl.BlockSpec((1, tb, d_gdn), tile),
                   pl.BlockSpec((1, GDN_HEADS, HEAD_DIM, HEAD_DIM), lambda i, j: (i, 0, 0, 0))],
        out_shape=[jax.ShapeDtypeStruct((b, t, d_gdn), BF16),
                   jax.ShapeDtypeStruct((b, GDN_HEADS, HEAD_DIM, HEAD_DIM), F32)],
        compiler_params=_cparams(("parallel", "arbitrary")),
        name="gdn_delta",
    )(q, k, v, cols, gct, proj3, wn, s0)


def _gelu_tanh(x):
    return 0.5 * x * (1.0 + jnp.tanh(math.sqrt(2.0 / math.pi) * (x + 0.044715 * (x * x * x))))


def _s5_body(u_ref, km_ref, wm_ref, vm_ref, lre_ref, lim_ref, dsk_ref, h0re_ref, h0im_ref,
             y_ref, hre_ref, him_ref, xbuf_ref, hbuf_ref, *, n_sub, seg):
    half = GROUPS_PER_BLOCK * SSM_P

    if seg is None:
        @pl.when(pl.program_id(2) == 0)
        def _():
            hre_ref[...] = h0re_ref[...]
            him_ref[...] = h0im_ref[...]

    u_f = [u_ref[0, pl.ds(j, n_sub, stride=S5_SUB), :] for j in range(S5_SUB)]
    u_b = [x.astype(BF16) for x in u_f]

    x = _dot(u_b[0], wm_ref[0, 0])
    for j in range(1, S5_SUB):
        x = x + _dot(u_b[j], wm_ref[j, 0])
    xbuf_ref[...] = x

    lre = lre_ref[0]
    lim = lim_ref[0]

    def advance(n, hre, him):
        hbuf_ref[pl.ds(n, 1), 0:half] = hre
        hbuf_ref[pl.ds(n, 1), half:2 * half] = him
        xr = xbuf_ref[pl.ds(n, 1), 0:half]
        xi = xbuf_ref[pl.ds(n, 1), half:2 * half]
        return (lre * hre - lim * him + xr, lre * him + lim * hre + xi)

    if seg is None:
        hre, him = lax.fori_loop(0, n_sub, lambda n, c: advance(n, *c), (hre_ref[0, 0], him_ref[0, 0]))
        hre_ref[0, 0] = hre
        him_ref[0, 0] = him
    else:
        def step(n, carry):
            q = n // seg
            first = n - q * seg == 0
            hre = jnp.where(first, h0re_ref[q, 0], carry[0])
            him = jnp.where(first, h0im_ref[q, 0], carry[1])
            hre, him = advance(n, hre, him)
            hre_ref[q, 0] = hre
            him_ref[q, 0] = him
            return hre, him

        lax.fori_loop(0, n_sub, step, (h0re_ref[0, 0], h0im_ref[0, 0]))

    h_b = hbuf_ref[...].astype(BF16)
    dsk = dsk_ref[0]
    for l in range(S5_SUB):
        y = _dot(h_b, vm_ref[l, 0])
        for d in range(l + 1):
            y = y + _dot(u_b[l - d], km_ref[d, 0])
        y = y + dsk * u_f[l]
        y_ref[0, pl.ds(l, n_sub, stride=S5_SUB), :] = _gelu_tanh(y)


def _s5(proj3, mats, h0re, h0im, u_block0):
    km, wm, vm, lre, lim, dsk = mats
    b, t, width = proj3.shape
    n_gb = km.shape[1]
    half = GROUPS_PER_BLOCK * SSM_P
    if b > 1 and b * t <= S5_TILE:
        y, hre, him = _s5_call(proj3.reshape(1, b * t, width), mats, h0re, h0im, u_block0,
                               tt=b * t, seg=t // S5_SUB, state_rows=b)
        return y.reshape(b, t, n_gb * LANES), hre, him
    return _s5_call(proj3, mats, h0re, h0im, u_block0, tt=min(t, S5_TILE), seg=None, state_rows=1)


def _s5_call(proj3, mats, h0re, h0im, u_block0, tt, seg, state_rows):
    km, wm, vm, lre, lim, dsk = mats
    b, t, _ = proj3.shape
    n_gb = km.shape[1]
    n_sub = tt // S5_SUB
    half = GROUPS_PER_BLOCK * SSM_P
    body = functools.partial(_s5_body, n_sub=n_sub, seg=seg)
    state_spec = pl.BlockSpec((state_rows, 1, 1, half), lambda g, i, j: (i, g, 0, 0))
    par_spec = pl.BlockSpec((1, 1, half), lambda g, i, j: (g, 0, 0))
    return pl.pallas_call(
        body,
        grid=(n_gb, b, t // tt),
        in_specs=[pl.BlockSpec((1, tt, LANES), lambda g, i, j: (i, j, u_block0 + g)),
                  pl.BlockSpec((S5_SUB, 1, LANES, LANES), lambda g, i, j: (0, g, 0, 0)),
                  pl.BlockSpec((S5_SUB, 1, LANES, 2 * half), lambda g, i, j: (0, g, 0, 0)),
                  pl.BlockSpec((S5_SUB, 1, 2 * half, LANES), lambda g, i, j: (0, g, 0, 0)),
                  par_spec, par_spec,
                  pl.BlockSpec((1, 1, LANES), lambda g, i, j: (g, 0, 0)),
                  state_spec, state_spec],
        out_specs=[pl.BlockSpec((1, tt, LANES), lambda g, i, j: (i, j, g)),
                   state_spec, state_spec],
        out_shape=[jax.ShapeDtypeStruct((b, t, n_gb * LANES), F32),
                   jax.ShapeDtypeStruct(h0re.shape, F32),
                   jax.ShapeDtypeStruct(h0im.shape, F32)],
        scratch_shapes=[pltpu.VMEM((n_sub, 2 * half), F32),
                        pltpu.VMEM((n_sub, 2 * half), F32)],
        compiler_params=_cparams(("parallel", "parallel", "arbitrary")),
        name="s5_scan",
    )(proj3, km, wm, vm, lre, lim, dsk, h0re, h0im)


def _s5_matrices(lam_re, lam_im, log_dt, b_re, b_im, c_re, c_im, d_skip):
    g, p = lam_re.shape
    n_gb = g // GROUPS_PER_BLOCK
    gpb = GROUPS_PER_BLOCK
    dt = jnp.exp(log_dt.astype(F32))
    lam = lax.complex(jnp.minimum(lam_re.astype(F32), -1e-4), lam_im.astype(F32))
    lam_bar = jnp.exp(lam * dt[:, None])
    b_bar = ((lam_bar - 1.0) / lam)[..., None] * lax.complex(b_re.astype(F32), b_im.astype(F32))
    c_c = lax.complex(c_re.astype(F32), c_im.astype(F32))
    pows = [jnp.ones_like(lam_bar)]
    for _ in range(S5_SUB):
        pows.append(pows[-1] * lam_bar)
    pw = jnp.stack(pows)
    eye = jnp.eye(gpb, dtype=F32)

    kd = jnp.real(jnp.einsum('gop,dgp,gpi->dgio', c_c, pw[:S5_SUB], b_bar))
    km = jnp.einsum('dbgio,gh->dbgiho', kd.reshape(S5_SUB, n_gb, gpb, SSM_CG, SSM_CG), eye)
    km = km.reshape(S5_SUB, n_gb, LANES, LANES)

    wj = pw[:S5_SUB][::-1][:, :, :, None] * b_bar[None]
    wj = jnp.transpose(wj, (0, 1, 3, 2)).reshape(S5_SUB, n_gb, gpb, SSM_CG, p)
    wre = jnp.einsum('dbgcp,gh->dbgchp', jnp.real(wj), eye).reshape(S5_SUB, n_gb, LANES, gpb * p)
    wim = jnp.einsum('dbgcp,gh->dbgchp', jnp.imag(wj), eye).reshape(S5_SUB, n_gb, LANES, gpb * p)
    wm = jnp.concatenate([wre, wim], axis=-1)

    cl = c_c[None] * pw[1:S5_SUB + 1][:, :, None, :]
    cl = jnp.transpose(cl, (0, 1, 3, 2)).reshape(S5_SUB, n_gb, gpb, p, SSM_CG)
    vre = jnp.einsum('dbgpc,gh->dbgphc', jnp.real(cl), eye).reshape(S5_SUB, n_gb, gpb * p, LANES)
    vim = jnp.einsum('dbgpc,gh->dbgphc', -jnp.imag(cl), eye).reshape(S5_SUB, n_gb, gpb * p, LANES)
    vm = jnp.concatenate([vre, vim], axis=-2)

    lam_s = pw[S5_SUB].reshape(n_gb, 1, gpb * p)
    dsk = d_skip.astype(F32).reshape(n_gb, 1, LANES)
    return (km.astype(BF16), wm.astype(BF16), vm.astype(BF16),
            jnp.real(lam_s), jnp.imag(lam_s), dsk)


def _glu_body(y_ref, w_ref, b_ref, o_ref):
    y = y_ref[...]
    gate = _dot(y.astype(BF16), w_ref[...]) + b_ref[...]
    o_ref[...] = (y * _sigmoid(gate)).astype(BF16)


def _glu(y2, w_bf, b_row, tm):
    m, d = y2.shape
    return pl.pallas_call(
        _glu_body,
        grid=(m // tm,),
        in_specs=[pl.BlockSpec((tm, d), lambda i: (i, 0)),
                  pl.BlockSpec((d, d), lambda i: (0, 0)),
                  pl.BlockSpec((1, d), lambda i: (0, 0))],
        out_specs=pl.BlockSpec((tm, d), lambda i: (i, 0)),
        out_shape=jax.ShapeDtypeStruct((m, d), BF16),
        compiler_params=_cparams(("parallel",)),
        name="s5_glu",
    )(y2, w_bf, b_row)


def _layernorm(v, g, b):
    mu = jnp.mean(v, axis=-1, keepdims=True)
    var = jnp.mean(jnp.square(v - mu), axis=-1, keepdims=True)
    return (v - mu) * lax.rsqrt(var + LN_EPS) * g + b


def _mix_body(og_ref, os_ref, wa_ref, wb_ref, x_ref, g_ref, b_ref, wr_ref, br_ref,
              h_ref, hb_ref, route_ref, cnt_ref, *, alpha):
    @pl.when(pl.program_id(0) == 0)
    def _():
        cnt_ref[...] = jnp.zeros_like(cnt_ref)

    mix = _dot(og_ref[...], wa_ref[...]) + _dot(os_ref[...], wb_ref[...])
    h = _layernorm(alpha * x_ref[...] + mix, g_ref[...], b_ref[...])
    h_ref[...] = h
    tm = h.shape[0]
    _pack_rows(h, hb_ref, tm)

    logits = _dot_x3(h, wr_ref[...]) + br_ref[...]
    lane = lax.broadcasted_iota(jnp.int32, (tm, LANES), 1)
    work = jnp.where(lane < N_EXPERTS, logits, -jnp.inf)
    vals, idxs = [], []
    for _ in range(TOP_K):
        mx = jnp.max(work, axis=-1, keepdims=True)
        ix = jnp.min(jnp.where(work == mx, lane, LANES), axis=-1, keepdims=True)
        vals.append(mx)
        idxs.append(ix)
        work = jnp.where(lane == ix, -jnp.inf, work)
    exps = [jnp.exp(v - vals[0]) for v in vals]
    denom = exps[0]
    for e in exps[1:]:
        denom = denom + e
    chosen = jnp.zeros((tm, LANES), F32)
    for k in range(TOP_K):
        chosen = jnp.where(lane == idxs[k], 1.0, chosen)
    rr = lax.broadcasted_iota(jnp.int32, (tm, tm), 0)
    cc = lax.broadcasted_iota(jnp.int32, (tm, tm), 1)
    lower = jnp.where(rr > cc, 1.0, 0.0).astype(BF16)
    before = cnt_ref[...] + _dot(lower, chosen.astype(BF16))
    cnt_ref[...] = cnt_ref[...] + jnp.sum(chosen, axis=0, keepdims=True)

    route = jnp.zeros((tm, LANES), F32)
    for k in range(TOP_K):
        rank = jnp.sum(jnp.where(lane == idxs[k], before, 0.0), axis=-1, keepdims=True)
        route = jnp.where(lane == k, idxs[k].astype(F32), route)
        route = jnp.where(lane == TOP_K + k, exps[k] / denom, route)
        route = jnp.where(lane == 2 * TOP_K + k, rank, route)
    route_ref[...] = route


def _mix_ln_route(og, osm, wa, wb, x2, g_row, b_row, wr, br, alpha, tm):
    m, d = x2.shape
    dh = og.shape[1]
    body = functools.partial(_mix_body, alpha=alpha)
    row = lambda i: (i, 0)
    fix = lambda i: (0, 0)
    return pl.pallas_call(
        body,
        grid=(m // tm,),
        in_specs=[pl.BlockSpec((tm, dh), row), pl.BlockSpec((tm, dh), row),
                  pl.BlockSpec((dh, d), fix), pl.BlockSpec((dh, d), fix),
                  pl.BlockSpec((tm, d), row),
                  pl.BlockSpec((1, d), fix), pl.BlockSpec((1, d), fix),
                  pl.BlockSpec((d, LANES), fix), pl.BlockSpec((1, LANES), fix)],
        out_specs=[pl.BlockSpec((tm, d), row), pl.BlockSpec((tm * ROW_SUB, LANES), row),
                   pl.BlockSpec((tm, LANES), row), pl.BlockSpec((1, LANES), fix)],
        out_shape=[jax.ShapeDtypeStruct((m, d), F32),
                   jax.ShapeDtypeStruct((m * ROW_SUB, LANES), jnp.uint32),
                   jax.ShapeDtypeStruct((m, LANES), F32),
                   jax.ShapeDtypeStruct((1, LANES), F32)],
        compiler_params=_cparams(("arbitrary",)),
        name="mix_ln_route",
    )(og, osm, wa, wb, x2, g_row, b_row, wr, br)


def _dispatch_body(pos_ref, src_ref, init_ref, dst_ref, sem, *, tt):
    del init_ref

    def issue(t, carry):
        srow = pl.multiple_of(t * ROW_SUB, ROW_SUB)
        for k in range(TOP_K):
            drow = pl.multiple_of(pos_ref[t * TOP_K + k] * ROW_SUB, ROW_SUB)
            pltpu.make_async_copy(src_ref.at[pl.ds(srow, ROW_SUB)], dst_ref.at[pl.ds(drow, ROW_SUB)],
                                  sem).start(priority=k % 2)
        return carry

    lax.fori_loop(0, tt, issue, 0, unroll=4)
    n = tt * ROW_SUB
    for _ in range(TOP_K):
        pltpu.make_async_copy(src_ref.at[pl.ds(0, n)], dst_ref.at[pl.ds(0, n)], sem).wait()


def _dispatch_rows(pos_flat, hbp, cap, tt):
    m = hbp.shape[0] // ROW_SUB
    body = functools.partial(_dispatch_body, tt=tt)
    init = jnp.zeros((cap * ROW_SUB, LANES), jnp.uint32)
    return pl.pallas_call(
        body,
        grid=(m // tt,),
        in_specs=[pl.BlockSpec((tt * TOP_K,), lambda i: (i,), memory_space=pltpu.SMEM),
                  pl.BlockSpec((tt * ROW_SUB, LANES), lambda i: (i, 0)), pl.BlockSpec(memory_space=pl.ANY)],
        out_specs=pl.BlockSpec(memory_space=pl.ANY),
        out_shape=jax.ShapeDtypeStruct((cap * ROW_SUB, LANES), jnp.uint32),
        scratch_shapes=[pltpu.SemaphoreType.DMA(())],
        input_output_aliases={2: 0},
        compiler_params=_cparams(("arbitrary",)),
        name="moe_dispatch",
    )(pos_flat, hbp, init)


MOE_COL = 256
MOE_GROUP = 2


def _moe_body(te_ref, nu_ref, x_ref, wg_ref, bg_ref, wu_ref, bu_ref, wd_ref, bd_ref,
              o_ref, xb_ref, acc_ref, *, n_f, tm, tf):
    g = pl.program_id(2)
    f = pl.program_id(1)
    used = pl.program_id(0) * MOE_GROUP + g < nu_ref[0]
    half = xb_ref.shape[2] // 2
    xb_ref = xb_ref.at[g]
    acc_ref = acc_ref.at[g]

    @pl.when(used & (f == 0))
    def _():
        lo, hi = _unpack_rows(x_ref, 0, tm)
        for r in range(ROW_SUB):
            xb_ref[:, r * LANES:(r + 1) * LANES] = lo[r].astype(BF16)
            xb_ref[:, half + r * LANES:half + (r + 1) * LANES] = hi[r].astype(BF16)

    @pl.when(used)
    def _():
        xb = xb_ref[...]
        part = None
        for c in range(tf // MOE_COL):
            cs = slice(c * MOE_COL, (c + 1) * MOE_COL)
            hg = jnp.minimum(_dot(xb, wg_ref[0, :, cs].astype(BF16)) + bg_ref[0, :, cs], SWIGLU_LIMIT)
            hu = jnp.clip(_dot(xb, wu_ref[0, :, cs].astype(BF16)) + bu_ref[0, :, cs], -SWIGLU_LIMIT, SWIGLU_LIMIT)
            hh = ((hu + 1.0) * (hg * _sigmoid(SWIGLU_ALPHA * hg))).astype(BF16)
            p = _dot(hh, wd_ref[0, cs, :].astype(BF16))
            part = p if part is None else part + p

        @pl.when(f == 0)
        def _():
            acc_ref[...] = part

        @pl.when(f != 0)
        def _():
            acc_ref[...] += part

        @pl.when(f == n_f - 1)
        def _():
            _pack_rows(acc_ref[...] + bd_ref[0], o_ref, tm)

    @pl.when(jnp.logical_not(used) & (f == n_f - 1))
    def _():
        o_ref[...] = jnp.zeros_like(o_ref)


def _moe_ffn(tile_expert, n_used, x_rows, w_gate, b_gate, w_up, b_up, w_down, b_down, tm, tf):
    n_e, d, d_ff = w_gate.shape
    n_tiles = x_rows.shape[0] // (tm * ROW_SUB)
    n_f = d_ff // tf
    grp = MOE_GROUP
    assert n_tiles % grp == 0
    body = functools.partial(_moe_body, n_f=n_f, tm=tm, tf=tf)

    def tile(p, g):
        return p * grp + g

    def fcol(p, f, nu):
        return jnp.where(p * grp < nu[0], f, n_f - 1)

    def xrow(p, f, g):
        return jnp.where(f == 0, tile(p, g), tile(p, grp - 1))

    def orow(p, f, g):
        return jnp.where(f == n_f - 1, tile(p, g), tile(p, 0))

    grid_spec = pltpu.PrefetchScalarGridSpec(
        num_scalar_prefetch=2,
        grid=(n_tiles // grp, n_f, grp),
        in_specs=[pl.BlockSpec((tm * ROW_SUB, LANES), lambda p, f, g, te, nu: (xrow(p, f, g), 0)),
                  pl.BlockSpec((1, d, tf), lambda p, f, g, te, nu: (te[tile(p, g)], 0, fcol(p, f, nu))),
                  pl.BlockSpec((1, 1, tf), lambda p, f, g, te, nu: (te[tile(p, g)], 0, fcol(p, f, nu))),
                  pl.BlockSpec((1, d, tf), lambda p, f, g, te, nu: (te[tile(p, g)], 0, fcol(p, f, nu))),
                  pl.BlockSpec((1, 1, tf), lambda p, f, g, te, nu: (te[tile(p, g)], 0, fcol(p, f, nu))),
                  pl.BlockSpec((1, tf, d), lambda p, f, g, te, nu: (te[tile(p, g)], fcol(p, f, nu), 0)),
                  pl.BlockSpec((1, 1, d), lambda p, f, g, te, nu: (te[tile(p, g)], 0, 0))],
        out_specs=pl.BlockSpec((tm * ROW_SUB, LANES), lambda p, f, g, te, nu: (orow(p, f, g), 0)),
        scratch_shapes=[pltpu.VMEM((grp, tm, d), BF16), pltpu.VMEM((grp, tm, d), F32)],
    )
    return pl.pallas_call(
        body,
        grid_spec=grid_spec,
        out_shape=jax.ShapeDtypeStruct(x_rows.shape, jnp.uint32),
        compiler_params=_cparams(("arbitrary", "arbitrary", "arbitrary")),
        name="moe_ffn",
    )(tile_expert, n_used, x_rows, w_gate, b_gate.reshape(n_e, 1, d_ff),
      w_up, b_up.reshape(n_e, 1, d_ff), w_down, b_down.reshape(n_e, 1, d))


def _final_body(pos_cur_ref, pos_nxt_ref, h_ref, route_ref, g_ref, b_ref, src_ref,
                o_ref, buf_ref, sem, *, alpha, tt, n_steps):
    i = pl.program_id(0)
    slot_rows = tt * TOP_K * ROW_SUB

    def issue(pref, slot):
        def one(t, carry):
            for k in range(TOP_K):
                srow = pl.multiple_of(pref[t * TOP_K + k] * ROW_SUB, ROW_SUB)
                drow = pl.multiple_of(slot * slot_rows + (k * tt + t) * ROW_SUB, ROW_SUB)
                pltpu.make_async_copy(src_ref.at[pl.ds(srow, ROW_SUB)], buf_ref.at[pl.ds(drow, ROW_SUB)],
                                      sem.at[slot]).start(priority=k % 2)
            return carry
        lax.fori_loop(0, tt, one, 0, unroll=4)

    @pl.when(i == 0)
    def _():
        issue(pos_cur_ref, 0)

    @pl.when(i + 1 < n_steps)
    def _():
        issue(pos_nxt_ref, (i + 1) % 2)

    slot = i % 2
    base = pl.multiple_of(slot * slot_rows, ROW_SUB)
    pltpu.make_async_copy(src_ref.at[pl.ds(0, slot_rows)], buf_ref.at[pl.ds(base, slot_rows)], sem.at[slot]).wait()

    ff_lo = [None] * ROW_SUB
    ff_hi = [None] * ROW_SUB
    for k in range(TOP_K):
        gate = route_ref[:, TOP_K + k:TOP_K + k + 1]
        lo, hi = _unpack_rows(buf_ref, base + k * tt * ROW_SUB, tt)
        for s in range(ROW_SUB):
            ff_lo[s] = lo[s] * gate if k == 0 else ff_lo[s] + lo[s] * gate
            ff_hi[s] = hi[s] * gate if k == 0 else ff_hi[s] + hi[s] * gate
    ff = jnp.concatenate(ff_lo + ff_hi, axis=1)
    o_ref[...] = _layernorm(alpha * h_ref[...] + ff, g_ref[...], b_ref[...])


def _combine_ln(pos_flat, h, route, g_row, b_row, outs, alpha, tt):
    m, d = h.shape
    n_steps = m // tt
    body = functools.partial(_final_body, alpha=alpha, tt=tt, n_steps=n_steps)
    row = lambda i: (i, 0)
    fix = lambda i: (0, 0)
    return pl.pallas_call(
        body,
        grid=(n_steps,),
        in_specs=[pl.BlockSpec((tt * TOP_K,), lambda i: (i,), memory_space=pltpu.SMEM),
                  pl.BlockSpec((tt * TOP_K,), lambda i: (jnp.minimum(i + 1, n_steps - 1),),
                               memory_space=pltpu.SMEM),
                  pl.BlockSpec((tt, d), row), pl.BlockSpec((tt, LANES), row),
                  pl.BlockSpec((1, d), fix), pl.BlockSpec((1, d), fix),
                  pl.BlockSpec(memory_space=pl.ANY)],
        out_specs=pl.BlockSpec((tt, d), row),
        out_shape=jax.ShapeDtypeStruct((m, d), F32),
        scratch_shapes=[pltpu.VMEM((2 * tt * TOP_K * ROW_SUB, LANES), jnp.uint32),
                        pltpu.SemaphoreType.DMA((2,))],
        compiler_params=_cparams(("arbitrary",)),
        name="combine_ln",
    )(pos_flat, pos_flat, h, route, g_row, b_row, outs)


def _row_tile(m, pref):
    t = min(m, pref)
    while m % t:
        t //= 2
    return t


def _route_tables(route, counts_row, m, tm):
    n_assign = m * TOP_K
    e_tok = route[:, :TOP_K].astype(jnp.int32)
    rank = route[:, 2 * TOP_K:3 * TOP_K].astype(jnp.int32)
    counts = counts_row[0, :N_EXPERTS].astype(jnp.int32)
    padded = (counts + tm - 1) // tm * tm
    pends = jnp.cumsum(padded)
    pstarts = pends - padded
    pos = pstarts[e_tok] + rank
    n_tiles = -(-n_assign // tm) + N_EXPERTS
    tile_start = jnp.arange(n_tiles, dtype=jnp.int32) * tm
    tile_expert = jnp.minimum(jnp.sum((pends[None, :] <= tile_start[:, None]).astype(jnp.int32), axis=1),
                              N_EXPERTS - 1)
    n_used = (pends[-1] // tm).astype(jnp.int32).reshape(1)
    last_used = jnp.maximum(n_used[0] - 1, 0)
    tile_expert = jnp.where(tile_start // tm < n_used[0], tile_expert, tile_expert[last_used])
    return pos.reshape(n_assign), tile_expert, n_used


def _layer(x, conv_hist, s_gdn, h_re, h_im, chunk, alpha, p):
    b, t, d = x.shape
    m = b * t
    d_gdn = GDN_HEADS * HEAD_DIM
    x2 = x.reshape(m, d)

    proj = _in_proj(x2, p['w_in'], _row_tile(m, 1024), p['proj_tn'])
    proj3 = proj.reshape(b, t, proj.shape[1])
    conv_new = proj3[:, t - (CONV_W - 1):, :3 * d_gdn]

    q, k, v, cols, gct = _gdn_prep(proj3, conv_hist, p['w_conv'], p['alog_row'], p['dtb_row'],
                                   chunk, p['ab_block'])
    o_gdn, s_new = _gdn(q, k, v, cols, gct, proj3, p['wn_row'], s_gdn, chunk, p['z_block'])

    n_gb = p['s5_mats'][0].shape[1]
    half = GROUPS_PER_BLOCK * SSM_P
    yg, hre_new, him_new = _s5(proj3, p['s5_mats'], h_re.reshape(b, n_gb, 1, half),
                               h_im.reshape(b, n_gb, 1, half), p['u_block0'])
    o_ssm = _glu(yg.reshape(m, yg.shape[2]), p['w_glu'], p['b_glu_row'], _row_tile(m, 512))

    h, hb, route, counts_row = _mix_ln_route(o_gdn.reshape(m, d_gdn), o_ssm, p['w_out_a'], p['w_out_b'], x2,
                                             p['ln1_g'], p['ln1_b'], p['w_router'], p['b_router'], alpha,
                                             _row_tile(m, 256))

    big = m * TOP_K >= 8 * p['moe_tm']
    tm = p['moe_tm'] if big else 128
    pos_flat, tile_expert, n_used = _route_tables(route, counts_row, m, tm)
    cap = tile_expert.shape[0] * tm
    x_rows = _dispatch_rows(pos_flat, hb, cap, _row_tile(m, 256))
    outs = _moe_ffn(tile_expert, n_used, x_rows, p['w_gate'], p['b_gate'],
                    p['w_up'], p['b_up'], p['w_down'], p['b_down'], tm, p['moe_tf'])
    y = _combine_ln(pos_flat, h, route, p['ln2_g'], p['ln2_b'], outs, alpha, _row_tile(m, 128))
    g_all = h_re.shape[1]
    return (y.reshape(b, t, d), conv_new, s_new,
            hre_new.reshape(b, g_all, SSM_P), him_new.reshape(b, g_all, SSM_P))


def _pad_lanes(v, fill=0.0):
    return jnp.pad(v.astype(F32), (0, LANES - v.shape[0]), constant_values=fill).reshape(1, LANES)


def _layer_params(l, w_in, w_conv, a_log, dt_bias, w_onorm, lam_re, lam_im, log_dt, b_re, b_im, c_re, c_im,
                  d_skip, w_glu, b_glu, w_out, ln1_g, ln1_b, w_router, b_router, w_gate, b_gate,
                  w_up, b_up, w_down, b_down, ln2_g, ln2_b):
    d_model = w_in.shape[1]
    d_gdn = GDN_HEADS * HEAD_DIM
    d_qkvz = 4 * d_gdn
    d_ssm = d_model - d_gdn
    wi = w_in[l]
    proj_tn = 1792
    n_cols = d_qkvz + d_ssm + LANES
    n_pad = -(-n_cols // proj_tn) * proj_tn
    w_in_r = jnp.concatenate([wi[:, :d_qkvz], wi[:, d_qkvz + 2 * GDN_HEADS:],
                              wi[:, d_qkvz:d_qkvz + 2 * GDN_HEADS],
                              jnp.zeros((d_model, n_pad - d_qkvz - d_ssm - 2 * GDN_HEADS), wi.dtype)], axis=1)
    wo = w_out[l].astype(BF16)
    wr = jnp.pad(w_router[l].astype(F32), ((0, 0), (0, LANES - N_EXPERTS)))
    return {
        'w_in': w_in_r.astype(BF16), 'proj_tn': proj_tn,
        'z_block': 3, 'u_block0': (d_qkvz) // LANES, 'ab_block': (d_qkvz + d_ssm) // LANES,
        'w_conv': w_conv[l].astype(F32),
        'alog_row': _pad_lanes(a_log[l]), 'dtb_row': _pad_lanes(dt_bias[l]),
        'wn_row': w_onorm[l].astype(F32).reshape(1, HEAD_DIM),
        's5_mats': _s5_matrices(lam_re[l], lam_im[l], log_dt[l], b_re[l], b_im[l], c_re[l], c_im[l], d_skip[l]),
        'w_glu': w_glu[l].astype(BF16), 'b_glu_row': b_glu[l].astype(F32).reshape(1, d_ssm),
        'w_out_a': wo[:d_gdn], 'w_out_b': wo[d_gdn:],
        'ln1_g': ln1_g[l].astype(F32).reshape(1, d_model), 'ln1_b': ln1_b[l].astype(F32).reshape(1, d_model),
        'w_router': wr, 'b_router': _pad_lanes(b_router[l]),
        'w_gate': w_gate[l], 'b_gate': b_gate[l], 'w_up': w_up[l], 'b_up': b_up[l],
        'w_down': w_down[l], 'b_down': b_down[l],
        'ln2_g': ln2_g[l].astype(F32).reshape(1, d_model), 'ln2_b': ln2_b[l].astype(F32).reshape(1, d_model),
        'moe_tm': 512, 'moe_tf': 512,
    }


def kernel(x_prompt, x_sample, state_conv, state_gdn, state_ssm_re, state_ssm_im, w_in, w_conv, a_log, dt_bias, w_onorm, lam_re, lam_im, log_dt, b_re, b_im, c_re, c_im, d_skip, w_glu, b_glu, w_out, ln1_g, ln1_b, w_router, b_router, w_gate, b_gate, w_up, b_up, w_down, b_down, ln2_g, ln2_b):
    depth = w_in.shape[0]
    alpha = (2.0 * depth) ** 0.25
    bp, seq, _ = x_prompt.shape
    chunk_p = 64
    d_qkv = state_conv.shape[-1]
    n_groups, n_p = state_ssm_re.shape[-2:]
    yp, ys = x_prompt, x_sample
    outs_p = [[], [], [], []]
    outs_s = [[], [], [], []]
    for l in range(depth):
        p = _layer_params(l, w_in, w_conv, a_log, dt_bias, w_onorm, lam_re, lam_im, log_dt, b_re, b_im,
                          c_re, c_im, d_skip, w_glu, b_glu, w_out, ln1_g, ln1_b, w_router, b_router,
                          w_gate, b_gate, w_up, b_up, w_down, b_down, ln2_g, ln2_b)
        yp, cp, sp, rp, ip = _layer(
            yp, jnp.zeros((bp, CONV_W - 1, d_qkv), F32),
            jnp.zeros((bp, GDN_HEADS, HEAD_DIM, HEAD_DIM), F32),
            jnp.zeros((bp, n_groups, n_p), F32), jnp.zeros((bp, n_groups, n_p), F32),
            chunk_p, alpha, p)
        ys, cs, ss, rs, is_ = _layer(
            ys, state_conv[l].astype(F32), state_gdn[l].astype(F32),
            state_ssm_re[l].astype(F32), state_ssm_im[l].astype(F32),
            ys.shape[1], alpha, p)
        for acc, val in zip(outs_p, (cp, sp, rp, ip)):
            acc.append(val)
        for acc, val in zip(outs_s, (cs, ss, rs, is_)):
            acc.append(val)
    return (yp, ys, *[jnp.stack(a) for a in outs_p], *[jnp.stack(a) for a in outs_s])
```

```python
import functools
import math

import jax
import jax.numpy as jnp
from jax import lax
from jax.experimental import pallas as pl
from jax.experimental.pallas import tpu as pltpu

F32 = jnp.float32
BF16 = jnp.bfloat16

GDN_HEADS = 8
HEAD_DIM = 128
CONV_W = 4
SSM_CG = 16
SSM_P = 64
N_EXPERTS = 32
TOP_K = 4
SWIGLU_ALPHA = 1.702
SWIGLU_LIMIT = 7.0
LN_EPS = 1e-5
RMS_EPS = 1e-6
L2_EPS = 1e-6

LANES = 128
SUBLANES = 8
S5_SUB = 8
S5_TILE = 2048
GDN_TILE = 128
GROUPS_PER_BLOCK = LANES // SSM_CG
VMEM_LIMIT = 56 * 1024 * 1024

NT_DIMS = (((1,), (1,)), ((), ()))
TN_DIMS = (((0,), (0,)), ((), ()))


def _dot(a, b, dims=(((1,), (0,)), ((), ()))):
    return lax.dot_general(a, b, dims, preferred_element_type=F32)


def _split(a):
    hi = a.astype(BF16)
    lo = (a - hi.astype(F32)).astype(BF16)
    return hi, lo


def _dot_x3(a, b, dims=(((1,), (0,)), ((), ()))):
    ah, al = _split(a)
    bh, bl = _split(b)
    return _dot(ah, bh, dims) + (_dot(ah, bl, dims) + _dot(al, bh, dims))


def _dot_bf(a, b, dims=(((1,), (0,)), ((), ()))):
    return _dot(a.astype(BF16), b.astype(BF16), dims)


_gdn_mm = _dot_bf


def _sigmoid(x):
    return 1.0 / (1.0 + jnp.exp(-x))


ROW_SUB = SUBLANES


def _pack_rows(x, ref, tm):
    half = x.shape[1] // 2
    for s in range(ROW_SUB):
        lo = x[:, s * LANES:(s + 1) * LANES].astype(BF16).astype(F32)
        hi = x[:, half + s * LANES:half + (s + 1) * LANES].astype(BF16).astype(F32)
        word = (lax.bitcast_convert_type(lo, jnp.uint32) >> 16) | lax.bitcast_convert_type(hi, jnp.uint32)
        ref[pl.ds(s, tm, stride=ROW_SUB), :] = word


def _unpack_rows(ref, base, tm):
    lo, hi = [], []
    for s in range(ROW_SUB):
        word = ref[pl.ds(base + s, tm, stride=ROW_SUB), :]
        lo.append(lax.bitcast_convert_type(word << 16, F32))
        hi.append(lax.bitcast_convert_type(word & jnp.uint32(0xFFFF0000), F32))
    return lo, hi


def _cparams(sem):
    return pltpu.CompilerParams(dimension_semantics=sem, vmem_limit_bytes=VMEM_LIMIT)


def _proj_body(x_ref, w_ref, o_ref, xb_ref):
    @pl.when(pl.program_id(1) == 0)
    def _():
        xb_ref[...] = x_ref[...].astype(BF16)

    o_ref[...] = _dot(xb_ref[...], w_ref[...])


def _in_proj(x2, w_bf, tm, tn):
    m, k = x2.shape
    n = w_bf.shape[1]
    return pl.pallas_call(
        _proj_body,
        grid=(m // tm, n // tn),
        in_specs=[pl.BlockSpec((tm, k), lambda i, j: (i, 0)),
                  pl.BlockSpec((k, tn), lambda i, j: (0, j))],
        out_specs=pl.BlockSpec((tm, tn), lambda i, j: (i, j)),
        out_shape=jax.ShapeDtypeStruct((m, n), F32),
        scratch_shapes=[pltpu.VMEM((tm, k), BF16)],
        compiler_params=_cparams(("parallel", "arbitrary")),
        name="in_proj",
    )(x2, w_bf)


def _gdn_prep_body(qkv_ref, ab_ref, hist_ref, wc_ref, alog_ref, dtb_ref,
                   q_ref, k_ref, v_ref, cols_ref, gct_ref, xbuf_ref, *, tt, chunk):
    d_gdn = GDN_HEADS * HEAD_DIM
    halo = SUBLANES

    @pl.when(pl.program_id(1) == 0)
    def _():
        xbuf_ref[0:halo, :] = jnp.zeros((halo, 3 * d_gdn), F32)
        xbuf_ref[halo - (CONV_W - 1):halo, :] = hist_ref[0]

    xbuf_ref[halo:halo + tt, :] = qkv_ref[0]

    for part, out_ref in enumerate((q_ref, k_ref, v_ref)):
        c0 = part * d_gdn
        y = None
        for j in range(CONV_W):
            r0 = halo - (CONV_W - 1) + j
            term = xbuf_ref[r0:r0 + tt, c0:c0 + d_gdn] * wc_ref[j:j + 1, c0:c0 + d_gdn]
            y = term if y is None else y + term
        s = y * _sigmoid(y)
        if part == 2:
            out_ref[0] = s
        else:
            for h in range(GDN_HEADS):
                sh = s[:, h * HEAD_DIM:(h + 1) * HEAD_DIM]
                nrm = sh * lax.rsqrt(jnp.sum(sh * sh, axis=-1, keepdims=True) + L2_EPS)
                if part == 0:
                    nrm = nrm * (HEAD_DIM ** -0.5)
                out_ref[0, :, h * HEAD_DIM:(h + 1) * HEAD_DIM] = nrm

    xbuf_ref[0:halo, :] = xbuf_ref[tt:tt + halo, :]

    ab = ab_ref[0]
    lane = lax.broadcasted_iota(jnp.int32, (tt, LANES), 1)
    is_a = lane < GDN_HEADS
    z = ab + dtb_ref[...]
    softplus = jnp.maximum(z, 0.0) + jnp.log1p(jnp.exp(-jnp.abs(z)))
    g = jnp.where(is_a, -jnp.exp(alog_ref[...]) * softplus, 0.0)
    beta = _sigmoid(ab)

    shift = int(math.log2(chunk))
    r = lax.broadcasted_iota(jnp.int32, (tt, tt), 0)
    c = lax.broadcasted_iota(jnp.int32, (tt, tt), 1)
    same = (r >> shift) == (c >> shift)
    m_incl = jnp.where(same & (r >= c), 1.0, 0.0).astype(BF16)
    m_all = jnp.where(same, 1.0, 0.0).astype(BF16)
    gc = _dot_exact_lhs_rhs(m_incl, g)
    glast = _dot_exact_lhs_rhs(m_all, g)
    eg = jnp.exp(gc)
    egl = jnp.exp(glast - gc)
    egt = jnp.exp(glast)
    zero = jnp.zeros_like(gc)
    cols = (jnp.where(is_a, gc, zero)
            + jnp.where((lane >= 8) & (lane < 16), beta, zero)
            + pltpu.roll(jnp.where(is_a, eg, zero), 16, 1)
            + pltpu.roll(jnp.where(is_a, egl, zero), 24, 1)
            + pltpu.roll(jnp.where(is_a, egt, zero), 32, 1))
    cols_ref[0] = cols

    er = lax.broadcasted_iota(jnp.int32, (SUBLANES, LANES), 0)
    ec = lax.broadcasted_iota(jnp.int32, (SUBLANES, LANES), 1)
    sel = jnp.where(er == ec, 1.0, 0.0).astype(BF16)
    gct_ref[0] = _dot_exact_rhs(sel, jnp.where(is_a, gc, zero), NT_DIMS)


def _three_pieces(a):
    a0 = a.astype(BF16)
    r1 = a - a0.astype(F32)
    a1 = r1.astype(BF16)
    a2 = (r1 - a1.astype(F32)).astype(BF16)
    return a0, a1, a2


def _dot_exact_lhs_rhs(mask_bf, a):
    a0, a1, a2 = _three_pieces(a)
    return _dot(mask_bf, a0) + (_dot(mask_bf, a1) + _dot(mask_bf, a2))


def _dot_exact_rhs(mask_bf, a, dims):
    a0, a1, a2 = _three_pieces(a)
    return _dot(mask_bf, a0, dims) + (_dot(mask_bf, a1, dims) + _dot(mask_bf, a2, dims))


def _gdn_prep(proj3, hist, w_conv, alog_p, dtb_p, chunk, ab_block):
    b, t, _ = proj3.shape
    d_gdn = GDN_HEADS * HEAD_DIM
    tt = min(t, GDN_TILE)
    body = functools.partial(_gdn_prep_body, tt=tt, chunk=chunk)
    big = jax.ShapeDtypeStruct((b, t, d_gdn), F32)
    return pl.pallas_call(
        body,
        grid=(b, t // tt),
        in_specs=[pl.BlockSpec((1, tt, 3 * d_gdn), lambda i, j: (i, j, 0)),
                  pl.BlockSpec((1, tt, LANES), lambda i, j: (i, j, ab_block)),
                  pl.BlockSpec((1, CONV_W - 1, 3 * d_gdn), lambda i, j: (i, 0, 0)),
                  pl.BlockSpec((CONV_W, 3 * d_gdn), lambda i, j: (0, 0)),
                  pl.BlockSpec((1, LANES), lambda i, j: (0, 0)),
                  pl.BlockSpec((1, LANES), lambda i, j: (0, 0))],
        out_specs=[pl.BlockSpec((1, tt, d_gdn), lambda i, j: (i, j, 0)),
                   pl.BlockSpec((1, tt, d_gdn), lambda i, j: (i, j, 0)),
                   pl.BlockSpec((1, tt, d_gdn), lambda i, j: (i, j, 0)),
                   pl.BlockSpec((1, tt, LANES), lambda i, j: (i, j, 0)),
                   pl.BlockSpec((1, SUBLANES, tt), lambda i, j: (i, 0, j))],
        out_shape=[big, big, big,
                   jax.ShapeDtypeStruct((b, t, LANES), F32),
                   jax.ShapeDtypeStruct((b, SUBLANES, t), F32)],
        scratch_shapes=[pltpu.VMEM((tt + SUBLANES, 3 * d_gdn), F32)],
        compiler_params=_cparams(("parallel", "arbitrary")),
        name="gdn_prep",
    )(proj3, proj3, hist, w_conv, alog_p, dtb_p)


def _unit_lower_inverses(mats, r, c, chunk):
    base = 16
    eye = jnp.where(r == c, 1.0, 0.0)

    def blk(bs):
        s = int(math.log2(bs))
        return (r >> s) == (c >> s)

    d1 = [jnp.where(blk(base), a, 0.0) for a in mats]
    d2 = [_gdn_mm(x, x) for x in d1]
    d4 = [_gdn_mm(x, x) for x in d2]
    d8 = [_gdn_mm(x, x) for x in d4]
    t = [eye - x for x in d1]
    t = [x + _gdn_mm(x, y) for x, y in zip(t, d2)]
    t = [x + _gdn_mm(x, y) for x, y in zip(t, d4)]
    t = [x + _gdn_mm(x, y) for x, y in zip(t, d8)]
    bs = base
    while bs < chunk:
        off_mask = blk(2 * bs) & jnp.logical_not(blk(bs))
        inner = [_gdn_mm(jnp.where(off_mask, a, 0.0), x) for a, x in zip(mats, t)]
        t = [x - _gdn_mm(x, y) for x, y in zip(t, inner)]
        bs *= 2
    return t


def _gdn_body(q_ref, k_ref, v_ref, cols_ref, gct_ref, z_ref, wn_ref, s0_ref,
              o_ref, s_ref, *, tb, chunk):
    @pl.when(pl.program_id(1) == 0)
    def _():
        s_ref[...] = s0_ref[...]

    shift = int(math.log2(chunk))
    r = lax.broadcasted_iota(jnp.int32, (tb, tb), 0)
    c = lax.broadcasted_iota(jnp.int32, (tb, tb), 1)
    same = (r >> shift) == (c >> shift)
    incl = same & (r >= c)
    strict = same & (r > c)
    n_chunks = tb // chunk

    heads = range(GDN_HEADS)
    hsl = [slice(h * HEAD_DIM, (h + 1) * HEAD_DIM) for h in heads]
    q = [q_ref[0, :, s] for s in hsl]
    k = [k_ref[0, :, s] for s in hsl]
    beta = [cols_ref[0, :, 8 + h:9 + h] for h in heads]
    eg = [cols_ref[0, :, 16 + h:17 + h] for h in heads]
    egl = [cols_ref[0, :, 24 + h:25 + h] for h in heads]
    decay = [jnp.exp(jnp.where(incl, cols_ref[0, :, h:h + 1] - gct_ref[0, h:h + 1, :], -jnp.inf)) for h in heads]
    kb = [k[h] * beta[h] for h in heads]
    k_bf = [x.astype(BF16) for x in k]
    a = [jnp.where(strict, _dot(kb[h].astype(BF16), k_bf[h], NT_DIMS) * decay[h], 0.0) for h in heads]
    tinv = _unit_lower_inverses(a, r, c, chunk)
    sol = [_gdn_mm(tinv[h], jnp.concatenate([v_ref[0, :, hsl[h]] * beta[h], kb[h] * eg[h]], axis=1)) for h in heads]
    u = [x[:, :HEAD_DIM] for x in sol]
    w_bf = [x[:, HEAD_DIM:].astype(BF16) for x in sol]
    attn = [(_dot(q[h].astype(BF16), k_bf[h], NT_DIMS) * decay[h]).astype(BF16) for h in heads]
    q_dec = [(q[h] * eg[h]).astype(BF16) for h in heads]
    k_dec = [(k[h] * egl[h]).astype(BF16) for h in heads]

    s = [s_ref[0, h] for h in heads]
    v_new = [[] for _ in heads]
    o_state = [[] for _ in heads]
    for ci in range(n_chunks):
        rs = slice(ci * chunk, (ci + 1) * chunk)
        s_bf = [x.astype(BF16) for x in s]
        vn = [u[h][rs] - _dot(w_bf[h][rs], s_bf[h]) for h in heads]
        for h in heads:
            o_state[h].append(_dot(q_dec[h][rs], s_bf[h]))
            v_new[h].append(vn[h])
        s = [s[h] * cols_ref[0, ci * chunk:ci * chunk + 1, 32 + h:33 + h]
             + _dot(k_dec[h][rs], vn[h].astype(BF16), TN_DIMS) for h in heads]
    for h in heads:
        s_ref[0, h] = s[h]

    def cat(parts):
        return parts[0] if len(parts) == 1 else jnp.concatenate(parts, axis=0)

    for h in heads:
        o = cat(o_state[h]) + _dot(attn[h], cat(v_new[h]).astype(BF16))
        zh = z_ref[0, :, hsl[h]]
        o = (o * lax.rsqrt(jnp.mean(o * o, axis=-1, keepdims=True) + RMS_EPS) * wn_ref[...]
             * (zh * _sigmoid(zh)))
        o_ref[0, :, hsl[h]] = o.astype(BF16)


def _gdn(q, k, v, cols, gct, proj3, wn, s0, chunk, z_block):
    b, t, d_gdn = q.shape
    tb = min(t, GDN_TILE)
    body = functools.partial(_gdn_body, tb=tb, chunk=chunk)
    tile = lambda i, j: (i, j, 0)
    return pl.pallas_call(
        body,
        grid=(b, t // tb),
        in_specs=[pl.BlockSpec((1, tb, d_gdn), tile),
                  pl.BlockSpec((1, tb, d_gdn), tile),
                  pl.BlockSpec((1, tb, d_gdn), tile),
                  pl.BlockSpec((1, tb, LANES), tile),
                  pl.BlockSpec((1, SUBLANES, tb), lambda i, j: (i, 0, j)),
                  pl.BlockSpec((1, tb, d_gdn), lambda i, j: (i, j, z_block)),
                  pl.BlockSpec((1, HEAD_DIM), lambda i, j: (0, 0)),
                  pl.BlockSpec((1, GDN_HEADS, HEAD_DIM, HEAD_DIM), lambda i, j: (i, 0, 0, 0))],
        out_specs=[pl.BlockSpec((1, tb, d_gdn), tile),
                   pl.BlockSpec((1, GDN_HEADS, HEAD_DIM, HEAD_DIM), lambda i, j: (i, 0, 0, 0))],
        out_shape=[jax.ShapeDtypeStruct((b, t, d_gdn), BF16),
                   jax.ShapeDtypeStruct((b, GDN_HEADS, HEAD_DIM, HEAD_DIM), F32)],
        compiler_params=_cparams(("parallel", "arbitrary")),
        name="gdn_delta",
    )(q, k, v, cols, gct, proj3, wn, s0)


def _gelu_tanh(x):
    return 0.5 * x * (1.0 + jnp.tanh(math.sqrt(2.0 / math.pi) * (x + 0.044715 * (x * x * x))))


def _s5_body(u_ref, wm_ref, ym0_ref, ym1_ref, ym2_ref, ym3_ref, lre_ref, lim_ref, dsk_ref, h0re_ref, h0im_ref,
             y_ref, hre_ref, him_ref, xbuf_ref, hbuf_ref, *, n_sub, seg):
    half = GROUPS_PER_BLOCK * SSM_P

    if seg is None:
        @pl.when(pl.program_id(2) == 0)
        def _():
            hre_ref[...] = h0re_ref[...]
            him_ref[...] = h0im_ref[...]

    u_f = [u_ref[0, pl.ds(j, n_sub, stride=S5_SUB), :] for j in range(S5_SUB)]
    u_b = [x.astype(BF16) for x in u_f]

    xbuf_ref[...] = _dot(jnp.concatenate(u_b, axis=1), wm_ref[0])

    lre = lre_ref[0]
    lim = lim_ref[0]

    def advance(n, hre, him):
        hbuf_ref[pl.ds(n, 1), 0:half] = hre
        hbuf_ref[pl.ds(n, 1), half:2 * half] = him
        xr = xbuf_ref[pl.ds(n, 1), 0:half]
        xi = xbuf_ref[pl.ds(n, 1), half:2 * half]
        return (lre * hre - lim * him + xr, lre * him + lim * hre + xi)

    if seg is None:
        hre, him = lax.fori_loop(0, n_sub, lambda n, c: advance(n, *c), (hre_ref[0, 0], him_ref[0, 0]))
        hre_ref[0, 0] = hre
        him_ref[0, 0] = him
    else:
        def step(n, carry):
            q = n // seg
            first = n - q * seg == 0
            hre = jnp.where(first, h0re_ref[q, 0], carry[0])
            him = jnp.where(first, h0im_ref[q, 0], carry[1])
            hre, him = advance(n, hre, him)
            hre_ref[q, 0] = hre
            him_ref[q, 0] = him
            return hre, him

        lax.fori_loop(0, n_sub, step, (h0re_ref[0, 0], h0im_ref[0, 0]))

    h_b = hbuf_ref[...].astype(BF16)
    dsk = dsk_ref[0]
    for l, ym_ref in zip(range(0, S5_SUB, 2), (ym0_ref, ym1_ref, ym2_ref, ym3_ref)):
        lhs = jnp.concatenate([h_b] + [u_b[m] for m in range(l + 1, -1, -1)], axis=1)
        y2 = _dot(lhs, ym_ref[0])
        for o in range(2):
            y = y2[:, o * LANES:(o + 1) * LANES] + dsk * u_f[l + o]
            y_ref[0, pl.ds(l + o, n_sub, stride=S5_SUB), :] = _gelu_tanh(y)


def _s5(proj3, mats, h0re, h0im, u_block0):
    b, t, width = proj3.shape
    n_gb = mats[0].shape[0]
    if b > 1 and b * t <= S5_TILE:
        y, hre, him = _s5_call(proj3.reshape(1, b * t, width), mats, h0re, h0im, u_block0,
                               tt=b * t, seg=t // S5_SUB, state_rows=b)
        return y.reshape(b, t, n_gb * LANES), hre, him
    return _s5_call(proj3, mats, h0re, h0im, u_block0, tt=min(t, S5_TILE), seg=None, state_rows=1)


def _s5_call(proj3, mats, h0re, h0im, u_block0, tt, seg, state_rows):
    wcat, ycat, lre, lim, dsk = mats
    b, t, _ = proj3.shape
    n_gb = wcat.shape[0]
    n_sub = tt // S5_SUB
    half = GROUPS_PER_BLOCK * SSM_P
    body = functools.partial(_s5_body, n_sub=n_sub, seg=seg)
    state_spec = pl.BlockSpec((state_rows, 1, 1, half), lambda g, i, j: (i, g, 0, 0))
    par_spec = pl.BlockSpec((1, 1, half), lambda g, i, j: (g, 0, 0))
    mat_spec = lambda a: pl.BlockSpec((1,) + a.shape[1:], lambda g, i, j: (g, 0, 0))
    return pl.pallas_call(
        body,
        grid=(n_gb, b, t // tt),
        in_specs=[pl.BlockSpec((1, tt, LANES), lambda g, i, j: (i, j, u_block0 + g)),
                  mat_spec(wcat)] + [mat_spec(a) for a in ycat] + [
                  par_spec, par_spec,
                  pl.BlockSpec((1, 1, LANES), lambda g, i, j: (g, 0, 0)),
                  state_spec, state_spec],
        out_specs=[pl.BlockSpec((1, tt, LANES), lambda g, i, j: (i, j, g)),
                   state_spec, state_spec],
        out_shape=[jax.ShapeDtypeStruct((b, t, n_gb * LANES), F32),
                   jax.ShapeDtypeStruct(h0re.shape, F32),
                   jax.ShapeDtypeStruct(h0im.shape, F32)],
        scratch_shapes=[pltpu.VMEM((n_sub, 2 * half), F32),
                        pltpu.VMEM((n_sub, 2 * half), F32)],
        compiler_params=_cparams(("parallel", "parallel", "arbitrary")),
        name="s5_scan",
    )(proj3, wcat, *ycat, lre, lim, dsk, h0re, h0im)


def _s5_matrices(lam_re, lam_im, log_dt, b_re, b_im, c_re, c_im, d_skip):
    g, p = lam_re.shape
    n_gb = g // GROUPS_PER_BLOCK
    gpb = GROUPS_PER_BLOCK
    dt = jnp.exp(log_dt.astype(F32))
    lam = lax.complex(jnp.minimum(lam_re.astype(F32), -1e-4), lam_im.astype(F32))
    lam_bar = jnp.exp(lam * dt[:, None])
    b_bar = ((lam_bar - 1.0) / lam)[..., None] * lax.complex(b_re.astype(F32), b_im.astype(F32))
    c_c = lax.complex(c_re.astype(F32), c_im.astype(F32))
    pows = [jnp.ones_like(lam_bar)]
    for _ in range(S5_SUB):
        pows.append(pows[-1] * lam_bar)
    pw = jnp.stack(pows)
    eye = jnp.eye(gpb, dtype=F32)

    kd = jnp.real(jnp.einsum('gop,dgp,gpi->dgio', c_c, pw[:S5_SUB], b_bar))
    km = jnp.einsum('dbgio,gh->dbgiho', kd.reshape(S5_SUB, n_gb, gpb, SSM_CG, SSM_CG), eye)
    km = km.reshape(S5_SUB, n_gb, LANES, LANES)

    wj = pw[:S5_SUB][::-1][:, :, :, None] * b_bar[None]
    wj = jnp.transpose(wj, (0, 1, 3, 2)).reshape(S5_SUB, n_gb, gpb, SSM_CG, p)
    wre = jnp.einsum('dbgcp,gh->dbgchp', jnp.real(wj), eye).reshape(S5_SUB, n_gb, LANES, gpb * p)
    wim = jnp.einsum('dbgcp,gh->dbgchp', jnp.imag(wj), eye).reshape(S5_SUB, n_gb, LANES, gpb * p)
    wm = jnp.concatenate([wre, wim], axis=-1)

    cl = c_c[None] * pw[1:S5_SUB + 1][:, :, None, :]
    cl = jnp.transpose(cl, (0, 1, 3, 2)).reshape(S5_SUB, n_gb, gpb, p, SSM_CG)
    vre = jnp.einsum('dbgpc,gh->dbgphc', jnp.real(cl), eye).reshape(S5_SUB, n_gb, gpb * p, LANES)
    vim = jnp.einsum('dbgpc,gh->dbgphc', -jnp.imag(cl), eye).reshape(S5_SUB, n_gb, gpb * p, LANES)
    vm = jnp.concatenate([vre, vim], axis=-2)

    lam_s = pw[S5_SUB].reshape(n_gb, 1, gpb * p)
    dsk = d_skip.astype(F32).reshape(n_gb, 1, LANES)

    km, wm, vm = km.astype(BF16), wm.astype(BF16), vm.astype(BF16)
    wcat = jnp.transpose(wm, (1, 0, 2, 3)).reshape(n_gb, S5_SUB * LANES, 2 * gpb * p)
    ycat = []
    zero = jnp.zeros_like(km[0])
    for l in range(0, S5_SUB, 2):
        rows = [jnp.concatenate([vm[l], vm[l + 1]], axis=-1)]
        for m in range(l + 1, -1, -1):
            left = km[l - m] if m <= l else zero
            rows.append(jnp.concatenate([left, km[l + 1 - m]], axis=-1))
        ycat.append(jnp.concatenate(rows, axis=-2))
    return (wcat, tuple(ycat), jnp.real(lam_s), jnp.imag(lam_s), dsk)


def _glu_body(y_ref, w_ref, b_ref, o_ref):
    y = y_ref[...]
    gate = _dot(y.astype(BF16), w_ref[...]) + b_ref[...]
    o_ref[...] = (y * _sigmoid(gate)).astype(BF16)


def _glu(y2, w_bf, b_row, tm):
    m, d = y2.shape
    return pl.pallas_call(
        _glu_body,
        grid=(m // tm,),
        in_specs=[pl.BlockSpec((tm, d), lambda i: (i, 0)),
                  pl.BlockSpec((d, d), lambda i: (0, 0)),
                  pl.BlockSpec((1, d), lambda i: (0, 0))],
        out_specs=pl.BlockSpec((tm, d), lambda i: (i, 0)),
        out_shape=jax.ShapeDtypeStruct((m, d), BF16),
        compiler_params=_cparams(("parallel",)),
        name="s5_glu",
    )(y2, w_bf, b_row)


def _layernorm(v, g, b):
    mu = jnp.mean(v, axis=-1, keepdims=True)
    var = jnp.mean(jnp.square(v - mu), axis=-1, keepdims=True)
    return (v - mu) * lax.rsqrt(var + LN_EPS) * g + b


def _mix_body(og_ref, os_ref, wa_ref, wb_ref, x_ref, g_ref, b_ref, wr_ref, br_ref,
              h_ref, hb_ref, route_ref, cnt_ref, *, alpha):
    @pl.when(pl.program_id(0) == 0)
    def _():
        cnt_ref[...] = jnp.zeros_like(cnt_ref)

    mix = _dot(og_ref[...], wa_ref[...]) + _dot(os_ref[...], wb_ref[...])
    h = _layernorm(alpha * x_ref[...] + mix, g_ref[...], b_ref[...])
    h_ref[...] = h
    tm = h.shape[0]
    _pack_rows(h, hb_ref, tm)

    logits = _dot_x3(h, wr_ref[...]) + br_ref[...]
    lane = lax.broadcasted_iota(jnp.int32, (tm, LANES), 1)
    work = jnp.where(lane < N_EXPERTS, logits, -jnp.inf)
    vals, idxs = [], []
    for _ in range(TOP_K):
        mx = jnp.max(work, axis=-1, keepdims=True)
        ix = jnp.min(jnp.where(work == mx, lane, LANES), axis=-1, keepdims=True)
        vals.append(mx)
        idxs.append(ix)
        work = jnp.where(lane == ix, -jnp.inf, work)
    exps = [jnp.exp(v - vals[0]) for v in vals]
    denom = exps[0]
    for e in exps[1:]:
        denom = denom + e
    chosen = jnp.zeros((tm, LANES), F32)
    for k in range(TOP_K):
        chosen = jnp.where(lane == idxs[k], 1.0, chosen)
    rr = lax.broadcasted_iota(jnp.int32, (tm, tm), 0)
    cc = lax.broadcasted_iota(jnp.int32, (tm, tm), 1)
    lower = jnp.where(rr > cc, 1.0, 0.0).astype(BF16)
    before = cnt_ref[...] + _dot(lower, chosen.astype(BF16))
    cnt_ref[...] = cnt_ref[...] + jnp.sum(chosen, axis=0, keepdims=True)

    route = jnp.zeros((tm, LANES), F32)
    for k in range(TOP_K):
        rank = jnp.sum(jnp.where(lane == idxs[k], before, 0.0), axis=-1, keepdims=True)
        route = jnp.where(lane == k, idxs[k].astype(F32), route)
        route = jnp.where(lane == TOP_K + k, exps[k] / denom, route)
        route = jnp.where(lane == 2 * TOP_K + k, rank, route)
    route_ref[...] = route


def _mix_ln_route(og, osm, wa, wb, x2, g_row, b_row, wr, br, alpha, tm):
    m, d = x2.shape
    dh = og.shape[1]
    body = functools.partial(_mix_body, alpha=alpha)
    row = lambda i: (i, 0)
    fix = lambda i: (0, 0)
    return pl.pallas_call(
        body,
        grid=(m // tm,),
        in_specs=[pl.BlockSpec((tm, dh), row), pl.BlockSpec((tm, dh), row),
                  pl.BlockSpec((dh, d), fix), pl.BlockSpec((dh, d), fix),
                  pl.BlockSpec((tm, d), row),
                  pl.BlockSpec((1, d), fix), pl.BlockSpec((1, d), fix),
                  pl.BlockSpec((d, LANES), fix), pl.BlockSpec((1, LANES), fix)],
        out_specs=[pl.BlockSpec((tm, d), row), pl.BlockSpec((tm * ROW_SUB, LANES), row),
                   pl.BlockSpec((tm, LANES), row), pl.BlockSpec((1, LANES), fix)],
        out_shape=[jax.ShapeDtypeStruct((m, d), F32),
                   jax.ShapeDtypeStruct((m * ROW_SUB, LANES), jnp.uint32),
                   jax.ShapeDtypeStruct((m, LANES), F32),
                   jax.ShapeDtypeStruct((1, LANES), F32)],
        compiler_params=_cparams(("arbitrary",)),
        name="mix_ln_route",
    )(og, osm, wa, wb, x2, g_row, b_row, wr, br)


def _dispatch_body(pos_ref, src_ref, init_ref, dst_ref, sem, *, tt):
    del init_ref

    def issue(t, carry):
        srow = pl.multiple_of(t * ROW_SUB, ROW_SUB)
        for k in range(TOP_K):
            drow = pl.multiple_of(pos_ref[t * TOP_K + k] * ROW_SUB, ROW_SUB)
            pltpu.make_async_copy(src_ref.at[pl.ds(srow, ROW_SUB)], dst_ref.at[pl.ds(drow, ROW_SUB)],
                                  sem).start(priority=k % 2)
        return carry

    lax.fori_loop(0, tt, issue, 0, unroll=4)
    n = tt * ROW_SUB
    for _ in range(TOP_K):
        pltpu.make_async_copy(src_ref.at[pl.ds(0, n)], dst_ref.at[pl.ds(0, n)], sem).wait()


def _dispatch_rows(pos_flat, hbp, cap, tt):
    m = hbp.shape[0] // ROW_SUB
    body = functools.partial(_dispatch_body, tt=tt)
    init = jnp.zeros((cap * ROW_SUB, LANES), jnp.uint32)
    return pl.pallas_call(
        body,
        grid=(m // tt,),
        in_specs=[pl.BlockSpec((tt * TOP_K,), lambda i: (i,), memory_space=pltpu.SMEM),
                  pl.BlockSpec((tt * ROW_SUB, LANES), lambda i: (i, 0)), pl.BlockSpec(memory_space=pl.ANY)],
        out_specs=pl.BlockSpec(memory_space=pl.ANY),
        out_shape=jax.ShapeDtypeStruct((cap * ROW_SUB, LANES), jnp.uint32),
        scratch_shapes=[pltpu.SemaphoreType.DMA(())],
        input_output_aliases={2: 0},
        compiler_params=_cparams(("arbitrary",)),
        name="moe_dispatch",
    )(pos_flat, hbp, init)


MOE_COL = 256
MOE_GROUP = 2


def _moe_body(te_ref, nu_ref, x_ref, wg_ref, bg_ref, wu_ref, bu_ref, wd_ref, bd_ref,
              o_ref, xb_ref, acc_ref, *, n_f, tm, tf):
    g = pl.program_id(2)
    f = pl.program_id(1)
    used = pl.program_id(0) * MOE_GROUP + g < nu_ref[0]
    half = xb_ref.shape[2] // 2
    xb_ref = xb_ref.at[g]
    acc_ref = acc_ref.at[g]

    @pl.when(used & (f == 0))
    def _():
        lo, hi = _unpack_rows(x_ref, 0, tm)
        for r in range(ROW_SUB):
            xb_ref[:, r * LANES:(r + 1) * LANES] = lo[r].astype(BF16)
            xb_ref[:, half + r * LANES:half + (r + 1) * LANES] = hi[r].astype(BF16)

    @pl.when(used)
    def _():
        xb = xb_ref[...]
        part = None
        for c in range(tf // MOE_COL):
            cs = slice(c * MOE_COL, (c + 1) * MOE_COL)
            hg = jnp.minimum(_dot(xb, wg_ref[0, :, cs].astype(BF16)) + bg_ref[0, :, cs], SWIGLU_LIMIT)
            hu = jnp.clip(_dot(xb, wu_ref[0, :, cs].astype(BF16)) + bu_ref[0, :, cs], -SWIGLU_LIMIT, SWIGLU_LIMIT)
            hh = ((hu + 1.0) * (hg * _sigmoid(SWIGLU_ALPHA * hg))).astype(BF16)
            p = _dot(hh, wd_ref[0, cs, :].astype(BF16))
            part = p if part is None else part + p

        @pl.when(f == 0)
        def _():
            acc_ref[...] = part

        @pl.when(f != 0)
        def _():
            acc_ref[...] += part

        @pl.when(f == n_f - 1)
        def _():
            _pack_rows(acc_ref[...] + bd_ref[0], o_ref, tm)

    @pl.when(jnp.logical_not(used) & (f == n_f - 1))
    def _():
        o_ref[...] = jnp.zeros_like(o_ref)


def _moe_ffn(tile_expert, n_used, x_rows, w_gate, b_gate, w_up, b_up, w_down, b_down, tm, tf):
    n_e, d, d_ff = w_gate.shape
    n_tiles = x_rows.shape[0] // (tm * ROW_SUB)
    n_f = d_ff // tf
    grp = MOE_GROUP
    assert n_tiles % grp == 0
    body = functools.partial(_moe_body, n_f=n_f, tm=tm, tf=tf)

    def tile(p, g):
        return p * grp + g

    def fcol(p, f, nu):
        return jnp.where(p * grp < nu[0], f, n_f - 1)

    def xrow(p, f, g):
        return jnp.where(f == 0, tile(p, g), tile(p, grp - 1))

    def orow(p, f, g):
        return jnp.where(f == n_f - 1, tile(p, g), tile(p, 0))

    grid_spec = pltpu.PrefetchScalarGridSpec(
        num_scalar_prefetch=2,
        grid=(n_tiles // grp, n_f, grp),
        in_specs=[pl.BlockSpec((tm * ROW_SUB, LANES), lambda p, f, g, te, nu: (xrow(p, f, g), 0)),
                  pl.BlockSpec((1, d, tf), lambda p, f, g, te, nu: (te[tile(p, g)], 0, fcol(p, f, nu))),
                  pl.BlockSpec((1, 1, tf), lambda p, f, g, te, nu: (te[tile(p, g)], 0, fcol(p, f, nu))),
                  pl.BlockSpec((1, d, tf), lambda p, f, g, te, nu: (te[tile(p, g)], 0, fcol(p, f, nu))),
                  pl.BlockSpec((1, 1, tf), lambda p, f, g, te, nu: (te[tile(p, g)], 0, fcol(p, f, nu))),
                  pl.BlockSpec((1, tf, d), lambda p, f, g, te, nu: (te[tile(p, g)], fcol(p, f, nu), 0)),
                  pl.BlockSpec((1, 1, d), lambda p, f, g, te, nu: (te[tile(p, g)], 0, 0))],
        out_specs=pl.BlockSpec((tm * ROW_SUB, LANES), lambda p, f, g, te, nu: (orow(p, f, g), 0)),
        scratch_shapes=[pltpu.VMEM((grp, tm, d), BF16), pltpu.VMEM((grp, tm, d), F32)],
    )
    return pl.pallas_call(
        body,
        grid_spec=grid_spec,
        out_shape=jax.ShapeDtypeStruct(x_rows.shape, jnp.uint32),
        compiler_params=_cparams(("arbitrary", "arbitrary", "arbitrary")),
        name="moe_ffn",
    )(tile_expert, n_used, x_rows, w_gate, b_gate.reshape(n_e, 1, d_ff),
      w_up, b_up.reshape(n_e, 1, d_ff), w_down, b_down.reshape(n_e, 1, d))


def _final_body(pos_cur_ref, pos_nxt_ref, h_ref, route_ref, g_ref, b_ref, src_ref,
                o_ref, buf_ref, sem, *, alpha, tt, n_steps):
    i = pl.program_id(0)
    slot_rows = tt * TOP_K * ROW_SUB

    def issue(pref, slot):
        def one(t, carry):
            for k in range(TOP_K):
                srow = pl.multiple_of(pref[t * TOP_K + k] * ROW_SUB, ROW_SUB)
                drow = pl.multiple_of(slot * slot_rows + (k * tt + t) * ROW_SUB, ROW_SUB)
                pltpu.make_async_copy(src_ref.at[pl.ds(srow, ROW_SUB)], buf_ref.at[pl.ds(drow, ROW_SUB)],
                                      sem.at[slot]).start(priority=k % 2)
            return carry
        lax.fori_loop(0, tt, one, 0, unroll=4)

    @pl.when(i == 0)
    def _():
        issue(pos_cur_ref, 0)

    @pl.when(i + 1 < n_steps)
    def _():
        issue(pos_nxt_ref, (i + 1) % 2)

    slot = i % 2
    base = pl.multiple_of(slot * slot_rows, ROW_SUB)
    pltpu.make_async_copy(src_ref.at[pl.ds(0, slot_rows)], buf_ref.at[pl.ds(base, slot_rows)], sem.at[slot]).wait()

    ff_lo = [None] * ROW_SUB
    ff_hi = [None] * ROW_SUB
    for k in range(TOP_K):
        gate = route_ref[:, TOP_K + k:TOP_K + k + 1]
        lo, hi = _unpack_rows(buf_ref, base + k * tt * ROW_SUB, tt)
        for s in range(ROW_SUB):
            ff_lo[s] = lo[s] * gate if k == 0 else ff_lo[s] + lo[s] * gate
            ff_hi[s] = hi[s] * gate if k == 0 else ff_hi[s] + hi[s] * gate
    ff = jnp.concatenate(ff_lo + ff_hi, axis=1)
    o_ref[...] = _layernorm(alpha * h_ref[...] + ff, g_ref[...], b_ref[...])


def _combine_ln(pos_flat, h, route, g_row, b_row, outs, alpha, tt):
    m, d = h.shape
    n_steps = m // tt
    body = functools.partial(_final_body, alpha=alpha, tt=tt, n_steps=n_steps)
    row = lambda i: (i, 0)
    fix = lambda i: (0, 0)
    return pl.pallas_call(
        body,
        grid=(n_steps,),
        in_specs=[pl.BlockSpec((tt * TOP_K,), lambda i: (i,), memory_space=pltpu.SMEM),
                  pl.BlockSpec((tt * TOP_K,), lambda i: (jnp.minimum(i + 1, n_steps - 1),),
                               memory_space=pltpu.SMEM),
                  pl.BlockSpec((tt, d), row), pl.BlockSpec((tt, LANES), row),
                  pl.BlockSpec((1, d), fix), pl.BlockSpec((1, d), fix),
                  pl.BlockSpec(memory_space=pl.ANY)],
        out_specs=pl.BlockSpec((tt, d), row),
        out_shape=jax.ShapeDtypeStruct((m, d), F32),
        scratch_shapes=[pltpu.VMEM((2 * tt * TOP_K * ROW_SUB, LANES), jnp.uint32),
                        pltpu.SemaphoreType.DMA((2,))],
        compiler_params=_cparams(("arbitrary",)),
        name="combine_ln",
    )(pos_flat, pos_flat, h, route, g_row, b_row, outs)


def _row_tile(m, pref):
    t = min(m, pref)
    while m % t:
        t //= 2
    return t


def _route_tables(route, counts_row, m, tm):
    n_assign = m * TOP_K
    e_tok = route[:, :TOP_K].astype(jnp.int32)
    rank = route[:, 2 * TOP_K:3 * TOP_K].astype(jnp.int32)
    counts = counts_row[0, :N_EXPERTS].astype(jnp.int32)
    padded = (counts + tm - 1) // tm * tm
    pends = jnp.cumsum(padded)
    pstarts = pends - padded
    pos = pstarts[e_tok] + rank
    n_tiles = -(-n_assign // tm) + N_EXPERTS
    tile_start = jnp.arange(n_tiles, dtype=jnp.int32) * tm
    tile_expert = jnp.minimum(jnp.sum((pends[None, :] <= tile_start[:, None]).astype(jnp.int32), axis=1),
                              N_EXPERTS - 1)
    n_used = (pends[-1] // tm).astype(jnp.int32).reshape(1)
    last_used = jnp.maximum(n_used[0] - 1, 0)
    tile_expert = jnp.where(tile_start // tm < n_used[0], tile_expert, tile_expert[last_used])
    return pos.reshape(n_assign), tile_expert, n_used


def _layer(x, conv_hist, s_gdn, h_re, h_im, chunk, alpha, p):
    b, t, d = x.shape
    m = b * t
    d_gdn = GDN_HEADS * HEAD_DIM
    x2 = x.reshape(m, d)

    proj = _in_proj(x2, p['w_in'], _row_tile(m, 1024), p['proj_tn'])
    proj3 = proj.reshape(b, t, proj.shape[1])
    conv_new = proj3[:, t - (CONV_W - 1):, :3 * d_gdn]

    q, k, v, cols, gct = _gdn_prep(proj3, conv_hist, p['w_conv'], p['alog_row'], p['dtb_row'],
                                   chunk, p['ab_block'])
    o_gdn, s_new = _gdn(q, k, v, cols, gct, proj3, p['wn_row'], s_gdn, chunk, p['z_block'])

    n_gb = p['s5_mats'][0].shape[0]
    half = GROUPS_PER_BLOCK * SSM_P
    yg, hre_new, him_new = _s5(proj3, p['s5_mats'], h_re.reshape(b, n_gb, 1, half),
                               h_im.reshape(b, n_gb, 1, half), p['u_block0'])
    o_ssm = _glu(yg.reshape(m, yg.shape[2]), p['w_glu'], p['b_glu_row'], _row_tile(m, 512))

    h, hb, route, counts_row = _mix_ln_route(o_gdn.reshape(m, d_gdn), o_ssm, p['w_out_a'], p['w_out_b'], x2,
                                             p['ln1_g'], p['ln1_b'], p['w_router'], p['b_router'], alpha,
                                             _row_tile(m, 256))

    big = m * TOP_K >= 8 * p['moe_tm']
    tm = p['moe_tm'] if big else 128
    pos_flat, tile_expert, n_used = _route_tables(route, counts_row, m, tm)
    cap = tile_expert.shape[0] * tm
    x_rows = _dispatch_rows(pos_flat, hb, cap, _row_tile(m, 256))
    outs = _moe_ffn(tile_expert, n_used, x_rows, p['w_gate'], p['b_gate'],
                    p['w_up'], p['b_up'], p['w_down'], p['b_down'], tm, p['moe_tf'])
    y = _combine_ln(pos_flat, h, route, p['ln2_g'], p['ln2_b'], outs, alpha, _row_tile(m, 128))
    g_all = h_re.shape[1]
    return (y.reshape(b, t, d), conv_new, s_new,
            hre_new.reshape(b, g_all, SSM_P), him_new.reshape(b, g_all, SSM_P))


def _pad_lanes(v, fill=0.0):
    return jnp.pad(v.astype(F32), (0, LANES - v.shape[0]), constant_values=fill).reshape(1, LANES)


def _layer_params(l, w_in, w_conv, a_log, dt_bias, w_onorm, lam_re, lam_im, log_dt, b_re, b_im, c_re, c_im,
                  d_skip, w_glu, b_glu, w_out, ln1_g, ln1_b, w_router, b_router, w_gate, b_gate,
                  w_up, b_up, w_down, b_down, ln2_g, ln2_b):
    d_model = w_in.shape[1]
    d_gdn = GDN_HEADS * HEAD_DIM
    d_qkvz = 4 * d_gdn
    d_ssm = d_model - d_gdn
    wi = w_in[l]
    proj_tn = 1792
    n_cols = d_qkvz + d_ssm + LANES
    n_pad = -(-n_cols // proj_tn) * proj_tn
    w_in_r = jnp.concatenate([wi[:, :d_qkvz], wi[:, d_qkvz + 2 * GDN_HEADS:],
                              wi[:, d_qkvz:d_qkvz + 2 * GDN_HEADS],
                              jnp.zeros((d_model, n_pad - d_qkvz - d_ssm - 2 * GDN_HEADS), wi.dtype)], axis=1)
    wo = w_out[l].astype(BF16)
    wr = jnp.pad(w_router[l].astype(F32), ((0, 0), (0, LANES - N_EXPERTS)))
    return {
        'w_in': w_in_r.astype(BF16), 'proj_tn': proj_tn,
        'z_block': 3, 'u_block0': (d_qkvz) // LANES, 'ab_block': (d_qkvz + d_ssm) // LANES,
        'w_conv': w_conv[l].astype(F32),
        'alog_row': _pad_lanes(a_log[l]), 'dtb_row': _pad_lanes(dt_bias[l]),
        'wn_row': w_onorm[l].astype(F32).reshape(1, HEAD_DIM),
        's5_mats': _s5_matrices(lam_re[l], lam_im[l], log_dt[l], b_re[l], b_im[l], c_re[l], c_im[l], d_skip[l]),
        'w_glu': w_glu[l].astype(BF16), 'b_glu_row': b_glu[l].astype(F32).reshape(1, d_ssm),
        'w_out_a': wo[:d_gdn], 'w_out_b': wo[d_gdn:],
        'ln1_g': ln1_g[l].astype(F32).reshape(1, d_model), 'ln1_b': ln1_b[l].astype(F32).reshape(1, d_model),
        'w_router': wr, 'b_router': _pad_lanes(b_router[l]),
        'w_gate': w_gate[l], 'b_gate': b_gate[l], 'w_up': w_up[l], 'b_up': b_up[l],
        'w_down': w_down[l], 'b_down': b_down[l],
        'ln2_g': ln2_g[l].astype(F32).reshape(1, d_model), 'ln2_b': ln2_b[l].astype(F32).reshape(1, d_model),
        'moe_tm': 512, 'moe_tf': 512,
    }


def kernel(x_prompt, x_sample, state_conv, state_gdn, state_ssm_re, state_ssm_im, w_in, w_conv, a_log, dt_bias, w_onorm, lam_re, lam_im, log_dt, b_re, b_im, c_re, c_im, d_skip, w_glu, b_glu, w_out, ln1_g, ln1_b, w_router, b_router, w_gate, b_gate, w_up, b_up, w_down, b_down, ln2_g, ln2_b):
    depth = w_in.shape[0]
    alpha = (2.0 * depth) ** 0.25
    bp, seq, _ = x_prompt.shape
    chunk_p = 64
    d_qkv = state_conv.shape[-1]
    n_groups, n_p = state_ssm_re.shape[-2:]
    yp, ys = x_prompt, x_sample
    outs_p = [[], [], [], []]
    outs_s = [[], [], [], []]
    for l in range(depth):
        p = _layer_params(l, w_in, w_conv, a_log, dt_bias, w_onorm, lam_re, lam_im, log_dt, b_re, b_im,
                          c_re, c_im, d_skip, w_glu, b_glu, w_out, ln1_g, ln1_b, w_router, b_router,
                          w_gate, b_gate, w_up, b_up, w_down, b_down, ln2_g, ln2_b)
        yp, cp, sp, rp, ip = _layer(
            yp, jnp.zeros((bp, CONV_W - 1, d_qkv), F32),
            jnp.zeros((bp, GDN_HEADS, HEAD_DIM, HEAD_DIM), F32),
            jnp.zeros((bp, n_groups, n_p), F32), jnp.zeros((bp, n_groups, n_p), F32),
            chunk_p, alpha, p)
        ys, cs, ss, rs, is_ = _layer(
            ys, state_conv[l].astype(F32), state_gdn[l].astype(F32),
            state_ssm_re[l].astype(F32), state_ssm_im[l].astype(F32),
            ys.shape[1], alpha, p)
        for acc, val in zip(outs_p, (cp, sp, rp, ip)):
            acc.append(val)
        for acc, val in zip(outs_s, (cs, ss, rs, is_)):
            acc.append(val)
    return (yp, ys, *[jnp.stack(a) for a in outs_p], *[jnp.stack(a) for a in outs_s])
```

```python
import functools
import math

import jax
import jax.numpy as jnp
from jax import lax
from jax.experimental import pallas as pl
from jax.experimental.pallas import tpu as pltpu

F32 = jnp.float32
BF16 = jnp.bfloat16

GDN_HEADS = 8
HEAD_DIM = 128
CONV_W = 4
SSM_CG = 16
SSM_P = 64
N_EXPERTS = 32
TOP_K = 4
SWIGLU_ALPHA = 1.702
SWIGLU_LIMIT = 7.0
LN_EPS = 1e-5
RMS_EPS = 1e-6
L2_EPS = 1e-6

LANES = 128
SUBLANES = 8
S5_SUB = 8
S5_TILE = 2048
GDN_TILE = 128
GROUPS_PER_BLOCK = LANES // SSM_CG
VMEM_LIMIT = 56 * 1024 * 1024

NT_DIMS = (((1,), (1,)), ((), ()))
TN_DIMS = (((0,), (0,)), ((), ()))


def _dot(a, b, dims=(((1,), (0,)), ((), ()))):
    return lax.dot_general(a, b, dims, preferred_element_type=F32)


def _split(a):
    hi = a.astype(BF16)
    lo = (a - hi.astype(F32)).astype(BF16)
    return hi, lo


def _dot_x3(a, b, dims=(((1,), (0,)), ((), ()))):
    ah, al = _split(a)
    bh, bl = _split(b)
    return _dot(ah, bh, dims) + (_dot(ah, bl, dims) + _dot(al, bh, dims))


def _dot_bf(a, b, dims=(((1,), (0,)), ((), ()))):
    return _dot(a.astype(BF16), b.astype(BF16), dims)


_gdn_mm = _dot_bf


def _sigmoid(x):
    return 1.0 / (1.0 + jnp.exp(-x))


ROW_SUB = SUBLANES


def _pack_rows(x, ref, tm):
    half = x.shape[1] // 2
    for s in range(ROW_SUB):
        lo = x[:, s * LANES:(s + 1) * LANES].astype(BF16).astype(F32)
        hi = x[:, half + s * LANES:half + (s + 1) * LANES].astype(BF16).astype(F32)
        word = (lax.bitcast_convert_type(lo, jnp.uint32) >> 16) | lax.bitcast_convert_type(hi, jnp.uint32)
        ref[pl.ds(s, tm, stride=ROW_SUB), :] = word


def _unpack_rows(ref, base, tm):
    lo, hi = [], []
    for s in range(ROW_SUB):
        word = ref[pl.ds(base + s, tm, stride=ROW_SUB), :]
        lo.append(lax.bitcast_convert_type(word << 16, F32))
        hi.append(lax.bitcast_convert_type(word & jnp.uint32(0xFFFF0000), F32))
    return lo, hi


def _cparams(sem):
    return pltpu.CompilerParams(dimension_semantics=sem, vmem_limit_bytes=VMEM_LIMIT)


def _proj_body(x_ref, w_ref, o_ref, xb_ref):
    @pl.when(pl.program_id(1) == 0)
    def _():
        xb_ref[...] = x_ref[...].astype(BF16)

    o_ref[...] = _dot(xb_ref[...], w_ref[...])


def _in_proj(x2, w_bf, tm, tn):
    m, k = x2.shape
    n = w_bf.shape[1]
    return pl.pallas_call(
        _proj_body,
        grid=(m // tm, n // tn),
        in_specs=[pl.BlockSpec((tm, k), lambda i, j: (i, 0)),
                  pl.BlockSpec((k, tn), lambda i, j: (0, j))],
        out_specs=pl.BlockSpec((tm, tn), lambda i, j: (i, j)),
        out_shape=jax.ShapeDtypeStruct((m, n), F32),
        scratch_shapes=[pltpu.VMEM((tm, k), BF16)],
        compiler_params=_cparams(("parallel", "arbitrary")),
        name="in_proj",
    )(x2, w_bf)


def _gdn_prep_body(qkv_ref, ab_ref, hist_ref, wc_ref, alog_ref, dtb_ref,
                   q_ref, k_ref, v_ref, cols_ref, gct_ref, xbuf_ref, *, tt, chunk):
    d_gdn = GDN_HEADS * HEAD_DIM
    halo = SUBLANES

    @pl.when(pl.program_id(1) == 0)
    def _():
        xbuf_ref[0:halo, :] = jnp.zeros((halo, 3 * d_gdn), F32)
        xbuf_ref[halo - (CONV_W - 1):halo, :] = hist_ref[0]

    xbuf_ref[halo:halo + tt, :] = qkv_ref[0]

    for part, out_ref in enumerate((q_ref, k_ref, v_ref)):
        c0 = part * d_gdn
        y = None
        for j in range(CONV_W):
            r0 = halo - (CONV_W - 1) + j
            term = xbuf_ref[r0:r0 + tt, c0:c0 + d_gdn] * wc_ref[j:j + 1, c0:c0 + d_gdn]
            y = term if y is None else y + term
        s = y * _sigmoid(y)
        if part == 2:
            out_ref[0] = s
        else:
            for h in range(GDN_HEADS):
                sh = s[:, h * HEAD_DIM:(h + 1) * HEAD_DIM]
                nrm = sh * lax.rsqrt(jnp.sum(sh * sh, axis=-1, keepdims=True) + L2_EPS)
                if part == 0:
                    nrm = nrm * (HEAD_DIM ** -0.5)
                out_ref[0, :, h * HEAD_DIM:(h + 1) * HEAD_DIM] = nrm

    xbuf_ref[0:halo, :] = xbuf_ref[tt:tt + halo, :]

    ab = ab_ref[0]
    lane = lax.broadcasted_iota(jnp.int32, (tt, LANES), 1)
    is_a = lane < GDN_HEADS
    z = ab + dtb_ref[...]
    softplus = jnp.maximum(z, 0.0) + jnp.log1p(jnp.exp(-jnp.abs(z)))
    g = jnp.where(is_a, -jnp.exp(alog_ref[...]) * softplus, 0.0)
    beta = _sigmoid(ab)

    shift = int(math.log2(chunk))
    r = lax.broadcasted_iota(jnp.int32, (tt, tt), 0)
    c = lax.broadcasted_iota(jnp.int32, (tt, tt), 1)
    same = (r >> shift) == (c >> shift)
    m_incl = jnp.where(same & (r >= c), 1.0, 0.0).astype(BF16)
    m_all = jnp.where(same, 1.0, 0.0).astype(BF16)
    gc = _dot_exact_lhs_rhs(m_incl, g)
    glast = _dot_exact_lhs_rhs(m_all, g)
    eg = jnp.exp(gc)
    egl = jnp.exp(glast - gc)
    egt = jnp.exp(glast)
    zero = jnp.zeros_like(gc)
    cols = (jnp.where(is_a, gc, zero)
            + jnp.where((lane >= 8) & (lane < 16), beta, zero)
            + pltpu.roll(jnp.where(is_a, eg, zero), 16, 1)
            + pltpu.roll(jnp.where(is_a, egl, zero), 24, 1)
            + pltpu.roll(jnp.where(is_a, egt, zero), 32, 1))
    cols_ref[0] = cols

    er = lax.broadcasted_iota(jnp.int32, (SUBLANES, LANES), 0)
    ec = lax.broadcasted_iota(jnp.int32, (SUBLANES, LANES), 1)
    sel = jnp.where(er == ec, 1.0, 0.0).astype(BF16)
    gct_ref[0] = _dot_exact_rhs(sel, jnp.where(is_a, gc, zero), NT_DIMS)


def _three_pieces(a):
    a0 = a.astype(BF16)
    r1 = a - a0.astype(F32)
    a1 = r1.astype(BF16)
    a2 = (r1 - a1.astype(F32)).astype(BF16)
    return a0, a1, a2


def _dot_exact_lhs_rhs(mask_bf, a):
    a0, a1, a2 = _three_pieces(a)
    return _dot(mask_bf, a0) + (_dot(mask_bf, a1) + _dot(mask_bf, a2))


def _dot_exact_rhs(mask_bf, a, dims):
    a0, a1, a2 = _three_pieces(a)
    return _dot(mask_bf, a0, dims) + (_dot(mask_bf, a1, dims) + _dot(mask_bf, a2, dims))


def _gdn_prep(proj3, hist, w_conv, alog_p, dtb_p, chunk, ab_block):
    b, t, _ = proj3.shape
    d_gdn = GDN_HEADS * HEAD_DIM
    tt = min(t, GDN_TILE)
    body = functools.partial(_gdn_prep_body, tt=tt, chunk=chunk)
    big = jax.ShapeDtypeStruct((b, t, d_gdn), F32)
    return pl.pallas_call(
        body,
        grid=(b, t // tt),
        in_specs=[pl.BlockSpec((1, tt, 3 * d_gdn), lambda i, j: (i, j, 0)),
                  pl.BlockSpec((1, tt, LANES), lambda i, j: (i, j, ab_block)),
                  pl.BlockSpec((1, CONV_W - 1, 3 * d_gdn), lambda i, j: (i, 0, 0)),
                  pl.BlockSpec((CONV_W, 3 * d_gdn), lambda i, j: (0, 0)),
                  pl.BlockSpec((1, LANES), lambda i, j: (0, 0)),
                  pl.BlockSpec((1, LANES), lambda i, j: (0, 0))],
        out_specs=[pl.BlockSpec((1, tt, d_gdn), lambda i, j: (i, j, 0)),
                   pl.BlockSpec((1, tt, d_gdn), lambda i, j: (i, j, 0)),
                   pl.BlockSpec((1, tt, d_gdn), lambda i, j: (i, j, 0)),
                   pl.BlockSpec((1, tt, LANES), lambda i, j: (i, j, 0)),
                   pl.BlockSpec((1, SUBLANES, tt), lambda i, j: (i, 0, j))],
        out_shape=[big, big, big,
                   jax.ShapeDtypeStruct((b, t, LANES), F32),
                   jax.ShapeDtypeStruct((b, SUBLANES, t), F32)],
        scratch_shapes=[pltpu.VMEM((tt + SUBLANES, 3 * d_gdn), F32)],
        compiler_params=_cparams(("parallel", "arbitrary")),
        name="gdn_prep",
    )(proj3, proj3, hist, w_conv, alog_p, dtb_p)


def _unit_lower_inverses(mats, r, c, chunk):
    base = 16
    eye = jnp.where(r == c, 1.0, 0.0)

    def blk(bs):
        s = int(math.log2(bs))
        return (r >> s) == (c >> s)

    d1 = [jnp.where(blk(base), a, 0.0) for a in mats]
    d2 = [_gdn_mm(x, x) for x in d1]
    d4 = [_gdn_mm(x, x) for x in d2]
    d8 = [_gdn_mm(x, x) for x in d4]
    t = [eye - x for x in d1]
    t = [x + _gdn_mm(x, y) for x, y in zip(t, d2)]
    t = [x + _gdn_mm(x, y) for x, y in zip(t, d4)]
    t = [x + _gdn_mm(x, y) for x, y in zip(t, d8)]
    bs = base
    while bs < chunk:
        off_mask = blk(2 * bs) & jnp.logical_not(blk(bs))
        inner = [_gdn_mm(jnp.where(off_mask, a, 0.0), x) for a, x in zip(mats, t)]
        t = [x - _gdn_mm(x, y) for x, y in zip(t, inner)]
        bs *= 2
    return t


def _gdn_body(q_ref, k_ref, v_ref, cols_ref, gct_ref, z_ref, wn_ref, s0_ref,
              o_ref, s_ref, *, tb, chunk):
    @pl.when(pl.program_id(1) == 0)
    def _():
        s_ref[...] = s0_ref[...]

    shift = int(math.log2(chunk))
    r = lax.broadcasted_iota(jnp.int32, (tb, tb), 0)
    c = lax.broadcasted_iota(jnp.int32, (tb, tb), 1)
    same = (r >> shift) == (c >> shift)
    incl = same & (r >= c)
    strict = same & (r > c)
    n_chunks = tb // chunk

    heads = range(GDN_HEADS)
    hsl = [slice(h * HEAD_DIM, (h + 1) * HEAD_DIM) for h in heads]
    q = [q_ref[0, :, s] for s in hsl]
    k = [k_ref[0, :, s] for s in hsl]
    beta = [cols_ref[0, :, 8 + h:9 + h] for h in heads]
    eg = [cols_ref[0, :, 16 + h:17 + h] for h in heads]
    egl = [cols_ref[0, :, 24 + h:25 + h] for h in heads]
    decay = [jnp.exp(jnp.where(incl, cols_ref[0, :, h:h + 1] - gct_ref[0, h:h + 1, :], -jnp.inf)) for h in heads]
    kb = [k[h] * beta[h] for h in heads]
    k_bf = [x.astype(BF16) for x in k]
    a = [jnp.where(strict, _dot(kb[h].astype(BF16), k_bf[h], NT_DIMS) * decay[h], 0.0) for h in heads]
    tinv = _unit_lower_inverses(a, r, c, chunk)
    sol = [_gdn_mm(tinv[h], jnp.concatenate([v_ref[0, :, hsl[h]] * beta[h], kb[h] * eg[h]], axis=1)) for h in heads]
    u = [x[:, :HEAD_DIM] for x in sol]
    w_bf = [x[:, HEAD_DIM:].astype(BF16) for x in sol]
    attn = [(_dot(q[h].astype(BF16), k_bf[h], NT_DIMS) * decay[h]).astype(BF16) for h in heads]
    q_dec = [(q[h] * eg[h]).astype(BF16) for h in heads]
    k_dec = [(k[h] * egl[h]).astype(BF16) for h in heads]

    s = [s_ref[0, h] for h in heads]
    v_new = [[] for _ in heads]
    o_state = [[] for _ in heads]
    for ci in range(n_chunks):
        rs = slice(ci * chunk, (ci + 1) * chunk)
        s_bf = [x.astype(BF16) for x in s]
        vn = [u[h][rs] - _dot(w_bf[h][rs], s_bf[h]) for h in heads]
        for h in heads:
            o_state[h].append(_dot(q_dec[h][rs], s_bf[h]))
            v_new[h].append(vn[h])
        s = [s[h] * cols_ref[0, ci * chunk:ci * chunk + 1, 32 + h:33 + h]
             + _dot(k_dec[h][rs], vn[h].astype(BF16), TN_DIMS) for h in heads]
    for h in heads:
        s_ref[0, h] = s[h]

    def cat(parts):
        return parts[0] if len(parts) == 1 else jnp.concatenate(parts, axis=0)

    for h in heads:
        o = cat(o_state[h]) + _dot(attn[h], cat(v_new[h]).astype(BF16))
        zh = z_ref[0, :, hsl[h]]
        o = (o * lax.rsqrt(jnp.mean(o * o, axis=-1, keepdims=True) + RMS_EPS) * wn_ref[...]
             * (zh * _sigmoid(zh)))
        o_ref[0, :, hsl[h]] = o.astype(BF16)


def _gdn(q, k, v, cols, gct, proj3, wn, s0, chunk, z_block):
    b, t, d_gdn = q.shape
    tb = min(t, GDN_TILE)
    body = functools.partial(_gdn_body, tb=tb, chunk=chunk)
    tile = lambda i, j: (i, j, 0)
    return pl.pallas_call(
        body,
        grid=(b, t // tb),
        in_specs=[pl.BlockSpec((1, tb, d_gdn), tile),
                  pl.BlockSpec((1, tb, d_gdn), tile),
                  pl.BlockSpec((1, tb, d_gdn), tile),
                  pl.BlockSpec((1, tb, LANES), tile),
                  pl.BlockSpec((1, SUBLANES, tb), lambda i, j: (i, 0, j)),
                  pl.BlockSpec((1, tb, d_gdn), lambda i, j: (i, j, z_block)),
                  pl.BlockSpec((1, HEAD_DIM), lambda i, j: (0, 0)),
                  pl.BlockSpec((1, GDN_HEADS, HEAD_DIM, HEAD_DIM), lambda i, j: (i, 0, 0, 0))],
        out_specs=[pl.BlockSpec((1, tb, d_gdn), tile),
                   pl.BlockSpec((1, GDN_HEADS, HEAD_DIM, HEAD_DIM), lambda i, j: (i, 0, 0, 0))],
        out_shape=[jax.ShapeDtypeStruct((b, t, d_gdn), BF16),
                   jax.ShapeDtypeStruct((b, GDN_HEADS, HEAD_DIM, HEAD_DIM), F32)],
        compiler_params=_cparams(("parallel", "arbitrary")),
        name="gdn_delta",
    )(q, k, v, cols, gct, proj3, wn, s0)


def _gelu_tanh(x):
    return 0.5 * x * (1.0 + jnp.tanh(math.sqrt(2.0 / math.pi) * (x + 0.044715 * (x * x * x))))


def _s5_body(u_ref, wm_ref, ym0_ref, ym1_ref, ym2_ref, ym3_ref, lre_ref, lim_ref, dsk_ref, h0re_ref, h0im_ref,
             y_ref, hre_ref, him_ref, xbuf_ref, hbuf_ref, *, n_sub, seg):
    half = GROUPS_PER_BLOCK * SSM_P

    if seg is None:
        @pl.when(pl.program_id(2) == 0)
        def _():
            hre_ref[...] = h0re_ref[...]
            him_ref[...] = h0im_ref[...]

    u_f = [u_ref[0, pl.ds(j, n_sub, stride=S5_SUB), :] for j in range(S5_SUB)]
    u_b = [x.astype(BF16) for x in u_f]

    xbuf_ref[...] = _dot(jnp.concatenate(u_b, axis=1), wm_ref[0])

    lre = lre_ref[0]
    lim = lim_ref[0]

    def advance(n, hre, him):
        hbuf_ref[pl.ds(n, 1), 0:half] = hre
        hbuf_ref[pl.ds(n, 1), half:2 * half] = him
        xr = xbuf_ref[pl.ds(n, 1), 0:half]
        xi = xbuf_ref[pl.ds(n, 1), half:2 * half]
        return (lre * hre - lim * him + xr, lre * him + lim * hre + xi)

    if seg is None:
        hre, him = lax.fori_loop(0, n_sub, lambda n, c: advance(n, *c), (hre_ref[0, 0], him_ref[0, 0]))
        hre_ref[0, 0] = hre
        him_ref[0, 0] = him
    else:
        def step(n, carry):
            q = n // seg
            first = n - q * seg == 0
            hre = jnp.where(first, h0re_ref[q, 0], carry[0])
            him = jnp.where(first, h0im_ref[q, 0], carry[1])
            hre, him = advance(n, hre, him)
            hre_ref[q, 0] = hre
            him_ref[q, 0] = him
            return hre, him

        lax.fori_loop(0, n_sub, step, (h0re_ref[0, 0], h0im_ref[0, 0]))

    h_b = hbuf_ref[...].astype(BF16)
    dsk = dsk_ref[0]
    for l, ym_ref in zip(range(0, S5_SUB, 2), (ym0_ref, ym1_ref, ym2_ref, ym3_ref)):
        lhs = jnp.concatenate([h_b] + [u_b[m] for m in range(l + 1, -1, -1)], axis=1)
        y2 = _dot(lhs, ym_ref[0])
        for o in range(2):
            y = y2[:, o * LANES:(o + 1) * LANES] + dsk * u_f[l + o]
            y_ref[0, pl.ds(l + o, n_sub, stride=S5_SUB), :] = _gelu_tanh(y)


def _s5(proj3, mats, h0re, h0im, u_block0):
    b, t, width = proj3.shape
    n_gb = mats[0].shape[0]
    if b > 1 and b * t <= S5_TILE:
        y, hre, him = _s5_call(proj3.reshape(1, b * t, width), mats, h0re, h0im, u_block0,
                               tt=b * t, seg=t // S5_SUB, state_rows=b)
        return y.reshape(b, t, n_gb * LANES), hre, him
    return _s5_call(proj3, mats, h0re, h0im, u_block0, tt=min(t, S5_TILE), seg=None, state_rows=1)


def _s5_call(proj3, mats, h0re, h0im, u_block0, tt, seg, state_rows):
    wcat, ycat, lre, lim, dsk = mats
    b, t, _ = proj3.shape
    n_gb = wcat.shape[0]
    n_sub = tt // S5_SUB
    half = GROUPS_PER_BLOCK * SSM_P
    body = functools.partial(_s5_body, n_sub=n_sub, seg=seg)
    state_spec = pl.BlockSpec((state_rows, 1, 1, half), lambda g, i, j: (i, g, 0, 0))
    par_spec = pl.BlockSpec((1, 1, half), lambda g, i, j: (g, 0, 0))
    mat_spec = lambda a: pl.BlockSpec((1,) + a.shape[1:], lambda g, i, j: (g, 0, 0))
    return pl.pallas_call(
        body,
        grid=(n_gb, b, t // tt),
        in_specs=[pl.BlockSpec((1, tt, LANES), lambda g, i, j: (i, j, u_block0 + g)),
                  mat_spec(wcat)] + [mat_spec(a) for a in ycat] + [
                  par_spec, par_spec,
                  pl.BlockSpec((1, 1, LANES), lambda g, i, j: (g, 0, 0)),
                  state_spec, state_spec],
        out_specs=[pl.BlockSpec((1, tt, LANES), lambda g, i, j: (i, j, g)),
                   state_spec, state_spec],
        out_shape=[jax.ShapeDtypeStruct((b, t, n_gb * LANES), F32),
                   jax.ShapeDtypeStruct(h0re.shape, F32),
                   jax.ShapeDtypeStruct(h0im.shape, F32)],
        scratch_shapes=[pltpu.VMEM((n_sub, 2 * half), F32),
                        pltpu.VMEM((n_sub, 2 * half), F32)],
        compiler_params=_cparams(("parallel", "parallel", "arbitrary")),
        name="s5_scan",
    )(proj3, wcat, *ycat, lre, lim, dsk, h0re, h0im)


def _s5_matrices(lam_re, lam_im, log_dt, b_re, b_im, c_re, c_im, d_skip):
    g, p = lam_re.shape
    n_gb = g // GROUPS_PER_BLOCK
    gpb = GROUPS_PER_BLOCK
    dt = jnp.exp(log_dt.astype(F32))
    lam = lax.complex(jnp.minimum(lam_re.astype(F32), -1e-4), lam_im.astype(F32))
    lam_bar = jnp.exp(lam * dt[:, None])
    b_bar = ((lam_bar - 1.0) / lam)[..., None] * lax.complex(b_re.astype(F32), b_im.astype(F32))
    c_c = lax.complex(c_re.astype(F32), c_im.astype(F32))
    pows = [jnp.ones_like(lam_bar)]
    for _ in range(S5_SUB):
        pows.append(pows[-1] * lam_bar)
    pw = jnp.stack(pows)
    eye = jnp.eye(gpb, dtype=F32)

    kd = jnp.real(jnp.einsum('gop,dgp,gpi->dgio', c_c, pw[:S5_SUB], b_bar))
    km = jnp.einsum('dbgio,gh->dbgiho', kd.reshape(S5_SUB, n_gb, gpb, SSM_CG, SSM_CG), eye)
    km = km.reshape(S5_SUB, n_gb, LANES, LANES)

    wj = pw[:S5_SUB][::-1][:, :, :, None] * b_bar[None]
    wj = jnp.transpose(wj, (0, 1, 3, 2)).reshape(S5_SUB, n_gb, gpb, SSM_CG, p)
    wre = jnp.einsum('dbgcp,gh->dbgchp', jnp.real(wj), eye).reshape(S5_SUB, n_gb, LANES, gpb * p)
    wim = jnp.einsum('dbgcp,gh->dbgchp', jnp.imag(wj), eye).reshape(S5_SUB, n_gb, LANES, gpb * p)
    wm = jnp.concatenate([wre, wim], axis=-1)

    cl = c_c[None] * pw[1:S5_SUB + 1][:, :, None, :]
    cl = jnp.transpose(cl, (0, 1, 3, 2)).reshape(S5_SUB, n_gb, gpb, p, SSM_CG)
    vre = jnp.einsum('dbgpc,gh->dbgphc', jnp.real(cl), eye).reshape(S5_SUB, n_gb, gpb * p, LANES)
    vim = jnp.einsum('dbgpc,gh->dbgphc', -jnp.imag(cl), eye).reshape(S5_SUB, n_gb, gpb * p, LANES)
    vm = jnp.concatenate([vre, vim], axis=-2)

    lam_s = pw[S5_SUB].reshape(n_gb, 1, gpb * p)
    dsk = d_skip.astype(F32).reshape(n_gb, 1, LANES)

    km, wm, vm = km.astype(BF16), wm.astype(BF16), vm.astype(BF16)
    wcat = jnp.transpose(wm, (1, 0, 2, 3)).reshape(n_gb, S5_SUB * LANES, 2 * gpb * p)
    ycat = []
    zero = jnp.zeros_like(km[0])
    for l in range(0, S5_SUB, 2):
        rows = [jnp.concatenate([vm[l], vm[l + 1]], axis=-1)]
        for m in range(l + 1, -1, -1):
            left = km[l - m] if m <= l else zero
            rows.append(jnp.concatenate([left, km[l + 1 - m]], axis=-1))
        ycat.append(jnp.concatenate(rows, axis=-2))
    return (wcat, tuple(ycat), jnp.real(lam_s), jnp.imag(lam_s), dsk)


def _glu_body(y_ref, w_ref, b_ref, o_ref):
    y = y_ref[...]
    gate = _dot(y.astype(BF16), w_ref[...]) + b_ref[...]
    o_ref[...] = (y * _sigmoid(gate)).astype(BF16)


def _glu(y2, w_bf, b_row, tm):
    m, d = y2.shape
    return pl.pallas_call(
        _glu_body,
        grid=(m // tm,),
        in_specs=[pl.BlockSpec((tm, d), lambda i: (i, 0)),
                  pl.BlockSpec((d, d), lambda i: (0, 0)),
                  pl.BlockSpec((1, d), lambda i: (0, 0))],
        out_specs=pl.BlockSpec((tm, d), lambda i: (i, 0)),
        out_shape=jax.ShapeDtypeStruct((m, d), BF16),
        compiler_params=_cparams(("parallel",)),
        name="s5_glu",
    )(y2, w_bf, b_row)


def _layernorm(v, g, b):
    mu = jnp.mean(v, axis=-1, keepdims=True)
    var = jnp.mean(jnp.square(v - mu), axis=-1, keepdims=True)
    return (v - mu) * lax.rsqrt(var + LN_EPS) * g + b


def _mix_body(og_ref, os_ref, wa_ref, wb_ref, x_ref, g_ref, b_ref, wr_ref, br_ref,
              h_ref, hb_ref, route_ref, cnt_ref, *, alpha):
    @pl.when(pl.program_id(0) == 0)
    def _():
        cnt_ref[...] = jnp.zeros_like(cnt_ref)

    mix = _dot(og_ref[...], wa_ref[...]) + _dot(os_ref[...], wb_ref[...])
    h = _layernorm(alpha * x_ref[...] + mix, g_ref[...], b_ref[...])
    h_ref[...] = h
    tm = h.shape[0]
    _pack_rows(h, hb_ref, tm)

    logits = _dot_x3(h, wr_ref[...]) + br_ref[...]
    lane = lax.broadcasted_iota(jnp.int32, (tm, LANES), 1)
    work = jnp.where(lane < N_EXPERTS, logits, -jnp.inf)
    vals, idxs = [], []
    for _ in range(TOP_K):
        mx = jnp.max(work, axis=-1, keepdims=True)
        ix = jnp.min(jnp.where(work == mx, lane, LANES), axis=-1, keepdims=True)
        vals.append(mx)
        idxs.append(ix)
        work = jnp.where(lane == ix, -jnp.inf, work)
    exps = [jnp.exp(v - vals[0]) for v in vals]
    denom = exps[0]
    for e in exps[1:]:
        denom = denom + e
    chosen = jnp.zeros((tm, LANES), F32)
    for k in range(TOP_K):
        chosen = jnp.where(lane == idxs[k], 1.0, chosen)
    rr = lax.broadcasted_iota(jnp.int32, (tm, tm), 0)
    cc = lax.broadcasted_iota(jnp.int32, (tm, tm), 1)
    lower = jnp.where(rr > cc, 1.0, 0.0).astype(BF16)
    before = cnt_ref[...] + _dot(lower, chosen.astype(BF16))
    cnt_ref[...] = cnt_ref[...] + jnp.sum(chosen, axis=0, keepdims=True)

    route = jnp.zeros((tm, LANES), F32)
    for k in range(TOP_K):
        rank = jnp.sum(jnp.where(lane == idxs[k], before, 0.0), axis=-1, keepdims=True)
        route = jnp.where(lane == k, idxs[k].astype(F32), route)
        route = jnp.where(lane == TOP_K + k, exps[k] / denom, route)
        route = jnp.where(lane == 2 * TOP_K + k, rank, route)
    route_ref[...] = route


def _mix_ln_route(og, osm, wa, wb, x2, g_row, b_row, wr, br, alpha, tm):
    m, d = x2.shape
    dh = og.shape[1]
    body = functools.partial(_mix_body, alpha=alpha)
    row = lambda i: (i, 0)
    fix = lambda i: (0, 0)
    return pl.pallas_call(
        body,
        grid=(m // tm,),
        in_specs=[pl.BlockSpec((tm, dh), row), pl.BlockSpec((tm, dh), row),
                  pl.BlockSpec((dh, d), fix), pl.BlockSpec((dh, d), fix),
                  pl.BlockSpec((tm, d), row),
                  pl.BlockSpec((1, d), fix), pl.BlockSpec((1, d), fix),
                  pl.BlockSpec((d, LANES), fix), pl.BlockSpec((1, LANES), fix)],
        out_specs=[pl.BlockSpec((tm, d), row), pl.BlockSpec((tm * ROW_SUB, LANES), row),
                   pl.BlockSpec((tm, LANES), row), pl.BlockSpec((1, LANES), fix)],
        out_shape=[jax.ShapeDtypeStruct((m, d), F32),
                   jax.ShapeDtypeStruct((m * ROW_SUB, LANES), jnp.uint32),
                   jax.ShapeDtypeStruct((m, LANES), F32),
                   jax.ShapeDtypeStruct((1, LANES), F32)],
        compiler_params=_cparams(("arbitrary",)),
        name="mix_ln_route",
    )(og, osm, wa, wb, x2, g_row, b_row, wr, br)


def _dispatch_body(pos_ref, src_ref, init_ref, dst_ref, sem, *, tt):
    del init_ref

    def issue(t, carry):
        srow = pl.multiple_of(t * ROW_SUB, ROW_SUB)
        for k in range(TOP_K):
            drow = pl.multiple_of(pos_ref[t * TOP_K + k] * ROW_SUB, ROW_SUB)
            pltpu.make_async_copy(src_ref.at[pl.ds(srow, ROW_SUB)], dst_ref.at[pl.ds(drow, ROW_SUB)],
                                  sem).start(priority=k % 2)
        return carry

    lax.fori_loop(0, tt, issue, 0, unroll=4)
    n = tt * ROW_SUB
    for _ in range(TOP_K):
        pltpu.make_async_copy(src_ref.at[pl.ds(0, n)], dst_ref.at[pl.ds(0, n)], sem).wait()


def _dispatch_rows(pos_flat, hbp, cap, tt):
    m = hbp.shape[0] // ROW_SUB
    body = functools.partial(_dispatch_body, tt=tt)
    init = jnp.zeros((cap * ROW_SUB, LANES), jnp.uint32)
    return pl.pallas_call(
        body,
        grid=(m // tt,),
        in_specs=[pl.BlockSpec((tt * TOP_K,), lambda i: (i,), memory_space=pltpu.SMEM),
                  pl.BlockSpec((tt * ROW_SUB, LANES), lambda i: (i, 0)), pl.BlockSpec(memory_space=pl.ANY)],
        out_specs=pl.BlockSpec(memory_space=pl.ANY),
        out_shape=jax.ShapeDtypeStruct((cap * ROW_SUB, LANES), jnp.uint32),
        scratch_shapes=[pltpu.SemaphoreType.DMA(())],
        input_output_aliases={2: 0},
        compiler_params=_cparams(("arbitrary",)),
        name="moe_dispatch",
    )(pos_flat, hbp, init)


MOE_COL = 256
MOE_GROUP = 1


def _moe_body(te_ref, nu_ref, x_ref, wg_ref, bg_ref, wu_ref, bu_ref, wd_ref, bd_ref,
              o_ref, xb_ref, acc_ref, *, n_f, tm, tf):
    g = pl.program_id(2)
    f = pl.program_id(1)
    used = pl.program_id(0) * MOE_GROUP + g < nu_ref[0]
    half = xb_ref.shape[2] // 2
    xb_ref = xb_ref.at[g]
    acc_ref = acc_ref.at[g]

    @pl.when(used & (f == 0))
    def _():
        lo, hi = _unpack_rows(x_ref, 0, tm)
        for r in range(ROW_SUB):
            xb_ref[:, r * LANES:(r + 1) * LANES] = lo[r].astype(BF16)
            xb_ref[:, half + r * LANES:half + (r + 1) * LANES] = hi[r].astype(BF16)

    @pl.when(used)
    def _():
        xb = xb_ref[...]
        part = None
        for c in range(tf // MOE_COL):
            cs = slice(c * MOE_COL, (c + 1) * MOE_COL)
            hg = jnp.minimum(_dot(xb, wg_ref[0, :, cs].astype(BF16)) + bg_ref[0, :, cs], SWIGLU_LIMIT)
            hu = jnp.clip(_dot(xb, wu_ref[0, :, cs].astype(BF16)) + bu_ref[0, :, cs], -SWIGLU_LIMIT, SWIGLU_LIMIT)
            hh = ((hu + 1.0) * (hg * _sigmoid(SWIGLU_ALPHA * hg))).astype(BF16)
            p = _dot(hh, wd_ref[0, cs, :].astype(BF16))
            part = p if part is None else part + p

        @pl.when(f == 0)
        def _():
            acc_ref[...] = part

        @pl.when(f != 0)
        def _():
            acc_ref[...] += part

        @pl.when(f == n_f - 1)
        def _():
            _pack_rows(acc_ref[...] + bd_ref[0], o_ref, tm)

    @pl.when(jnp.logical_not(used) & (f == n_f - 1))
    def _():
        o_ref[...] = jnp.zeros_like(o_ref)


def _moe_ffn(tile_expert, n_used, x_rows, w_gate, b_gate, w_up, b_up, w_down, b_down, tm, tf):
    n_e, d, d_ff = w_gate.shape
    n_tiles = x_rows.shape[0] // (tm * ROW_SUB)
    n_f = d_ff // tf
    grp = MOE_GROUP
    assert n_tiles % grp == 0
    body = functools.partial(_moe_body, n_f=n_f, tm=tm, tf=tf)

    def tile(p, g):
        return p * grp + g

    def fcol(p, f, nu):
        return jnp.where(p * grp < nu[0], f, n_f - 1)

    def xrow(p, f, g):
        return jnp.where(f == 0, tile(p, g), tile(p, grp - 1))

    def orow(p, f, g):
        return jnp.where(f == n_f - 1, tile(p, g), tile(p, 0))

    grid_spec = pltpu.PrefetchScalarGridSpec(
        num_scalar_prefetch=2,
        grid=(n_tiles // grp, n_f, grp),
        in_specs=[pl.BlockSpec((tm * ROW_SUB, LANES), lambda p, f, g, te, nu: (xrow(p, f, g), 0)),
                  pl.BlockSpec((1, d, tf), lambda p, f, g, te, nu: (te[tile(p, g)], 0, fcol(p, f, nu))),
                  pl.BlockSpec((1, 1, tf), lambda p, f, g, te, nu: (te[tile(p, g)], 0, fcol(p, f, nu))),
                  pl.BlockSpec((1, d, tf), lambda p, f, g, te, nu: (te[tile(p, g)], 0, fcol(p, f, nu))),
                  pl.BlockSpec((1, 1, tf), lambda p, f, g, te, nu: (te[tile(p, g)], 0, fcol(p, f, nu))),
                  pl.BlockSpec((1, tf, d), lambda p, f, g, te, nu: (te[tile(p, g)], fcol(p, f, nu), 0)),
                  pl.BlockSpec((1, 1, d), lambda p, f, g, te, nu: (te[tile(p, g)], 0, 0))],
        out_specs=pl.BlockSpec((tm * ROW_SUB, LANES), lambda p, f, g, te, nu: (orow(p, f, g), 0)),
        scratch_shapes=[pltpu.VMEM((grp, tm, d), BF16), pltpu.VMEM((grp, tm, d), F32)],
    )
    return pl.pallas_call(
        body,
        grid_spec=grid_spec,
        out_shape=jax.ShapeDtypeStruct(x_rows.shape, jnp.uint32),
        compiler_params=_cparams(("arbitrary", "arbitrary", "arbitrary")),
        name="moe_ffn",
    )(tile_expert, n_used, x_rows, w_gate, b_gate.reshape(n_e, 1, d_ff),
      w_up, b_up.reshape(n_e, 1, d_ff), w_down, b_down.reshape(n_e, 1, d))


def _final_body(pos_cur_ref, pos_nxt_ref, h_ref, route_ref, g_ref, b_ref, src_ref,
                o_ref, buf_ref, sem, *, alpha, tt, n_steps):
    i = pl.program_id(0)
    slot_rows = tt * TOP_K * ROW_SUB

    def issue(pref, slot):
        def one(t, carry):
            for k in range(TOP_K):
                srow = pl.multiple_of(pref[t * TOP_K + k] * ROW_SUB, ROW_SUB)
                drow = pl.multiple_of(slot * slot_rows + (k * tt + t) * ROW_SUB, ROW_SUB)
                pltpu.make_async_copy(src_ref.at[pl.ds(srow, ROW_SUB)], buf_ref.at[pl.ds(drow, ROW_SUB)],
                                      sem.at[slot]).start(priority=k % 2)
            return carry
        lax.fori_loop(0, tt, one, 0, unroll=4)

    @pl.when(i == 0)
    def _():
        issue(pos_cur_ref, 0)

    @pl.when(i + 1 < n_steps)
    def _():
        issue(pos_nxt_ref, (i + 1) % 2)

    slot = i % 2
    base = pl.multiple_of(slot * slot_rows, ROW_SUB)
    pltpu.make_async_copy(src_ref.at[pl.ds(0, slot_rows)], buf_ref.at[pl.ds(base, slot_rows)], sem.at[slot]).wait()

    ff_lo = [None] * ROW_SUB
    ff_hi = [None] * ROW_SUB
    for k in range(TOP_K):
        gate = route_ref[:, TOP_K + k:TOP_K + k + 1]
        lo, hi = _unpack_rows(buf_ref, base + k * tt * ROW_SUB, tt)
        for s in range(ROW_SUB):
            ff_lo[s] = lo[s] * gate if k == 0 else ff_lo[s] + lo[s] * gate
            ff_hi[s] = hi[s] * gate if k == 0 else ff_hi[s] + hi[s] * gate
    ff = jnp.concatenate(ff_lo + ff_hi, axis=1)
    o_ref[...] = _layernorm(alpha * h_ref[...] + ff, g_ref[...], b_ref[...])


def _combine_ln(pos_flat, h, route, g_row, b_row, outs, alpha, tt):
    m, d = h.shape
    n_steps = m // tt
    body = functools.partial(_final_body, alpha=alpha, tt=tt, n_steps=n_steps)
    row = lambda i: (i, 0)
    fix = lambda i: (0, 0)
    return pl.pallas_call(
        body,
        grid=(n_steps,),
        in_specs=[pl.BlockSpec((tt * TOP_K,), lambda i: (i,), memory_space=pltpu.SMEM),
                  pl.BlockSpec((tt * TOP_K,), lambda i: (jnp.minimum(i + 1, n_steps - 1),),
                               memory_space=pltpu.SMEM),
                  pl.BlockSpec((tt, d), row), pl.BlockSpec((tt, LANES), row),
                  pl.BlockSpec((1, d), fix), pl.BlockSpec((1, d), fix),
                  pl.BlockSpec(memory_space=pl.ANY)],
        out_specs=pl.BlockSpec((tt, d), row),
        out_shape=jax.ShapeDtypeStruct((m, d), F32),
        scratch_shapes=[pltpu.VMEM((2 * tt * TOP_K * ROW_SUB, LANES), jnp.uint32),
                        pltpu.SemaphoreType.DMA((2,))],
        compiler_params=_cparams(("arbitrary",)),
        name="combine_ln",
    )(pos_flat, pos_flat, h, route, g_row, b_row, outs)


def _row_tile(m, pref):
    t = min(m, pref)
    while m % t:
        t //= 2
    return t


def _route_tables(route, counts_row, m, tm):
    n_assign = m * TOP_K
    e_tok = route[:, :TOP_K].astype(jnp.int32)
    rank = route[:, 2 * TOP_K:3 * TOP_K].astype(jnp.int32)
    counts = counts_row[0, :N_EXPERTS].astype(jnp.int32)
    padded = (counts + tm - 1) // tm * tm
    pends = jnp.cumsum(padded)
    pstarts = pends - padded
    pos = pstarts[e_tok] + rank
    n_tiles = -(-n_assign // tm) + N_EXPERTS
    tile_start = jnp.arange(n_tiles, dtype=jnp.int32) * tm
    tile_expert = jnp.minimum(jnp.sum((pends[None, :] <= tile_start[:, None]).astype(jnp.int32), axis=1),
                              N_EXPERTS - 1)
    n_used = (pends[-1] // tm).astype(jnp.int32).reshape(1)
    last_used = jnp.maximum(n_used[0] - 1, 0)
    tile_expert = jnp.where(tile_start // tm < n_used[0], tile_expert, tile_expert[last_used])
    return pos.reshape(n_assign), tile_expert, n_used


def _layer(x, conv_hist, s_gdn, h_re, h_im, chunk, alpha, p):
    b, t, d = x.shape
    m = b * t
    d_gdn = GDN_HEADS * HEAD_DIM
    x2 = x.reshape(m, d)

    proj = _in_proj(x2, p['w_in'], _row_tile(m, 1024), p['proj_tn'])
    proj3 = proj.reshape(b, t, proj.shape[1])
    conv_new = proj3[:, t - (CONV_W - 1):, :3 * d_gdn]

    q, k, v, cols, gct = _gdn_prep(proj3, conv_hist, p['w_conv'], p['alog_row'], p['dtb_row'],
                                   chunk, p['ab_block'])
    o_gdn, s_new = _gdn(q, k, v, cols, gct, proj3, p['wn_row'], s_gdn, chunk, p['z_block'])

    n_gb = p['s5_mats'][0].shape[0]
    half = GROUPS_PER_BLOCK * SSM_P
    yg, hre_new, him_new = _s5(proj3, p['s5_mats'], h_re.reshape(b, n_gb, 1, half),
                               h_im.reshape(b, n_gb, 1, half), p['u_block0'])
    o_ssm = _glu(yg.reshape(m, yg.shape[2]), p['w_glu'], p['b_glu_row'], _row_tile(m, 512))

    h, hb, route, counts_row = _mix_ln_route(o_gdn.reshape(m, d_gdn), o_ssm, p['w_out_a'], p['w_out_b'], x2,
                                             p['ln1_g'], p['ln1_b'], p['w_router'], p['b_router'], alpha,
                                             _row_tile(m, 256))

    big = m * TOP_K >= 8 * p['moe_tm']
    tm = p['moe_tm'] if big else 128
    pos_flat, tile_expert, n_used = _route_tables(route, counts_row, m, tm)
    cap = tile_expert.shape[0] * tm
    x_rows = _dispatch_rows(pos_flat, hb, cap, _row_tile(m, 256))
    outs = _moe_ffn(tile_expert, n_used, x_rows, p['w_gate'], p['b_gate'],
                    p['w_up'], p['b_up'], p['w_down'], p['b_down'], tm, p['moe_tf'])
    y = _combine_ln(pos_flat, h, route, p['ln2_g'], p['ln2_b'], outs, alpha, _row_tile(m, 128))
    g_all = h_re.shape[1]
    return (y.reshape(b, t, d), conv_new, s_new,
            hre_new.reshape(b, g_all, SSM_P), him_new.reshape(b, g_all, SSM_P))


def _pad_lanes(v, fill=0.0):
    return jnp.pad(v.astype(F32), (0, LANES - v.shape[0]), constant_values=fill).reshape(1, LANES)


def _layer_params(l, w_in, w_conv, a_log, dt_bias, w_onorm, lam_re, lam_im, log_dt, b_re, b_im, c_re, c_im,
                  d_skip, w_glu, b_glu, w_out, ln1_g, ln1_b, w_router, b_router, w_gate, b_gate,
                  w_up, b_up, w_down, b_down, ln2_g, ln2_b):
    d_model = w_in.shape[1]
    d_gdn = GDN_HEADS * HEAD_DIM
    d_qkvz = 4 * d_gdn
    d_ssm = d_model - d_gdn
    wi = w_in[l]
    proj_tn = 1792
    n_cols = d_qkvz + d_ssm + LANES
    n_pad = -(-n_cols // proj_tn) * proj_tn
    w_in_r = jnp.concatenate([wi[:, :d_qkvz], wi[:, d_qkvz + 2 * GDN_HEADS:],
                              wi[:, d_qkvz:d_qkvz + 2 * GDN_HEADS],
                              jnp.zeros((d_model, n_pad - d_qkvz - d_ssm - 2 * GDN_HEADS), wi.dtype)], axis=1)
    wo = w_out[l].astype(BF16)
    wr = jnp.pad(w_router[l].astype(F32), ((0, 0), (0, LANES - N_EXPERTS)))
    return {
        'w_in': w_in_r.astype(BF16), 'proj_tn': proj_tn,
        'z_block': 3, 'u_block0': (d_qkvz) // LANES, 'ab_block': (d_qkvz + d_ssm) // LANES,
        'w_conv': w_conv[l].astype(F32),
        'alog_row': _pad_lanes(a_log[l]), 'dtb_row': _pad_lanes(dt_bias[l]),
        'wn_row': w_onorm[l].astype(F32).reshape(1, HEAD_DIM),
        's5_mats': _s5_matrices(lam_re[l], lam_im[l], log_dt[l], b_re[l], b_im[l], c_re[l], c_im[l], d_skip[l]),
        'w_glu': w_glu[l].astype(BF16), 'b_glu_row': b_glu[l].astype(F32).reshape(1, d_ssm),
        'w_out_a': wo[:d_gdn], 'w_out_b': wo[d_gdn:],
        'ln1_g': ln1_g[l].astype(F32).reshape(1, d_model), 'ln1_b': ln1_b[l].astype(F32).reshape(1, d_model),
        'w_router': wr, 'b_router': _pad_lanes(b_router[l]),
        'w_gate': w_gate[l], 'b_gate': b_gate[l], 'w_up': w_up[l], 'b_up': b_up[l],
        'w_down': w_down[l], 'b_down': b_down[l],
        'ln2_g': ln2_g[l].astype(F32).reshape(1, d_model), 'ln2_b': ln2_b[l].astype(F32).reshape(1, d_model),
        'moe_tm': 768, 'moe_tf': 512,
    }


def kernel(x_prompt, x_sample, state_conv, state_gdn, state_ssm_re, state_ssm_im, w_in, w_conv, a_log, dt_bias, w_onorm, lam_re, lam_im, log_dt, b_re, b_im, c_re, c_im, d_skip, w_glu, b_glu, w_out, ln1_g, ln1_b, w_router, b_router, w_gate, b_gate, w_up, b_up, w_down, b_down, ln2_g, ln2_b):
    depth = w_in.shape[0]
    alpha = (2.0 * depth) ** 0.25
    bp, seq, _ = x_prompt.shape
    chunk_p = 64
    d_qkv = state_conv.shape[-1]
    n_groups, n_p = state_ssm_re.shape[-2:]
    yp, ys = x_prompt, x_sample
    outs_p = [[], [], [], []]
    outs_s = [[], [], [], []]
    for l in range(depth):
        p = _layer_params(l, w_in, w_conv, a_log, dt_bias, w_onorm, lam_re, lam_im, log_dt, b_re, b_im,
                          c_re, c_im, d_skip, w_glu, b_glu, w_out, ln1_g, ln1_b, w_router, b_router,
                          w_gate, b_gate, w_up, b_up, w_down, b_down, ln2_g, ln2_b)
        yp, cp, sp, rp, ip = _layer(
            yp, jnp.zeros((bp, CONV_W - 1, d_qkv), F32),
            jnp.zeros((bp, GDN_HEADS, HEAD_DIM, HEAD_DIM), F32),
            jnp.zeros((bp, n_groups, n_p), F32), jnp.zeros((bp, n_groups, n_p), F32),
            chunk_p, alpha, p)
        ys, cs, ss, rs, is_ = _layer(
            ys, state_conv[l].astype(F32), state_gdn[l].astype(F32),
            state_ssm_re[l].astype(F32), state_ssm_im[l].astype(F32),
            ys.shape[1], alpha, p)
        for acc, val in zip(outs_p, (cp, sp, rp, ip)):
            acc.append(val)
        for acc, val in zip(outs_s, (cs, ss, rs, is_)):
            acc.append(val)
    return (yp, ys, *[jnp.stack(a) for a in outs_p], *[jnp.stack(a) for a in outs_s])
```

```python
import functools
import math

import jax
import jax.numpy as jnp
from jax import lax
from jax.experimental import pallas as pl
from jax.experimental.pallas import tpu as pltpu

F32 = jnp.float32
BF16 = jnp.bfloat16

GDN_HEADS = 8
HEAD_DIM = 128
CONV_W = 4
SSM_CG = 16
SSM_P = 64
N_EXPERTS = 32
TOP_K = 4
SWIGLU_ALPHA = 1.702
SWIGLU_LIMIT = 7.0
LN_EPS = 1e-5
RMS_EPS = 1e-6
L2_EPS = 1e-6

LANES = 128
SUBLANES = 8
S5_SUB = 8
S5_TILE = 2048
GDN_TILE = 128
GROUPS_PER_BLOCK = LANES // SSM_CG
VMEM_LIMIT = 56 * 1024 * 1024

NT_DIMS = (((1,), (1,)), ((), ()))
TN_DIMS = (((0,), (0,)), ((), ()))


def _dot(a, b, dims=(((1,), (0,)), ((), ()))):
    return lax.dot_general(a, b, dims, preferred_element_type=F32)


def _split(a):
    hi = a.astype(BF16)
    lo = (a - hi.astype(F32)).astype(BF16)
    return hi, lo


def _dot_x3(a, b, dims=(((1,), (0,)), ((), ()))):
    ah, al = _split(a)
    bh, bl = _split(b)
    return _dot(ah, bh, dims) + (_dot(ah, bl, dims) + _dot(al, bh, dims))


def _dot_bf(a, b, dims=(((1,), (0,)), ((), ()))):
    return _dot(a.astype(BF16), b.astype(BF16), dims)


_gdn_mm = _dot_bf


def _sigmoid(x):
    return 1.0 / (1.0 + jnp.exp(-x))


ROW_SUB = SUBLANES


def _pack_rows(x, ref, tm):
    half = x.shape[1] // 2
    for s in range(ROW_SUB):
        lo = x[:, s * LANES:(s + 1) * LANES].astype(BF16).astype(F32)
        hi = x[:, half + s * LANES:half + (s + 1) * LANES].astype(BF16).astype(F32)
        word = (lax.bitcast_convert_type(lo, jnp.uint32) >> 16) | lax.bitcast_convert_type(hi, jnp.uint32)
        ref[pl.ds(s, tm, stride=ROW_SUB), :] = word


def _unpack_rows(ref, base, tm):
    lo, hi = [], []
    for s in range(ROW_SUB):
        word = ref[pl.ds(base + s, tm, stride=ROW_SUB), :]
        lo.append(lax.bitcast_convert_type(word << 16, F32))
        hi.append(lax.bitcast_convert_type(word & jnp.uint32(0xFFFF0000), F32))
    return lo, hi


def _cparams(sem):
    return pltpu.CompilerParams(dimension_semantics=sem, vmem_limit_bytes=VMEM_LIMIT)


def _proj_body(x_ref, w_ref, o_ref, xb_ref):
    @pl.when(pl.program_id(1) == 0)
    def _():
        xb_ref[...] = x_ref[...].astype(BF16)

    o_ref[...] = _dot(xb_ref[...], w_ref[...])


def _in_proj(x2, w_bf, tm, tn):
    m, k = x2.shape
    n = w_bf.shape[1]
    return pl.pallas_call(
        _proj_body,
        grid=(m // tm, n // tn),
        in_specs=[pl.BlockSpec((tm, k), lambda i, j: (i, 0)),
                  pl.BlockSpec((k, tn), lambda i, j: (0, j))],
        out_specs=pl.BlockSpec((tm, tn), lambda i, j: (i, j)),
        out_shape=jax.ShapeDtypeStruct((m, n), F32),
        scratch_shapes=[pltpu.VMEM((tm, k), BF16)],
        compiler_params=_cparams(("parallel", "arbitrary")),
        name="in_proj",
    )(x2, w_bf)


def _gdn_prep_body(qkv_ref, ab_ref, hist_ref, wc_ref, alog_ref, dtb_ref,
                   q_ref, k_ref, v_ref, cols_ref, gct_ref, xbuf_ref, *, tt, chunk):
    d_gdn = GDN_HEADS * HEAD_DIM
    halo = SUBLANES

    @pl.when(pl.program_id(1) == 0)
    def _():
        xbuf_ref[0:halo, :] = jnp.zeros((halo, 3 * d_gdn), F32)
        xbuf_ref[halo - (CONV_W - 1):halo, :] = hist_ref[0]

    xbuf_ref[halo:halo + tt, :] = qkv_ref[0]

    for part, out_ref in enumerate((q_ref, k_ref, v_ref)):
        c0 = part * d_gdn
        y = None
        for j in range(CONV_W):
            r0 = halo - (CONV_W - 1) + j
            term = xbuf_ref[r0:r0 + tt, c0:c0 + d_gdn] * wc_ref[j:j + 1, c0:c0 + d_gdn]
            y = term if y is None else y + term
        s = y * _sigmoid(y)
        if part == 2:
            out_ref[0] = s
        else:
            for h in range(GDN_HEADS):
                sh = s[:, h * HEAD_DIM:(h + 1) * HEAD_DIM]
                nrm = sh * lax.rsqrt(jnp.sum(sh * sh, axis=-1, keepdims=True) + L2_EPS)
                if part == 0:
                    nrm = nrm * (HEAD_DIM ** -0.5)
                out_ref[0, :, h * HEAD_DIM:(h + 1) * HEAD_DIM] = nrm

    xbuf_ref[0:halo, :] = xbuf_ref[tt:tt + halo, :]

    ab = ab_ref[0]
    lane = lax.broadcasted_iota(jnp.int32, (tt, LANES), 1)
    is_a = lane < GDN_HEADS
    z = ab + dtb_ref[...]
    softplus = jnp.maximum(z, 0.0) + jnp.log1p(jnp.exp(-jnp.abs(z)))
    g = jnp.where(is_a, -jnp.exp(alog_ref[...]) * softplus, 0.0)
    beta = _sigmoid(ab)

    shift = int(math.log2(chunk))
    r = lax.broadcasted_iota(jnp.int32, (tt, tt), 0)
    c = lax.broadcasted_iota(jnp.int32, (tt, tt), 1)
    same = (r >> shift) == (c >> shift)
    m_incl = jnp.where(same & (r >= c), 1.0, 0.0).astype(BF16)
    m_all = jnp.where(same, 1.0, 0.0).astype(BF16)
    gc = _dot_exact_lhs_rhs(m_incl, g)
    glast = _dot_exact_lhs_rhs(m_all, g)
    eg = jnp.exp(gc)
    egl = jnp.exp(glast - gc)
    egt = jnp.exp(glast)
    zero = jnp.zeros_like(gc)
    cols = (jnp.where(is_a, gc, zero)
            + jnp.where((lane >= 8) & (lane < 16), beta, zero)
            + pltpu.roll(jnp.where(is_a, eg, zero), 16, 1)
            + pltpu.roll(jnp.where(is_a, egl, zero), 24, 1)
            + pltpu.roll(jnp.where(is_a, egt, zero), 32, 1))
    cols_ref[0] = cols

    er = lax.broadcasted_iota(jnp.int32, (SUBLANES, LANES), 0)
    ec = lax.broadcasted_iota(jnp.int32, (SUBLANES, LANES), 1)
    sel = jnp.where(er == ec, 1.0, 0.0).astype(BF16)
    gct_ref[0] = _dot_exact_rhs(sel, jnp.where(is_a, gc, zero), NT_DIMS)


def _three_pieces(a):
    a0 = a.astype(BF16)
    r1 = a - a0.astype(F32)
    a1 = r1.astype(BF16)
    a2 = (r1 - a1.astype(F32)).astype(BF16)
    return a0, a1, a2


def _dot_exact_lhs_rhs(mask_bf, a):
    a0, a1, a2 = _three_pieces(a)
    return _dot(mask_bf, a0) + (_dot(mask_bf, a1) + _dot(mask_bf, a2))


def _dot_exact_rhs(mask_bf, a, dims):
    a0, a1, a2 = _three_pieces(a)
    return _dot(mask_bf, a0, dims) + (_dot(mask_bf, a1, dims) + _dot(mask_bf, a2, dims))


def _gdn_prep(proj3, hist, w_conv, alog_p, dtb_p, chunk, ab_block):
    b, t, _ = proj3.shape
    d_gdn = GDN_HEADS * HEAD_DIM
    tt = min(t, GDN_TILE)
    body = functools.partial(_gdn_prep_body, tt=tt, chunk=chunk)
    big = jax.ShapeDtypeStruct((b, t, d_gdn), F32)
    return pl.pallas_call(
        body,
        grid=(b, t // tt),
        in_specs=[pl.BlockSpec((1, tt, 3 * d_gdn), lambda i, j: (i, j, 0)),
                  pl.BlockSpec((1, tt, LANES), lambda i, j: (i, j, ab_block)),
                  pl.BlockSpec((1, CONV_W - 1, 3 * d_gdn), lambda i, j: (i, 0, 0)),
                  pl.BlockSpec((CONV_W, 3 * d_gdn), lambda i, j: (0, 0)),
                  pl.BlockSpec((1, LANES), lambda i, j: (0, 0)),
                  pl.BlockSpec((1, LANES), lambda i, j: (0, 0))],
        out_specs=[pl.BlockSpec((1, tt, d_gdn), lambda i, j: (i, j, 0)),
                   pl.BlockSpec((1, tt, d_gdn), lambda i, j: (i, j, 0)),
                   pl.BlockSpec((1, tt, d_gdn), lambda i, j: (i, j, 0)),
                   pl.BlockSpec((1, tt, LANES), lambda i, j: (i, j, 0)),
                   pl.BlockSpec((1, SUBLANES, tt), lambda i, j: (i, 0, j))],
        out_shape=[big, big, big,
                   jax.ShapeDtypeStruct((b, t, LANES), F32),
                   jax.ShapeDtypeStruct((b, SUBLANES, t), F32)],
        scratch_shapes=[pltpu.VMEM((tt + SUBLANES, 3 * d_gdn), F32)],
        compiler_params=_cparams(("parallel", "arbitrary")),
        name="gdn_prep",
    )(proj3, proj3, hist, w_conv, alog_p, dtb_p)


def _unit_lower_inverses(mats, r, c, chunk):
    base = 16
    eye = jnp.where(r == c, 1.0, 0.0)

    def blk(bs):
        s = int(math.log2(bs))
        return (r >> s) == (c >> s)

    d1 = [jnp.where(blk(base), a, 0.0) for a in mats]
    d2 = [_gdn_mm(x, x) for x in d1]
    d4 = [_gdn_mm(x, x) for x in d2]
    d8 = [_gdn_mm(x, x) for x in d4]
    t = [eye - x for x in d1]
    t = [x + _gdn_mm(x, y) for x, y in zip(t, d2)]
    t = [x + _gdn_mm(x, y) for x, y in zip(t, d4)]
    t = [x + _gdn_mm(x, y) for x, y in zip(t, d8)]
    bs = base
    while bs < chunk:
        off_mask = blk(2 * bs) & jnp.logical_not(blk(bs))
        inner = [_gdn_mm(jnp.where(off_mask, a, 0.0), x) for a, x in zip(mats, t)]
        t = [x - _gdn_mm(x, y) for x, y in zip(t, inner)]
        bs *= 2
    return t


def _gdn_body(q_ref, k_ref, v_ref, cols_ref, gct_ref, z_ref, wn_ref, s0_ref,
              o_ref, s_ref, *, tb, chunk):
    @pl.when(pl.program_id(1) == 0)
    def _():
        s_ref[...] = s0_ref[...]

    shift = int(math.log2(chunk))
    r = lax.broadcasted_iota(jnp.int32, (tb, tb), 0)
    c = lax.broadcasted_iota(jnp.int32, (tb, tb), 1)
    same = (r >> shift) == (c >> shift)
    incl = same & (r >= c)
    strict = same & (r > c)
    n_chunks = tb // chunk

    heads = range(GDN_HEADS)
    hsl = [slice(h * HEAD_DIM, (h + 1) * HEAD_DIM) for h in heads]
    q = [q_ref[0, :, s] for s in hsl]
    k = [k_ref[0, :, s] for s in hsl]
    beta = [cols_ref[0, :, 8 + h:9 + h] for h in heads]
    eg = [cols_ref[0, :, 16 + h:17 + h] for h in heads]
    egl = [cols_ref[0, :, 24 + h:25 + h] for h in heads]
    decay = [jnp.exp(jnp.where(incl, cols_ref[0, :, h:h + 1] - gct_ref[0, h:h + 1, :], -jnp.inf)) for h in heads]
    kb = [k[h] * beta[h] for h in heads]
    k_bf = [x.astype(BF16) for x in k]
    a = [jnp.where(strict, _dot(kb[h].astype(BF16), k_bf[h], NT_DIMS) * decay[h], 0.0) for h in heads]
    tinv = _unit_lower_inverses(a, r, c, chunk)
    sol = [_gdn_mm(tinv[h], jnp.concatenate([v_ref[0, :, hsl[h]] * beta[h], kb[h] * eg[h]], axis=1)) for h in heads]
    u = [x[:, :HEAD_DIM] for x in sol]
    w_bf = [x[:, HEAD_DIM:].astype(BF16) for x in sol]
    attn = [(_dot(q[h].astype(BF16), k_bf[h], NT_DIMS) * decay[h]).astype(BF16) for h in heads]
    q_dec = [(q[h] * eg[h]).astype(BF16) for h in heads]
    k_dec = [(k[h] * egl[h]).astype(BF16) for h in heads]

    s = [s_ref[0, h] for h in heads]
    v_new = [[] for _ in heads]
    o_state = [[] for _ in heads]
    for ci in range(n_chunks):
        rs = slice(ci * chunk, (ci + 1) * chunk)
        s_bf = [x.astype(BF16) for x in s]
        vn = [u[h][rs] - _dot(w_bf[h][rs], s_bf[h]) for h in heads]
        for h in heads:
            o_state[h].append(_dot(q_dec[h][rs], s_bf[h]))
            v_new[h].append(vn[h])
        s = [s[h] * cols_ref[0, ci * chunk:ci * chunk + 1, 32 + h:33 + h]
             + _dot(k_dec[h][rs], vn[h].astype(BF16), TN_DIMS) for h in heads]
    for h in heads:
        s_ref[0, h] = s[h]

    def cat(parts):
        return parts[0] if len(parts) == 1 else jnp.concatenate(parts, axis=0)

    for h in heads:
        o = cat(o_state[h]) + _dot(attn[h], cat(v_new[h]).astype(BF16))
        zh = z_ref[0, :, hsl[h]]
        o = (o * lax.rsqrt(jnp.mean(o * o, axis=-1, keepdims=True) + RMS_EPS) * wn_ref[...]
             * (zh * _sigmoid(zh)))
        o_ref[0, :, hsl[h]] = o.astype(BF16)


def _gdn(q, k, v, cols, gct, proj3, wn, s0, chunk, z_block):
    b, t, d_gdn = q.shape
    tb = min(t, GDN_TILE)
    body = functools.partial(_gdn_body, tb=tb, chunk=chunk)
    tile = lambda i, j: (i, j, 0)
    return pl.pallas_call(
        body,
        grid=(b, t // tb),
        in_specs=[pl.BlockSpec((1, tb, d_gdn), tile),
                  pl.BlockSpec((1, tb, d_gdn), tile),
                  pl.BlockSpec((1, tb, d_gdn), tile),
                  pl.BlockSpec((1, tb, LANES), tile),
                  pl.BlockSpec((1, SUBLANES, tb), lambda i, j: (i, 0, j)),
                  pl.BlockSpec((1, tb, d_gdn), lambda i, j: (i, j, z_block)),
                  pl.BlockSpec((1, HEAD_DIM), lambda i, j: (0, 0)),
                  pl.BlockSpec((1, GDN_HEADS, HEAD_DIM, HEAD_DIM), lambda i, j: (i, 0, 0, 0))],
        out_specs=[pl.BlockSpec((1, tb, d_gdn), tile),
                   pl.BlockSpec((1, GDN_HEADS, HEAD_DIM, HEAD_DIM), lambda i, j: (i, 0, 0, 0))],
        out_shape=[jax.ShapeDtypeStruct((b, t, d_gdn), BF16),
                   jax.ShapeDtypeStruct((b, GDN_HEADS, HEAD_DIM, HEAD_DIM), F32)],
        compiler_params=_cparams(("parallel", "arbitrary")),
        name="gdn_delta",
    )(q, k, v, cols, gct, proj3, wn, s0)


def _gelu_tanh(x):
    return 0.5 * x * (1.0 + jnp.tanh(math.sqrt(2.0 / math.pi) * (x + 0.044715 * (x * x * x))))


def _s5_body(u_ref, wm_ref, ym0_ref, ym1_ref, ym2_ref, ym3_ref, lre_ref, lim_ref, dsk_ref, h0re_ref, h0im_ref,
             y_ref, hre_ref, him_ref, xbuf_ref, hbuf_ref, *, n_sub, seg):
    half = GROUPS_PER_BLOCK * SSM_P

    if seg is None:
        @pl.when(pl.program_id(2) == 0)
        def _():
            hre_ref[...] = h0re_ref[...]
            him_ref[...] = h0im_ref[...]

    u_f = [u_ref[0, pl.ds(j, n_sub, stride=S5_SUB), :] for j in range(S5_SUB)]
    u_b = [x.astype(BF16) for x in u_f]

    xbuf_ref[...] = _dot(jnp.concatenate(u_b, axis=1), wm_ref[0])

    lre = lre_ref[0]
    lim = lim_ref[0]

    def advance(n, hre, him):
        hbuf_ref[pl.ds(n, 1), 0:half] = hre
        hbuf_ref[pl.ds(n, 1), half:2 * half] = him
        xr = xbuf_ref[pl.ds(n, 1), 0:half]
        xi = xbuf_ref[pl.ds(n, 1), half:2 * half]
        return (lre * hre - lim * him + xr, lre * him + lim * hre + xi)

    if seg is None:
        hre, him = lax.fori_loop(0, n_sub, lambda n, c: advance(n, *c), (hre_ref[0, 0], him_ref[0, 0]))
        hre_ref[0, 0] = hre
        him_ref[0, 0] = him
    else:
        def step(n, carry):
            q = n // seg
            first = n - q * seg == 0
            hre = jnp.where(first, h0re_ref[q, 0], carry[0])
            him = jnp.where(first, h0im_ref[q, 0], carry[1])
            hre, him = advance(n, hre, him)
            hre_ref[q, 0] = hre
            him_ref[q, 0] = him
            return hre, him

        lax.fori_loop(0, n_sub, step, (h0re_ref[0, 0], h0im_ref[0, 0]))

    h_b = hbuf_ref[...].astype(BF16)
    dsk = dsk_ref[0]
    for l, ym_ref in zip(range(0, S5_SUB, 2), (ym0_ref, ym1_ref, ym2_ref, ym3_ref)):
        lhs = jnp.concatenate([h_b] + [u_b[m] for m in range(l + 1, -1, -1)], axis=1)
        y2 = _dot(lhs, ym_ref[0])
        for o in range(2):
            y = y2[:, o * LANES:(o + 1) * LANES] + dsk * u_f[l + o]
            y_ref[0, pl.ds(l + o, n_sub, stride=S5_SUB), :] = _gelu_tanh(y)


def _s5(proj3, mats, h0re, h0im, u_block0):
    b, t, width = proj3.shape
    n_gb = mats[0].shape[0]
    if b > 1 and b * t <= S5_TILE:
        y, hre, him = _s5_call(proj3.reshape(1, b * t, width), mats, h0re, h0im, u_block0,
                               tt=b * t, seg=t // S5_SUB, state_rows=b)
        return y.reshape(b, t, n_gb * LANES), hre, him
    return _s5_call(proj3, mats, h0re, h0im, u_block0, tt=min(t, S5_TILE), seg=None, state_rows=1)


def _s5_call(proj3, mats, h0re, h0im, u_block0, tt, seg, state_rows):
    wcat, ycat, lre, lim, dsk = mats
    b, t, _ = proj3.shape
    n_gb = wcat.shape[0]
    n_sub = tt // S5_SUB
    half = GROUPS_PER_BLOCK * SSM_P
    body = functools.partial(_s5_body, n_sub=n_sub, seg=seg)
    state_spec = pl.BlockSpec((state_rows, 1, 1, half), lambda g, i, j: (i, g, 0, 0))
    par_spec = pl.BlockSpec((1, 1, half), lambda g, i, j: (g, 0, 0))
    mat_spec = lambda a: pl.BlockSpec((1,) + a.shape[1:], lambda g, i, j: (g, 0, 0))
    return pl.pallas_call(
        body,
        grid=(n_gb, b, t // tt),
        in_specs=[pl.BlockSpec((1, tt, LANES), lambda g, i, j: (i, j, u_block0 + g)),
                  mat_spec(wcat)] + [mat_spec(a) for a in ycat] + [
                  par_spec, par_spec,
                  pl.BlockSpec((1, 1, LANES), lambda g, i, j: (g, 0, 0)),
                  state_spec, state_spec],
        out_specs=[pl.BlockSpec((1, tt, LANES), lambda g, i, j: (i, j, g)),
                   state_spec, state_spec],
        out_shape=[jax.ShapeDtypeStruct((b, t, n_gb * LANES), F32),
                   jax.ShapeDtypeStruct(h0re.shape, F32),
                   jax.ShapeDtypeStruct(h0im.shape, F32)],
        scratch_shapes=[pltpu.VMEM((n_sub, 2 * half), F32),
                        pltpu.VMEM((n_sub, 2 * half), F32)],
        compiler_params=_cparams(("parallel", "parallel", "arbitrary")),
        name="s5_scan",
    )(proj3, wcat, *ycat, lre, lim, dsk, h0re, h0im)


def _s5_matrices(lam_re, lam_im, log_dt, b_re, b_im, c_re, c_im, d_skip):
    g, p = lam_re.shape
    n_gb = g // GROUPS_PER_BLOCK
    gpb = GROUPS_PER_BLOCK
    dt = jnp.exp(log_dt.astype(F32))
    lam = lax.complex(jnp.minimum(lam_re.astype(F32), -1e-4), lam_im.astype(F32))
    lam_bar = jnp.exp(lam * dt[:, None])
    b_bar = ((lam_bar - 1.0) / lam)[..., None] * lax.complex(b_re.astype(F32), b_im.astype(F32))
    c_c = lax.complex(c_re.astype(F32), c_im.astype(F32))
    pows = [jnp.ones_like(lam_bar)]
    for _ in range(S5_SUB):
        pows.append(pows[-1] * lam_bar)
    pw = jnp.stack(pows)

    def block_diag(x):
        k = x.shape[-1]
        lanes = jnp.arange(gpb * k)
        rep = (lanes[None, :] % k == jnp.arange(k)[:, None]).astype(F32)
        wide = jnp.einsum('...k,kl->...l', x, rep, precision=lax.Precision.HIGHEST)
        own = (lanes[None, None, :] // k) == jnp.arange(gpb)[:, None, None]
        wide = jnp.where(own, wide, 0.0)
        return wide.reshape(x.shape[:-3] + (gpb * x.shape[-2], gpb * k))

    kd = jnp.real(jnp.einsum('gop,dgp,gpi->dgio', c_c, pw[:S5_SUB], b_bar))
    km = block_diag(kd.reshape(S5_SUB, n_gb, gpb, SSM_CG, SSM_CG))

    wj = pw[:S5_SUB][::-1][:, :, :, None] * b_bar[None]
    wj = jnp.transpose(wj.reshape(S5_SUB, n_gb, gpb, p, SSM_CG), (1, 0, 2, 4, 3))
    wcat = jnp.concatenate([block_diag(jnp.real(wj)), block_diag(jnp.imag(wj))], axis=-1)
    wcat = wcat.reshape(n_gb, S5_SUB * LANES, 2 * gpb * p).astype(BF16)

    cl = c_c[None] * pw[1:S5_SUB + 1][:, :, None, :]
    cl = jnp.transpose(cl, (0, 1, 3, 2)).reshape(S5_SUB, n_gb, gpb, p, SSM_CG)
    vm = jnp.concatenate([block_diag(jnp.real(cl)), block_diag(-jnp.imag(cl))], axis=-2)

    lam_s = pw[S5_SUB].reshape(n_gb, 1, gpb * p)
    dsk = d_skip.astype(F32).reshape(n_gb, 1, LANES)

    km, vm = km.astype(BF16), vm.astype(BF16)
    ycat = []
    zero = jnp.zeros_like(km[0])
    for l in range(0, S5_SUB, 2):
        rows = [jnp.concatenate([vm[l], vm[l + 1]], axis=-1)]
        for m in range(l + 1, -1, -1):
            left = km[l - m] if m <= l else zero
            rows.append(jnp.concatenate([left, km[l + 1 - m]], axis=-1))
        ycat.append(jnp.concatenate(rows, axis=-2))
    return (wcat, tuple(ycat), jnp.real(lam_s), jnp.imag(lam_s), dsk)


def _glu_body(y_ref, w_ref, b_ref, o_ref):
    y = y_ref[...]
    gate = _dot(y.astype(BF16), w_ref[...]) + b_ref[...]
    o_ref[...] = (y * _sigmoid(gate)).astype(BF16)


def _glu(y2, w_bf, b_row, tm):
    m, d = y2.shape
    return pl.pallas_call(
        _glu_body,
        grid=(m // tm,),
        in_specs=[pl.BlockSpec((tm, d), lambda i: (i, 0)),
                  pl.BlockSpec((d, d), lambda i: (0, 0)),
                  pl.BlockSpec((1, d), lambda i: (0, 0))],
        out_specs=pl.BlockSpec((tm, d), lambda i: (i, 0)),
        out_shape=jax.ShapeDtypeStruct((m, d), BF16),
        compiler_params=_cparams(("parallel",)),
        name="s5_glu",
    )(y2, w_bf, b_row)


def _layernorm(v, g, b):
    mu = jnp.mean(v, axis=-1, keepdims=True)
    var = jnp.mean(jnp.square(v - mu), axis=-1, keepdims=True)
    return (v - mu) * lax.rsqrt(var + LN_EPS) * g + b


def _mix_body(og_ref, os_ref, wa_ref, wb_ref, x_ref, g_ref, b_ref, wr_ref, br_ref,
              h_ref, hb_ref, route_ref, cnt_ref, *, alpha):
    @pl.when(pl.program_id(0) == 0)
    def _():
        cnt_ref[...] = jnp.zeros_like(cnt_ref)

    mix = _dot(og_ref[...], wa_ref[...]) + _dot(os_ref[...], wb_ref[...])
    h = _layernorm(alpha * x_ref[...] + mix, g_ref[...], b_ref[...])
    h_ref[...] = h
    tm = h.shape[0]
    _pack_rows(h, hb_ref, tm)

    logits = _dot_x3(h, wr_ref[...]) + br_ref[...]
    lane = lax.broadcasted_iota(jnp.int32, (tm, LANES), 1)
    work = jnp.where(lane < N_EXPERTS, logits, -jnp.inf)
    vals, idxs = [], []
    for _ in range(TOP_K):
        mx = jnp.max(work, axis=-1, keepdims=True)
        ix = jnp.min(jnp.where(work == mx, lane, LANES), axis=-1, keepdims=True)
        vals.append(mx)
        idxs.append(ix)
        work = jnp.where(lane == ix, -jnp.inf, work)
    exps = [jnp.exp(v - vals[0]) for v in vals]
    denom = exps[0]
    for e in exps[1:]:
        denom = denom + e
    chosen = jnp.zeros((tm, LANES), F32)
    for k in range(TOP_K):
        chosen = jnp.where(lane == idxs[k], 1.0, chosen)
    rr = lax.broadcasted_iota(jnp.int32, (tm, tm), 0)
    cc = lax.broadcasted_iota(jnp.int32, (tm, tm), 1)
    lower = jnp.where(rr > cc, 1.0, 0.0).astype(BF16)
    before = cnt_ref[...] + _dot(lower, chosen.astype(BF16))
    cnt_ref[...] = cnt_ref[...] + jnp.sum(chosen, axis=0, keepdims=True)

    route = jnp.zeros((tm, LANES), F32)
    for k in range(TOP_K):
        rank = jnp.sum(jnp.where(lane == idxs[k], before, 0.0), axis=-1, keepdims=True)
        route = jnp.where(lane == k, idxs[k].astype(F32), route)
        route = jnp.where(lane == TOP_K + k, exps[k] / denom, route)
        route = jnp.where(lane == 2 * TOP_K + k, rank, route)
    route_ref[...] = route


def _mix_ln_route(og, osm, wa, wb, x2, g_row, b_row, wr, br, alpha, tm):
    m, d = x2.shape
    dh = og.shape[1]
    body = functools.partial(_mix_body, alpha=alpha)
    row = lambda i: (i, 0)
    fix = lambda i: (0, 0)
    return pl.pallas_call(
        body,
        grid=(m // tm,),
        in_specs=[pl.BlockSpec((tm, dh), row), pl.BlockSpec((tm, dh), row),
                  pl.BlockSpec((dh, d), fix), pl.BlockSpec((dh, d), lambda i: (1, 0)),
                  pl.BlockSpec((tm, d), row),
                  pl.BlockSpec((1, d), fix), pl.BlockSpec((1, d), fix),
                  pl.BlockSpec((d, LANES), fix), pl.BlockSpec((1, LANES), fix)],
        out_specs=[pl.BlockSpec((tm, d), row), pl.BlockSpec((tm * ROW_SUB, LANES), row),
                   pl.BlockSpec((tm, LANES), row), pl.BlockSpec((1, LANES), fix)],
        out_shape=[jax.ShapeDtypeStruct((m, d), F32),
                   jax.ShapeDtypeStruct((m * ROW_SUB, LANES), jnp.uint32),
                   jax.ShapeDtypeStruct((m, LANES), F32),
                   jax.ShapeDtypeStruct((1, LANES), F32)],
        compiler_params=_cparams(("arbitrary",)),
        name="mix_ln_route",
    )(og, osm, wa, wb, x2, g_row, b_row, wr, br)


def _dispatch_body(pos_ref, src_ref, init_ref, dst_ref, sem, *, tt):
    del init_ref

    def issue(t, carry):
        srow = pl.multiple_of(t * ROW_SUB, ROW_SUB)
        for k in range(TOP_K):
            drow = pl.multiple_of(pos_ref[t * TOP_K + k] * ROW_SUB, ROW_SUB)
            pltpu.make_async_copy(src_ref.at[pl.ds(srow, ROW_SUB)], dst_ref.at[pl.ds(drow, ROW_SUB)],
                                  sem).start(priority=k % 2)
        return carry

    lax.fori_loop(0, tt, issue, 0, unroll=4)
    n = tt * ROW_SUB
    for _ in range(TOP_K):
        pltpu.make_async_copy(src_ref.at[pl.ds(0, n)], dst_ref.at[pl.ds(0, n)], sem).wait()


def _dispatch_rows(pos_flat, hbp, cap, tt):
    m = hbp.shape[0] // ROW_SUB
    body = functools.partial(_dispatch_body, tt=tt)
    init = jnp.zeros((cap * ROW_SUB, LANES), jnp.uint32)
    return pl.pallas_call(
        body,
        grid=(m // tt,),
        in_specs=[pl.BlockSpec((tt * TOP_K,), lambda i: (i,), memory_space=pltpu.SMEM),
                  pl.BlockSpec((tt * ROW_SUB, LANES), lambda i: (i, 0)), pl.BlockSpec(memory_space=pl.ANY)],
        out_specs=pl.BlockSpec(memory_space=pl.ANY),
        out_shape=jax.ShapeDtypeStruct((cap * ROW_SUB, LANES), jnp.uint32),
        scratch_shapes=[pltpu.SemaphoreType.DMA(())],
        input_output_aliases={2: 0},
        compiler_params=_cparams(("arbitrary",)),
        name="moe_dispatch",
    )(pos_flat, hbp, init)


MOE_COL = 256
MOE_GROUP = 2


def _moe_body(te_ref, nu_ref, x_ref, wg_ref, bg_ref, wu_ref, bu_ref, wd_ref, bd_ref,
              o_ref, xb_ref, acc_ref, *, n_f, tm, tf):
    g = pl.program_id(2)
    f = pl.program_id(1)
    used = pl.program_id(0) * MOE_GROUP + g < nu_ref[0]
    half = xb_ref.shape[2] // 2
    xb_ref = xb_ref.at[g]
    acc_ref = acc_ref.at[g]

    @pl.when(used & (f == 0))
    def _():
        lo, hi = _unpack_rows(x_ref, 0, tm)
        for r in range(ROW_SUB):
            xb_ref[:, r * LANES:(r + 1) * LANES] = lo[r].astype(BF16)
            xb_ref[:, half + r * LANES:half + (r + 1) * LANES] = hi[r].astype(BF16)

    @pl.when(used)
    def _():
        xb = xb_ref[...]
        part = None
        for c in range(tf // MOE_COL):
            cs = slice(c * MOE_COL, (c + 1) * MOE_COL)
            hg = jnp.minimum(_dot(xb, wg_ref[0, :, cs].astype(BF16)) + bg_ref[0, :, cs], SWIGLU_LIMIT)
            hu = jnp.clip(_dot(xb, wu_ref[0, :, cs].astype(BF16)) + bu_ref[0, :, cs], -SWIGLU_LIMIT, SWIGLU_LIMIT)
            hh = ((hu + 1.0) * (hg * _sigmoid(SWIGLU_ALPHA * hg))).astype(BF16)
            p = _dot(hh, wd_ref[0, cs, :].astype(BF16))
            part = p if part is None else part + p

        @pl.when(f == 0)
        def _():
            acc_ref[...] = part

        @pl.when(f != 0)
        def _():
            acc_ref[...] += part

        @pl.when(f == n_f - 1)
        def _():
            _pack_rows(acc_ref[...] + bd_ref[0], o_ref, tm)

    @pl.when(jnp.logical_not(used) & (f == n_f - 1))
    def _():
        o_ref[...] = jnp.zeros_like(o_ref)


def _moe_ffn(tile_expert, n_used, x_rows, w_gate, b_gate, w_up, b_up, w_down, b_down, tm, tf):
    n_e, d, d_ff = w_gate.shape
    n_tiles = x_rows.shape[0] // (tm * ROW_SUB)
    n_f = d_ff // tf
    grp = MOE_GROUP
    assert n_tiles % grp == 0
    body = functools.partial(_moe_body, n_f=n_f, tm=tm, tf=tf)

    def tile(p, g):
        return p * grp + g

    def fcol(p, f, nu):
        return jnp.where(p * grp < nu[0], f, n_f - 1)

    def xrow(p, f, g):
        return jnp.where(f == 0, tile(p, g), tile(p, grp - 1))

    def orow(p, f, g):
        return jnp.where(f == n_f - 1, tile(p, g), tile(p, 0))

    grid_spec = pltpu.PrefetchScalarGridSpec(
        num_scalar_prefetch=2,
        grid=(n_tiles // grp, n_f, grp),
        in_specs=[pl.BlockSpec((tm * ROW_SUB, LANES), lambda p, f, g, te, nu: (xrow(p, f, g), 0)),
                  pl.BlockSpec((1, d, tf), lambda p, f, g, te, nu: (te[tile(p, g)], 0, fcol(p, f, nu))),
                  pl.BlockSpec((1, 1, tf), lambda p, f, g, te, nu: (te[tile(p, g)], 0, fcol(p, f, nu))),
                  pl.BlockSpec((1, d, tf), lambda p, f, g, te, nu: (te[tile(p, g)], 0, fcol(p, f, nu))),
                  pl.BlockSpec((1, 1, tf), lambda p, f, g, te, nu: (te[tile(p, g)], 0, fcol(p, f, nu))),
                  pl.BlockSpec((1, tf, d), lambda p, f, g, te, nu: (te[tile(p, g)], fcol(p, f, nu), 0)),
                  pl.BlockSpec((1, 1, d), lambda p, f, g, te, nu: (te[tile(p, g)], 0, 0))],
        out_specs=pl.BlockSpec((tm * ROW_SUB, LANES), lambda p, f, g, te, nu: (orow(p, f, g), 0)),
        scratch_shapes=[pltpu.VMEM((grp, tm, d), BF16), pltpu.VMEM((grp, tm, d), F32)],
    )
    return pl.pallas_call(
        body,
        grid_spec=grid_spec,
        out_shape=jax.ShapeDtypeStruct(x_rows.shape, jnp.uint32),
        compiler_params=_cparams(("arbitrary", "arbitrary", "arbitrary")),
        name="moe_ffn",
    )(tile_expert, n_used, x_rows, w_gate, b_gate.reshape(n_e, 1, d_ff),
      w_up, b_up.reshape(n_e, 1, d_ff), w_down, b_down.reshape(n_e, 1, d))


def _final_body(pos_cur_ref, pos_nxt_ref, h_ref, route_ref, g_ref, b_ref, src_ref,
                o_ref, buf_ref, sem, *, alpha, tt, n_steps):
    i = pl.program_id(0)
    slot_rows = tt * TOP_K * ROW_SUB

    def issue(pref, slot):
        def one(t, carry):
            for k in range(TOP_K):
                srow = pl.multiple_of(pref[t * TOP_K + k] * ROW_SUB, ROW_SUB)
                drow = pl.multiple_of(slot * slot_rows + (k * tt + t) * ROW_SUB, ROW_SUB)
                pltpu.make_async_copy(src_ref.at[pl.ds(srow, ROW_SUB)], buf_ref.at[pl.ds(drow, ROW_SUB)],
                                      sem.at[slot]).start(priority=k % 2)
            return carry
        lax.fori_loop(0, tt, one, 0, unroll=4)

    @pl.when(i == 0)
    def _():
        issue(pos_cur_ref, 0)

    @pl.when(i + 1 < n_steps)
    def _():
        issue(pos_nxt_ref, (i + 1) % 2)

    slot = i % 2
    base = pl.multiple_of(slot * slot_rows, ROW_SUB)
    pltpu.make_async_copy(src_ref.at[pl.ds(0, slot_rows)], buf_ref.at[pl.ds(base, slot_rows)], sem.at[slot]).wait()

    ff_lo = [None] * ROW_SUB
    ff_hi = [None] * ROW_SUB
    for k in range(TOP_K):
        gate = route_ref[:, TOP_K + k:TOP_K + k + 1]
        lo, hi = _unpack_rows(buf_ref, base + k * tt * ROW_SUB, tt)
        for s in range(ROW_SUB):
            ff_lo[s] = lo[s] * gate if k == 0 else ff_lo[s] + lo[s] * gate
            ff_hi[s] = hi[s] * gate if k == 0 else ff_hi[s] + hi[s] * gate
    ff = jnp.concatenate(ff_lo + ff_hi, axis=1)
    o_ref[...] = _layernorm(alpha * h_ref[...] + ff, g_ref[...], b_ref[...])


def _combine_ln(pos_flat, h, route, g_row, b_row, outs, alpha, tt):
    m, d = h.shape
    n_steps = m // tt
    body = functools.partial(_final_body, alpha=alpha, tt=tt, n_steps=n_steps)
    row = lambda i: (i, 0)
    fix = lambda i: (0, 0)
    return pl.pallas_call(
        body,
        grid=(n_steps,),
        in_specs=[pl.BlockSpec((tt * TOP_K,), lambda i: (i,), memory_space=pltpu.SMEM),
                  pl.BlockSpec((tt * TOP_K,), lambda i: (jnp.minimum(i + 1, n_steps - 1),),
                               memory_space=pltpu.SMEM),
                  pl.BlockSpec((tt, d), row), pl.BlockSpec((tt, LANES), row),
                  pl.BlockSpec((1, d), fix), pl.BlockSpec((1, d), fix),
                  pl.BlockSpec(memory_space=pl.ANY)],
        out_specs=pl.BlockSpec((tt, d), row),
        out_shape=jax.ShapeDtypeStruct((m, d), F32),
        scratch_shapes=[pltpu.VMEM((2 * tt * TOP_K * ROW_SUB, LANES), jnp.uint32),
                        pltpu.SemaphoreType.DMA((2,))],
        compiler_params=_cparams(("arbitrary",)),
        name="combine_ln",
    )(pos_flat, pos_flat, h, route, g_row, b_row, outs)


def _row_tile(m, pref):
    t = min(m, pref)
    while m % t:
        t //= 2
    return t


def _route_tables(route, counts_row, m, tm):
    n_assign = m * TOP_K
    e_tok = route[:, :TOP_K].astype(jnp.int32)
    rank = route[:, 2 * TOP_K:3 * TOP_K].astype(jnp.int32)
    counts = counts_row[0, :N_EXPERTS].astype(jnp.int32)
    padded = (counts + tm - 1) // tm * tm
    pends = jnp.cumsum(padded)
    pstarts = pends - padded
    pos = pstarts[e_tok] + rank
    n_tiles = -(-n_assign // tm) + N_EXPERTS
    tile_start = jnp.arange(n_tiles, dtype=jnp.int32) * tm
    tile_expert = jnp.minimum(jnp.sum((pends[None, :] <= tile_start[:, None]).astype(jnp.int32), axis=1),
                              N_EXPERTS - 1)
    n_used = (pends[-1] // tm).astype(jnp.int32).reshape(1)
    last_used = jnp.maximum(n_used[0] - 1, 0)
    tile_expert = jnp.where(tile_start // tm < n_used[0], tile_expert, tile_expert[last_used])
    return pos.reshape(n_assign), tile_expert, n_used


def _layer(x, conv_hist, s_gdn, h_re, h_im, chunk, alpha, p):
    b, t, d = x.shape
    m = b * t
    d_gdn = GDN_HEADS * HEAD_DIM
    x2 = x.reshape(m, d)

    proj = _in_proj(x2, p['w_in'], _row_tile(m, 1024), p['proj_tn'])
    proj3 = proj.reshape(b, t, proj.shape[1])
    conv_new = proj3[:, t - (CONV_W - 1):, :3 * d_gdn]

    q, k, v, cols, gct = _gdn_prep(proj3, conv_hist, p['w_conv'], p['alog_row'], p['dtb_row'],
                                   chunk, p['ab_block'])
    o_gdn, s_new = _gdn(q, k, v, cols, gct, proj3, p['wn_row'], s_gdn, chunk, p['z_block'])

    n_gb = p['s5_mats'][0].shape[0]
    half = GROUPS_PER_BLOCK * SSM_P
    yg, hre_new, him_new = _s5(proj3, p['s5_mats'], h_re.reshape(b, n_gb, 1, half),
                               h_im.reshape(b, n_gb, 1, half), p['u_block0'])
    o_ssm = _glu(yg.reshape(m, yg.shape[2]), p['w_glu'], p['b_glu_row'], _row_tile(m, 512))

    h, hb, route, counts_row = _mix_ln_route(o_gdn.reshape(m, d_gdn), o_ssm, p['w_out'], p['w_out'], x2,
                                             p['ln1_g'], p['ln1_b'], p['w_router'], p['b_router'], alpha,
                                             _row_tile(m, 256))

    big = m * TOP_K >= 8 * p['moe_tm']
    tm = p['moe_tm'] if big else 128
    pos_flat, tile_expert, n_used = _route_tables(route, counts_row, m, tm)
    cap = tile_expert.shape[0] * tm
    x_rows = _dispatch_rows(pos_flat, hb, cap, _row_tile(m, 256))
    outs = _moe_ffn(tile_expert, n_used, x_rows, p['w_gate'], p['b_gate'],
                    p['w_up'], p['b_up'], p['w_down'], p['b_down'], tm, p['moe_tf'])
    y = _combine_ln(pos_flat, h, route, p['ln2_g'], p['ln2_b'], outs, alpha, _row_tile(m, 128))
    g_all = h_re.shape[1]
    return (y.reshape(b, t, d), conv_new, s_new,
            hre_new.reshape(b, g_all, SSM_P), him_new.reshape(b, g_all, SSM_P))


def _pad_lanes(v, fill=0.0):
    return jnp.pad(v.astype(F32), (0, LANES - v.shape[0]), constant_values=fill).reshape(1, LANES)


def _layer_params(l, w_in, w_conv, a_log, dt_bias, w_onorm, lam_re, lam_im, log_dt, b_re, b_im, c_re, c_im,
                  d_skip, w_glu, b_glu, w_out, ln1_g, ln1_b, w_router, b_router, w_gate, b_gate,
                  w_up, b_up, w_down, b_down, ln2_g, ln2_b):
    d_model = w_in.shape[1]
    d_gdn = GDN_HEADS * HEAD_DIM
    d_qkvz = 4 * d_gdn
    d_ssm = d_model - d_gdn
    wi = w_in[l]
    proj_tn = 1792
    n_cols = d_qkvz + d_ssm + LANES
    n_pad = -(-n_cols // proj_tn) * proj_tn
    w_in_r = jnp.concatenate([wi[:, :d_qkvz], wi[:, d_qkvz + 2 * GDN_HEADS:],
                              wi[:, d_qkvz:d_qkvz + 2 * GDN_HEADS],
                              jnp.zeros((d_model, n_pad - d_qkvz - d_ssm - 2 * GDN_HEADS), wi.dtype)], axis=1)
    wo = w_out[l].astype(BF16)
    wr = jnp.pad(w_router[l].astype(F32), ((0, 0), (0, LANES - N_EXPERTS)))
    return {
        'w_in': w_in_r.astype(BF16), 'proj_tn': proj_tn,
        'z_block': 3, 'u_block0': (d_qkvz) // LANES, 'ab_block': (d_qkvz + d_ssm) // LANES,
        'w_conv': w_conv[l].astype(F32),
        'alog_row': _pad_lanes(a_log[l]), 'dtb_row': _pad_lanes(dt_bias[l]),
        'wn_row': w_onorm[l].astype(F32).reshape(1, HEAD_DIM),
        's5_mats': _s5_matrices(lam_re[l], lam_im[l], log_dt[l], b_re[l], b_im[l], c_re[l], c_im[l], d_skip[l]),
        'w_glu': w_glu[l].astype(BF16), 'b_glu_row': b_glu[l].astype(F32).reshape(1, d_ssm),
        'w_out': wo,
        'ln1_g': ln1_g[l].astype(F32).reshape(1, d_model), 'ln1_b': ln1_b[l].astype(F32).reshape(1, d_model),
        'w_router': wr, 'b_router': _pad_lanes(b_router[l]),
        'w_gate': w_gate[l], 'b_gate': b_gate[l], 'w_up': w_up[l], 'b_up': b_up[l],
        'w_down': w_down[l], 'b_down': b_down[l],
        'ln2_g': ln2_g[l].astype(F32).reshape(1, d_model), 'ln2_b': ln2_b[l].astype(F32).reshape(1, d_model),
        'moe_tm': 512, 'moe_tf': 512,
    }


def kernel(x_prompt, x_sample, state_conv, state_gdn, state_ssm_re, state_ssm_im, w_in, w_conv, a_log, dt_bias, w_onorm, lam_re, lam_im, log_dt, b_re, b_im, c_re, c_im, d_skip, w_glu, b_glu, w_out, ln1_g, ln1_b, w_router, b_router, w_gate, b_gate, w_up, b_up, w_down, b_down, ln2_g, ln2_b):
    depth = w_in.shape[0]
    alpha = (2.0 * depth) ** 0.25
    bp, seq, _ = x_prompt.shape
    chunk_p = 64
    d_qkv = state_conv.shape[-1]
    n_groups, n_p = state_ssm_re.shape[-2:]
    yp, ys = x_prompt, x_sample
    outs_p = [[], [], [], []]
    outs_s = [[], [], [], []]
    for l in range(depth):
        p = _layer_params(l, w_in, w_conv, a_log, dt_bias, w_onorm, lam_re, lam_im, log_dt, b_re, b_im,
                          c_re, c_im, d_skip, w_glu, b_glu, w_out, ln1_g, ln1_b, w_router, b_router,
                          w_gate, b_gate, w_up, b_up, w_down, b_down, ln2_g, ln2_b)
        yp, cp, sp, rp, ip = _layer(
            yp, jnp.zeros((bp, CONV_W - 1, d_qkv), F32),
            jnp.zeros((bp, GDN_HEADS, HEAD_DIM, HEAD_DIM), F32),
            jnp.zeros((bp, n_groups, n_p), F32), jnp.zeros((bp, n_groups, n_p), F32),
            chunk_p, alpha, p)
        ys, cs, ss, rs, is_ = _layer(
            ys, state_conv[l].astype(F32), state_gdn[l].astype(F32),
            state_ssm_re[l].astype(F32), state_ssm_im[l].astype(F32),
            ys.shape[1], alpha, p)
        for acc, val in zip(outs_p, (cp, sp, rp, ip)):
            acc.append(val)
        for acc, val in zip(outs_s, (cs, ss, rs, is_)):
            acc.append(val)
    return (yp, ys, *[jnp.stack(a) for a in outs_p], *[jnp.stack(a) for a in outs_s])
```

```python
import functools
import math

import jax
import jax.numpy as jnp
from jax import lax
from jax.experimental import pallas as pl
from jax.experimental.pallas import tpu as pltpu

F32 = jnp.float32
BF16 = jnp.bfloat16

GDN_HEADS = 8
HEAD_DIM = 128
CONV_W = 4
SSM_CG = 16
SSM_P = 64
N_EXPERTS = 32
TOP_K = 4
SWIGLU_ALPHA = 1.702
SWIGLU_LIMIT = 7.0
LN_EPS = 1e-5
RMS_EPS = 1e-6
L2_EPS = 1e-6

LANES = 128
SUBLANES = 8
S5_SUB = 8
S5_TILE = 2048
GDN_TILE = 128
GROUPS_PER_BLOCK = LANES // SSM_CG
VMEM_LIMIT = 56 * 1024 * 1024

NT_DIMS = (((1,), (1,)), ((), ()))
TN_DIMS = (((0,), (0,)), ((), ()))


def _dot(a, b, dims=(((1,), (0,)), ((), ()))):
    return lax.dot_general(a, b, dims, preferred_element_type=F32)


def _split(a):
    hi = a.astype(BF16)
    lo = (a - hi.astype(F32)).astype(BF16)
    return hi, lo


def _dot_x3(a, b, dims=(((1,), (0,)), ((), ()))):
    ah, al = _split(a)
    bh, bl = _split(b)
    return _dot(ah, bh, dims) + (_dot(ah, bl, dims) + _dot(al, bh, dims))


def _dot_bf(a, b, dims=(((1,), (0,)), ((), ()))):
    return _dot(a.astype(BF16), b.astype(BF16), dims)


_gdn_mm = _dot_bf


def _sigmoid(x):
    return 1.0 / (1.0 + jnp.exp(-x))


ROW_SUB = SUBLANES


def _pack_rows(x, ref, tm):
    half = x.shape[1] // 2
    for s in range(ROW_SUB):
        lo = x[:, s * LANES:(s + 1) * LANES].astype(BF16).astype(F32)
        hi = x[:, half + s * LANES:half + (s + 1) * LANES].astype(BF16).astype(F32)
        word = (lax.bitcast_convert_type(lo, jnp.uint32) >> 16) | lax.bitcast_convert_type(hi, jnp.uint32)
        ref[pl.ds(s, tm, stride=ROW_SUB), :] = word


def _unpack_rows(ref, base, tm):
    lo, hi = [], []
    for s in range(ROW_SUB):
        word = ref[pl.ds(base + s, tm, stride=ROW_SUB), :]
        lo.append(lax.bitcast_convert_type(word << 16, F32))
        hi.append(lax.bitcast_convert_type(word & jnp.uint32(0xFFFF0000), F32))
    return lo, hi


def _cparams(sem):
    return pltpu.CompilerParams(dimension_semantics=sem, vmem_limit_bytes=VMEM_LIMIT)


def _proj_body(x_ref, w_ref, o_ref, xb_ref):
    @pl.when(pl.program_id(1) == 0)
    def _():
        xb_ref[...] = x_ref[...].astype(BF16)

    o_ref[...] = _dot(xb_ref[...], w_ref[...])


def _in_proj(x2, w_bf, tm, tn):
    m, k = x2.shape
    n = w_bf.shape[1]
    return pl.pallas_call(
        _proj_body,
        grid=(m // tm, n // tn),
        in_specs=[pl.BlockSpec((tm, k), lambda i, j: (i, 0)),
                  pl.BlockSpec((k, tn), lambda i, j: (0, j))],
        out_specs=pl.BlockSpec((tm, tn), lambda i, j: (i, j)),
        out_shape=jax.ShapeDtypeStruct((m, n), F32),
        scratch_shapes=[pltpu.VMEM((tm, k), BF16)],
        compiler_params=_cparams(("parallel", "arbitrary")),
        name="in_proj",
    )(x2, w_bf)


def _gdn_prep_body(qkv_ref, ab_ref, hist_ref, wc_ref, alog_ref, dtb_ref,
                   q_ref, k_ref, v_ref, cols_ref, gct_ref, xbuf_ref, *, tt, chunk):
    d_gdn = GDN_HEADS * HEAD_DIM
    halo = SUBLANES

    @pl.when(pl.program_id(1) == 0)
    def _():
        xbuf_ref[0:halo, :] = jnp.zeros((halo, 3 * d_gdn), F32)
        xbuf_ref[halo - (CONV_W - 1):halo, :] = hist_ref[0]

    xbuf_ref[halo:halo + tt, :] = qkv_ref[0]

    for part, out_ref in enumerate((q_ref, k_ref, v_ref)):
        c0 = part * d_gdn
        y = None
        for j in range(CONV_W):
            r0 = halo - (CONV_W - 1) + j
            term = xbuf_ref[r0:r0 + tt, c0:c0 + d_gdn] * wc_ref[j:j + 1, c0:c0 + d_gdn]
            y = term if y is None else y + term
        s = y * _sigmoid(y)
        if part == 2:
            out_ref[0] = s
        else:
            for h in range(GDN_HEADS):
                sh = s[:, h * HEAD_DIM:(h + 1) * HEAD_DIM]
                nrm = sh * lax.rsqrt(jnp.sum(sh * sh, axis=-1, keepdims=True) + L2_EPS)
                if part == 0:
                    nrm = nrm * (HEAD_DIM ** -0.5)
                out_ref[0, :, h * HEAD_DIM:(h + 1) * HEAD_DIM] = nrm

    xbuf_ref[0:halo, :] = xbuf_ref[tt:tt + halo, :]

    ab = ab_ref[0]
    lane = lax.broadcasted_iota(jnp.int32, (tt, LANES), 1)
    is_a = lane < GDN_HEADS
    z = ab + dtb_ref[...]
    softplus = jnp.maximum(z, 0.0) + jnp.log1p(jnp.exp(-jnp.abs(z)))
    g = jnp.where(is_a, -jnp.exp(alog_ref[...]) * softplus, 0.0)
    beta = _sigmoid(ab)

    shift = int(math.log2(chunk))
    r = lax.broadcasted_iota(jnp.int32, (tt, tt), 0)
    c = lax.broadcasted_iota(jnp.int32, (tt, tt), 1)
    same = (r >> shift) == (c >> shift)
    m_incl = jnp.where(same & (r >= c), 1.0, 0.0).astype(BF16)
    m_all = jnp.where(same, 1.0, 0.0).astype(BF16)
    gc = _dot_exact_lhs_rhs(m_incl, g)
    glast = _dot_exact_lhs_rhs(m_all, g)
    eg = jnp.exp(gc)
    egl = jnp.exp(glast - gc)
    egt = jnp.exp(glast)
    zero = jnp.zeros_like(gc)
    cols = (jnp.where(is_a, gc, zero)
            + jnp.where((lane >= 8) & (lane < 16), beta, zero)
            + pltpu.roll(jnp.where(is_a, eg, zero), 16, 1)
            + pltpu.roll(jnp.where(is_a, egl, zero), 24, 1)
            + pltpu.roll(jnp.where(is_a, egt, zero), 32, 1))
    cols_ref[0] = cols

    er = lax.broadcasted_iota(jnp.int32, (SUBLANES, LANES), 0)
    ec = lax.broadcasted_iota(jnp.int32, (SUBLANES, LANES), 1)
    sel = jnp.where(er == ec, 1.0, 0.0).astype(BF16)
    gct_ref[0] = _dot_exact_rhs(sel, jnp.where(is_a, gc, zero), NT_DIMS)


def _three_pieces(a):
    a0 = a.astype(BF16)
    r1 = a - a0.astype(F32)
    a1 = r1.astype(BF16)
    a2 = (r1 - a1.astype(F32)).astype(BF16)
    return a0, a1, a2


def _dot_exact_lhs_rhs(mask_bf, a):
    a0, a1, a2 = _three_pieces(a)
    return _dot(mask_bf, a0) + (_dot(mask_bf, a1) + _dot(mask_bf, a2))


def _dot_exact_rhs(mask_bf, a, dims):
    a0, a1, a2 = _three_pieces(a)
    return _dot(mask_bf, a0, dims) + (_dot(mask_bf, a1, dims) + _dot(mask_bf, a2, dims))


def _gdn_prep(proj3, hist, w_conv, alog_p, dtb_p, chunk, ab_block):
    b, t, _ = proj3.shape
    d_gdn = GDN_HEADS * HEAD_DIM
    tt = min(t, GDN_TILE)
    body = functools.partial(_gdn_prep_body, tt=tt, chunk=chunk)
    big = jax.ShapeDtypeStruct((b, t, d_gdn), F32)
    return pl.pallas_call(
        body,
        grid=(b, t // tt),
        in_specs=[pl.BlockSpec((1, tt, 3 * d_gdn), lambda i, j: (i, j, 0)),
                  pl.BlockSpec((1, tt, LANES), lambda i, j: (i, j, ab_block)),
                  pl.BlockSpec((1, CONV_W - 1, 3 * d_gdn), lambda i, j: (i, 0, 0)),
                  pl.BlockSpec((CONV_W, 3 * d_gdn), lambda i, j: (0, 0)),
                  pl.BlockSpec((1, LANES), lambda i, j: (0, 0)),
                  pl.BlockSpec((1, LANES), lambda i, j: (0, 0))],
        out_specs=[pl.BlockSpec((1, tt, d_gdn), lambda i, j: (i, j, 0)),
                   pl.BlockSpec((1, tt, d_gdn), lambda i, j: (i, j, 0)),
                   pl.BlockSpec((1, tt, d_gdn), lambda i, j: (i, j, 0)),
                   pl.BlockSpec((1, tt, LANES), lambda i, j: (i, j, 0)),
                   pl.BlockSpec((1, SUBLANES, tt), lambda i, j: (i, 0, j))],
        out_shape=[big, big, big,
                   jax.ShapeDtypeStruct((b, t, LANES), F32),
                   jax.ShapeDtypeStruct((b, SUBLANES, t), F32)],
        scratch_shapes=[pltpu.VMEM((tt + SUBLANES, 3 * d_gdn), F32)],
        compiler_params=_cparams(("parallel", "arbitrary")),
        name="gdn_prep",
    )(proj3, proj3, hist, w_conv, alog_p, dtb_p)


def _unit_lower_inverses(mats, r, c, chunk):
    base = 16
    eye = jnp.where(r == c, 1.0, 0.0)

    def blk(bs):
        s = int(math.log2(bs))
        return (r >> s) == (c >> s)

    d1 = [jnp.where(blk(base), a, 0.0) for a in mats]
    d2 = [_gdn_mm(x, x) for x in d1]
    d4 = [_gdn_mm(x, x) for x in d2]
    d8 = [_gdn_mm(x, x) for x in d4]
    t = [eye - x for x in d1]
    t = [x + _gdn_mm(x, y) for x, y in zip(t, d2)]
    t = [x + _gdn_mm(x, y) for x, y in zip(t, d4)]
    t = [x + _gdn_mm(x, y) for x, y in zip(t, d8)]
    bs = base
    while bs < chunk:
        off_mask = blk(2 * bs) & jnp.logical_not(blk(bs))
        inner = [_gdn_mm(jnp.where(off_mask, a, 0.0), x) for a, x in zip(mats, t)]
        t = [x - _gdn_mm(x, y) for x, y in zip(t, inner)]
        bs *= 2
    return t


def _gdn_body(q_ref, k_ref, v_ref, cols_ref, gct_ref, z_ref, wn_ref, s0_ref,
              o_ref, s_ref, *, tb, chunk):
    @pl.when(pl.program_id(1) == 0)
    def _():
        s_ref[...] = s0_ref[...]

    shift = int(math.log2(chunk))
    r = lax.broadcasted_iota(jnp.int32, (tb, tb), 0)
    c = lax.broadcasted_iota(jnp.int32, (tb, tb), 1)
    same = (r >> shift) == (c >> shift)
    incl = same & (r >= c)
    strict = same & (r > c)
    n_chunks = tb // chunk

    heads = range(GDN_HEADS)
    hsl = [slice(h * HEAD_DIM, (h + 1) * HEAD_DIM) for h in heads]
    q = [q_ref[0, :, s] for s in hsl]
    k = [k_ref[0, :, s] for s in hsl]
    beta = [cols_ref[0, :, 8 + h:9 + h] for h in heads]
    eg = [cols_ref[0, :, 16 + h:17 + h] for h in heads]
    egl = [cols_ref[0, :, 24 + h:25 + h] for h in heads]
    decay = [jnp.exp(jnp.where(incl, cols_ref[0, :, h:h + 1] - gct_ref[0, h:h + 1, :], -jnp.inf)) for h in heads]
    kb = [k[h] * beta[h] for h in heads]
    k_bf = [x.astype(BF16) for x in k]
    a = [jnp.where(strict, _dot(kb[h].astype(BF16), k_bf[h], NT_DIMS) * decay[h], 0.0) for h in heads]
    tinv = _unit_lower_inverses(a, r, c, chunk)
    sol = [_gdn_mm(tinv[h], jnp.concatenate([v_ref[0, :, hsl[h]] * beta[h], kb[h] * eg[h]], axis=1)) for h in heads]
    u = [x[:, :HEAD_DIM] for x in sol]
    w_bf = [x[:, HEAD_DIM:].astype(BF16) for x in sol]
    attn = [(_dot(q[h].astype(BF16), k_bf[h], NT_DIMS) * decay[h]).astype(BF16) for h in heads]
    q_dec = [(q[h] * eg[h]).astype(BF16) for h in heads]
    k_dec = [(k[h] * egl[h]).astype(BF16) for h in heads]

    s = [s_ref[0, h] for h in heads]
    v_new = [[] for _ in heads]
    o_state = [[] for _ in heads]
    for ci in range(n_chunks):
        rs = slice(ci * chunk, (ci + 1) * chunk)
        s_bf = [x.astype(BF16) for x in s]
        vn = [u[h][rs] - _dot(w_bf[h][rs], s_bf[h]) for h in heads]
        for h in heads:
            o_state[h].append(_dot(q_dec[h][rs], s_bf[h]))
            v_new[h].append(vn[h])
        s = [s[h] * cols_ref[0, ci * chunk:ci * chunk + 1, 32 + h:33 + h]
             + _dot(k_dec[h][rs], vn[h].astype(BF16), TN_DIMS) for h in heads]
    for h in heads:
        s_ref[0, h] = s[h]

    def cat(parts):
        return parts[0] if len(parts) == 1 else jnp.concatenate(parts, axis=0)

    for h in heads:
        o = cat(o_state[h]) + _dot(attn[h], cat(v_new[h]).astype(BF16))
        zh = z_ref[0, :, hsl[h]]
        o = (o * lax.rsqrt(jnp.mean(o * o, axis=-1, keepdims=True) + RMS_EPS) * wn_ref[...]
             * (zh * _sigmoid(zh)))
        o_ref[0, :, hsl[h]] = o.astype(BF16)


def _gdn(q, k, v, cols, gct, proj3, wn, s0, chunk, z_block):
    b, t, d_gdn = q.shape
    tb = min(t, GDN_TILE)
    body = functools.partial(_gdn_body, tb=tb, chunk=chunk)
    tile = lambda i, j: (i, j, 0)
    return pl.pallas_call(
        body,
        grid=(b, t // tb),
        in_specs=[pl.BlockSpec((1, tb, d_gdn), tile),
                  pl.BlockSpec((1, tb, d_gdn), tile),
                  pl.BlockSpec((1, tb, d_gdn), tile),
                  pl.BlockSpec((1, tb, LANES), tile),
                  pl.BlockSpec((1, SUBLANES, tb), lambda i, j: (i, 0, j)),
                  pl.BlockSpec((1, tb, d_gdn), lambda i, j: (i, j, z_block)),
                  pl.BlockSpec((1, HEAD_DIM), lambda i, j: (0, 0)),
                  pl.BlockSpec((1, GDN_HEADS, HEAD_DIM, HEAD_DIM), lambda i, j: (i, 0, 0, 0))],
        out_specs=[pl.BlockSpec((1, tb, d_gdn), tile),
                   pl.BlockSpec((1, GDN_HEADS, HEAD_DIM, HEAD_DIM), lambda i, j: (i, 0, 0, 0))],
        out_shape=[jax.ShapeDtypeStruct((b, t, d_gdn), BF16),
                   jax.ShapeDtypeStruct((b, GDN_HEADS, HEAD_DIM, HEAD_DIM), F32)],
        compiler_params=_cparams(("parallel", "arbitrary")),
        name="gdn_delta",
    )(q, k, v, cols, gct, proj3, wn, s0)


def _gelu_tanh(x):
    return 0.5 * x * (1.0 + jnp.tanh(math.sqrt(2.0 / math.pi) * (x + 0.044715 * (x * x * x))))


def _s5_body(u_ref, wm_ref, ym0_ref, ym1_ref, ym2_ref, ym3_ref, lre_ref, lim_ref, dsk_ref, h0re_ref, h0im_ref,
             y_ref, hre_ref, him_ref, xbuf_ref, hbuf_ref, *, n_sub, seg):
    half = GROUPS_PER_BLOCK * SSM_P

    if seg is None:
        @pl.when(pl.program_id(2) == 0)
        def _():
            hre_ref[...] = h0re_ref[...]
            him_ref[...] = h0im_ref[...]

    u_f = [u_ref[0, pl.ds(j, n_sub, stride=S5_SUB), :] for j in range(S5_SUB)]
    u_b = [x.astype(BF16) for x in u_f]

    xbuf_ref[...] = _dot(jnp.concatenate(u_b, axis=1), wm_ref[0])

    lre = lre_ref[0]
    lim = lim_ref[0]

    def advance(n, hre, him):
        hbuf_ref[pl.ds(n, 1), 0:half] = hre
        hbuf_ref[pl.ds(n, 1), half:2 * half] = him
        xr = xbuf_ref[pl.ds(n, 1), 0:half]
        xi = xbuf_ref[pl.ds(n, 1), half:2 * half]
        return (lre * hre - lim * him + xr, lre * him + lim * hre + xi)

    if seg is None:
        hre, him = lax.fori_loop(0, n_sub, lambda n, c: advance(n, *c), (hre_ref[0, 0], him_ref[0, 0]))
        hre_ref[0, 0] = hre
        him_ref[0, 0] = him
    else:
        def step(n, carry):
            q = n // seg
            first = n - q * seg == 0
            hre = jnp.where(first, h0re_ref[q, 0], carry[0])
            him = jnp.where(first, h0im_ref[q, 0], carry[1])
            hre, him = advance(n, hre, him)
            hre_ref[q, 0] = hre
            him_ref[q, 0] = him
            return hre, him

        lax.fori_loop(0, n_sub, step, (h0re_ref[0, 0], h0im_ref[0, 0]))

    h_b = hbuf_ref[...].astype(BF16)
    dsk = dsk_ref[0]
    for l, ym_ref in zip(range(0, S5_SUB, 2), (ym0_ref, ym1_ref, ym2_ref, ym3_ref)):
        lhs = jnp.concatenate([h_b] + [u_b[m] for m in range(l + 1, -1, -1)], axis=1)
        y2 = _dot(lhs, ym_ref[0])
        for o in range(2):
            y = y2[:, o * LANES:(o + 1) * LANES] + dsk * u_f[l + o]
            y_ref[0, pl.ds(l + o, n_sub, stride=S5_SUB), :] = _gelu_tanh(y)


def _s5(proj3, mats, h0re, h0im, u_block0):
    b, t, width = proj3.shape
    n_gb = mats[0].shape[0]
    if b > 1 and b * t <= S5_TILE:
        y, hre, him = _s5_call(proj3.reshape(1, b * t, width), mats, h0re, h0im, u_block0,
                               tt=b * t, seg=t // S5_SUB, state_rows=b)
        return y.reshape(b, t, n_gb * LANES), hre, him
    return _s5_call(proj3, mats, h0re, h0im, u_block0, tt=min(t, S5_TILE), seg=None, state_rows=1)


def _s5_call(proj3, mats, h0re, h0im, u_block0, tt, seg, state_rows):
    wcat, ycat, lre, lim, dsk = mats
    b, t, _ = proj3.shape
    n_gb = wcat.shape[0]
    n_sub = tt // S5_SUB
    half = GROUPS_PER_BLOCK * SSM_P
    body = functools.partial(_s5_body, n_sub=n_sub, seg=seg)
    state_spec = pl.BlockSpec((state_rows, 1, 1, half), lambda g, i, j: (i, g, 0, 0))
    par_spec = pl.BlockSpec((1, 1, half), lambda g, i, j: (g, 0, 0))
    mat_spec = lambda a: pl.BlockSpec((1,) + a.shape[1:], lambda g, i, j: (g, 0, 0))
    return pl.pallas_call(
        body,
        grid=(n_gb, b, t // tt),
        in_specs=[pl.BlockSpec((1, tt, LANES), lambda g, i, j: (i, j, u_block0 + g)),
                  mat_spec(wcat)] + [mat_spec(a) for a in ycat] + [
                  par_spec, par_spec,
                  pl.BlockSpec((1, 1, LANES), lambda g, i, j: (g, 0, 0)),
                  state_spec, state_spec],
        out_specs=[pl.BlockSpec((1, tt, LANES), lambda g, i, j: (i, j, g)),
                   state_spec, state_spec],
        out_shape=[jax.ShapeDtypeStruct((b, t, n_gb * LANES), F32),
                   jax.ShapeDtypeStruct(h0re.shape, F32),
                   jax.ShapeDtypeStruct(h0im.shape, F32)],
        scratch_shapes=[pltpu.VMEM((n_sub, 2 * half), F32),
                        pltpu.VMEM((n_sub, 2 * half), F32)],
        compiler_params=_cparams(("parallel", "parallel", "arbitrary")),
        name="s5_scan",
    )(proj3, wcat, *ycat, lre, lim, dsk, h0re, h0im)


def _s5_matrices(lam_re, lam_im, log_dt, b_re, b_im, c_re, c_im, d_skip):
    g, p = lam_re.shape
    n_gb = g // GROUPS_PER_BLOCK
    gpb = GROUPS_PER_BLOCK
    dt = jnp.exp(log_dt.astype(F32))
    lam = lax.complex(jnp.minimum(lam_re.astype(F32), -1e-4), lam_im.astype(F32))
    lam_bar = jnp.exp(lam * dt[:, None])
    b_bar = ((lam_bar - 1.0) / lam)[..., None] * lax.complex(b_re.astype(F32), b_im.astype(F32))
    c_c = lax.complex(c_re.astype(F32), c_im.astype(F32))
    pows = [jnp.ones_like(lam_bar)]
    for _ in range(S5_SUB):
        pows.append(pows[-1] * lam_bar)
    pw = jnp.stack(pows)

    def block_diag(x):
        k = x.shape[-1]
        lanes = jnp.arange(gpb * k)
        rep = (lanes[None, :] % k == jnp.arange(k)[:, None]).astype(F32)
        wide = jnp.einsum('...k,kl->...l', x, rep, precision=lax.Precision.HIGHEST)
        own = (lanes[None, None, :] // k) == jnp.arange(gpb)[:, None, None]
        wide = jnp.where(own, wide, 0.0)
        return wide.reshape(x.shape[:-3] + (gpb * x.shape[-2], gpb * k))

    kd = jnp.real(jnp.einsum('gop,dgp,gpi->dgio', c_c, pw[:S5_SUB], b_bar))
    km = block_diag(kd.reshape(S5_SUB, n_gb, gpb, SSM_CG, SSM_CG))

    wj = pw[:S5_SUB][::-1][:, :, :, None] * b_bar[None]
    wj = jnp.transpose(wj.reshape(S5_SUB, n_gb, gpb, p, SSM_CG), (1, 0, 2, 4, 3))
    wcat = jnp.concatenate([block_diag(jnp.real(wj)), block_diag(jnp.imag(wj))], axis=-1)
    wcat = wcat.reshape(n_gb, S5_SUB * LANES, 2 * gpb * p).astype(BF16)

    cl = c_c[None] * pw[1:S5_SUB + 1][:, :, None, :]
    cl = jnp.transpose(cl, (0, 1, 3, 2)).reshape(S5_SUB, n_gb, gpb, p, SSM_CG)
    vm = jnp.concatenate([block_diag(jnp.real(cl)), block_diag(-jnp.imag(cl))], axis=-2)

    lam_s = pw[S5_SUB].reshape(n_gb, 1, gpb * p)
    dsk = d_skip.astype(F32).reshape(n_gb, 1, LANES)

    km, vm = km.astype(BF16), vm.astype(BF16)
    ycat = []
    zero = jnp.zeros_like(km[0])
    for l in range(0, S5_SUB, 2):
        rows = [jnp.concatenate([vm[l], vm[l + 1]], axis=-1)]
        for m in range(l + 1, -1, -1):
            left = km[l - m] if m <= l else zero
            rows.append(jnp.concatenate([left, km[l + 1 - m]], axis=-1))
        ycat.append(jnp.concatenate(rows, axis=-2))
    return (wcat, tuple(ycat), jnp.real(lam_s), jnp.imag(lam_s), dsk)


def _glu_body(y_ref, w_ref, b_ref, o_ref):
    y = y_ref[...]
    gate = _dot(y.astype(BF16), w_ref[...]) + b_ref[...]
    o_ref[...] = (y * _sigmoid(gate)).astype(BF16)


def _glu(y2, w_bf, b_row, tm):
    m, d = y2.shape
    return pl.pallas_call(
        _glu_body,
        grid=(m // tm,),
        in_specs=[pl.BlockSpec((tm, d), lambda i: (i, 0)),
                  pl.BlockSpec((d, d), lambda i: (0, 0)),
                  pl.BlockSpec((1, d), lambda i: (0, 0))],
        out_specs=pl.BlockSpec((tm, d), lambda i: (i, 0)),
        out_shape=jax.ShapeDtypeStruct((m, d), BF16),
        compiler_params=_cparams(("parallel",)),
        name="s5_glu",
    )(y2, w_bf, b_row)


def _layernorm(v, g, b):
    mu = jnp.mean(v, axis=-1, keepdims=True)
    var = jnp.mean(jnp.square(v - mu), axis=-1, keepdims=True)
    return (v - mu) * lax.rsqrt(var + LN_EPS) * g + b


def _mix_body(og_ref, os_ref, wa_ref, wb_ref, x_ref, g_ref, b_ref, wr_ref, br_ref,
              h_ref, hb_ref, route_ref, cnt_ref, *, alpha):
    @pl.when(pl.program_id(0) == 0)
    def _():
        cnt_ref[...] = jnp.zeros_like(cnt_ref)

    mix = _dot(og_ref[...], wa_ref[...]) + _dot(os_ref[...], wb_ref[...])
    h = _layernorm(alpha * x_ref[...] + mix, g_ref[...], b_ref[...])
    h_ref[...] = h
    tm = h.shape[0]
    _pack_rows(h, hb_ref, tm)

    logits = _dot_x3(h, wr_ref[...]) + br_ref[...]
    lane = lax.broadcasted_iota(jnp.int32, (tm, LANES), 1)
    work = jnp.where(lane < N_EXPERTS, logits, -jnp.inf)
    vals, idxs = [], []
    for _ in range(TOP_K):
        mx = jnp.max(work, axis=-1, keepdims=True)
        ix = jnp.min(jnp.where(work == mx, lane, LANES), axis=-1, keepdims=True)
        vals.append(mx)
        idxs.append(ix)
        work = jnp.where(lane == ix, -jnp.inf, work)
    exps = [jnp.exp(v - vals[0]) for v in vals]
    denom = exps[0]
    for e in exps[1:]:
        denom = denom + e
    chosen = jnp.zeros((tm, LANES), F32)
    for k in range(TOP_K):
        chosen = jnp.where(lane == idxs[k], 1.0, chosen)
    rr = lax.broadcasted_iota(jnp.int32, (tm, tm), 0)
    cc = lax.broadcasted_iota(jnp.int32, (tm, tm), 1)
    lower = jnp.where(rr > cc, 1.0, 0.0).astype(BF16)
    before = cnt_ref[...] + _dot(lower, chosen.astype(BF16))
    cnt_ref[...] = cnt_ref[...] + jnp.sum(chosen, axis=0, keepdims=True)

    route = jnp.zeros((tm, LANES), F32)
    for k in range(TOP_K):
        rank = jnp.sum(jnp.where(lane == idxs[k], before, 0.0), axis=-1, keepdims=True)
        route = jnp.where(lane == k, idxs[k].astype(F32), route)
        route = jnp.where(lane == TOP_K + k, exps[k] / denom, route)
        route = jnp.where(lane == 2 * TOP_K + k, rank, route)
    route_ref[...] = route


def _mix_ln_route(og, osm, wa, wb, x2, g_row, b_row, wr, br, alpha, tm):
    m, d = x2.shape
    dh = og.shape[1]
    body = functools.partial(_mix_body, alpha=alpha)
    row = lambda i: (i, 0)
    fix = lambda i: (0, 0)
    return pl.pallas_call(
        body,
        grid=(m // tm,),
        in_specs=[pl.BlockSpec((tm, dh), row), pl.BlockSpec((tm, dh), row),
                  pl.BlockSpec((dh, d), fix), pl.BlockSpec((dh, d), lambda i: (1, 0)),
                  pl.BlockSpec((tm, d), row),
                  pl.BlockSpec((1, d), fix), pl.BlockSpec((1, d), fix),
                  pl.BlockSpec((d, LANES), fix), pl.BlockSpec((1, LANES), fix)],
        out_specs=[pl.BlockSpec((tm, d), row), pl.BlockSpec((tm * ROW_SUB, LANES), row),
                   pl.BlockSpec((tm, LANES), row), pl.BlockSpec((1, LANES), fix)],
        out_shape=[jax.ShapeDtypeStruct((m, d), F32),
                   jax.ShapeDtypeStruct((m * ROW_SUB, LANES), jnp.uint32),
                   jax.ShapeDtypeStruct((m, LANES), F32),
                   jax.ShapeDtypeStruct((1, LANES), F32)],
        compiler_params=_cparams(("arbitrary",)),
        name="mix_ln_route",
    )(og, osm, wa, wb, x2, g_row, b_row, wr, br)


def _dispatch_body(pos_ref, src_ref, init_ref, dst_ref, sem, *, tt):
    del init_ref

    def issue(t, carry):
        srow = pl.multiple_of(t * ROW_SUB, ROW_SUB)
        for k in range(TOP_K):
            drow = pl.multiple_of(pos_ref[t * TOP_K + k] * ROW_SUB, ROW_SUB)
            pltpu.make_async_copy(src_ref.at[pl.ds(srow, ROW_SUB)], dst_ref.at[pl.ds(drow, ROW_SUB)],
                                  sem).start(priority=k % 2)
        return carry

    lax.fori_loop(0, tt, issue, 0, unroll=4)
    n = tt * ROW_SUB
    for _ in range(TOP_K):
        pltpu.make_async_copy(src_ref.at[pl.ds(0, n)], dst_ref.at[pl.ds(0, n)], sem).wait()


def _dispatch_rows(pos_flat, hbp, init, tt):
    m = hbp.shape[0] // ROW_SUB
    body = functools.partial(_dispatch_body, tt=tt)
    return pl.pallas_call(
        body,
        grid=(m // tt,),
        in_specs=[pl.BlockSpec((tt * TOP_K,), lambda i: (i,), memory_space=pltpu.SMEM),
                  pl.BlockSpec((tt * ROW_SUB, LANES), lambda i: (i, 0)), pl.BlockSpec(memory_space=pl.ANY)],
        out_specs=pl.BlockSpec(memory_space=pl.ANY),
        out_shape=jax.ShapeDtypeStruct(init.shape, jnp.uint32),
        scratch_shapes=[pltpu.SemaphoreType.DMA(())],
        input_output_aliases={2: 0},
        compiler_params=_cparams(("arbitrary",)),
        name="moe_dispatch",
    )(pos_flat, hbp, init)


MOE_COL = 256
MOE_GROUP = 2


def _moe_body(te_ref, nu_ref, x_ref, wg_ref, bg_ref, wu_ref, bu_ref, wd_ref, bd_ref,
              o_ref, xb_ref, acc_ref, *, n_f, tm, tf):
    g = pl.program_id(2)
    f = pl.program_id(1)
    used = pl.program_id(0) * MOE_GROUP + g < nu_ref[0]
    half = xb_ref.shape[2] // 2
    xb_ref = xb_ref.at[g]
    acc_ref = acc_ref.at[g]

    @pl.when(used & (f == 0))
    def _():
        lo, hi = _unpack_rows(x_ref, 0, tm)
        for r in range(ROW_SUB):
            xb_ref[:, r * LANES:(r + 1) * LANES] = lo[r].astype(BF16)
            xb_ref[:, half + r * LANES:half + (r + 1) * LANES] = hi[r].astype(BF16)

    @pl.when(used)
    def _():
        xb = xb_ref[...]
        part = None
        for c in range(tf // MOE_COL):
            cs = slice(c * MOE_COL, (c + 1) * MOE_COL)
            hg = jnp.minimum(_dot(xb, wg_ref[0, :, cs].astype(BF16)) + bg_ref[0, :, cs], SWIGLU_LIMIT)
            hu = jnp.clip(_dot(xb, wu_ref[0, :, cs].astype(BF16)) + bu_ref[0, :, cs], -SWIGLU_LIMIT, SWIGLU_LIMIT)
            hh = ((hu + 1.0) * (hg * _sigmoid(SWIGLU_ALPHA * hg))).astype(BF16)
            p = _dot(hh, wd_ref[0, cs, :].astype(BF16))
            part = p if part is None else part + p

        @pl.when(f == 0)
        def _():
            acc_ref[...] = part

        @pl.when(f != 0)
        def _():
            acc_ref[...] += part

        @pl.when(f == n_f - 1)
        def _():
            _pack_rows(acc_ref[...] + bd_ref[0], o_ref, tm)

    @pl.when(jnp.logical_not(used) & (f == n_f - 1))
    def _():
        o_ref[...] = jnp.zeros_like(o_ref)


def _moe_ffn(tile_expert, n_used, x_rows, w_gate, b_gate, w_up, b_up, w_down, b_down, tm, tf):
    n_e, d, d_ff = w_gate.shape
    n_tiles = x_rows.shape[0] // (tm * ROW_SUB)
    n_f = d_ff // tf
    grp = MOE_GROUP
    assert n_tiles % grp == 0
    body = functools.partial(_moe_body, n_f=n_f, tm=tm, tf=tf)

    def tile(p, g):
        return p * grp + g

    def fcol(p, f, nu):
        return jnp.where(p * grp < nu[0], f, n_f - 1)

    def xrow(p, f, g):
        return jnp.where(f == 0, tile(p, g), tile(p, grp - 1))

    def orow(p, f, g):
        return jnp.where(f == n_f - 1, tile(p, g), tile(p, 0))

    grid_spec = pltpu.PrefetchScalarGridSpec(
        num_scalar_prefetch=2,
        grid=(n_tiles // grp, n_f, grp),
        in_specs=[pl.BlockSpec((tm * ROW_SUB, LANES), lambda p, f, g, te, nu: (xrow(p, f, g), 0)),
                  pl.BlockSpec((1, d, tf), lambda p, f, g, te, nu: (te[tile(p, g)], 0, fcol(p, f, nu))),
                  pl.BlockSpec((1, 1, tf), lambda p, f, g, te, nu: (te[tile(p, g)], 0, fcol(p, f, nu))),
                  pl.BlockSpec((1, d, tf), lambda p, f, g, te, nu: (te[tile(p, g)], 0, fcol(p, f, nu))),
                  pl.BlockSpec((1, 1, tf), lambda p, f, g, te, nu: (te[tile(p, g)], 0, fcol(p, f, nu))),
                  pl.BlockSpec((1, tf, d), lambda p, f, g, te, nu: (te[tile(p, g)], fcol(p, f, nu), 0)),
                  pl.BlockSpec((1, 1, d), lambda p, f, g, te, nu: (te[tile(p, g)], 0, 0))],
        out_specs=pl.BlockSpec((tm * ROW_SUB, LANES), lambda p, f, g, te, nu: (orow(p, f, g), 0)),
        scratch_shapes=[pltpu.VMEM((grp, tm, d), BF16), pltpu.VMEM((grp, tm, d), F32)],
    )
    return pl.pallas_call(
        body,
        grid_spec=grid_spec,
        out_shape=jax.ShapeDtypeStruct(x_rows.shape, jnp.uint32),
        compiler_params=_cparams(("arbitrary", "arbitrary", "arbitrary")),
        name="moe_ffn",
    )(tile_expert, n_used, x_rows, w_gate, b_gate.reshape(n_e, 1, d_ff),
      w_up, b_up.reshape(n_e, 1, d_ff), w_down, b_down.reshape(n_e, 1, d))


def _final_body(pos_cur_ref, pos_nxt_ref, h_ref, route_ref, g_ref, b_ref, src_ref,
                o_ref, buf_ref, sem, *, alpha, tt, n_steps):
    i = pl.program_id(0)
    slot_rows = tt * TOP_K * ROW_SUB

    def issue(pref, slot):
        def one(t, carry):
            for k in range(TOP_K):
                srow = pl.multiple_of(pref[t * TOP_K + k] * ROW_SUB, ROW_SUB)
                drow = pl.multiple_of(slot * slot_rows + (k * tt + t) * ROW_SUB, ROW_SUB)
                pltpu.make_async_copy(src_ref.at[pl.ds(srow, ROW_SUB)], buf_ref.at[pl.ds(drow, ROW_SUB)],
                                      sem.at[slot]).start(priority=k % 2)
            return carry
        lax.fori_loop(0, tt, one, 0, unroll=4)

    @pl.when(i == 0)
    def _():
        issue(pos_cur_ref, 0)

    @pl.when(i + 1 < n_steps)
    def _():
        issue(pos_nxt_ref, (i + 1) % 2)

    slot = i % 2
    base = pl.multiple_of(slot * slot_rows, ROW_SUB)
    pltpu.make_async_copy(src_ref.at[pl.ds(0, slot_rows)], buf_ref.at[pl.ds(base, slot_rows)], sem.at[slot]).wait()

    ff_lo = [None] * ROW_SUB
    ff_hi = [None] * ROW_SUB
    for k in range(TOP_K):
        gate = route_ref[:, TOP_K + k:TOP_K + k + 1]
        lo, hi = _unpack_rows(buf_ref, base + k * tt * ROW_SUB, tt)
        for s in range(ROW_SUB):
            ff_lo[s] = lo[s] * gate if k == 0 else ff_lo[s] + lo[s] * gate
            ff_hi[s] = hi[s] * gate if k == 0 else ff_hi[s] + hi[s] * gate
    ff = jnp.concatenate(ff_lo + ff_hi, axis=1)
    o_ref[...] = _layernorm(alpha * h_ref[...] + ff, g_ref[...], b_ref[...])


def _combine_ln(pos_flat, h, route, g_row, b_row, outs, alpha, tt):
    m, d = h.shape
    n_steps = m // tt
    body = functools.partial(_final_body, alpha=alpha, tt=tt, n_steps=n_steps)
    row = lambda i: (i, 0)
    fix = lambda i: (0, 0)
    return pl.pallas_call(
        body,
        grid=(n_steps,),
        in_specs=[pl.BlockSpec((tt * TOP_K,), lambda i: (i,), memory_space=pltpu.SMEM),
                  pl.BlockSpec((tt * TOP_K,), lambda i: (jnp.minimum(i + 1, n_steps - 1),),
                               memory_space=pltpu.SMEM),
                  pl.BlockSpec((tt, d), row), pl.BlockSpec((tt, LANES), row),
                  pl.BlockSpec((1, d), fix), pl.BlockSpec((1, d), fix),
                  pl.BlockSpec(memory_space=pl.ANY)],
        out_specs=pl.BlockSpec((tt, d), row),
        out_shape=jax.ShapeDtypeStruct((m, d), F32),
        scratch_shapes=[pltpu.VMEM((2 * tt * TOP_K * ROW_SUB, LANES), jnp.uint32),
                        pltpu.SemaphoreType.DMA((2,))],
        compiler_params=_cparams(("arbitrary",)),
        name="combine_ln",
    )(pos_flat, pos_flat, h, route, g_row, b_row, outs)


def _row_tile(m, pref):
    t = min(m, pref)
    while m % t:
        t //= 2
    return t


def _route_tables(routes, counts_rows, tm):
    counts = [c[0, :N_EXPERTS].astype(jnp.int32) for c in counts_rows]
    total = functools.reduce(lambda a, b: a + b, counts)
    padded = (total + tm - 1) // tm * tm
    pends = jnp.cumsum(padded)
    pstarts = pends - padded
    pos, before = [], jnp.zeros_like(total)
    for route, cnt in zip(routes, counts):
        e_tok = route[:, :TOP_K].astype(jnp.int32)
        rank = route[:, 2 * TOP_K:3 * TOP_K].astype(jnp.int32)
        pos.append(((pstarts + before)[e_tok] + rank).reshape(-1))
        before = before + cnt
    n_assign = sum(r.shape[0] for r in routes) * TOP_K
    n_tiles = -(-n_assign // tm) + N_EXPERTS
    n_tiles = -(-n_tiles // MOE_GROUP) * MOE_GROUP
    tile_start = jnp.arange(n_tiles, dtype=jnp.int32) * tm
    tile_expert = jnp.minimum(jnp.sum((pends[None, :] <= tile_start[:, None]).astype(jnp.int32), axis=1),
                              N_EXPERTS - 1)
    n_used = (pends[-1] // tm).astype(jnp.int32).reshape(1)
    last_used = jnp.maximum(n_used[0] - 1, 0)
    tile_expert = jnp.where(tile_start // tm < n_used[0], tile_expert, tile_expert[last_used])
    return pos, tile_expert, n_used


def _moe_joint(parts, alpha, p):
    tm = p['moe_tm']
    pos, tile_expert, n_used = _route_tables([q['route'] for q in parts], [q['counts'] for q in parts], tm)
    cap = tile_expert.shape[0] * tm
    x_rows = jnp.zeros((cap * ROW_SUB, LANES), jnp.uint32)
    for q, pq in zip(parts, pos):
        x_rows = _dispatch_rows(pq, q['hb'], x_rows, _row_tile(q['h'].shape[0], 256))
    outs = _moe_ffn(tile_expert, n_used, x_rows, p['w_gate'], p['b_gate'],
                    p['w_up'], p['b_up'], p['w_down'], p['b_down'], tm, p['moe_tf'])
    return [_combine_ln(pq, q['h'], q['route'], p['ln2_g'], p['ln2_b'], outs, alpha, _row_tile(q['h'].shape[0], 128))
            for q, pq in zip(parts, pos)]


def _layer_pre(x, conv_hist, s_gdn, h_re, h_im, chunk, alpha, p):
    b, t, d = x.shape
    m = b * t
    d_gdn = GDN_HEADS * HEAD_DIM
    x2 = x.reshape(m, d)

    proj = _in_proj(x2, p['w_in'], _row_tile(m, 1024), p['proj_tn'])
    proj3 = proj.reshape(b, t, proj.shape[1])
    conv_new = proj3[:, t - (CONV_W - 1):, :3 * d_gdn]

    q, k, v, cols, gct = _gdn_prep(proj3, conv_hist, p['w_conv'], p['alog_row'], p['dtb_row'],
                                   chunk, p['ab_block'])
    o_gdn, s_new = _gdn(q, k, v, cols, gct, proj3, p['wn_row'], s_gdn, chunk, p['z_block'])

    n_gb = p['s5_mats'][0].shape[0]
    half = GROUPS_PER_BLOCK * SSM_P
    yg, hre_new, him_new = _s5(proj3, p['s5_mats'], h_re.reshape(b, n_gb, 1, half),
                               h_im.reshape(b, n_gb, 1, half), p['u_block0'])
    o_ssm = _glu(yg.reshape(m, yg.shape[2]), p['w_glu'], p['b_glu_row'], _row_tile(m, 512))

    h, hb, route, counts_row = _mix_ln_route(o_gdn.reshape(m, d_gdn), o_ssm, p['w_out'], p['w_out'], x2,
                                             p['ln1_g'], p['ln1_b'], p['w_router'], p['b_router'], alpha,
                                             _row_tile(m, 256))

    g_all = h_re.shape[1]
    return {'h': h, 'hb': hb, 'route': route, 'counts': counts_row, 'shape': (b, t, d),
            'state': (conv_new, s_new, hre_new.reshape(b, g_all, SSM_P), him_new.reshape(b, g_all, SSM_P))}


def _pad_lanes(v, fill=0.0):
    return jnp.pad(v.astype(F32), (0, LANES - v.shape[0]), constant_values=fill).reshape(1, LANES)


def _layer_params(l, w_in, w_conv, a_log, dt_bias, w_onorm, lam_re, lam_im, log_dt, b_re, b_im, c_re, c_im,
                  d_skip, w_glu, b_glu, w_out, ln1_g, ln1_b, w_router, b_router, w_gate, b_gate,
                  w_up, b_up, w_down, b_down, ln2_g, ln2_b):
    d_model = w_in.shape[1]
    d_gdn = GDN_HEADS * HEAD_DIM
    d_qkvz = 4 * d_gdn
    d_ssm = d_model - d_gdn
    wi = w_in[l]
    proj_tn = 1792
    n_cols = d_qkvz + d_ssm + LANES
    n_pad = -(-n_cols // proj_tn) * proj_tn
    w_in_r = jnp.concatenate([wi[:, :d_qkvz], wi[:, d_qkvz + 2 * GDN_HEADS:],
                              wi[:, d_qkvz:d_qkvz + 2 * GDN_HEADS],
                              jnp.zeros((d_model, n_pad - d_qkvz - d_ssm - 2 * GDN_HEADS), wi.dtype)], axis=1)
    wo = w_out[l].astype(BF16)
    wr = jnp.pad(w_router[l].astype(F32), ((0, 0), (0, LANES - N_EXPERTS)))
    return {
        'w_in': w_in_r.astype(BF16), 'proj_tn': proj_tn,
        'z_block': 3, 'u_block0': (d_qkvz) // LANES, 'ab_block': (d_qkvz + d_ssm) // LANES,
        'w_conv': w_conv[l].astype(F32),
        'alog_row': _pad_lanes(a_log[l]), 'dtb_row': _pad_lanes(dt_bias[l]),
        'wn_row': w_onorm[l].astype(F32).reshape(1, HEAD_DIM),
        's5_mats': _s5_matrices(lam_re[l], lam_im[l], log_dt[l], b_re[l], b_im[l], c_re[l], c_im[l], d_skip[l]),
        'w_glu': w_glu[l].astype(BF16), 'b_glu_row': b_glu[l].astype(F32).reshape(1, d_ssm),
        'w_out': wo,
        'ln1_g': ln1_g[l].astype(F32).reshape(1, d_model), 'ln1_b': ln1_b[l].astype(F32).reshape(1, d_model),
        'w_router': wr, 'b_router': _pad_lanes(b_router[l]),
        'w_gate': w_gate[l], 'b_gate': b_gate[l], 'w_up': w_up[l], 'b_up': b_up[l],
        'w_down': w_down[l], 'b_down': b_down[l],
        'ln2_g': ln2_g[l].astype(F32).reshape(1, d_model), 'ln2_b': ln2_b[l].astype(F32).reshape(1, d_model),
        'moe_tm': 512, 'moe_tf': 512,
    }


def kernel(x_prompt, x_sample, state_conv, state_gdn, state_ssm_re, state_ssm_im, w_in, w_conv, a_log, dt_bias, w_onorm, lam_re, lam_im, log_dt, b_re, b_im, c_re, c_im, d_skip, w_glu, b_glu, w_out, ln1_g, ln1_b, w_router, b_router, w_gate, b_gate, w_up, b_up, w_down, b_down, ln2_g, ln2_b):
    depth = w_in.shape[0]
    alpha = (2.0 * depth) ** 0.25
    bp, seq, _ = x_prompt.shape
    chunk_p = 64
    d_qkv = state_conv.shape[-1]
    n_groups, n_p = state_ssm_re.shape[-2:]
    yp, ys = x_prompt, x_sample
    outs_p = [[], [], [], []]
    outs_s = [[], [], [], []]
    for l in range(depth):
        p = _layer_params(l, w_in, w_conv, a_log, dt_bias, w_onorm, lam_re, lam_im, log_dt, b_re, b_im,
                          c_re, c_im, d_skip, w_glu, b_glu, w_out, ln1_g, ln1_b, w_router, b_router,
                          w_gate, b_gate, w_up, b_up, w_down, b_down, ln2_g, ln2_b)
        part_p = _layer_pre(
            yp, jnp.zeros((bp, CONV_W - 1, d_qkv), F32),
            jnp.zeros((bp, GDN_HEADS, HEAD_DIM, HEAD_DIM), F32),
            jnp.zeros((bp, n_groups, n_p), F32), jnp.zeros((bp, n_groups, n_p), F32),
            chunk_p, alpha, p)
        part_s = _layer_pre(
            ys, state_conv[l].astype(F32), state_gdn[l].astype(F32),
            state_ssm_re[l].astype(F32), state_ssm_im[l].astype(F32),
            ys.shape[1], alpha, p)
        yp2, ys2 = _moe_joint([part_p, part_s], alpha, p)
        yp, ys = yp2.reshape(part_p['shape']), ys2.reshape(part_s['shape'])
        for acc, val in zip(outs_p, part_p['state']):
            acc.append(val)
        for acc, val in zip(outs_s, part_s['state']):
            acc.append(val)
    return (yp, ys, *[jnp.stack(a) for a in outs_p], *[jnp.stack(a) for a in outs_s])
```

```python
import functools
import math

import jax
import jax.numpy as jnp
from jax import lax
from jax.experimental import pallas as pl
from jax.experimental.pallas import tpu as pltpu

F32 = jnp.float32
BF16 = jnp.bfloat16

GDN_HEADS = 8
HEAD_DIM = 128
CONV_W = 4
SSM_CG = 16
SSM_P = 64
N_EXPERTS = 32
TOP_K = 4
SWIGLU_ALPHA = 1.702
SWIGLU_LIMIT = 7.0
LN_EPS = 1e-5
RMS_EPS = 1e-6
L2_EPS = 1e-6

LANES = 128
SUBLANES = 8
S5_SUB = 8
S5_TILE = 2048
GDN_TILE = 128
GROUPS_PER_BLOCK = LANES // SSM_CG
VMEM_LIMIT = 56 * 1024 * 1024

NT_DIMS = (((1,), (1,)), ((), ()))
TN_DIMS = (((0,), (0,)), ((), ()))


def _dot(a, b, dims=(((1,), (0,)), ((), ()))):
    return lax.dot_general(a, b, dims, preferred_element_type=F32)


def _split(a):
    hi = a.astype(BF16)
    lo = (a - hi.astype(F32)).astype(BF16)
    return hi, lo


def _dot_x3(a, b, dims=(((1,), (0,)), ((), ()))):
    ah, al = _split(a)
    bh, bl = _split(b)
    return _dot(ah, bh, dims) + (_dot(ah, bl, dims) + _dot(al, bh, dims))


def _dot_bf(a, b, dims=(((1,), (0,)), ((), ()))):
    return _dot(a.astype(BF16), b.astype(BF16), dims)


_gdn_mm = _dot_bf


def _sigmoid(x):
    return 1.0 / (1.0 + jnp.exp(-x))


ROW_SUB = SUBLANES


def _pack_rows(x, ref, tm):
    half = x.shape[1] // 2
    for s in range(ROW_SUB):
        lo = x[:, s * LANES:(s + 1) * LANES].astype(BF16).astype(F32)
        hi = x[:, half + s * LANES:half + (s + 1) * LANES].astype(BF16).astype(F32)
        word = (lax.bitcast_convert_type(lo, jnp.uint32) >> 16) | lax.bitcast_convert_type(hi, jnp.uint32)
        ref[pl.ds(s, tm, stride=ROW_SUB), :] = word


def _unpack_rows(ref, base, tm):
    lo, hi = [], []
    for s in range(ROW_SUB):
        word = ref[pl.ds(base + s, tm, stride=ROW_SUB), :]
        lo.append(lax.bitcast_convert_type(word << 16, F32))
        hi.append(lax.bitcast_convert_type(word & jnp.uint32(0xFFFF0000), F32))
    return lo, hi


def _cparams(sem):
    return pltpu.CompilerParams(dimension_semantics=sem, vmem_limit_bytes=VMEM_LIMIT)


def _proj_body(x_ref, w_ref, o_ref, xb_ref):
    @pl.when(pl.program_id(1) == 0)
    def _():
        xb_ref[...] = x_ref[...].astype(BF16)

    o_ref[...] = _dot(xb_ref[...], w_ref[...])


def _in_proj(x2, w_bf, tm, tn):
    m, k = x2.shape
    n = w_bf.shape[1]
    return pl.pallas_call(
        _proj_body,
        grid=(m // tm, n // tn),
        in_specs=[pl.BlockSpec((tm, k), lambda i, j: (i, 0)),
                  pl.BlockSpec((k, tn), lambda i, j: (0, j))],
        out_specs=pl.BlockSpec((tm, tn), lambda i, j: (i, j)),
        out_shape=jax.ShapeDtypeStruct((m, n), F32),
        scratch_shapes=[pltpu.VMEM((tm, k), BF16)],
        compiler_params=_cparams(("parallel", "arbitrary")),
        name="in_proj",
    )(x2, w_bf)


def _gdn_prep_body(qkv_ref, ab_ref, hist_ref, wc_ref, alog_ref, dtb_ref,
                   q_ref, k_ref, v_ref, cols_ref, gct_ref, xbuf_ref, *, tt, chunk):
    d_gdn = GDN_HEADS * HEAD_DIM
    halo = SUBLANES

    @pl.when(pl.program_id(1) == 0)
    def _():
        xbuf_ref[0:halo, :] = jnp.zeros((halo, 3 * d_gdn), F32)
        xbuf_ref[halo - (CONV_W - 1):halo, :] = hist_ref[0]

    xbuf_ref[halo:halo + tt, :] = qkv_ref[0]

    for part, out_ref in enumerate((q_ref, k_ref, v_ref)):
        c0 = part * d_gdn
        y = None
        for j in range(CONV_W):
            r0 = halo - (CONV_W - 1) + j
            term = xbuf_ref[r0:r0 + tt, c0:c0 + d_gdn] * wc_ref[j:j + 1, c0:c0 + d_gdn]
            y = term if y is None else y + term
        s = y * _sigmoid(y)
        if part == 2:
            out_ref[0] = s
        else:
            for h in range(GDN_HEADS):
                sh = s[:, h * HEAD_DIM:(h + 1) * HEAD_DIM]
                nrm = sh * lax.rsqrt(jnp.sum(sh * sh, axis=-1, keepdims=True) + L2_EPS)
                if part == 0:
                    nrm = nrm * (HEAD_DIM ** -0.5)
                out_ref[0, :, h * HEAD_DIM:(h + 1) * HEAD_DIM] = nrm

    xbuf_ref[0:halo, :] = xbuf_ref[tt:tt + halo, :]

    ab = ab_ref[0]
    lane = lax.broadcasted_iota(jnp.int32, (tt, LANES), 1)
    is_a = lane < GDN_HEADS
    z = ab + dtb_ref[...]
    softplus = jnp.maximum(z, 0.0) + jnp.log1p(jnp.exp(-jnp.abs(z)))
    g = jnp.where(is_a, -jnp.exp(alog_ref[...]) * softplus, 0.0)
    beta = _sigmoid(ab)

    shift = int(math.log2(chunk))
    r = lax.broadcasted_iota(jnp.int32, (tt, tt), 0)
    c = lax.broadcasted_iota(jnp.int32, (tt, tt), 1)
    same = (r >> shift) == (c >> shift)
    m_incl = jnp.where(same & (r >= c), 1.0, 0.0).astype(BF16)
    m_all = jnp.where(same, 1.0, 0.0).astype(BF16)
    gc = _dot_exact_lhs_rhs(m_incl, g)
    glast = _dot_exact_lhs_rhs(m_all, g)
    eg = jnp.exp(gc)
    egl = jnp.exp(glast - gc)
    egt = jnp.exp(glast)
    zero = jnp.zeros_like(gc)
    cols = (jnp.where(is_a, gc, zero)
            + jnp.where((lane >= 8) & (lane < 16), beta, zero)
            + pltpu.roll(jnp.where(is_a, eg, zero), 16, 1)
            + pltpu.roll(jnp.where(is_a, egl, zero), 24, 1)
            + pltpu.roll(jnp.where(is_a, egt, zero), 32, 1))
    cols_ref[0] = cols

    er = lax.broadcasted_iota(jnp.int32, (SUBLANES, LANES), 0)
    ec = lax.broadcasted_iota(jnp.int32, (SUBLANES, LANES), 1)
    sel = jnp.where(er == ec, 1.0, 0.0).astype(BF16)
    gct_ref[0] = _dot_exact_rhs(sel, jnp.where(is_a, gc, zero), NT_DIMS)


def _three_pieces(a):
    a0 = a.astype(BF16)
    r1 = a - a0.astype(F32)
    a1 = r1.astype(BF16)
    a2 = (r1 - a1.astype(F32)).astype(BF16)
    return a0, a1, a2


def _dot_exact_lhs_rhs(mask_bf, a):
    a0, a1, a2 = _three_pieces(a)
    return _dot(mask_bf, a0) + (_dot(mask_bf, a1) + _dot(mask_bf, a2))


def _dot_exact_rhs(mask_bf, a, dims):
    a0, a1, a2 = _three_pieces(a)
    return _dot(mask_bf, a0, dims) + (_dot(mask_bf, a1, dims) + _dot(mask_bf, a2, dims))


def _gdn_prep(proj3, hist, w_conv, alog_p, dtb_p, chunk, ab_block):
    b, t, _ = proj3.shape
    d_gdn = GDN_HEADS * HEAD_DIM
    tt = min(t, GDN_TILE)
    body = functools.partial(_gdn_prep_body, tt=tt, chunk=chunk)
    big = jax.ShapeDtypeStruct((b, t, d_gdn), F32)
    return pl.pallas_call(
        body,
        grid=(b, t // tt),
        in_specs=[pl.BlockSpec((1, tt, 3 * d_gdn), lambda i, j: (i, j, 0)),
                  pl.BlockSpec((1, tt, LANES), lambda i, j: (i, j, ab_block)),
                  pl.BlockSpec((1, CONV_W - 1, 3 * d_gdn), lambda i, j: (i, 0, 0)),
                  pl.BlockSpec((CONV_W, 3 * d_gdn), lambda i, j: (0, 0)),
                  pl.BlockSpec((1, LANES), lambda i, j: (0, 0)),
                  pl.BlockSpec((1, LANES), lambda i, j: (0, 0))],
        out_specs=[pl.BlockSpec((1, tt, d_gdn), lambda i, j: (i, j, 0)),
                   pl.BlockSpec((1, tt, d_gdn), lambda i, j: (i, j, 0)),
                   pl.BlockSpec((1, tt, d_gdn), lambda i, j: (i, j, 0)),
                   pl.BlockSpec((1, tt, LANES), lambda i, j: (i, j, 0)),
                   pl.BlockSpec((1, SUBLANES, tt), lambda i, j: (i, 0, j))],
        out_shape=[big, big, big,
                   jax.ShapeDtypeStruct((b, t, LANES), F32),
                   jax.ShapeDtypeStruct((b, SUBLANES, t), F32)],
        scratch_shapes=[pltpu.VMEM((tt + SUBLANES, 3 * d_gdn), F32)],
        compiler_params=_cparams(("parallel", "arbitrary")),
        name="gdn_prep",
    )(proj3, proj3, hist, w_conv, alog_p, dtb_p)


def _unit_lower_inverses(mats, r, c, chunk):
    base = 16
    eye = jnp.where(r == c, 1.0, 0.0)

    def blk(bs):
        s = int(math.log2(bs))
        return (r >> s) == (c >> s)

    d1 = [jnp.where(blk(base), a, 0.0) for a in mats]
    d2 = [_gdn_mm(x, x) for x in d1]
    d4 = [_gdn_mm(x, x) for x in d2]
    d8 = [_gdn_mm(x, x) for x in d4]
    t = [eye - x for x in d1]
    t = [x + _gdn_mm(x, y) for x, y in zip(t, d2)]
    t = [x + _gdn_mm(x, y) for x, y in zip(t, d4)]
    t = [x + _gdn_mm(x, y) for x, y in zip(t, d8)]
    bs = base
    while bs < chunk:
        off_mask = blk(2 * bs) & jnp.logical_not(blk(bs))
        inner = [_gdn_mm(jnp.where(off_mask, a, 0.0), x) for a, x in zip(mats, t)]
        t = [x - _gdn_mm(x, y) for x, y in zip(t, inner)]
        bs *= 2
    return t


def _gdn_body(q_ref, k_ref, v_ref, cols_ref, gct_ref, z_ref, wn_ref, s0_ref,
              o_ref, s_ref, *, tb, chunk):
    @pl.when(pl.program_id(1) == 0)
    def _():
        s_ref[...] = s0_ref[...]

    shift = int(math.log2(chunk))
    r = lax.broadcasted_iota(jnp.int32, (tb, tb), 0)
    c = lax.broadcasted_iota(jnp.int32, (tb, tb), 1)
    same = (r >> shift) == (c >> shift)
    incl = same & (r >= c)
    strict = same & (r > c)
    n_chunks = tb // chunk

    heads = range(GDN_HEADS)
    hsl = [slice(h * HEAD_DIM, (h + 1) * HEAD_DIM) for h in heads]
    q = [q_ref[0, :, s] for s in hsl]
    k = [k_ref[0, :, s] for s in hsl]
    beta = [cols_ref[0, :, 8 + h:9 + h] for h in heads]
    eg = [cols_ref[0, :, 16 + h:17 + h] for h in heads]
    egl = [cols_ref[0, :, 24 + h:25 + h] for h in heads]
    decay = [jnp.exp(jnp.where(incl, cols_ref[0, :, h:h + 1] - gct_ref[0, h:h + 1, :], -jnp.inf)) for h in heads]
    kb = [k[h] * beta[h] for h in heads]
    k_bf = [x.astype(BF16) for x in k]
    a = [jnp.where(strict, _dot(kb[h].astype(BF16), k_bf[h], NT_DIMS) * decay[h], 0.0) for h in heads]
    tinv = _unit_lower_inverses(a, r, c, chunk)
    sol = [_gdn_mm(tinv[h], jnp.concatenate([v_ref[0, :, hsl[h]] * beta[h], kb[h] * eg[h]], axis=1)) for h in heads]
    u = [x[:, :HEAD_DIM] for x in sol]
    w_bf = [x[:, HEAD_DIM:].astype(BF16) for x in sol]
    attn = [(_dot(q[h].astype(BF16), k_bf[h], NT_DIMS) * decay[h]).astype(BF16) for h in heads]
    q_dec = [(q[h] * eg[h]).astype(BF16) for h in heads]
    k_dec = [(k[h] * egl[h]).astype(BF16) for h in heads]

    s = [s_ref[0, h] for h in heads]
    v_new = [[] for _ in heads]
    o_state = [[] for _ in heads]
    for ci in range(n_chunks):
        rs = slice(ci * chunk, (ci + 1) * chunk)
        s_bf = [x.astype(BF16) for x in s]
        vn = [u[h][rs] - _dot(w_bf[h][rs], s_bf[h]) for h in heads]
        for h in heads:
            o_state[h].append(_dot(q_dec[h][rs], s_bf[h]))
            v_new[h].append(vn[h])
        s = [s[h] * cols_ref[0, ci * chunk:ci * chunk + 1, 32 + h:33 + h]
             + _dot(k_dec[h][rs], vn[h].astype(BF16), TN_DIMS) for h in heads]
    for h in heads:
        s_ref[0, h] = s[h]

    def cat(parts):
        return parts[0] if len(parts) == 1 else jnp.concatenate(parts, axis=0)

    for h in heads:
        o = cat(o_state[h]) + _dot(attn[h], cat(v_new[h]).astype(BF16))
        zh = z_ref[0, :, hsl[h]]
        o = (o * lax.rsqrt(jnp.mean(o * o, axis=-1, keepdims=True) + RMS_EPS) * wn_ref[...]
             * (zh * _sigmoid(zh)))
        o_ref[0, :, hsl[h]] = o.astype(BF16)


def _gdn(q, k, v, cols, gct, proj3, wn, s0, chunk, z_block):
    b, t, d_gdn = q.shape
    tb = min(t, GDN_TILE)
    body = functools.partial(_gdn_body, tb=tb, chunk=chunk)
    tile = lambda i, j: (i, j, 0)
    return pl.pallas_call(
        body,
        grid=(b, t // tb),
        in_specs=[pl.BlockSpec((1, tb, d_gdn), tile),
                  pl.BlockSpec((1, tb, d_gdn), tile),
                  pl.BlockSpec((1, tb, d_gdn), tile),
                  pl.BlockSpec((1, tb, LANES), tile),
                  pl.BlockSpec((1, SUBLANES, tb), lambda i, j: (i, 0, j)),
                  pl.BlockSpec((1, tb, d_gdn), lambda i, j: (i, j, z_block)),
                  pl.BlockSpec((1, HEAD_DIM), lambda i, j: (0, 0)),
                  pl.BlockSpec((1, GDN_HEADS, HEAD_DIM, HEAD_DIM), lambda i, j: (i, 0, 0, 0))],
        out_specs=[pl.BlockSpec((1, tb, d_gdn), tile),
                   pl.BlockSpec((1, GDN_HEADS, HEAD_DIM, HEAD_DIM), lambda i, j: (i, 0, 0, 0))],
        out_shape=[jax.ShapeDtypeStruct((b, t, d_gdn), BF16),
                   jax.ShapeDtypeStruct((b, GDN_HEADS, HEAD_DIM, HEAD_DIM), F32)],
        compiler_params=_cparams(("parallel", "arbitrary")),
        name="gdn_delta",
    )(q, k, v, cols, gct, proj3, wn, s0)


def _gelu_tanh(x):
    return 0.5 * x * (1.0 + jnp.tanh(math.sqrt(2.0 / math.pi) * (x + 0.044715 * (x * x * x))))


def _s5_body(u_ref, wm_ref, ym0_ref, ym1_ref, ym2_ref, ym3_ref, lre_ref, lim_ref, dsk_ref, h0re_ref, h0im_ref,
             y_ref, hre_ref, him_ref, xbuf_ref, hbuf_ref, *, n_sub, seg):
    half = GROUPS_PER_BLOCK * SSM_P

    if seg is None:
        @pl.when(pl.program_id(2) == 0)
        def _():
            hre_ref[...] = h0re_ref[...]
            him_ref[...] = h0im_ref[...]

    u_f = [u_ref[0, pl.ds(j, n_sub, stride=S5_SUB), :] for j in range(S5_SUB)]
    u_b = [x.astype(BF16) for x in u_f]

    xbuf_ref[...] = _dot(jnp.concatenate(u_b, axis=1), wm_ref[0])

    lre = lre_ref[0]
    lim = lim_ref[0]

    def advance(n, hre, him):
        hbuf_ref[pl.ds(n, 1), 0:half] = hre
        hbuf_ref[pl.ds(n, 1), half:2 * half] = him
        xr = xbuf_ref[pl.ds(n, 1), 0:half]
        xi = xbuf_ref[pl.ds(n, 1), half:2 * half]
        return (lre * hre - lim * him + xr, lre * him + lim * hre + xi)

    if seg is None:
        hre, him = lax.fori_loop(0, n_sub, lambda n, c: advance(n, *c), (hre_ref[0, 0], him_ref[0, 0]))
        hre_ref[0, 0] = hre
        him_ref[0, 0] = him
    else:
        def step(n, carry):
            q = n // seg
            first = n - q * seg == 0
            hre = jnp.where(first, h0re_ref[q, 0], carry[0])
            him = jnp.where(first, h0im_ref[q, 0], carry[1])
            hre, him = advance(n, hre, him)
            hre_ref[q, 0] = hre
            him_ref[q, 0] = him
            return hre, him

        lax.fori_loop(0, n_sub, step, (h0re_ref[0, 0], h0im_ref[0, 0]))

    h_b = hbuf_ref[...].astype(BF16)
    dsk = dsk_ref[0]
    for l, ym_ref in zip(range(0, S5_SUB, 2), (ym0_ref, ym1_ref, ym2_ref, ym3_ref)):
        lhs = jnp.concatenate([h_b] + [u_b[m] for m in range(l + 1, -1, -1)], axis=1)
        y2 = _dot(lhs, ym_ref[0])
        for o in range(2):
            y = y2[:, o * LANES:(o + 1) * LANES] + dsk * u_f[l + o]
            y_ref[0, pl.ds(l + o, n_sub, stride=S5_SUB), :] = _gelu_tanh(y)


def _s5(proj3, mats, h0re, h0im, u_block0):
    b, t, width = proj3.shape
    n_gb = mats[0].shape[0]
    if b > 1 and b * t <= S5_TILE:
        y, hre, him = _s5_call(proj3.reshape(1, b * t, width), mats, h0re, h0im, u_block0,
                               tt=b * t, seg=t // S5_SUB, state_rows=b)
        return y.reshape(b, t, n_gb * LANES), hre, him
    return _s5_call(proj3, mats, h0re, h0im, u_block0, tt=min(t, S5_TILE), seg=None, state_rows=1)


def _s5_call(proj3, mats, h0re, h0im, u_block0, tt, seg, state_rows):
    wcat, ycat, lre, lim, dsk = mats
    b, t, _ = proj3.shape
    n_gb = wcat.shape[0]
    n_sub = tt // S5_SUB
    half = GROUPS_PER_BLOCK * SSM_P
    body = functools.partial(_s5_body, n_sub=n_sub, seg=seg)
    state_spec = pl.BlockSpec((state_rows, 1, 1, half), lambda g, i, j: (i, g, 0, 0))
    par_spec = pl.BlockSpec((1, 1, half), lambda g, i, j: (g, 0, 0))
    mat_spec = lambda a: pl.BlockSpec((1,) + a.shape[1:], lambda g, i, j: (g, 0, 0))
    return pl.pallas_call(
        body,
        grid=(n_gb, b, t // tt),
        in_specs=[pl.BlockSpec((1, tt, LANES), lambda g, i, j: (i, j, u_block0 + g)),
                  mat_spec(wcat)] + [mat_spec(a) for a in ycat] + [
                  par_spec, par_spec,
                  pl.BlockSpec((1, 1, LANES), lambda g, i, j: (g, 0, 0)),
                  state_spec, state_spec],
        out_specs=[pl.BlockSpec((1, tt, LANES), lambda g, i, j: (i, j, g)),
                   state_spec, state_spec],
        out_shape=[jax.ShapeDtypeStruct((b, t, n_gb * LANES), F32),
                   jax.ShapeDtypeStruct(h0re.shape, F32),
                   jax.ShapeDtypeStruct(h0im.shape, F32)],
        scratch_shapes=[pltpu.VMEM((n_sub, 2 * half), F32),
                        pltpu.VMEM((n_sub, 2 * half), F32)],
        compiler_params=_cparams(("parallel", "parallel", "arbitrary")),
        name="s5_scan",
    )(proj3, wcat, *ycat, lre, lim, dsk, h0re, h0im)


def _s5_matrices(lam_re, lam_im, log_dt, b_re, b_im, c_re, c_im, d_skip):
    g, p = lam_re.shape
    n_gb = g // GROUPS_PER_BLOCK
    gpb = GROUPS_PER_BLOCK
    dt = jnp.exp(log_dt.astype(F32))
    lam = lax.complex(jnp.minimum(lam_re.astype(F32), -1e-4), lam_im.astype(F32))
    lam_bar = jnp.exp(lam * dt[:, None])
    b_bar = ((lam_bar - 1.0) / lam)[..., None] * lax.complex(b_re.astype(F32), b_im.astype(F32))
    c_c = lax.complex(c_re.astype(F32), c_im.astype(F32))
    pows = [jnp.ones_like(lam_bar)]
    for _ in range(S5_SUB):
        pows.append(pows[-1] * lam_bar)
    pw = jnp.stack(pows)

    def block_diag(x):
        k = x.shape[-1]
        lanes = jnp.arange(gpb * k)
        rep = (lanes[None, :] % k == jnp.arange(k)[:, None]).astype(F32)
        wide = jnp.einsum('...k,kl->...l', x, rep, precision=lax.Precision.HIGHEST)
        own = (lanes[None, None, :] // k) == jnp.arange(gpb)[:, None, None]
        wide = jnp.where(own, wide, 0.0)
        return wide.reshape(x.shape[:-3] + (gpb * x.shape[-2], gpb * k))

    kd = jnp.real(jnp.einsum('gop,dgp,gpi->dgio', c_c, pw[:S5_SUB], b_bar))
    km = block_diag(kd.reshape(S5_SUB, n_gb, gpb, SSM_CG, SSM_CG))

    wj = pw[:S5_SUB][::-1][:, :, :, None] * b_bar[None]
    wj = jnp.transpose(wj.reshape(S5_SUB, n_gb, gpb, p, SSM_CG), (1, 0, 2, 4, 3))
    wcat = jnp.concatenate([block_diag(jnp.real(wj)), block_diag(jnp.imag(wj))], axis=-1)
    wcat = wcat.reshape(n_gb, S5_SUB * LANES, 2 * gpb * p).astype(BF16)

    cl = c_c[None] * pw[1:S5_SUB + 1][:, :, None, :]
    cl = jnp.transpose(cl, (0, 1, 3, 2)).reshape(S5_SUB, n_gb, gpb, p, SSM_CG)
    vm = jnp.concatenate([block_diag(jnp.real(cl)), block_diag(-jnp.imag(cl))], axis=-2)

    lam_s = pw[S5_SUB].reshape(n_gb, 1, gpb * p)
    dsk = d_skip.astype(F32).reshape(n_gb, 1, LANES)

    km, vm = km.astype(BF16), vm.astype(BF16)
    ycat = []
    zero = jnp.zeros_like(km[0])
    for l in range(0, S5_SUB, 2):
        rows = [jnp.concatenate([vm[l], vm[l + 1]], axis=-1)]
        for m in range(l + 1, -1, -1):
            left = km[l - m] if m <= l else zero
            rows.append(jnp.concatenate([left, km[l + 1 - m]], axis=-1))
        ycat.append(jnp.concatenate(rows, axis=-2))
    return (wcat, tuple(ycat), jnp.real(lam_s), jnp.imag(lam_s), dsk)


def _glu_body(y_ref, w_ref, b_ref, o_ref):
    y = y_ref[...]
    gate = _dot(y.astype(BF16), w_ref[...]) + b_ref[...]
    o_ref[...] = (y * _sigmoid(gate)).astype(BF16)


def _glu(y2, w_bf, b_row, tm):
    m, d = y2.shape
    return pl.pallas_call(
        _glu_body,
        grid=(m // tm,),
        in_specs=[pl.BlockSpec((tm, d), lambda i: (i, 0)),
                  pl.BlockSpec((d, d), lambda i: (0, 0)),
                  pl.BlockSpec((1, d), lambda i: (0, 0))],
        out_specs=pl.BlockSpec((tm, d), lambda i: (i, 0)),
        out_shape=jax.ShapeDtypeStruct((m, d), BF16),
        compiler_params=_cparams(("parallel",)),
        name="s5_glu",
    )(y2, w_bf, b_row)


def _layernorm(v, g, b):
    mu = jnp.mean(v, axis=-1, keepdims=True)
    var = jnp.mean(jnp.square(v - mu), axis=-1, keepdims=True)
    return (v - mu) * lax.rsqrt(var + LN_EPS) * g + b


def _mix_body(og_ref, os_ref, wa_ref, wb_ref, x_ref, g_ref, b_ref, wr_ref, br_ref,
              h_ref, hb_ref, route_ref, cnt_ref, *, alpha):
    @pl.when(pl.program_id(0) == 0)
    def _():
        cnt_ref[...] = jnp.zeros_like(cnt_ref)

    mix = _dot(og_ref[...], wa_ref[...]) + _dot(os_ref[...], wb_ref[...])
    h = _layernorm(alpha * x_ref[...] + mix, g_ref[...], b_ref[...])
    h_ref[...] = h
    tm = h.shape[0]
    _pack_rows(h, hb_ref, tm)

    logits = _dot_x3(h, wr_ref[...]) + br_ref[...]
    lane = lax.broadcasted_iota(jnp.int32, (tm, LANES), 1)
    work = jnp.where(lane < N_EXPERTS, logits, -jnp.inf)
    vals, idxs = [], []
    for _ in range(TOP_K):
        mx = jnp.max(work, axis=-1, keepdims=True)
        ix = jnp.min(jnp.where(work == mx, lane, LANES), axis=-1, keepdims=True)
        vals.append(mx)
        idxs.append(ix)
        work = jnp.where(lane == ix, -jnp.inf, work)
    exps = [jnp.exp(v - vals[0]) for v in vals]
    denom = exps[0]
    for e in exps[1:]:
        denom = denom + e
    chosen = jnp.zeros((tm, LANES), F32)
    for k in range(TOP_K):
        chosen = jnp.where(lane == idxs[k], 1.0, chosen)
    rr = lax.broadcasted_iota(jnp.int32, (tm, tm), 0)
    cc = lax.broadcasted_iota(jnp.int32, (tm, tm), 1)
    lower = jnp.where(rr > cc, 1.0, 0.0).astype(BF16)
    before = cnt_ref[...] + _dot(lower, chosen.astype(BF16))
    cnt_ref[...] = cnt_ref[...] + jnp.sum(chosen, axis=0, keepdims=True)

    route = jnp.zeros((tm, LANES), F32)
    for k in range(TOP_K):
        rank = jnp.sum(jnp.where(lane == idxs[k], before, 0.0), axis=-1, keepdims=True)
        route = jnp.where(lane == k, idxs[k].astype(F32), route)
        route = jnp.where(lane == TOP_K + k, exps[k] / denom, route)
        route = jnp.where(lane == 2 * TOP_K + k, rank, route)
    route_ref[...] = route


def _mix_ln_route(og, osm, wa, wb, x2, g_row, b_row, wr, br, alpha, tm):
    m, d = x2.shape
    dh = og.shape[1]
    body = functools.partial(_mix_body, alpha=alpha)
    row = lambda i: (i, 0)
    fix = lambda i: (0, 0)
    return pl.pallas_call(
        body,
        grid=(m // tm,),
        in_specs=[pl.BlockSpec((tm, dh), row), pl.BlockSpec((tm, dh), row),
                  pl.BlockSpec((dh, d), fix), pl.BlockSpec((dh, d), lambda i: (1, 0)),
                  pl.BlockSpec((tm, d), row),
                  pl.BlockSpec((1, d), fix), pl.BlockSpec((1, d), fix),
                  pl.BlockSpec((d, LANES), fix), pl.BlockSpec((1, LANES), fix)],
        out_specs=[pl.BlockSpec((tm, d), row), pl.BlockSpec((tm * ROW_SUB, LANES), row),
                   pl.BlockSpec((tm, LANES), row), pl.BlockSpec((1, LANES), fix)],
        out_shape=[jax.ShapeDtypeStruct((m, d), F32),
                   jax.ShapeDtypeStruct((m * ROW_SUB, LANES), jnp.uint32),
                   jax.ShapeDtypeStruct((m, LANES), F32),
                   jax.ShapeDtypeStruct((1, LANES), F32)],
        compiler_params=_cparams(("arbitrary",)),
        name="mix_ln_route",
    )(og, osm, wa, wb, x2, g_row, b_row, wr, br)


def _dispatch_body(pos_ref, src_ref, init_ref, dst_ref, sem, *, tt):
    del init_ref

    def issue(t, carry):
        srow = pl.multiple_of(t * ROW_SUB, ROW_SUB)
        for k in range(TOP_K):
            drow = pl.multiple_of(pos_ref[t * TOP_K + k] * ROW_SUB, ROW_SUB)
            pltpu.make_async_copy(src_ref.at[pl.ds(srow, ROW_SUB)], dst_ref.at[pl.ds(drow, ROW_SUB)],
                                  sem).start(priority=k % 2)
        return carry

    lax.fori_loop(0, tt, issue, 0, unroll=4)
    n = tt * ROW_SUB
    for _ in range(TOP_K):
        pltpu.make_async_copy(src_ref.at[pl.ds(0, n)], dst_ref.at[pl.ds(0, n)], sem).wait()


def _dispatch_rows(pos_flat, hbp, init, tt):
    m = hbp.shape[0] // ROW_SUB
    body = functools.partial(_dispatch_body, tt=tt)
    return pl.pallas_call(
        body,
        grid=(m // tt,),
        in_specs=[pl.BlockSpec((tt * TOP_K,), lambda i: (i,), memory_space=pltpu.SMEM),
                  pl.BlockSpec((tt * ROW_SUB, LANES), lambda i: (i, 0)), pl.BlockSpec(memory_space=pl.ANY)],
        out_specs=pl.BlockSpec(memory_space=pl.ANY),
        out_shape=jax.ShapeDtypeStruct(init.shape, jnp.uint32),
        scratch_shapes=[pltpu.SemaphoreType.DMA(())],
        input_output_aliases={2: 0},
        compiler_params=_cparams(("arbitrary",)),
        name="moe_dispatch",
    )(pos_flat, hbp, init)


MOE_COL = 256
MOE_GROUP = 2


def _moe_body(te_ref, nu_ref, x_ref, wg_ref, bg_ref, wu_ref, bu_ref, wd_ref, bd_ref,
              o_ref, xb_ref, acc_ref, *, n_f, tm, tf):
    g = pl.program_id(2)
    f = pl.program_id(1)
    used = pl.program_id(0) * MOE_GROUP + g < nu_ref[0]
    half = xb_ref.shape[2] // 2
    xb_ref = xb_ref.at[g]
    acc_ref = acc_ref.at[g]

    @pl.when(used & (f == 0))
    def _():
        lo, hi = _unpack_rows(x_ref, 0, tm)
        for r in range(ROW_SUB):
            xb_ref[:, r * LANES:(r + 1) * LANES] = lo[r].astype(BF16)
            xb_ref[:, half + r * LANES:half + (r + 1) * LANES] = hi[r].astype(BF16)

    @pl.when(used)
    def _():
        xb = xb_ref[...]
        part = None
        for c in range(tf // MOE_COL):
            cs = slice(c * MOE_COL, (c + 1) * MOE_COL)
            hg = jnp.minimum(_dot(xb, wg_ref[0, :, cs].astype(BF16)) + bg_ref[0, :, cs], SWIGLU_LIMIT)
            hu = jnp.clip(_dot(xb, wu_ref[0, :, cs].astype(BF16)) + bu_ref[0, :, cs], -SWIGLU_LIMIT, SWIGLU_LIMIT)
            hh = ((hu + 1.0) * (hg * _sigmoid(SWIGLU_ALPHA * hg))).astype(BF16)
            p = _dot(hh, wd_ref[0, cs, :].astype(BF16))
            part = p if part is None else part + p

        @pl.when(f == 0)
        def _():
            acc_ref[...] = part

        @pl.when(f != 0)
        def _():
            acc_ref[...] += part

        @pl.when(f == n_f - 1)
        def _():
            _pack_rows(acc_ref[...] + bd_ref[0], o_ref, tm)

    @pl.when(jnp.logical_not(used) & (f == n_f - 1))
    def _():
        o_ref[...] = jnp.zeros_like(o_ref)


def _moe_ffn(tile_expert, n_used, x_rows, w_gate, b_gate, w_up, b_up, w_down, b_down, tm, tf):
    n_e, d, d_ff = w_gate.shape
    n_tiles = x_rows.shape[0] // (tm * ROW_SUB)
    n_f = d_ff // tf
    grp = MOE_GROUP
    assert n_tiles % grp == 0
    body = functools.partial(_moe_body, n_f=n_f, tm=tm, tf=tf)

    def tile(p, g):
        return p * grp + g

    def fcol(p, f, nu):
        return jnp.where(p * grp < nu[0], f, n_f - 1)

    def xrow(p, f, g):
        return jnp.where(f == 0, tile(p, g), tile(p, grp - 1))

    def orow(p, f, g):
        return jnp.where(f == n_f - 1, tile(p, g), tile(p, 0))

    grid_spec = pltpu.PrefetchScalarGridSpec(
        num_scalar_prefetch=2,
        grid=(n_tiles // grp, n_f, grp),
        in_specs=[pl.BlockSpec((tm * ROW_SUB, LANES), lambda p, f, g, te, nu: (xrow(p, f, g), 0)),
                  pl.BlockSpec((1, d, tf), lambda p, f, g, te, nu: (te[tile(p, g)], 0, fcol(p, f, nu))),
                  pl.BlockSpec((1, 1, tf), lambda p, f, g, te, nu: (te[tile(p, g)], 0, fcol(p, f, nu))),
                  pl.BlockSpec((1, d, tf), lambda p, f, g, te, nu: (te[tile(p, g)], 0, fcol(p, f, nu))),
                  pl.BlockSpec((1, 1, tf), lambda p, f, g, te, nu: (te[tile(p, g)], 0, fcol(p, f, nu))),
                  pl.BlockSpec((1, tf, d), lambda p, f, g, te, nu: (te[tile(p, g)], fcol(p, f, nu), 0)),
                  pl.BlockSpec((1, 1, d), lambda p, f, g, te, nu: (te[tile(p, g)], 0, 0))],
        out_specs=pl.BlockSpec((tm * ROW_SUB, LANES), lambda p, f, g, te, nu: (orow(p, f, g), 0)),
        scratch_shapes=[pltpu.VMEM((grp, tm, d), BF16), pltpu.VMEM((grp, tm, d), F32)],
    )
    return pl.pallas_call(
        body,
        grid_spec=grid_spec,
        out_shape=jax.ShapeDtypeStruct(x_rows.shape, jnp.uint32),
        compiler_params=_cparams(("arbitrary", "arbitrary", "arbitrary")),
        name="moe_ffn",
    )(tile_expert, n_used, x_rows, w_gate, b_gate.reshape(n_e, 1, d_ff),
      w_up, b_up.reshape(n_e, 1, d_ff), w_down, b_down.reshape(n_e, 1, d))


def _final_body(pos_cur_ref, pos_nxt_ref, h_ref, route_ref, g_ref, b_ref, src_ref,
                o_ref, buf_ref, sem, *, alpha, tt, n_steps):
    i = pl.program_id(0)
    slot_rows = tt * TOP_K * ROW_SUB

    def issue(pref, slot):
        def one(t, carry):
            for k in range(TOP_K):
                srow = pl.multiple_of(pref[t * TOP_K + k] * ROW_SUB, ROW_SUB)
                drow = pl.multiple_of(slot * slot_rows + (k * tt + t) * ROW_SUB, ROW_SUB)
                pltpu.make_async_copy(src_ref.at[pl.ds(srow, ROW_SUB)], buf_ref.at[pl.ds(drow, ROW_SUB)],
                                      sem.at[slot]).start(priority=k % 2)
            return carry
        lax.fori_loop(0, tt, one, 0, unroll=4)

    @pl.when(i == 0)
    def _():
        issue(pos_cur_ref, 0)

    @pl.when(i + 1 < n_steps)
    def _():
        issue(pos_nxt_ref, (i + 1) % 2)

    slot = i % 2
    base = pl.multiple_of(slot * slot_rows, ROW_SUB)
    pltpu.make_async_copy(src_ref.at[pl.ds(0, slot_rows)], buf_ref.at[pl.ds(base, slot_rows)], sem.at[slot]).wait()

    ff_lo = [None] * ROW_SUB
    ff_hi = [None] * ROW_SUB
    for k in range(TOP_K):
        gate = route_ref[:, TOP_K + k:TOP_K + k + 1]
        lo, hi = _unpack_rows(buf_ref, base + k * tt * ROW_SUB, tt)
        for s in range(ROW_SUB):
            ff_lo[s] = lo[s] * gate if k == 0 else ff_lo[s] + lo[s] * gate
            ff_hi[s] = hi[s] * gate if k == 0 else ff_hi[s] + hi[s] * gate
    ff = jnp.concatenate(ff_lo + ff_hi, axis=1)
    o_ref[...] = _layernorm(alpha * h_ref[...] + ff, g_ref[...], b_ref[...])


def _combine_ln(pos_flat, h, route, g_row, b_row, outs, alpha, tt):
    m, d = h.shape
    n_steps = m // tt
    body = functools.partial(_final_body, alpha=alpha, tt=tt, n_steps=n_steps)
    row = lambda i: (i, 0)
    fix = lambda i: (0, 0)
    return pl.pallas_call(
        body,
        grid=(n_steps,),
        in_specs=[pl.BlockSpec((tt * TOP_K,), lambda i: (i,), memory_space=pltpu.SMEM),
                  pl.BlockSpec((tt * TOP_K,), lambda i: (jnp.minimum(i + 1, n_steps - 1),),
                               memory_space=pltpu.SMEM),
                  pl.BlockSpec((tt, d), row), pl.BlockSpec((tt, LANES), row),
                  pl.BlockSpec((1, d), fix), pl.BlockSpec((1, d), fix),
                  pl.BlockSpec(memory_space=pl.ANY)],
        out_specs=pl.BlockSpec((tt, d), row),
        out_shape=jax.ShapeDtypeStruct((m, d), F32),
        scratch_shapes=[pltpu.VMEM((2 * tt * TOP_K * ROW_SUB, LANES), jnp.uint32),
                        pltpu.SemaphoreType.DMA((2,))],
        compiler_params=_cparams(("arbitrary",)),
        name="combine_ln",
    )(pos_flat, pos_flat, h, route, g_row, b_row, outs)


def _row_tile(m, pref):
    t = min(m, pref)
    while m % t:
        t //= 2
    return t


def _route_tables(routes, counts_rows, tm):
    counts = [c[0, :N_EXPERTS].astype(jnp.int32) for c in counts_rows]
    total = functools.reduce(lambda a, b: a + b, counts)
    padded = (total + tm - 1) // tm * tm
    pends = jnp.cumsum(padded)
    pstarts = pends - padded
    pos, before = [], jnp.zeros_like(total)
    for route, cnt in zip(routes, counts):
        e_tok = route[:, :TOP_K].astype(jnp.int32)
        rank = route[:, 2 * TOP_K:3 * TOP_K].astype(jnp.int32)
        pos.append(((pstarts + before)[e_tok] + rank).reshape(-1))
        before = before + cnt
    n_assign = sum(r.shape[0] for r in routes) * TOP_K
    n_tiles = -(-n_assign // tm) + N_EXPERTS
    n_tiles = -(-n_tiles // MOE_GROUP) * MOE_GROUP
    tile_start = jnp.arange(n_tiles, dtype=jnp.int32) * tm
    tile_expert = jnp.minimum(jnp.sum((pends[None, :] <= tile_start[:, None]).astype(jnp.int32), axis=1),
                              N_EXPERTS - 1)
    n_used = (pends[-1] // tm).astype(jnp.int32).reshape(1)
    last_used = jnp.maximum(n_used[0] - 1, 0)
    tile_expert = jnp.where(tile_start // tm < n_used[0], tile_expert, tile_expert[last_used])
    return pos, tile_expert, n_used


def _moe_joint(parts, alpha, p):
    tm = p['moe_tm']
    pos, tile_expert, n_used = _route_tables([q['route'] for q in parts], [q['counts'] for q in parts], tm)
    cap = tile_expert.shape[0] * tm
    x_rows = jnp.zeros((cap * ROW_SUB, LANES), jnp.uint32)
    for q, pq in zip(parts, pos):
        x_rows = _dispatch_rows(pq, q['hb'], x_rows, _row_tile(q['h'].shape[0], 256))
    outs = _moe_ffn(tile_expert, n_used, x_rows, p['w_gate'], p['b_gate'],
                    p['w_up'], p['b_up'], p['w_down'], p['b_down'], tm, p['moe_tf'])
    return [_combine_ln(pq, q['h'], q['route'], p['ln2_g'], p['ln2_b'], outs, alpha, _row_tile(q['h'].shape[0], 128))
            for q, pq in zip(parts, pos)]


def _layer_pre(x, conv_hist, s_gdn, h_re, h_im, chunk, alpha, p):
    b, t, d = x.shape
    m = b * t
    d_gdn = GDN_HEADS * HEAD_DIM
    x2 = x.reshape(m, d)

    proj = _in_proj(x2, p['w_in'], _row_tile(m, 1024), p['proj_tn'])
    proj3 = proj.reshape(b, t, proj.shape[1])
    conv_new = proj3[:, t - (CONV_W - 1):, :3 * d_gdn]

    q, k, v, cols, gct = _gdn_prep(proj3, conv_hist, p['w_conv'], p['alog_row'], p['dtb_row'],
                                   chunk, p['ab_block'])
    o_gdn, s_new = _gdn(q, k, v, cols, gct, proj3, p['wn_row'], s_gdn, chunk, p['z_block'])

    n_gb = p['s5_mats'][0].shape[0]
    half = GROUPS_PER_BLOCK * SSM_P
    yg, hre_new, him_new = _s5(proj3, p['s5_mats'], h_re.reshape(b, n_gb, 1, half),
                               h_im.reshape(b, n_gb, 1, half), p['u_block0'])
    o_ssm = _glu(yg.reshape(m, yg.shape[2]), p['w_glu'], p['b_glu_row'], _row_tile(m, 512))

    h, hb, route, counts_row = _mix_ln_route(o_gdn.reshape(m, d_gdn), o_ssm, p['w_out'], p['w_out'], x2,
                                             p['ln1_g'], p['ln1_b'], p['w_router'], p['b_router'], alpha,
                                             _row_tile(m, 256))

    g_all = h_re.shape[1]
    return {'h': h, 'hb': hb, 'route': route, 'counts': counts_row, 'shape': (b, t, d),
            'state': (conv_new, s_new, hre_new.reshape(b, g_all, SSM_P), him_new.reshape(b, g_all, SSM_P))}


def _pad_lanes(v, fill=0.0):
    return jnp.pad(v.astype(F32), (0, LANES - v.shape[0]), constant_values=fill).reshape(1, LANES)


def _layer_params(l, w_in, w_conv, a_log, dt_bias, w_onorm, lam_re, lam_im, log_dt, b_re, b_im, c_re, c_im,
                  d_skip, w_glu, b_glu, w_out, ln1_g, ln1_b, w_router, b_router, w_gate, b_gate,
                  w_up, b_up, w_down, b_down, ln2_g, ln2_b):
    d_model = w_in.shape[1]
    d_gdn = GDN_HEADS * HEAD_DIM
    d_qkvz = 4 * d_gdn
    d_ssm = d_model - d_gdn
    wi = w_in[l]
    proj_tn = 1792
    n_cols = d_qkvz + d_ssm + LANES
    n_pad = -(-n_cols // proj_tn) * proj_tn
    w_in_r = jnp.concatenate([wi[:, :d_qkvz], wi[:, d_qkvz + 2 * GDN_HEADS:],
                              wi[:, d_qkvz:d_qkvz + 2 * GDN_HEADS],
                              jnp.zeros((d_model, n_pad - d_qkvz - d_ssm - 2 * GDN_HEADS), wi.dtype)], axis=1)
    wo = w_out[l].astype(BF16)
    wr = jnp.pad(w_router[l].astype(F32), ((0, 0), (0, LANES - N_EXPERTS)))
    return {
        'w_in': w_in_r.astype(BF16), 'proj_tn': proj_tn,
        'z_block': 3, 'u_block0': (d_qkvz) // LANES, 'ab_block': (d_qkvz + d_ssm) // LANES,
        'w_conv': w_conv[l].astype(F32),
        'alog_row': _pad_lanes(a_log[l]), 'dtb_row': _pad_lanes(dt_bias[l]),
        'wn_row': w_onorm[l].astype(F32).reshape(1, HEAD_DIM),
        's5_mats': _s5_matrices(lam_re[l], lam_im[l], log_dt[l], b_re[l], b_im[l], c_re[l], c_im[l], d_skip[l]),
        'w_glu': w_glu[l].astype(BF16), 'b_glu_row': b_glu[l].astype(F32).reshape(1, d_ssm),
        'w_out': wo,
        'ln1_g': ln1_g[l].astype(F32).reshape(1, d_model), 'ln1_b': ln1_b[l].astype(F32).reshape(1, d_model),
        'w_router': wr, 'b_router': _pad_lanes(b_router[l]),
        'w_gate': w_gate[l], 'b_gate': b_gate[l], 'w_up': w_up[l], 'b_up': b_up[l],
        'w_down': w_down[l], 'b_down': b_down[l],
        'ln2_g': ln2_g[l].astype(F32).reshape(1, d_model), 'ln2_b': ln2_b[l].astype(F32).reshape(1, d_model),
        'moe_tm': 576, 'moe_tf': 512,
    }


def kernel(x_prompt, x_sample, state_conv, state_gdn, state_ssm_re, state_ssm_im, w_in, w_conv, a_log, dt_bias, w_onorm, lam_re, lam_im, log_dt, b_re, b_im, c_re, c_im, d_skip, w_glu, b_glu, w_out, ln1_g, ln1_b, w_router, b_router, w_gate, b_gate, w_up, b_up, w_down, b_down, ln2_g, ln2_b):
    depth = w_in.shape[0]
    alpha = (2.0 * depth) ** 0.25
    bp, seq, _ = x_prompt.shape
    chunk_p = 64
    d_qkv = state_conv.shape[-1]
    n_groups, n_p = state_ssm_re.shape[-2:]
    yp, ys = x_prompt, x_sample
    outs_p = [[], [], [], []]
    outs_s = [[], [], [], []]
    for l in range(depth):
        p = _layer_params(l, w_in, w_conv, a_log, dt_bias, w_onorm, lam_re, lam_im, log_dt, b_re, b_im,
                          c_re, c_im, d_skip, w_glu, b_glu, w_out, ln1_g, ln1_b, w_router, b_router,
                          w_gate, b_gate, w_up, b_up, w_down, b_down, ln2_g, ln2_b)
        part_p = _layer_pre(
            yp, jnp.zeros((bp, CONV_W - 1, d_qkv), F32),
            jnp.zeros((bp, GDN_HEADS, HEAD_DIM, HEAD_DIM), F32),
            jnp.zeros((bp, n_groups, n_p), F32), jnp.zeros((bp, n_groups, n_p), F32),
            chunk_p, alpha, p)
        part_s = _layer_pre(
            ys, state_conv[l].astype(F32), state_gdn[l].astype(F32),
            state_ssm_re[l].astype(F32), state_ssm_im[l].astype(F32),
            ys.shape[1], alpha, p)
        yp2, ys2 = _moe_joint([part_p, part_s], alpha, p)
        yp, ys = yp2.reshape(part_p['shape']), ys2.reshape(part_s['shape'])
        for acc, val in zip(outs_p, part_p['state']):
            acc.append(val)
        for acc, val in zip(outs_s, part_s['state']):
            acc.append(val)
    return (yp, ys, *[jnp.stack(a) for a in outs_p], *[jnp.stack(a) for a in outs_s])
```

```python
import functools
import math

import jax
import jax.numpy as jnp
from jax import lax
from jax.experimental import pallas as pl
from jax.experimental.pallas import tpu as pltpu

F32 = jnp.float32
BF16 = jnp.bfloat16

GDN_HEADS = 8
HEAD_DIM = 128
CONV_W = 4
SSM_CG = 16
SSM_P = 64
N_EXPERTS = 32
TOP_K = 4
SWIGLU_ALPHA = 1.702
SWIGLU_LIMIT = 7.0
LN_EPS = 1e-5
RMS_EPS = 1e-6
L2_EPS = 1e-6

LANES = 128
SUBLANES = 8
S5_SUB = 8
S5_TILE = 2048
GDN_TILE = 128
GROUPS_PER_BLOCK = LANES // SSM_CG
VMEM_LIMIT = 56 * 1024 * 1024

NT_DIMS = (((1,), (1,)), ((), ()))
TN_DIMS = (((0,), (0,)), ((), ()))


def _dot(a, b, dims=(((1,), (0,)), ((), ()))):
    return lax.dot_general(a, b, dims, preferred_element_type=F32)


def _split(a):
    hi = a.astype(BF16)
    lo = (a - hi.astype(F32)).astype(BF16)
    return hi, lo


def _dot_x3(a, b, dims=(((1,), (0,)), ((), ()))):
    ah, al = _split(a)
    bh, bl = _split(b)
    return _dot(ah, bh, dims) + (_dot(ah, bl, dims) + _dot(al, bh, dims))


def _dot_bf(a, b, dims=(((1,), (0,)), ((), ()))):
    return _dot(a.astype(BF16), b.astype(BF16), dims)


_gdn_mm = _dot_bf


def _sigmoid(x):
    return 1.0 / (1.0 + jnp.exp(-x))


ROW_SUB = SUBLANES


def _pack_rows(x, ref, tm):
    half = x.shape[1] // 2
    for s in range(ROW_SUB):
        lo = x[:, s * LANES:(s + 1) * LANES].astype(BF16).astype(F32)
        hi = x[:, half + s * LANES:half + (s + 1) * LANES].astype(BF16).astype(F32)
        word = (lax.bitcast_convert_type(lo, jnp.uint32) >> 16) | lax.bitcast_convert_type(hi, jnp.uint32)
        ref[pl.ds(s, tm, stride=ROW_SUB), :] = word


def _unpack_rows(ref, base, tm):
    lo, hi = [], []
    for s in range(ROW_SUB):
        word = ref[pl.ds(base + s, tm, stride=ROW_SUB), :]
        lo.append(lax.bitcast_convert_type(word << 16, F32))
        hi.append(lax.bitcast_convert_type(word & jnp.uint32(0xFFFF0000), F32))
    return lo, hi


def _cparams(sem):
    return pltpu.CompilerParams(dimension_semantics=sem, vmem_limit_bytes=VMEM_LIMIT)


def _proj_body(x_ref, w_ref, o_ref, xb_ref):
    @pl.when(pl.program_id(1) == 0)
    def _():
        xb_ref[...] = x_ref[...].astype(BF16)

    o_ref[...] = _dot(xb_ref[...], w_ref[...])


def _in_proj(x2, w_bf, tm, tn):
    m, k = x2.shape
    n = w_bf.shape[1]
    return pl.pallas_call(
        _proj_body,
        grid=(m // tm, n // tn),
        in_specs=[pl.BlockSpec((tm, k), lambda i, j: (i, 0)),
                  pl.BlockSpec((k, tn), lambda i, j: (0, j))],
        out_specs=pl.BlockSpec((tm, tn), lambda i, j: (i, j)),
        out_shape=jax.ShapeDtypeStruct((m, n), F32),
        scratch_shapes=[pltpu.VMEM((tm, k), BF16)],
        compiler_params=_cparams(("parallel", "arbitrary")),
        name="in_proj",
    )(x2, w_bf)


def _gdn_prep_body(qkv_ref, ab_ref, hist_ref, wc_ref, alog_ref, dtb_ref,
                   q_ref, k_ref, v_ref, cols_ref, gct_ref, xbuf_ref, *, tt, chunk):
    d_gdn = GDN_HEADS * HEAD_DIM
    halo = SUBLANES

    @pl.when(pl.program_id(1) == 0)
    def _():
        xbuf_ref[0:halo, :] = jnp.zeros((halo, 3 * d_gdn), F32)
        xbuf_ref[halo - (CONV_W - 1):halo, :] = hist_ref[0]

    xbuf_ref[halo:halo + tt, :] = qkv_ref[0]

    for part, out_ref in enumerate((q_ref, k_ref, v_ref)):
        c0 = part * d_gdn
        y = None
        for j in range(CONV_W):
            r0 = halo - (CONV_W - 1) + j
            term = xbuf_ref[r0:r0 + tt, c0:c0 + d_gdn] * wc_ref[j:j + 1, c0:c0 + d_gdn]
            y = term if y is None else y + term
        s = y * _sigmoid(y)
        if part == 2:
            out_ref[0] = s
        else:
            for h in range(GDN_HEADS):
                sh = s[:, h * HEAD_DIM:(h + 1) * HEAD_DIM]
                nrm = sh * lax.rsqrt(jnp.sum(sh * sh, axis=-1, keepdims=True) + L2_EPS)
                if part == 0:
                    nrm = nrm * (HEAD_DIM ** -0.5)
                out_ref[0, :, h * HEAD_DIM:(h + 1) * HEAD_DIM] = nrm

    xbuf_ref[0:halo, :] = xbuf_ref[tt:tt + halo, :]

    ab = ab_ref[0]
    lane = lax.broadcasted_iota(jnp.int32, (tt, LANES), 1)
    is_a = lane < GDN_HEADS
    z = ab + dtb_ref[...]
    softplus = jnp.maximum(z, 0.0) + jnp.log1p(jnp.exp(-jnp.abs(z)))
    g = jnp.where(is_a, -jnp.exp(alog_ref[...]) * softplus, 0.0)
    beta = _sigmoid(ab)

    shift = int(math.log2(chunk))
    r = lax.broadcasted_iota(jnp.int32, (tt, tt), 0)
    c = lax.broadcasted_iota(jnp.int32, (tt, tt), 1)
    same = (r >> shift) == (c >> shift)
    m_incl = jnp.where(same & (r >= c), 1.0, 0.0).astype(BF16)
    m_all = jnp.where(same, 1.0, 0.0).astype(BF16)
    gc = _dot_exact_lhs_rhs(m_incl, g)
    glast = _dot_exact_lhs_rhs(m_all, g)
    eg = jnp.exp(gc)
    egl = jnp.exp(glast - gc)
    egt = jnp.exp(glast)
    zero = jnp.zeros_like(gc)
    cols = (jnp.where(is_a, gc, zero)
            + jnp.where((lane >= 8) & (lane < 16), beta, zero)
            + pltpu.roll(jnp.where(is_a, eg, zero), 16, 1)
            + pltpu.roll(jnp.where(is_a, egl, zero), 24, 1)
            + pltpu.roll(jnp.where(is_a, egt, zero), 32, 1))
    cols_ref[0] = cols

    er = lax.broadcasted_iota(jnp.int32, (SUBLANES, LANES), 0)
    ec = lax.broadcasted_iota(jnp.int32, (SUBLANES, LANES), 1)
    sel = jnp.where(er == ec, 1.0, 0.0).astype(BF16)
    gct_ref[0] = _dot_exact_rhs(sel, jnp.where(is_a, gc, zero), NT_DIMS)


def _three_pieces(a):
    a0 = a.astype(BF16)
    r1 = a - a0.astype(F32)
    a1 = r1.astype(BF16)
    a2 = (r1 - a1.astype(F32)).astype(BF16)
    return a0, a1, a2


def _dot_exact_lhs_rhs(mask_bf, a):
    a0, a1, a2 = _three_pieces(a)
    return _dot(mask_bf, a0) + (_dot(mask_bf, a1) + _dot(mask_bf, a2))


def _dot_exact_rhs(mask_bf, a, dims):
    a0, a1, a2 = _three_pieces(a)
    return _dot(mask_bf, a0, dims) + (_dot(mask_bf, a1, dims) + _dot(mask_bf, a2, dims))


def _gdn_prep(proj3, hist, w_conv, alog_p, dtb_p, chunk, ab_block):
    b, t, _ = proj3.shape
    d_gdn = GDN_HEADS * HEAD_DIM
    tt = min(t, GDN_TILE)
    body = functools.partial(_gdn_prep_body, tt=tt, chunk=chunk)
    big = jax.ShapeDtypeStruct((b, t, d_gdn), F32)
    return pl.pallas_call(
        body,
        grid=(b, t // tt),
        in_specs=[pl.BlockSpec((1, tt, 3 * d_gdn), lambda i, j: (i, j, 0)),
                  pl.BlockSpec((1, tt, LANES), lambda i, j: (i, j, ab_block)),
                  pl.BlockSpec((1, CONV_W - 1, 3 * d_gdn), lambda i, j: (i, 0, 0)),
                  pl.BlockSpec((CONV_W, 3 * d_gdn), lambda i, j: (0, 0)),
                  pl.BlockSpec((1, LANES), lambda i, j: (0, 0)),
                  pl.BlockSpec((1, LANES), lambda i, j: (0, 0))],
        out_specs=[pl.BlockSpec((1, tt, d_gdn), lambda i, j: (i, j, 0)),
                   pl.BlockSpec((1, tt, d_gdn), lambda i, j: (i, j, 0)),
                   pl.BlockSpec((1, tt, d_gdn), lambda i, j: (i, j, 0)),
                   pl.BlockSpec((1, tt, LANES), lambda i, j: (i, j, 0)),
                   pl.BlockSpec((1, SUBLANES, tt), lambda i, j: (i, 0, j))],
        out_shape=[big, big, big,
                   jax.ShapeDtypeStruct((b, t, LANES), F32),
                   jax.ShapeDtypeStruct((b, SUBLANES, t), F32)],
        scratch_shapes=[pltpu.VMEM((tt + SUBLANES, 3 * d_gdn), F32)],
        compiler_params=_cparams(("parallel", "arbitrary")),
        name="gdn_prep",
    )(proj3, proj3, hist, w_conv, alog_p, dtb_p)


def _unit_lower_inverses(mats, r, c, chunk):
    base = 16
    eye = jnp.where(r == c, 1.0, 0.0)

    def blk(bs):
        s = int(math.log2(bs))
        return (r >> s) == (c >> s)

    d1 = [jnp.where(blk(base), a, 0.0) for a in mats]
    d2 = [_gdn_mm(x, x) for x in d1]
    d4 = [_gdn_mm(x, x) for x in d2]
    d8 = [_gdn_mm(x, x) for x in d4]
    t = [eye - x for x in d1]
    t = [x + _gdn_mm(x, y) for x, y in zip(t, d2)]
    t = [x + _gdn_mm(x, y) for x, y in zip(t, d4)]
    t = [x + _gdn_mm(x, y) for x, y in zip(t, d8)]
    bs = base
    while bs < chunk:
        off_mask = blk(2 * bs) & jnp.logical_not(blk(bs))
        inner = [_gdn_mm(jnp.where(off_mask, a, 0.0), x) for a, x in zip(mats, t)]
        t = [x - _gdn_mm(x, y) for x, y in zip(t, inner)]
        bs *= 2
    return t


def _gdn_body(q_ref, k_ref, v_ref, cols_ref, gct_ref, z_ref, wn_ref, s0_ref,
              o_ref, s_ref, *, tb, chunk):
    @pl.when(pl.program_id(1) == 0)
    def _():
        s_ref[...] = s0_ref[...]

    shift = int(math.log2(chunk))
    r = lax.broadcasted_iota(jnp.int32, (tb, tb), 0)
    c = lax.broadcasted_iota(jnp.int32, (tb, tb), 1)
    same = (r >> shift) == (c >> shift)
    incl = same & (r >= c)
    strict = same & (r > c)
    n_chunks = tb // chunk

    heads = range(GDN_HEADS)
    hsl = [slice(h * HEAD_DIM, (h + 1) * HEAD_DIM) for h in heads]
    q = [q_ref[0, :, s] for s in hsl]
    k = [k_ref[0, :, s] for s in hsl]
    beta = [cols_ref[0, :, 8 + h:9 + h] for h in heads]
    eg = [cols_ref[0, :, 16 + h:17 + h] for h in heads]
    egl = [cols_ref[0, :, 24 + h:25 + h] for h in heads]
    decay = [jnp.exp(jnp.where(incl, cols_ref[0, :, h:h + 1] - gct_ref[0, h:h + 1, :], -jnp.inf)) for h in heads]
    kb = [k[h] * beta[h] for h in heads]
    k_bf = [x.astype(BF16) for x in k]
    a = [jnp.where(strict, _dot(kb[h].astype(BF16), k_bf[h], NT_DIMS) * decay[h], 0.0) for h in heads]
    tinv = _unit_lower_inverses(a, r, c, chunk)
    sol = [_gdn_mm(tinv[h], jnp.concatenate([v_ref[0, :, hsl[h]] * beta[h], kb[h] * eg[h]], axis=1)) for h in heads]
    u = [x[:, :HEAD_DIM] for x in sol]
    w_bf = [x[:, HEAD_DIM:].astype(BF16) for x in sol]
    attn = [(_dot(q[h].astype(BF16), k_bf[h], NT_DIMS) * decay[h]).astype(BF16) for h in heads]
    q_dec = [(q[h] * eg[h]).astype(BF16) for h in heads]
    k_dec = [(k[h] * egl[h]).astype(BF16) for h in heads]

    s = [s_ref[0, h] for h in heads]
    v_new = [[] for _ in heads]
    o_state = [[] for _ in heads]
    for ci in range(n_chunks):
        rs = slice(ci * chunk, (ci + 1) * chunk)
        s_bf = [x.astype(BF16) for x in s]
        vn = [u[h][rs] - _dot(w_bf[h][rs], s_bf[h]) for h in heads]
        for h in heads:
            o_state[h].append(_dot(q_dec[h][rs], s_bf[h]))
            v_new[h].append(vn[h])
        s = [s[h] * cols_ref[0, ci * chunk:ci * chunk + 1, 32 + h:33 + h]
             + _dot(k_dec[h][rs], vn[h].astype(BF16), TN_DIMS) for h in heads]
    for h in heads:
        s_ref[0, h] = s[h]

    def cat(parts):
        return parts[0] if len(parts) == 1 else jnp.concatenate(parts, axis=0)

    for h in heads:
        o = cat(o_state[h]) + _dot(attn[h], cat(v_new[h]).astype(BF16))
        zh = z_ref[0, :, hsl[h]]
        o = (o * lax.rsqrt(jnp.mean(o * o, axis=-1, keepdims=True) + RMS_EPS) * wn_ref[...]
             * (zh * _sigmoid(zh)))
        o_ref[0, :, hsl[h]] = o.astype(BF16)


def _gdn(q, k, v, cols, gct, proj3, wn, s0, chunk, z_block):
    b, t, d_gdn = q.shape
    tb = min(t, GDN_TILE)
    body = functools.partial(_gdn_body, tb=tb, chunk=chunk)
    tile = lambda i, j: (i, j, 0)
    return pl.pallas_call(
        body,
        grid=(b, t // tb),
        in_specs=[pl.BlockSpec((1, tb, d_gdn), tile),
                  pl.BlockSpec((1, tb, d_gdn), tile),
                  pl.BlockSpec((1, tb, d_gdn), tile),
                  pl.BlockSpec((1, tb, LANES), tile),
                  pl.BlockSpec((1, SUBLANES, tb), lambda i, j: (i, 0, j)),
                  pl.BlockSpec((1, tb, d_gdn), lambda i, j: (i, j, z_block)),
                  pl.BlockSpec((1, HEAD_DIM), lambda i, j: (0, 0)),
                  pl.BlockSpec((1, GDN_HEADS, HEAD_DIM, HEAD_DIM), lambda i, j: (i, 0, 0, 0))],
        out_specs=[pl.BlockSpec((1, tb, d_gdn), tile),
                   pl.BlockSpec((1, GDN_HEADS, HEAD_DIM, HEAD_DIM), lambda i, j: (i, 0, 0, 0))],
        out_shape=[jax.ShapeDtypeStruct((b, t, d_gdn), BF16),
                   jax.ShapeDtypeStruct((b, GDN_HEADS, HEAD_DIM, HEAD_DIM), F32)],
        compiler_params=_cparams(("parallel", "arbitrary")),
        name="gdn_delta",
    )(q, k, v, cols, gct, proj3, wn, s0)


def _gelu_tanh(x):
    return 0.5 * x * (1.0 + jnp.tanh(math.sqrt(2.0 / math.pi) * (x + 0.044715 * (x * x * x))))


def _s5_body(u_ref, wm_ref, ym0_ref, ym1_ref, ym2_ref, ym3_ref, lre_ref, lim_ref, dsk_ref, h0re_ref, h0im_ref,
             y_ref, hre_ref, him_ref, xbuf_ref, hbuf_ref, *, n_sub, seg):
    half = GROUPS_PER_BLOCK * SSM_P

    if seg is None:
        @pl.when(pl.program_id(2) == 0)
        def _():
            hre_ref[...] = h0re_ref[...]
            him_ref[...] = h0im_ref[...]

    u_f = [u_ref[0, pl.ds(j, n_sub, stride=S5_SUB), :] for j in range(S5_SUB)]
    u_b = [x.astype(BF16) for x in u_f]

    xbuf_ref[...] = _dot(jnp.concatenate(u_b, axis=1), wm_ref[0])

    lre = lre_ref[0]
    lim = lim_ref[0]

    def advance(n, hre, him):
        hbuf_ref[pl.ds(n, 1), 0:half] = hre
        hbuf_ref[pl.ds(n, 1), half:2 * half] = him
        xr = xbuf_ref[pl.ds(n, 1), 0:half]
        xi = xbuf_ref[pl.ds(n, 1), half:2 * half]
        return (lre * hre - lim * him + xr, lre * him + lim * hre + xi)

    if seg is None:
        hre, him = lax.fori_loop(0, n_sub, lambda n, c: advance(n, *c), (hre_ref[0, 0], him_ref[0, 0]))
        hre_ref[0, 0] = hre
        him_ref[0, 0] = him
    else:
        def step(n, carry):
            q = n // seg
            first = n - q * seg == 0
            hre = jnp.where(first, h0re_ref[q, 0], carry[0])
            him = jnp.where(first, h0im_ref[q, 0], carry[1])
            hre, him = advance(n, hre, him)
            hre_ref[q, 0] = hre
            him_ref[q, 0] = him
            return hre, him

        lax.fori_loop(0, n_sub, step, (h0re_ref[0, 0], h0im_ref[0, 0]))

    h_b = hbuf_ref[...].astype(BF16)
    dsk = dsk_ref[0]
    for l, ym_ref in zip(range(0, S5_SUB, 2), (ym0_ref, ym1_ref, ym2_ref, ym3_ref)):
        lhs = jnp.concatenate([h_b] + [u_b[m] for m in range(l + 1, -1, -1)], axis=1)
        y2 = _dot(lhs, ym_ref[0])
        for o in range(2):
            y = y2[:, o * LANES:(o + 1) * LANES] + dsk * u_f[l + o]
            y_ref[0, pl.ds(l + o, n_sub, stride=S5_SUB), :] = _gelu_tanh(y)


def _s5(proj3, mats, h0re, h0im, u_block0):
    b, t, width = proj3.shape
    n_gb = mats[0].shape[0]
    if b > 1 and b * t <= S5_TILE:
        y, hre, him = _s5_call(proj3.reshape(1, b * t, width), mats, h0re, h0im, u_block0,
                               tt=b * t, seg=t // S5_SUB, state_rows=b)
        return y.reshape(b, t, n_gb * LANES), hre, him
    return _s5_call(proj3, mats, h0re, h0im, u_block0, tt=min(t, S5_TILE), seg=None, state_rows=1)


def _s5_call(proj3, mats, h0re, h0im, u_block0, tt, seg, state_rows):
    wcat, ycat, lre, lim, dsk = mats
    b, t, _ = proj3.shape
    n_gb = wcat.shape[0]
    n_sub = tt // S5_SUB
    half = GROUPS_PER_BLOCK * SSM_P
    body = functools.partial(_s5_body, n_sub=n_sub, seg=seg)
    state_spec = pl.BlockSpec((state_rows, 1, 1, half), lambda g, i, j: (i, g, 0, 0))
    par_spec = pl.BlockSpec((1, 1, half), lambda g, i, j: (g, 0, 0))
    mat_spec = lambda a: pl.BlockSpec((1,) + a.shape[1:], lambda g, i, j: (g, 0, 0))
    return pl.pallas_call(
        body,
        grid=(n_gb, b, t // tt),
        in_specs=[pl.BlockSpec((1, tt, LANES), lambda g, i, j: (i, j, u_block0 + g)),
                  mat_spec(wcat)] + [mat_spec(a) for a in ycat] + [
                  par_spec, par_spec,
                  pl.BlockSpec((1, 1, LANES), lambda g, i, j: (g, 0, 0)),
                  state_spec, state_spec],
        out_specs=[pl.BlockSpec((1, tt, LANES), lambda g, i, j: (i, j, g)),
                   state_spec, state_spec],
        out_shape=[jax.ShapeDtypeStruct((b, t, n_gb * LANES), F32),
                   jax.ShapeDtypeStruct(h0re.shape, F32),
                   jax.ShapeDtypeStruct(h0im.shape, F32)],
        scratch_shapes=[pltpu.VMEM((n_sub, 2 * half), F32),
                        pltpu.VMEM((n_sub, 2 * half), F32)],
        compiler_params=_cparams(("parallel", "parallel", "arbitrary")),
        name="s5_scan",
    )(proj3, wcat, *ycat, lre, lim, dsk, h0re, h0im)


def _s5_matrices(lam_re, lam_im, log_dt, b_re, b_im, c_re, c_im, d_skip):
    g, p = lam_re.shape
    n_gb = g // GROUPS_PER_BLOCK
    gpb = GROUPS_PER_BLOCK
    dt = jnp.exp(log_dt.astype(F32))
    lam = lax.complex(jnp.minimum(lam_re.astype(F32), -1e-4), lam_im.astype(F32))
    lam_bar = jnp.exp(lam * dt[:, None])
    b_bar = ((lam_bar - 1.0) / lam)[..., None] * lax.complex(b_re.astype(F32), b_im.astype(F32))
    c_c = lax.complex(c_re.astype(F32), c_im.astype(F32))
    pows = [jnp.ones_like(lam_bar)]
    for _ in range(S5_SUB):
        pows.append(pows[-1] * lam_bar)
    pw = jnp.stack(pows)

    def block_diag(x):
        k = x.shape[-1]
        lanes = jnp.arange(gpb * k)
        rep = (lanes[None, :] % k == jnp.arange(k)[:, None]).astype(F32)
        wide = jnp.einsum('...k,kl->...l', x, rep, precision=lax.Precision.HIGHEST)
        own = (lanes[None, None, :] // k) == jnp.arange(gpb)[:, None, None]
        wide = jnp.where(own, wide, 0.0)
        return wide.reshape(x.shape[:-3] + (gpb * x.shape[-2], gpb * k))

    kd = jnp.real(jnp.einsum('gop,dgp,gpi->dgio', c_c, pw[:S5_SUB], b_bar))
    km = block_diag(kd.reshape(S5_SUB, n_gb, gpb, SSM_CG, SSM_CG))

    wj = pw[:S5_SUB][::-1][:, :, :, None] * b_bar[None]
    wj = jnp.transpose(wj.reshape(S5_SUB, n_gb, gpb, p, SSM_CG), (1, 0, 2, 4, 3))
    wcat = jnp.concatenate([block_diag(jnp.real(wj)), block_diag(jnp.imag(wj))], axis=-1)
    wcat = wcat.reshape(n_gb, S5_SUB * LANES, 2 * gpb * p).astype(BF16)

    cl = c_c[None] * pw[1:S5_SUB + 1][:, :, None, :]
    cl = jnp.transpose(cl, (0, 1, 3, 2)).reshape(S5_SUB, n_gb, gpb, p, SSM_CG)
    vm = jnp.concatenate([block_diag(jnp.real(cl)), block_diag(-jnp.imag(cl))], axis=-2)

    lam_s = pw[S5_SUB].reshape(n_gb, 1, gpb * p)
    dsk = d_skip.astype(F32).reshape(n_gb, 1, LANES)

    km, vm = km.astype(BF16), vm.astype(BF16)
    ycat = []
    zero = jnp.zeros_like(km[0])
    for l in range(0, S5_SUB, 2):
        rows = [jnp.concatenate([vm[l], vm[l + 1]], axis=-1)]
        for m in range(l + 1, -1, -1):
            left = km[l - m] if m <= l else zero
            rows.append(jnp.concatenate([left, km[l + 1 - m]], axis=-1))
        ycat.append(jnp.concatenate(rows, axis=-2))
    return (wcat, tuple(ycat), jnp.real(lam_s), jnp.imag(lam_s), dsk)


def _glu_body(y_ref, w_ref, b_ref, o_ref):
    y = y_ref[...]
    gate = _dot(y.astype(BF16), w_ref[...]) + b_ref[...]
    o_ref[...] = (y * _sigmoid(gate)).astype(BF16)


def _glu(y2, w_bf, b_row, tm):
    m, d = y2.shape
    return pl.pallas_call(
        _glu_body,
        grid=(m // tm,),
        in_specs=[pl.BlockSpec((tm, d), lambda i: (i, 0)),
                  pl.BlockSpec((d, d), lambda i: (0, 0)),
                  pl.BlockSpec((1, d), lambda i: (0, 0))],
        out_specs=pl.BlockSpec((tm, d), lambda i: (i, 0)),
        out_shape=jax.ShapeDtypeStruct((m, d), BF16),
        compiler_params=_cparams(("parallel",)),
        name="s5_glu",
    )(y2, w_bf, b_row)


def _layernorm(v, g, b):
    mu = jnp.mean(v, axis=-1, keepdims=True)
    var = jnp.mean(jnp.square(v - mu), axis=-1, keepdims=True)
    return (v - mu) * lax.rsqrt(var + LN_EPS) * g + b


def _mix_body(og_ref, os_ref, wa_ref, wb_ref, x_ref, g_ref, b_ref, wr_ref, br_ref,
              h_ref, hb_ref, route_ref, cnt_ref, *, alpha):
    @pl.when(pl.program_id(0) == 0)
    def _():
        cnt_ref[...] = jnp.zeros_like(cnt_ref)

    mix = _dot(og_ref[...], wa_ref[...]) + _dot(os_ref[...], wb_ref[...])
    h = _layernorm(alpha * x_ref[...] + mix, g_ref[...], b_ref[...])
    h_ref[...] = h
    tm = h.shape[0]
    _pack_rows(h, hb_ref, tm)

    logits = _dot_x3(h, wr_ref[...]) + br_ref[...]
    lane = lax.broadcasted_iota(jnp.int32, (tm, LANES), 1)
    work = jnp.where(lane < N_EXPERTS, logits, -jnp.inf)
    vals, idxs = [], []
    for _ in range(TOP_K):
        mx = jnp.max(work, axis=-1, keepdims=True)
        ix = jnp.min(jnp.where(work == mx, lane, LANES), axis=-1, keepdims=True)
        vals.append(mx)
        idxs.append(ix)
        work = jnp.where(lane == ix, -jnp.inf, work)
    exps = [jnp.exp(v - vals[0]) for v in vals]
    denom = exps[0]
    for e in exps[1:]:
        denom = denom + e
    chosen = jnp.zeros((tm, LANES), F32)
    for k in range(TOP_K):
        chosen = jnp.where(lane == idxs[k], 1.0, chosen)
    rr = lax.broadcasted_iota(jnp.int32, (tm, tm), 0)
    cc = lax.broadcasted_iota(jnp.int32, (tm, tm), 1)
    lower = jnp.where(rr > cc, 1.0, 0.0).astype(BF16)
    before = cnt_ref[...] + _dot(lower, chosen.astype(BF16))
    cnt_ref[...] = cnt_ref[...] + jnp.sum(chosen, axis=0, keepdims=True)

    route = jnp.zeros((tm, LANES), F32)
    for k in range(TOP_K):
        rank = jnp.sum(jnp.where(lane == idxs[k], before, 0.0), axis=-1, keepdims=True)
        route = jnp.where(lane == k, idxs[k].astype(F32), route)
        route = jnp.where(lane == TOP_K + k, exps[k] / denom, route)
        route = jnp.where(lane == 2 * TOP_K + k, rank, route)
    route_ref[...] = route


def _mix_ln_route(og, osm, wa, wb, x2, g_row, b_row, wr, br, alpha, tm):
    m, d = x2.shape
    dh = og.shape[1]
    body = functools.partial(_mix_body, alpha=alpha)
    row = lambda i: (i, 0)
    fix = lambda i: (0, 0)
    return pl.pallas_call(
        body,
        grid=(m // tm,),
        in_specs=[pl.BlockSpec((tm, dh), row), pl.BlockSpec((tm, dh), row),
                  pl.BlockSpec((dh, d), fix), pl.BlockSpec((dh, d), lambda i: (1, 0)),
                  pl.BlockSpec((tm, d), row),
                  pl.BlockSpec((1, d), fix), pl.BlockSpec((1, d), fix),
                  pl.BlockSpec((d, LANES), fix), pl.BlockSpec((1, LANES), fix)],
        out_specs=[pl.BlockSpec((tm, d), row), pl.BlockSpec((tm * ROW_SUB, LANES), row),
                   pl.BlockSpec((tm, LANES), row), pl.BlockSpec((1, LANES), fix)],
        out_shape=[jax.ShapeDtypeStruct((m, d), F32),
                   jax.ShapeDtypeStruct((m * ROW_SUB, LANES), jnp.uint32),
                   jax.ShapeDtypeStruct((m, LANES), F32),
                   jax.ShapeDtypeStruct((1, LANES), F32)],
        compiler_params=_cparams(("arbitrary",)),
        name="mix_ln_route",
    )(og, osm, wa, wb, x2, g_row, b_row, wr, br)


def _dispatch_body(pos_ref, src_ref, init_ref, dst_ref, sem, *, tt):
    del init_ref

    def issue(t, carry):
        srow = pl.multiple_of(t * ROW_SUB, ROW_SUB)
        for k in range(TOP_K):
            drow = pl.multiple_of(pos_ref[t * TOP_K + k] * ROW_SUB, ROW_SUB)
            pltpu.make_async_copy(src_ref.at[pl.ds(srow, ROW_SUB)], dst_ref.at[pl.ds(drow, ROW_SUB)],
                                  sem).start(priority=k % 2)
        return carry

    lax.fori_loop(0, tt, issue, 0, unroll=4)
    n = tt * ROW_SUB
    for _ in range(TOP_K):
        pltpu.make_async_copy(src_ref.at[pl.ds(0, n)], dst_ref.at[pl.ds(0, n)], sem).wait()


def _dispatch_rows(pos_flat, hbp, init, tt):
    m = hbp.shape[0] // ROW_SUB
    body = functools.partial(_dispatch_body, tt=tt)
    return pl.pallas_call(
        body,
        grid=(m // tt,),
        in_specs=[pl.BlockSpec((tt * TOP_K,), lambda i: (i,), memory_space=pltpu.SMEM),
                  pl.BlockSpec((tt * ROW_SUB, LANES), lambda i: (i, 0)), pl.BlockSpec(memory_space=pl.ANY)],
        out_specs=pl.BlockSpec(memory_space=pl.ANY),
        out_shape=jax.ShapeDtypeStruct(init.shape, jnp.uint32),
        scratch_shapes=[pltpu.SemaphoreType.DMA(())],
        input_output_aliases={2: 0},
        compiler_params=_cparams(("arbitrary",)),
        name="moe_dispatch",
    )(pos_flat, hbp, init)


MOE_COL = 256
MOE_GROUP = 2


def _moe_body(te_ref, nu_ref, x_ref, wg_ref, bg_ref, wu_ref, bu_ref, wd_ref, bd_ref,
              o_ref, xb_ref, acc_ref, *, n_f, tm, tf):
    g = pl.program_id(2)
    f = pl.program_id(1)
    used = pl.program_id(0) * MOE_GROUP + g < nu_ref[0]
    half = xb_ref.shape[2] // 2
    xb_ref = xb_ref.at[g]
    acc_ref = acc_ref.at[g]

    @pl.when(used & (f == 0))
    def _():
        lo, hi = _unpack_rows(x_ref, 0, tm)
        for r in range(ROW_SUB):
            xb_ref[:, r * LANES:(r + 1) * LANES] = lo[r].astype(BF16)
            xb_ref[:, half + r * LANES:half + (r + 1) * LANES] = hi[r].astype(BF16)

    @pl.when(used)
    def _():
        xb = xb_ref[...]
        part = None
        for c in range(tf // MOE_COL):
            cs = slice(c * MOE_COL, (c + 1) * MOE_COL)
            hg = jnp.minimum(_dot(xb, wg_ref[0, :, cs].astype(BF16)) + bg_ref[0, :, cs], SWIGLU_LIMIT)
            hu = jnp.clip(_dot(xb, wu_ref[0, :, cs].astype(BF16)) + bu_ref[0, :, cs], -SWIGLU_LIMIT, SWIGLU_LIMIT)
            hh = ((hu + 1.0) * (hg * _sigmoid(SWIGLU_ALPHA * hg))).astype(BF16)
            p = _dot(hh, wd_ref[0, cs, :].astype(BF16))
            part = p if part is None else part + p

        @pl.when(f == 0)
        def _():
            acc_ref[...] = part

        @pl.when(f != 0)
        def _():
            acc_ref[...] += part

        @pl.when(f == n_f - 1)
        def _():
            _pack_rows(acc_ref[...] + bd_ref[0], o_ref, tm)

    @pl.when(jnp.logical_not(used) & (f == n_f - 1))
    def _():
        o_ref[...] = jnp.zeros_like(o_ref)


def _moe_ffn(tile_expert, n_used, x_rows, w_gate, b_gate, w_up, b_up, w_down, b_down, tm, tf):
    n_e, d, d_ff = w_gate.shape
    n_tiles = x_rows.shape[0] // (tm * ROW_SUB)
    n_f = d_ff // tf
    grp = MOE_GROUP
    assert n_tiles % grp == 0
    body = functools.partial(_moe_body, n_f=n_f, tm=tm, tf=tf)

    def tile(p, g):
        return p * grp + g

    def fcol(p, f, nu):
        return jnp.where(p * grp < nu[0], f, n_f - 1)

    def xrow(p, f, g):
        return jnp.where(f == 0, tile(p, g), tile(p, grp - 1))

    def orow(p, f, g):
        return jnp.where(f == n_f - 1, tile(p, g), tile(p, 0))

    grid_spec = pltpu.PrefetchScalarGridSpec(
        num_scalar_prefetch=2,
        grid=(n_tiles // grp, n_f, grp),
        in_specs=[pl.BlockSpec((tm * ROW_SUB, LANES), lambda p, f, g, te, nu: (xrow(p, f, g), 0)),
                  pl.BlockSpec((1, d, tf), lambda p, f, g, te, nu: (te[tile(p, g)], 0, fcol(p, f, nu))),
                  pl.BlockSpec((1, 1, tf), lambda p, f, g, te, nu: (te[tile(p, g)], 0, fcol(p, f, nu))),
                  pl.BlockSpec((1, d, tf), lambda p, f, g, te, nu: (te[tile(p, g)], 0, fcol(p, f, nu))),
                  pl.BlockSpec((1, 1, tf), lambda p, f, g, te, nu: (te[tile(p, g)], 0, fcol(p, f, nu))),
                  pl.BlockSpec((1, tf, d), lambda p, f, g, te, nu: (te[tile(p, g)], fcol(p, f, nu), 0)),
                  pl.BlockSpec((1, 1, d), lambda p, f, g, te, nu: (te[tile(p, g)], 0, 0))],
        out_specs=pl.BlockSpec((tm * ROW_SUB, LANES), lambda p, f, g, te, nu: (orow(p, f, g), 0)),
        scratch_shapes=[pltpu.VMEM((grp, tm, d), BF16), pltpu.VMEM((grp, tm, d), F32)],
    )
    return pl.pallas_call(
        body,
        grid_spec=grid_spec,
        out_shape=jax.ShapeDtypeStruct(x_rows.shape, jnp.uint32),
        compiler_params=_cparams(("arbitrary", "arbitrary", "arbitrary")),
        name="moe_ffn",
    )(tile_expert, n_used, x_rows, w_gate, b_gate.reshape(n_e, 1, d_ff),
      w_up, b_up.reshape(n_e, 1, d_ff), w_down, b_down.reshape(n_e, 1, d))


def _final_body(pos_cur_ref, pos_nxt_ref, h_ref, route_ref, g_ref, b_ref, src_ref,
                o_ref, buf_ref, sem, *, alpha, tt, n_steps):
    i = pl.program_id(0)
    slot_rows = tt * TOP_K * ROW_SUB

    def issue(pref, slot):
        def one(t, carry):
            for k in range(TOP_K):
                srow = pl.multiple_of(pref[t * TOP_K + k] * ROW_SUB, ROW_SUB)
                drow = pl.multiple_of(slot * slot_rows + (k * tt + t) * ROW_SUB, ROW_SUB)
                pltpu.make_async_copy(src_ref.at[pl.ds(srow, ROW_SUB)], buf_ref.at[pl.ds(drow, ROW_SUB)],
                                      sem.at[slot]).start(priority=k % 2)
            return carry
        lax.fori_loop(0, tt, one, 0, unroll=4)

    @pl.when(i == 0)
    def _():
        issue(pos_cur_ref, 0)

    @pl.when(i + 1 < n_steps)
    def _():
        issue(pos_nxt_ref, (i + 1) % 2)

    slot = i % 2
    base = pl.multiple_of(slot * slot_rows, ROW_SUB)
    pltpu.make_async_copy(src_ref.at[pl.ds(0, slot_rows)], buf_ref.at[pl.ds(base, slot_rows)], sem.at[slot]).wait()

    ff_lo = [None] * ROW_SUB
    ff_hi = [None] * ROW_SUB
    for k in range(TOP_K):
        gate = route_ref[:, TOP_K + k:TOP_K + k + 1]
        lo, hi = _unpack_rows(buf_ref, base + k * tt * ROW_SUB, tt)
        for s in range(ROW_SUB):
            ff_lo[s] = lo[s] * gate if k == 0 else ff_lo[s] + lo[s] * gate
            ff_hi[s] = hi[s] * gate if k == 0 else ff_hi[s] + hi[s] * gate
    ff = jnp.concatenate(ff_lo + ff_hi, axis=1)
    o_ref[...] = _layernorm(alpha * h_ref[...] + ff, g_ref[...], b_ref[...])


def _combine_ln(pos_flat, h, route, g_row, b_row, outs, alpha, tt):
    m, d = h.shape
    n_steps = m // tt
    body = functools.partial(_final_body, alpha=alpha, tt=tt, n_steps=n_steps)
    row = lambda i: (i, 0)
    fix = lambda i: (0, 0)
    return pl.pallas_call(
        body,
        grid=(n_steps,),
        in_specs=[pl.BlockSpec((tt * TOP_K,), lambda i: (i,), memory_space=pltpu.SMEM),
                  pl.BlockSpec((tt * TOP_K,), lambda i: (jnp.minimum(i + 1, n_steps - 1),),
                               memory_space=pltpu.SMEM),
                  pl.BlockSpec((tt, d), row), pl.BlockSpec((tt, LANES), row),
                  pl.BlockSpec((1, d), fix), pl.BlockSpec((1, d), fix),
                  pl.BlockSpec(memory_space=pl.ANY)],
        out_specs=pl.BlockSpec((tt, d), row),
        out_shape=jax.ShapeDtypeStruct((m, d), F32),
        scratch_shapes=[pltpu.VMEM((2 * tt * TOP_K * ROW_SUB, LANES), jnp.uint32),
                        pltpu.SemaphoreType.DMA((2,))],
        compiler_params=_cparams(("arbitrary",)),
        name="combine_ln",
    )(pos_flat, pos_flat, h, route, g_row, b_row, outs)


def _row_tile(m, pref):
    t = min(m, pref)
    while m % t:
        t //= 2
    return t


def _route_tables(routes, counts_rows, tm):
    counts = [c[0, :N_EXPERTS].astype(jnp.int32) for c in counts_rows]
    total = functools.reduce(lambda a, b: a + b, counts)
    padded = (total + tm - 1) // tm * tm
    pends = jnp.cumsum(padded)
    pstarts = pends - padded
    pos, before = [], jnp.zeros_like(total)
    for route, cnt in zip(routes, counts):
        e_tok = route[:, :TOP_K].astype(jnp.int32)
        rank = route[:, 2 * TOP_K:3 * TOP_K].astype(jnp.int32)
        pos.append(((pstarts + before)[e_tok] + rank).reshape(-1))
        before = before + cnt
    n_assign = sum(r.shape[0] for r in routes) * TOP_K
    n_tiles = -(-n_assign // tm) + N_EXPERTS
    n_tiles = -(-n_tiles // MOE_GROUP) * MOE_GROUP
    tile_start = jnp.arange(n_tiles, dtype=jnp.int32) * tm
    tile_expert = jnp.minimum(jnp.sum((pends[None, :] <= tile_start[:, None]).astype(jnp.int32), axis=1),
                              N_EXPERTS - 1)
    n_used = (pends[-1] // tm).astype(jnp.int32).reshape(1)
    last_used = jnp.maximum(n_used[0] - 1, 0)
    tile_expert = jnp.where(tile_start // tm < n_used[0], tile_expert, tile_expert[last_used])
    return pos, tile_expert, n_used


def _moe_joint(parts, alpha, p):
    tm = p['moe_tm']
    pos, tile_expert, n_used = _route_tables([q['route'] for q in parts], [q['counts'] for q in parts], tm)
    cap = tile_expert.shape[0] * tm
    x_rows = jnp.zeros((cap * ROW_SUB, LANES), jnp.uint32)
    for q, pq in zip(parts, pos):
        x_rows = _dispatch_rows(pq, q['hb'], x_rows, _row_tile(q['h'].shape[0], 512))
    outs = _moe_ffn(tile_expert, n_used, x_rows, p['w_gate'], p['b_gate'],
                    p['w_up'], p['b_up'], p['w_down'], p['b_down'], tm, p['moe_tf'])
    return [_combine_ln(pq, q['h'], q['route'], p['ln2_g'], p['ln2_b'], outs, alpha, _row_tile(q['h'].shape[0], 256))
            for q, pq in zip(parts, pos)]


def _layer_pre(x, conv_hist, s_gdn, h_re, h_im, chunk, alpha, p):
    b, t, d = x.shape
    m = b * t
    d_gdn = GDN_HEADS * HEAD_DIM
    x2 = x.reshape(m, d)

    proj = _in_proj(x2, p['w_in'], _row_tile(m, 1024), p['proj_tn'])
    proj3 = proj.reshape(b, t, proj.shape[1])
    conv_new = proj3[:, t - (CONV_W - 1):, :3 * d_gdn]

    q, k, v, cols, gct = _gdn_prep(proj3, conv_hist, p['w_conv'], p['alog_row'], p['dtb_row'],
                                   chunk, p['ab_block'])
    o_gdn, s_new = _gdn(q, k, v, cols, gct, proj3, p['wn_row'], s_gdn, chunk, p['z_block'])

    n_gb = p['s5_mats'][0].shape[0]
    half = GROUPS_PER_BLOCK * SSM_P
    yg, hre_new, him_new = _s5(proj3, p['s5_mats'], h_re.reshape(b, n_gb, 1, half),
                               h_im.reshape(b, n_gb, 1, half), p['u_block0'])
    o_ssm = _glu(yg.reshape(m, yg.shape[2]), p['w_glu'], p['b_glu_row'], _row_tile(m, 512))

    h, hb, route, counts_row = _mix_ln_route(o_gdn.reshape(m, d_gdn), o_ssm, p['w_out'], p['w_out'], x2,
                                             p['ln1_g'], p['ln1_b'], p['w_router'], p['b_router'], alpha,
                                             _row_tile(m, 256))

    g_all = h_re.shape[1]
    return {'h': h, 'hb': hb, 'route': route, 'counts': counts_row, 'shape': (b, t, d),
            'state': (conv_new, s_new, hre_new.reshape(b, g_all, SSM_P), him_new.reshape(b, g_all, SSM_P))}


def _pad_lanes(v, fill=0.0):
    return jnp.pad(v.astype(F32), (0, LANES - v.shape[0]), constant_values=fill).reshape(1, LANES)


def _layer_params(l, w_in, w_conv, a_log, dt_bias, w_onorm, lam_re, lam_im, log_dt, b_re, b_im, c_re, c_im,
                  d_skip, w_glu, b_glu, w_out, ln1_g, ln1_b, w_router, b_router, w_gate, b_gate,
                  w_up, b_up, w_down, b_down, ln2_g, ln2_b):
    d_model = w_in.shape[1]
    d_gdn = GDN_HEADS * HEAD_DIM
    d_qkvz = 4 * d_gdn
    d_ssm = d_model - d_gdn
    wi = w_in[l]
    proj_tn = 1792
    n_cols = d_qkvz + d_ssm + LANES
    n_pad = -(-n_cols // proj_tn) * proj_tn
    w_in_r = jnp.concatenate([wi[:, :d_qkvz], wi[:, d_qkvz + 2 * GDN_HEADS:],
                              wi[:, d_qkvz:d_qkvz + 2 * GDN_HEADS],
                              jnp.zeros((d_model, n_pad - d_qkvz - d_ssm - 2 * GDN_HEADS), wi.dtype)], axis=1)
    wo = w_out[l].astype(BF16)
    wr = jnp.pad(w_router[l].astype(F32), ((0, 0), (0, LANES - N_EXPERTS)))
    return {
        'w_in': w_in_r.astype(BF16), 'proj_tn': proj_tn,
        'z_block': 3, 'u_block0': (d_qkvz) // LANES, 'ab_block': (d_qkvz + d_ssm) // LANES,
        'w_conv': w_conv[l].astype(F32),
        'alog_row': _pad_lanes(a_log[l]), 'dtb_row': _pad_lanes(dt_bias[l]),
        'wn_row': w_onorm[l].astype(F32).reshape(1, HEAD_DIM),
        's5_mats': _s5_matrices(lam_re[l], lam_im[l], log_dt[l], b_re[l], b_im[l], c_re[l], c_im[l], d_skip[l]),
        'w_glu': w_glu[l].astype(BF16), 'b_glu_row': b_glu[l].astype(F32).reshape(1, d_ssm),
        'w_out': wo,
        'ln1_g': ln1_g[l].astype(F32).reshape(1, d_model), 'ln1_b': ln1_b[l].astype(F32).reshape(1, d_model),
        'w_router': wr, 'b_router': _pad_lanes(b_router[l]),
        'w_gate': w_gate[l], 'b_gate': b_gate[l], 'w_up': w_up[l], 'b_up': b_up[l],
        'w_down': w_down[l], 'b_down': b_down[l],
        'ln2_g': ln2_g[l].astype(F32).reshape(1, d_model), 'ln2_b': ln2_b[l].astype(F32).reshape(1, d_model),
        'moe_tm': 576, 'moe_tf': 512,
    }


def kernel(x_prompt, x_sample, state_conv, state_gdn, state_ssm_re, state_ssm_im, w_in, w_conv, a_log, dt_bias, w_onorm, lam_re, lam_im, log_dt, b_re, b_im, c_re, c_im, d_skip, w_glu, b_glu, w_out, ln1_g, ln1_b, w_router, b_router, w_gate, b_gate, w_up, b_up, w_down, b_down, ln2_g, ln2_b):
    depth = w_in.shape[0]
    alpha = (2.0 * depth) ** 0.25
    bp, seq, _ = x_prompt.shape
    chunk_p = 64
    d_qkv = state_conv.shape[-1]
    n_groups, n_p = state_ssm_re.shape[-2:]
    yp, ys = x_prompt, x_sample
    outs_p = [[], [], [], []]
    outs_s = [[], [], [], []]
    for l in range(depth):
        p = _layer_params(l, w_in, w_conv, a_log, dt_bias, w_onorm, lam_re, lam_im, log_dt, b_re, b_im,
                          c_re, c_im, d_skip, w_glu, b_glu, w_out, ln1_g, ln1_b, w_router, b_router,
                          w_gate, b_gate, w_up, b_up, w_down, b_down, ln2_g, ln2_b)
        part_p = _layer_pre(
            yp, jnp.zeros((bp, CONV_W - 1, d_qkv), F32),
            jnp.zeros((bp, GDN_HEADS, HEAD_DIM, HEAD_DIM), F32),
            jnp.zeros((bp, n_groups, n_p), F32), jnp.zeros((bp, n_groups, n_p), F32),
            chunk_p, alpha, p)
        part_s = _layer_pre(
            ys, state_conv[l].astype(F32), state_gdn[l].astype(F32),
            state_ssm_re[l].astype(F32), state_ssm_im[l].astype(F32),
            ys.shape[1], alpha, p)
        yp2, ys2 = _moe_joint([part_p, part_s], alpha, p)
        yp, ys = yp2.reshape(part_p['shape']), ys2.reshape(part_s['shape'])
        for acc, val in zip(outs_p, part_p['state']):
            acc.append(val)
        for acc, val in zip(outs_s, part_s['state']):
            acc.append(val)
    return (yp, ys, *[jnp.stack(a) for a in outs_p], *[jnp.stack(a) for a in outs_s])
```

```python
import functools
import math

import jax
import jax.numpy as jnp
from jax import lax
from jax.experimental import pallas as pl
from jax.experimental.pallas import tpu as pltpu

F32 = jnp.float32
BF16 = jnp.bfloat16

GDN_HEADS = 8
HEAD_DIM = 128
CONV_W = 4
SSM_CG = 16
SSM_P = 64
N_EXPERTS = 32
TOP_K = 4
SWIGLU_ALPHA = 1.702
SWIGLU_LIMIT = 7.0
LN_EPS = 1e-5
RMS_EPS = 1e-6
L2_EPS = 1e-6

LANES = 128
SUBLANES = 8
S5_SUB = 8
S5_TILE = 2048
GDN_TILE = 128
GROUPS_PER_BLOCK = LANES // SSM_CG
VMEM_LIMIT = 56 * 1024 * 1024

NT_DIMS = (((1,), (1,)), ((), ()))
TN_DIMS = (((0,), (0,)), ((), ()))


def _dot(a, b, dims=(((1,), (0,)), ((), ()))):
    return lax.dot_general(a, b, dims, preferred_element_type=F32)


def _split(a):
    hi = a.astype(BF16)
    lo = (a - hi.astype(F32)).astype(BF16)
    return hi, lo


def _dot_x3(a, b, dims=(((1,), (0,)), ((), ()))):
    ah, al = _split(a)
    bh, bl = _split(b)
    return _dot(ah, bh, dims) + (_dot(ah, bl, dims) + _dot(al, bh, dims))


def _dot_bf(a, b, dims=(((1,), (0,)), ((), ()))):
    return _dot(a.astype(BF16), b.astype(BF16), dims)


_gdn_mm = _dot_bf


def _sigmoid(x):
    return 1.0 / (1.0 + jnp.exp(-x))


ROW_SUB = SUBLANES


def _pack_rows(x, ref, tm):
    half = x.shape[1] // 2
    for s in range(ROW_SUB):
        lo = x[:, s * LANES:(s + 1) * LANES].astype(BF16).astype(F32)
        hi = x[:, half + s * LANES:half + (s + 1) * LANES].astype(BF16).astype(F32)
        word = (lax.bitcast_convert_type(lo, jnp.uint32) >> 16) | lax.bitcast_convert_type(hi, jnp.uint32)
        ref[pl.ds(s, tm, stride=ROW_SUB), :] = word


def _unpack_rows(ref, base, tm):
    lo, hi = [], []
    for s in range(ROW_SUB):
        word = ref[pl.ds(base + s, tm, stride=ROW_SUB), :]
        lo.append(lax.bitcast_convert_type(word << 16, F32))
        hi.append(lax.bitcast_convert_type(word & jnp.uint32(0xFFFF0000), F32))
    return lo, hi


def _cparams(sem):
    return pltpu.CompilerParams(dimension_semantics=sem, vmem_limit_bytes=VMEM_LIMIT)


def _proj_body(x_ref, w_ref, o_ref, xb_ref):
    @pl.when(pl.program_id(1) == 0)
    def _():
        xb_ref[...] = x_ref[...].astype(BF16)

    o_ref[...] = _dot(xb_ref[...], w_ref[...])


def _in_proj(x2, w_bf, tm, tn):
    m, k = x2.shape
    n = w_bf.shape[1]
    return pl.pallas_call(
        _proj_body,
        grid=(m // tm, n // tn),
        in_specs=[pl.BlockSpec((tm, k), lambda i, j: (i, 0)),
                  pl.BlockSpec((k, tn), lambda i, j: (0, j))],
        out_specs=pl.BlockSpec((tm, tn), lambda i, j: (i, j)),
        out_shape=jax.ShapeDtypeStruct((m, n), F32),
        scratch_shapes=[pltpu.VMEM((tm, k), BF16)],
        compiler_params=_cparams(("parallel", "arbitrary")),
        name="in_proj",
    )(x2, w_bf)


def _gdn_prep_body(qkv_ref, ab_ref, hist_ref, wc_ref, alog_ref, dtb_ref,
                   q_ref, k_ref, v_ref, cols_ref, gct_ref, xbuf_ref, *, tt, chunk):
    d_gdn = GDN_HEADS * HEAD_DIM
    halo = SUBLANES

    @pl.when(pl.program_id(1) == 0)
    def _():
        xbuf_ref[0:halo, :] = jnp.zeros((halo, 3 * d_gdn), F32)
        xbuf_ref[halo - (CONV_W - 1):halo, :] = hist_ref[0]

    xbuf_ref[halo:halo + tt, :] = qkv_ref[0]

    for part, out_ref in enumerate((q_ref, k_ref, v_ref)):
        c0 = part * d_gdn
        y = None
        for j in range(CONV_W):
            r0 = halo - (CONV_W - 1) + j
            term = xbuf_ref[r0:r0 + tt, c0:c0 + d_gdn] * wc_ref[j:j + 1, c0:c0 + d_gdn]
            y = term if y is None else y + term
        s = y * _sigmoid(y)
        if part == 2:
            out_ref[0] = s
        else:
            for h in range(GDN_HEADS):
                sh = s[:, h * HEAD_DIM:(h + 1) * HEAD_DIM]
                nrm = sh * lax.rsqrt(jnp.sum(sh * sh, axis=-1, keepdims=True) + L2_EPS)
                if part == 0:
                    nrm = nrm * (HEAD_DIM ** -0.5)
                out_ref[0, :, h * HEAD_DIM:(h + 1) * HEAD_DIM] = nrm

    xbuf_ref[0:halo, :] = xbuf_ref[tt:tt + halo, :]

    ab = ab_ref[0]
    lane = lax.broadcasted_iota(jnp.int32, (tt, LANES), 1)
    is_a = lane < GDN_HEADS
    z = ab + dtb_ref[...]
    softplus = jnp.maximum(z, 0.0) + jnp.log1p(jnp.exp(-jnp.abs(z)))
    g = jnp.where(is_a, -jnp.exp(alog_ref[...]) * softplus, 0.0)
    beta = _sigmoid(ab)

    shift = int(math.log2(chunk))
    r = lax.broadcasted_iota(jnp.int32, (tt, tt), 0)
    c = lax.broadcasted_iota(jnp.int32, (tt, tt), 1)
    same = (r >> shift) == (c >> shift)
    m_incl = jnp.where(same & (r >= c), 1.0, 0.0).astype(BF16)
    m_all = jnp.where(same, 1.0, 0.0).astype(BF16)
    gc = _dot_exact_lhs_rhs(m_incl, g)
    glast = _dot_exact_lhs_rhs(m_all, g)
    eg = jnp.exp(gc)
    egl = jnp.exp(glast - gc)
    egt = jnp.exp(glast)
    zero = jnp.zeros_like(gc)
    cols = (jnp.where(is_a, gc, zero)
            + jnp.where((lane >= 8) & (lane < 16), beta, zero)
            + pltpu.roll(jnp.where(is_a, eg, zero), 16, 1)
            + pltpu.roll(jnp.where(is_a, egl, zero), 24, 1)
            + pltpu.roll(jnp.where(is_a, egt, zero), 32, 1))
    cols_ref[0] = cols

    er = lax.broadcasted_iota(jnp.int32, (SUBLANES, LANES), 0)
    ec = lax.broadcasted_iota(jnp.int32, (SUBLANES, LANES), 1)
    sel = jnp.where(er == ec, 1.0, 0.0).astype(BF16)
    gct_ref[0] = _dot_exact_rhs(sel, jnp.where(is_a, gc, zero), NT_DIMS)


def _three_pieces(a):
    a0 = a.astype(BF16)
    r1 = a - a0.astype(F32)
    a1 = r1.astype(BF16)
    a2 = (r1 - a1.astype(F32)).astype(BF16)
    return a0, a1, a2


def _dot_exact_lhs_rhs(mask_bf, a):
    a0, a1, a2 = _three_pieces(a)
    return _dot(mask_bf, a0) + (_dot(mask_bf, a1) + _dot(mask_bf, a2))


def _dot_exact_rhs(mask_bf, a, dims):
    a0, a1, a2 = _three_pieces(a)
    return _dot(mask_bf, a0, dims) + (_dot(mask_bf, a1, dims) + _dot(mask_bf, a2, dims))


def _gdn_prep(proj3, hist, w_conv, alog_p, dtb_p, chunk, ab_block):
    b, t, _ = proj3.shape
    d_gdn = GDN_HEADS * HEAD_DIM
    tt = min(t, GDN_TILE)
    body = functools.partial(_gdn_prep_body, tt=tt, chunk=chunk)
    big = jax.ShapeDtypeStruct((b, t, d_gdn), F32)
    return pl.pallas_call(
        body,
        grid=(b, t // tt),
        in_specs=[pl.BlockSpec((1, tt, 3 * d_gdn), lambda i, j: (i, j, 0)),
                  pl.BlockSpec((1, tt, LANES), lambda i, j: (i, j, ab_block)),
                  pl.BlockSpec((1, CONV_W - 1, 3 * d_gdn), lambda i, j: (i, 0, 0)),
                  pl.BlockSpec((CONV_W, 3 * d_gdn), lambda i, j: (0, 0)),
                  pl.BlockSpec((1, LANES), lambda i, j: (0, 0)),
                  pl.BlockSpec((1, LANES), lambda i, j: (0, 0))],
        out_specs=[pl.BlockSpec((1, tt, d_gdn), lambda i, j: (i, j, 0)),
                   pl.BlockSpec((1, tt, d_gdn), lambda i, j: (i, j, 0)),
                   pl.BlockSpec((1, tt, d_gdn), lambda i, j: (i, j, 0)),
                   pl.BlockSpec((1, tt, LANES), lambda i, j: (i, j, 0)),
                   pl.BlockSpec((1, SUBLANES, tt), lambda i, j: (i, 0, j))],
        out_shape=[big, big, big,
                   jax.ShapeDtypeStruct((b, t, LANES), F32),
                   jax.ShapeDtypeStruct((b, SUBLANES, t), F32)],
        scratch_shapes=[pltpu.VMEM((tt + SUBLANES, 3 * d_gdn), F32)],
        compiler_params=_cparams(("parallel", "arbitrary")),
        name="gdn_prep",
    )(proj3, proj3, hist, w_conv, alog_p, dtb_p)


def _unit_lower_inverses(mats, r, c, chunk):
    base = 16
    eye = jnp.where(r == c, 1.0, 0.0)

    def blk(bs):
        s = int(math.log2(bs))
        return (r >> s) == (c >> s)

    d1 = [jnp.where(blk(base), a, 0.0) for a in mats]
    d2 = [_gdn_mm(x, x) for x in d1]
    d4 = [_gdn_mm(x, x) for x in d2]
    d8 = [_gdn_mm(x, x) for x in d4]
    t = [eye - x for x in d1]
    t = [x + _gdn_mm(x, y) for x, y in zip(t, d2)]
    t = [x + _gdn_mm(x, y) for x, y in zip(t, d4)]
    t = [x + _gdn_mm(x, y) for x, y in zip(t, d8)]
    bs = base
    while bs < chunk:
        off_mask = blk(2 * bs) & jnp.logical_not(blk(bs))
        inner = [_gdn_mm(jnp.where(off_mask, a, 0.0), x) for a, x in zip(mats, t)]
        t = [x - _gdn_mm(x, y) for x, y in zip(t, inner)]
        bs *= 2
    return t


def _gdn_body(q_ref, k_ref, v_ref, cols_ref, gct_ref, z_ref, wn_ref, s0_ref,
              o_ref, s_ref, *, tb, chunk):
    @pl.when(pl.program_id(1) == 0)
    def _():
        s_ref[...] = s0_ref[...]

    shift = int(math.log2(chunk))
    r = lax.broadcasted_iota(jnp.int32, (tb, tb), 0)
    c = lax.broadcasted_iota(jnp.int32, (tb, tb), 1)
    same = (r >> shift) == (c >> shift)
    incl = same & (r >= c)
    strict = same & (r > c)
    n_chunks = tb // chunk

    heads = range(GDN_HEADS)
    hsl = [slice(h * HEAD_DIM, (h + 1) * HEAD_DIM) for h in heads]
    q = [q_ref[0, :, s] for s in hsl]
    k = [k_ref[0, :, s] for s in hsl]
    beta = [cols_ref[0, :, 8 + h:9 + h] for h in heads]
    eg = [cols_ref[0, :, 16 + h:17 + h] for h in heads]
    egl = [cols_ref[0, :, 24 + h:25 + h] for h in heads]
    decay = [jnp.exp(jnp.where(incl, cols_ref[0, :, h:h + 1] - gct_ref[0, h:h + 1, :], -jnp.inf)) for h in heads]
    kb = [k[h] * beta[h] for h in heads]
    k_bf = [x.astype(BF16) for x in k]
    a = [jnp.where(strict, _dot(kb[h].astype(BF16), k_bf[h], NT_DIMS) * decay[h], 0.0) for h in heads]
    tinv = _unit_lower_inverses(a, r, c, chunk)
    sol = [_gdn_mm(tinv[h], jnp.concatenate([v_ref[0, :, hsl[h]] * beta[h], kb[h] * eg[h]], axis=1)) for h in heads]
    u = [x[:, :HEAD_DIM] for x in sol]
    w_bf = [x[:, HEAD_DIM:].astype(BF16) for x in sol]
    attn = [(_dot(q[h].astype(BF16), k_bf[h], NT_DIMS) * decay[h]).astype(BF16) for h in heads]
    q_dec = [(q[h] * eg[h]).astype(BF16) for h in heads]
    k_dec = [(k[h] * egl[h]).astype(BF16) for h in heads]

    s = [s_ref[0, h] for h in heads]
    v_new = [[] for _ in heads]
    o_state = [[] for _ in heads]
    for ci in range(n_chunks):
        rs = slice(ci * chunk, (ci + 1) * chunk)
        s_bf = [x.astype(BF16) for x in s]
        vn = [u[h][rs] - _dot(w_bf[h][rs], s_bf[h]) for h in heads]
        for h in heads:
            o_state[h].append(_dot(q_dec[h][rs], s_bf[h]))
            v_new[h].append(vn[h])
        s = [s[h] * cols_ref[0, ci * chunk:ci * chunk + 1, 32 + h:33 + h]
             + _dot(k_dec[h][rs], vn[h].astype(BF16), TN_DIMS) for h in heads]
    for h in heads:
        s_ref[0, h] = s[h]

    def cat(parts):
        return parts[0] if len(parts) == 1 else jnp.concatenate(parts, axis=0)

    for h in heads:
        o = cat(o_state[h]) + _dot(attn[h], cat(v_new[h]).astype(BF16))
        zh = z_ref[0, :, hsl[h]]
        o = (o * lax.rsqrt(jnp.mean(o * o, axis=-1, keepdims=True) + RMS_EPS) * wn_ref[...]
             * (zh * _sigmoid(zh)))
        o_ref[0, :, hsl[h]] = o.astype(BF16)


def _gdn(q, k, v, cols, gct, proj3, wn, s0, chunk, z_block):
    b, t, d_gdn = q.shape
    tb = min(t, GDN_TILE)
    body = functools.partial(_gdn_body, tb=tb, chunk=chunk)
    tile = lambda i, j: (i, j, 0)
    return pl.pallas_call(
        body,
        grid=(b, t // tb),
        in_specs=[pl.BlockSpec((1, tb, d_gdn), tile),
                  pl.BlockSpec((1, tb, d_gdn), tile),
                  pl.BlockSpec((1, tb, d_gdn), tile),
                  pl.BlockSpec((1, tb, LANES), tile),
                  pl.BlockSpec((1, SUBLANES, tb), lambda i, j: (i, 0, j)),
                  pl.BlockSpec((1, tb, d_gdn), lambda i, j: (i, j, z_block)),
                  pl.BlockSpec((1, HEAD_DIM), lambda i, j: (0, 0)),
                  pl.BlockSpec((1, GDN_HEADS, HEAD_DIM, HEAD_DIM), lambda i, j: (i, 0, 0, 0))],
        out_specs=[pl.BlockSpec((1, tb, d_gdn), tile),
                   pl.BlockSpec((1, GDN_HEADS, HEAD_DIM, HEAD_DIM), lambda i, j: (i, 0, 0, 0))],
        out_shape=[jax.ShapeDtypeStruct((b, t, d_gdn), BF16),
                   jax.ShapeDtypeStruct((b, GDN_HEADS, HEAD_DIM, HEAD_DIM), F32)],
        compiler_params=_cparams(("parallel", "arbitrary")),
        name="gdn_delta",
    )(q, k, v, cols, gct, proj3, wn, s0)


def _gelu_tanh(x):
    return 0.5 * x * (1.0 + jnp.tanh(math.sqrt(2.0 / math.pi) * (x + 0.044715 * (x * x * x))))


def _s5_body(u_ref, wm_ref, ym0_ref, ym1_ref, ym2_ref, ym3_ref, lre_ref, lim_ref, dsk_ref, h0re_ref, h0im_ref,
             y_ref, hre_ref, him_ref, xbuf_ref, hbuf_ref, *, n_sub, seg):
    half = GROUPS_PER_BLOCK * SSM_P

    if seg is None:
        @pl.when(pl.program_id(2) == 0)
        def _():
            hre_ref[...] = h0re_ref[...]
            him_ref[...] = h0im_ref[...]

    u_f = [u_ref[0, pl.ds(j, n_sub, stride=S5_SUB), :] for j in range(S5_SUB)]
    u_b = [x.astype(BF16) for x in u_f]

    xbuf_ref[...] = _dot(jnp.concatenate(u_b, axis=1), wm_ref[0])

    lre = lre_ref[0]
    lim = lim_ref[0]

    def advance(n, hre, him):
        hbuf_ref[pl.ds(n, 1), 0:half] = hre
        hbuf_ref[pl.ds(n, 1), half:2 * half] = him
        xr = xbuf_ref[pl.ds(n, 1), 0:half]
        xi = xbuf_ref[pl.ds(n, 1), half:2 * half]
        return (lre * hre - lim * him + xr, lre * him + lim * hre + xi)

    if seg is None:
        hre, him = lax.fori_loop(0, n_sub, lambda n, c: advance(n, *c), (hre_ref[0, 0], him_ref[0, 0]),
                                 unroll=8)
        hre_ref[0, 0] = hre
        him_ref[0, 0] = him
    else:
        def step(n, carry):
            q = n // seg
            first = n - q * seg == 0
            hre = jnp.where(first, h0re_ref[q, 0], carry[0])
            him = jnp.where(first, h0im_ref[q, 0], carry[1])
            hre, him = advance(n, hre, him)
            hre_ref[q, 0] = hre
            him_ref[q, 0] = him
            return hre, him

        lax.fori_loop(0, n_sub, step, (h0re_ref[0, 0], h0im_ref[0, 0]))

    h_b = hbuf_ref[...].astype(BF16)
    dsk = dsk_ref[0]
    for l, ym_ref in zip(range(0, S5_SUB, 2), (ym0_ref, ym1_ref, ym2_ref, ym3_ref)):
        lhs = jnp.concatenate([h_b] + [u_b[m] for m in range(l + 1, -1, -1)], axis=1)
        y2 = _dot(lhs, ym_ref[0])
        for o in range(2):
            y = y2[:, o * LANES:(o + 1) * LANES] + dsk * u_f[l + o]
            y_ref[0, pl.ds(l + o, n_sub, stride=S5_SUB), :] = _gelu_tanh(y)


def _s5(proj3, mats, h0re, h0im, u_block0):
    b, t, width = proj3.shape
    n_gb = mats[0].shape[0]
    if b > 1 and b * t <= S5_TILE:
        y, hre, him = _s5_call(proj3.reshape(1, b * t, width), mats, h0re, h0im, u_block0,
                               tt=b * t, seg=t // S5_SUB, state_rows=b)
        return y.reshape(b, t, n_gb * LANES), hre, him
    return _s5_call(proj3, mats, h0re, h0im, u_block0, tt=min(t, S5_TILE), seg=None, state_rows=1)


def _s5_call(proj3, mats, h0re, h0im, u_block0, tt, seg, state_rows):
    wcat, ycat, lre, lim, dsk = mats
    b, t, _ = proj3.shape
    n_gb = wcat.shape[0]
    n_sub = tt // S5_SUB
    half = GROUPS_PER_BLOCK * SSM_P
    body = functools.partial(_s5_body, n_sub=n_sub, seg=seg)
    state_spec = pl.BlockSpec((state_rows, 1, 1, half), lambda g, i, j: (i, g, 0, 0))
    par_spec = pl.BlockSpec((1, 1, half), lambda g, i, j: (g, 0, 0))
    mat_spec = lambda a: pl.BlockSpec((1,) + a.shape[1:], lambda g, i, j: (g, 0, 0))
    return pl.pallas_call(
        body,
        grid=(n_gb, b, t // tt),
        in_specs=[pl.BlockSpec((1, tt, LANES), lambda g, i, j: (i, j, u_block0 + g)),
                  mat_spec(wcat)] + [mat_spec(a) for a in ycat] + [
                  par_spec, par_spec,
                  pl.BlockSpec((1, 1, LANES), lambda g, i, j: (g, 0, 0)),
                  state_spec, state_spec],
        out_specs=[pl.BlockSpec((1, tt, LANES), lambda g, i, j: (i, j, g)),
                   state_spec, state_spec],
        out_shape=[jax.ShapeDtypeStruct((b, t, n_gb * LANES), F32),
                   jax.ShapeDtypeStruct(h0re.shape, F32),
                   jax.ShapeDtypeStruct(h0im.shape, F32)],
        scratch_shapes=[pltpu.VMEM((n_sub, 2 * half), F32),
                        pltpu.VMEM((n_sub, 2 * half), F32)],
        compiler_params=_cparams(("parallel", "parallel", "arbitrary")),
        name="s5_scan",
    )(proj3, wcat, *ycat, lre, lim, dsk, h0re, h0im)


def _s5_matrices(lam_re, lam_im, log_dt, b_re, b_im, c_re, c_im, d_skip):
    g, p = lam_re.shape
    n_gb = g // GROUPS_PER_BLOCK
    gpb = GROUPS_PER_BLOCK
    dt = jnp.exp(log_dt.astype(F32))
    lam = lax.complex(jnp.minimum(lam_re.astype(F32), -1e-4), lam_im.astype(F32))
    lam_bar = jnp.exp(lam * dt[:, None])
    b_bar = ((lam_bar - 1.0) / lam)[..., None] * lax.complex(b_re.astype(F32), b_im.astype(F32))
    c_c = lax.complex(c_re.astype(F32), c_im.astype(F32))
    pows = [jnp.ones_like(lam_bar)]
    for _ in range(S5_SUB):
        pows.append(pows[-1] * lam_bar)
    pw = jnp.stack(pows)

    def block_diag(x):
        k = x.shape[-1]
        lanes = jnp.arange(gpb * k)
        rep = (lanes[None, :] % k == jnp.arange(k)[:, None]).astype(F32)
        wide = jnp.einsum('...k,kl->...l', x, rep, precision=lax.Precision.HIGHEST)
        own = (lanes[None, None, :] // k) == jnp.arange(gpb)[:, None, None]
        wide = jnp.where(own, wide, 0.0)
        return wide.reshape(x.shape[:-3] + (gpb * x.shape[-2], gpb * k))

    kd = jnp.real(jnp.einsum('gop,dgp,gpi->dgio', c_c, pw[:S5_SUB], b_bar))
    km = block_diag(kd.reshape(S5_SUB, n_gb, gpb, SSM_CG, SSM_CG))

    wj = pw[:S5_SUB][::-1][:, :, :, None] * b_bar[None]
    wj = jnp.transpose(wj.reshape(S5_SUB, n_gb, gpb, p, SSM_CG), (1, 0, 2, 4, 3))
    wcat = jnp.concatenate([block_diag(jnp.real(wj)), block_diag(jnp.imag(wj))], axis=-1)
    wcat = wcat.reshape(n_gb, S5_SUB * LANES, 2 * gpb * p).astype(BF16)

    cl = c_c[None] * pw[1:S5_SUB + 1][:, :, None, :]
    cl = jnp.transpose(cl, (0, 1, 3, 2)).reshape(S5_SUB, n_gb, gpb, p, SSM_CG)
    vm = jnp.concatenate([block_diag(jnp.real(cl)), block_diag(-jnp.imag(cl))], axis=-2)

    lam_s = pw[S5_SUB].reshape(n_gb, 1, gpb * p)
    dsk = d_skip.astype(F32).reshape(n_gb, 1, LANES)

    km, vm = km.astype(BF16), vm.astype(BF16)
    ycat = []
    zero = jnp.zeros_like(km[0])
    for l in range(0, S5_SUB, 2):
        rows = [jnp.concatenate([vm[l], vm[l + 1]], axis=-1)]
        for m in range(l + 1, -1, -1):
            left = km[l - m] if m <= l else zero
            rows.append(jnp.concatenate([left, km[l + 1 - m]], axis=-1))
        ycat.append(jnp.concatenate(rows, axis=-2))
    return (wcat, tuple(ycat), jnp.real(lam_s), jnp.imag(lam_s), dsk)


def _glu_body(y_ref, w_ref, b_ref, o_ref):
    y = y_ref[...]
    gate = _dot(y.astype(BF16), w_ref[...]) + b_ref[...]
    o_ref[...] = (y * _sigmoid(gate)).astype(BF16)


def _glu(y2, w_bf, b_row, tm):
    m, d = y2.shape
    return pl.pallas_call(
        _glu_body,
        grid=(m // tm,),
        in_specs=[pl.BlockSpec((tm, d), lambda i: (i, 0)),
                  pl.BlockSpec((d, d), lambda i: (0, 0)),
                  pl.BlockSpec((1, d), lambda i: (0, 0))],
        out_specs=pl.BlockSpec((tm, d), lambda i: (i, 0)),
        out_shape=jax.ShapeDtypeStruct((m, d), BF16),
        compiler_params=_cparams(("parallel",)),
        name="s5_glu",
    )(y2, w_bf, b_row)


def _layernorm(v, g, b):
    mu = jnp.mean(v, axis=-1, keepdims=True)
    var = jnp.mean(jnp.square(v - mu), axis=-1, keepdims=True)
    return (v - mu) * lax.rsqrt(var + LN_EPS) * g + b


def _mix_body(og_ref, os_ref, wa_ref, wb_ref, x_ref, g_ref, b_ref, wr_ref, br_ref,
              h_ref, hb_ref, route_ref, cnt_ref, *, alpha):
    @pl.when(pl.program_id(0) == 0)
    def _():
        cnt_ref[...] = jnp.zeros_like(cnt_ref)

    mix = _dot(og_ref[...], wa_ref[...]) + _dot(os_ref[...], wb_ref[...])
    h = _layernorm(alpha * x_ref[...] + mix, g_ref[...], b_ref[...])
    h_ref[...] = h
    tm = h.shape[0]
    _pack_rows(h, hb_ref, tm)

    logits = _dot_x3(h, wr_ref[...]) + br_ref[...]
    lane = lax.broadcasted_iota(jnp.int32, (tm, LANES), 1)
    work = jnp.where(lane < N_EXPERTS, logits, -jnp.inf)
    vals, idxs = [], []
    for _ in range(TOP_K):
        mx = jnp.max(work, axis=-1, keepdims=True)
        ix = jnp.min(jnp.where(work == mx, lane, LANES), axis=-1, keepdims=True)
        vals.append(mx)
        idxs.append(ix)
        work = jnp.where(lane == ix, -jnp.inf, work)
    exps = [jnp.exp(v - vals[0]) for v in vals]
    denom = exps[0]
    for e in exps[1:]:
        denom = denom + e
    chosen = jnp.zeros((tm, LANES), F32)
    for k in range(TOP_K):
        chosen = jnp.where(lane == idxs[k], 1.0, chosen)
    rr = lax.broadcasted_iota(jnp.int32, (tm, tm), 0)
    cc = lax.broadcasted_iota(jnp.int32, (tm, tm), 1)
    lower = jnp.where(rr > cc, 1.0, 0.0).astype(BF16)
    before = cnt_ref[...] + _dot(lower, chosen.astype(BF16))
    cnt_ref[...] = cnt_ref[...] + jnp.sum(chosen, axis=0, keepdims=True)

    route = jnp.zeros((tm, LANES), F32)
    for k in range(TOP_K):
        rank = jnp.sum(jnp.where(lane == idxs[k], before, 0.0), axis=-1, keepdims=True)
        route = jnp.where(lane == k, idxs[k].astype(F32), route)
        route = jnp.where(lane == TOP_K + k, exps[k] / denom, route)
        route = jnp.where(lane == 2 * TOP_K + k, rank, route)
    route_ref[...] = route


def _mix_ln_route(og, osm, wa, wb, x2, g_row, b_row, wr, br, alpha, tm):
    m, d = x2.shape
    dh = og.shape[1]
    body = functools.partial(_mix_body, alpha=alpha)
    row = lambda i: (i, 0)
    fix = lambda i: (0, 0)
    return pl.pallas_call(
        body,
        grid=(m // tm,),
        in_specs=[pl.BlockSpec((tm, dh), row), pl.BlockSpec((tm, dh), row),
                  pl.BlockSpec((dh, d), fix), pl.BlockSpec((dh, d), lambda i: (1, 0)),
                  pl.BlockSpec((tm, d), row),
                  pl.BlockSpec((1, d), fix), pl.BlockSpec((1, d), fix),
                  pl.BlockSpec((d, LANES), fix), pl.BlockSpec((1, LANES), fix)],
        out_specs=[pl.BlockSpec((tm, d), row), pl.BlockSpec((tm * ROW_SUB, LANES), row),
                   pl.BlockSpec((tm, LANES), row), pl.BlockSpec((1, LANES), fix)],
        out_shape=[jax.ShapeDtypeStruct((m, d), F32),
                   jax.ShapeDtypeStruct((m * ROW_SUB, LANES), jnp.uint32),
                   jax.ShapeDtypeStruct((m, LANES), F32),
                   jax.ShapeDtypeStruct((1, LANES), F32)],
        compiler_params=_cparams(("arbitrary",)),
        name="mix_ln_route",
    )(og, osm, wa, wb, x2, g_row, b_row, wr, br)


def _dispatch_body(pos_ref, src_ref, init_ref, dst_ref, sem, *, tt):
    del init_ref

    def issue(t, carry):
        srow = pl.multiple_of(t * ROW_SUB, ROW_SUB)
        for k in range(TOP_K):
            drow = pl.multiple_of(pos_ref[t * TOP_K + k] * ROW_SUB, ROW_SUB)
            pltpu.make_async_copy(src_ref.at[pl.ds(srow, ROW_SUB)], dst_ref.at[pl.ds(drow, ROW_SUB)],
                                  sem).start(priority=k % 2)
        return carry

    lax.fori_loop(0, tt, issue, 0, unroll=4)
    n = tt * ROW_SUB
    for _ in range(TOP_K):
        pltpu.make_async_copy(src_ref.at[pl.ds(0, n)], dst_ref.at[pl.ds(0, n)], sem).wait()


def _dispatch_rows(pos_flat, hbp, init, tt):
    m = hbp.shape[0] // ROW_SUB
    body = functools.partial(_dispatch_body, tt=tt)
    return pl.pallas_call(
        body,
        grid=(m // tt,),
        in_specs=[pl.BlockSpec((tt * TOP_K,), lambda i: (i,), memory_space=pltpu.SMEM),
                  pl.BlockSpec((tt * ROW_SUB, LANES), lambda i: (i, 0)), pl.BlockSpec(memory_space=pl.ANY)],
        out_specs=pl.BlockSpec(memory_space=pl.ANY),
        out_shape=jax.ShapeDtypeStruct(init.shape, jnp.uint32),
        scratch_shapes=[pltpu.SemaphoreType.DMA(())],
        input_output_aliases={2: 0},
        compiler_params=_cparams(("arbitrary",)),
        name="moe_dispatch",
    )(pos_flat, hbp, init)


MOE_COL = 256
MOE_GROUP = 2


def _moe_body(te_ref, nu_ref, x_ref, wg_ref, bg_ref, wu_ref, bu_ref, wd_ref, bd_ref,
              o_ref, xb_ref, acc_ref, *, n_f, tm, tf):
    g = pl.program_id(2)
    f = pl.program_id(1)
    used = pl.program_id(0) * MOE_GROUP + g < nu_ref[0]
    half = xb_ref.shape[2] // 2
    xb_ref = xb_ref.at[g]
    acc_ref = acc_ref.at[g]

    @pl.when(used & (f == 0))
    def _():
        lo, hi = _unpack_rows(x_ref, 0, tm)
        for r in range(ROW_SUB):
            xb_ref[:, r * LANES:(r + 1) * LANES] = lo[r].astype(BF16)
            xb_ref[:, half + r * LANES:half + (r + 1) * LANES] = hi[r].astype(BF16)

    @pl.when(used)
    def _():
        xb = xb_ref[...]
        part = None
        for c in range(tf // MOE_COL):
            cs = slice(c * MOE_COL, (c + 1) * MOE_COL)
            hg = jnp.minimum(_dot(xb, wg_ref[0, :, cs].astype(BF16)) + bg_ref[0, :, cs], SWIGLU_LIMIT)
            hu = jnp.clip(_dot(xb, wu_ref[0, :, cs].astype(BF16)) + bu_ref[0, :, cs], -SWIGLU_LIMIT, SWIGLU_LIMIT)
            hh = ((hu + 1.0) * (hg * _sigmoid(SWIGLU_ALPHA * hg))).astype(BF16)
            p = _dot(hh, wd_ref[0, cs, :].astype(BF16))
            part = p if part is None else part + p

        @pl.when(f == 0)
        def _():
            acc_ref[...] = part

        @pl.when(f != 0)
        def _():
            acc_ref[...] += part

        @pl.when(f == n_f - 1)
        def _():
            _pack_rows(acc_ref[...] + bd_ref[0], o_ref, tm)

    @pl.when(jnp.logical_not(used) & (f == n_f - 1))
    def _():
        o_ref[...] = jnp.zeros_like(o_ref)


def _moe_ffn(tile_expert, n_used, x_rows, w_gate, b_gate, w_up, b_up, w_down, b_down, tm, tf):
    n_e, d, d_ff = w_gate.shape
    n_tiles = x_rows.shape[0] // (tm * ROW_SUB)
    n_f = d_ff // tf
    grp = MOE_GROUP
    assert n_tiles % grp == 0
    body = functools.partial(_moe_body, n_f=n_f, tm=tm, tf=tf)

    def tile(p, g):
        return p * grp + g

    def fcol(p, f, nu):
        return jnp.where(p * grp < nu[0], f, n_f - 1)

    def xrow(p, f, g):
        return jnp.where(f == 0, tile(p, g), tile(p, grp - 1))

    def orow(p, f, g):
        return jnp.where(f == n_f - 1, tile(p, g), tile(p, 0))

    grid_spec = pltpu.PrefetchScalarGridSpec(
        num_scalar_prefetch=2,
        grid=(n_tiles // grp, n_f, grp),
        in_specs=[pl.BlockSpec((tm * ROW_SUB, LANES), lambda p, f, g, te, nu: (xrow(p, f, g), 0)),
                  pl.BlockSpec((1, d, tf), lambda p, f, g, te, nu: (te[tile(p, g)], 0, fcol(p, f, nu))),
                  pl.BlockSpec((1, 1, tf), lambda p, f, g, te, nu: (te[tile(p, g)], 0, fcol(p, f, nu))),
                  pl.BlockSpec((1, d, tf), lambda p, f, g, te, nu: (te[tile(p, g)], 0, fcol(p, f, nu))),
                  pl.BlockSpec((1, 1, tf), lambda p, f, g, te, nu: (te[tile(p, g)], 0, fcol(p, f, nu))),
                  pl.BlockSpec((1, tf, d), lambda p, f, g, te, nu: (te[tile(p, g)], fcol(p, f, nu), 0)),
                  pl.BlockSpec((1, 1, d), lambda p, f, g, te, nu: (te[tile(p, g)], 0, 0))],
        out_specs=pl.BlockSpec((tm * ROW_SUB, LANES), lambda p, f, g, te, nu: (orow(p, f, g), 0)),
        scratch_shapes=[pltpu.VMEM((grp, tm, d), BF16), pltpu.VMEM((grp, tm, d), F32)],
    )
    return pl.pallas_call(
        body,
        grid_spec=grid_spec,
        out_shape=jax.ShapeDtypeStruct(x_rows.shape, jnp.uint32),
        compiler_params=_cparams(("arbitrary", "arbitrary", "arbitrary")),
        name="moe_ffn",
    )(tile_expert, n_used, x_rows, w_gate, b_gate.reshape(n_e, 1, d_ff),
      w_up, b_up.reshape(n_e, 1, d_ff), w_down, b_down.reshape(n_e, 1, d))


def _final_body(pos_cur_ref, pos_nxt_ref, h_ref, route_ref, g_ref, b_ref, src_ref,
                o_ref, buf_ref, sem, *, alpha, tt, n_steps):
    i = pl.program_id(0)
    slot_rows = tt * TOP_K * ROW_SUB

    def issue(pref, slot):
        def one(t, carry):
            for k in range(TOP_K):
                srow = pl.multiple_of(pref[t * TOP_K + k] * ROW_SUB, ROW_SUB)
                drow = pl.multiple_of(slot * slot_rows + (k * tt + t) * ROW_SUB, ROW_SUB)
                pltpu.make_async_copy(src_ref.at[pl.ds(srow, ROW_SUB)], buf_ref.at[pl.ds(drow, ROW_SUB)],
                                      sem.at[slot]).start(priority=k % 2)
            return carry
        lax.fori_loop(0, tt, one, 0, unroll=4)

    @pl.when(i == 0)
    def _():
        issue(pos_cur_ref, 0)

    @pl.when(i + 1 < n_steps)
    def _():
        issue(pos_nxt_ref, (i + 1) % 2)

    slot = i % 2
    base = pl.multiple_of(slot * slot_rows, ROW_SUB)
    pltpu.make_async_copy(src_ref.at[pl.ds(0, slot_rows)], buf_ref.at[pl.ds(base, slot_rows)], sem.at[slot]).wait()

    ff_lo = [None] * ROW_SUB
    ff_hi = [None] * ROW_SUB
    for k in range(TOP_K):
        gate = route_ref[:, TOP_K + k:TOP_K + k + 1]
        lo, hi = _unpack_rows(buf_ref, base + k * tt * ROW_SUB, tt)
        for s in range(ROW_SUB):
            ff_lo[s] = lo[s] * gate if k == 0 else ff_lo[s] + lo[s] * gate
            ff_hi[s] = hi[s] * gate if k == 0 else ff_hi[s] + hi[s] * gate
    ff = jnp.concatenate(ff_lo + ff_hi, axis=1)
    o_ref[...] = _layernorm(alpha * h_ref[...] + ff, g_ref[...], b_ref[...])


def _combine_ln(pos_flat, h, route, g_row, b_row, outs, alpha, tt):
    m, d = h.shape
    n_steps = m // tt
    body = functools.partial(_final_body, alpha=alpha, tt=tt, n_steps=n_steps)
    row = lambda i: (i, 0)
    fix = lambda i: (0, 0)
    return pl.pallas_call(
        body,
        grid=(n_steps,),
        in_specs=[pl.BlockSpec((tt * TOP_K,), lambda i: (i,), memory_space=pltpu.SMEM),
                  pl.BlockSpec((tt * TOP_K,), lambda i: (jnp.minimum(i + 1, n_steps - 1),),
                               memory_space=pltpu.SMEM),
                  pl.BlockSpec((tt, d), row), pl.BlockSpec((tt, LANES), row),
                  pl.BlockSpec((1, d), fix), pl.BlockSpec((1, d), fix),
                  pl.BlockSpec(memory_space=pl.ANY)],
        out_specs=pl.BlockSpec((tt, d), row),
        out_shape=jax.ShapeDtypeStruct((m, d), F32),
        scratch_shapes=[pltpu.VMEM((2 * tt * TOP_K * ROW_SUB, LANES), jnp.uint32),
                        pltpu.SemaphoreType.DMA((2,))],
        compiler_params=_cparams(("arbitrary",)),
        name="combine_ln",
    )(pos_flat, pos_flat, h, route, g_row, b_row, outs)


def _row_tile(m, pref):
    t = min(m, pref)
    while m % t:
        t //= 2
    return t


def _route_tables(routes, counts_rows, tm):
    counts = [c[0, :N_EXPERTS].astype(jnp.int32) for c in counts_rows]
    total = functools.reduce(lambda a, b: a + b, counts)
    padded = (total + tm - 1) // tm * tm
    pends = jnp.cumsum(padded)
    pstarts = pends - padded
    pos, before = [], jnp.zeros_like(total)
    for route, cnt in zip(routes, counts):
        e_tok = route[:, :TOP_K].astype(jnp.int32)
        rank = route[:, 2 * TOP_K:3 * TOP_K].astype(jnp.int32)
        pos.append(((pstarts + before)[e_tok] + rank).reshape(-1))
        before = before + cnt
    n_assign = sum(r.shape[0] for r in routes) * TOP_K
    n_tiles = -(-n_assign // tm) + N_EXPERTS
    n_tiles = -(-n_tiles // MOE_GROUP) * MOE_GROUP
    tile_start = jnp.arange(n_tiles, dtype=jnp.int32) * tm
    tile_expert = jnp.minimum(jnp.sum((pends[None, :] <= tile_start[:, None]).astype(jnp.int32), axis=1),
                              N_EXPERTS - 1)
    n_used = (pends[-1] // tm).astype(jnp.int32).reshape(1)
    last_used = jnp.maximum(n_used[0] - 1, 0)
    tile_expert = jnp.where(tile_start // tm < n_used[0], tile_expert, tile_expert[last_used])
    return pos, tile_expert, n_used


def _moe_joint(parts, alpha, p):
    tm = p['moe_tm']
    pos, tile_expert, n_used = _route_tables([q['route'] for q in parts], [q['counts'] for q in parts], tm)
    cap = tile_expert.shape[0] * tm
    x_rows = jnp.zeros((cap * ROW_SUB, LANES), jnp.uint32)
    for q, pq in zip(parts, pos):
        x_rows = _dispatch_rows(pq, q['hb'], x_rows, _row_tile(q['h'].shape[0], 512))
    outs = _moe_ffn(tile_expert, n_used, x_rows, p['w_gate'], p['b_gate'],
                    p['w_up'], p['b_up'], p['w_down'], p['b_down'], tm, p['moe_tf'])
    return [_combine_ln(pq, q['h'], q['route'], p['ln2_g'], p['ln2_b'], outs, alpha, _row_tile(q['h'].shape[0], 128))
            for q, pq in zip(parts, pos)]


def _layer_pre(x, conv_hist, s_gdn, h_re, h_im, chunk, alpha, p):
    b, t, d = x.shape
    m = b * t
    d_gdn = GDN_HEADS * HEAD_DIM
    x2 = x.reshape(m, d)

    proj = _in_proj(x2, p['w_in'], _row_tile(m, 1024), p['proj_tn'])
    proj3 = proj.reshape(b, t, proj.shape[1])
    conv_new = proj3[:, t - (CONV_W - 1):, :3 * d_gdn]

    q, k, v, cols, gct = _gdn_prep(proj3, conv_hist, p['w_conv'], p['alog_row'], p['dtb_row'],
                                   chunk, p['ab_block'])
    o_gdn, s_new = _gdn(q, k, v, cols, gct, proj3, p['wn_row'], s_gdn, chunk, p['z_block'])

    n_gb = p['s5_mats'][0].shape[0]
    half = GROUPS_PER_BLOCK * SSM_P
    yg, hre_new, him_new = _s5(proj3, p['s5_mats'], h_re.reshape(b, n_gb, 1, half),
                               h_im.reshape(b, n_gb, 1, half), p['u_block0'])
    o_ssm = _glu(yg.reshape(m, yg.shape[2]), p['w_glu'], p['b_glu_row'], _row_tile(m, 512))

    h, hb, route, counts_row = _mix_ln_route(o_gdn.reshape(m, d_gdn), o_ssm, p['w_out'], p['w_out'], x2,
                                             p['ln1_g'], p['ln1_b'], p['w_router'], p['b_router'], alpha,
                                             _row_tile(m, 256))

    g_all = h_re.shape[1]
    return {'h': h, 'hb': hb, 'route': route, 'counts': counts_row, 'shape': (b, t, d),
            'state': (conv_new, s_new, hre_new.reshape(b, g_all, SSM_P), him_new.reshape(b, g_all, SSM_P))}


def _pad_lanes(v, fill=0.0):
    return jnp.pad(v.astype(F32), (0, LANES - v.shape[0]), constant_values=fill).reshape(1, LANES)


def _layer_params(l, w_in, w_conv, a_log, dt_bias, w_onorm, lam_re, lam_im, log_dt, b_re, b_im, c_re, c_im,
                  d_skip, w_glu, b_glu, w_out, ln1_g, ln1_b, w_router, b_router, w_gate, b_gate,
                  w_up, b_up, w_down, b_down, ln2_g, ln2_b):
    d_model = w_in.shape[1]
    d_gdn = GDN_HEADS * HEAD_DIM
    d_qkvz = 4 * d_gdn
    d_ssm = d_model - d_gdn
    wi = w_in[l]
    proj_tn = 1792
    n_cols = d_qkvz + d_ssm + LANES
    n_pad = -(-n_cols // proj_tn) * proj_tn
    w_in_r = jnp.concatenate([wi[:, :d_qkvz], wi[:, d_qkvz + 2 * GDN_HEADS:],
                              wi[:, d_qkvz:d_qkvz + 2 * GDN_HEADS],
                              jnp.zeros((d_model, n_pad - d_qkvz - d_ssm - 2 * GDN_HEADS), wi.dtype)], axis=1)
    wo = w_out[l].astype(BF16)
    wr = jnp.pad(w_router[l].astype(F32), ((0, 0), (0, LANES - N_EXPERTS)))
    return {
        'w_in': w_in_r.astype(BF16), 'proj_tn': proj_tn,
        'z_block': 3, 'u_block0': (d_qkvz) // LANES, 'ab_block': (d_qkvz + d_ssm) // LANES,
        'w_conv': w_conv[l].astype(F32),
        'alog_row': _pad_lanes(a_log[l]), 'dtb_row': _pad_lanes(dt_bias[l]),
        'wn_row': w_onorm[l].astype(F32).reshape(1, HEAD_DIM),
        's5_mats': _s5_matrices(lam_re[l], lam_im[l], log_dt[l], b_re[l], b_im[l], c_re[l], c_im[l], d_skip[l]),
        'w_glu': w_glu[l].astype(BF16), 'b_glu_row': b_glu[l].astype(F32).reshape(1, d_ssm),
        'w_out': wo,
        'ln1_g': ln1_g[l].astype(F32).reshape(1, d_model), 'ln1_b': ln1_b[l].astype(F32).reshape(1, d_model),
        'w_router': wr, 'b_router': _pad_lanes(b_router[l]),
        'w_gate': w_gate[l], 'b_gate': b_gate[l], 'w_up': w_up[l], 'b_up': b_up[l],
        'w_down': w_down[l], 'b_down': b_down[l],
        'ln2_g': ln2_g[l].astype(F32).reshape(1, d_model), 'ln2_b': ln2_b[l].astype(F32).reshape(1, d_model),
        'moe_tm': 576, 'moe_tf': 512,
    }


def kernel(x_prompt, x_sample, state_conv, state_gdn, state_ssm_re, state_ssm_im, w_in, w_conv, a_log, dt_bias, w_onorm, lam_re, lam_im, log_dt, b_re, b_im, c_re, c_im, d_skip, w_glu, b_glu, w_out, ln1_g, ln1_b, w_router, b_router, w_gate, b_gate, w_up, b_up, w_down, b_down, ln2_g, ln2_b):
    depth = w_in.shape[0]
    alpha = (2.0 * depth) ** 0.25
    bp, seq, _ = x_prompt.shape
    chunk_p = 64
    d_qkv = state_conv.shape[-1]
    n_groups, n_p = state_ssm_re.shape[-2:]
    yp, ys = x_prompt, x_sample
    outs_p = [[], [], [], []]
    outs_s = [[], [], [], []]
    for l in range(depth):
        p = _layer_params(l, w_in, w_conv, a_log, dt_bias, w_onorm, lam_re, lam_im, log_dt, b_re, b_im,
                          c_re, c_im, d_skip, w_glu, b_glu, w_out, ln1_g, ln1_b, w_router, b_router,
                          w_gate, b_gate, w_up, b_up, w_down, b_down, ln2_g, ln2_b)
        part_p = _layer_pre(
            yp, jnp.zeros((bp, CONV_W - 1, d_qkv), F32),
            jnp.zeros((bp, GDN_HEADS, HEAD_DIM, HEAD_DIM), F32),
            jnp.zeros((bp, n_groups, n_p), F32), jnp.zeros((bp, n_groups, n_p), F32),
            chunk_p, alpha, p)
        part_s = _layer_pre(
            ys, state_conv[l].astype(F32), state_gdn[l].astype(F32),
            state_ssm_re[l].astype(F32), state_ssm_im[l].astype(F32),
            ys.shape[1], alpha, p)
        yp2, ys2 = _moe_joint([part_p, part_s], alpha, p)
        yp, ys = yp2.reshape(part_p['shape']), ys2.reshape(part_s['shape'])
        for acc, val in zip(outs_p, part_p['state']):
            acc.append(val)
        for acc, val in zip(outs_s, part_s['state']):
            acc.append(val)
    return (yp, ys, *[jnp.stack(a) for a in outs_p], *[jnp.stack(a) for a in outs_s])
```

```python
import functools
import math

import jax
import jax.numpy as jnp
from jax import lax
from jax.experimental import pallas as pl
from jax.experimental.pallas import tpu as pltpu

F32 = jnp.float32
BF16 = jnp.bfloat16

GDN_HEADS = 8
HEAD_DIM = 128
CONV_W = 4
SSM_CG = 16
SSM_P = 64
N_EXPERTS = 32
TOP_K = 4
SWIGLU_ALPHA = 1.702
SWIGLU_LIMIT = 7.0
LN_EPS = 1e-5
RMS_EPS = 1e-6
L2_EPS = 1e-6

LANES = 128
SUBLANES = 8
S5_SUB = 8
S5_TILE = 2048
GDN_TILE = 128
GROUPS_PER_BLOCK = LANES // SSM_CG
VMEM_LIMIT = 56 * 1024 * 1024

NT_DIMS = (((1,), (1,)), ((), ()))
TN_DIMS = (((0,), (0,)), ((), ()))


def _dot(a, b, dims=(((1,), (0,)), ((), ()))):
    return lax.dot_general(a, b, dims, preferred_element_type=F32)


def _split(a):
    hi = a.astype(BF16)
    lo = (a - hi.astype(F32)).astype(BF16)
    return hi, lo


def _dot_x3(a, b, dims=(((1,), (0,)), ((), ()))):
    ah, al = _split(a)
    bh, bl = _split(b)
    return _dot(ah, bh, dims) + (_dot(ah, bl, dims) + _dot(al, bh, dims))


def _dot_bf(a, b, dims=(((1,), (0,)), ((), ()))):
    return _dot(a.astype(BF16), b.astype(BF16), dims)


_gdn_mm = _dot_bf


def _sigmoid(x):
    return 1.0 / (1.0 + jnp.exp(-x))


ROW_SUB = SUBLANES


def _pack_rows(x, ref, tm):
    half = x.shape[1] // 2
    for s in range(ROW_SUB):
        lo = x[:, s * LANES:(s + 1) * LANES].astype(BF16).astype(F32)
        hi = x[:, half + s * LANES:half + (s + 1) * LANES].astype(BF16).astype(F32)
        word = (lax.bitcast_convert_type(lo, jnp.uint32) >> 16) | lax.bitcast_convert_type(hi, jnp.uint32)
        ref[pl.ds(s, tm, stride=ROW_SUB), :] = word


def _unpack_rows(ref, base, tm):
    lo, hi = [], []
    for s in range(ROW_SUB):
        word = ref[pl.ds(base + s, tm, stride=ROW_SUB), :]
        lo.append(lax.bitcast_convert_type(word << 16, F32))
        hi.append(lax.bitcast_convert_type(word & jnp.uint32(0xFFFF0000), F32))
    return lo, hi


def _cparams(sem):
    return pltpu.CompilerParams(dimension_semantics=sem, vmem_limit_bytes=VMEM_LIMIT)


def _proj_body(x_ref, w_ref, o_ref, xb_ref):
    @pl.when(pl.program_id(1) == 0)
    def _():
        xb_ref[...] = x_ref[...].astype(BF16)

    o_ref[...] = _dot(xb_ref[...], w_ref[...])


def _in_proj(x2, w_bf, tm, tn):
    m, k = x2.shape
    n = w_bf.shape[1]
    return pl.pallas_call(
        _proj_body,
        grid=(m // tm, n // tn),
        in_specs=[pl.BlockSpec((tm, k), lambda i, j: (i, 0)),
                  pl.BlockSpec((k, tn), lambda i, j: (0, j))],
        out_specs=pl.BlockSpec((tm, tn), lambda i, j: (i, j)),
        out_shape=jax.ShapeDtypeStruct((m, n), F32),
        scratch_shapes=[pltpu.VMEM((tm, k), BF16)],
        compiler_params=_cparams(("parallel", "arbitrary")),
        name="in_proj",
    )(x2, w_bf)


def _gdn_prep_body(qkv_ref, ab_ref, hist_ref, wc_ref, alog_ref, dtb_ref,
                   q_ref, k_ref, v_ref, cols_ref, gct_ref, xbuf_ref, *, tt, chunk):
    d_gdn = GDN_HEADS * HEAD_DIM
    halo = SUBLANES

    @pl.when(pl.program_id(1) == 0)
    def _():
        xbuf_ref[0:halo, :] = jnp.zeros((halo, 3 * d_gdn), F32)
        xbuf_ref[halo - (CONV_W - 1):halo, :] = hist_ref[0]

    xbuf_ref[halo:halo + tt, :] = qkv_ref[0]

    for part, out_ref in enumerate((q_ref, k_ref, v_ref)):
        c0 = part * d_gdn
        y = None
        for j in range(CONV_W):
            r0 = halo - (CONV_W - 1) + j
            term = xbuf_ref[r0:r0 + tt, c0:c0 + d_gdn] * wc_ref[j:j + 1, c0:c0 + d_gdn]
            y = term if y is None else y + term
        s = y * _sigmoid(y)
        if part == 2:
            out_ref[0] = s
        else:
            for h in range(GDN_HEADS):
                sh = s[:, h * HEAD_DIM:(h + 1) * HEAD_DIM]
                nrm = sh * lax.rsqrt(jnp.sum(sh * sh, axis=-1, keepdims=True) + L2_EPS)
                if part == 0:
                    nrm = nrm * (HEAD_DIM ** -0.5)
                out_ref[0, :, h * HEAD_DIM:(h + 1) * HEAD_DIM] = nrm

    xbuf_ref[0:halo, :] = xbuf_ref[tt:tt + halo, :]

    ab = ab_ref[0]
    lane = lax.broadcasted_iota(jnp.int32, (tt, LANES), 1)
    is_a = lane < GDN_HEADS
    z = ab + dtb_ref[...]
    softplus = jnp.maximum(z, 0.0) + jnp.log1p(jnp.exp(-jnp.abs(z)))
    g = jnp.where(is_a, -jnp.exp(alog_ref[...]) * softplus, 0.0)
    beta = _sigmoid(ab)

    shift = int(math.log2(chunk))
    r = lax.broadcasted_iota(jnp.int32, (tt, tt), 0)
    c = lax.broadcasted_iota(jnp.int32, (tt, tt), 1)
    same = (r >> shift) == (c >> shift)
    m_incl = jnp.where(same & (r >= c), 1.0, 0.0).astype(BF16)
    m_all = jnp.where(same, 1.0, 0.0).astype(BF16)
    gc = _dot_exact_lhs_rhs(m_incl, g)
    glast = _dot_exact_lhs_rhs(m_all, g)
    eg = jnp.exp(gc)
    egl = jnp.exp(glast - gc)
    egt = jnp.exp(glast)
    zero = jnp.zeros_like(gc)
    cols = (jnp.where(is_a, gc, zero)
            + jnp.where((lane >= 8) & (lane < 16), beta, zero)
            + pltpu.roll(jnp.where(is_a, eg, zero), 16, 1)
            + pltpu.roll(jnp.where(is_a, egl, zero), 24, 1)
            + pltpu.roll(jnp.where(is_a, egt, zero), 32, 1))
    cols_ref[0] = cols

    er = lax.broadcasted_iota(jnp.int32, (SUBLANES, LANES), 0)
    ec = lax.broadcasted_iota(jnp.int32, (SUBLANES, LANES), 1)
    sel = jnp.where(er == ec, 1.0, 0.0).astype(BF16)
    gct_ref[0] = _dot_exact_rhs(sel, jnp.where(is_a, gc, zero), NT_DIMS)


def _three_pieces(a):
    a0 = a.astype(BF16)
    r1 = a - a0.astype(F32)
    a1 = r1.astype(BF16)
    a2 = (r1 - a1.astype(F32)).astype(BF16)
    return a0, a1, a2


def _dot_exact_lhs_rhs(mask_bf, a):
    a0, a1, a2 = _three_pieces(a)
    return _dot(mask_bf, a0) + (_dot(mask_bf, a1) + _dot(mask_bf, a2))


def _dot_exact_rhs(mask_bf, a, dims):
    a0, a1, a2 = _three_pieces(a)
    return _dot(mask_bf, a0, dims) + (_dot(mask_bf, a1, dims) + _dot(mask_bf, a2, dims))


def _gdn_prep(proj3, hist, w_conv, alog_p, dtb_p, chunk, ab_block):
    b, t, _ = proj3.shape
    d_gdn = GDN_HEADS * HEAD_DIM
    tt = min(t, GDN_TILE)
    body = functools.partial(_gdn_prep_body, tt=tt, chunk=chunk)
    big = jax.ShapeDtypeStruct((b, t, d_gdn), F32)
    return pl.pallas_call(
        body,
        grid=(b, t // tt),
        in_specs=[pl.BlockSpec((1, tt, 3 * d_gdn), lambda i, j: (i, j, 0)),
                  pl.BlockSpec((1, tt, LANES), lambda i, j: (i, j, ab_block)),
                  pl.BlockSpec((1, CONV_W - 1, 3 * d_gdn), lambda i, j: (i, 0, 0)),
                  pl.BlockSpec((CONV_W, 3 * d_gdn), lambda i, j: (0, 0)),
                  pl.BlockSpec((1, LANES), lambda i, j: (0, 0)),
                  pl.BlockSpec((1, LANES), lambda i, j: (0, 0))],
        out_specs=[pl.BlockSpec((1, tt, d_gdn), lambda i, j: (i, j, 0)),
                   pl.BlockSpec((1, tt, d_gdn), lambda i, j: (i, j, 0)),
                   pl.BlockSpec((1, tt, d_gdn), lambda i, j: (i, j, 0)),
                   pl.BlockSpec((1, tt, LANES), lambda i, j: (i, j, 0)),
                   pl.BlockSpec((1, SUBLANES, tt), lambda i, j: (i, 0, j))],
        out_shape=[big, big, big,
                   jax.ShapeDtypeStruct((b, t, LANES), F32),
                   jax.ShapeDtypeStruct((b, SUBLANES, t), F32)],
        scratch_shapes=[pltpu.VMEM((tt + SUBLANES, 3 * d_gdn), F32)],
        compiler_params=_cparams(("parallel", "arbitrary")),
        name="gdn_prep",
    )(proj3, proj3, hist, w_conv, alog_p, dtb_p)


def _unit_lower_inverses(mats, r, c, chunk):
    base = 16
    eye = jnp.where(r == c, 1.0, 0.0)

    def blk(bs):
        s = int(math.log2(bs))
        return (r >> s) == (c >> s)

    d1 = [jnp.where(blk(base), a, 0.0) for a in mats]
    d2 = [_gdn_mm(x, x) for x in d1]
    d4 = [_gdn_mm(x, x) for x in d2]
    d8 = [_gdn_mm(x, x) for x in d4]
    t = [eye - x for x in d1]
    t = [x + _gdn_mm(x, y) for x, y in zip(t, d2)]
    t = [x + _gdn_mm(x, y) for x, y in zip(t, d4)]
    t = [x + _gdn_mm(x, y) for x, y in zip(t, d8)]
    bs = base
    while bs < chunk:
        off_mask = blk(2 * bs) & jnp.logical_not(blk(bs))
        inner = [_gdn_mm(jnp.where(off_mask, a, 0.0), x) for a, x in zip(mats, t)]
        t = [x - _gdn_mm(x, y) for x, y in zip(t, inner)]
        bs *= 2
    return t


def _gdn_body(q_ref, k_ref, v_ref, cols_ref, gct_ref, z_ref, wn_ref, s0_ref,
              o_ref, s_ref, *, tb, chunk):
    @pl.when(pl.program_id(1) == 0)
    def _():
        s_ref[...] = s0_ref[...]

    shift = int(math.log2(chunk))
    r = lax.broadcasted_iota(jnp.int32, (tb, tb), 0)
    c = lax.broadcasted_iota(jnp.int32, (tb, tb), 1)
    same = (r >> shift) == (c >> shift)
    incl = same & (r >= c)
    strict = same & (r > c)
    n_chunks = tb // chunk

    heads = range(GDN_HEADS)
    hsl = [slice(h * HEAD_DIM, (h + 1) * HEAD_DIM) for h in heads]
    q = [q_ref[0, :, s] for s in hsl]
    k = [k_ref[0, :, s] for s in hsl]
    beta = [cols_ref[0, :, 8 + h:9 + h] for h in heads]
    eg = [cols_ref[0, :, 16 + h:17 + h] for h in heads]
    egl = [cols_ref[0, :, 24 + h:25 + h] for h in heads]
    decay = [jnp.exp(jnp.where(incl, cols_ref[0, :, h:h + 1] - gct_ref[0, h:h + 1, :], -jnp.inf)) for h in heads]
    kb = [k[h] * beta[h] for h in heads]
    k_bf = [x.astype(BF16) for x in k]
    a = [jnp.where(strict, _dot(kb[h].astype(BF16), k_bf[h], NT_DIMS) * decay[h], 0.0) for h in heads]
    tinv = _unit_lower_inverses(a, r, c, chunk)
    sol = [_gdn_mm(tinv[h], jnp.concatenate([v_ref[0, :, hsl[h]] * beta[h], kb[h] * eg[h]], axis=1)) for h in heads]
    u = [x[:, :HEAD_DIM] for x in sol]
    w_bf = [x[:, HEAD_DIM:].astype(BF16) for x in sol]
    attn = [(_dot(q[h].astype(BF16), k_bf[h], NT_DIMS) * decay[h]).astype(BF16) for h in heads]
    q_dec = [(q[h] * eg[h]).astype(BF16) for h in heads]
    k_dec = [(k[h] * egl[h]).astype(BF16) for h in heads]

    s = [s_ref[0, h] for h in heads]
    v_new = [[] for _ in heads]
    o_state = [[] for _ in heads]
    for ci in range(n_chunks):
        rs = slice(ci * chunk, (ci + 1) * chunk)
        s_bf = [x.astype(BF16) for x in s]
        vn = [u[h][rs] - _dot(w_bf[h][rs], s_bf[h]) for h in heads]
        for h in heads:
            o_state[h].append(_dot(q_dec[h][rs], s_bf[h]))
            v_new[h].append(vn[h])
        s = [s[h] * cols_ref[0, ci * chunk:ci * chunk + 1, 32 + h:33 + h]
             + _dot(k_dec[h][rs], vn[h].astype(BF16), TN_DIMS) for h in heads]
    for h in heads:
        s_ref[0, h] = s[h]

    def cat(parts):
        return parts[0] if len(parts) == 1 else jnp.concatenate(parts, axis=0)

    for h in heads:
        o = cat(o_state[h]) + _dot(attn[h], cat(v_new[h]).astype(BF16))
        zh = z_ref[0, :, hsl[h]]
        o = (o * lax.rsqrt(jnp.mean(o * o, axis=-1, keepdims=True) + RMS_EPS) * wn_ref[...]
             * (zh * _sigmoid(zh)))
        o_ref[0, :, hsl[h]] = o.astype(BF16)


def _gdn(q, k, v, cols, gct, proj3, wn, s0, chunk, z_block):
    b, t, d_gdn = q.shape
    tb = min(t, GDN_TILE)
    body = functools.partial(_gdn_body, tb=tb, chunk=chunk)
    tile = lambda i, j: (i, j, 0)
    return pl.pallas_call(
        body,
        grid=(b, t // tb),
        in_specs=[pl.BlockSpec((1, tb, d_gdn), tile),
                  pl.BlockSpec((1, tb, d_gdn), tile),
                  pl.BlockSpec((1, tb, d_gdn), tile),
                  pl.BlockSpec((1, tb, LANES), tile),
                  pl.BlockSpec((1, SUBLANES, tb), lambda i, j: (i, 0, j)),
                  pl.BlockSpec((1, tb, d_gdn), lambda i, j: (i, j, z_block)),
                  pl.BlockSpec((1, HEAD_DIM), lambda i, j: (0, 0)),
                  pl.BlockSpec((1, GDN_HEADS, HEAD_DIM, HEAD_DIM), lambda i, j: (i, 0, 0, 0))],
        out_specs=[pl.BlockSpec((1, tb, d_gdn), tile),
                   pl.BlockSpec((1, GDN_HEADS, HEAD_DIM, HEAD_DIM), lambda i, j: (i, 0, 0, 0))],
        out_shape=[jax.ShapeDtypeStruct((b, t, d_gdn), BF16),
                   jax.ShapeDtypeStruct((b, GDN_HEADS, HEAD_DIM, HEAD_DIM), F32)],
        compiler_params=_cparams(("parallel", "arbitrary")),
        name="gdn_delta",
    )(q, k, v, cols, gct, proj3, wn, s0)


def _gelu_tanh(x):
    return 0.5 * x * (1.0 + jnp.tanh(math.sqrt(2.0 / math.pi) * (x + 0.044715 * (x * x * x))))


def _s5_body(u_ref, wm_ref, ym0_ref, ym1_ref, ym2_ref, ym3_ref, lre_ref, lim_ref, dsk_ref, h0re_ref, h0im_ref,
             y_ref, hre_ref, him_ref, xbuf_ref, hbuf_ref, *, n_sub, seg):
    half = GROUPS_PER_BLOCK * SSM_P

    if seg is None:
        @pl.when(pl.program_id(2) == 0)
        def _():
            hre_ref[...] = h0re_ref[...]
            him_ref[...] = h0im_ref[...]

    u_f = [u_ref[0, pl.ds(j, n_sub, stride=S5_SUB), :] for j in range(S5_SUB)]
    u_b = [x.astype(BF16) for x in u_f]

    xbuf_ref[...] = _dot(jnp.concatenate(u_b, axis=1), wm_ref[0])

    lre = lre_ref[0]
    lim = lim_ref[0]

    def advance(n, hre, him):
        hbuf_ref[pl.ds(n, 1), 0:half] = hre
        hbuf_ref[pl.ds(n, 1), half:2 * half] = him
        xr = xbuf_ref[pl.ds(n, 1), 0:half]
        xi = xbuf_ref[pl.ds(n, 1), half:2 * half]
        return (lre * hre - lim * him + xr, lre * him + lim * hre + xi)

    if seg is None:
        hre, him = lax.fori_loop(0, n_sub, lambda n, c: advance(n, *c), (hre_ref[0, 0], him_ref[0, 0]),
                                 unroll=8)
        hre_ref[0, 0] = hre
        him_ref[0, 0] = him
    else:
        def step(n, carry):
            q = n // seg
            first = n - q * seg == 0
            hre = jnp.where(first, h0re_ref[q, 0], carry[0])
            him = jnp.where(first, h0im_ref[q, 0], carry[1])
            hre, him = advance(n, hre, him)
            hre_ref[q, 0] = hre
            him_ref[q, 0] = him
            return hre, him

        lax.fori_loop(0, n_sub, step, (h0re_ref[0, 0], h0im_ref[0, 0]))

    h_b = hbuf_ref[...].astype(BF16)
    dsk = dsk_ref[0]
    for l, ym_ref in zip(range(0, S5_SUB, 2), (ym0_ref, ym1_ref, ym2_ref, ym3_ref)):
        lhs = jnp.concatenate([h_b] + [u_b[m] for m in range(l + 1, -1, -1)], axis=1)
        y2 = _dot(lhs, ym_ref[0])
        for o in range(2):
            y = y2[:, o * LANES:(o + 1) * LANES] + dsk * u_f[l + o]
            y_ref[0, pl.ds(l + o, n_sub, stride=S5_SUB), :] = _gelu_tanh(y)


def _s5(proj3, mats, h0re, h0im, u_block0):
    b, t, width = proj3.shape
    n_gb = mats[0].shape[0]
    if b > 1 and b * t <= S5_TILE:
        y, hre, him = _s5_call(proj3.reshape(1, b * t, width), mats, h0re, h0im, u_block0,
                               tt=b * t, seg=t // S5_SUB, state_rows=b)
        return y.reshape(b, t, n_gb * LANES), hre, him
    return _s5_call(proj3, mats, h0re, h0im, u_block0, tt=min(t, S5_TILE), seg=None, state_rows=1)


def _s5_call(proj3, mats, h0re, h0im, u_block0, tt, seg, state_rows):
    wcat, ycat, lre, lim, dsk = mats
    b, t, _ = proj3.shape
    n_gb = wcat.shape[0]
    n_sub = tt // S5_SUB
    half = GROUPS_PER_BLOCK * SSM_P
    body = functools.partial(_s5_body, n_sub=n_sub, seg=seg)
    state_spec = pl.BlockSpec((state_rows, 1, 1, half), lambda g, i, j: (i, g, 0, 0))
    par_spec = pl.BlockSpec((1, 1, half), lambda g, i, j: (g, 0, 0))
    mat_spec = lambda a: pl.BlockSpec((1,) + a.shape[1:], lambda g, i, j: (g, 0, 0))
    return pl.pallas_call(
        body,
        grid=(n_gb, b, t // tt),
        in_specs=[pl.BlockSpec((1, tt, LANES), lambda g, i, j: (i, j, u_block0 + g)),
                  mat_spec(wcat)] + [mat_spec(a) for a in ycat] + [
                  par_spec, par_spec,
                  pl.BlockSpec((1, 1, LANES), lambda g, i, j: (g, 0, 0)),
                  state_spec, state_spec],
        out_specs=[pl.BlockSpec((1, tt, LANES), lambda g, i, j: (i, j, g)),
                   state_spec, state_spec],
        out_shape=[jax.ShapeDtypeStruct((b, t, n_gb * LANES), F32),
                   jax.ShapeDtypeStruct(h0re.shape, F32),
                   jax.ShapeDtypeStruct(h0im.shape, F32)],
        scratch_shapes=[pltpu.VMEM((n_sub, 2 * half), F32),
                        pltpu.VMEM((n_sub, 2 * half), F32)],
        compiler_params=_cparams(("parallel", "parallel", "arbitrary")),
        name="s5_scan",
    )(proj3, wcat, *ycat, lre, lim, dsk, h0re, h0im)


def _s5_matrices(lam_re, lam_im, log_dt, b_re, b_im, c_re, c_im, d_skip):
    g, p = lam_re.shape
    n_gb = g // GROUPS_PER_BLOCK
    gpb = GROUPS_PER_BLOCK
    dt = jnp.exp(log_dt.astype(F32))
    lam = lax.complex(jnp.minimum(lam_re.astype(F32), -1e-4), lam_im.astype(F32))
    lam_bar = jnp.exp(lam * dt[:, None])
    b_bar = ((lam_bar - 1.0) / lam)[..., None] * lax.complex(b_re.astype(F32), b_im.astype(F32))
    c_c = lax.complex(c_re.astype(F32), c_im.astype(F32))
    pows = [jnp.ones_like(lam_bar)]
    for _ in range(S5_SUB):
        pows.append(pows[-1] * lam_bar)
    pw = jnp.stack(pows)

    def block_diag(x):
        k = x.shape[-1]
        lanes = jnp.arange(gpb * k)
        rep = (lanes[None, :] % k == jnp.arange(k)[:, None]).astype(F32)
        wide = jnp.einsum('...k,kl->...l', x, rep, precision=lax.Precision.HIGHEST)
        own = (lanes[None, None, :] // k) == jnp.arange(gpb)[:, None, None]
        wide = jnp.where(own, wide, 0.0)
        return wide.reshape(x.shape[:-3] + (gpb * x.shape[-2], gpb * k))

    kd = jnp.real(jnp.einsum('gop,dgp,gpi->dgio', c_c, pw[:S5_SUB], b_bar))
    km = block_diag(kd.reshape(S5_SUB, n_gb, gpb, SSM_CG, SSM_CG))

    wj = pw[:S5_SUB][::-1][:, :, :, None] * b_bar[None]
    wj = jnp.transpose(wj.reshape(S5_SUB, n_gb, gpb, p, SSM_CG), (1, 0, 2, 4, 3))
    wcat = jnp.concatenate([block_diag(jnp.real(wj)), block_diag(jnp.imag(wj))], axis=-1)
    wcat = wcat.reshape(n_gb, S5_SUB * LANES, 2 * gpb * p).astype(BF16)

    cl = c_c[None] * pw[1:S5_SUB + 1][:, :, None, :]
    cl = jnp.transpose(cl, (0, 1, 3, 2)).reshape(S5_SUB, n_gb, gpb, p, SSM_CG)
    vm = jnp.concatenate([block_diag(jnp.real(cl)), block_diag(-jnp.imag(cl))], axis=-2)

    lam_s = pw[S5_SUB].reshape(n_gb, 1, gpb * p)
    dsk = d_skip.astype(F32).reshape(n_gb, 1, LANES)

    km, vm = km.astype(BF16), vm.astype(BF16)
    ycat = []
    zero = jnp.zeros_like(km[0])
    for l in range(0, S5_SUB, 2):
        rows = [jnp.concatenate([vm[l], vm[l + 1]], axis=-1)]
        for m in range(l + 1, -1, -1):
            left = km[l - m] if m <= l else zero
            rows.append(jnp.concatenate([left, km[l + 1 - m]], axis=-1))
        ycat.append(jnp.concatenate(rows, axis=-2))
    return (wcat, tuple(ycat), jnp.real(lam_s), jnp.imag(lam_s), dsk)


def _glu_body(y_ref, w_ref, b_ref, o_ref):
    y = y_ref[...]
    gate = _dot(y.astype(BF16), w_ref[...]) + b_ref[...]
    o_ref[...] = (y * _sigmoid(gate)).astype(BF16)


def _glu(y2, w_bf, b_row, tm):
    m, d = y2.shape
    return pl.pallas_call(
        _glu_body,
        grid=(m // tm,),
        in_specs=[pl.BlockSpec((tm, d), lambda i: (i, 0)),
                  pl.BlockSpec((d, d), lambda i: (0, 0)),
                  pl.BlockSpec((1, d), lambda i: (0, 0))],
        out_specs=pl.BlockSpec((tm, d), lambda i: (i, 0)),
        out_shape=jax.ShapeDtypeStruct((m, d), BF16),
        compiler_params=_cparams(("parallel",)),
        name="s5_glu",
    )(y2, w_bf, b_row)


def _layernorm(v, g, b):
    mu = jnp.mean(v, axis=-1, keepdims=True)
    var = jnp.mean(jnp.square(v - mu), axis=-1, keepdims=True)
    return (v - mu) * lax.rsqrt(var + LN_EPS) * g + b


def _mix_body(og_ref, os_ref, wa_ref, wb_ref, x_ref, g_ref, b_ref, wr_ref, br_ref,
              h_ref, hb_ref, route_ref, cnt_ref, *, alpha):
    @pl.when(pl.program_id(0) == 0)
    def _():
        cnt_ref[...] = jnp.zeros_like(cnt_ref)

    mix = _dot(og_ref[...], wa_ref[...]) + _dot(os_ref[...], wb_ref[...])
    h = _layernorm(alpha * x_ref[...] + mix, g_ref[...], b_ref[...])
    h_ref[...] = h
    tm = h.shape[0]
    _pack_rows(h, hb_ref, tm)

    logits = _dot_x3(h, wr_ref[...]) + br_ref[...]
    lane = lax.broadcasted_iota(jnp.int32, (tm, LANES), 1)
    work = jnp.where(lane < N_EXPERTS, logits, -jnp.inf)
    vals, idxs = [], []
    for _ in range(TOP_K):
        mx = jnp.max(work, axis=-1, keepdims=True)
        ix = jnp.min(jnp.where(work == mx, lane, LANES), axis=-1, keepdims=True)
        vals.append(mx)
        idxs.append(ix)
        work = jnp.where(lane == ix, -jnp.inf, work)
    exps = [jnp.exp(v - vals[0]) for v in vals]
    denom = exps[0]
    for e in exps[1:]:
        denom = denom + e
    chosen = jnp.zeros((tm, LANES), F32)
    for k in range(TOP_K):
        chosen = jnp.where(lane == idxs[k], 1.0, chosen)
    rr = lax.broadcasted_iota(jnp.int32, (tm, tm), 0)
    cc = lax.broadcasted_iota(jnp.int32, (tm, tm), 1)
    lower = jnp.where(rr > cc, 1.0, 0.0).astype(BF16)
    before = cnt_ref[...] + _dot(lower, chosen.astype(BF16))
    cnt_ref[...] = cnt_ref[...] + jnp.sum(chosen, axis=0, keepdims=True)

    route = jnp.zeros((tm, LANES), F32)
    for k in range(TOP_K):
        rank = jnp.sum(jnp.where(lane == idxs[k], before, 0.0), axis=-1, keepdims=True)
        route = jnp.where(lane == k, idxs[k].astype(F32), route)
        route = jnp.where(lane == TOP_K + k, exps[k] / denom, route)
        route = jnp.where(lane == 2 * TOP_K + k, rank, route)
    route_ref[...] = route


def _mix_ln_route(og, osm, wa, wb, x2, g_row, b_row, wr, br, alpha, tm):
    m, d = x2.shape
    dh = og.shape[1]
    body = functools.partial(_mix_body, alpha=alpha)
    row = lambda i: (i, 0)
    fix = lambda i: (0, 0)
    return pl.pallas_call(
        body,
        grid=(m // tm,),
        in_specs=[pl.BlockSpec((tm, dh), row), pl.BlockSpec((tm, dh), row),
                  pl.BlockSpec((dh, d), fix), pl.BlockSpec((dh, d), lambda i: (1, 0)),
                  pl.BlockSpec((tm, d), row),
                  pl.BlockSpec((1, d), fix), pl.BlockSpec((1, d), fix),
                  pl.BlockSpec((d, LANES), fix), pl.BlockSpec((1, LANES), fix)],
        out_specs=[pl.BlockSpec((tm, d), row), pl.BlockSpec((tm * ROW_SUB, LANES), row),
                   pl.BlockSpec((tm, LANES), row), pl.BlockSpec((1, LANES), fix)],
        out_shape=[jax.ShapeDtypeStruct((m, d), F32),
                   jax.ShapeDtypeStruct((m * ROW_SUB, LANES), jnp.uint32),
                   jax.ShapeDtypeStruct((m, LANES), F32),
                   jax.ShapeDtypeStruct((1, LANES), F32)],
        compiler_params=_cparams(("arbitrary",)),
        name="mix_ln_route",
    )(og, osm, wa, wb, x2, g_row, b_row, wr, br)


def _dispatch_body(pos_ref, src_ref, init_ref, dst_ref, sem, *, tt):
    del init_ref

    def issue(t, carry):
        srow = pl.multiple_of(t * ROW_SUB, ROW_SUB)
        for k in range(TOP_K):
            drow = pl.multiple_of(pos_ref[t * TOP_K + k] * ROW_SUB, ROW_SUB)
            pltpu.make_async_copy(src_ref.at[pl.ds(srow, ROW_SUB)], dst_ref.at[pl.ds(drow, ROW_SUB)],
                                  sem).start(priority=k % 2)
        return carry

    lax.fori_loop(0, tt, issue, 0, unroll=4)
    n = tt * ROW_SUB
    for _ in range(TOP_K):
        pltpu.make_async_copy(src_ref.at[pl.ds(0, n)], dst_ref.at[pl.ds(0, n)], sem).wait()


def _dispatch_rows(pos_flat, hbp, init, tt):
    m = hbp.shape[0] // ROW_SUB
    body = functools.partial(_dispatch_body, tt=tt)
    return pl.pallas_call(
        body,
        grid=(m // tt,),
        in_specs=[pl.BlockSpec((tt * TOP_K,), lambda i: (i,), memory_space=pltpu.SMEM),
                  pl.BlockSpec((tt * ROW_SUB, LANES), lambda i: (i, 0)), pl.BlockSpec(memory_space=pl.ANY)],
        out_specs=pl.BlockSpec(memory_space=pl.ANY),
        out_shape=jax.ShapeDtypeStruct(init.shape, jnp.uint32),
        scratch_shapes=[pltpu.SemaphoreType.DMA(())],
        input_output_aliases={2: 0},
        compiler_params=_cparams(("arbitrary",)),
        name="moe_dispatch",
    )(pos_flat, hbp, init)


MOE_COL = 256
MOE_GROUP = 2


def _moe_body(te_ref, nu_ref, x_ref, wg_ref, bg_ref, wu_ref, bu_ref, wd_ref, bd_ref,
              o_ref, xb_ref, acc_ref, *, n_f, tm, tf):
    g = pl.program_id(2)
    f = pl.program_id(1)
    used = pl.program_id(0) * MOE_GROUP + g < nu_ref[0]
    half = xb_ref.shape[2] // 2
    xb_ref = xb_ref.at[g]
    acc_ref = acc_ref.at[g]

    @pl.when(used & (f == 0))
    def _():
        lo, hi = _unpack_rows(x_ref, 0, tm)
        for r in range(ROW_SUB):
            xb_ref[:, r * LANES:(r + 1) * LANES] = lo[r].astype(BF16)
            xb_ref[:, half + r * LANES:half + (r + 1) * LANES] = hi[r].astype(BF16)

    @pl.when(used)
    def _():
        xb = xb_ref[...]
        part = None
        for c in range(tf // MOE_COL):
            cs = slice(c * MOE_COL, (c + 1) * MOE_COL)
            hg = jnp.minimum(_dot(xb, wg_ref[0, :, cs].astype(BF16)) + bg_ref[0, :, cs], SWIGLU_LIMIT)
            hu = jnp.clip(_dot(xb, wu_ref[0, :, cs].astype(BF16)) + bu_ref[0, :, cs], -SWIGLU_LIMIT, SWIGLU_LIMIT)
            hh = ((hu + 1.0) * (hg * _sigmoid(SWIGLU_ALPHA * hg))).astype(BF16)
            p = _dot(hh, wd_ref[0, cs, :].astype(BF16))
            part = p if part is None else part + p

        @pl.when(f == 0)
        def _():
            acc_ref[...] = part

        @pl.when(f != 0)
        def _():
            acc_ref[...] += part

        @pl.when(f == n_f - 1)
        def _():
            _pack_rows(acc_ref[...] + bd_ref[0], o_ref, tm)

    @pl.when(jnp.logical_not(used) & (f == n_f - 1))
    def _():
        o_ref[...] = jnp.zeros_like(o_ref)


def _moe_ffn(tile_expert, n_used, x_rows, w_gate, b_gate, w_up, b_up, w_down, b_down, tm, tf):
    n_e, d, d_ff = w_gate.shape
    n_tiles = x_rows.shape[0] // (tm * ROW_SUB)
    n_f = d_ff // tf
    grp = MOE_GROUP
    assert n_tiles % grp == 0
    body = functools.partial(_moe_body, n_f=n_f, tm=tm, tf=tf)

    def tile(p, g):
        return p * grp + g

    def fcol(p, f, nu):
        return jnp.where(p * grp < nu[0], f, n_f - 1)

    def xrow(p, f, g):
        return jnp.where(f == 0, tile(p, g), tile(p, grp - 1))

    def orow(p, f, g):
        return jnp.where(f == n_f - 1, tile(p, g), tile(p, 0))

    grid_spec = pltpu.PrefetchScalarGridSpec(
        num_scalar_prefetch=2,
        grid=(n_tiles // grp, n_f, grp),
        in_specs=[pl.BlockSpec((tm * ROW_SUB, LANES), lambda p, f, g, te, nu: (xrow(p, f, g), 0)),
                  pl.BlockSpec((1, d, tf), lambda p, f, g, te, nu: (te[tile(p, g)], 0, fcol(p, f, nu))),
                  pl.BlockSpec((1, 1, tf), lambda p, f, g, te, nu: (te[tile(p, g)], 0, fcol(p, f, nu))),
                  pl.BlockSpec((1, d, tf), lambda p, f, g, te, nu: (te[tile(p, g)], 0, fcol(p, f, nu))),
                  pl.BlockSpec((1, 1, tf), lambda p, f, g, te, nu: (te[tile(p, g)], 0, fcol(p, f, nu))),
                  pl.BlockSpec((1, tf, d), lambda p, f, g, te, nu: (te[tile(p, g)], fcol(p, f, nu), 0)),
                  pl.BlockSpec((1, 1, d), lambda p, f, g, te, nu: (te[tile(p, g)], 0, 0))],
        out_specs=pl.BlockSpec((tm * ROW_SUB, LANES), lambda p, f, g, te, nu: (orow(p, f, g), 0)),
        scratch_shapes=[pltpu.VMEM((grp, tm, d), BF16), pltpu.VMEM((grp, tm, d), F32)],
    )
    return pl.pallas_call(
        body,
        grid_spec=grid_spec,
        out_shape=jax.ShapeDtypeStruct(x_rows.shape, jnp.uint32),
        compiler_params=_cparams(("arbitrary", "arbitrary", "arbitrary")),
        name="moe_ffn",
    )(tile_expert, n_used, x_rows, w_gate, b_gate.reshape(n_e, 1, d_ff),
      w_up, b_up.reshape(n_e, 1, d_ff), w_down, b_down.reshape(n_e, 1, d))


def _final_body(pos_cur_ref, pos_nxt_ref, h_ref, route_ref, g_ref, b_ref, src_ref,
                o_ref, buf_ref, sem, *, alpha, tt, n_steps):
    i = pl.program_id(0)
    slot_rows = tt * TOP_K * ROW_SUB

    def issue(pref, slot):
        def one(t, carry):
            for k in range(TOP_K):
                srow = pl.multiple_of(pref[t * TOP_K + k] * ROW_SUB, ROW_SUB)
                drow = pl.multiple_of(slot * slot_rows + (k * tt + t) * ROW_SUB, ROW_SUB)
                pltpu.make_async_copy(src_ref.at[pl.ds(srow, ROW_SUB)], buf_ref.at[pl.ds(drow, ROW_SUB)],
                                      sem.at[slot]).start(priority=k % 2)
            return carry
        lax.fori_loop(0, tt, one, 0, unroll=8)

    @pl.when(i == 0)
    def _():
        issue(pos_cur_ref, 0)

    @pl.when(i + 1 < n_steps)
    def _():
        issue(pos_nxt_ref, (i + 1) % 2)

    slot = i % 2
    base = pl.multiple_of(slot * slot_rows, ROW_SUB)
    pltpu.make_async_copy(src_ref.at[pl.ds(0, slot_rows)], buf_ref.at[pl.ds(base, slot_rows)], sem.at[slot]).wait()

    ff_lo = [None] * ROW_SUB
    ff_hi = [None] * ROW_SUB
    for k in range(TOP_K):
        gate = route_ref[:, TOP_K + k:TOP_K + k + 1]
        lo, hi = _unpack_rows(buf_ref, base + k * tt * ROW_SUB, tt)
        for s in range(ROW_SUB):
            ff_lo[s] = lo[s] * gate if k == 0 else ff_lo[s] + lo[s] * gate
            ff_hi[s] = hi[s] * gate if k == 0 else ff_hi[s] + hi[s] * gate
    ff = jnp.concatenate(ff_lo + ff_hi, axis=1)
    o_ref[...] = _layernorm(alpha * h_ref[...] + ff, g_ref[...], b_ref[...])


def _combine_ln(pos_flat, h, route, g_row, b_row, outs, alpha, tt):
    m, d = h.shape
    n_steps = m // tt
    body = functools.partial(_final_body, alpha=alpha, tt=tt, n_steps=n_steps)
    row = lambda i: (i, 0)
    fix = lambda i: (0, 0)
    return pl.pallas_call(
        body,
        grid=(n_steps,),
        in_specs=[pl.BlockSpec((tt * TOP_K,), lambda i: (i,), memory_space=pltpu.SMEM),
                  pl.BlockSpec((tt * TOP_K,), lambda i: (jnp.minimum(i + 1, n_steps - 1),),
                               memory_space=pltpu.SMEM),
                  pl.BlockSpec((tt, d), row), pl.BlockSpec((tt, LANES), row),
                  pl.BlockSpec((1, d), fix), pl.BlockSpec((1, d), fix),
                  pl.BlockSpec(memory_space=pl.ANY)],
        out_specs=pl.BlockSpec((tt, d), row),
        out_shape=jax.ShapeDtypeStruct((m, d), F32),
        scratch_shapes=[pltpu.VMEM((2 * tt * TOP_K * ROW_SUB, LANES), jnp.uint32),
                        pltpu.SemaphoreType.DMA((2,))],
        compiler_params=_cparams(("arbitrary",)),
        name="combine_ln",
    )(pos_flat, pos_flat, h, route, g_row, b_row, outs)


def _row_tile(m, pref):
    t = min(m, pref)
    while m % t:
        t //= 2
    return t


def _route_tables(routes, counts_rows, tm):
    counts = [c[0, :N_EXPERTS].astype(jnp.int32) for c in counts_rows]
    total = functools.reduce(lambda a, b: a + b, counts)
    padded = (total + tm - 1) // tm * tm
    pends = jnp.cumsum(padded)
    pstarts = pends - padded
    pos, before = [], jnp.zeros_like(total)
    for route, cnt in zip(routes, counts):
        e_tok = route[:, :TOP_K].astype(jnp.int32)
        rank = route[:, 2 * TOP_K:3 * TOP_K].astype(jnp.int32)
        pos.append(((pstarts + before)[e_tok] + rank).reshape(-1))
        before = before + cnt
    n_assign = sum(r.shape[0] for r in routes) * TOP_K
    n_tiles = -(-n_assign // tm) + N_EXPERTS
    n_tiles = -(-n_tiles // MOE_GROUP) * MOE_GROUP
    tile_start = jnp.arange(n_tiles, dtype=jnp.int32) * tm
    tile_expert = jnp.minimum(jnp.sum((pends[None, :] <= tile_start[:, None]).astype(jnp.int32), axis=1),
                              N_EXPERTS - 1)
    n_used = (pends[-1] // tm).astype(jnp.int32).reshape(1)
    last_used = jnp.maximum(n_used[0] - 1, 0)
    tile_expert = jnp.where(tile_start // tm < n_used[0], tile_expert, tile_expert[last_used])
    return pos, tile_expert, n_used


def _moe_joint(parts, alpha, p):
    tm = p['moe_tm']
    pos, tile_expert, n_used = _route_tables([q['route'] for q in parts], [q['counts'] for q in parts], tm)
    cap = tile_expert.shape[0] * tm
    x_rows = jnp.zeros((cap * ROW_SUB, LANES), jnp.uint32)
    for q, pq in zip(parts, pos):
        x_rows = _dispatch_rows(pq, q['hb'], x_rows, _row_tile(q['h'].shape[0], 512))
    outs = _moe_ffn(tile_expert, n_used, x_rows, p['w_gate'], p['b_gate'],
                    p['w_up'], p['b_up'], p['w_down'], p['b_down'], tm, p['moe_tf'])
    return [_combine_ln(pq, q['h'], q['route'], p['ln2_g'], p['ln2_b'], outs, alpha, _row_tile(q['h'].shape[0], 128))
            for q, pq in zip(parts, pos)]


def _layer_pre(x, conv_hist, s_gdn, h_re, h_im, chunk, alpha, p):
    b, t, d = x.shape
    m = b * t
    d_gdn = GDN_HEADS * HEAD_DIM
    x2 = x.reshape(m, d)

    proj = _in_proj(x2, p['w_in'], _row_tile(m, 1024), p['proj_tn'])
    proj3 = proj.reshape(b, t, proj.shape[1])
    conv_new = proj3[:, t - (CONV_W - 1):, :3 * d_gdn]

    q, k, v, cols, gct = _gdn_prep(proj3, conv_hist, p['w_conv'], p['alog_row'], p['dtb_row'],
                                   chunk, p['ab_block'])
    o_gdn, s_new = _gdn(q, k, v, cols, gct, proj3, p['wn_row'], s_gdn, chunk, p['z_block'])

    n_gb = p['s5_mats'][0].shape[0]
    half = GROUPS_PER_BLOCK * SSM_P
    yg, hre_new, him_new = _s5(proj3, p['s5_mats'], h_re.reshape(b, n_gb, 1, half),
                               h_im.reshape(b, n_gb, 1, half), p['u_block0'])
    o_ssm = _glu(yg.reshape(m, yg.shape[2]), p['w_glu'], p['b_glu_row'], _row_tile(m, 512))

    h, hb, route, counts_row = _mix_ln_route(o_gdn.reshape(m, d_gdn), o_ssm, p['w_out'], p['w_out'], x2,
                                             p['ln1_g'], p['ln1_b'], p['w_router'], p['b_router'], alpha,
                                             _row_tile(m, 256))

    g_all = h_re.shape[1]
    return {'h': h, 'hb': hb, 'route': route, 'counts': counts_row, 'shape': (b, t, d),
            'state': (conv_new, s_new, hre_new.reshape(b, g_all, SSM_P), him_new.reshape(b, g_all, SSM_P))}


def _pad_lanes(v, fill=0.0):
    return jnp.pad(v.astype(F32), (0, LANES - v.shape[0]), constant_values=fill).reshape(1, LANES)


def _layer_params(l, w_in, w_conv, a_log, dt_bias, w_onorm, lam_re, lam_im, log_dt, b_re, b_im, c_re, c_im,
                  d_skip, w_glu, b_glu, w_out, ln1_g, ln1_b, w_router, b_router, w_gate, b_gate,
                  w_up, b_up, w_down, b_down, ln2_g, ln2_b):
    d_model = w_in.shape[1]
    d_gdn = GDN_HEADS * HEAD_DIM
    d_qkvz = 4 * d_gdn
    d_ssm = d_model - d_gdn
    wi = w_in[l]
    proj_tn = 1792
    n_cols = d_qkvz + d_ssm + LANES
    n_pad = -(-n_cols // proj_tn) * proj_tn
    w_in_r = jnp.concatenate([wi[:, :d_qkvz], wi[:, d_qkvz + 2 * GDN_HEADS:],
                              wi[:, d_qkvz:d_qkvz + 2 * GDN_HEADS],
                              jnp.zeros((d_model, n_pad - d_qkvz - d_ssm - 2 * GDN_HEADS), wi.dtype)], axis=1)
    wo = w_out[l].astype(BF16)
    wr = jnp.pad(w_router[l].astype(F32), ((0, 0), (0, LANES - N_EXPERTS)))
    return {
        'w_in': w_in_r.astype(BF16), 'proj_tn': proj_tn,
        'z_block': 3, 'u_block0': (d_qkvz) // LANES, 'ab_block': (d_qkvz + d_ssm) // LANES,
        'w_conv': w_conv[l].astype(F32),
        'alog_row': _pad_lanes(a_log[l]), 'dtb_row': _pad_lanes(dt_bias[l]),
        'wn_row': w_onorm[l].astype(F32).reshape(1, HEAD_DIM),
        's5_mats': _s5_matrices(lam_re[l], lam_im[l], log_dt[l], b_re[l], b_im[l], c_re[l], c_im[l], d_skip[l]),
        'w_glu': w_glu[l].astype(BF16), 'b_glu_row': b_glu[l].astype(F32).reshape(1, d_ssm),
        'w_out': wo,
        'ln1_g': ln1_g[l].astype(F32).reshape(1, d_model), 'ln1_b': ln1_b[l].astype(F32).reshape(1, d_model),
        'w_router': wr, 'b_router': _pad_lanes(b_router[l]),
        'w_gate': w_gate[l], 'b_gate': b_gate[l], 'w_up': w_up[l], 'b_up': b_up[l],
        'w_down': w_down[l], 'b_down': b_down[l],
        'ln2_g': ln2_g[l].astype(F32).reshape(1, d_model), 'ln2_b': ln2_b[l].astype(F32).reshape(1, d_model),
        'moe_tm': 576, 'moe_tf': 512,
    }


def kernel(x_prompt, x_sample, state_conv, state_gdn, state_ssm_re, state_ssm_im, w_in, w_conv, a_log, dt_bias, w_onorm, lam_re, lam_im, log_dt, b_re, b_im, c_re, c_im, d_skip, w_glu, b_glu, w_out, ln1_g, ln1_b, w_router, b_router, w_gate, b_gate, w_up, b_up, w_down, b_down, ln2_g, ln2_b):
    depth = w_in.shape[0]
    alpha = (2.0 * depth) ** 0.25
    bp, seq, _ = x_prompt.shape
    chunk_p = 64
    d_qkv = state_conv.shape[-1]
    n_groups, n_p = state_ssm_re.shape[-2:]
    yp, ys = x_prompt, x_sample
    outs_p = [[], [], [], []]
    outs_s = [[], [], [], []]
    for l in range(depth):
        p = _layer_params(l, w_in, w_conv, a_log, dt_bias, w_onorm, lam_re, lam_im, log_dt, b_re, b_im,
                          c_re, c_im, d_skip, w_glu, b_glu, w_out, ln1_g, ln1_b, w_router, b_router,
                          w_gate, b_gate, w_up, b_up, w_down, b_down, ln2_g, ln2_b)
        part_p = _layer_pre(
            yp, jnp.zeros((bp, CONV_W - 1, d_qkv), F32),
            jnp.zeros((bp, GDN_HEADS, HEAD_DIM, HEAD_DIM), F32),
            jnp.zeros((bp, n_groups, n_p), F32), jnp.zeros((bp, n_groups, n_p), F32),
            chunk_p, alpha, p)
        part_s = _layer_pre(
            ys, state_conv[l].astype(F32), state_gdn[l].astype(F32),
            state_ssm_re[l].astype(F32), state_ssm_im[l].astype(F32),
            ys.shape[1], alpha, p)
        yp2, ys2 = _moe_joint([part_p, part_s], alpha, p)
        yp, ys = yp2.reshape(part_p['shape']), ys2.reshape(part_s['shape'])
        for acc, val in zip(outs_p, part_p['state']):
            acc.append(val)
        for acc, val in zip(outs_s, part_s['state']):
            acc.append(val)
    return (yp, ys, *[jnp.stack(a) for a in outs_p], *[jnp.stack(a) for a in outs_s])
```
